```python
import math
import jax
import jax.numpy as jnp
from jax import lax
import numpy as np

D_MODEL = 1024
BATCH = 8
SEQ = 2048
DEPTH = 2

CTX_LEN = 256
GRID_W = 64

A_HEADS = 4
A_DK = 64
A_DV = 64
B_HEADS = 4
B_DQK = 32
B_DV = 64
C_HEADS = 4
C_DQK = 64
C_DV = 2 * C_DQK

A_WIDTH = A_HEADS * A_DV
B_WIDTH = B_HEADS * B_DV
C_WIDTH = C_HEADS * C_DV
D_MIX = A_WIDTH + B_WIDTH + C_WIDTH

PROJ_SIZES = (
    A_HEADS * A_DK,
    A_WIDTH,
    A_HEADS * A_DK,
    A_HEADS * A_DK,
    A_WIDTH,
    B_HEADS * B_DQK,
    B_HEADS * B_DQK,
    B_WIDTH,
    B_WIDTH,
    4 * B_HEADS,
    C_HEADS * 2 * C_DQK,
    C_HEADS * 2 * C_DQK,
    C_WIDTH,
)
D_PROJ = sum(PROJ_SIZES)

N_EXPERTS = 32
TOP_K = 4
D_FF = 1024
SWIGLU_ALPHA = 1.702
SWIGLU_LIMIT = 7.0

SCAN_CHUNK = 64
Q_BLOCK = 128
MOE_BLOCK = 128
ROPE_THETA = 10000.0
NORM_EPS = 1e-6
MLSTM_F_BIAS = 3.0
MASK_NEG = -1e30
F_MIN = 1e-12

kernel_name = 'hybrid_dit_hgrn2_mlstm_diffattn_moe'


def rmsnorm(x, gain):
    xf = x.astype(jnp.float32)
    y = xf * lax.rsqrt(jnp.mean(xf * xf, axis=-1, keepdims=True) + NORM_EPS)
    return (y * gain.astype(jnp.float32)).astype(x.dtype)


def modulated_norm(t, gain, shift, scale):
    return rmsnorm(t, gain) * (1.0 + scale) + shift


def split_heads(t, n_heads):
    b, l, w = t.shape
    return t.reshape(b, l, n_heads, w // n_heads).transpose(0, 2, 1, 3)


def merge_heads(t):
    b, h, l, d = t.shape
    return t.transpose(0, 2, 1, 3).reshape(b, l, h * d)


def flip_t(t):
    return jnp.flip(t, axis=2)


def split_projection(p):
    parts, start = [], 0
    for size in PROJ_SIZES:
        parts.append(p[..., start:start + size])
        start += size
    return parts


def to_chunks(t):
    b, h, l = t.shape[:3]
    t = t.reshape(b, h, l // SCAN_CHUNK, SCAN_CHUNK, *t.shape[3:])
    return jnp.moveaxis(t, 2, 0)


def from_chunks(t):
    t = jnp.moveaxis(t, 0, 2)
    b, h, n, c = t.shape[:4]
    return t.reshape(b, h, n * c, *t.shape[4:])


def axial_rope_tables(length):
    rows = length // GRID_W
    row = jnp.repeat(jnp.arange(rows, dtype=jnp.float32), GRID_W)
    col = jnp.tile(jnp.arange(GRID_W, dtype=jnp.float32), rows)
    n_freq = C_DQK // 4
    inv_freq = ROPE_THETA ** (-jnp.arange(n_freq, dtype=jnp.float32) / n_freq)
    ang_r = row[:, None] * inv_freq
    ang_c = col[:, None] * inv_freq
    return jnp.cos(ang_r), jnp.sin(ang_r), jnp.cos(ang_c), jnp.sin(ang_c)


def rotate_half(x, cos, sin):
    n = cos.shape[-1]
    x1, x2 = x[..., :n], x[..., n:]
    return jnp.concatenate([x1 * cos - x2 * sin, x1 * sin + x2 * cos], axis=-1)


def apply_axial_rope(x, tables):
    cr, sr, cc, sc = tables
    half = x.shape[-1] // 2
    xf = x.astype(jnp.float32)
    out = jnp.concatenate([rotate_half(xf[..., :half], cr, sr),
                           rotate_half(xf[..., half:], cc, sc)], axis=-1)
    return out.astype(x.dtype)


def hgrn_lower_bound(table, layer):
    p = jax.nn.softmax(table.astype(jnp.float32), axis=1)
    cum = jnp.cumsum(p, axis=1) - p[:, :1]
    return jnp.clip(cum[:, layer], 0.0, 1.0)


def hgrn2_gates(f_pre, lb):
    f_pre = f_pre.astype(jnp.float32)
    f = lb + (1.0 - lb) * jax.nn.sigmoid(f_pre)
    log_f = jnp.log(jnp.maximum(f, F_MIN))
    k = (1.0 - lb) * jax.nn.sigmoid(-f_pre)
    return split_heads(log_f, A_HEADS), split_heads(k, A_HEADS)


def hgrn2_scan(q, k, v, log_f, state):
    causal = jnp.tril(jnp.ones((SCAN_CHUNK, SCAN_CHUNK), dtype=bool))[:, :, None]

    def step(s, blk):
        qc, kc, vc, lfc = blk
        cum = jnp.cumsum(lfc, axis=-2)
        o_inter = jnp.einsum('bhtk,bhkv->bhtv', qc * jnp.exp(cum), s)
        rel = cum[:, :, :, None, :] - cum[:, :, None, :, :]
        decay = jnp.where(causal, jnp.exp(jnp.where(causal, rel, 0.0)), 0.0)
        scores = jnp.einsum('bhtk,bhsk,bhtsk->bhts', qc, kc, decay)
        o = o_inter + jnp.einsum('bhts,bhsv->bhtv', scores, vc)
        k_end = kc * jnp.exp(cum[:, :, -1:, :] - cum)
        s_new = jnp.exp(cum[:, :, -1, :])[..., None] * s + jnp.einsum('bhsk,bhsv->bhkv', k_end, vc)
        return s_new, o

    state, o = lax.scan(step, state, (to_chunks(q), to_chunks(k), to_chunks(v), to_chunks(log_f)))
    return from_chunks(o), state


def hgrn2_group(parts, lb, gain, s_fwd, s_bwd, readout):
    a_q, a_i, a_f_fwd, a_f_bwd, a_g = parts
    q = split_heads(jax.nn.silu(a_q.astype(jnp.float32)), A_HEADS)
    v = split_heads(a_i.astype(jnp.float32), A_HEADS)
    lf_f, k_f = hgrn2_gates(a_f_fwd, lb[0])
    lf_b, k_b = hgrn2_gates(a_f_bwd, lb[1])
    o_f, s_f = hgrn2_scan(q, k_f, v, lf_f, s_fwd)
    o_b, s_b = hgrn2_scan(flip_t(q), flip_t(k_b), flip_t(v), flip_t(lf_b), s_bwd)
    if not readout:
        return None, s_f, s_b
    o = rmsnorm(o_f + flip_t(o_b), gain)
    return merge_heads(o) * jax.nn.silu(a_g.astype(jnp.float32)), s_f, s_b


def mlstm_scan(q, k, v, log_i, log_f, state):
    causal = jnp.tril(jnp.ones((SCAN_CHUNK, SCAN_CHUNK), dtype=bool))

    def step(carry, blk):
        c_mat, n_vec, m = carry
        qc, kc, vc, ic, fc = blk
        cum = jnp.cumsum(fc, axis=-1)
        log_d = jnp.where(causal, cum[..., :, None] - cum[..., None, :] + ic[..., None, :], MASK_NEG)
        log_inter = cum + m[..., None]
        m_t = jnp.maximum(log_inter, jnp.max(log_d, axis=-1))
        w_intra = jnp.where(causal, jnp.exp(log_d - m_t[..., None]), 0.0)
        w_inter = jnp.exp(log_inter - m_t)
        scores = jnp.einsum('bhtk,bhsk->bhts', qc, kc) * w_intra
        num = (w_inter[..., None] * jnp.einsum('bhtk,bhkv->bhtv', qc, c_mat)
               + jnp.einsum('bhts,bhsv->bhtv', scores, vc))
        den = w_inter * jnp.einsum('bhtk,bhk->bht', qc, n_vec) + jnp.sum(scores, axis=-1)
        h = num / jnp.maximum(jnp.abs(den), jnp.exp(-m_t))[..., None]
        log_end = cum[..., -1:] - cum + ic
        m_end = jnp.maximum(cum[..., -1] + m, jnp.max(log_end, axis=-1))
        w_end = jnp.exp(log_end - m_end[..., None])
        w_carry = jnp.exp(cum[..., -1] + m - m_end)
        c_new = w_carry[..., None, None] * c_mat + jnp.einsum('bhs,bhsk,bhsv->bhkv', w_end, kc, vc)
        n_new = w_carry[..., None] * n_vec + jnp.einsum('bhs,bhsk->bhk', w_end, kc)
        return (c_new, n_new, m_end), h

    state, h = lax.scan(step, state, (to_chunks(q), to_chunks(k), to_chunks(v),
                                      to_chunks(log_i), to_chunks(log_f)))
    return from_chunks(h), state


def mlstm_group(parts, gate_bias, gain, st_fwd, st_bwd, readout):
    b_q, b_k, b_v, b_o, b_g = parts
    q = split_heads(b_q.astype(jnp.float32), B_HEADS) * (B_DQK ** -0.5)
    k = split_heads(b_k.astype(jnp.float32), B_HEADS)
    v = split_heads(b_v.astype(jnp.float32), B_HEADS)
    bsz, l, _ = b_g.shape
    g = (b_g.astype(jnp.float32) + gate_bias.astype(jnp.float32)).reshape(bsz, l, 4, B_HEADS)
    g = g.transpose(2, 0, 3, 1)
    h_f, st_f = mlstm_scan(q, k, v, g[0], jax.nn.log_sigmoid(g[1]), st_fwd)
    h_b, st_b = mlstm_scan(flip_t(q), flip_t(k), flip_t(v), flip_t(g[2]),
                           flip_t(jax.nn.log_sigmoid(g[3])), st_bwd)
    if not readout:
        return None, st_f, st_b
    gain_h = gain.astype(jnp.float32).reshape(B_HEADS, 1, B_DV)
    h = rmsnorm(h_f + flip_t(h_b), gain_h)
    return merge_heads(h) * jax.nn.sigmoid(b_o.astype(jnp.float32)), st_f, st_b


def diff_lambda(lam_params, lam_init):
    lq1, lk1, lq2, lk2 = lam_params.astype(jnp.float32)
    return jnp.exp(jnp.sum(lq1 * lk1)) - jnp.exp(jnp.sum(lq2 * lk2)) + lam_init


def diff_qkv(c_q, c_k, c_v):
    b, l, _ = c_q.shape
    q = c_q.reshape(b, l, C_HEADS, 2, C_DQK).transpose(0, 2, 3, 1, 4)
    k = c_k.reshape(b, l, C_HEADS, 2, C_DQK).transpose(0, 2, 3, 1, 4)
    v = split_heads(c_v, C_HEADS)
    return q, k, v


def diff_combine(s, v, lam):
    p = jax.nn.softmax(s.astype(jnp.float32) * (C_DQK ** -0.5), axis=-1)
    w = p[:, :, 0] - lam * p[:, :, 1]
    return jnp.einsum('bhqk,bhkv->bhqv', w, v.astype(jnp.float32))


def diff_readout(o, gain, lam_init):
    return merge_heads(rmsnorm(o, gain)) * (1.0 - lam_init)


def diff_group_context(parts, lam, lam_init, gain, readout):
    q, k, v = diff_qkv(*parts)
    if not readout:
        return None, k, v
    o = diff_combine(jnp.einsum('bhjqd,bhjkd->bhjqk', q, k), v, lam)
    return diff_readout(o, gain, lam_init), k, v


def diff_group_latent(parts, k_ctx, v_ctx, lam, lam_init, gain, tables):
    q, k, v = diff_qkv(*parts)
    b, h, _, l, d = q.shape
    q_rot = apply_axial_rope(q, tables)
    k_rot = apply_axial_rope(k, tables)
    v_all = jnp.concatenate([v, v_ctx], axis=2)
    n_blk = l // Q_BLOCK

    def blocks(t):
        return jnp.moveaxis(t.reshape(b, h, 2, n_blk, Q_BLOCK, d), 3, 0)

    def attend(args):
        qr, qp = args
        s = jnp.concatenate([jnp.einsum('bhjqd,bhjkd->bhjqk', qr, k_rot),
                             jnp.einsum('bhjqd,bhjkd->bhjqk', qp, k_ctx)],
                            axis=-1)
        return diff_combine(s, v_all, lam)

    o = lax.map(attend, (blocks(q_rot), blocks(q)))
    o = jnp.moveaxis(o, 0, 2).reshape(b, h, l, C_DV)
    return diff_readout(o, gain, lam_init)


def moe_ffn(h, router_w, router_b, w1, b1, w2, b2):
    shape = h.shape
    tokens = h.reshape(-1, MOE_BLOCK, shape[-1])

    def block(t):
        logits = (t @ router_w + router_b).astype(jnp.float32)
        top_val, top_idx = lax.top_k(logits, TOP_K)
        weights = jax.nn.softmax(top_val, axis=-1)
        gate = jnp.einsum('tk,tke->te', weights, jax.nn.one_hot(top_idx, N_EXPERTS, dtype=jnp.float32))
        hid = jnp.einsum('td,edf->tef', t, w1) + b1
        glu = jnp.minimum(hid[..., 0::2], SWIGLU_LIMIT)
        lin = jnp.clip(hid[..., 1::2], -SWIGLU_LIMIT, SWIGLU_LIMIT)
        act = glu * jax.nn.sigmoid(SWIGLU_ALPHA * glu) * (lin + 1.0)
        act = act * gate[..., None].astype(act.dtype)
        return jnp.einsum('tef,efd->td', act, w2) + gate.astype(t.dtype) @ b2

    return lax.map(block, tokens).reshape(shape)


def setup_inputs(seed: int = 0) -> dict:
    key = jax.random.key(seed)
    ks = jax.random.split(key, 21)
    f32 = jnp.float32

    def nrm(k, shape, scale):
        return jax.random.normal(k, shape, f32) * scale

    gate_offset = jnp.array([0.0, MLSTM_F_BIAS, 0.0, MLSTM_F_BIAS], f32)[None, :, None]
    return {
        'x': nrm(ks[0], (BATCH, SEQ, D_MODEL), 1.0),
        'c': nrm(ks[1], (BATCH, D_MODEL), 1.0),
        'ctx': nrm(ks[2], (BATCH, CTX_LEN, D_MODEL), 1.0),
        'c_ctx': nrm(ks[3], (D_MODEL,), 1.0),
        'ada_w': nrm(ks[4], (DEPTH, D_MODEL, 6 * D_MODEL), 0.5 * D_MODEL ** -0.5),
        'ada_b': nrm(ks[5], (DEPTH, 6 * D_MODEL), 0.02),
        'sandwich_norms': 1.0 + nrm(ks[6], (DEPTH, 4, D_MODEL), 0.05),
        'w_in': nrm(ks[7], (DEPTH, D_MODEL, D_PROJ), D_MODEL ** -0.5),
        'w_out': nrm(ks[8], (DEPTH, D_MIX, D_MODEL), D_MIX ** -0.5),
        'hgrn_lower_bounds': nrm(ks[9], (2, DEPTH, A_HEADS * A_DK), 0.5),
        'hgrn_norm': 1.0 + nrm(ks[10], (DEPTH, A_DV), 0.05),
        'mlstm_gate_bias': (nrm(ks[11], (DEPTH, 4, B_HEADS), 0.1) + gate_offset).reshape(DEPTH, 4 * B_HEADS),
        'mlstm_norm': 1.0 + nrm(ks[12], (DEPTH, B_WIDTH), 0.05),
        'diff_lambdas': nrm(ks[13], (DEPTH, 4, C_DQK), 0.1),
        'diff_norm': 1.0 + nrm(ks[14], (DEPTH, C_DV), 0.05),
        'router_w': nrm(ks[15], (DEPTH, D_MODEL, N_EXPERTS), D_MODEL ** -0.5),
        'router_b': nrm(ks[16], (DEPTH, N_EXPERTS), 0.01),
        'moe_w1': nrm(ks[17], (DEPTH, N_EXPERTS, D_MODEL, 2 * D_FF), D_MODEL ** -0.5),
        'moe_b1': nrm(ks[18], (DEPTH, N_EXPERTS, 2 * D_FF), 0.01),
        'moe_w2': nrm(ks[19], (DEPTH, N_EXPERTS, D_FF, D_MODEL), D_FF ** -0.5),
        'moe_b2': nrm(ks[20], (DEPTH, N_EXPERTS, D_MODEL), 0.01),
    }


def reference(x, c, ctx, c_ctx, ada_w, ada_b, sandwich_norms, w_in, w_out,
              hgrn_lower_bounds, hgrn_norm, mlstm_gate_bias, mlstm_norm,
              diff_lambdas, diff_norm, router_w, router_b, moe_w1, moe_b1, moe_w2, moe_b2):
    f32 = jnp.float32
    bsz, length, _ = x.shape
    tables = axial_rope_tables(length)
    y = ctx
    cond_lat = jax.nn.silu(c)
    cond_ctx = jax.nn.silu(c_ctx)[None]
    zero_a = jnp.zeros((bsz, A_HEADS, A_DK, A_DV), f32)
    zero_b = (jnp.zeros((bsz, B_HEADS, B_DQK, B_DV), f32),
              jnp.zeros((bsz, B_HEADS, B_DQK), f32),
              jnp.zeros((bsz, B_HEADS), f32))

    for layer in range(DEPTH):
        last = layer == DEPTH - 1
        norms = sandwich_norms[layer]
        mod_l = jnp.split((cond_lat @ ada_w[layer] + ada_b[layer])[:, None, :], 6, axis=-1)
        mod_c = jnp.split((cond_ctx @ ada_w[layer] + ada_b[layer])[:, None, :], 6, axis=-1)
        lb = hgrn_lower_bound(hgrn_lower_bounds, layer)
        lam_init = 0.8 - 0.6 * math.exp(-0.3 * layer)
        lam = diff_lambda(diff_lambdas[layer], lam_init)

        p_ctx = split_projection(modulated_norm(y, norms[0], mod_c[0], mod_c[1]) @ w_in[layer])
        p_lat = split_projection(modulated_norm(x, norms[0], mod_l[0], mod_l[1]) @ w_in[layer])

        oa_c, sa_f, sa_b = hgrn2_group(p_ctx[0:5], lb, hgrn_norm[layer], zero_a, zero_a, not last)
        ob_c, sb_f, sb_b = mlstm_group(p_ctx[5:10], mlstm_gate_bias[layer], mlstm_norm[layer],
                                       zero_b, zero_b, not last)
        oc_c, k_ctx, v_ctx = diff_group_context(p_ctx[10:13], lam, lam_init, diff_norm[layer], not last)

        oa_l, _, _ = hgrn2_group(p_lat[0:5], lb, hgrn_norm[layer], sa_f, sa_b, True)
        ob_l, _, _ = mlstm_group(p_lat[5:10], mlstm_gate_bias[layer], mlstm_norm[layer], sb_f, sb_b, True)
        oc_l = diff_group_latent(p_lat[10:13], k_ctx, v_ctx, lam, lam_init, diff_norm[layer], tables)

        mix_l = jnp.concatenate([oa_l, ob_l, oc_l], axis=-1).astype(x.dtype) @ w_out[layer]
        x = x + mod_l[2] * rmsnorm(mix_l, norms[1])
        ffn_l = moe_ffn(modulated_norm(x, norms[2], mod_l[3], mod_l[4]), router_w[layer], router_b[layer],
                        moe_w1[layer], moe_b1[layer], moe_w2[layer], moe_b2[layer])
        x = x + mod_l[5] * rmsnorm(ffn_l, norms[3])

        if not last:
            mix_c = jnp.concatenate([oa_c, ob_c, oc_c], axis=-1).astype(y.dtype) @ w_out[layer]
            y = y + mod_c[2] * rmsnorm(mix_c, norms[1])
            ffn_c = moe_ffn(modulated_norm(y, norms[2], mod_c[3], mod_c[4]), router_w[layer], router_b[layer],
                            moe_w1[layer], moe_b1[layer], moe_w2[layer], moe_b2[layer])
            y = y + mod_c[5] * rmsnorm(ffn_c, norms[3])
    return x
```

```python
import functools
import math

import jax
import jax.numpy as jnp
from jax import lax
from jax.experimental import pallas as pl
from jax.experimental.pallas import tpu as pltpu

F32 = jnp.float32
BF16 = jnp.bfloat16
HI = lax.Precision.HIGHEST

HEADS = 4
A_W = 256
B_QK = 32
B_W = 256
C_DQK = 64
C_W = 512
HEAD_V = 64
N_GATE = 16
GRID_W = 64
TOP_K = 4
SWIGLU_ALPHA = 1.702
SWIGLU_LIMIT = 7.0
ROPE_THETA = 10000.0
NORM_EPS = 1e-6
MASK_NEG = -1e30
F_MIN = 1e-12

LANES = 128
ROW_TILE = 256
CHUNK = 64
SUB = 16
MOE_TILE = 256
VMEM_LIMIT = 56 * 1024 * 1024

PA_W = 5 * A_W
PB_W = 2 * HEADS * B_QK + 2 * B_W + LANES
PC_IN = 3 * C_W
PC_OUT = 4 * C_W
W_IN_PAD = PA_W + PB_W + PC_IN


def _cparams(sem):
    return pltpu.CompilerParams(dimension_semantics=sem, vmem_limit_bytes=VMEM_LIMIT)


def _nt(a, b):
    return lax.dot_general(a, b, (((1,), (1,)), ((), ())), preferred_element_type=F32)


def _tn(a, b, precision=None):
    return lax.dot_general(a, b, (((0,), (0,)), ((), ())), preferred_element_type=F32, precision=precision)


def _rms(x):
    return x * lax.rsqrt(jnp.mean(x * x, axis=-1, keepdims=True) + NORM_EPS)


def _silu(x):
    return x * jax.nn.sigmoid(x)


def _adaln_kernel(c_ref, w_ref, b_ref, o_ref):
    cond = _silu(c_ref[...])
    o_ref[...] = jnp.dot(cond, w_ref[...], preferred_element_type=F32, precision=HI) + b_ref[...]


def _adaln(cvec, ada_w, ada_b, layer):
    rows, d = cvec.shape
    return pl.pallas_call(
        _adaln_kernel,
        out_shape=jax.ShapeDtypeStruct((rows, 6 * d), F32),
        grid=(6,),
        in_specs=[pl.BlockSpec((rows, d), lambda j: (0, 0)),
                  pl.BlockSpec((None, d, d), lambda j: (layer, 0, j)),
                  pl.BlockSpec((None, 1, d), lambda j: (layer, 0, j))],
        out_specs=pl.BlockSpec((rows, d), lambda j: (0, j)),
        compiler_params=_cparams(("arbitrary",)),
        name="adaln",
    )(cvec, ada_w, ada_b)


def _inproj_kernel(x_ref, mod_ref, g_ref, w_ref, cos_ref, sin_ref, pa_ref, pb_ref, pc_ref):
    d = x_ref.shape[1]
    shift = mod_ref[0, :, 0:d]
    scale = mod_ref[0, :, d:2 * d]
    h = _rms(x_ref[...]) * g_ref[0:1, :] * (1.0 + scale) + shift
    hb = h.astype(BF16)
    pa_ref[...] = jnp.dot(hb, w_ref[:, 0:PA_W], preferred_element_type=F32)
    pb_ref[...] = jnp.dot(hb, w_ref[:, PA_W:PA_W + PB_W], preferred_element_type=F32)
    pc = jnp.dot(hb, w_ref[:, PA_W + PB_W:W_IN_PAD], preferred_element_type=F32)
    q = pc[:, 0:C_W]
    k = pc[:, C_W:2 * C_W]
    cos = cos_ref[...]
    sin = sin_ref[...]
    lane = lax.broadcasted_iota(jnp.int32, q.shape, 1)
    first = (lane % 32) < 16

    def rope(t):
        partner = jnp.where(first, pltpu.roll(t, C_W - 16, 1), pltpu.roll(t, 16, 1))
        return t * cos + partner * sin

    pc_ref[:, 0:C_W] = q.astype(BF16)
    pc_ref[:, C_W:2 * C_W] = rope(q).astype(BF16)
    pc_ref[:, 2 * C_W:3 * C_W] = rope(k).astype(BF16)
    pc_ref[:, 3 * C_W:4 * C_W] = pc[:, 2 * C_W:3 * C_W].astype(BF16)


def _inproj(xy, mod, norms, w_in_p, cos, sin, layer, tiles_per_batch, n_ctx_tiles, n_batch):
    t_all, d = xy.shape
    n_tiles = t_all // ROW_TILE

    def mod_row(i):
        return jnp.where(i % tiles_per_batch < n_ctx_tiles, n_batch, i // tiles_per_batch)

    return pl.pallas_call(
        _inproj_kernel,
        out_shape=(jax.ShapeDtypeStruct((t_all, PA_W), F32),
                   jax.ShapeDtypeStruct((t_all, PB_W), F32),
                   jax.ShapeDtypeStruct((t_all, PC_OUT), BF16)),
        grid=(n_tiles,),
        in_specs=[pl.BlockSpec((ROW_TILE, d), lambda i: (i, 0)),
                  pl.BlockSpec((1, 1, 6 * d), lambda i: (mod_row(i), 0, 0)),
                  pl.BlockSpec((None, 4, d), lambda i: (layer, 0, 0)),
                  pl.BlockSpec((None, d, W_IN_PAD), lambda i: (layer, 0, 0)),
                  pl.BlockSpec((ROW_TILE, C_W), lambda i: (i % tiles_per_batch, 0)),
                  pl.BlockSpec((ROW_TILE, C_W), lambda i: (i % tiles_per_batch, 0))],
        out_specs=(pl.BlockSpec((ROW_TILE, PA_W), lambda i: (i, 0)),
                   pl.BlockSpec((ROW_TILE, PB_W), lambda i: (i, 0)),
                   pl.BlockSpec((ROW_TILE, PC_OUT), lambda i: (i, 0))),
        compiler_params=_cparams(("arbitrary",)),
        name="inproj",
    )(xy, mod, norms, w_in_p, cos, sin)


def _hgrn_kernel(pa_ref, lb_ref, gain_ref, o_ref, st_ref, acc_ref, *, n_ctx, n_all):
    c_sz, w = CHUNK, A_W
    r_i = lax.broadcasted_iota(jnp.int32, (c_sz, c_sz), 0)
    c_i = lax.broadcasted_iota(jnp.int32, (c_sz, c_sz), 1)
    tri_incl_past = (c_i <= r_i).astype(F32)
    tri_incl_future = (c_i >= r_i).astype(F32)
    same_head = (lax.broadcasted_iota(jnp.int32, (w, w), 0) // HEAD_V
                 == lax.broadcasted_iota(jnp.int32, (w, w), 1) // HEAD_V)
    head_ones = same_head.astype(BF16)
    head_ones_f32 = same_head.astype(F32)
    sub_pos = lax.broadcasted_iota(jnp.int32, (c_sz, 1), 0) % SUB
    n_sub = c_sz // SUB

    def sub_rows(a, sp):
        return jnp.concatenate(
            [jnp.broadcast_to(a[SUB * i + sp:SUB * i + sp + 1, :], (SUB, w)) for i in range(n_sub)], axis=0)

    def chunk(c, rev, final):
        r0 = pl.multiple_of(c * c_sz, c_sz)
        rows = pl.ds(r0, c_sz)
        q_pre = pa_ref[0, rows, 0:A_W]
        v = pa_ref[0, rows, A_W:2 * A_W]
        f_pre = pa_ref[0, rows, 3 * A_W:4 * A_W] if rev else pa_ref[0, rows, 2 * A_W:3 * A_W]
        lb = lb_ref[1:2, :] if rev else lb_ref[0:1, :]
        q = _silu(q_pre)
        f = lb + (1.0 - lb) * jax.nn.sigmoid(f_pre)
        log_f = jnp.log(jnp.maximum(f, F_MIN))
        kk = (1.0 - lb) * jax.nn.sigmoid(-f_pre)
        cum = jnp.dot(tri_incl_future if rev else tri_incl_past, log_f, preferred_element_type=F32, precision=HI)
        e = 0 if rev else c_sz - 1
        cum_end = cum[e:e + 1, :]
        vb = v.astype(BF16)

        st = st_ref[...]
        acc_ref[...] = _nt((q * jnp.exp(cum)).astype(BF16), st.astype(BF16))
        k_end = (kk * jnp.exp(cum_end - cum)).astype(BF16)
        st_ref[...] = st * jnp.exp(cum_end) + jnp.where(same_head, _tn(vb, k_end), 0.0)

        def off(t0, s0, n, ref_row):
            ref = cum[ref_row:ref_row + 1, :]
            qa = (q[t0:t0 + n] * jnp.exp(cum[t0:t0 + n] - ref)).astype(BF16)
            ka = (kk[s0:s0 + n] * jnp.exp(ref - cum[s0:s0 + n])).astype(BF16)
            u = jnp.where(same_head, _tn(vb[s0:s0 + n], ka), 0.0).astype(BF16)
            acc_ref[t0:t0 + n, :] += _nt(qa, u)

        if rev:
            off(0, 32, 32, 32)
            off(0, 16, 16, 16)
            off(32, 48, 16, 48)
        else:
            off(32, 0, 32, 31)
            off(16, 0, 16, 15)
            off(48, 32, 16, 47)

        fs = []
        for sp in range(SUB):
            live = (sub_pos <= sp) if rev else (sub_pos >= sp)
            arg = jnp.where(live, cum - sub_rows(cum, sp), MASK_NEG)
            fs.append((q * jnp.exp(arg) * sub_rows(kk, sp)).astype(BF16))
        r = jnp.dot(jnp.concatenate(fs, axis=0), head_ones, preferred_element_type=F32)
        o = acc_ref[...]
        for sp in range(SUB):
            o = o + r[sp * c_sz:(sp + 1) * c_sz, :] * sub_rows(v, sp)

        if not final:
            o_ref[0, rows, :] = o
        else:
            tot = o_ref[0, rows, :] + o
            ms = jnp.dot(tot * tot, head_ones_f32, preferred_element_type=F32, precision=HI) * (1.0 / HEAD_V)
            g = pa_ref[0, rows, 4 * A_W:5 * A_W]
            o_ref[0, rows, :] = tot * lax.rsqrt(ms + NORM_EPS) * gain_ref[...] * _silu(g)

    def run(first, count, rev, final):
        def body(n, carry):
            chunk(first - n if rev else first + n, rev, final)
            return carry
        lax.fori_loop(0, count, body, 0)

    st_ref[...] = jnp.zeros_like(st_ref)
    run(0, n_all, False, False)
    st_ref[...] = jnp.zeros_like(st_ref)
    run(n_ctx - 1, n_ctx, True, True)
    run(n_all - 1, n_all - n_ctx, True, True)


def _hgrn(pa3, lb, gain, n_ctx_rows):
    n_batch, rows, _ = pa3.shape
    kern = functools.partial(_hgrn_kernel, n_ctx=n_ctx_rows // CHUNK, n_all=rows // CHUNK)
    return pl.pallas_call(
        kern,
        out_shape=jax.ShapeDtypeStruct((n_batch, rows, A_W), F32),
        grid=(n_batch,),
        in_specs=[pl.BlockSpec((1, rows, PA_W), lambda b: (b, 0, 0)),
                  pl.BlockSpec((2, A_W), lambda b: (0, 0)),
                  pl.BlockSpec((1, A_W), lambda b: (0, 0))],
        out_specs=pl.BlockSpec((1, rows, A_W), lambda b: (b, 0, 0)),
        scratch_shapes=[pltpu.VMEM((A_W, A_W), F32), pltpu.VMEM((CHUNK, A_W), F32)],
        compiler_params=_cparams(("arbitrary",)),
        name="hgrn2",
    )(pa3, lb, gain)


def _mlstm_kernel(pb_ref, bias_ref, gain_ref, o_ref, c_ref, *, n_ctx, n_all):
    c_sz = CHUNK
    r_i = lax.broadcasted_iota(jnp.int32, (c_sz, c_sz), 0)
    c_i = lax.broadcasted_iota(jnp.int32, (c_sz, c_sz), 1)
    past = c_i <= r_i
    future = c_i >= r_i
    eye = (c_i == r_i).astype(F32)
    lane = lax.broadcasted_iota(jnp.int32, (1, LANES), 1)
    ones_col = (lax.broadcasted_iota(jnp.int32, (c_sz, HEAD_V), 1) == 0).astype(F32)
    q_off, k_off, v_off, o_off, g_off = 0, HEADS * B_QK, 2 * HEADS * B_QK, 2 * HEADS * B_QK + B_W, 2 * HEADS * B_QK + 2 * B_W

    def chunk(c, rev, final, mvec):
        r0 = pl.multiple_of(c * c_sz, c_sz)
        rows = pl.ds(r0, c_sz)
        q = pb_ref[0, rows, q_off:q_off + HEADS * B_QK] * (B_QK ** -0.5)
        k = pb_ref[0, rows, k_off:k_off + HEADS * B_QK]
        v = pb_ref[0, rows, v_off:v_off + B_W]
        gts = pb_ref[0, rows, g_off:g_off + LANES] + bias_ref[...]
        log_f = jnp.minimum(gts, 0.0) - jnp.log(1.0 + jnp.exp(-jnp.abs(gts)))
        mask = future if rev else past
        tri = mask.astype(F32)
        tri_t = (past if rev else future).astype(F32)
        cum_col = jnp.dot(tri, log_f, preferred_element_type=F32, precision=HI)
        cum_row = _tn(log_f, tri_t, precision=HI)
        i_row = _tn(gts, eye, precision=HI)
        e = 0 if rev else c_sz - 1
        qb = q.astype(BF16)
        kb = k.astype(BF16)
        outs = []
        for h in range(HEADS):
            il = (2 * HEADS if rev else 0) + h
            fl = il + HEADS
            cc = cum_col[:, fl:fl + 1]
            cr = cum_row[fl:fl + 1, :]
            ir = i_row[il:il + 1, :]
            ic = gts[:, il:il + 1]
            m_prev = mvec[:, h:h + 1]
            log_d = jnp.where(mask, cc - cr + ir, MASK_NEG)
            log_inter = cc + m_prev
            m_t = jnp.maximum(log_inter, jnp.max(log_d, axis=-1, keepdims=True))
            w_intra = jnp.where(mask, jnp.exp(log_d - m_t), 0.0)
            w_inter = jnp.exp(log_inter - m_t)
            qh = qb[:, h * B_QK:(h + 1) * B_QK]
            kh = kb[:, h * B_QK:(h + 1) * B_QK]
            v_aug = jnp.concatenate([v[:, h * HEAD_V:(h + 1) * HEAD_V], ones_col], axis=1).astype(BF16)
            scores = _nt(qh, kh) * w_intra
            cm = c_ref[h]
            nd = (w_inter * jnp.dot(qh, cm.astype(BF16), preferred_element_type=F32)
                  + jnp.dot(scores.astype(BF16), v_aug, preferred_element_type=F32))
            den = nd[:, HEAD_V:HEAD_V + 1]
            outs.append(nd[:, 0:HEAD_V] / jnp.maximum(jnp.abs(den), jnp.exp(-m_t)))
            ce = cc[e:e + 1, :]
            log_end = ce - cc + ic
            m_end = jnp.maximum(ce + m_prev, jnp.max(log_end, axis=0, keepdims=True))
            w_end = jnp.exp(log_end - m_end)
            w_carry = jnp.exp(ce + m_prev - m_end)
            k_w = (k[:, h * B_QK:(h + 1) * B_QK] * w_end).astype(BF16)
            c_ref[h] = w_carry * cm + _tn(k_w, v_aug)
            mvec = jnp.where(lane == h, m_end, mvec)
        hcat = jnp.concatenate(outs, axis=1)
        if not final:
            o_ref[0, rows, :] = hcat
        else:
            tot = o_ref[0, rows, :] + hcat
            normed = jnp.concatenate(
                [_rms(tot[:, h * HEAD_V:(h + 1) * HEAD_V]) for h in range(HEADS)], axis=1) * gain_ref[...]
            og = pb_ref[0, rows, o_off:o_off + B_W]
            o_ref[0, rows, :] = normed * jax.nn.sigmoid(og)
        return mvec

    def run(first, count, rev, final, mvec):
        return lax.fori_loop(0, count, lambda n, m: chunk(first - n if rev else first + n, rev, final, m), mvec)

    zero_m = jnp.zeros((1, LANES), F32)
    c_ref[...] = jnp.zeros_like(c_ref)
    run(0, n_all, False, False, zero_m)
    c_ref[...] = jnp.zeros_like(c_ref)
    m_ctx = run(n_ctx - 1, n_ctx, True, True, zero_m)
    run(n_all - 1, n_all - n_ctx, True, True, m_ctx)


def _mlstm(pb3, bias, gain, n_ctx_rows):
    n_batch, rows, _ = pb3.shape
    kern = functools.partial(_mlstm_kernel, n_ctx=n_ctx_rows // CHUNK, n_all=rows // CHUNK)
    return pl.pallas_call(
        kern,
        out_shape=jax.ShapeDtypeStruct((n_batch, rows, B_W), F32),
        grid=(n_batch,),
        in_specs=[pl.BlockSpec((1, rows, PB_W), lambda b: (b, 0, 0)),
                  pl.BlockSpec((1, LANES), lambda b: (0, 0)),
                  pl.BlockSpec((1, B_W), lambda b: (0, 0))],
        out_specs=pl.BlockSpec((1, rows, B_W), lambda b: (b, 0, 0)),
        scratch_shapes=[pltpu.VMEM((HEADS, B_QK, 2 * HEAD_V), F32)],
        compiler_params=_cparams(("arbitrary",)),
        name="mlstm",
    )(pb3, bias, gain)


def _attn_kernel(lam_ref, qp_ref, qr_ref, k_ref, v_ref, gain_ref, o_ref, *, n_ctx, q_tile0, lam_init):
    lam = lam_ref[0]
    q_tile = pl.program_id(2) + q_tile0
    lane = lax.broadcasted_iota(jnp.int32, (1, 2 * C_DQK), 1)
    scale = C_DQK ** -0.5

    def finish(parts):
        o = parts[0] - lam * parts[1]
        o_ref[0] = _rms(o) * gain_ref[...] * (1.0 - lam_init)

    def softmax_pv(s, v):
        s = s * scale
        ex = jnp.exp(s - jnp.max(s, axis=-1, keepdims=True))
        return jnp.dot(ex.astype(BF16), v, preferred_element_type=F32) / jnp.sum(ex, axis=-1, keepdims=True)

    def sub_query(q, j):
        return jnp.where(lane // C_DQK == j, q, jnp.zeros_like(q))

    @pl.when(q_tile * ROW_TILE < n_ctx)
    def _():
        kc = k_ref[0, 0:n_ctx, :]
        vc = v_ref[0, 0:n_ctx, :]
        finish([softmax_pv(_nt(sub_query(qp_ref[0], j), kc), vc) for j in range(2)])

    @pl.when(q_tile * ROW_TILE >= n_ctx)
    def _():
        n_all = k_ref.shape[1]
        kc = k_ref[0, 0:n_ctx, :]
        kl = k_ref[0, n_ctx:n_all, :]
        v = v_ref[0]
        parts = []
        for j in range(2):
            s = jnp.concatenate([_nt(sub_query(qp_ref[0], j), kc), _nt(sub_query(qr_ref[0], j), kl)], axis=1)
            parts.append(softmax_pv(s, v))
        finish(parts)


def _attention(pc3, lam, gain, n_ctx_rows, lam_init, with_ctx):
    n_batch, rows, _ = pc3.shape
    q_tile0 = 0 if with_ctx else n_ctx_rows // ROW_TILE
    n_q = rows // ROW_TILE - q_tile0
    hb = C_W // LANES
    kern = functools.partial(_attn_kernel, n_ctx=n_ctx_rows, q_tile0=q_tile0, lam_init=lam_init)
    grid_spec = pltpu.PrefetchScalarGridSpec(
        num_scalar_prefetch=1,
        grid=(n_batch, HEADS, n_q),
        in_specs=[pl.BlockSpec((1, ROW_TILE, LANES), lambda b, h, i, lam: (b, i + q_tile0, h)),
                  pl.BlockSpec((1, ROW_TILE, LANES), lambda b, h, i, lam: (b, i + q_tile0, hb + h)),
                  pl.BlockSpec((1, rows, LANES), lambda b, h, i, lam: (b, 0, 2 * hb + h)),
                  pl.BlockSpec((1, rows, LANES), lambda b, h, i, lam: (b, 0, 3 * hb + h)),
                  pl.BlockSpec((1, LANES), lambda b, h, i, lam: (0, 0))],
        out_specs=pl.BlockSpec((1, ROW_TILE, LANES), lambda b, h, i, lam: (b, i + q_tile0, h)),
    )
    return pl.pallas_call(
        kern,
        out_shape=jax.ShapeDtypeStruct((n_batch, rows, C_W), F32),
        grid_spec=grid_spec,
        compiler_params=_cparams(("arbitrary", "arbitrary", "arbitrary")),
        name="diff_attn",
    )(lam, pc3, pc3, pc3, pc3, gain)


def _outproj_kernel(oa_ref, ob_ref, oc_ref, x_ref, mod_ref, n_ref, w_ref, rw_ref, rb_ref,
                    xn_ref, h_ref, route_ref, cnt_ref, carry_ref):
    d = x_ref.shape[1]
    n_exp = rw_ref.shape[1]

    @pl.when(pl.program_id(0) == 0)
    def _():
        carry_ref[...] = jnp.zeros_like(carry_ref)

    mix = (jnp.dot(oa_ref[...].astype(BF16), w_ref[0:A_W, :], preferred_element_type=F32)
           + jnp.dot(ob_ref[...].astype(BF16), w_ref[A_W:A_W + B_W, :], preferred_element_type=F32)
           + jnp.dot(oc_ref[...].astype(BF16), w_ref[A_W + B_W:A_W + B_W + C_W, :], preferred_element_type=F32))
    xn = x_ref[...] + mod_ref[0, :, 2 * d:3 * d] * (_rms(mix) * n_ref[1:2, :])
    xn_ref[...] = xn
    h = _rms(xn) * n_ref[2:3, :] * (1.0 + mod_ref[0, :, 4 * d:5 * d]) + mod_ref[0, :, 3 * d:4 * d]
    h_ref[...] = h.astype(BF16)

    logits = jnp.dot(h, rw_ref[...], preferred_element_type=F32, precision=HI) + rb_ref[...]
    e_lane = lax.broadcasted_iota(jnp.int32, logits.shape, 1)
    cur = logits
    picks, vals = [], []
    for _ in range(TOP_K):
        mx = jnp.max(cur, axis=-1, keepdims=True)
        idx = jnp.min(jnp.where(cur == mx, e_lane, n_exp), axis=-1, keepdims=True)
        hit = e_lane == idx
        cur = jnp.where(hit, -jnp.inf, cur)
        picks.append((idx, hit.astype(F32)))
        vals.append(mx)
    exps = [jnp.exp(vv - vals[0]) for vv in vals]
    total = exps[0] + exps[1] + exps[2] + exps[3]

    chosen = picks[0][1] + picks[1][1] + picks[2][1] + picks[3][1]
    tm = logits.shape[0]
    before = (lax.broadcasted_iota(jnp.int32, (tm, tm), 1) < lax.broadcasted_iota(jnp.int32, (tm, tm), 0))
    seen = jnp.dot(before.astype(BF16), chosen.astype(BF16), preferred_element_type=F32) + carry_ref[...]
    new_carry = carry_ref[...] + jnp.sum(chosen, axis=0, keepdims=True)
    carry_ref[...] = new_carry
    cnt_ref[...] = new_carry

    o_lane = lax.broadcasted_iota(jnp.int32, (tm, LANES), 1)
    route = jnp.zeros((tm, LANES), F32)
    for kk in range(TOP_K):
        idx, hit = picks[kk]
        rank = jnp.sum(hit * seen, axis=-1, keepdims=True)
        route = (route + jnp.where(o_lane == kk, idx.astype(F32), 0.0)
                 + jnp.where(o_lane == TOP_K + kk, exps[kk] / total, 0.0)
                 + jnp.where(o_lane == 2 * TOP_K + kk, rank, 0.0))
    route_ref[...] = route


def _outproj(oa, ob, oc, xy, mod, norms, w_out_b, router_w, router_b, layer, n_tiles, in_tile, mod_row):
    d = xy.shape[1]
    n_exp = router_w.shape[2]
    t_out = n_tiles * ROW_TILE
    return pl.pallas_call(
        _outproj_kernel,
        out_shape=(jax.ShapeDtypeStruct((t_out, d), F32),
                   jax.ShapeDtypeStruct((t_out, d), BF16),
                   jax.ShapeDtypeStruct((t_out, LANES), F32),
                   jax.ShapeDtypeStruct((1, n_exp), F32)),
        grid=(n_tiles,),
        in_specs=[pl.BlockSpec((ROW_TILE, A_W), lambda i: (in_tile(i), 0)),
                  pl.BlockSpec((ROW_TILE, B_W), lambda i: (in_tile(i), 0)),
                  pl.BlockSpec((ROW_TILE, C_W), lambda i: (in_tile(i), 0)),
                  pl.BlockSpec((ROW_TILE, d), lambda i: (in_tile(i), 0)),
                  pl.BlockSpec((1, 1, 6 * d), lambda i: (mod_row(i), 0, 0)),
                  pl.BlockSpec((None, 4, d), lambda i: (layer, 0, 0)),
                  pl.BlockSpec((None, d, d), lambda i: (layer, 0, 0)),
                  pl.BlockSpec((None, d, n_exp), lambda i: (layer, 0, 0)),
                  pl.BlockSpec((None, 1, n_exp), lambda i: (layer, 0, 0))],
        out_specs=(pl.BlockSpec((ROW_TILE, d), lambda i: (i, 0)),
                   pl.BlockSpec((ROW_TILE, d), lambda i: (i, 0)),
                   pl.BlockSpec((ROW_TILE, LANES), lambda i: (i, 0)),
                   pl.BlockSpec((1, n_exp), lambda i: (0, 0))),
        scratch_shapes=[pltpu.VMEM((1, n_exp), F32)],
        compiler_params=_cparams(("arbitrary",)),
        name="outproj_router",
    )(oa, ob, oc, xy, mod, norms, w_out_b, router_w, router_b)


def _moe_kernel(te_ref, nu_ref, xs_ref, gw_ref, w1g_ref, w1l_ref, b1g_ref, b1l_ref, w2_ref, b2_ref, ys_ref):
    i = pl.program_id(0)

    @pl.when(i < nu_ref[0])
    def _():
        x = xs_ref[...]
        glu = jnp.minimum(jnp.dot(x, w1g_ref[...], preferred_element_type=F32) + b1g_ref[...], SWIGLU_LIMIT)
        lin = jnp.clip(jnp.dot(x, w1l_ref[...], preferred_element_type=F32) + b1l_ref[...],
                       -SWIGLU_LIMIT, SWIGLU_LIMIT)
        gate = gw_ref[...]
        act = glu * jax.nn.sigmoid(SWIGLU_ALPHA * glu) * (lin + 1.0) * gate
        y = jnp.dot(act.astype(BF16), w2_ref[...], preferred_element_type=F32) + gate * b2_ref[...]
        ys_ref[...] = y.astype(ys_ref.dtype)

    @pl.when(i >= nu_ref[0])
    def _():
        ys_ref[...] = jnp.zeros_like(ys_ref)


def _moe(tile_expert, n_used, xs, gws, w1g, w1l, b1g, b1l, w2, b2, layer):
    r_max, d = xs.shape
    f = w1g.shape[3]
    n_tiles = r_max // MOE_TILE

    def row_tile(i, te, nu):
        return jnp.minimum(i, nu[0] - 1)

    grid_spec = pltpu.PrefetchScalarGridSpec(
        num_scalar_prefetch=2,
        grid=(n_tiles,),
        in_specs=[pl.BlockSpec((MOE_TILE, d), lambda i, te, nu: (row_tile(i, te, nu), 0)),
                  pl.BlockSpec((MOE_TILE, 1), lambda i, te, nu: (row_tile(i, te, nu), 0)),
                  pl.BlockSpec((None, None, d, f), lambda i, te, nu: (layer, te[i], 0, 0)),
                  pl.BlockSpec((None, None, d, f), lambda i, te, nu: (layer, te[i], 0, 0)),
                  pl.BlockSpec((None, None, 1, f), lambda i, te, nu: (layer, te[i], 0, 0)),
                  pl.BlockSpec((None, None, 1, f), lambda i, te, nu: (layer, te[i], 0, 0)),
                  pl.BlockSpec((None, None, f, d), lambda i, te, nu: (layer, te[i], 0, 0)),
                  pl.BlockSpec((None, None, 1, d), lambda i, te, nu: (layer, te[i], 0, 0))],
        out_specs=pl.BlockSpec((MOE_TILE, d), lambda i, te, nu: (i, 0)),
    )
    return pl.pallas_call(
        _moe_kernel,
        out_shape=jax.ShapeDtypeStruct((r_max, d), BF16),
        grid_spec=grid_spec,
        compiler_params=_cparams(("arbitrary",)),
        name="moe_experts",
    )(tile_expert, n_used, xs, gws, w1g, w1l, b1g, b1l, w2, b2)


def _ffn_residual_kernel(x_ref, f_ref, mod_ref, n_ref, o_ref):
    d = x_ref.shape[1]
    o_ref[...] = x_ref[...] + mod_ref[0, :, 5 * d:6 * d] * (_rms(f_ref[...]) * n_ref[3:4, :])


def _ffn_residual(xn, ffn, mod, norms, layer, mod_row):
    t, d = xn.shape
    return pl.pallas_call(
        _ffn_residual_kernel,
        out_shape=jax.ShapeDtypeStruct((t, d), F32),
        grid=(t // ROW_TILE,),
        in_specs=[pl.BlockSpec((ROW_TILE, d), lambda i: (i, 0)),
                  pl.BlockSpec((ROW_TILE, d), lambda i: (i, 0)),
                  pl.BlockSpec((1, 1, 6 * d), lambda i: (mod_row(i), 0, 0)),
                  pl.BlockSpec((None, 4, d), lambda i: (layer, 0, 0))],
        out_specs=pl.BlockSpec((ROW_TILE, d), lambda i: (i, 0)),
        compiler_params=_cparams(("arbitrary",)),
        name="ffn_residual",
    )(xn, ffn, mod, norms)


def _dispatch_plan(route, counts, n_exp):
    t = route.shape[0]
    idx = route[:, 0:TOP_K].astype(jnp.int32)
    wts = route[:, TOP_K:2 * TOP_K]
    rank = route[:, 2 * TOP_K:3 * TOP_K].astype(jnp.int32)
    cnt = counts[0].astype(jnp.int32)
    padded = (cnt + MOE_TILE - 1) // MOE_TILE * MOE_TILE
    ends = jnp.cumsum(padded)
    pos = (ends - padded)[idx] + rank
    r_max = t * TOP_K + n_exp * MOE_TILE
    n_tiles = r_max // MOE_TILE
    n_used = ends[-1] // MOE_TILE
    tile_ids = jnp.minimum(jnp.arange(n_tiles, dtype=jnp.int32), n_used - 1)
    tile_expert = jnp.searchsorted(ends // MOE_TILE, tile_ids, side="right").astype(jnp.int32)
    flat = pos.reshape(-1)
    src = jnp.zeros((r_max,), jnp.int32).at[flat].set(jnp.repeat(jnp.arange(t, dtype=jnp.int32), TOP_K))
    gws = jnp.zeros((r_max,), F32).at[flat].set(wts.reshape(-1))
    return pos, src, gws.reshape(r_max, 1), tile_expert, n_used.reshape(1).astype(jnp.int32)


def _rope_tables(length, n_ctx_rows):
    rows = length // GRID_W
    row = jnp.repeat(jnp.arange(rows, dtype=F32), GRID_W)
    col = jnp.tile(jnp.arange(GRID_W, dtype=F32), rows)
    n_freq = C_DQK // 4
    inv_freq = ROPE_THETA ** (-jnp.arange(n_freq, dtype=F32) / n_freq)
    ang_r = row[:, None] * inv_freq
    ang_c = col[:, None] * inv_freq
    cos = jnp.concatenate([jnp.cos(ang_r), jnp.cos(ang_r), jnp.cos(ang_c), jnp.cos(ang_c)], axis=-1)
    sin = jnp.concatenate([-jnp.sin(ang_r), jnp.sin(ang_r), -jnp.sin(ang_c), jnp.sin(ang_c)], axis=-1)
    reps = C_W // C_DQK
    cos = jnp.concatenate([jnp.ones((n_ctx_rows, C_W), F32), jnp.tile(cos, (1, reps))], axis=0)
    sin = jnp.concatenate([jnp.zeros((n_ctx_rows, C_W), F32), jnp.tile(sin, (1, reps))], axis=0)
    return cos, sin


def _hgrn_lower_bound(table, layer):
    p = jax.nn.softmax(table.astype(F32), axis=1)
    cum = jnp.cumsum(p, axis=1) - p[:, :1]
    return jnp.clip(cum[:, layer], 0.0, 1.0)


def kernel(x, c, ctx, c_ctx, ada_w, ada_b, sandwich_norms, w_in, w_out, hgrn_lower_bounds, hgrn_norm,
           mlstm_gate_bias, mlstm_norm, diff_lambdas, diff_norm, router_w, router_b, moe_w1, moe_b1,
           moe_w2, moe_b2):
    n_batch, seq, d = x.shape
    n_ctx_rows = ctx.shape[1]
    depth = w_in.shape[0]
    n_exp = router_w.shape[2]
    assert seq % ROW_TILE == 0 and n_ctx_rows % ROW_TILE == 0 and seq % GRID_W == 0
    rows_b = n_ctx_rows + seq
    tiles_b = rows_b // ROW_TILE
    ctx_tiles = n_ctx_rows // ROW_TILE
    lat_tiles = seq // ROW_TILE

    b_main = PA_W + 2 * HEADS * B_QK + 2 * B_W
    w_in_p = jnp.concatenate(
        [w_in[:, :, :b_main],
         jnp.pad(w_in[:, :, b_main:b_main + N_GATE], ((0, 0), (0, 0), (0, LANES - N_GATE))),
         w_in[:, :, b_main + N_GATE:]], axis=2).astype(BF16)
    w_out_b = w_out.astype(BF16)
    w1g = moe_w1[..., 0::2].astype(BF16)
    w1l = moe_w1[..., 1::2].astype(BF16)
    b1g = moe_b1[:, :, None, 0::2]
    b1l = moe_b1[:, :, None, 1::2]
    w2b = moe_w2.astype(BF16)
    b2 = moe_b2[:, :, None, :]
    ada_b3 = ada_b[:, None, :]
    router_b3 = router_b[:, None, :]
    gate_bias = jnp.pad(mlstm_gate_bias, ((0, 0), (0, LANES - N_GATE)))
    cos, sin = _rope_tables(seq, n_ctx_rows)

    cond_rows = (n_batch + 1 + 7) // 8 * 8
    cvec = jnp.zeros((cond_rows, d), F32).at[:n_batch].set(c).at[n_batch].set(c_ctx)

    def mod_row_all(i):
        return jnp.where(i % tiles_b < ctx_tiles, n_batch, i // tiles_b)

    xy = jnp.concatenate([ctx, x], axis=1).reshape(n_batch * rows_b, d)

    for layer in range(depth):
        last = layer == depth - 1
        mod = _adaln(cvec, ada_w, ada_b3, layer).reshape(cond_rows, 1, 6 * d)
        lb = _hgrn_lower_bound(hgrn_lower_bounds, layer)
        lam_init = 0.8 - 0.6 * math.exp(-0.3 * layer)
        lq1, lk1, lq2, lk2 = diff_lambdas[layer].astype(F32)
        lam = (jnp.exp(jnp.sum(lq1 * lk1)) - jnp.exp(jnp.sum(lq2 * lk2)) + lam_init).reshape(1)

        pa, pb, pc = _inproj(xy, mod, sandwich_norms, w_in_p, cos, sin, layer, tiles_b, ctx_tiles, n_batch)
        oa = _hgrn(pa.reshape(n_batch, rows_b, PA_W), lb, jnp.tile(hgrn_norm[layer], HEADS)[None, :], n_ctx_rows)
        ob = _mlstm(pb.reshape(n_batch, rows_b, PB_W), gate_bias[layer][None, :], mlstm_norm[layer][None, :],
                    n_ctx_rows)
        oc = _attention(pc.reshape(n_batch, rows_b, PC_OUT), lam, diff_norm[layer][None, :], n_ctx_rows,
                        lam_init, not last)

        if last:
            n_tiles = n_batch * lat_tiles
            in_tile = lambda i: (i // lat_tiles) * tiles_b + ctx_tiles + i % lat_tiles
            mod_row = lambda i: i // lat_tiles
        else:
            n_tiles = n_batch * tiles_b
            in_tile = lambda i: i
            mod_row = mod_row_all
        xn, h, route, counts = _outproj(
            oa.reshape(-1, A_W), ob.reshape(-1, B_W), oc.reshape(-1, C_W), xy, mod, sandwich_norms, w_out_b,
            router_w, router_b3, layer, n_tiles, in_tile, mod_row)

        pos, src, gws, tile_expert, n_used = _dispatch_plan(route, counts, n_exp)
        xs = jnp.take(h, src, axis=0)
        ys = _moe(tile_expert, n_used, xs, gws, w1g, w1l, b1g, b1l, w2b, b2, layer)
        ffn = jnp.take(ys, pos.reshape(-1), axis=0).reshape(-1, TOP_K, d).astype(F32).sum(axis=1)
        xy = _ffn_residual(xn, ffn, mod, sandwich_norms, layer, mod_row)

    return xy.reshape(n_batch, seq, d)
```

```python
import functools
import math

import jax
import jax.numpy as jnp
from jax import lax
from jax.experimental import pallas as pl
from jax.experimental.pallas import tpu as pltpu

F32 = jnp.float32
BF16 = jnp.bfloat16
HI = lax.Precision.HIGHEST

HEADS = 4
A_W = 256
B_QK = 32
B_W = 256
C_DQK = 64
C_W = 512
HEAD_V = 64
N_GATE = 16
GRID_W = 64
TOP_K = 4
SWIGLU_ALPHA = 1.702
SWIGLU_LIMIT = 7.0
ROPE_THETA = 10000.0
NORM_EPS = 1e-6
MASK_NEG = -1e30
F_MIN = 1e-12

LANES = 128
ROW_TILE = 256
CHUNK = 64
SUB = 16
MOE_TILE = 256
VMEM_LIMIT = 56 * 1024 * 1024

PA_W = 5 * A_W
PB_W = 2 * HEADS * B_QK + 2 * B_W + LANES
PC_IN = 3 * C_W
PC_OUT = 4 * C_W
W_IN_PAD = PA_W + PB_W + PC_IN


def _cparams(sem):
    return pltpu.CompilerParams(dimension_semantics=sem, vmem_limit_bytes=VMEM_LIMIT)


def _nt(a, b):
    return lax.dot_general(a, b, (((1,), (1,)), ((), ())), preferred_element_type=F32)


def _tn(a, b, precision=None):
    return lax.dot_general(a, b, (((0,), (0,)), ((), ())), preferred_element_type=F32, precision=precision)


def _rms(x):
    return x * lax.rsqrt(jnp.mean(x * x, axis=-1, keepdims=True) + NORM_EPS)


def _silu(x):
    return x * jax.nn.sigmoid(x)


def _adaln_kernel(c_ref, w_ref, b_ref, o_ref):
    cond = _silu(c_ref[...])
    o_ref[...] = jnp.dot(cond, w_ref[...], preferred_element_type=F32, precision=HI) + b_ref[...]


def _adaln(cvec, ada_w, ada_b, layer):
    rows, d = cvec.shape
    return pl.pallas_call(
        _adaln_kernel,
        out_shape=jax.ShapeDtypeStruct((rows, 6 * d), F32),
        grid=(6,),
        in_specs=[pl.BlockSpec((rows, d), lambda j: (0, 0)),
                  pl.BlockSpec((None, d, d), lambda j: (layer, 0, j)),
                  pl.BlockSpec((None, 1, d), lambda j: (layer, 0, j))],
        out_specs=pl.BlockSpec((rows, d), lambda j: (0, j)),
        compiler_params=_cparams(("arbitrary",)),
        name="adaln",
    )(cvec, ada_w, ada_b)


def _inproj_kernel(x_ref, mod_ref, g_ref, w_ref, cos_ref, sin_ref, pa_ref, pb_ref, pc_ref):
    d = x_ref.shape[1]
    shift = mod_ref[0, :, 0:d]
    scale = mod_ref[0, :, d:2 * d]
    h = _rms(x_ref[...]) * g_ref[0:1, :] * (1.0 + scale) + shift
    hb = h.astype(BF16)
    pa_ref[...] = jnp.dot(hb, w_ref[:, 0:PA_W], preferred_element_type=F32)
    pb_ref[...] = jnp.dot(hb, w_ref[:, PA_W:PA_W + PB_W], preferred_element_type=F32)
    pc = jnp.dot(hb, w_ref[:, PA_W + PB_W:W_IN_PAD], preferred_element_type=F32)
    q = pc[:, 0:C_W]
    k = pc[:, C_W:2 * C_W]
    cos = cos_ref[...]
    sin = sin_ref[...]
    lane = lax.broadcasted_iota(jnp.int32, q.shape, 1)
    first = (lane % 32) < 16

    def rope(t):
        partner = jnp.where(first, pltpu.roll(t, C_W - 16, 1), pltpu.roll(t, 16, 1))
        return t * cos + partner * sin

    pc_ref[:, 0:C_W] = q.astype(BF16)
    pc_ref[:, C_W:2 * C_W] = rope(q).astype(BF16)
    pc_ref[:, 2 * C_W:3 * C_W] = rope(k).astype(BF16)
    pc_ref[:, 3 * C_W:4 * C_W] = pc[:, 2 * C_W:3 * C_W].astype(BF16)


def _inproj(xy, mod, norms, w_in_p, cos, sin, layer, tiles_per_batch, n_ctx_tiles, n_batch):
    t_all, d = xy.shape
    n_tiles = t_all // ROW_TILE

    def mod_row(i):
        return jnp.where(i % tiles_per_batch < n_ctx_tiles, n_batch, i // tiles_per_batch)

    return pl.pallas_call(
        _inproj_kernel,
        out_shape=(jax.ShapeDtypeStruct((t_all, PA_W), F32),
                   jax.ShapeDtypeStruct((t_all, PB_W), F32),
                   jax.ShapeDtypeStruct((t_all, PC_OUT), BF16)),
        grid=(n_tiles,),
        in_specs=[pl.BlockSpec((ROW_TILE, d), lambda i: (i, 0)),
                  pl.BlockSpec((1, 1, 6 * d), lambda i: (mod_row(i), 0, 0)),
                  pl.BlockSpec((None, 4, d), lambda i: (layer, 0, 0)),
                  pl.BlockSpec((None, d, W_IN_PAD), lambda i: (layer, 0, 0)),
                  pl.BlockSpec((ROW_TILE, C_W), lambda i: (i % tiles_per_batch, 0)),
                  pl.BlockSpec((ROW_TILE, C_W), lambda i: (i % tiles_per_batch, 0))],
        out_specs=(pl.BlockSpec((ROW_TILE, PA_W), lambda i: (i, 0)),
                   pl.BlockSpec((ROW_TILE, PB_W), lambda i: (i, 0)),
                   pl.BlockSpec((ROW_TILE, PC_OUT), lambda i: (i, 0))),
        compiler_params=_cparams(("arbitrary",)),
        name="inproj",
    )(xy, mod, norms, w_in_p, cos, sin)


def _hgrn_kernel(pa_ref, lb_ref, gain_ref, o_ref, st_ref, acc_ref, *, n_ctx, n_all):
    c_sz, w = CHUNK, A_W
    r_i = lax.broadcasted_iota(jnp.int32, (c_sz, c_sz), 0)
    c_i = lax.broadcasted_iota(jnp.int32, (c_sz, c_sz), 1)
    tri_incl_past = (c_i <= r_i).astype(F32)
    tri_incl_future = (c_i >= r_i).astype(F32)
    same_head = (lax.broadcasted_iota(jnp.int32, (w, w), 0) // HEAD_V
                 == lax.broadcasted_iota(jnp.int32, (w, w), 1) // HEAD_V)
    head_ones = same_head.astype(BF16)
    head_ones_f32 = same_head.astype(F32)
    sub_pos = lax.broadcasted_iota(jnp.int32, (c_sz, 1), 0) % SUB
    n_sub = c_sz // SUB

    def sub_rows(a, sp):
        return jnp.concatenate(
            [jnp.broadcast_to(a[SUB * i + sp:SUB * i + sp + 1, :], (SUB, w)) for i in range(n_sub)], axis=0)

    def chunk(c, rev, final):
        r0 = pl.multiple_of(c * c_sz, c_sz)
        rows = pl.ds(r0, c_sz)
        q_pre = pa_ref[0, rows, 0:A_W]
        v = pa_ref[0, rows, A_W:2 * A_W]
        f_pre = pa_ref[0, rows, 3 * A_W:4 * A_W] if rev else pa_ref[0, rows, 2 * A_W:3 * A_W]
        lb = lb_ref[1:2, :] if rev else lb_ref[0:1, :]
        q = _silu(q_pre)
        f = lb + (1.0 - lb) * jax.nn.sigmoid(f_pre)
        log_f = jnp.log(jnp.maximum(f, F_MIN))
        kk = (1.0 - lb) * jax.nn.sigmoid(-f_pre)
        cum = jnp.dot(tri_incl_future if rev else tri_incl_past, log_f, preferred_element_type=F32, precision=HI)
        e = 0 if rev else c_sz - 1
        cum_end = cum[e:e + 1, :]
        vb = v.astype(BF16)

        st = st_ref[...]
        acc_ref[...] = _nt((q * jnp.exp(cum)).astype(BF16), st.astype(BF16))
        k_end = (kk * jnp.exp(cum_end - cum)).astype(BF16)
        st_ref[...] = st * jnp.exp(cum_end) + jnp.where(same_head, _tn(vb, k_end), 0.0)

        def off(t0, s0, n, ref_row):
            ref = cum[ref_row:ref_row + 1, :]
            qa = (q[t0:t0 + n] * jnp.exp(cum[t0:t0 + n] - ref)).astype(BF16)
            ka = (kk[s0:s0 + n] * jnp.exp(ref - cum[s0:s0 + n])).astype(BF16)
            u = jnp.where(same_head, _tn(vb[s0:s0 + n], ka), 0.0).astype(BF16)
            acc_ref[t0:t0 + n, :] += _nt(qa, u)

        if rev:
            off(0, 32, 32, 32)
            off(0, 16, 16, 16)
            off(32, 48, 16, 48)
        else:
            off(32, 0, 32, 31)
            off(16, 0, 16, 15)
            off(48, 32, 16, 47)

        fs = []
        for sp in range(SUB):
            live = (sub_pos <= sp) if rev else (sub_pos >= sp)
            arg = jnp.where(live, cum - sub_rows(cum, sp), MASK_NEG)
            fs.append((q * jnp.exp(arg) * sub_rows(kk, sp)).astype(BF16))
        r = jnp.dot(jnp.concatenate(fs, axis=0), head_ones, preferred_element_type=F32)
        o = acc_ref[...]
        for sp in range(SUB):
            o = o + r[sp * c_sz:(sp + 1) * c_sz, :] * sub_rows(v, sp)

        if not final:
            o_ref[0, rows, :] = o
        else:
            tot = o_ref[0, rows, :] + o
            ms = jnp.dot(tot * tot, head_ones_f32, preferred_element_type=F32, precision=HI) * (1.0 / HEAD_V)
            g = pa_ref[0, rows, 4 * A_W:5 * A_W]
            o_ref[0, rows, :] = tot * lax.rsqrt(ms + NORM_EPS) * gain_ref[...] * _silu(g)

    def run(first, count, rev, final):
        def body(n, carry):
            chunk(first - n if rev else first + n, rev, final)
            return carry
        lax.fori_loop(0, count, body, 0)

    st_ref[...] = jnp.zeros_like(st_ref)
    run(0, n_all, False, False)
    st_ref[...] = jnp.zeros_like(st_ref)
    run(n_ctx - 1, n_ctx, True, True)
    run(n_all - 1, n_all - n_ctx, True, True)


def _hgrn(pa3, lb, gain, n_ctx_rows):
    n_batch, rows, _ = pa3.shape
    kern = functools.partial(_hgrn_kernel, n_ctx=n_ctx_rows // CHUNK, n_all=rows // CHUNK)
    return pl.pallas_call(
        kern,
        out_shape=jax.ShapeDtypeStruct((n_batch, rows, A_W), F32),
        grid=(n_batch,),
        in_specs=[pl.BlockSpec((1, rows, PA_W), lambda b: (b, 0, 0)),
                  pl.BlockSpec((2, A_W), lambda b: (0, 0)),
                  pl.BlockSpec((1, A_W), lambda b: (0, 0))],
        out_specs=pl.BlockSpec((1, rows, A_W), lambda b: (b, 0, 0)),
        scratch_shapes=[pltpu.VMEM((A_W, A_W), F32), pltpu.VMEM((CHUNK, A_W), F32)],
        compiler_params=_cparams(("arbitrary",)),
        name="hgrn2",
    )(pa3, lb, gain)


def _mlstm_kernel(pb_ref, bias_ref, gain_ref, o_ref, c_ref, *, n_ctx, n_all):
    c_sz = CHUNK
    r_i = lax.broadcasted_iota(jnp.int32, (c_sz, c_sz), 0)
    c_i = lax.broadcasted_iota(jnp.int32, (c_sz, c_sz), 1)
    past = c_i <= r_i
    future = c_i >= r_i
    eye = (c_i == r_i).astype(F32)
    lane = lax.broadcasted_iota(jnp.int32, (1, LANES), 1)
    ones_col = (lax.broadcasted_iota(jnp.int32, (c_sz, HEAD_V), 1) == 0).astype(F32)
    q_off, k_off, v_off, o_off, g_off = 0, HEADS * B_QK, 2 * HEADS * B_QK, 2 * HEADS * B_QK + B_W, 2 * HEADS * B_QK + 2 * B_W

    def chunk(c, rev, final, mvec):
        r0 = pl.multiple_of(c * c_sz, c_sz)
        rows = pl.ds(r0, c_sz)
        q = pb_ref[0, rows, q_off:q_off + HEADS * B_QK] * (B_QK ** -0.5)
        k = pb_ref[0, rows, k_off:k_off + HEADS * B_QK]
        v = pb_ref[0, rows, v_off:v_off + B_W]
        gts = pb_ref[0, rows, g_off:g_off + LANES] + bias_ref[...]
        log_f = jnp.minimum(gts, 0.0) - jnp.log(1.0 + jnp.exp(-jnp.abs(gts)))
        mask = future if rev else past
        tri = mask.astype(F32)
        tri_t = (past if rev else future).astype(F32)
        cum_col = jnp.dot(tri, log_f, preferred_element_type=F32, precision=HI)
        cum_row = _tn(log_f, tri_t, precision=HI)
        i_row = _tn(gts, eye, precision=HI)
        e = 0 if rev else c_sz - 1
        qb = q.astype(BF16)
        kb = k.astype(BF16)
        outs = []
        for h in range(HEADS):
            il = (2 * HEADS if rev else 0) + h
            fl = il + HEADS
            cc = cum_col[:, fl:fl + 1]
            cr = cum_row[fl:fl + 1, :]
            ir = i_row[il:il + 1, :]
            ic = gts[:, il:il + 1]
            m_prev = mvec[:, h:h + 1]
            log_d = jnp.where(mask, cc - cr + ir, MASK_NEG)
            log_inter = cc + m_prev
            m_t = jnp.maximum(log_inter, jnp.max(log_d, axis=-1, keepdims=True))
            w_intra = jnp.where(mask, jnp.exp(log_d - m_t), 0.0)
            w_inter = jnp.exp(log_inter - m_t)
            qh = qb[:, h * B_QK:(h + 1) * B_QK]
            kh = kb[:, h * B_QK:(h + 1) * B_QK]
            v_aug = jnp.concatenate([v[:, h * HEAD_V:(h + 1) * HEAD_V], ones_col], axis=1).astype(BF16)
            scores = _nt(qh, kh) * w_intra
            cm = c_ref[h]
            nd = (w_inter * jnp.dot(qh, cm.astype(BF16), preferred_element_type=F32)
                  + jnp.dot(scores.astype(BF16), v_aug, preferred_element_type=F32))
            den = nd[:, HEAD_V:HEAD_V + 1]
            outs.append(nd[:, 0:HEAD_V] / jnp.maximum(jnp.abs(den), jnp.exp(-m_t)))
            ce = cc[e:e + 1, :]
            log_end = ce - cc + ic
            m_end = jnp.maximum(ce + m_prev, jnp.max(log_end, axis=0, keepdims=True))
            w_end = jnp.exp(log_end - m_end)
            w_carry = jnp.exp(ce + m_prev - m_end)
            k_w = (k[:, h * B_QK:(h + 1) * B_QK] * w_end).astype(BF16)
            c_ref[h] = w_carry * cm + _tn(k_w, v_aug)
            mvec = jnp.where(lane == h, m_end, mvec)
        hcat = jnp.concatenate(outs, axis=1)
        if not final:
            o_ref[0, rows, :] = hcat
        else:
            tot = o_ref[0, rows, :] + hcat
            normed = jnp.concatenate(
                [_rms(tot[:, h * HEAD_V:(h + 1) * HEAD_V]) for h in range(HEADS)], axis=1) * gain_ref[...]
            og = pb_ref[0, rows, o_off:o_off + B_W]
            o_ref[0, rows, :] = normed * jax.nn.sigmoid(og)
        return mvec

    def run(first, count, rev, final, mvec):
        return lax.fori_loop(0, count, lambda n, m: chunk(first - n if rev else first + n, rev, final, m), mvec)

    zero_m = jnp.zeros((1, LANES), F32)
    c_ref[...] = jnp.zeros_like(c_ref)
    run(0, n_all, False, False, zero_m)
    c_ref[...] = jnp.zeros_like(c_ref)
    m_ctx = run(n_ctx - 1, n_ctx, True, True, zero_m)
    run(n_all - 1, n_all - n_ctx, True, True, m_ctx)


def _mlstm(pb3, bias, gain, n_ctx_rows):
    n_batch, rows, _ = pb3.shape
    kern = functools.partial(_mlstm_kernel, n_ctx=n_ctx_rows // CHUNK, n_all=rows // CHUNK)
    return pl.pallas_call(
        kern,
        out_shape=jax.ShapeDtypeStruct((n_batch, rows, B_W), F32),
        grid=(n_batch,),
        in_specs=[pl.BlockSpec((1, rows, PB_W), lambda b: (b, 0, 0)),
                  pl.BlockSpec((1, LANES), lambda b: (0, 0)),
                  pl.BlockSpec((1, B_W), lambda b: (0, 0))],
        out_specs=pl.BlockSpec((1, rows, B_W), lambda b: (b, 0, 0)),
        scratch_shapes=[pltpu.VMEM((HEADS, B_QK, 2 * HEAD_V), F32)],
        compiler_params=_cparams(("arbitrary",)),
        name="mlstm",
    )(pb3, bias, gain)


def _attn_kernel(lam_ref, qp_ref, qr_ref, k_ref, v_ref, gain_ref, o_ref, *, n_ctx, q_tile0, lam_init):
    lam = lam_ref[0]
    q_tile = pl.program_id(2) + q_tile0
    lane = lax.broadcasted_iota(jnp.int32, (1, 2 * C_DQK), 1)
    scale = C_DQK ** -0.5

    def finish(parts):
        o = parts[0] - lam * parts[1]
        o_ref[0] = _rms(o) * gain_ref[...] * (1.0 - lam_init)

    def softmax_pv(s, v):
        s = s * scale
        ex = jnp.exp(s - jnp.max(s, axis=-1, keepdims=True))
        return jnp.dot(ex.astype(BF16), v, preferred_element_type=F32) / jnp.sum(ex, axis=-1, keepdims=True)

    def sub_query(q, j):
        return jnp.where(lane // C_DQK == j, q, jnp.zeros_like(q))

    @pl.when(q_tile * ROW_TILE < n_ctx)
    def _():
        kc = k_ref[0, 0:n_ctx, :]
        vc = v_ref[0, 0:n_ctx, :]
        finish([softmax_pv(_nt(sub_query(qp_ref[0], j), kc), vc) for j in range(2)])

    @pl.when(q_tile * ROW_TILE >= n_ctx)
    def _():
        n_all = k_ref.shape[1]
        kc = k_ref[0, 0:n_ctx, :]
        kl = k_ref[0, n_ctx:n_all, :]
        v = v_ref[0]
        parts = []
        for j in range(2):
            s = jnp.concatenate([_nt(sub_query(qp_ref[0], j), kc), _nt(sub_query(qr_ref[0], j), kl)], axis=1)
            parts.append(softmax_pv(s, v))
        finish(parts)


def _attention(pc3, lam, gain, n_ctx_rows, lam_init, with_ctx):
    n_batch, rows, _ = pc3.shape
    q_tile0 = 0 if with_ctx else n_ctx_rows // ROW_TILE
    n_q = rows // ROW_TILE - q_tile0
    hb = C_W // LANES
    kern = functools.partial(_attn_kernel, n_ctx=n_ctx_rows, q_tile0=q_tile0, lam_init=lam_init)
    grid_spec = pltpu.PrefetchScalarGridSpec(
        num_scalar_prefetch=1,
        grid=(n_batch, HEADS, n_q),
        in_specs=[pl.BlockSpec((1, ROW_TILE, LANES), lambda b, h, i, lam: (b, i + q_tile0, h)),
                  pl.BlockSpec((1, ROW_TILE, LANES), lambda b, h, i, lam: (b, i + q_tile0, hb + h)),
                  pl.BlockSpec((1, rows, LANES), lambda b, h, i, lam: (b, 0, 2 * hb + h)),
                  pl.BlockSpec((1, rows, LANES), lambda b, h, i, lam: (b, 0, 3 * hb + h)),
                  pl.BlockSpec((1, LANES), lambda b, h, i, lam: (0, 0))],
        out_specs=pl.BlockSpec((1, ROW_TILE, LANES), lambda b, h, i, lam: (b, i + q_tile0, h)),
    )
    return pl.pallas_call(
        kern,
        out_shape=jax.ShapeDtypeStruct((n_batch, rows, C_W), F32),
        grid_spec=grid_spec,
        compiler_params=_cparams(("arbitrary", "arbitrary", "arbitrary")),
        name="diff_attn",
    )(lam, pc3, pc3, pc3, pc3, gain)


def _outproj_kernel(oa_ref, ob_ref, oc_ref, x_ref, mod_ref, n_ref, w_ref, rw_ref, rb_ref,
                    xn_ref, h_ref, route_ref, cnt_ref, carry_ref):
    d = x_ref.shape[1]
    n_exp = rw_ref.shape[1]

    @pl.when(pl.program_id(0) == 0)
    def _():
        carry_ref[...] = jnp.zeros_like(carry_ref)

    mix = (jnp.dot(oa_ref[...].astype(BF16), w_ref[0:A_W, :], preferred_element_type=F32)
           + jnp.dot(ob_ref[...].astype(BF16), w_ref[A_W:A_W + B_W, :], preferred_element_type=F32)
           + jnp.dot(oc_ref[...].astype(BF16), w_ref[A_W + B_W:A_W + B_W + C_W, :], preferred_element_type=F32))
    xn = x_ref[...] + mod_ref[0, :, 2 * d:3 * d] * (_rms(mix) * n_ref[1:2, :])
    xn_ref[...] = xn
    h = _rms(xn) * n_ref[2:3, :] * (1.0 + mod_ref[0, :, 4 * d:5 * d]) + mod_ref[0, :, 3 * d:4 * d]
    h_ref[...] = h.astype(BF16)

    logits = jnp.dot(h, rw_ref[...], preferred_element_type=F32, precision=HI) + rb_ref[...]
    e_lane = lax.broadcasted_iota(jnp.int32, logits.shape, 1)
    cur = logits
    picks, vals = [], []
    for _ in range(TOP_K):
        mx = jnp.max(cur, axis=-1, keepdims=True)
        idx = jnp.min(jnp.where(cur == mx, e_lane, n_exp), axis=-1, keepdims=True)
        hit = e_lane == idx
        cur = jnp.where(hit, -jnp.inf, cur)
        picks.append((idx, hit.astype(F32)))
        vals.append(mx)
    exps = [jnp.exp(vv - vals[0]) for vv in vals]
    total = exps[0] + exps[1] + exps[2] + exps[3]

    chosen = picks[0][1] + picks[1][1] + picks[2][1] + picks[3][1]
    tm = logits.shape[0]
    before = (lax.broadcasted_iota(jnp.int32, (tm, tm), 1) < lax.broadcasted_iota(jnp.int32, (tm, tm), 0))
    seen = jnp.dot(before.astype(BF16), chosen.astype(BF16), preferred_element_type=F32) + carry_ref[...]
    new_carry = carry_ref[...] + jnp.sum(chosen, axis=0, keepdims=True)
    carry_ref[...] = new_carry
    cnt_ref[...] = new_carry

    o_lane = lax.broadcasted_iota(jnp.int32, (tm, LANES), 1)
    route = jnp.zeros((tm, LANES), F32)
    for kk in range(TOP_K):
        idx, hit = picks[kk]
        rank = jnp.sum(hit * seen, axis=-1, keepdims=True)
        route = (route + jnp.where(o_lane == kk, idx.astype(F32), 0.0)
                 + jnp.where(o_lane == TOP_K + kk, exps[kk] / total, 0.0)
                 + jnp.where(o_lane == 2 * TOP_K + kk, rank, 0.0))
    route_ref[...] = route


def _outproj(oa, ob, oc, xy, mod, norms, w_out_b, router_w, router_b, layer, n_tiles, in_tile, mod_row):
    d = xy.shape[1]
    n_exp = router_w.shape[2]
    t_out = n_tiles * ROW_TILE
    return pl.pallas_call(
        _outproj_kernel,
        out_shape=(jax.ShapeDtypeStruct((t_out, d), F32),
                   jax.ShapeDtypeStruct((t_out, d), BF16),
                   jax.ShapeDtypeStruct((t_out, LANES), F32),
                   jax.ShapeDtypeStruct((1, n_exp), F32)),
        grid=(n_tiles,),
        in_specs=[pl.BlockSpec((ROW_TILE, A_W), lambda i: (in_tile(i), 0)),
                  pl.BlockSpec((ROW_TILE, B_W), lambda i: (in_tile(i), 0)),
                  pl.BlockSpec((ROW_TILE, C_W), lambda i: (in_tile(i), 0)),
                  pl.BlockSpec((ROW_TILE, d), lambda i: (in_tile(i), 0)),
                  pl.BlockSpec((1, 1, 6 * d), lambda i: (mod_row(i), 0, 0)),
                  pl.BlockSpec((None, 4, d), lambda i: (layer, 0, 0)),
                  pl.BlockSpec((None, d, d), lambda i: (layer, 0, 0)),
                  pl.BlockSpec((None, d, n_exp), lambda i: (layer, 0, 0)),
                  pl.BlockSpec((None, 1, n_exp), lambda i: (layer, 0, 0))],
        out_specs=(pl.BlockSpec((ROW_TILE, d), lambda i: (i, 0)),
                   pl.BlockSpec((ROW_TILE, d), lambda i: (i, 0)),
                   pl.BlockSpec((ROW_TILE, LANES), lambda i: (i, 0)),
                   pl.BlockSpec((1, n_exp), lambda i: (0, 0))),
        scratch_shapes=[pltpu.VMEM((1, n_exp), F32)],
        compiler_params=_cparams(("arbitrary",)),
        name="outproj_router",
    )(oa, ob, oc, xy, mod, norms, w_out_b, router_w, router_b)


PAIR_BLOCK = 2 * LANES


def _moe_kernel(te_ref, first_ref, nu_ref, xs_ref, w1_ref, b1_ref, w2_ref, b2_ref, ys_ref, w1p_ref, w2b_ref):
    i = pl.program_id(0)
    two_f = w1_ref.shape[1]
    n_blk = two_f // PAIR_BLOCK

    @pl.when(jnp.logical_and(i < nu_ref[0], first_ref[i] == 1))
    def _():
        r = lax.broadcasted_iota(jnp.int32, (PAIR_BLOCK, PAIR_BLOCK), 0)
        c = lax.broadcasted_iota(jnp.int32, (PAIR_BLOCK, PAIR_BLOCK), 1)
        perm = (r == jnp.where(c < LANES, 2 * c, 2 * (c - LANES) + 1)).astype(BF16)
        for blk in range(n_blk):
            cols = slice(blk * PAIR_BLOCK, (blk + 1) * PAIR_BLOCK)
            w1p_ref[:, cols] = jnp.dot(w1_ref[:, cols].astype(BF16), perm,
                                       preferred_element_type=F32).astype(BF16)
        w2b_ref[...] = w2_ref[...].astype(BF16)

    @pl.when(i < nu_ref[0])
    def _():
        hid = jnp.dot(xs_ref[...], w1p_ref[...], preferred_element_type=F32) + b1_ref[...]
        acts = []
        for blk in range(n_blk):
            glu = jnp.minimum(hid[:, blk * PAIR_BLOCK:blk * PAIR_BLOCK + LANES], SWIGLU_LIMIT)
            lin = jnp.clip(hid[:, blk * PAIR_BLOCK + LANES:(blk + 1) * PAIR_BLOCK], -SWIGLU_LIMIT, SWIGLU_LIMIT)
            acts.append((glu * jax.nn.sigmoid(SWIGLU_ALPHA * glu) * (lin + 1.0)).astype(BF16))
        y = jnp.dot(jnp.concatenate(acts, axis=1), w2b_ref[...], preferred_element_type=F32) + b2_ref[...]
        ys_ref[...] = y.astype(ys_ref.dtype)

    @pl.when(i >= nu_ref[0])
    def _():
        ys_ref[...] = jnp.zeros_like(ys_ref)


def _moe(tile_expert, tile_first, n_used, xs, w1, b1p, w2, b2, layer):
    r_max, d = xs.shape
    two_f = w1.shape[3]
    f = two_f // 2
    n_tiles = r_max // MOE_TILE

    def row_tile(i, te, fi, nu):
        return jnp.minimum(i, nu[0] - 1)

    grid_spec = pltpu.PrefetchScalarGridSpec(
        num_scalar_prefetch=3,
        grid=(n_tiles,),
        in_specs=[pl.BlockSpec((MOE_TILE, d), lambda i, te, fi, nu: (row_tile(i, te, fi, nu), 0)),
                  pl.BlockSpec((None, None, d, two_f), lambda i, te, fi, nu: (layer, te[i], 0, 0)),
                  pl.BlockSpec((None, None, 1, two_f), lambda i, te, fi, nu: (layer, te[i], 0, 0)),
                  pl.BlockSpec((None, None, f, d), lambda i, te, fi, nu: (layer, te[i], 0, 0)),
                  pl.BlockSpec((None, None, 1, d), lambda i, te, fi, nu: (layer, te[i], 0, 0))],
        out_specs=pl.BlockSpec((MOE_TILE, d), lambda i, te, fi, nu: (i, 0)),
        scratch_shapes=[pltpu.VMEM((d, two_f), BF16), pltpu.VMEM((f, d), BF16)],
    )
    return pl.pallas_call(
        _moe_kernel,
        out_shape=jax.ShapeDtypeStruct((r_max, d), BF16),
        grid_spec=grid_spec,
        compiler_params=_cparams(("arbitrary",)),
        name="moe_experts",
    )(tile_expert, tile_first, n_used, xs, w1, b1p, w2, b2)


def _ffn_residual_kernel(x_ref, f_ref, mod_ref, n_ref, o_ref):
    d = x_ref.shape[1]
    o_ref[...] = x_ref[...] + mod_ref[0, :, 5 * d:6 * d] * (_rms(f_ref[...]) * n_ref[3:4, :])


def _ffn_residual(xn, ffn, mod, norms, layer, mod_row):
    t, d = xn.shape
    return pl.pallas_call(
        _ffn_residual_kernel,
        out_shape=jax.ShapeDtypeStruct((t, d), F32),
        grid=(t // ROW_TILE,),
        in_specs=[pl.BlockSpec((ROW_TILE, d), lambda i: (i, 0)),
                  pl.BlockSpec((ROW_TILE, d), lambda i: (i, 0)),
                  pl.BlockSpec((1, 1, 6 * d), lambda i: (mod_row(i), 0, 0)),
                  pl.BlockSpec((None, 4, d), lambda i: (layer, 0, 0))],
        out_specs=pl.BlockSpec((ROW_TILE, d), lambda i: (i, 0)),
        compiler_params=_cparams(("arbitrary",)),
        name="ffn_residual",
    )(xn, ffn, mod, norms)


def _dispatch_plan(route, counts, n_exp):
    t = route.shape[0]
    idx = route[:, 0:TOP_K].astype(jnp.int32)
    wts = route[:, TOP_K:2 * TOP_K]
    rank = route[:, 2 * TOP_K:3 * TOP_K].astype(jnp.int32)
    cnt = counts[0].astype(jnp.int32)
    padded = (cnt + MOE_TILE - 1) // MOE_TILE * MOE_TILE
    ends = jnp.cumsum(padded)
    pos = (ends - padded)[idx] + rank
    r_max = t * TOP_K + n_exp * MOE_TILE
    n_tiles = r_max // MOE_TILE
    n_used = ends[-1] // MOE_TILE
    tile_ids = jnp.minimum(jnp.arange(n_tiles, dtype=jnp.int32), n_used - 1)
    tile_expert = jnp.sum((ends // MOE_TILE)[None, :] <= tile_ids[:, None], axis=1).astype(jnp.int32)
    tile_first = jnp.concatenate(
        [jnp.ones((1,), jnp.int32), (tile_expert[1:] != tile_expert[:-1]).astype(jnp.int32)])
    flat = pos.reshape(-1)
    src = jnp.zeros((r_max,), jnp.int32).at[flat].set(jnp.repeat(jnp.arange(t, dtype=jnp.int32), TOP_K))
    return pos, wts, src, tile_expert, tile_first, n_used.reshape(1).astype(jnp.int32)


def _rope_tables(length, n_ctx_rows):
    rows = length // GRID_W
    row = jnp.repeat(jnp.arange(rows, dtype=F32), GRID_W)
    col = jnp.tile(jnp.arange(GRID_W, dtype=F32), rows)
    n_freq = C_DQK // 4
    inv_freq = ROPE_THETA ** (-jnp.arange(n_freq, dtype=F32) / n_freq)
    ang_r = row[:, None] * inv_freq
    ang_c = col[:, None] * inv_freq
    cos = jnp.concatenate([jnp.cos(ang_r), jnp.cos(ang_r), jnp.cos(ang_c), jnp.cos(ang_c)], axis=-1)
    sin = jnp.concatenate([-jnp.sin(ang_r), jnp.sin(ang_r), -jnp.sin(ang_c), jnp.sin(ang_c)], axis=-1)
    reps = C_W // C_DQK
    cos = jnp.concatenate([jnp.ones((n_ctx_rows, C_W), F32), jnp.tile(cos, (1, reps))], axis=0)
    sin = jnp.concatenate([jnp.zeros((n_ctx_rows, C_W), F32), jnp.tile(sin, (1, reps))], axis=0)
    return cos, sin


def _hgrn_lower_bound(table, layer):
    p = jax.nn.softmax(table.astype(F32), axis=1)
    cum = jnp.cumsum(p, axis=1) - p[:, :1]
    return jnp.clip(cum[:, layer], 0.0, 1.0)


def kernel(x, c, ctx, c_ctx, ada_w, ada_b, sandwich_norms, w_in, w_out, hgrn_lower_bounds, hgrn_norm,
           mlstm_gate_bias, mlstm_norm, diff_lambdas, diff_norm, router_w, router_b, moe_w1, moe_b1,
           moe_w2, moe_b2):
    n_batch, seq, d = x.shape
    n_ctx_rows = ctx.shape[1]
    depth = w_in.shape[0]
    n_exp = router_w.shape[2]
    assert seq % ROW_TILE == 0 and n_ctx_rows % ROW_TILE == 0 and seq % GRID_W == 0
    rows_b = n_ctx_rows + seq
    tiles_b = rows_b // ROW_TILE
    ctx_tiles = n_ctx_rows // ROW_TILE
    lat_tiles = seq // ROW_TILE

    b_main = PA_W + 2 * HEADS * B_QK + 2 * B_W
    w_in_p = jnp.concatenate(
        [w_in[:, :, :b_main],
         jnp.pad(w_in[:, :, b_main:b_main + N_GATE], ((0, 0), (0, 0), (0, LANES - N_GATE))),
         w_in[:, :, b_main + N_GATE:]], axis=2).astype(BF16)
    w_out_b = w_out.astype(BF16)
    two_f = moe_b1.shape[2]
    b1p = moe_b1.reshape(depth, n_exp, two_f // PAIR_BLOCK, LANES, 2).transpose(0, 1, 2, 4, 3).reshape(
        depth, n_exp, 1, two_f)
    b2 = moe_b2[:, :, None, :]
    ada_b3 = ada_b[:, None, :]
    router_b3 = router_b[:, None, :]
    gate_bias = jnp.pad(mlstm_gate_bias, ((0, 0), (0, LANES - N_GATE)))
    cos, sin = _rope_tables(seq, n_ctx_rows)

    cond_rows = (n_batch + 1 + 7) // 8 * 8
    cvec = jnp.zeros((cond_rows, d), F32).at[:n_batch].set(c).at[n_batch].set(c_ctx)

    def mod_row_all(i):
        return jnp.where(i % tiles_b < ctx_tiles, n_batch, i // tiles_b)

    xy = jnp.concatenate([ctx, x], axis=1).reshape(n_batch * rows_b, d)

    for layer in range(depth):
        last = layer == depth - 1
        mod = _adaln(cvec, ada_w, ada_b3, layer).reshape(cond_rows, 1, 6 * d)
        lb = _hgrn_lower_bound(hgrn_lower_bounds, layer)
        lam_init = 0.8 - 0.6 * math.exp(-0.3 * layer)
        lq1, lk1, lq2, lk2 = diff_lambdas[layer].astype(F32)
        lam = (jnp.exp(jnp.sum(lq1 * lk1)) - jnp.exp(jnp.sum(lq2 * lk2)) + lam_init).reshape(1)

        pa, pb, pc = _inproj(xy, mod, sandwich_norms, w_in_p, cos, sin, layer, tiles_b, ctx_tiles, n_batch)
        oa = _hgrn(pa.reshape(n_batch, rows_b, PA_W), lb, jnp.tile(hgrn_norm[layer], HEADS)[None, :], n_ctx_rows)
        ob = _mlstm(pb.reshape(n_batch, rows_b, PB_W), gate_bias[layer][None, :], mlstm_norm[layer][None, :],
                    n_ctx_rows)
        oc = _attention(pc.reshape(n_batch, rows_b, PC_OUT), lam, diff_norm[layer][None, :], n_ctx_rows,
                        lam_init, not last)

        if last:
            n_tiles = n_batch * lat_tiles
            in_tile = lambda i: (i // lat_tiles) * tiles_b + ctx_tiles + i % lat_tiles
            mod_row = lambda i: i // lat_tiles
        else:
            n_tiles = n_batch * tiles_b
            in_tile = lambda i: i
            mod_row = mod_row_all
        xn, h, route, counts = _outproj(
            oa.reshape(-1, A_W), ob.reshape(-1, B_W), oc.reshape(-1, C_W), xy, mod, sandwich_norms, w_out_b,
            router_w, router_b3, layer, n_tiles, in_tile, mod_row)

        pos, wts, src, tile_expert, tile_first, n_used = _dispatch_plan(route, counts, n_exp)
        xs = jnp.take(h, src, axis=0)
        ys = _moe(tile_expert, tile_first, n_used, xs, moe_w1, b1p, moe_w2, b2, layer)
        ffn = sum(wts[:, k:k + 1] * jnp.take(ys, pos[:, k], axis=0).astype(F32) for k in range(TOP_K))
        xy = _ffn_residual(xn, ffn, mod, sandwich_norms, layer, mod_row)

    return xy.reshape(n_batch, seq, d)
```

```python
import functools
import math

import jax
import jax.numpy as jnp
from jax import lax
from jax.experimental import pallas as pl
from jax.experimental.pallas import tpu as pltpu

F32 = jnp.float32
BF16 = jnp.bfloat16
HI = lax.Precision.HIGHEST

HEADS = 4
A_W = 256
B_QK = 32
B_W = 256
C_DQK = 64
C_W = 512
HEAD_V = 64
N_GATE = 16
GRID_W = 64
TOP_K = 4
SWIGLU_ALPHA = 1.702
SWIGLU_LIMIT = 7.0
ROPE_THETA = 10000.0
NORM_EPS = 1e-6
MASK_NEG = -1e30
F_MIN = 1e-12

LANES = 128
ROW_TILE = 256
CHUNK = 64
SUB = 16
MOE_TILE = 256
GATHER_ROWS = 16384
VMEM_LIMIT = 56 * 1024 * 1024

PA_W = 5 * A_W
PB_W = 2 * HEADS * B_QK + 2 * B_W + LANES
PC_IN = 3 * C_W
PC_OUT = 4 * C_W
W_IN_PAD = PA_W + PB_W + PC_IN


def _cparams(sem):
    return pltpu.CompilerParams(dimension_semantics=sem, vmem_limit_bytes=VMEM_LIMIT)


def _nt(a, b):
    return lax.dot_general(a, b, (((1,), (1,)), ((), ())), preferred_element_type=F32)


def _tn(a, b, precision=None):
    return lax.dot_general(a, b, (((0,), (0,)), ((), ())), preferred_element_type=F32, precision=precision)


def _rms(x):
    return x * lax.rsqrt(jnp.mean(x * x, axis=-1, keepdims=True) + NORM_EPS)


def _silu(x):
    return x * jax.nn.sigmoid(x)


def _adaln_kernel(c_ref, w_ref, b_ref, o_ref):
    cond = _silu(c_ref[...])
    o_ref[...] = jnp.dot(cond, w_ref[...], preferred_element_type=F32, precision=HI) + b_ref[...]


def _adaln(cvec, ada_w, ada_b, layer):
    rows, d = cvec.shape
    return pl.pallas_call(
        _adaln_kernel,
        out_shape=jax.ShapeDtypeStruct((rows, 6 * d), F32),
        grid=(6,),
        in_specs=[pl.BlockSpec((rows, d), lambda j: (0, 0)),
                  pl.BlockSpec((None, d, d), lambda j: (layer, 0, j)),
                  pl.BlockSpec((None, 1, d), lambda j: (layer, 0, j))],
        out_specs=pl.BlockSpec((rows, d), lambda j: (0, j)),
        compiler_params=_cparams(("arbitrary",)),
        name="adaln",
    )(cvec, ada_w, ada_b)


def _inproj_kernel(x_ref, mod_ref, g_ref, w_ref, cos_ref, sin_ref, pa_ref, pb_ref, pc_ref):
    d = x_ref.shape[1]
    shift = mod_ref[0, :, 0:d]
    scale = mod_ref[0, :, d:2 * d]
    h = _rms(x_ref[...]) * g_ref[0:1, :] * (1.0 + scale) + shift
    hb = h.astype(BF16)
    pa_ref[...] = jnp.dot(hb, w_ref[:, 0:PA_W], preferred_element_type=F32)
    pb_ref[...] = jnp.dot(hb, w_ref[:, PA_W:PA_W + PB_W], preferred_element_type=F32)
    pc = jnp.dot(hb, w_ref[:, PA_W + PB_W:W_IN_PAD], preferred_element_type=F32)
    q = pc[:, 0:C_W]
    k = pc[:, C_W:2 * C_W]
    cos = cos_ref[...]
    sin = sin_ref[...]
    lane = lax.broadcasted_iota(jnp.int32, q.shape, 1)
    first = (lane % 32) < 16

    def rope(t):
        partner = jnp.where(first, pltpu.roll(t, C_W - 16, 1), pltpu.roll(t, 16, 1))
        return t * cos + partner * sin

    pc_ref[:, 0:C_W] = q.astype(BF16)
    pc_ref[:, C_W:2 * C_W] = rope(q).astype(BF16)
    pc_ref[:, 2 * C_W:3 * C_W] = rope(k).astype(BF16)
    pc_ref[:, 3 * C_W:4 * C_W] = pc[:, 2 * C_W:3 * C_W].astype(BF16)


def _inproj(xy, mod, norms, w_in_p, cos, sin, layer, tiles_per_batch, n_ctx_tiles, n_batch):
    t_all, d = xy.shape
    n_tiles = t_all // ROW_TILE

    def mod_row(i):
        return jnp.where(i % tiles_per_batch < n_ctx_tiles, n_batch, i // tiles_per_batch)

    return pl.pallas_call(
        _inproj_kernel,
        out_shape=(jax.ShapeDtypeStruct((t_all, PA_W), F32),
                   jax.ShapeDtypeStruct((t_all, PB_W), F32),
                   jax.ShapeDtypeStruct((t_all, PC_OUT), BF16)),
        grid=(n_tiles,),
        in_specs=[pl.BlockSpec((ROW_TILE, d), lambda i: (i, 0)),
                  pl.BlockSpec((1, 1, 6 * d), lambda i: (mod_row(i), 0, 0)),
                  pl.BlockSpec((None, 4, d), lambda i: (layer, 0, 0)),
                  pl.BlockSpec((None, d, W_IN_PAD), lambda i: (layer, 0, 0)),
                  pl.BlockSpec((ROW_TILE, C_W), lambda i: (i % tiles_per_batch, 0)),
                  pl.BlockSpec((ROW_TILE, C_W), lambda i: (i % tiles_per_batch, 0))],
        out_specs=(pl.BlockSpec((ROW_TILE, PA_W), lambda i: (i, 0)),
                   pl.BlockSpec((ROW_TILE, PB_W), lambda i: (i, 0)),
                   pl.BlockSpec((ROW_TILE, PC_OUT), lambda i: (i, 0))),
        compiler_params=_cparams(("arbitrary",)),
        name="inproj",
    )(xy, mod, norms, w_in_p, cos, sin)


def _hgrn_kernel(pa_ref, lb_ref, gain_ref, o_ref, st_ref, acc_ref, ob_ref, *, n_ctx, n_all):
    c_sz, w = CHUNK, A_W
    r_i = lax.broadcasted_iota(jnp.int32, (c_sz, c_sz), 0)
    c_i = lax.broadcasted_iota(jnp.int32, (c_sz, c_sz), 1)
    tri_incl_past = (c_i <= r_i).astype(F32)
    tri_incl_future = (c_i >= r_i).astype(F32)
    same_head = (lax.broadcasted_iota(jnp.int32, (w, w), 0) // HEAD_V
                 == lax.broadcasted_iota(jnp.int32, (w, w), 1) // HEAD_V)
    head_ones = same_head.astype(BF16)
    head_ones_f32 = same_head.astype(F32)
    sub_pos = lax.broadcasted_iota(jnp.int32, (c_sz, 1), 0) % SUB
    n_sub = c_sz // SUB

    def sub_rows(a, sp):
        return jnp.concatenate(
            [jnp.broadcast_to(a[SUB * i + sp:SUB * i + sp + 1, :], (SUB, w)) for i in range(n_sub)], axis=0)

    def chunk(c, rev):
        d = 1 if rev else 0
        r0 = pl.multiple_of(c * c_sz, c_sz)
        rows = pl.ds(r0, c_sz)
        q_pre = pa_ref[0, rows, 0:A_W]
        v = pa_ref[0, rows, A_W:2 * A_W]
        f_pre = pa_ref[0, rows, 3 * A_W:4 * A_W] if rev else pa_ref[0, rows, 2 * A_W:3 * A_W]
        lb = lb_ref[1:2, :] if rev else lb_ref[0:1, :]
        q = _silu(q_pre)
        f = lb + (1.0 - lb) * jax.nn.sigmoid(f_pre)
        log_f = jnp.log(jnp.maximum(f, F_MIN))
        kk = (1.0 - lb) * jax.nn.sigmoid(-f_pre)
        cum = jnp.dot(tri_incl_future if rev else tri_incl_past, log_f, preferred_element_type=F32, precision=HI)
        e = 0 if rev else c_sz - 1
        cum_end = cum[e:e + 1, :]
        vb = v.astype(BF16)

        st = st_ref[d]
        acc_ref[d] = _nt((q * jnp.exp(cum)).astype(BF16), st.astype(BF16))
        k_end = (kk * jnp.exp(cum_end - cum)).astype(BF16)
        st_ref[d] = st * jnp.exp(cum_end) + jnp.where(same_head, _tn(vb, k_end), 0.0)

        def off(t0, s0, n, ref_row):
            ref = cum[ref_row:ref_row + 1, :]
            qa = (q[t0:t0 + n] * jnp.exp(cum[t0:t0 + n] - ref)).astype(BF16)
            ka = (kk[s0:s0 + n] * jnp.exp(ref - cum[s0:s0 + n])).astype(BF16)
            u = jnp.where(same_head, _tn(vb[s0:s0 + n], ka), 0.0).astype(BF16)
            acc_ref[d, t0:t0 + n, :] += _nt(qa, u)

        if rev:
            off(0, 32, 32, 32)
            off(0, 16, 16, 16)
            off(32, 48, 16, 48)
        else:
            off(32, 0, 32, 31)
            off(16, 0, 16, 15)
            off(48, 32, 16, 47)

        fs = []
        for sp in range(SUB):
            live = (sub_pos <= sp) if rev else (sub_pos >= sp)
            arg = jnp.where(live, cum - sub_rows(cum, sp), MASK_NEG)
            fs.append((q * jnp.exp(arg) * sub_rows(kk, sp)).astype(BF16))
        r = jnp.dot(jnp.concatenate(fs, axis=0), head_ones, preferred_element_type=F32)
        o = acc_ref[d]
        for sp in range(SUB):
            o = o + r[sp * c_sz:(sp + 1) * c_sz, :] * sub_rows(v, sp)
        if rev:
            ob_ref[rows, :] = o
        else:
            o_ref[0, rows, :] = o

    def scan_step(n, carry):
        chunk(n, False)
        chunk(jnp.where(n < n_ctx, n_ctx - 1 - n, n_all - 1 - (n - n_ctx)), True)
        return carry

    st_ref[...] = jnp.zeros_like(st_ref)
    lax.fori_loop(0, n_all, scan_step, 0)

    def readout(c, carry):
        rows = pl.ds(pl.multiple_of(c * ROW_TILE, ROW_TILE), ROW_TILE)
        tot = o_ref[0, rows, :] + ob_ref[rows, :]
        ms = jnp.dot(tot * tot, head_ones_f32, preferred_element_type=F32, precision=HI) * (1.0 / HEAD_V)
        g = pa_ref[0, rows, 4 * A_W:5 * A_W]
        o_ref[0, rows, :] = tot * lax.rsqrt(ms + NORM_EPS) * gain_ref[...] * _silu(g)
        return carry

    lax.fori_loop(0, n_all * c_sz // ROW_TILE, readout, 0)


def _hgrn(pa3, lb, gain, n_ctx_rows):
    n_batch, rows, _ = pa3.shape
    kern = functools.partial(_hgrn_kernel, n_ctx=n_ctx_rows // CHUNK, n_all=rows // CHUNK)
    return pl.pallas_call(
        kern,
        out_shape=jax.ShapeDtypeStruct((n_batch, rows, A_W), F32),
        grid=(n_batch,),
        in_specs=[pl.BlockSpec((1, rows, PA_W), lambda b: (b, 0, 0)),
                  pl.BlockSpec((2, A_W), lambda b: (0, 0)),
                  pl.BlockSpec((1, A_W), lambda b: (0, 0))],
        out_specs=pl.BlockSpec((1, rows, A_W), lambda b: (b, 0, 0)),
        scratch_shapes=[pltpu.VMEM((2, A_W, A_W), F32), pltpu.VMEM((2, CHUNK, A_W), F32),
                        pltpu.VMEM((rows, A_W), F32)],
        compiler_params=_cparams(("arbitrary",)),
        name="hgrn2",
    )(pa3, lb, gain)


def _mlstm_kernel(pb_ref, bias_ref, gain_ref, o_ref, c_ref, ob_ref, *, n_ctx, n_all):
    c_sz = CHUNK
    r_i = lax.broadcasted_iota(jnp.int32, (c_sz, c_sz), 0)
    c_i = lax.broadcasted_iota(jnp.int32, (c_sz, c_sz), 1)
    past = c_i <= r_i
    future = c_i >= r_i
    eye = (c_i == r_i).astype(F32)
    lane = lax.broadcasted_iota(jnp.int32, (1, LANES), 1)
    ones_col = (lax.broadcasted_iota(jnp.int32, (c_sz, HEAD_V), 1) == 0).astype(F32)
    q_off, k_off, v_off, o_off, g_off = 0, HEADS * B_QK, 2 * HEADS * B_QK, 2 * HEADS * B_QK + B_W, 2 * HEADS * B_QK + 2 * B_W

    def chunk(c, rev, mvec):
        d = 1 if rev else 0
        r0 = pl.multiple_of(c * c_sz, c_sz)
        rows = pl.ds(r0, c_sz)
        q = pb_ref[0, rows, q_off:q_off + HEADS * B_QK] * (B_QK ** -0.5)
        k = pb_ref[0, rows, k_off:k_off + HEADS * B_QK]
        v = pb_ref[0, rows, v_off:v_off + B_W]
        gts = pb_ref[0, rows, g_off:g_off + LANES] + bias_ref[...]
        log_f = jnp.minimum(gts, 0.0) - jnp.log(1.0 + jnp.exp(-jnp.abs(gts)))
        mask = future if rev else past
        tri = mask.astype(F32)
        tri_t = (past if rev else future).astype(F32)
        cum_col = jnp.dot(tri, log_f, preferred_element_type=F32, precision=HI)
        cum_row = _tn(log_f, tri_t, precision=HI)
        i_row = _tn(gts, eye, precision=HI)
        e = 0 if rev else c_sz - 1
        qb = q.astype(BF16)
        kb = k.astype(BF16)
        outs = []
        for h in range(HEADS):
            il = (2 * HEADS if rev else 0) + h
            fl = il + HEADS
            cc = cum_col[:, fl:fl + 1]
            cr = cum_row[fl:fl + 1, :]
            ir = i_row[il:il + 1, :]
            ic = gts[:, il:il + 1]
            m_prev = mvec[:, h:h + 1]
            log_d = jnp.where(mask, cc - cr + ir, MASK_NEG)
            log_inter = cc + m_prev
            m_t = jnp.maximum(log_inter, jnp.max(log_d, axis=-1, keepdims=True))
            w_intra = jnp.where(mask, jnp.exp(log_d - m_t), 0.0)
            w_inter = jnp.exp(log_inter - m_t)
            qh = qb[:, h * B_QK:(h + 1) * B_QK]
            kh = kb[:, h * B_QK:(h + 1) * B_QK]
            v_aug = jnp.concatenate([v[:, h * HEAD_V:(h + 1) * HEAD_V], ones_col], axis=1).astype(BF16)
            scores = _nt(qh, kh) * w_intra
            cm = c_ref[d, h]
            nd =(w_inter * jnp.dot(qh, cm.astype(BF16), preferred_element_type=F32)
                  + jnp.dot(scores.astype(BF16), v_aug, preferred_element_type=F32))
            den = nd[:, HEAD_V:HEAD_V + 1]
            outs.append(nd[:, 0:HEAD_V] / jnp.maximum(jnp.abs(den), jnp.exp(-m_t)))
            ce = cc[e:e + 1, :]
            log_end = ce - cc + ic
            m_end = jnp.maximum(ce + m_prev, jnp.max(log_end, axis=0, keepdims=True))
            w_end = jnp.exp(log_end - m_end)
            w_carry = jnp.exp(ce + m_prev - m_end)
            k_w = (k[:, h * B_QK:(h + 1) * B_QK] * w_end).astype(BF16)
            c_ref[d, h] = w_carry * cm + _tn(k_w, v_aug)
            mvec = jnp.where(lane == h, m_end, mvec)
        hcat = jnp.concatenate(outs, axis=1)
        if rev:
            ob_ref[rows, :] = hcat
        else:
            o_ref[0, rows, :] = hcat
        return mvec

    def scan_step(n, carry):
        m_fwd, m_bwd = carry
        m_fwd = chunk(n, False, m_fwd)
        m_bwd = chunk(jnp.where(n < n_ctx, n_ctx - 1 - n, n_all - 1 - (n - n_ctx)), True, m_bwd)
        return m_fwd, m_bwd

    zero_m = jnp.zeros((1, LANES), F32)
    c_ref[...] = jnp.zeros_like(c_ref)
    lax.fori_loop(0, n_all, scan_step, (zero_m, zero_m), unroll=2)

    def readout(c, carry):
        rows = pl.ds(pl.multiple_of(c * ROW_TILE, ROW_TILE), ROW_TILE)
        tot = o_ref[0, rows, :] + ob_ref[rows, :]
        normed = jnp.concatenate(
            [_rms(tot[:, h * HEAD_V:(h + 1) * HEAD_V]) for h in range(HEADS)], axis=1) * gain_ref[...]
        og = pb_ref[0, rows, o_off:o_off + B_W]
        o_ref[0, rows, :] = normed * jax.nn.sigmoid(og)
        return carry

    lax.fori_loop(0, n_all * c_sz // ROW_TILE, readout, 0)


def _mlstm(pb3, bias, gain, n_ctx_rows):
    n_batch, rows, _ = pb3.shape
    kern = functools.partial(_mlstm_kernel, n_ctx=n_ctx_rows // CHUNK, n_all=rows // CHUNK)
    return pl.pallas_call(
        kern,
        out_shape=jax.ShapeDtypeStruct((n_batch, rows, B_W), F32),
        grid=(n_batch,),
        in_specs=[pl.BlockSpec((1, rows, PB_W), lambda b: (b, 0, 0)),
                  pl.BlockSpec((1, LANES), lambda b: (0, 0)),
                  pl.BlockSpec((1, B_W), lambda b: (0, 0))],
        out_specs=pl.BlockSpec((1, rows, B_W), lambda b: (b, 0, 0)),
        scratch_shapes=[pltpu.VMEM((2, HEADS, B_QK, 2 * HEAD_V), F32), pltpu.VMEM((rows, B_W), F32)],
        compiler_params=_cparams(("arbitrary",)),
        name="mlstm",
    )(pb3, bias, gain)


def _attn_kernel(lam_ref, qp_ref, qr_ref, k_ref, v_ref, gain_ref, o_ref, *, n_ctx, q_tile0, lam_init):
    lam = lam_ref[0]
    q_tile = pl.program_id(2) + q_tile0
    lane = lax.broadcasted_iota(jnp.int32, (1, 2 * C_DQK), 1)
    scale = C_DQK ** -0.5

    def finish(parts):
        o = parts[0] - lam * parts[1]
        o_ref[0] = _rms(o) * gain_ref[...] * (1.0 - lam_init)

    def sub_query(q, j):
        return jnp.where(lane // C_DQK == j, q * scale, jnp.zeros_like(q))

    def row_max(s):
        return jnp.max(s, axis=-1, keepdims=True)

    def row_sum(s):
        return jnp.sum(s, axis=-1, keepdims=True)

    def pv(ex, v):
        return jnp.dot(ex.astype(BF16), v, preferred_element_type=F32)

    @pl.when(q_tile * ROW_TILE < n_ctx)
    def _():
        kc = k_ref[0, 0:n_ctx, :]
        vc = v_ref[0, 0:n_ctx, :]
        parts = []
        for j in range(2):
            s = _nt(sub_query(qp_ref[0], j), kc)
            ex = jnp.exp(s - row_max(s))
            parts.append(pv(ex, vc) / row_sum(ex))
        finish(parts)

    @pl.when(q_tile * ROW_TILE >= n_ctx)
    def _():
        n_all = k_ref.shape[1]
        kc = k_ref[0, 0:n_ctx, :]
        kl = k_ref[0, n_ctx:n_all, :]
        vc = v_ref[0, 0:n_ctx, :]
        vl = v_ref[0, n_ctx:n_all, :]
        parts = []
        for j in range(2):
            s_c = _nt(sub_query(qp_ref[0], j), kc)
            s_l = _nt(sub_query(qr_ref[0], j), kl)
            m = jnp.maximum(row_max(s_c), row_max(s_l))
            e_c = jnp.exp(s_c - m)
            e_l = jnp.exp(s_l - m)
            parts.append((pv(e_c, vc) + pv(e_l, vl)) / (row_sum(e_c) + row_sum(e_l)))
        finish(parts)


def _attention(pc3, lam, gain, n_ctx_rows, lam_init, with_ctx):
    n_batch, rows, _ = pc3.shape
    q_tile0 = 0 if with_ctx else n_ctx_rows // ROW_TILE
    n_q = rows // ROW_TILE - q_tile0
    hb = C_W // LANES
    kern = functools.partial(_attn_kernel, n_ctx=n_ctx_rows, q_tile0=q_tile0, lam_init=lam_init)
    grid_spec = pltpu.PrefetchScalarGridSpec(
        num_scalar_prefetch=1,
        grid=(n_batch, HEADS, n_q),
        in_specs=[pl.BlockSpec((1, ROW_TILE, LANES), lambda b, h, i, lam: (b, i + q_tile0, h)),
                  pl.BlockSpec((1, ROW_TILE, LANES), lambda b, h, i, lam: (b, i + q_tile0, hb + h)),
                  pl.BlockSpec((1, rows, LANES), lambda b, h, i, lam: (b, 0, 2 * hb + h)),
                  pl.BlockSpec((1, rows, LANES), lambda b, h, i, lam: (b, 0, 3 * hb + h)),
                  pl.BlockSpec((1, LANES), lambda b, h, i, lam: (0, 0))],
        out_specs=pl.BlockSpec((1, ROW_TILE, LANES), lambda b, h, i, lam: (b, i, h)),
    )
    return pl.pallas_call(
        kern,
        out_shape=jax.ShapeDtypeStruct((n_batch, n_q * ROW_TILE, C_W), F32),
        grid_spec=grid_spec,
        compiler_params=_cparams(("arbitrary", "arbitrary", "arbitrary")),
        name="diff_attn",
    )(lam, pc3, pc3, pc3, pc3, gain)


def _outproj_kernel(oa_ref, ob_ref, oc_ref, x_ref, mod_ref, n_ref, w_ref, rw_ref, rb_ref,
                    xn_ref, h_ref, route_ref, cnt_ref, carry_ref):
    d = x_ref.shape[1]
    n_exp = rw_ref.shape[1]

    @pl.when(pl.program_id(0) == 0)
    def _():
        carry_ref[...] = jnp.zeros_like(carry_ref)

    mix = (jnp.dot(oa_ref[...].astype(BF16), w_ref[0:A_W, :], preferred_element_type=F32)
           + jnp.dot(ob_ref[...].astype(BF16), w_ref[A_W:A_W + B_W, :], preferred_element_type=F32)
           + jnp.dot(oc_ref[...].astype(BF16), w_ref[A_W + B_W:A_W + B_W + C_W, :], preferred_element_type=F32))
    xn = x_ref[...] + mod_ref[0, :, 2 * d:3 * d] * (_rms(mix) * n_ref[1:2, :])
    xn_ref[...] = xn
    h = _rms(xn) * n_ref[2:3, :] * (1.0 + mod_ref[0, :, 4 * d:5 * d]) + mod_ref[0, :, 3 * d:4 * d]
    h_ref[...] = h.astype(BF16)

    logits = jnp.dot(h, rw_ref[...], preferred_element_type=F32, precision=HI) + rb_ref[...]
    e_lane = lax.broadcasted_iota(jnp.int32, logits.shape, 1)
    cur = logits
    picks, vals = [], []
    for _ in range(TOP_K):
        mx = jnp.max(cur, axis=-1, keepdims=True)
        idx = jnp.min(jnp.where(cur == mx, e_lane, n_exp), axis=-1, keepdims=True)
        hit = e_lane == idx
        cur = jnp.where(hit, -jnp.inf, cur)
        picks.append((idx, hit.astype(F32)))
        vals.append(mx)
    exps = [jnp.exp(vv - vals[0]) for vv in vals]
    total = exps[0] + exps[1] + exps[2] + exps[3]

    chosen = picks[0][1] + picks[1][1] + picks[2][1] + picks[3][1]
    tm = logits.shape[0]
    before = (lax.broadcasted_iota(jnp.int32, (tm, tm), 1) < lax.broadcasted_iota(jnp.int32, (tm, tm), 0))
    seen = jnp.dot(before.astype(BF16), chosen.astype(BF16), preferred_element_type=F32) + carry_ref[...]
    new_carry = carry_ref[...] + jnp.sum(chosen, axis=0, keepdims=True)
    carry_ref[...] = new_carry
    cnt_ref[...] = new_carry

    o_lane = lax.broadcasted_iota(jnp.int32, (tm, LANES), 1)
    route = jnp.zeros((tm, LANES), F32)
    for kk in range(TOP_K):
        idx, hit = picks[kk]
        rank = jnp.sum(hit * seen, axis=-1, keepdims=True)
        route = (route + jnp.where(o_lane == kk, idx.astype(F32), 0.0)
                 + jnp.where(o_lane == TOP_K + kk, exps[kk] / total, 0.0)
                 + jnp.where(o_lane == 2 * TOP_K + kk, rank, 0.0))
    route_ref[...] = route


def _outproj(oa, ob, oc, xy, mod, norms, w_out_b, router_w, router_b, layer, n_tiles, in_tile, oc_tile, mod_row):
    d = xy.shape[1]
    n_exp = router_w.shape[2]
    t_out = n_tiles * ROW_TILE
    return pl.pallas_call(
        _outproj_kernel,
        out_shape=(jax.ShapeDtypeStruct((t_out, d), F32),
                   jax.ShapeDtypeStruct((t_out, d), BF16),
                   jax.ShapeDtypeStruct((t_out, LANES), F32),
                   jax.ShapeDtypeStruct((1, n_exp), F32)),
        grid=(n_tiles,),
        in_specs=[pl.BlockSpec((ROW_TILE, A_W), lambda i: (in_tile(i), 0)),
                  pl.BlockSpec((ROW_TILE, B_W), lambda i: (in_tile(i), 0)),
                  pl.BlockSpec((ROW_TILE, C_W), lambda i: (oc_tile(i), 0)),
                  pl.BlockSpec((ROW_TILE, d), lambda i: (in_tile(i), 0)),
                  pl.BlockSpec((1, 1, 6 * d), lambda i: (mod_row(i), 0, 0)),
                  pl.BlockSpec((None, 4, d), lambda i: (layer, 0, 0)),
                  pl.BlockSpec((None, d, d), lambda i: (layer, 0, 0)),
                  pl.BlockSpec((None, d, n_exp), lambda i: (layer, 0, 0)),
                  pl.BlockSpec((None, 1, n_exp), lambda i: (layer, 0, 0))],
        out_specs=(pl.BlockSpec((ROW_TILE, d), lambda i: (i, 0)),
                   pl.BlockSpec((ROW_TILE, d), lambda i: (i, 0)),
                   pl.BlockSpec((ROW_TILE, LANES), lambda i: (i, 0)),
                   pl.BlockSpec((1, n_exp), lambda i: (0, 0))),
        scratch_shapes=[pltpu.VMEM((1, n_exp), F32)],
        compiler_params=_cparams(("arbitrary",)),
        name="outproj_router",
    )(oa, ob, oc, xy, mod, norms, w_out_b, router_w, router_b)


PAIR_BLOCK = 2 * LANES


def _moe_kernel(te_ref, first_ref, nu_ref, *refs, n_parts):
    xs_refs = refs[:n_parts]
    w1_ref, b1_ref, w2_ref, b2_ref, ys_ref, w1p_ref, w2b_ref, x_ref = refs[n_parts:]
    i = pl.program_id(0)
    two_f = w1_ref.shape[1]
    n_blk = two_f // PAIR_BLOCK
    part_tiles = GATHER_ROWS // MOE_TILE

    @pl.when(jnp.logical_and(i < nu_ref[0], first_ref[i] == 1))
    def _():
        r = lax.broadcasted_iota(jnp.int32, (PAIR_BLOCK, PAIR_BLOCK), 0)
        c = lax.broadcasted_iota(jnp.int32, (PAIR_BLOCK, PAIR_BLOCK), 1)
        perm = (r == jnp.where(c < LANES, 2 * c, 2 * (c - LANES) + 1)).astype(BF16)
        for blk in range(n_blk):
            cols = slice(blk * PAIR_BLOCK, (blk + 1) * PAIR_BLOCK)
            w1p_ref[:, cols] = jnp.dot(w1_ref[:, cols].astype(BF16), perm,
                                       preferred_element_type=F32).astype(BF16)
        w2b_ref[...] = w2_ref[...].astype(BF16)

    for part in range(n_parts):
        @pl.when(jnp.logical_and(i < nu_ref[0], i // part_tiles == part))
        def _():
            x_ref[...] = xs_refs[part][...]

    @pl.when(i < nu_ref[0])
    def _():
        hid = jnp.dot(x_ref[...], w1p_ref[...], preferred_element_type=F32) + b1_ref[...]
        acts = []
        for blk in range(n_blk):
            glu = jnp.minimum(hid[:, blk * PAIR_BLOCK:blk * PAIR_BLOCK + LANES], SWIGLU_LIMIT)
            lin = jnp.clip(hid[:, blk * PAIR_BLOCK + LANES:(blk + 1) * PAIR_BLOCK], -SWIGLU_LIMIT, SWIGLU_LIMIT)
            acts.append((glu * jax.nn.sigmoid(SWIGLU_ALPHA * glu) * (lin + 1.0)).astype(BF16))
        y = jnp.dot(jnp.concatenate(acts, axis=1), w2b_ref[...], preferred_element_type=F32) + b2_ref[...]
        ys_ref[...] = y.astype(ys_ref.dtype)

    @pl.when(i >= nu_ref[0])
    def _():
        ys_ref[...] = jnp.zeros_like(ys_ref)


def _moe(tile_expert, tile_first, n_used, xs_parts, w1, b1p, w2, b2, layer):
    n_parts = len(xs_parts)
    d = xs_parts[0].shape[1]
    r_max = n_parts * GATHER_ROWS
    two_f = w1.shape[3]
    f = two_f // 2
    n_tiles = r_max // MOE_TILE
    part_tiles = GATHER_ROWS // MOE_TILE

    def part_spec(part):
        def index(i, te, fi, nu):
            used = jnp.maximum(jnp.minimum(i, nu[0] - 1), 0)
            return (jnp.clip(used - part * part_tiles, 0, part_tiles - 1), 0)
        return pl.BlockSpec((MOE_TILE, d), index)

    grid_spec = pltpu.PrefetchScalarGridSpec(
        num_scalar_prefetch=3,
        grid=(n_tiles,),
        in_specs=[part_spec(part) for part in range(n_parts)] + [
                  pl.BlockSpec((None, None, d, two_f), lambda i, te, fi, nu: (layer, te[i], 0, 0)),
                  pl.BlockSpec((None, None, 1, two_f), lambda i, te, fi, nu: (layer, te[i], 0, 0)),
                  pl.BlockSpec((None, None, f, d), lambda i, te, fi, nu: (layer, te[i], 0, 0)),
                  pl.BlockSpec((None, None, 1, d), lambda i, te, fi, nu: (layer, te[i], 0, 0))],
        out_specs=pl.BlockSpec((MOE_TILE, d), lambda i, te, fi, nu: (i, 0)),
        scratch_shapes=[pltpu.VMEM((d, two_f), BF16), pltpu.VMEM((f, d), BF16), pltpu.VMEM((MOE_TILE, d), BF16)],
    )
    return pl.pallas_call(
        functools.partial(_moe_kernel, n_parts=n_parts),
        out_shape=jax.ShapeDtypeStruct((r_max, d), BF16),
        grid_spec=grid_spec,
        compiler_params=_cparams(("arbitrary",)),
        name="moe_experts",
    )(tile_expert, tile_first, n_used, *xs_parts, w1, b1p, w2, b2)


def _ffn_residual_kernel(x_ref, f_ref, mod_ref, n_ref, o_ref):
    d = x_ref.shape[1]
    o_ref[...] = x_ref[...] + mod_ref[0, :, 5 * d:6 * d] * (_rms(f_ref[...]) * n_ref[3:4, :])


def _ffn_residual(xn, ffn, mod, norms, layer, mod_row):
    t, d = xn.shape
    return pl.pallas_call(
        _ffn_residual_kernel,
        out_shape=jax.ShapeDtypeStruct((t, d), F32),
        grid=(t // ROW_TILE,),
        in_specs=[pl.BlockSpec((ROW_TILE, d), lambda i: (i, 0)),
                  pl.BlockSpec((ROW_TILE, d), lambda i: (i, 0)),
                  pl.BlockSpec((1, 1, 6 * d), lambda i: (mod_row(i), 0, 0)),
                  pl.BlockSpec((None, 4, d), lambda i: (layer, 0, 0))],
        out_specs=pl.BlockSpec((ROW_TILE, d), lambda i: (i, 0)),
        compiler_params=_cparams(("arbitrary",)),
        name="ffn_residual",
    )(xn, ffn, mod, norms)


def _dispatch_plan(route, counts, r_max):
    t = route.shape[0]
    idx = route[:, 0:TOP_K].astype(jnp.int32)
    wts = route[:, TOP_K:2 * TOP_K]
    rank = route[:, 2 * TOP_K:3 * TOP_K].astype(jnp.int32)
    cnt = counts[0].astype(jnp.int32)
    padded = (cnt + MOE_TILE - 1) // MOE_TILE * MOE_TILE
    ends = jnp.cumsum(padded)
    pos = (ends - padded)[idx] + rank
    n_tiles = r_max // MOE_TILE
    n_used = ends[-1] // MOE_TILE
    tile_ids = jnp.minimum(jnp.arange(n_tiles, dtype=jnp.int32), n_used - 1)
    tile_expert = jnp.sum((ends // MOE_TILE)[None, :] <= tile_ids[:, None], axis=1).astype(jnp.int32)
    tile_first = jnp.concatenate(
        [jnp.ones((1,), jnp.int32), (tile_expert[1:] != tile_expert[:-1]).astype(jnp.int32)])
    flat = pos.reshape(-1)
    src = jnp.zeros((r_max,), jnp.int32).at[flat].set(jnp.repeat(jnp.arange(t, dtype=jnp.int32), TOP_K))
    return pos, wts, src, tile_expert, tile_first, n_used.reshape(1).astype(jnp.int32)


def _rope_tables(length, n_ctx_rows):
    rows = length // GRID_W
    row = jnp.repeat(jnp.arange(rows, dtype=F32), GRID_W)
    col = jnp.tile(jnp.arange(GRID_W, dtype=F32), rows)
    n_freq = C_DQK // 4
    inv_freq = ROPE_THETA ** (-jnp.arange(n_freq, dtype=F32) / n_freq)
    ang_r = row[:, None] * inv_freq
    ang_c = col[:, None] * inv_freq
    cos = jnp.concatenate([jnp.cos(ang_r), jnp.cos(ang_r), jnp.cos(ang_c), jnp.cos(ang_c)], axis=-1)
    sin = jnp.concatenate([-jnp.sin(ang_r), jnp.sin(ang_r), -jnp.sin(ang_c), jnp.sin(ang_c)], axis=-1)
    reps = C_W // C_DQK
    cos = jnp.concatenate([jnp.ones((n_ctx_rows, C_W), F32), jnp.tile(cos, (1, reps))], axis=0)
    sin = jnp.concatenate([jnp.zeros((n_ctx_rows, C_W), F32), jnp.tile(sin, (1, reps))], axis=0)
    return cos, sin


def _hgrn_lower_bound(table, layer):
    p = jax.nn.softmax(table.astype(F32), axis=1)
    cum = jnp.cumsum(p, axis=1) - p[:, :1]
    return jnp.clip(cum[:, layer], 0.0, 1.0)


def kernel(x, c, ctx, c_ctx, ada_w, ada_b, sandwich_norms, w_in, w_out, hgrn_lower_bounds, hgrn_norm,
           mlstm_gate_bias, mlstm_norm, diff_lambdas, diff_norm, router_w, router_b, moe_w1, moe_b1,
           moe_w2, moe_b2):
    n_batch, seq, d = x.shape
    n_ctx_rows = ctx.shape[1]
    depth = w_in.shape[0]
    n_exp = router_w.shape[2]
    assert seq % ROW_TILE == 0 and n_ctx_rows % ROW_TILE == 0 and seq % GRID_W == 0
    rows_b = n_ctx_rows + seq
    tiles_b = rows_b // ROW_TILE
    ctx_tiles = n_ctx_rows // ROW_TILE
    lat_tiles = seq // ROW_TILE

    b_main = PA_W + 2 * HEADS * B_QK + 2 * B_W
    w_in_p = jnp.concatenate(
        [w_in[:, :, :b_main],
         jnp.pad(w_in[:, :, b_main:b_main + N_GATE], ((0, 0), (0, 0), (0, LANES - N_GATE))),
         w_in[:, :, b_main + N_GATE:]], axis=2).astype(BF16)
    w_out_b = w_out.astype(BF16)
    two_f = moe_b1.shape[2]
    b1p = moe_b1.reshape(depth, n_exp, two_f // PAIR_BLOCK, LANES, 2).transpose(0, 1, 2, 4, 3).reshape(
        depth, n_exp, 1, two_f)
    b2 = moe_b2[:, :, None, :]
    ada_b3 = ada_b[:, None, :]
    router_b3 = router_b[:, None, :]
    gate_bias = jnp.pad(mlstm_gate_bias, ((0, 0), (0, LANES - N_GATE)))
    cos, sin = _rope_tables(seq, n_ctx_rows)

    cond_rows = (n_batch + 1 + 7) // 8 * 8
    cvec = jnp.zeros((cond_rows, d), F32).at[:n_batch].set(c).at[n_batch].set(c_ctx)

    def mod_row_all(i):
        return jnp.where(i % tiles_b < ctx_tiles, n_batch, i // tiles_b)

    xy = jnp.concatenate([ctx, x], axis=1).reshape(n_batch * rows_b, d)

    for layer in range(depth):
        last = layer == depth - 1
        mod = _adaln(cvec, ada_w, ada_b3, layer).reshape(cond_rows, 1, 6 * d)
        lb = _hgrn_lower_bound(hgrn_lower_bounds, layer)
        lam_init = 0.8 - 0.6 * math.exp(-0.3 * layer)
        lq1, lk1, lq2, lk2 = diff_lambdas[layer].astype(F32)
        lam = (jnp.exp(jnp.sum(lq1 * lk1)) - jnp.exp(jnp.sum(lq2 * lk2)) + lam_init).reshape(1)

        pa, pb, pc = _inproj(xy, mod, sandwich_norms, w_in_p, cos, sin, layer, tiles_b, ctx_tiles, n_batch)
        oa = _hgrn(pa.reshape(n_batch, rows_b, PA_W), lb, jnp.tile(hgrn_norm[layer], HEADS)[None, :], n_ctx_rows)
        ob = _mlstm(pb.reshape(n_batch, rows_b, PB_W), gate_bias[layer][None, :], mlstm_norm[layer][None, :],
                    n_ctx_rows)
        oc = _attention(pc.reshape(n_batch, rows_b, PC_OUT), lam, diff_norm[layer][None, :], n_ctx_rows,
                        lam_init, not last)

        if last:
            n_tiles = n_batch * lat_tiles
            in_tile = lambda i: (i // lat_tiles) * tiles_b + ctx_tiles + i % lat_tiles
            mod_row = lambda i: i // lat_tiles
        else:
            n_tiles = n_batch * tiles_b
            in_tile = lambda i: i
            mod_row = mod_row_all
        xn, h, route, counts = _outproj(
            oa.reshape(-1, A_W), ob.reshape(-1, B_W), oc.reshape(-1, C_W), xy, mod, sandwich_norms, w_out_b,
            router_w, router_b3, layer, n_tiles, in_tile, lambda i: i, mod_row)

        r_rows = -(-(n_tiles * ROW_TILE * TOP_K + n_exp * MOE_TILE) // GATHER_ROWS) * GATHER_ROWS
        pos, wts, src, tile_expert, tile_first, n_used = _dispatch_plan(route, counts, r_rows)
        xs_parts = [jnp.take(h, src[j * GATHER_ROWS:(j + 1) * GATHER_ROWS], axis=0)
                    for j in range(r_rows // GATHER_ROWS)]
        ys = _moe(tile_expert, tile_first, n_used, xs_parts, moe_w1, b1p, moe_w2, b2, layer)
        ffn = sum(wts[:, k:k + 1] * jnp.take(ys, pos[:, k], axis=0).astype(F32) for k in range(TOP_K))
        xy = _ffn_residual(xn, ffn, mod, sandwich_norms, layer, mod_row)

    return xy.reshape(n_batch, seq, d)
```

```python
import functools
import math

import jax
import jax.numpy as jnp
from jax import lax
from jax.experimental import pallas as pl
from jax.experimental.pallas import tpu as pltpu
from jax.experimental.pallas import tpu_sc as plsc

F32 = jnp.float32
BF16 = jnp.bfloat16
HI = lax.Precision.HIGHEST

HEADS = 4
A_W = 256
B_QK = 32
B_W = 256
C_DQK = 64
C_W = 512
HEAD_V = 64
N_GATE = 16
GRID_W = 64
TOP_K = 4
SWIGLU_ALPHA = 1.702
SWIGLU_LIMIT = 7.0
ROPE_THETA = 10000.0
NORM_EPS = 1e-6
MASK_NEG = -1e30
F_MIN = 1e-12

LANES = 128
ROW_TILE = 256
CHUNK = 64
SUB = 16
MOE_TILE = 256
VMEM_LIMIT = 56 * 1024 * 1024

PA_W = 5 * A_W
PB_W = 2 * HEADS * B_QK + 2 * B_W + LANES
PC_IN = 3 * C_W
PC_OUT = 4 * C_W
W_IN_PAD = PA_W + PB_W + PC_IN


def _cparams(sem):
    return pltpu.CompilerParams(dimension_semantics=sem, vmem_limit_bytes=VMEM_LIMIT)


def _nt(a, b):
    return lax.dot_general(a, b, (((1,), (1,)), ((), ())), preferred_element_type=F32)


def _tn(a, b, precision=None):
    return lax.dot_general(a, b, (((0,), (0,)), ((), ())), preferred_element_type=F32, precision=precision)


def _rms(x):
    return x * lax.rsqrt(jnp.mean(x * x, axis=-1, keepdims=True) + NORM_EPS)


def _silu(x):
    return x * jax.nn.sigmoid(x)


def _adaln_kernel(c_ref, w_ref, b_ref, o_ref):
    cond = _silu(c_ref[...])
    o_ref[...] = jnp.dot(cond, w_ref[...], preferred_element_type=F32, precision=HI) + b_ref[...]


def _adaln(cvec, ada_w, ada_b, layer):
    rows, d = cvec.shape
    return pl.pallas_call(
        _adaln_kernel,
        out_shape=jax.ShapeDtypeStruct((rows, 6 * d), F32),
        grid=(6,),
        in_specs=[pl.BlockSpec((rows, d), lambda j: (0, 0)),
                  pl.BlockSpec((None, d, d), lambda j: (layer, 0, j)),
                  pl.BlockSpec((None, 1, d), lambda j: (layer, 0, j))],
        out_specs=pl.BlockSpec((rows, d), lambda j: (0, j)),
        compiler_params=_cparams(("arbitrary",)),
        name="adaln",
    )(cvec, ada_w, ada_b)


def _inproj_kernel(x_ref, mod_ref, g_ref, w_ref, cos_ref, sin_ref, pa_ref, pb_ref, pc_ref):
    d = x_ref.shape[1]
    shift = mod_ref[0, :, 0:d]
    scale = mod_ref[0, :, d:2 * d]
    h = _rms(x_ref[...]) * g_ref[0:1, :] * (1.0 + scale) + shift
    hb = h.astype(BF16)
    pa_ref[...] = jnp.dot(hb, w_ref[:, 0:PA_W], preferred_element_type=F32)
    pb_ref[...] = jnp.dot(hb, w_ref[:, PA_W:PA_W + PB_W], preferred_element_type=F32)
    pc = jnp.dot(hb, w_ref[:, PA_W + PB_W:W_IN_PAD], preferred_element_type=F32)
    q = pc[:, 0:C_W]
    k = pc[:, C_W:2 * C_W]
    cos = cos_ref[...]
    sin = sin_ref[...]
    lane = lax.broadcasted_iota(jnp.int32, q.shape, 1)
    first = (lane % 32) < 16

    def rope(t):
        partner = jnp.where(first, pltpu.roll(t, C_W - 16, 1), pltpu.roll(t, 16, 1))
        return t * cos + partner * sin

    pc_ref[:, 0:C_W] = q.astype(BF16)
    pc_ref[:, C_W:2 * C_W] = rope(q).astype(BF16)
    pc_ref[:, 2 * C_W:3 * C_W] = rope(k).astype(BF16)
    pc_ref[:, 3 * C_W:4 * C_W] = pc[:, 2 * C_W:3 * C_W].astype(BF16)


def _inproj(xy, mod, norms, w_in_p, cos, sin, layer, tiles_per_batch, n_ctx_tiles, n_batch):
    t_all, d = xy.shape
    n_tiles = t_all // ROW_TILE

    def mod_row(i):
        return jnp.where(i % tiles_per_batch < n_ctx_tiles, n_batch, i // tiles_per_batch)

    return pl.pallas_call(
        _inproj_kernel,
        out_shape=(jax.ShapeDtypeStruct((t_all, PA_W), F32),
                   jax.ShapeDtypeStruct((t_all, PB_W), F32),
                   jax.ShapeDtypeStruct((t_all, PC_OUT), BF16)),
        grid=(n_tiles,),
        in_specs=[pl.BlockSpec((ROW_TILE, d), lambda i: (i, 0)),
                  pl.BlockSpec((1, 1, 6 * d), lambda i: (mod_row(i), 0, 0)),
                  pl.BlockSpec((None, 4, d), lambda i: (layer, 0, 0)),
                  pl.BlockSpec((None, d, W_IN_PAD), lambda i: (layer, 0, 0)),
                  pl.BlockSpec((ROW_TILE, C_W), lambda i: (i % tiles_per_batch, 0)),
                  pl.BlockSpec((ROW_TILE, C_W), lambda i: (i % tiles_per_batch, 0))],
        out_specs=(pl.BlockSpec((ROW_TILE, PA_W), lambda i: (i, 0)),
                   pl.BlockSpec((ROW_TILE, PB_W), lambda i: (i, 0)),
                   pl.BlockSpec((ROW_TILE, PC_OUT), lambda i: (i, 0))),
        compiler_params=_cparams(("arbitrary",)),
        name="inproj",
    )(xy, mod, norms, w_in_p, cos, sin)


def _hgrn_kernel(pa_ref, lb_ref, gain_ref, o_ref, st_ref, acc_ref, ob_ref, *, n_ctx, n_all):
    c_sz, w = CHUNK, A_W
    r_i = lax.broadcasted_iota(jnp.int32, (c_sz, c_sz), 0)
    c_i = lax.broadcasted_iota(jnp.int32, (c_sz, c_sz), 1)
    tri_incl_past = (c_i <= r_i).astype(F32)
    tri_incl_future = (c_i >= r_i).astype(F32)
    same_head = (lax.broadcasted_iota(jnp.int32, (w, w), 0) // HEAD_V
                 == lax.broadcasted_iota(jnp.int32, (w, w), 1) // HEAD_V)
    head_ones = same_head.astype(BF16)
    head_ones_f32 = same_head.astype(F32)
    sub_pos = lax.broadcasted_iota(jnp.int32, (c_sz, 1), 0) % SUB
    n_sub = c_sz // SUB

    def sub_rows(a, sp):
        return jnp.concatenate(
            [jnp.broadcast_to(a[SUB * i + sp:SUB * i + sp + 1, :], (SUB, w)) for i in range(n_sub)], axis=0)

    def chunk(c, rev):
        d = 1 if rev else 0
        r0 = pl.multiple_of(c * c_sz, c_sz)
        rows = pl.ds(r0, c_sz)
        q_pre = pa_ref[0, rows, 0:A_W]
        v = pa_ref[0, rows, A_W:2 * A_W]
        f_pre = pa_ref[0, rows, 3 * A_W:4 * A_W] if rev else pa_ref[0, rows, 2 * A_W:3 * A_W]
        lb = lb_ref[1:2, :] if rev else lb_ref[0:1, :]
        q = _silu(q_pre)
        f = lb + (1.0 - lb) * jax.nn.sigmoid(f_pre)
        log_f = jnp.log(jnp.maximum(f, F_MIN))
        kk = (1.0 - lb) * jax.nn.sigmoid(-f_pre)
        cum = jnp.dot(tri_incl_future if rev else tri_incl_past, log_f, preferred_element_type=F32, precision=HI)
        e = 0 if rev else c_sz - 1
        cum_end = cum[e:e + 1, :]
        vb = v.astype(BF16)

        st = st_ref[d]
        acc_ref[d] = _nt((q * jnp.exp(cum)).astype(BF16), st.astype(BF16))
        k_end = (kk * jnp.exp(cum_end - cum)).astype(BF16)
        st_ref[d] = st * jnp.exp(cum_end) + jnp.where(same_head, _tn(vb, k_end), 0.0)

        def off(t0, s0, n, ref_row):
            ref = cum[ref_row:ref_row + 1, :]
            qa = (q[t0:t0 + n] * jnp.exp(cum[t0:t0 + n] - ref)).astype(BF16)
            ka = (kk[s0:s0 + n] * jnp.exp(ref - cum[s0:s0 + n])).astype(BF16)
            u = jnp.where(same_head, _tn(vb[s0:s0 + n], ka), 0.0).astype(BF16)
            acc_ref[d, t0:t0 + n, :] += _nt(qa, u)

        if rev:
            off(0, 32, 32, 32)
            off(0, 16, 16, 16)
            off(32, 48, 16, 48)
        else:
            off(32, 0, 32, 31)
            off(16, 0, 16, 15)
            off(48, 32, 16, 47)

        fs = []
        for sp in range(SUB):
            live = (sub_pos <= sp) if rev else (sub_pos >= sp)
            arg = jnp.where(live, cum - sub_rows(cum, sp), MASK_NEG)
            fs.append((q * jnp.exp(arg) * sub_rows(kk, sp)).astype(BF16))
        r = jnp.dot(jnp.concatenate(fs, axis=0), head_ones, preferred_element_type=F32)
        o = acc_ref[d]
        for sp in range(SUB):
            o = o + r[sp * c_sz:(sp + 1) * c_sz, :] * sub_rows(v, sp)
        if rev:
            ob_ref[rows, :] = o
        else:
            o_ref[0, rows, :] = o

    def scan_step(n, carry):
        chunk(n, False)
        chunk(jnp.where(n < n_ctx, n_ctx - 1 - n, n_all - 1 - (n - n_ctx)), True)
        return carry

    st_ref[...] = jnp.zeros_like(st_ref)
    lax.fori_loop(0, n_all, scan_step, 0)

    def readout(c, carry):
        rows = pl.ds(pl.multiple_of(c * ROW_TILE, ROW_TILE), ROW_TILE)
        tot = o_ref[0, rows, :] + ob_ref[rows, :]
        ms = jnp.dot(tot * tot, head_ones_f32, preferred_element_type=F32, precision=HI) * (1.0 / HEAD_V)
        g = pa_ref[0, rows, 4 * A_W:5 * A_W]
        o_ref[0, rows, :] = tot * lax.rsqrt(ms + NORM_EPS) * gain_ref[...] * _silu(g)
        return carry

    lax.fori_loop(0, n_all * c_sz // ROW_TILE, readout, 0)


def _hgrn(pa3, lb, gain, n_ctx_rows):
    n_batch, rows, _ = pa3.shape
    kern = functools.partial(_hgrn_kernel, n_ctx=n_ctx_rows // CHUNK, n_all=rows // CHUNK)
    return pl.pallas_call(
        kern,
        out_shape=jax.ShapeDtypeStruct((n_batch, rows, A_W), F32),
        grid=(n_batch,),
        in_specs=[pl.BlockSpec((1, rows, PA_W), lambda b: (b, 0, 0)),
                  pl.BlockSpec((2, A_W), lambda b: (0, 0)),
                  pl.BlockSpec((1, A_W), lambda b: (0, 0))],
        out_specs=pl.BlockSpec((1, rows, A_W), lambda b: (b, 0, 0)),
        scratch_shapes=[pltpu.VMEM((2, A_W, A_W), F32), pltpu.VMEM((2, CHUNK, A_W), F32),
                        pltpu.VMEM((rows, A_W), F32)],
        compiler_params=_cparams(("arbitrary",)),
        name="hgrn2",
    )(pa3, lb, gain)


def _mlstm_kernel(pb_ref, bias_ref, gain_ref, o_ref, c_ref, ob_ref, *, n_ctx, n_all):
    c_sz = CHUNK
    r_i = lax.broadcasted_iota(jnp.int32, (c_sz, c_sz), 0)
    c_i = lax.broadcasted_iota(jnp.int32, (c_sz, c_sz), 1)
    past = c_i <= r_i
    future = c_i >= r_i
    eye = (c_i == r_i).astype(F32)
    lane = lax.broadcasted_iota(jnp.int32, (1, LANES), 1)
    ones_col = (lax.broadcasted_iota(jnp.int32, (c_sz, HEAD_V), 1) == 0).astype(F32)
    q_off, k_off, v_off, o_off, g_off = 0, HEADS * B_QK, 2 * HEADS * B_QK, 2 * HEADS * B_QK + B_W, 2 * HEADS * B_QK + 2 * B_W

    def chunk(c, rev, mvec):
        d = 1 if rev else 0
        r0 = pl.multiple_of(c * c_sz, c_sz)
        rows = pl.ds(r0, c_sz)
        q = pb_ref[0, rows, q_off:q_off + HEADS * B_QK] * (B_QK ** -0.5)
        k = pb_ref[0, rows, k_off:k_off + HEADS * B_QK]
        v = pb_ref[0, rows, v_off:v_off + B_W]
        gts = pb_ref[0, rows, g_off:g_off + LANES] + bias_ref[...]
        log_f = jnp.minimum(gts, 0.0) - jnp.log(1.0 + jnp.exp(-jnp.abs(gts)))
        mask = future if rev else past
        tri = mask.astype(F32)
        tri_t = (past if rev else future).astype(F32)
        cum_col = jnp.dot(tri, log_f, preferred_element_type=F32, precision=HI)
        cum_row = _tn(log_f, tri_t, precision=HI)
        i_row = _tn(gts, eye, precision=HI)
        e = 0 if rev else c_sz - 1
        qb = q.astype(BF16)
        kb = k.astype(BF16)
        outs = []
        for h in range(HEADS):
            il = (2 * HEADS if rev else 0) + h
            fl = il + HEADS
            cc = cum_col[:, fl:fl + 1]
            cr = cum_row[fl:fl + 1, :]
            ir = i_row[il:il + 1, :]
            ic = gts[:, il:il + 1]
            m_prev = mvec[:, h:h + 1]
            log_d = jnp.where(mask, cc - cr + ir, MASK_NEG)
            log_inter = cc + m_prev
            m_t = jnp.maximum(log_inter, jnp.max(log_d, axis=-1, keepdims=True))
            w_intra = jnp.where(mask, jnp.exp(log_d - m_t), 0.0)
            w_inter = jnp.exp(log_inter - m_t)
            qh = qb[:, h * B_QK:(h + 1) * B_QK]
            kh = kb[:, h * B_QK:(h + 1) * B_QK]
            v_aug = jnp.concatenate([v[:, h * HEAD_V:(h + 1) * HEAD_V], ones_col], axis=1).astype(BF16)
            scores = _nt(qh, kh) * w_intra
            cm = c_ref[d, h]
            nd =(w_inter * jnp.dot(qh, cm.astype(BF16), preferred_element_type=F32)
                  + jnp.dot(scores.astype(BF16), v_aug, preferred_element_type=F32))
            den = nd[:, HEAD_V:HEAD_V + 1]
            outs.append(nd[:, 0:HEAD_V] / jnp.maximum(jnp.abs(den), jnp.exp(-m_t)))
            ce = cc[e:e + 1, :]
            log_end = ce - cc + ic
            m_end = jnp.maximum(ce + m_prev, jnp.max(log_end, axis=0, keepdims=True))
            w_end = jnp.exp(log_end - m_end)
            w_carry = jnp.exp(ce + m_prev - m_end)
            k_w = (k[:, h * B_QK:(h + 1) * B_QK] * w_end).astype(BF16)
            c_ref[d, h] = w_carry * cm + _tn(k_w, v_aug)
            mvec = jnp.where(lane == h, m_end, mvec)
        hcat = jnp.concatenate(outs, axis=1)
        if rev:
            ob_ref[rows, :] = hcat
        else:
            o_ref[0, rows, :] = hcat
        return mvec

    def scan_step(n, carry):
        m_fwd, m_bwd = carry
        m_fwd = chunk(n, False, m_fwd)
        m_bwd = chunk(jnp.where(n < n_ctx, n_ctx - 1 - n, n_all - 1 - (n - n_ctx)), True, m_bwd)
        return m_fwd, m_bwd

    zero_m = jnp.zeros((1, LANES), F32)
    c_ref[...] = jnp.zeros_like(c_ref)
    lax.fori_loop(0, n_all, scan_step, (zero_m, zero_m), unroll=2)

    def readout(c, carry):
        rows = pl.ds(pl.multiple_of(c * ROW_TILE, ROW_TILE), ROW_TILE)
        tot = o_ref[0, rows, :] + ob_ref[rows, :]
        normed = jnp.concatenate(
            [_rms(tot[:, h * HEAD_V:(h + 1) * HEAD_V]) for h in range(HEADS)], axis=1) * gain_ref[...]
        og = pb_ref[0, rows, o_off:o_off + B_W]
        o_ref[0, rows, :] = normed * jax.nn.sigmoid(og)
        return carry

    lax.fori_loop(0, n_all * c_sz // ROW_TILE, readout, 0)


def _mlstm(pb3, bias, gain, n_ctx_rows):
    n_batch, rows, _ = pb3.shape
    kern = functools.partial(_mlstm_kernel, n_ctx=n_ctx_rows // CHUNK, n_all=rows // CHUNK)
    return pl.pallas_call(
        kern,
        out_shape=jax.ShapeDtypeStruct((n_batch, rows, B_W), F32),
        grid=(n_batch,),
        in_specs=[pl.BlockSpec((1, rows, PB_W), lambda b: (b, 0, 0)),
                  pl.BlockSpec((1, LANES), lambda b: (0, 0)),
                  pl.BlockSpec((1, B_W), lambda b: (0, 0))],
        out_specs=pl.BlockSpec((1, rows, B_W), lambda b: (b, 0, 0)),
        scratch_shapes=[pltpu.VMEM((2, HEADS, B_QK, 2 * HEAD_V), F32), pltpu.VMEM((rows, B_W), F32)],
        compiler_params=_cparams(("arbitrary",)),
        name="mlstm",
    )(pb3, bias, gain)


def _attn_kernel(lam_ref, qp_ref, qr_ref, k_ref, v_ref, gain_ref, o_ref, *, n_ctx, q_tile0, lam_init):
    lam = lam_ref[0]
    q_tile = pl.program_id(2) + q_tile0
    lane = lax.broadcasted_iota(jnp.int32, (1, 2 * C_DQK), 1)
    scale = C_DQK ** -0.5

    def finish(parts):
        o = parts[0] - lam * parts[1]
        o_ref[0] = _rms(o) * gain_ref[...] * (1.0 - lam_init)

    def sub_query(q, j):
        return jnp.where(lane // C_DQK == j, q * scale, jnp.zeros_like(q))

    def row_max(s):
        return jnp.max(s, axis=-1, keepdims=True)

    def row_sum(s):
        return jnp.sum(s, axis=-1, keepdims=True)

    def pv(ex, v):
        return jnp.dot(ex.astype(BF16), v, preferred_element_type=F32)

    @pl.when(q_tile * ROW_TILE < n_ctx)
    def _():
        kc = k_ref[0, 0:n_ctx, :]
        vc = v_ref[0, 0:n_ctx, :]
        parts = []
        for j in range(2):
            s = _nt(sub_query(qp_ref[0], j), kc)
            ex = jnp.exp(s - row_max(s))
            parts.append(pv(ex, vc) / row_sum(ex))
        finish(parts)

    @pl.when(q_tile * ROW_TILE >= n_ctx)
    def _():
        n_all = k_ref.shape[1]
        kc = k_ref[0, 0:n_ctx, :]
        kl = k_ref[0, n_ctx:n_all, :]
        vc = v_ref[0, 0:n_ctx, :]
        vl = v_ref[0, n_ctx:n_all, :]
        parts = []
        for j in range(2):
            s_c = _nt(sub_query(qp_ref[0], j), kc)
            s_l = _nt(sub_query(qr_ref[0], j), kl)
            m = jnp.maximum(row_max(s_c), row_max(s_l))
            e_c = jnp.exp(s_c - m)
            e_l = jnp.exp(s_l - m)
            parts.append((pv(e_c, vc) + pv(e_l, vl)) / (row_sum(e_c) + row_sum(e_l)))
        finish(parts)


def _attention(pc3, lam, gain, n_ctx_rows, lam_init, with_ctx):
    n_batch, rows, _ = pc3.shape
    q_tile0 = 0 if with_ctx else n_ctx_rows // ROW_TILE
    n_q = rows // ROW_TILE - q_tile0
    hb = C_W // LANES
    kern = functools.partial(_attn_kernel, n_ctx=n_ctx_rows, q_tile0=q_tile0, lam_init=lam_init)
    grid_spec = pltpu.PrefetchScalarGridSpec(
        num_scalar_prefetch=1,
        grid=(n_batch, HEADS, n_q),
        in_specs=[pl.BlockSpec((1, ROW_TILE, LANES), lambda b, h, i, lam: (b, i + q_tile0, h)),
                  pl.BlockSpec((1, ROW_TILE, LANES), lambda b, h, i, lam: (b, i + q_tile0, hb + h)),
                  pl.BlockSpec((1, rows, LANES), lambda b, h, i, lam: (b, 0, 2 * hb + h)),
                  pl.BlockSpec((1, rows, LANES), lambda b, h, i, lam: (b, 0, 3 * hb + h)),
                  pl.BlockSpec((1, LANES), lambda b, h, i, lam: (0, 0))],
        out_specs=pl.BlockSpec((1, ROW_TILE, LANES), lambda b, h, i, lam: (b, i, h)),
    )
    return pl.pallas_call(
        kern,
        out_shape=jax.ShapeDtypeStruct((n_batch, n_q * ROW_TILE, C_W), F32),
        grid_spec=grid_spec,
        compiler_params=_cparams(("arbitrary", "arbitrary", "arbitrary")),
        name="diff_attn",
    )(lam, pc3, pc3, pc3, pc3, gain)


def _outproj_kernel(oa_ref, ob_ref, oc_ref, x_ref, mod_ref, n_ref, w_ref, rw_ref, rb_ref,
                    xn_ref, h_ref, route_ref, cnt_ref, carry_ref):
    d = x_ref.shape[1]
    n_exp = rw_ref.shape[1]

    @pl.when(pl.program_id(0) == 0)
    def _():
        carry_ref[...] = jnp.zeros_like(carry_ref)

    mix = (jnp.dot(oa_ref[...].astype(BF16), w_ref[0:A_W, :], preferred_element_type=F32)
           + jnp.dot(ob_ref[...].astype(BF16), w_ref[A_W:A_W + B_W, :], preferred_element_type=F32)
           + jnp.dot(oc_ref[...].astype(BF16), w_ref[A_W + B_W:A_W + B_W + C_W, :], preferred_element_type=F32))
    xn = x_ref[...] + mod_ref[0, :, 2 * d:3 * d] * (_rms(mix) * n_ref[1:2, :])
    xn_ref[...] = xn
    h = _rms(xn) * n_ref[2:3, :] * (1.0 + mod_ref[0, :, 4 * d:5 * d]) + mod_ref[0, :, 3 * d:4 * d]
    bits = pltpu.bitcast(h.astype(BF16).astype(F32), jnp.uint32)
    words = (bits[:, :d // 2] >> 16) | (bits[:, d // 2:] & jnp.uint32(0xFFFF0000))
    h_ref[...] = pltpu.bitcast(words, jnp.int32)

    logits = jnp.dot(h, rw_ref[...], preferred_element_type=F32, precision=HI) + rb_ref[...]
    e_lane = lax.broadcasted_iota(jnp.int32, logits.shape, 1)
    cur = logits
    picks, vals = [], []
    for _ in range(TOP_K):
        mx = jnp.max(cur, axis=-1, keepdims=True)
        idx = jnp.min(jnp.where(cur == mx, e_lane, n_exp), axis=-1, keepdims=True)
        hit = e_lane == idx
        cur = jnp.where(hit, -jnp.inf, cur)
        picks.append((idx, hit.astype(F32)))
        vals.append(mx)
    exps = [jnp.exp(vv - vals[0]) for vv in vals]
    total = exps[0] + exps[1] + exps[2] + exps[3]

    chosen = picks[0][1] + picks[1][1] + picks[2][1] + picks[3][1]
    tm = logits.shape[0]
    before = (lax.broadcasted_iota(jnp.int32, (tm, tm), 1) < lax.broadcasted_iota(jnp.int32, (tm, tm), 0))
    seen = jnp.dot(before.astype(BF16), chosen.astype(BF16), preferred_element_type=F32) + carry_ref[...]
    new_carry = carry_ref[...] + jnp.sum(chosen, axis=0, keepdims=True)
    carry_ref[...] = new_carry
    cnt_ref[...] = new_carry

    o_lane = lax.broadcasted_iota(jnp.int32, (tm, LANES), 1)
    route = jnp.zeros((tm, LANES), F32)
    for kk in range(TOP_K):
        idx, hit = picks[kk]
        rank = jnp.sum(hit * seen, axis=-1, keepdims=True)
        route = (route + jnp.where(o_lane == kk, idx.astype(F32), 0.0)
                 + jnp.where(o_lane == TOP_K + kk, exps[kk] / total, 0.0)
                 + jnp.where(o_lane == 2 * TOP_K + kk, rank, 0.0))
    route_ref[...] = route


def _outproj(oa, ob, oc, xy, mod, norms, w_out_b, router_w, router_b, layer, n_tiles, in_tile, oc_tile, mod_row):
    d = xy.shape[1]
    n_exp = router_w.shape[2]
    t_out = n_tiles * ROW_TILE
    return pl.pallas_call(
        _outproj_kernel,
        out_shape=(jax.ShapeDtypeStruct((t_out, d), F32),
                   jax.ShapeDtypeStruct((t_out, d // 2), jnp.int32),
                   jax.ShapeDtypeStruct((t_out, LANES), F32),
                   jax.ShapeDtypeStruct((1, n_exp), F32)),
        grid=(n_tiles,),
        in_specs=[pl.BlockSpec((ROW_TILE, A_W), lambda i: (in_tile(i), 0)),
                  pl.BlockSpec((ROW_TILE, B_W), lambda i: (in_tile(i), 0)),
                  pl.BlockSpec((ROW_TILE, C_W), lambda i: (oc_tile(i), 0)),
                  pl.BlockSpec((ROW_TILE, d), lambda i: (in_tile(i), 0)),
                  pl.BlockSpec((1, 1, 6 * d), lambda i: (mod_row(i), 0, 0)),
                  pl.BlockSpec((None, 4, d), lambda i: (layer, 0, 0)),
                  pl.BlockSpec((None, d, d), lambda i: (layer, 0, 0)),
                  pl.BlockSpec((None, d, n_exp), lambda i: (layer, 0, 0)),
                  pl.BlockSpec((None, 1, n_exp), lambda i: (layer, 0, 0))],
        out_specs=(pl.BlockSpec((ROW_TILE, d), lambda i: (i, 0)),
                   pl.BlockSpec((ROW_TILE, d // 2), lambda i: (i, 0)),
                   pl.BlockSpec((ROW_TILE, LANES), lambda i: (i, 0)),
                   pl.BlockSpec((1, n_exp), lambda i: (0, 0))),
        scratch_shapes=[pltpu.VMEM((1, n_exp), F32)],
        compiler_params=_cparams(("arbitrary",)),
        name="outproj_router",
    )(oa, ob, oc, xy, mod, norms, w_out_b, router_w, router_b)


PAIR_BLOCK = 2 * LANES


def _moe_kernel(te_ref, first_ref, valid_ref, nu_ref, xs_ref, w1_ref, b1_ref, w2_ref, b2_ref, ys_ref,
                w1p_ref, w2b_ref):
    i = pl.program_id(0)
    two_f = w1_ref.shape[1]
    n_blk = two_f // PAIR_BLOCK

    @pl.when(jnp.logical_and(i < nu_ref[0], first_ref[i] == 1))
    def _():
        r = lax.broadcasted_iota(jnp.int32, (PAIR_BLOCK, PAIR_BLOCK), 0)
        c = lax.broadcasted_iota(jnp.int32, (PAIR_BLOCK, PAIR_BLOCK), 1)
        perm = (r == jnp.where(c < LANES, 2 * c, 2 * (c - LANES) + 1)).astype(BF16)
        for blk in range(n_blk):
            cols = slice(blk * PAIR_BLOCK, (blk + 1) * PAIR_BLOCK)
            w1p_ref[:, cols] = jnp.dot(w1_ref[:, cols].astype(BF16), perm,
                                       preferred_element_type=F32).astype(BF16)
        w2b_ref[...] = w2_ref[...].astype(BF16)

    @pl.when(i < nu_ref[0])
    def _():
        words = pltpu.bitcast(xs_ref[...], jnp.uint32)
        x = jnp.concatenate([pltpu.bitcast(words << 16, F32),
                             pltpu.bitcast(words & jnp.uint32(0xFFFF0000), F32)], axis=1).astype(BF16)
        row = lax.broadcasted_iota(jnp.int32, (x.shape[0], 1), 0)
        x = jnp.where(row < valid_ref[i], x, jnp.zeros_like(x))
        hid = jnp.dot(x, w1p_ref[...], preferred_element_type=F32) + b1_ref[...]
        acts = []
        for blk in range(n_blk):
            glu = jnp.minimum(hid[:, blk * PAIR_BLOCK:blk * PAIR_BLOCK + LANES], SWIGLU_LIMIT)
            lin = jnp.clip(hid[:, blk * PAIR_BLOCK + LANES:(blk + 1) * PAIR_BLOCK], -SWIGLU_LIMIT, SWIGLU_LIMIT)
            acts.append((glu * jax.nn.sigmoid(SWIGLU_ALPHA * glu) * (lin + 1.0)).astype(BF16))
        y = jnp.dot(jnp.concatenate(acts, axis=1), w2b_ref[...], preferred_element_type=F32) + b2_ref[...]
        ys_ref[...] = y.astype(ys_ref.dtype)

    @pl.when(i >= nu_ref[0])
    def _():
        ys_ref[...] = jnp.zeros_like(ys_ref)


def _moe(tile_expert, tile_first, tile_valid, n_used, xs, w1, b1p, w2, b2, layer):
    r_max, half_d = xs.shape
    d = 2 * half_d
    two_f = w1.shape[3]
    f = two_f // 2
    n_tiles = r_max // MOE_TILE

    def row_tile(i, te, fi, va, nu):
        return (jnp.maximum(jnp.minimum(i, nu[0] - 1), 0), 0)

    grid_spec = pltpu.PrefetchScalarGridSpec(
        num_scalar_prefetch=4,
        grid=(n_tiles,),
        in_specs=[pl.BlockSpec((MOE_TILE, half_d), row_tile),
                  pl.BlockSpec((None, None, d, two_f), lambda i, te, fi, va, nu: (layer, te[i], 0, 0)),
                  pl.BlockSpec((None, None, 1, two_f), lambda i, te, fi, va, nu: (layer, te[i], 0, 0)),
                  pl.BlockSpec((None, None, f, d), lambda i, te, fi, va, nu: (layer, te[i], 0, 0)),
                  pl.BlockSpec((None, None, 1, d), lambda i, te, fi, va, nu: (layer, te[i], 0, 0))],
        out_specs=pl.BlockSpec((MOE_TILE, d), lambda i, te, fi, va, nu: (i, 0)),
        scratch_shapes=[pltpu.VMEM((d, two_f), BF16), pltpu.VMEM((f, d), BF16)],
    )
    return pl.pallas_call(
        _moe_kernel,
        out_shape=jax.ShapeDtypeStruct((r_max, d), BF16),
        grid_spec=grid_spec,
        compiler_params=_cparams(("arbitrary",)),
        name="moe_experts",
    )(tile_expert, tile_first, tile_valid, n_used, xs, w1, b1p, w2, b2)


SC_CORES = 2
SC_SUBCORES = 16
SC_CHUNK = 64


def _dispatch_rows(hp, pos, r_max):
    t, width = hp.shape
    workers = SC_CORES * SC_SUBCORES
    assert t % (workers * SC_CHUNK) == 0
    per_worker = t // (workers * SC_CHUNK)
    idx = pos.reshape(workers, per_worker, SC_CHUNK, TOP_K).transpose(0, 1, 3, 2).reshape(
        workers, per_worker * TOP_K, SC_CHUNK)
    mesh = plsc.VectorSubcoreMesh(core_axis_name="c", subcore_axis_name="s",
                                  num_cores=SC_CORES, num_subcores=SC_SUBCORES)

    @functools.partial(
        pl.kernel, mesh=mesh,
        out_type=jax.ShapeDtypeStruct((r_max, width), hp.dtype),
        scratch_types=[pltpu.VMEM((per_worker * TOP_K, SC_CHUNK), jnp.int32),
                       pltpu.VMEM((SC_CHUNK, width), hp.dtype)],
    )
    def scatter(hp_hbm, idx_hbm, out_hbm, idx_v, rows_v):
        wid = lax.axis_index("s") * SC_CORES + lax.axis_index("c")
        pltpu.sync_copy(idx_hbm.at[wid], idx_v)

        @pl.loop(0, per_worker)
        def _(j):
            row0 = pl.multiple_of((wid * per_worker + j) * SC_CHUNK, SC_CHUNK)
            pltpu.sync_copy(hp_hbm.at[pl.ds(row0, SC_CHUNK)], rows_v)
            for k in range(TOP_K):
                pltpu.sync_copy(rows_v, out_hbm.at[idx_v.at[j * TOP_K + k]])

    return scatter(hp, idx)


def _ffn_residual_kernel(x_ref, f_ref, mod_ref, n_ref, o_ref):
    d = x_ref.shape[1]
    o_ref[...] = x_ref[...] + mod_ref[0, :, 5 * d:6 * d] * (_rms(f_ref[...]) * n_ref[3:4, :])


def _ffn_residual(xn, ffn, mod, norms, layer, mod_row):
    t, d = xn.shape
    return pl.pallas_call(
        _ffn_residual_kernel,
        out_shape=jax.ShapeDtypeStruct((t, d), F32),
        grid=(t // ROW_TILE,),
        in_specs=[pl.BlockSpec((ROW_TILE, d), lambda i: (i, 0)),
                  pl.BlockSpec((ROW_TILE, d), lambda i: (i, 0)),
                  pl.BlockSpec((1, 1, 6 * d), lambda i: (mod_row(i), 0, 0)),
                  pl.BlockSpec((None, 4, d), lambda i: (layer, 0, 0))],
        out_specs=pl.BlockSpec((ROW_TILE, d), lambda i: (i, 0)),
        compiler_params=_cparams(("arbitrary",)),
        name="ffn_residual",
    )(xn, ffn, mod, norms)


def _dispatch_plan(route, counts, r_max):
    t = route.shape[0]
    idx = route[:, 0:TOP_K].astype(jnp.int32)
    wts = route[:, TOP_K:2 * TOP_K]
    rank = route[:, 2 * TOP_K:3 * TOP_K].astype(jnp.int32)
    cnt = counts[0].astype(jnp.int32)
    padded = (cnt + MOE_TILE - 1) // MOE_TILE * MOE_TILE
    ends = jnp.cumsum(padded)
    pos = (ends - padded)[idx] + rank
    n_tiles = r_max // MOE_TILE
    n_used = ends[-1] // MOE_TILE
    tile_ids = jnp.minimum(jnp.arange(n_tiles, dtype=jnp.int32), n_used - 1)
    tile_expert = jnp.sum((ends // MOE_TILE)[None, :] <= tile_ids[:, None], axis=1).astype(jnp.int32)
    tile_first = jnp.concatenate(
        [jnp.ones((1,), jnp.int32), (tile_expert[1:] != tile_expert[:-1]).astype(jnp.int32)])
    first_tile = ((ends - padded) // MOE_TILE)[tile_expert]
    tile_valid = jnp.clip(cnt[tile_expert] - (tile_ids - first_tile) * MOE_TILE, 0, MOE_TILE).astype(jnp.int32)
    return pos, wts, tile_expert, tile_first, tile_valid, n_used.reshape(1).astype(jnp.int32)


def _rope_tables(length, n_ctx_rows):
    rows = length // GRID_W
    row = jnp.repeat(jnp.arange(rows, dtype=F32), GRID_W)
    col = jnp.tile(jnp.arange(GRID_W, dtype=F32), rows)
    n_freq = C_DQK // 4
    inv_freq = ROPE_THETA ** (-jnp.arange(n_freq, dtype=F32) / n_freq)
    ang_r = row[:, None] * inv_freq
    ang_c = col[:, None] * inv_freq
    cos = jnp.concatenate([jnp.cos(ang_r), jnp.cos(ang_r), jnp.cos(ang_c), jnp.cos(ang_c)], axis=-1)
    sin = jnp.concatenate([-jnp.sin(ang_r), jnp.sin(ang_r), -jnp.sin(ang_c), jnp.sin(ang_c)], axis=-1)
    reps = C_W // C_DQK
    cos = jnp.concatenate([jnp.ones((n_ctx_rows, C_W), F32), jnp.tile(cos, (1, reps))], axis=0)
    sin = jnp.concatenate([jnp.zeros((n_ctx_rows, C_W), F32), jnp.tile(sin, (1, reps))], axis=0)
    return cos, sin


def _hgrn_lower_bound(table, layer):
    p = jax.nn.softmax(table.astype(F32), axis=1)
    cum = jnp.cumsum(p, axis=1) - p[:, :1]
    return jnp.clip(cum[:, layer], 0.0, 1.0)


def kernel(x, c, ctx, c_ctx, ada_w, ada_b, sandwich_norms, w_in, w_out, hgrn_lower_bounds, hgrn_norm,
           mlstm_gate_bias, mlstm_norm, diff_lambdas, diff_norm, router_w, router_b, moe_w1, moe_b1,
           moe_w2, moe_b2):
    n_batch, seq, d = x.shape
    n_ctx_rows = ctx.shape[1]
    depth = w_in.shape[0]
    n_exp = router_w.shape[2]
    assert seq % ROW_TILE == 0 and n_ctx_rows % ROW_TILE == 0 and seq % GRID_W == 0
    rows_b = n_ctx_rows + seq
    tiles_b = rows_b // ROW_TILE
    ctx_tiles = n_ctx_rows // ROW_TILE
    lat_tiles = seq // ROW_TILE

    b_main = PA_W + 2 * HEADS * B_QK + 2 * B_W
    w_in_p = jnp.concatenate(
        [w_in[:, :, :b_main],
         jnp.pad(w_in[:, :, b_main:b_main + N_GATE], ((0, 0), (0, 0), (0, LANES - N_GATE))),
         w_in[:, :, b_main + N_GATE:]], axis=2).astype(BF16)
    w_out_b = w_out.astype(BF16)
    two_f = moe_b1.shape[2]
    b1p = moe_b1.reshape(depth, n_exp, two_f // PAIR_BLOCK, LANES, 2).transpose(0, 1, 2, 4, 3).reshape(
        depth, n_exp, 1, two_f)
    b2 = moe_b2[:, :, None, :]
    ada_b3 = ada_b[:, None, :]
    router_b3 = router_b[:, None, :]
    gate_bias = jnp.pad(mlstm_gate_bias, ((0, 0), (0, LANES - N_GATE)))
    cos, sin = _rope_tables(seq, n_ctx_rows)

    cond_rows = (n_batch + 1 + 7) // 8 * 8
    cvec = jnp.zeros((cond_rows, d), F32).at[:n_batch].set(c).at[n_batch].set(c_ctx)

    def mod_row_all(i):
        return jnp.where(i % tiles_b < ctx_tiles, n_batch, i // tiles_b)

    xy = jnp.concatenate([ctx, x], axis=1).reshape(n_batch * rows_b, d)

    for layer in range(depth):
        last = layer == depth - 1
        mod = _adaln(cvec, ada_w, ada_b3, layer).reshape(cond_rows, 1, 6 * d)
        lb = _hgrn_lower_bound(hgrn_lower_bounds, layer)
        lam_init = 0.8 - 0.6 * math.exp(-0.3 * layer)
        lq1, lk1, lq2, lk2 = diff_lambdas[layer].astype(F32)
        lam = (jnp.exp(jnp.sum(lq1 * lk1)) - jnp.exp(jnp.sum(lq2 * lk2)) + lam_init).reshape(1)

        pa, pb, pc = _inproj(xy, mod, sandwich_norms, w_in_p, cos, sin, layer, tiles_b, ctx_tiles, n_batch)
        oa = _hgrn(pa.reshape(n_batch, rows_b, PA_W), lb, jnp.tile(hgrn_norm[layer], HEADS)[None, :], n_ctx_rows)
        ob = _mlstm(pb.reshape(n_batch, rows_b, PB_W), gate_bias[layer][None, :], mlstm_norm[layer][None, :],
                    n_ctx_rows)
        oc = _attention(pc.reshape(n_batch, rows_b, PC_OUT), lam, diff_norm[layer][None, :], n_ctx_rows,
                        lam_init, not last)

        if last:
            n_tiles = n_batch * lat_tiles
            in_tile = lambda i: (i // lat_tiles) * tiles_b + ctx_tiles + i % lat_tiles
            mod_row = lambda i: i // lat_tiles
        else:
            n_tiles = n_batch * tiles_b
            in_tile = lambda i: i
            mod_row = mod_row_all
        xn, h, route, counts = _outproj(
            oa.reshape(-1, A_W), ob.reshape(-1, B_W), oc.reshape(-1, C_W), xy, mod, sandwich_norms, w_out_b,
            router_w, router_b3, layer, n_tiles, in_tile, lambda i: i, mod_row)

        r_max = n_tiles * ROW_TILE * TOP_K + n_exp * MOE_TILE
        pos, wts, tile_expert, tile_first, tile_valid, n_used = _dispatch_plan(route, counts, r_max)
        xs = _dispatch_rows(h, pos, r_max)
        ys = _moe(tile_expert, tile_first, tile_valid, n_used, xs, moe_w1, b1p, moe_w2, b2, layer)
        ffn = sum(wts[:, k:k + 1] * jnp.take(ys, pos[:, k], axis=0).astype(F32) for k in range(TOP_K))
        xy = _ffn_residual(xn, ffn, mod, sandwich_norms, layer, mod_row)

    return xy.reshape(n_batch, seq, d)
```

```python
import functools
import math

import jax
import jax.numpy as jnp
from jax import lax
from jax.experimental import pallas as pl
from jax.experimental.pallas import tpu as pltpu
from jax.experimental.pallas import tpu_sc as plsc

F32 = jnp.float32
BF16 = jnp.bfloat16
HI = lax.Precision.HIGHEST

HEADS = 4
A_W = 256
B_QK = 32
B_W = 256
C_DQK = 64
C_DV = 2 * C_DQK
C_W = 512
HEAD_V = 64
N_GATE = 16
GRID_W = 64
TOP_K = 4
SWIGLU_ALPHA = 1.702
SWIGLU_LIMIT = 7.0
ROPE_THETA = 10000.0
NORM_EPS = 1e-6
MASK_NEG = -1e30
F_MIN = 1e-12

LANES = 128
ROW_TILE = 256
CHUNK = 64
SUB = 16
MOE_TILE = 256
VMEM_LIMIT = 56 * 1024 * 1024

PA_W = 5 * A_W
PB_W = 2 * HEADS * B_QK + 2 * B_W + LANES
PC_IN = 3 * C_W
PC_OUT = 4 * C_W
W_IN_PAD = PA_W + PB_W + PC_IN


def _cparams(sem):
    return pltpu.CompilerParams(dimension_semantics=sem, vmem_limit_bytes=VMEM_LIMIT)


def _nt(a, b):
    return lax.dot_general(a, b, (((1,), (1,)), ((), ())), preferred_element_type=F32)


def _tn(a, b, precision=None):
    return lax.dot_general(a, b, (((0,), (0,)), ((), ())), preferred_element_type=F32, precision=precision)


def _rms(x):
    return x * lax.rsqrt(jnp.mean(x * x, axis=-1, keepdims=True) + NORM_EPS)


def _silu(x):
    return x * jax.nn.sigmoid(x)


def _adaln_kernel(c_ref, w_ref, b_ref, o_ref):
    cond = _silu(c_ref[...])
    o_ref[...] = jnp.dot(cond, w_ref[...], preferred_element_type=F32, precision=HI) + b_ref[...]


def _adaln(cvec, ada_w, ada_b, layer):
    rows, d = cvec.shape
    return pl.pallas_call(
        _adaln_kernel,
        out_shape=jax.ShapeDtypeStruct((rows, 6 * d), F32),
        grid=(6,),
        in_specs=[pl.BlockSpec((rows, d), lambda j: (0, 0)),
                  pl.BlockSpec((None, d, d), lambda j: (layer, 0, j)),
                  pl.BlockSpec((None, 1, d), lambda j: (layer, 0, j))],
        out_specs=pl.BlockSpec((rows, d), lambda j: (0, j)),
        compiler_params=_cparams(("arbitrary",)),
        name="adaln",
    )(cvec, ada_w, ada_b)


def _inproj_kernel(x_ref, mod_ref, g_ref, w_ref, cos_ref, sin_ref, pa_ref, pb_ref, pc_ref):
    d = x_ref.shape[1]
    shift = mod_ref[0, :, 0:d]
    scale = mod_ref[0, :, d:2 * d]
    h = _rms(x_ref[...]) * g_ref[0:1, :] * (1.0 + scale) + shift
    hb = h.astype(BF16)
    pa_ref[...] = jnp.dot(hb, w_ref[:, 0:PA_W], preferred_element_type=F32)
    pb_ref[...] = jnp.dot(hb, w_ref[:, PA_W:PA_W + PB_W], preferred_element_type=F32)
    pc = jnp.dot(hb, w_ref[:, PA_W + PB_W:W_IN_PAD], preferred_element_type=F32)
    q = pc[:, 0:C_W]
    k = pc[:, C_W:2 * C_W]
    cos = cos_ref[...]
    sin = sin_ref[...]
    lane = lax.broadcasted_iota(jnp.int32, q.shape, 1)
    first = (lane % 32) < 16

    def rope(t):
        partner = jnp.where(first, pltpu.roll(t, C_W - 16, 1), pltpu.roll(t, 16, 1))
        return t * cos + partner * sin

    pc_ref[:, 0:C_W] = q.astype(BF16)
    pc_ref[:, C_W:2 * C_W] = rope(q).astype(BF16)
    pc_ref[:, 2 * C_W:3 * C_W] = rope(k).astype(BF16)
    pc_ref[:, 3 * C_W:4 * C_W] = pc[:, 2 * C_W:3 * C_W].astype(BF16)


def _inproj(xy, mod, norms, w_in_p, cos, sin, layer, tiles_per_batch, n_ctx_tiles, n_batch):
    t_all, d = xy.shape
    n_tiles = t_all // ROW_TILE

    def mod_row(i):
        return jnp.where(i % tiles_per_batch < n_ctx_tiles, n_batch, i // tiles_per_batch)

    return pl.pallas_call(
        _inproj_kernel,
        out_shape=(jax.ShapeDtypeStruct((t_all, PA_W), F32),
                   jax.ShapeDtypeStruct((t_all, PB_W), F32),
                   jax.ShapeDtypeStruct((t_all, PC_OUT), BF16)),
        grid=(n_tiles,),
        in_specs=[pl.BlockSpec((ROW_TILE, d), lambda i: (i, 0)),
                  pl.BlockSpec((1, 1, 6 * d), lambda i: (mod_row(i), 0, 0)),
                  pl.BlockSpec((None, 4, d), lambda i: (layer, 0, 0)),
                  pl.BlockSpec((None, d, W_IN_PAD), lambda i: (layer, 0, 0)),
                  pl.BlockSpec((ROW_TILE, C_W), lambda i: (i % tiles_per_batch, 0)),
                  pl.BlockSpec((ROW_TILE, C_W), lambda i: (i % tiles_per_batch, 0))],
        out_specs=(pl.BlockSpec((ROW_TILE, PA_W), lambda i: (i, 0)),
                   pl.BlockSpec((ROW_TILE, PB_W), lambda i: (i, 0)),
                   pl.BlockSpec((ROW_TILE, PC_OUT), lambda i: (i, 0))),
        compiler_params=_cparams(("arbitrary",)),
        name="inproj",
    )(xy, mod, norms, w_in_p, cos, sin)


def _hgrn_kernel(pa_ref, lb_ref, gain_ref, o_ref, st_ref, acc_ref, ob_ref, *, n_ctx, n_all):
    c_sz, w = CHUNK, A_W
    r_i = lax.broadcasted_iota(jnp.int32, (c_sz, c_sz), 0)
    c_i = lax.broadcasted_iota(jnp.int32, (c_sz, c_sz), 1)
    tri_incl_past = (c_i <= r_i).astype(F32)
    tri_incl_future = (c_i >= r_i).astype(F32)
    same_head = (lax.broadcasted_iota(jnp.int32, (w, w), 0) // HEAD_V
                 == lax.broadcasted_iota(jnp.int32, (w, w), 1) // HEAD_V)
    head_ones = same_head.astype(BF16)
    head_ones_f32 = same_head.astype(F32)
    sub_pos = lax.broadcasted_iota(jnp.int32, (c_sz, 1), 0) % SUB
    n_sub = c_sz // SUB

    def sub_rows(a, sp):
        return jnp.concatenate(
            [jnp.broadcast_to(a[SUB * i + sp:SUB * i + sp + 1, :], (SUB, w)) for i in range(n_sub)], axis=0)

    def chunk(c, rev):
        d = 1 if rev else 0
        r0 = pl.multiple_of(c * c_sz, c_sz)
        rows = pl.ds(r0, c_sz)
        q_pre = pa_ref[0, rows, 0:A_W]
        v = pa_ref[0, rows, A_W:2 * A_W]
        f_pre = pa_ref[0, rows, 3 * A_W:4 * A_W] if rev else pa_ref[0, rows, 2 * A_W:3 * A_W]
        lb = lb_ref[1:2, :] if rev else lb_ref[0:1, :]
        q = _silu(q_pre)
        f = lb + (1.0 - lb) * jax.nn.sigmoid(f_pre)
        log_f = jnp.log(jnp.maximum(f, F_MIN))
        kk = (1.0 - lb) * jax.nn.sigmoid(-f_pre)
        cum = jnp.dot(tri_incl_future if rev else tri_incl_past, log_f, preferred_element_type=F32, precision=HI)
        e = 0 if rev else c_sz - 1
        cum_end = cum[e:e + 1, :]
        vb = v.astype(BF16)

        st = st_ref[d]
        acc_ref[d] = _nt((q * jnp.exp(cum)).astype(BF16), st.astype(BF16))
        k_end = (kk * jnp.exp(cum_end - cum)).astype(BF16)
        st_ref[d] = st * jnp.exp(cum_end) + jnp.where(same_head, _tn(vb, k_end), 0.0)

        def off(t0, s0, n, ref_row):
            ref = cum[ref_row:ref_row + 1, :]
            qa = (q[t0:t0 + n] * jnp.exp(cum[t0:t0 + n] - ref)).astype(BF16)
            ka = (kk[s0:s0 + n] * jnp.exp(ref - cum[s0:s0 + n])).astype(BF16)
            u = jnp.where(same_head, _tn(vb[s0:s0 + n], ka), 0.0).astype(BF16)
            acc_ref[d, t0:t0 + n, :] += _nt(qa, u)

        if rev:
            off(0, 32, 32, 32)
            off(0, 16, 16, 16)
            off(32, 48, 16, 48)
        else:
            off(32, 0, 32, 31)
            off(16, 0, 16, 15)
            off(48, 32, 16, 47)

        fs = []
        for sp in range(SUB):
            live = (sub_pos <= sp) if rev else (sub_pos >= sp)
            arg = jnp.where(live, cum - sub_rows(cum, sp), MASK_NEG)
            fs.append((q * jnp.exp(arg) * sub_rows(kk, sp)).astype(BF16))
        r = jnp.dot(jnp.concatenate(fs, axis=0), head_ones, preferred_element_type=F32)
        o = acc_ref[d]
        for sp in range(SUB):
            o = o + r[sp * c_sz:(sp + 1) * c_sz, :] * sub_rows(v, sp)
        if rev:
            ob_ref[rows, :] = o
        else:
            o_ref[0, rows, :] = o

    def scan_step(n, carry):
        chunk(n, False)
        chunk(jnp.where(n < n_ctx, n_ctx - 1 - n, n_all - 1 - (n - n_ctx)), True)
        return carry

    st_ref[...] = jnp.zeros_like(st_ref)
    lax.fori_loop(0, n_all, scan_step, 0)

    def readout(c, carry):
        rows = pl.ds(pl.multiple_of(c * ROW_TILE, ROW_TILE), ROW_TILE)
        tot = o_ref[0, rows, :] + ob_ref[rows, :]
        ms = jnp.dot(tot * tot, head_ones_f32, preferred_element_type=F32, precision=HI) * (1.0 / HEAD_V)
        g = pa_ref[0, rows, 4 * A_W:5 * A_W]
        o_ref[0, rows, :] = tot * lax.rsqrt(ms + NORM_EPS) * gain_ref[...] * _silu(g)
        return carry

    lax.fori_loop(0, n_all * c_sz // ROW_TILE, readout, 0)


def _hgrn(pa3, lb, gain, n_ctx_rows):
    n_batch, rows, _ = pa3.shape
    kern = functools.partial(_hgrn_kernel, n_ctx=n_ctx_rows // CHUNK, n_all=rows // CHUNK)
    return pl.pallas_call(
        kern,
        out_shape=jax.ShapeDtypeStruct((n_batch, rows, A_W), F32),
        grid=(n_batch,),
        in_specs=[pl.BlockSpec((1, rows, PA_W), lambda b: (b, 0, 0)),
                  pl.BlockSpec((2, A_W), lambda b: (0, 0)),
                  pl.BlockSpec((1, A_W), lambda b: (0, 0))],
        out_specs=pl.BlockSpec((1, rows, A_W), lambda b: (b, 0, 0)),
        scratch_shapes=[pltpu.VMEM((2, A_W, A_W), F32), pltpu.VMEM((2, CHUNK, A_W), F32),
                        pltpu.VMEM((rows, A_W), F32)],
        compiler_params=_cparams(("arbitrary",)),
        name="hgrn2",
    )(pa3, lb, gain)


def _split3(x):
    hi = x.astype(BF16)
    rest = x - hi.astype(F32)
    mid = rest.astype(BF16)
    return hi, mid, (rest - mid.astype(F32)).astype(BF16)


def _mlstm_kernel(pb_ref, bias_ref, gain_ref, o_ref, cn_ref, ob_ref, *, n_ctx, n_all):
    assert CHUNK == HEAD_V
    c_sz, w, qk_w = CHUNK, B_W, HEADS * B_QK
    t_i = lax.broadcasted_iota(jnp.int32, (c_sz, w), 0)
    s_i = lax.broadcasted_iota(jnp.int32, (c_sz, w), 1) % c_sz
    r_i = lax.broadcasted_iota(jnp.int32, (c_sz, c_sz), 0)
    c_i = lax.broadcasted_iota(jnp.int32, (c_sz, c_sz), 1)
    diag4 = s_i == t_i
    half_lane = lax.broadcasted_iota(jnp.int32, (1, LANES), 1) < HEAD_V
    k_block = (lax.broadcasted_iota(jnp.int32, (HEADS * c_sz, qk_w), 0) // c_sz
               == lax.broadcasted_iota(jnp.int32, (HEADS * c_sz, qk_w), 1) // B_QK)
    v_block = (lax.broadcasted_iota(jnp.int32, (HEADS * c_sz, w), 0) // c_sz
               == lax.broadcasted_iota(jnp.int32, (HEADS * c_sz, w), 1) // HEAD_V)
    state_block = (lax.broadcasted_iota(jnp.int32, (qk_w, 2 * w), 0) // B_QK
                   == (lax.broadcasted_iota(jnp.int32, (qk_w, 2 * w), 1) % w) // HEAD_V)
    head_ones = v_block.astype(BF16)
    q_off, k_off, v_off, o_off, g_off = 0, qk_w, 2 * qk_w, 2 * qk_w + B_W, 2 * qk_w + 2 * B_W

    def selector(rev):
        sr = lax.broadcasted_iota(jnp.int32, (2 * LANES, 2 * w), 0)
        sc = lax.broadcasted_iota(jnp.int32, (2 * LANES, 2 * w), 1)
        i_base = 2 * HEADS if rev else 0
        want = jnp.where(sc < w, i_base + HEADS + sc // HEAD_V, LANES + i_base + (sc - w) // HEAD_V)
        return (sr == want).astype(BF16)

    def head_max(x):
        outs = []
        for col in range(w // LANES):
            xc = x[:, col * LANES:(col + 1) * LANES]
            lo = jnp.max(jnp.where(half_lane, xc, -jnp.inf), axis=-1, keepdims=True)
            hi = jnp.max(jnp.where(half_lane, -jnp.inf, xc), axis=-1, keepdims=True)
            outs.append(jnp.where(half_lane, lo, hi))
        return jnp.concatenate(outs, axis=1)

    def chunk(c, rev, m_prev, sel):
        d = 1 if rev else 0
        rows = pl.ds(pl.multiple_of(c * c_sz, c_sz), c_sz)
        qb = (pb_ref[0, rows, q_off:q_off + qk_w] * (B_QK ** -0.5)).astype(BF16)
        kb = pb_ref[0, rows, k_off:k_off + qk_w].astype(BF16)
        v = pb_ref[0, rows, v_off:v_off + B_W]
        gts = pb_ref[0, rows, g_off:g_off + LANES] + bias_ref[...]
        log_f = jnp.minimum(gts, 0.0) - jnp.log(1.0 + jnp.exp(-jnp.abs(gts)))
        mask = (s_i >= t_i) if rev else (s_i <= t_i)
        tri = ((c_i >= r_i) if rev else (c_i <= r_i)).astype(BF16)
        cum_col = sum(jnp.dot(tri, p, preferred_element_type=F32) for p in _split3(log_f))
        both = sum(jnp.dot(p, sel, preferred_element_type=F32)
                   for p in _split3(jnp.concatenate([cum_col, gts], axis=1)))
        cum_t = both[:, :w]
        ig_t = both[:, w:]
        src = jnp.sum(jnp.where(diag4, cum_t - ig_t, 0.0), axis=0, keepdims=True)
        log_d = jnp.where(mask, cum_t - src, MASK_NEG)
        log_inter = cum_t + m_prev
        m_t = jnp.maximum(log_inter, head_max(log_d))
        w_intra = jnp.where(mask, jnp.exp(log_d - m_t), 0.0)
        w_inter = jnp.exp(log_inter - m_t)

        k_bd = jnp.where(k_block, jnp.concatenate([kb] * HEADS, axis=0), jnp.zeros((), BF16))
        p = (_nt(qb, k_bd) * w_intra).astype(BF16)
        vb = v.astype(BF16)
        v_bd = jnp.where(v_block, jnp.concatenate([vb] * HEADS, axis=0), jnp.zeros((), BF16))
        cn = cn_ref[d]
        inter = jnp.dot(qb, cn.astype(BF16), preferred_element_type=F32)
        num = w_inter * inter[:, :w] + jnp.dot(p, v_bd, preferred_element_type=F32)
        den = w_inter * inter[:, w:] + jnp.dot(p, head_ones, preferred_element_type=F32)
        h_out = num / jnp.maximum(jnp.abs(den), jnp.exp(-m_t))

        e = 0 if rev else c_sz - 1
        cum_e = cum_t[e:e + 1, :]
        log_end = cum_e - cum_t + ig_t
        m_end = jnp.maximum(cum_e + m_prev, jnp.max(log_end, axis=0, keepdims=True))
        w_end = jnp.exp(log_end - m_end)
        w_carry = jnp.exp(cum_e + m_prev - m_end)
        upd = _tn(kb, jnp.concatenate([(w_end * v).astype(BF16), w_end.astype(BF16)], axis=1))
        cn_ref[d] = jnp.concatenate([w_carry, w_carry], axis=1) * cn + jnp.where(state_block, upd, 0.0)
        if rev:
            ob_ref[rows, :] = h_out
        else:
            o_ref[0, rows, :] = h_out
        return m_end

    sel_fwd = selector(False)
    sel_bwd = selector(True)

    def scan_step(n, carry):
        m_fwd, m_bwd = carry
        m_fwd = chunk(n, False, m_fwd, sel_fwd)
        m_bwd = chunk(jnp.where(n < n_ctx, n_ctx - 1 - n, n_all - 1 - (n - n_ctx)), True, m_bwd, sel_bwd)
        return m_fwd, m_bwd

    zero_m = jnp.zeros((1, w), F32)
    cn_ref[...] = jnp.zeros_like(cn_ref)
    lax.fori_loop(0, n_all, scan_step, (zero_m, zero_m), unroll=2)

    def readout(c, carry):
        rows = pl.ds(pl.multiple_of(c * ROW_TILE, ROW_TILE), ROW_TILE)
        tot = o_ref[0, rows, :] + ob_ref[rows, :]
        normed = jnp.concatenate(
            [_rms(tot[:, h * HEAD_V:(h + 1) * HEAD_V]) for h in range(HEADS)], axis=1) * gain_ref[...]
        og = pb_ref[0, rows, o_off:o_off + B_W]
        o_ref[0, rows, :] = normed * jax.nn.sigmoid(og)
        return carry

    lax.fori_loop(0, n_all * c_sz // ROW_TILE, readout, 0)


def _mlstm(pb3, bias, gain, n_ctx_rows):
    n_batch, rows, _ = pb3.shape
    kern = functools.partial(_mlstm_kernel, n_ctx=n_ctx_rows // CHUNK, n_all=rows // CHUNK)
    return pl.pallas_call(
        kern,
        out_shape=jax.ShapeDtypeStruct((n_batch, rows, B_W), F32),
        grid=(n_batch,),
        in_specs=[pl.BlockSpec((1, rows, PB_W), lambda b: (b, 0, 0)),
                  pl.BlockSpec((1, LANES), lambda b: (0, 0)),
                  pl.BlockSpec((1, B_W), lambda b: (0, 0))],
        out_specs=pl.BlockSpec((1, rows, B_W), lambda b: (b, 0, 0)),
        scratch_shapes=[pltpu.VMEM((2, HEADS * B_QK, 2 * B_W), F32), pltpu.VMEM((rows, B_W), F32)],
        compiler_params=_cparams(("arbitrary",)),
        name="mlstm",
    )(pb3, bias, gain)


def _attn_kernel(lam_ref, qp_ref, qr_ref, k_ref, v_ref, gain_ref, o_ref, *, n_ctx, q_tile0, lam_init):
    lam = lam_ref[0]
    q_tile = pl.program_id(2) + q_tile0
    lane = lax.broadcasted_iota(jnp.int32, (1, 2 * C_DQK), 1)
    scale = C_DQK ** -0.5

    def finish(parts):
        o = parts[0] - lam * parts[1]
        o_ref[0] = _rms(o) * gain_ref[...] * (1.0 - lam_init)

    def sub_query(q, j):
        return jnp.where(lane // C_DQK == j, q * scale, jnp.zeros_like(q))

    def row_max(s):
        return jnp.max(s, axis=-1, keepdims=True)

    def row_sum(s):
        return jnp.sum(s, axis=-1, keepdims=True)

    def pv(ex, v):
        return jnp.dot(ex.astype(BF16), v, preferred_element_type=F32)

    @pl.when(q_tile * ROW_TILE < n_ctx)
    def _():
        kc = k_ref[0, 0:n_ctx, :]
        vc = v_ref[0, 0:n_ctx, :]
        parts = []
        for j in range(2):
            s = _nt(sub_query(qp_ref[0], j), kc)
            ex = jnp.exp(s - row_max(s))
            parts.append(pv(ex, vc) / row_sum(ex))
        finish(parts)

    @pl.when(q_tile * ROW_TILE >= n_ctx)
    def _():
        n_all = k_ref.shape[1]
        kc = k_ref[0, 0:n_ctx, :]
        kl = k_ref[0, n_ctx:n_all, :]
        vc = v_ref[0, 0:n_ctx, :]
        vl = v_ref[0, n_ctx:n_all, :]
        parts = []
        for j in range(2):
            s_c = _nt(sub_query(qp_ref[0], j), kc)
            s_l = _nt(sub_query(qr_ref[0], j), kl)
            m = jnp.maximum(row_max(s_c), row_max(s_l))
            e_c = jnp.exp(s_c - m)
            e_l = jnp.exp(s_l - m)
            parts.append((pv(e_c, vc) + pv(e_l, vl)) / (row_sum(e_c) + row_sum(e_l)))
        finish(parts)


def _attention(pc3, lam, gain, n_ctx_rows, lam_init, with_ctx):
    n_batch, rows, _ = pc3.shape
    q_tile0 = 0 if with_ctx else n_ctx_rows // ROW_TILE
    n_q = rows // ROW_TILE - q_tile0
    hb = C_W // LANES
    kern = functools.partial(_attn_kernel, n_ctx=n_ctx_rows, q_tile0=q_tile0, lam_init=lam_init)
    grid_spec = pltpu.PrefetchScalarGridSpec(
        num_scalar_prefetch=1,
        grid=(n_batch, HEADS, n_q),
        in_specs=[pl.BlockSpec((1, ROW_TILE, LANES), lambda b, h, i, lam: (b, i + q_tile0, h)),
                  pl.BlockSpec((1, ROW_TILE, LANES), lambda b, h, i, lam: (b, i + q_tile0, hb + h)),
                  pl.BlockSpec((1, rows, LANES), lambda b, h, i, lam: (b, 0, 2 * hb + h)),
                  pl.BlockSpec((1, rows, LANES), lambda b, h, i, lam: (b, 0, 3 * hb + h)),
                  pl.BlockSpec((1, LANES), lambda b, h, i, lam: (0, 0))],
        out_specs=pl.BlockSpec((1, ROW_TILE, LANES), lambda b, h, i, lam: (b, i, h)),
    )
    return pl.pallas_call(
        kern,
        out_shape=jax.ShapeDtypeStruct((n_batch, n_q * ROW_TILE, C_W), F32),
        grid_spec=grid_spec,
        compiler_params=_cparams(("arbitrary", "arbitrary", "arbitrary")),
        name="diff_attn",
    )(lam, pc3, pc3, pc3, pc3, gain)


def _outproj_kernel(oa_ref, ob_ref, oc_ref, x_ref, mod_ref, n_ref, w_ref, rw_ref, rb_ref,
                    xn_ref, h_ref, route_ref, cnt_ref, carry_ref):
    d = x_ref.shape[1]
    n_exp = rw_ref.shape[1]

    @pl.when(pl.program_id(0) == 0)
    def _():
        carry_ref[...] = jnp.zeros_like(carry_ref)

    mix = (jnp.dot(oa_ref[...].astype(BF16), w_ref[0:A_W, :], preferred_element_type=F32)
           + jnp.dot(ob_ref[...].astype(BF16), w_ref[A_W:A_W + B_W, :], preferred_element_type=F32)
           + jnp.dot(oc_ref[...].astype(BF16), w_ref[A_W + B_W:A_W + B_W + C_W, :], preferred_element_type=F32))
    xn = x_ref[...] + mod_ref[0, :, 2 * d:3 * d] * (_rms(mix) * n_ref[1:2, :])
    xn_ref[...] = xn
    h = _rms(xn) * n_ref[2:3, :] * (1.0 + mod_ref[0, :, 4 * d:5 * d]) + mod_ref[0, :, 3 * d:4 * d]
    bits = pltpu.bitcast(h.astype(BF16).astype(F32), jnp.uint32)
    words = (bits[:, :d // 2] >> 16) | (bits[:, d // 2:] & jnp.uint32(0xFFFF0000))
    h_ref[...] = pltpu.bitcast(words, jnp.int32)

    h_hi = h.astype(BF16)
    h_mid = (h - h_hi.astype(F32)).astype(BF16)
    logits = _nt(rw_ref[0], h_hi) + _nt(rw_ref[1], h_hi) + _nt(rw_ref[0], h_mid) + rb_ref[...]
    e_sub = lax.broadcasted_iota(jnp.int32, logits.shape, 0)
    cur = logits
    picks, vals = [], []
    for _ in range(TOP_K):
        mx = jnp.max(cur, axis=0, keepdims=True)
        idx = jnp.min(jnp.where(cur == mx, e_sub, n_exp), axis=0, keepdims=True)
        hit = e_sub == idx
        cur = jnp.where(hit, -jnp.inf, cur)
        picks.append((idx, hit.astype(F32)))
        vals.append(mx)
    exps = [jnp.exp(vv - vals[0]) for vv in vals]
    total = exps[0] + exps[1] + exps[2] + exps[3]

    chosen = picks[0][1] + picks[1][1] + picks[2][1] + picks[3][1]
    tm = logits.shape[1]
    before = (lax.broadcasted_iota(jnp.int32, (tm, tm), 0) < lax.broadcasted_iota(jnp.int32, (tm, tm), 1))
    seen = jnp.dot(chosen.astype(BF16), before.astype(BF16), preferred_element_type=F32) + carry_ref[...]
    new_carry = carry_ref[...] + jnp.sum(chosen, axis=1, keepdims=True)
    carry_ref[...] = new_carry
    cnt_ref[...] = new_carry

    o_row = lax.broadcasted_iota(jnp.int32, route_ref.shape, 0)
    route = jnp.zeros(route_ref.shape, F32)
    for kk in range(TOP_K):
        idx, hit = picks[kk]
        rank = jnp.sum(hit * seen, axis=0, keepdims=True)
        route = (route + jnp.where(o_row == kk, idx.astype(F32), 0.0)
                 + jnp.where(o_row == TOP_K + kk, exps[kk] / total, 0.0)
                 + jnp.where(o_row == 2 * TOP_K + kk, rank, 0.0))
    route_ref[...] = route


ROUTE_ROWS = 16


def _outproj(oa, ob, oc, xy, mod, norms, w_out_b, router_wt, router_b, layer, n_tiles, in_tile, oc_tile, mod_row):
    d = xy.shape[1]
    n_exp = router_wt.shape[2]
    t_out = n_tiles * ROW_TILE
    xn, h, route_t, counts = pl.pallas_call(
        _outproj_kernel,
        out_shape=(jax.ShapeDtypeStruct((t_out, d), F32),
                   jax.ShapeDtypeStruct((t_out, d // 2), jnp.int32),
                   jax.ShapeDtypeStruct((n_tiles * ROUTE_ROWS, ROW_TILE), F32),
                   jax.ShapeDtypeStruct((n_exp, 1), F32)),
        grid=(n_tiles,),
        in_specs=[pl.BlockSpec((ROW_TILE, A_W), lambda i: (in_tile(i), 0)),
                  pl.BlockSpec((ROW_TILE, B_W), lambda i: (in_tile(i), 0)),
                  pl.BlockSpec((ROW_TILE, C_W), lambda i: (oc_tile(i), 0)),
                  pl.BlockSpec((ROW_TILE, d), lambda i: (in_tile(i), 0)),
                  pl.BlockSpec((1, 1, 6 * d), lambda i: (mod_row(i), 0, 0)),
                  pl.BlockSpec((None, 4, d), lambda i: (layer, 0, 0)),
                  pl.BlockSpec((None, d, d), lambda i: (layer, 0, 0)),
                  pl.BlockSpec((None, 2, n_exp, d), lambda i: (layer, 0, 0, 0)),
                  pl.BlockSpec((None, n_exp, 1), lambda i: (layer, 0, 0))],
        out_specs=(pl.BlockSpec((ROW_TILE, d), lambda i: (i, 0)),
                   pl.BlockSpec((ROW_TILE, d // 2), lambda i: (i, 0)),
                   pl.BlockSpec((ROUTE_ROWS, ROW_TILE), lambda i: (i, 0)),
                   pl.BlockSpec((n_exp, 1), lambda i: (0, 0))),
        scratch_shapes=[pltpu.VMEM((n_exp, 1), F32)],
        compiler_params=_cparams(("arbitrary",)),
        name="outproj_router",
    )(oa, ob, oc, xy, mod, norms, w_out_b, router_wt, router_b)
    route = route_t.reshape(n_tiles, ROUTE_ROWS, ROW_TILE).transpose(0, 2, 1).reshape(t_out, ROUTE_ROWS)
    return xn, h, route, counts[:, 0]


PAIR_BLOCK = 2 * LANES


def _moe_kernel(te_ref, first_ref, valid_ref, nu_ref, xs_ref, w1_ref, b1_ref, w2_ref, b2_ref, ys_ref,
                w1p_ref, w2b_ref):
    i = pl.program_id(0)
    two_f = w1_ref.shape[1]
    n_blk = two_f // PAIR_BLOCK

    @pl.when(jnp.logical_and(i < nu_ref[0], first_ref[i] == 1))
    def _():
        r = lax.broadcasted_iota(jnp.int32, (PAIR_BLOCK, PAIR_BLOCK), 0)
        c = lax.broadcasted_iota(jnp.int32, (PAIR_BLOCK, PAIR_BLOCK), 1)
        perm = (r == jnp.where(c < LANES, 2 * c, 2 * (c - LANES) + 1)).astype(BF16)
        for blk in range(n_blk):
            cols = slice(blk * PAIR_BLOCK, (blk + 1) * PAIR_BLOCK)
            w1p_ref[:, cols] = jnp.dot(w1_ref[:, cols].astype(BF16), perm,
                                       preferred_element_type=F32).astype(BF16)
        w2b_ref[...] = w2_ref[...].astype(BF16)

    @pl.when(i < nu_ref[0])
    def _():
        words = pltpu.bitcast(xs_ref[...], jnp.uint32)
        x = jnp.concatenate([pltpu.bitcast(words << 16, F32),
                             pltpu.bitcast(words & jnp.uint32(0xFFFF0000), F32)], axis=1).astype(BF16)
        row = lax.broadcasted_iota(jnp.int32, (x.shape[0], 1), 0)
        x = jnp.where(row < valid_ref[i], x, jnp.zeros_like(x))
        hid = jnp.dot(x, w1p_ref[...], preferred_element_type=F32) + b1_ref[...]
        acts = []
        for blk in range(n_blk):
            glu = jnp.minimum(hid[:, blk * PAIR_BLOCK:blk * PAIR_BLOCK + LANES], SWIGLU_LIMIT)
            lin = jnp.clip(hid[:, blk * PAIR_BLOCK + LANES:(blk + 1) * PAIR_BLOCK], -SWIGLU_LIMIT, SWIGLU_LIMIT)
            acts.append((glu * jax.nn.sigmoid(SWIGLU_ALPHA * glu) * (lin + 1.0)).astype(BF16))
        y = jnp.dot(jnp.concatenate(acts, axis=1), w2b_ref[...], preferred_element_type=F32) + b2_ref[...]
        ys_ref[...] = y.astype(ys_ref.dtype)

    @pl.when(i >= nu_ref[0])
    def _():
        ys_ref[...] = jnp.zeros_like(ys_ref)


def _moe(tile_expert, tile_first, tile_valid, n_used, xs, w1, b1p, w2, b2, layer):
    r_max, half_d = xs.shape
    d = 2 * half_d
    two_f = w1.shape[3]
    f = two_f // 2
    n_tiles = r_max // MOE_TILE

    def row_tile(i, te, fi, va, nu):
        return (jnp.maximum(jnp.minimum(i, nu[0] - 1), 0), 0)

    grid_spec = pltpu.PrefetchScalarGridSpec(
        num_scalar_prefetch=4,
        grid=(n_tiles,),
        in_specs=[pl.BlockSpec((MOE_TILE, half_d), row_tile),
                  pl.BlockSpec((None, None, d, two_f), lambda i, te, fi, va, nu: (layer, te[i], 0, 0)),
                  pl.BlockSpec((None, None, 1, two_f), lambda i, te, fi, va, nu: (layer, te[i], 0, 0)),
                  pl.BlockSpec((None, None, f, d), lambda i, te, fi, va, nu: (layer, te[i], 0, 0)),
                  pl.BlockSpec((None, None, 1, d), lambda i, te, fi, va, nu: (layer, te[i], 0, 0))],
        out_specs=pl.BlockSpec((MOE_TILE, d), lambda i, te, fi, va, nu: (i, 0)),
        scratch_shapes=[pltpu.VMEM((d, two_f), BF16), pltpu.VMEM((f, d), BF16)],
    )
    return pl.pallas_call(
        _moe_kernel,
        out_shape=jax.ShapeDtypeStruct((r_max, d), BF16),
        grid_spec=grid_spec,
        compiler_params=_cparams(("arbitrary",)),
        name="moe_experts",
    )(tile_expert, tile_first, tile_valid, n_used, xs, w1, b1p, w2, b2)


SC_CORES = 2
SC_SUBCORES = 16
SC_CHUNK = 64


def _dispatch_rows(hp, pos, r_max):
    t, width = hp.shape
    workers = SC_CORES * SC_SUBCORES
    assert t % (workers * SC_CHUNK) == 0
    per_worker = t // (workers * SC_CHUNK)
    idx = pos.reshape(workers, per_worker, SC_CHUNK, TOP_K).transpose(0, 1, 3, 2).reshape(
        workers, per_worker * TOP_K, SC_CHUNK)
    mesh = plsc.VectorSubcoreMesh(core_axis_name="c", subcore_axis_name="s",
                                  num_cores=SC_CORES, num_subcores=SC_SUBCORES)

    @functools.partial(
        pl.kernel, mesh=mesh,
        out_type=jax.ShapeDtypeStruct((r_max, width), hp.dtype),
        scratch_types=[pltpu.VMEM((per_worker * TOP_K, SC_CHUNK), jnp.int32),
                       pltpu.VMEM((SC_CHUNK, width), hp.dtype)],
    )
    def scatter(hp_hbm, idx_hbm, out_hbm, idx_v, rows_v):
        wid = lax.axis_index("s") * SC_CORES + lax.axis_index("c")
        pltpu.sync_copy(idx_hbm.at[wid], idx_v)

        @pl.loop(0, per_worker)
        def _(j):
            row0 = pl.multiple_of((wid * per_worker + j) * SC_CHUNK, SC_CHUNK)
            pltpu.sync_copy(hp_hbm.at[pl.ds(row0, SC_CHUNK)], rows_v)
            for k in range(TOP_K):
                pltpu.sync_copy(rows_v, out_hbm.at[idx_v.at[j * TOP_K + k]])

    return scatter(hp, idx)


def _ffn_residual_kernel(x_ref, f_ref, mod_ref, n_ref, o_ref):
    d = x_ref.shape[1]
    o_ref[...] = x_ref[...] + mod_ref[0, :, 5 * d:6 * d] * (_rms(f_ref[...]) * n_ref[3:4, :])


def _ffn_residual(xn, ffn, mod, norms, layer, mod_row):
    t, d = xn.shape
    return pl.pallas_call(
        _ffn_residual_kernel,
        out_shape=jax.ShapeDtypeStruct((t, d), F32),
        grid=(t // ROW_TILE,),
        in_specs=[pl.BlockSpec((ROW_TILE, d), lambda i: (i, 0)),
                  pl.BlockSpec((ROW_TILE, d), lambda i: (i, 0)),
                  pl.BlockSpec((1, 1, 6 * d), lambda i: (mod_row(i), 0, 0)),
                  pl.BlockSpec((None, 4, d), lambda i: (layer, 0, 0))],
        out_specs=pl.BlockSpec((ROW_TILE, d), lambda i: (i, 0)),
        compiler_params=_cparams(("arbitrary",)),
        name="ffn_residual",
    )(xn, ffn, mod, norms)


def _dispatch_plan(route, counts, r_max):
    t = route.shape[0]
    idx = route[:, 0:TOP_K].astype(jnp.int32)
    wts = route[:, TOP_K:2 * TOP_K]
    rank = route[:, 2 * TOP_K:3 * TOP_K].astype(jnp.int32)
    cnt = counts.astype(jnp.int32)
    padded = (cnt + MOE_TILE - 1) // MOE_TILE * MOE_TILE
    ends = jnp.cumsum(padded)
    pos = (ends - padded)[idx] + rank
    n_tiles = r_max // MOE_TILE
    n_used = ends[-1] // MOE_TILE
    tile_ids = jnp.minimum(jnp.arange(n_tiles, dtype=jnp.int32), n_used - 1)
    tile_expert = jnp.sum((ends // MOE_TILE)[None, :] <= tile_ids[:, None], axis=1).astype(jnp.int32)
    tile_first = jnp.concatenate(
        [jnp.ones((1,), jnp.int32), (tile_expert[1:] != tile_expert[:-1]).astype(jnp.int32)])
    first_tile = ((ends - padded) // MOE_TILE)[tile_expert]
    tile_valid = jnp.clip(cnt[tile_expert] - (tile_ids - first_tile) * MOE_TILE, 0, MOE_TILE).astype(jnp.int32)
    return pos, wts, tile_expert, tile_first, tile_valid, n_used.reshape(1).astype(jnp.int32)


def _rope_tables(length, n_ctx_rows):
    rows = length // GRID_W
    row = jnp.repeat(jnp.arange(rows, dtype=F32), GRID_W)
    col = jnp.tile(jnp.arange(GRID_W, dtype=F32), rows)
    n_freq = C_DQK // 4
    inv_freq = ROPE_THETA ** (-jnp.arange(n_freq, dtype=F32) / n_freq)
    ang_r = row[:, None] * inv_freq
    ang_c = col[:, None] * inv_freq
    cos = jnp.concatenate([jnp.cos(ang_r), jnp.cos(ang_r), jnp.cos(ang_c), jnp.cos(ang_c)], axis=-1)
    sin = jnp.concatenate([-jnp.sin(ang_r), jnp.sin(ang_r), -jnp.sin(ang_c), jnp.sin(ang_c)], axis=-1)
    reps = C_W // C_DQK
    cos = jnp.concatenate([jnp.ones((n_ctx_rows, C_W), F32), jnp.tile(cos, (1, reps))], axis=0)
    sin = jnp.concatenate([jnp.zeros((n_ctx_rows, C_W), F32), jnp.tile(sin, (1, reps))], axis=0)
    return cos, sin


def _hgrn_lower_bound(table, layer):
    p = jax.nn.softmax(table.astype(F32), axis=1)
    cum = jnp.cumsum(p, axis=1) - p[:, :1]
    return jnp.clip(cum[:, layer], 0.0, 1.0)


def kernel(x, c, ctx, c_ctx, ada_w, ada_b, sandwich_norms, w_in, w_out, hgrn_lower_bounds, hgrn_norm,
           mlstm_gate_bias, mlstm_norm, diff_lambdas, diff_norm, router_w, router_b, moe_w1, moe_b1,
           moe_w2, moe_b2):
    n_batch, seq, d = x.shape
    n_ctx_rows = ctx.shape[1]
    depth = w_in.shape[0]
    n_exp = router_w.shape[2]
    assert seq % ROW_TILE == 0 and n_ctx_rows % ROW_TILE == 0 and seq % GRID_W == 0
    rows_b = n_ctx_rows + seq
    tiles_b = rows_b // ROW_TILE
    ctx_tiles = n_ctx_rows // ROW_TILE
    lat_tiles = seq // ROW_TILE

    b_main = PA_W + 2 * HEADS * B_QK + 2 * B_W
    w_in_p = jnp.concatenate(
        [w_in[:, :, :b_main],
         jnp.pad(w_in[:, :, b_main:b_main + N_GATE], ((0, 0), (0, 0), (0, LANES - N_GATE))),
         w_in[:, :, b_main + N_GATE:]], axis=2).astype(BF16)
    w_out_b = w_out.astype(BF16)
    two_f = moe_b1.shape[2]
    b1p = moe_b1.reshape(depth, n_exp, two_f // PAIR_BLOCK, LANES, 2).transpose(0, 1, 2, 4, 3).reshape(
        depth, n_exp, 1, two_f)
    b2 = moe_b2[:, :, None, :]
    ada_b3 = ada_b[:, None, :]
    router_b3 = router_b[:, :, None]
    rw_t = router_w.transpose(0, 2, 1)
    rw_hi = rw_t.astype(BF16)
    router_wt = jnp.stack([rw_hi, (rw_t - rw_hi.astype(F32)).astype(BF16)], axis=1)
    gate_bias = jnp.pad(mlstm_gate_bias, ((0, 0), (0, LANES - N_GATE)))
    cos, sin = _rope_tables(seq, n_ctx_rows)

    cond_rows = (n_batch + 1 + 7) // 8 * 8
    cvec = jnp.zeros((cond_rows, d), F32).at[:n_batch].set(c).at[n_batch].set(c_ctx)

    def mod_row_all(i):
        return jnp.where(i % tiles_b < ctx_tiles, n_batch, i // tiles_b)

    xy = jnp.concatenate([ctx, x], axis=1).reshape(n_batch * rows_b, d)

    for layer in range(depth):
        last = layer == depth - 1
        mod = _adaln(cvec, ada_w, ada_b3, layer).reshape(cond_rows, 1, 6 * d)
        lb = _hgrn_lower_bound(hgrn_lower_bounds, layer)
        lam_init = 0.8 - 0.6 * math.exp(-0.3 * layer)
        lq1, lk1, lq2, lk2 = diff_lambdas[layer].astype(F32)
        lam = (jnp.exp(jnp.sum(lq1 * lk1)) - jnp.exp(jnp.sum(lq2 * lk2)) + lam_init).reshape(1)

        pa, pb, pc = _inproj(xy, mod, sandwich_norms, w_in_p, cos, sin, layer, tiles_b, ctx_tiles, n_batch)
        oa = _hgrn(pa.reshape(n_batch, rows_b, PA_W), lb, jnp.tile(hgrn_norm[layer], HEADS)[None, :], n_ctx_rows)
        ob = _mlstm(pb.reshape(n_batch, rows_b, PB_W), gate_bias[layer][None, :], mlstm_norm[layer][None, :],
                    n_ctx_rows)
        oc = _attention(pc.reshape(n_batch, rows_b, PC_OUT), lam, diff_norm[layer][None, :], n_ctx_rows,
                        lam_init, not last)

        if last:
            n_tiles = n_batch * lat_tiles
            in_tile = lambda i: (i // lat_tiles) * tiles_b + ctx_tiles + i % lat_tiles
            mod_row = lambda i: i // lat_tiles
        else:
            n_tiles = n_batch * tiles_b
            in_tile = lambda i: i
            mod_row = mod_row_all
        xn, h, route, counts = _outproj(
            oa.reshape(-1, A_W), ob.reshape(-1, B_W), oc.reshape(-1, C_W), xy, mod, sandwich_norms, w_out_b,
            router_wt, router_b3, layer, n_tiles, in_tile, lambda i: i, mod_row)

        r_max = n_tiles * ROW_TILE * TOP_K + n_exp * MOE_TILE
        pos, wts, tile_expert, tile_first, tile_valid, n_used = _dispatch_plan(route, counts, r_max)
        xs = _dispatch_rows(h, pos, r_max)
        ys = _moe(tile_expert, tile_first, tile_valid, n_used, xs, moe_w1, b1p, moe_w2, b2, layer)
        ffn = sum(wts[:, k:k + 1] * jnp.take(ys, pos[:, k], axis=0).astype(F32) for k in range(TOP_K))
        xy = _ffn_residual(xn, ffn, mod, sandwich_norms, layer, mod_row)

    return xy.reshape(n_batch, seq, d)
```

```python
import functools
import math

import jax
import jax.numpy as jnp
from jax import lax
from jax.experimental import pallas as pl
from jax.experimental.pallas import tpu as pltpu
from jax.experimental.pallas import tpu_sc as plsc

F32 = jnp.float32
BF16 = jnp.bfloat16
HI = lax.Precision.HIGHEST

HEADS = 4
A_W = 256
B_QK = 32
B_W = 256
C_DQK = 64
C_DV = 2 * C_DQK
C_W = 512
HEAD_V = 64
N_GATE = 16
GRID_W = 64
TOP_K = 4
SWIGLU_ALPHA = 1.702
SWIGLU_LIMIT = 7.0
ROPE_THETA = 10000.0
NORM_EPS = 1e-6
MASK_NEG = -1e30
F_MIN = 1e-12

LANES = 128
ROW_TILE = 256
CHUNK = 64
MOE_TILE = 256
VMEM_LIMIT = 56 * 1024 * 1024

PA_W = 5 * A_W
PB_W = 2 * HEADS * B_QK + 2 * B_W + LANES
PC_IN = 3 * C_W
PC_OUT = 4 * C_W
W_IN_PAD = PA_W + PB_W + PC_IN


def _cparams(sem, flags=None):
    return pltpu.CompilerParams(dimension_semantics=sem, vmem_limit_bytes=VMEM_LIMIT, flags=flags)


def _nt(a, b):
    return lax.dot_general(a, b, (((1,), (1,)), ((), ())), preferred_element_type=F32)


def _tn(a, b, precision=None):
    return lax.dot_general(a, b, (((0,), (0,)), ((), ())), preferred_element_type=F32, precision=precision)


def _rms(x):
    return x * lax.rsqrt(jnp.mean(x * x, axis=-1, keepdims=True) + NORM_EPS)


def _silu(x):
    return x * jax.nn.sigmoid(x)


def _adaln_kernel(c_ref, w_ref, b_ref, o_ref):
    cond = _silu(c_ref[...])
    o_ref[...] = jnp.dot(cond, w_ref[...], preferred_element_type=F32, precision=HI) + b_ref[...]


def _adaln(cvec, ada_w, ada_b, layer):
    rows, d = cvec.shape
    return pl.pallas_call(
        _adaln_kernel,
        out_shape=jax.ShapeDtypeStruct((rows, 6 * d), F32),
        grid=(6,),
        in_specs=[pl.BlockSpec((rows, d), lambda j: (0, 0)),
                  pl.BlockSpec((None, d, d), lambda j: (layer, 0, j)),
                  pl.BlockSpec((None, 1, d), lambda j: (layer, 0, j))],
        out_specs=pl.BlockSpec((rows, d), lambda j: (0, j)),
        compiler_params=_cparams(("arbitrary",)),
        name="adaln",
    )(cvec, ada_w, ada_b)


def _inproj_kernel(x_ref, mod_ref, g_ref, w_ref, cos_ref, sin_ref, pa_ref, pb_ref, pc_ref):
    d = x_ref.shape[1]
    shift = mod_ref[0, :, 0:d]
    scale = mod_ref[0, :, d:2 * d]
    h = _rms(x_ref[...]) * g_ref[0:1, :] * (1.0 + scale) + shift
    hb = h.astype(BF16)
    pa_ref[...] = jnp.dot(hb, w_ref[:, 0:PA_W], preferred_element_type=F32)
    pb_ref[...] = jnp.dot(hb, w_ref[:, PA_W:PA_W + PB_W], preferred_element_type=F32)
    pc = jnp.dot(hb, w_ref[:, PA_W + PB_W:W_IN_PAD], preferred_element_type=F32)
    q = pc[:, 0:C_W]
    k = pc[:, C_W:2 * C_W]
    cos = cos_ref[...]
    sin = sin_ref[...]
    lane = lax.broadcasted_iota(jnp.int32, q.shape, 1)
    first = (lane % 32) < 16

    def rope(t):
        partner = jnp.where(first, pltpu.roll(t, C_W - 16, 1), pltpu.roll(t, 16, 1))
        return t * cos + partner * sin

    pc_ref[:, 0:C_W] = q.astype(BF16)
    pc_ref[:, C_W:2 * C_W] = rope(q).astype(BF16)
    pc_ref[:, 2 * C_W:3 * C_W] = rope(k).astype(BF16)
    pc_ref[:, 3 * C_W:4 * C_W] = pc[:, 2 * C_W:3 * C_W].astype(BF16)


def _inproj(xy, mod, norms, w_in_p, cos, sin, layer, tiles_per_batch, n_ctx_tiles, n_batch):
    t_all, d = xy.shape
    n_tiles = t_all // ROW_TILE

    def mod_row(i):
        return jnp.where(i % tiles_per_batch < n_ctx_tiles, n_batch, i // tiles_per_batch)

    return pl.pallas_call(
        _inproj_kernel,
        out_shape=(jax.ShapeDtypeStruct((t_all, PA_W), F32),
                   jax.ShapeDtypeStruct((t_all, PB_W), F32),
                   jax.ShapeDtypeStruct((t_all, PC_OUT), BF16)),
        grid=(n_tiles,),
        in_specs=[pl.BlockSpec((ROW_TILE, d), lambda i: (i, 0)),
                  pl.BlockSpec((1, 1, 6 * d), lambda i: (mod_row(i), 0, 0)),
                  pl.BlockSpec((None, 4, d), lambda i: (layer, 0, 0)),
                  pl.BlockSpec((None, d, W_IN_PAD), lambda i: (layer, 0, 0)),
                  pl.BlockSpec((ROW_TILE, C_W), lambda i: (i % tiles_per_batch, 0)),
                  pl.BlockSpec((ROW_TILE, C_W), lambda i: (i % tiles_per_batch, 0))],
        out_specs=(pl.BlockSpec((ROW_TILE, PA_W), lambda i: (i, 0)),
                   pl.BlockSpec((ROW_TILE, PB_W), lambda i: (i, 0)),
                   pl.BlockSpec((ROW_TILE, PC_OUT), lambda i: (i, 0))),
        compiler_params=_cparams(("arbitrary",)),
        name="inproj",
    )(xy, mod, norms, w_in_p, cos, sin)


def _hgrn_kernel(pa_ref, lb_ref, gain_ref, o_ref, st_ref, ob_ref, *, n_ctx, n_all):
    c_sz, w = CHUNK, A_W
    r_i = lax.broadcasted_iota(jnp.int32, (c_sz, c_sz), 0)
    c_i = lax.broadcasted_iota(jnp.int32, (c_sz, c_sz), 1)
    tri_incl_past = (c_i <= r_i).astype(BF16)
    tri_incl_future = (c_i >= r_i).astype(BF16)
    same_head = (lax.broadcasted_iota(jnp.int32, (w, w), 0) // HEAD_V
                 == lax.broadcasted_iota(jnp.int32, (w, w), 1) // HEAD_V)
    head_ones = same_head.astype(BF16)
    head_ones_f32 = same_head.astype(F32)
    assert c_sz == HEAD_V
    t_i = lax.broadcasted_iota(jnp.int32, (c_sz, w), 0)
    s_i = lax.broadcasted_iota(jnp.int32, (c_sz, w), 1) % c_sz
    t_row = lax.broadcasted_iota(jnp.int32, (c_sz, 1), 0)
    row8 = t_row % 8

    def grouped_rows(a, k):
        return jnp.concatenate(
            [jnp.broadcast_to(a[8 * j + k:8 * j + k + 1, :], (8, w)) for j in range(c_sz // 8)], axis=0)

    def halving_levels(rev):
        out = []
        b = c_sz // 2
        while b >= 1:
            def later(i):
                return ((i % (2 * b)) < b) if rev else ((i % (2 * b)) >= b)
            live = jnp.logical_and(t_i // (2 * b) == s_i // (2 * b),
                                   jnp.logical_and(later(t_i), jnp.logical_not(later(s_i))))
            out.append((b, live.astype(F32), jnp.where(later(t_row), 1.0, -1.0)))
            b //= 2
        return out

    self_mask = (s_i == t_i).astype(F32)
    levels_of = (halving_levels(False), halving_levels(True))

    def chunk(c, rev):
        d = 1 if rev else 0
        levels = levels_of[d]
        r0 = pl.multiple_of(c * c_sz, c_sz)
        rows = pl.ds(r0, c_sz)
        q_pre = pa_ref[0, rows, 0:A_W]
        v = pa_ref[0, rows, A_W:2 * A_W]
        f_pre = pa_ref[0, rows, 3 * A_W:4 * A_W] if rev else pa_ref[0, rows, 2 * A_W:3 * A_W]
        lb = lb_ref[1:2, :] if rev else lb_ref[0:1, :]
        q = _silu(q_pre)
        f = lb + (1.0 - lb) * jax.nn.sigmoid(f_pre)
        log_f = jnp.log(jnp.maximum(f, F_MIN))
        kk = (1.0 - lb) * jax.nn.sigmoid(-f_pre)
        tri = tri_incl_future if rev else tri_incl_past
        cum = sum(jnp.dot(tri, piece, preferred_element_type=F32) for piece in _split3(log_f))
        e = 0 if rev else c_sz - 1
        cum_end = cum[e:e + 1, :]
        vb = v.astype(BF16)

        st = st_ref[d]
        o = _nt((q * jnp.exp(cum)).astype(BF16), st.astype(BF16))
        k_end = (kk * jnp.exp(cum_end - cum)).astype(BF16)
        st_ref[d] = st * jnp.exp(cum_end) + jnp.where(same_head, _tn(vb, k_end), 0.0)

        p = self_mask * jnp.dot((q * kk).astype(BF16), head_ones, preferred_element_type=F32)
        for b, live, sign in levels:
            if b >= 8:
                ref = jnp.concatenate(
                    [jnp.broadcast_to(cum[r0 + (b if rev else b - 1):r0 + (b if rev else b - 1) + 1, :], (2 * b, w))
                     for r0 in range(0, c_sz, 2 * b)], axis=0)
            else:
                n_ref = 8 // (2 * b)
                ref = None
                for g in reversed(range(n_ref)):
                    cand = grouped_rows(cum, g * 2 * b + (b if rev else b - 1))
                    ref = cand if ref is None else jnp.where(row8 < (g + 1) * 2 * b, cand, ref)
            z = jnp.exp((cum - ref) * sign)
            qa = (q * z).astype(BF16)
            ka = (kk * z).astype(BF16)
            ka_bd = jnp.where(same_head, jnp.concatenate([ka] * HEADS, axis=0), jnp.zeros((), BF16))
            p = p + live * _nt(qa, ka_bd)
        v_bd = jnp.where(same_head, jnp.concatenate([vb] * HEADS, axis=0), jnp.zeros((), BF16))
        o = o + jnp.dot(p.astype(BF16), v_bd, preferred_element_type=F32)
        if rev:
            ob_ref[rows, :] = o
        else:
            o_ref[0, rows, :] = o

    def scan_step(n, carry):
        chunk(n, False)
        chunk(jnp.where(n < n_ctx, n_ctx - 1 - n, n_all - 1 - (n - n_ctx)), True)
        return carry

    st_ref[...] = jnp.zeros_like(st_ref)
    lax.fori_loop(0, n_all, scan_step, 0)

    def readout(c, carry):
        rows = pl.ds(pl.multiple_of(c * ROW_TILE, ROW_TILE), ROW_TILE)
        tot = o_ref[0, rows, :] + ob_ref[rows, :]
        ms = jnp.dot(tot * tot, head_ones_f32, preferred_element_type=F32, precision=HI) * (1.0 / HEAD_V)
        g = pa_ref[0, rows, 4 * A_W:5 * A_W]
        o_ref[0, rows, :] = tot * lax.rsqrt(ms + NORM_EPS) * gain_ref[...] * _silu(g)
        return carry

    lax.fori_loop(0, n_all * c_sz // ROW_TILE, readout, 0)


def _hgrn(pa3, lb, gain, n_ctx_rows):
    n_batch, rows, _ = pa3.shape
    kern = functools.partial(_hgrn_kernel, n_ctx=n_ctx_rows // CHUNK, n_all=rows // CHUNK)
    return pl.pallas_call(
        kern,
        out_shape=jax.ShapeDtypeStruct((n_batch, rows, A_W), F32),
        grid=(n_batch,),
        in_specs=[pl.BlockSpec((1, rows, PA_W), lambda b: (b, 0, 0)),
                  pl.BlockSpec((2, A_W), lambda b: (0, 0)),
                  pl.BlockSpec((1, A_W), lambda b: (0, 0))],
        out_specs=pl.BlockSpec((1, rows, A_W), lambda b: (b, 0, 0)),
        scratch_shapes=[pltpu.VMEM((2, A_W, A_W), F32), pltpu.VMEM((rows, A_W), F32)],
        compiler_params=_cparams(("arbitrary",)),
        name="hgrn2",
    )(pa3, lb, gain)


def _split3(x):
    hi = x.astype(BF16)
    rest = x - hi.astype(F32)
    mid = rest.astype(BF16)
    return hi, mid, (rest - mid.astype(F32)).astype(BF16)


def _mlstm_kernel(pb_ref, bias_ref, gain_ref, o_ref, cn_ref, ob_ref, *, n_ctx, n_all):
    assert CHUNK == HEAD_V
    c_sz, w, qk_w = CHUNK, B_W, HEADS * B_QK
    t_i = lax.broadcasted_iota(jnp.int32, (c_sz, w), 0)
    s_i = lax.broadcasted_iota(jnp.int32, (c_sz, w), 1) % c_sz
    r_i = lax.broadcasted_iota(jnp.int32, (c_sz, c_sz), 0)
    c_i = lax.broadcasted_iota(jnp.int32, (c_sz, c_sz), 1)
    diag4 = s_i == t_i
    half_lane = lax.broadcasted_iota(jnp.int32, (1, LANES), 1) < HEAD_V
    k_block = (lax.broadcasted_iota(jnp.int32, (HEADS * c_sz, qk_w), 0) // c_sz
               == lax.broadcasted_iota(jnp.int32, (HEADS * c_sz, qk_w), 1) // B_QK)
    v_block = (lax.broadcasted_iota(jnp.int32, (HEADS * c_sz, w), 0) // c_sz
               == lax.broadcasted_iota(jnp.int32, (HEADS * c_sz, w), 1) // HEAD_V)
    state_block = (lax.broadcasted_iota(jnp.int32, (qk_w, 2 * w), 0) // B_QK
                   == (lax.broadcasted_iota(jnp.int32, (qk_w, 2 * w), 1) % w) // HEAD_V)
    head_ones = v_block.astype(BF16)
    q_off, k_off, v_off, o_off, g_off = 0, qk_w, 2 * qk_w, 2 * qk_w + B_W, 2 * qk_w + 2 * B_W

    def selector(rev):
        sr = lax.broadcasted_iota(jnp.int32, (2 * LANES, 2 * w), 0)
        sc = lax.broadcasted_iota(jnp.int32, (2 * LANES, 2 * w), 1)
        i_base = 2 * HEADS if rev else 0
        want = jnp.where(sc < w, i_base + HEADS + sc // HEAD_V, LANES + i_base + (sc - w) // HEAD_V)
        return (sr == want).astype(BF16)

    def head_max(x):
        outs = []
        for col in range(w // LANES):
            xc = x[:, col * LANES:(col + 1) * LANES]
            lo = jnp.max(jnp.where(half_lane, xc, -jnp.inf), axis=-1, keepdims=True)
            hi = jnp.max(jnp.where(half_lane, -jnp.inf, xc), axis=-1, keepdims=True)
            outs.append(jnp.where(half_lane, lo, hi))
        return jnp.concatenate(outs, axis=1)

    def chunk(c, rev, m_prev, sel):
        d = 1 if rev else 0
        rows = pl.ds(pl.multiple_of(c * c_sz, c_sz), c_sz)
        qb = (pb_ref[0, rows, q_off:q_off + qk_w] * (B_QK ** -0.5)).astype(BF16)
        kb = pb_ref[0, rows, k_off:k_off + qk_w].astype(BF16)
        v = pb_ref[0, rows, v_off:v_off + B_W]
        gts = pb_ref[0, rows, g_off:g_off + LANES] + bias_ref[...]
        log_f = jnp.minimum(gts, 0.0) - jnp.log(1.0 + jnp.exp(-jnp.abs(gts)))
        mask = (s_i >= t_i) if rev else (s_i <= t_i)
        tri = ((c_i >= r_i) if rev else (c_i <= r_i)).astype(BF16)
        cum_col = sum(jnp.dot(tri, p, preferred_element_type=F32) for p in _split3(log_f))
        both = sum(jnp.dot(p, sel, preferred_element_type=F32)
                   for p in _split3(jnp.concatenate([cum_col, gts], axis=1)))
        cum_t = both[:, :w]
        ig_t = both[:, w:]
        src = jnp.sum(jnp.where(diag4, cum_t - ig_t, 0.0), axis=0, keepdims=True)
        log_d = jnp.where(mask, cum_t - src, MASK_NEG)
        log_inter = cum_t + m_prev
        m_t = jnp.maximum(log_inter, head_max(log_d))
        w_intra = jnp.where(mask, jnp.exp(log_d - m_t), 0.0)
        w_inter = jnp.exp(log_inter - m_t)

        k_bd = jnp.where(k_block, jnp.concatenate([kb] * HEADS, axis=0), jnp.zeros((), BF16))
        p = (_nt(qb, k_bd) * w_intra).astype(BF16)
        vb = v.astype(BF16)
        v_bd = jnp.where(v_block, jnp.concatenate([vb] * HEADS, axis=0), jnp.zeros((), BF16))
        cn = cn_ref[d]
        inter = jnp.dot(qb, cn.astype(BF16), preferred_element_type=F32)
        num = w_inter * inter[:, :w] + jnp.dot(p, v_bd, preferred_element_type=F32)
        den = w_inter * inter[:, w:] + jnp.dot(p, head_ones, preferred_element_type=F32)
        h_out = num / jnp.maximum(jnp.abs(den), jnp.exp(-m_t))

        e = 0 if rev else c_sz - 1
        cum_e = cum_t[e:e + 1, :]
        log_end = cum_e - cum_t + ig_t
        m_end = jnp.maximum(cum_e + m_prev, jnp.max(log_end, axis=0, keepdims=True))
        w_end = jnp.exp(log_end - m_end)
        w_carry = jnp.exp(cum_e + m_prev - m_end)
        upd = _tn(kb, jnp.concatenate([(w_end * v).astype(BF16), w_end.astype(BF16)], axis=1))
        cn_ref[d] = jnp.concatenate([w_carry, w_carry], axis=1) * cn + jnp.where(state_block, upd, 0.0)
        if rev:
            ob_ref[rows, :] = h_out
        else:
            o_ref[0, rows, :] = h_out
        return m_end

    sel_fwd = selector(False)
    sel_bwd = selector(True)

    def scan_step(n, carry):
        m_fwd, m_bwd = carry
        m_fwd = chunk(n, False, m_fwd, sel_fwd)
        m_bwd = chunk(jnp.where(n < n_ctx, n_ctx - 1 - n, n_all - 1 - (n - n_ctx)), True, m_bwd, sel_bwd)
        return m_fwd, m_bwd

    zero_m = jnp.zeros((1, w), F32)
    cn_ref[...] = jnp.zeros_like(cn_ref)
    lax.fori_loop(0, n_all, scan_step, (zero_m, zero_m), unroll=2)

    def readout(c, carry):
        rows = pl.ds(pl.multiple_of(c * ROW_TILE, ROW_TILE), ROW_TILE)
        tot = o_ref[0, rows, :] + ob_ref[rows, :]
        normed = jnp.concatenate(
            [_rms(tot[:, h * HEAD_V:(h + 1) * HEAD_V]) for h in range(HEADS)], axis=1) * gain_ref[...]
        og = pb_ref[0, rows, o_off:o_off + B_W]
        o_ref[0, rows, :] = normed * jax.nn.sigmoid(og)
        return carry

    lax.fori_loop(0, n_all * c_sz // ROW_TILE, readout, 0)


def _mlstm(pb3, bias, gain, n_ctx_rows):
    n_batch, rows, _ = pb3.shape
    kern = functools.partial(_mlstm_kernel, n_ctx=n_ctx_rows // CHUNK, n_all=rows // CHUNK)
    return pl.pallas_call(
        kern,
        out_shape=jax.ShapeDtypeStruct((n_batch, rows, B_W), F32),
        grid=(n_batch,),
        in_specs=[pl.BlockSpec((1, rows, PB_W), lambda b: (b, 0, 0)),
                  pl.BlockSpec((1, LANES), lambda b: (0, 0)),
                  pl.BlockSpec((1, B_W), lambda b: (0, 0))],
        out_specs=pl.BlockSpec((1, rows, B_W), lambda b: (b, 0, 0)),
        scratch_shapes=[pltpu.VMEM((2, HEADS * B_QK, 2 * B_W), F32), pltpu.VMEM((rows, B_W), F32)],
        compiler_params=_cparams(("arbitrary",)),
        name="mlstm",
    )(pb3, bias, gain)


def _attn_kernel(lam_ref, qp_ref, qr_ref, k_ref, v_ref, gain_ref, o_ref, *, n_ctx, q_tile0, lam_init):
    lam = lam_ref[0]
    q_tile = pl.program_id(2) + q_tile0
    lane = lax.broadcasted_iota(jnp.int32, (1, 2 * C_DQK), 1)
    scale = C_DQK ** -0.5

    def finish(parts):
        o = parts[0] - lam * parts[1]
        o_ref[0] = _rms(o) * gain_ref[...] * (1.0 - lam_init)

    def sub_query(q, j):
        return jnp.where(lane // C_DQK == j, q * scale, jnp.zeros_like(q))

    def row_max(s):
        return jnp.max(s, axis=-1, keepdims=True)

    def row_sum(s):
        return jnp.sum(s, axis=-1, keepdims=True)

    def pv(ex, v):
        return jnp.dot(ex.astype(BF16), v, preferred_element_type=F32)

    @pl.when(q_tile * ROW_TILE < n_ctx)
    def _():
        kc = k_ref[0, 0:n_ctx, :]
        vc = v_ref[0, 0:n_ctx, :]
        parts = []
        for j in range(2):
            s = _nt(sub_query(qp_ref[0], j), kc)
            ex = jnp.exp(s - row_max(s))
            parts.append(pv(ex, vc) / row_sum(ex))
        finish(parts)

    @pl.when(q_tile * ROW_TILE >= n_ctx)
    def _():
        n_all = k_ref.shape[1]
        kc = k_ref[0, 0:n_ctx, :]
        kl = k_ref[0, n_ctx:n_all, :]
        vc = v_ref[0, 0:n_ctx, :]
        vl = v_ref[0, n_ctx:n_all, :]
        parts = []
        for j in range(2):
            s_c = _nt(sub_query(qp_ref[0], j), kc)
            s_l = _nt(sub_query(qr_ref[0], j), kl)
            m = jnp.maximum(row_max(s_c), row_max(s_l))
            e_c = jnp.exp(s_c - m)
            e_l = jnp.exp(s_l - m)
            parts.append((pv(e_c, vc) + pv(e_l, vl)) / (row_sum(e_c) + row_sum(e_l)))
        finish(parts)


def _attention(pc3, lam, gain, n_ctx_rows, lam_init, with_ctx):
    n_batch, rows, _ = pc3.shape
    q_tile0 = 0 if with_ctx else n_ctx_rows // ROW_TILE
    n_q = rows // ROW_TILE - q_tile0
    hb = C_W // LANES
    kern = functools.partial(_attn_kernel, n_ctx=n_ctx_rows, q_tile0=q_tile0, lam_init=lam_init)
    grid_spec = pltpu.PrefetchScalarGridSpec(
        num_scalar_prefetch=1,
        grid=(n_batch, HEADS, n_q),
        in_specs=[pl.BlockSpec((1, ROW_TILE, LANES), lambda b, h, i, lam: (b, i + q_tile0, h)),
                  pl.BlockSpec((1, ROW_TILE, LANES), lambda b, h, i, lam: (b, i + q_tile0, hb + h)),
                  pl.BlockSpec((1, rows, LANES), lambda b, h, i, lam: (b, 0, 2 * hb + h)),
                  pl.BlockSpec((1, rows, LANES), lambda b, h, i, lam: (b, 0, 3 * hb + h)),
                  pl.BlockSpec((1, LANES), lambda b, h, i, lam: (0, 0))],
        out_specs=pl.BlockSpec((1, ROW_TILE, LANES), lambda b, h, i, lam: (b, i, h)),
    )
    return pl.pallas_call(
        kern,
        out_shape=jax.ShapeDtypeStruct((n_batch, n_q * ROW_TILE, C_W), F32),
        grid_spec=grid_spec,
        compiler_params=_cparams(("arbitrary", "arbitrary", "arbitrary")),
        name="diff_attn",
    )(lam, pc3, pc3, pc3, pc3, gain)


def _outproj_kernel(oa_ref, ob_ref, oc_ref, x_ref, mod_ref, n_ref, w_ref, rw_ref, rb_ref,
                    xn_ref, h_ref, route_ref, cnt_ref, carry_ref):
    d = x_ref.shape[1]
    n_exp = rw_ref.shape[1]

    @pl.when(pl.program_id(0) == 0)
    def _():
        carry_ref[...] = jnp.zeros_like(carry_ref)

    mix = (jnp.dot(oa_ref[...].astype(BF16), w_ref[0:A_W, :], preferred_element_type=F32)
           + jnp.dot(ob_ref[...].astype(BF16), w_ref[A_W:A_W + B_W, :], preferred_element_type=F32)
           + jnp.dot(oc_ref[...].astype(BF16), w_ref[A_W + B_W:A_W + B_W + C_W, :], preferred_element_type=F32))
    xn = x_ref[...] + mod_ref[0, :, 2 * d:3 * d] * (_rms(mix) * n_ref[1:2, :])
    xn_ref[...] = xn
    h = _rms(xn) * n_ref[2:3, :] * (1.0 + mod_ref[0, :, 4 * d:5 * d]) + mod_ref[0, :, 3 * d:4 * d]
    bits = pltpu.bitcast(h.astype(BF16).astype(F32), jnp.uint32)
    words = (bits[:, :d // 2] >> 16) | (bits[:, d // 2:] & jnp.uint32(0xFFFF0000))
    h_ref[...] = pltpu.bitcast(words, jnp.int32)

    h_hi = h.astype(BF16)
    h_mid = (h - h_hi.astype(F32)).astype(BF16)
    logits = _nt(rw_ref[0], h_hi) + _nt(rw_ref[1], h_hi) + _nt(rw_ref[0], h_mid) + rb_ref[...]
    e_sub = lax.broadcasted_iota(jnp.int32, logits.shape, 0)
    cur = logits
    picks, vals = [], []
    for _ in range(TOP_K):
        mx = jnp.max(cur, axis=0, keepdims=True)
        idx = jnp.min(jnp.where(cur == mx, e_sub, n_exp), axis=0, keepdims=True)
        hit = e_sub == idx
        cur = jnp.where(hit, -jnp.inf, cur)
        picks.append((idx, hit.astype(F32)))
        vals.append(mx)
    exps = [jnp.exp(vv - vals[0]) for vv in vals]
    total = exps[0] + exps[1] + exps[2] + exps[3]

    chosen = picks[0][1] + picks[1][1] + picks[2][1] + picks[3][1]
    tm = logits.shape[1]
    before = (lax.broadcasted_iota(jnp.int32, (tm, tm), 0) < lax.broadcasted_iota(jnp.int32, (tm, tm), 1))
    seen = jnp.dot(chosen.astype(BF16), before.astype(BF16), preferred_element_type=F32) + carry_ref[...]
    new_carry = carry_ref[...] + jnp.sum(chosen, axis=1, keepdims=True)
    carry_ref[...] = new_carry
    cnt_ref[...] = new_carry

    o_row = lax.broadcasted_iota(jnp.int32, route_ref.shape, 0)
    route = jnp.zeros(route_ref.shape, F32)
    for kk in range(TOP_K):
        idx, hit = picks[kk]
        rank = jnp.sum(hit * seen, axis=0, keepdims=True)
        route = (route + jnp.where(o_row == kk, idx.astype(F32), 0.0)
                 + jnp.where(o_row == TOP_K + kk, exps[kk] / total, 0.0)
                 + jnp.where(o_row == 2 * TOP_K + kk, rank, 0.0))
    route_ref[...] = route


ROUTE_ROWS = 16


def _outproj(oa, ob, oc, xy, mod, norms, w_out_b, router_wt, router_b, layer, n_tiles, in_tile, oc_tile, mod_row):
    d = xy.shape[1]
    n_exp = router_wt.shape[2]
    t_out = n_tiles * ROW_TILE
    xn, h, route_t, counts = pl.pallas_call(
        _outproj_kernel,
        out_shape=(jax.ShapeDtypeStruct((t_out, d), F32),
                   jax.ShapeDtypeStruct((t_out, d // 2), jnp.int32),
                   jax.ShapeDtypeStruct((n_tiles * ROUTE_ROWS, ROW_TILE), F32),
                   jax.ShapeDtypeStruct((n_exp, 1), F32)),
        grid=(n_tiles,),
        in_specs=[pl.BlockSpec((ROW_TILE, A_W), lambda i: (in_tile(i), 0)),
                  pl.BlockSpec((ROW_TILE, B_W), lambda i: (in_tile(i), 0)),
                  pl.BlockSpec((ROW_TILE, C_W), lambda i: (oc_tile(i), 0)),
                  pl.BlockSpec((ROW_TILE, d), lambda i: (in_tile(i), 0)),
                  pl.BlockSpec((1, 1, 6 * d), lambda i: (mod_row(i), 0, 0)),
                  pl.BlockSpec((None, 4, d), lambda i: (layer, 0, 0)),
                  pl.BlockSpec((None, d, d), lambda i: (layer, 0, 0)),
                  pl.BlockSpec((None, 2, n_exp, d), lambda i: (layer, 0, 0, 0)),
                  pl.BlockSpec((None, n_exp, 1), lambda i: (layer, 0, 0))],
        out_specs=(pl.BlockSpec((ROW_TILE, d), lambda i: (i, 0)),
                   pl.BlockSpec((ROW_TILE, d // 2), lambda i: (i, 0)),
                   pl.BlockSpec((ROUTE_ROWS, ROW_TILE), lambda i: (i, 0)),
                   pl.BlockSpec((n_exp, 1), lambda i: (0, 0))),
        scratch_shapes=[pltpu.VMEM((n_exp, 1), F32)],
        compiler_params=_cparams(("arbitrary",)),
        name="outproj_router",
    )(oa, ob, oc, xy, mod, norms, w_out_b, router_wt, router_b)
    route = route_t.reshape(n_tiles, ROUTE_ROWS, ROW_TILE).transpose(0, 2, 1).reshape(t_out, ROUTE_ROWS)
    return xn, h, route, counts[:, 0]


PAIR_BLOCK = 2 * LANES


def _moe_kernel(te_ref, first_ref, valid_ref, nu_ref, xs_ref, w1_ref, b1_ref, w2_ref, b2_ref, ys_ref,
                w1p_ref, w2b_ref):
    i = pl.program_id(0)
    two_f = w1_ref.shape[1]
    n_blk = two_f // PAIR_BLOCK

    @pl.when(jnp.logical_and(i < nu_ref[0], first_ref[i] == 1))
    def _():
        r = lax.broadcasted_iota(jnp.int32, (PAIR_BLOCK, PAIR_BLOCK), 0)
        c = lax.broadcasted_iota(jnp.int32, (PAIR_BLOCK, PAIR_BLOCK), 1)
        perm = (r == jnp.where(c < LANES, 2 * c, 2 * (c - LANES) + 1)).astype(BF16)
        for blk in range(n_blk):
            cols = slice(blk * PAIR_BLOCK, (blk + 1) * PAIR_BLOCK)
            w1p_ref[:, cols] = jnp.dot(w1_ref[:, cols].astype(BF16), perm,
                                       preferred_element_type=F32).astype(BF16)
        w2b_ref[...] = w2_ref[...].astype(BF16)

    @pl.when(i < nu_ref[0])
    def _():
        words = pltpu.bitcast(xs_ref[...], jnp.uint32)
        x = jnp.concatenate([pltpu.bitcast(words << 16, F32),
                             pltpu.bitcast(words & jnp.uint32(0xFFFF0000), F32)], axis=1).astype(BF16)
        row = lax.broadcasted_iota(jnp.int32, (x.shape[0], 1), 0)
        x = jnp.where(row < valid_ref[i], x, jnp.zeros_like(x))
        hid = jnp.dot(x, w1p_ref[...], preferred_element_type=F32) + b1_ref[...]
        acts = []
        for blk in range(n_blk):
            glu = jnp.minimum(hid[:, blk * PAIR_BLOCK:blk * PAIR_BLOCK + LANES], SWIGLU_LIMIT)
            lin = jnp.clip(hid[:, blk * PAIR_BLOCK + LANES:(blk + 1) * PAIR_BLOCK], -SWIGLU_LIMIT, SWIGLU_LIMIT)
            acts.append((glu * jax.nn.sigmoid(SWIGLU_ALPHA * glu) * (lin + 1.0)).astype(BF16))
        y = jnp.dot(jnp.concatenate(acts, axis=1), w2b_ref[...], preferred_element_type=F32) + b2_ref[...]
        ys_ref[...] = y.astype(ys_ref.dtype)

    @pl.when(i >= nu_ref[0])
    def _():
        ys_ref[...] = jnp.zeros_like(ys_ref)


def _moe(tile_expert, tile_first, tile_valid, n_used, xs, w1, b1p, w2, b2, layer):
    r_max, half_d = xs.shape
    d = 2 * half_d
    two_f = w1.shape[3]
    f = two_f // 2
    n_tiles = r_max // MOE_TILE

    def row_tile(i, te, fi, va, nu):
        return (jnp.maximum(jnp.minimum(i, nu[0] - 1), 0), 0)

    grid_spec = pltpu.PrefetchScalarGridSpec(
        num_scalar_prefetch=4,
        grid=(n_tiles,),
        in_specs=[pl.BlockSpec((MOE_TILE, half_d), row_tile),
                  pl.BlockSpec((None, None, d, two_f), lambda i, te, fi, va, nu: (layer, te[i], 0, 0)),
                  pl.BlockSpec((None, None, 1, two_f), lambda i, te, fi, va, nu: (layer, te[i], 0, 0)),
                  pl.BlockSpec((None, None, f, d), lambda i, te, fi, va, nu: (layer, te[i], 0, 0)),
                  pl.BlockSpec((None, None, 1, d), lambda i, te, fi, va, nu: (layer, te[i], 0, 0))],
        out_specs=pl.BlockSpec((MOE_TILE, d), lambda i, te, fi, va, nu: (i, 0)),
        scratch_shapes=[pltpu.VMEM((d, two_f), BF16), pltpu.VMEM((f, d), BF16)],
    )
    return pl.pallas_call(
        _moe_kernel,
        out_shape=jax.ShapeDtypeStruct((r_max, d), BF16),
        grid_spec=grid_spec,
        compiler_params=_cparams(("arbitrary",)),
        name="moe_experts",
    )(tile_expert, tile_first, tile_valid, n_used, xs, w1, b1p, w2, b2)


SC_CORES = 2
SC_SUBCORES = 16
SC_CHUNK = 64


def _dispatch_rows(hp, pos, r_max):
    t, width = hp.shape
    workers = SC_CORES * SC_SUBCORES
    assert t % (workers * SC_CHUNK) == 0
    per_worker = t // (workers * SC_CHUNK)
    idx = pos.reshape(workers, per_worker, SC_CHUNK, TOP_K).transpose(0, 1, 3, 2).reshape(
        workers, per_worker * TOP_K, SC_CHUNK)
    mesh = plsc.VectorSubcoreMesh(core_axis_name="c", subcore_axis_name="s",
                                  num_cores=SC_CORES, num_subcores=SC_SUBCORES)

    @functools.partial(
        pl.kernel, mesh=mesh,
        out_type=jax.ShapeDtypeStruct((r_max, width), hp.dtype),
        scratch_types=[pltpu.VMEM((per_worker * TOP_K, SC_CHUNK), jnp.int32),
                       pltpu.VMEM((SC_CHUNK, width), hp.dtype)],
    )
    def scatter(hp_hbm, idx_hbm, out_hbm, idx_v, rows_v):
        wid = lax.axis_index("s") * SC_CORES + lax.axis_index("c")
        pltpu.sync_copy(idx_hbm.at[wid], idx_v)

        @pl.loop(0, per_worker)
        def _(j):
            row0 = pl.multiple_of((wid * per_worker + j) * SC_CHUNK, SC_CHUNK)
            pltpu.sync_copy(hp_hbm.at[pl.ds(row0, SC_CHUNK)], rows_v)
            for k in range(TOP_K):
                pltpu.sync_copy(rows_v, out_hbm.at[idx_v.at[j * TOP_K + k]])

    return scatter(hp, idx)


def _ffn_residual_kernel(x_ref, w_ref, y0_ref, y1_ref, y2_ref, y3_ref, mod_ref, n_ref, o_ref):
    d = x_ref.shape[1]
    wts = w_ref[...]
    ffn = None
    for k, y_ref in enumerate((y0_ref, y1_ref, y2_ref, y3_ref)):
        term = wts[:, k:k + 1] * y_ref[...].astype(F32)
        ffn = term if ffn is None else ffn + term
    o_ref[...] = x_ref[...] + mod_ref[0, :, 5 * d:6 * d] * (_rms(ffn) * n_ref[3:4, :])


def _ffn_residual(xn, wts, ys_by_k, mod, norms, layer, mod_row):
    t, d = xn.shape
    row_spec = pl.BlockSpec((ROW_TILE, d), lambda i: (i, 0))
    return pl.pallas_call(
        _ffn_residual_kernel,
        out_shape=jax.ShapeDtypeStruct((t, d), F32),
        grid=(t // ROW_TILE,),
        in_specs=[row_spec,
                  pl.BlockSpec((ROW_TILE, TOP_K), lambda i: (i, 0)),
                  row_spec, row_spec, row_spec, row_spec,
                  pl.BlockSpec((1, 1, 6 * d), lambda i: (mod_row(i), 0, 0)),
                  pl.BlockSpec((None, 4, d), lambda i: (layer, 0, 0))],
        out_specs=row_spec,
        compiler_params=_cparams(("arbitrary",)),
        name="ffn_residual",
    )(xn, wts, *ys_by_k, mod, norms)


def _dispatch_plan(route, counts, r_max):
    t = route.shape[0]
    idx = route[:, 0:TOP_K].astype(jnp.int32)
    wts = route[:, TOP_K:2 * TOP_K]
    rank = route[:, 2 * TOP_K:3 * TOP_K].astype(jnp.int32)
    cnt = counts.astype(jnp.int32)
    padded = (cnt + MOE_TILE - 1) // MOE_TILE * MOE_TILE
    ends = jnp.cumsum(padded)
    pos = (ends - padded)[idx] + rank
    n_tiles = r_max // MOE_TILE
    n_used = ends[-1] // MOE_TILE
    tile_ids = jnp.minimum(jnp.arange(n_tiles, dtype=jnp.int32), n_used - 1)
    tile_expert = jnp.sum((ends // MOE_TILE)[None, :] <= tile_ids[:, None], axis=1).astype(jnp.int32)
    tile_first = jnp.concatenate(
        [jnp.ones((1,), jnp.int32), (tile_expert[1:] != tile_expert[:-1]).astype(jnp.int32)])
    first_tile = ((ends - padded) // MOE_TILE)[tile_expert]
    tile_valid = jnp.clip(cnt[tile_expert] - (tile_ids - first_tile) * MOE_TILE, 0, MOE_TILE).astype(jnp.int32)
    return pos, wts, tile_expert, tile_first, tile_valid, n_used.reshape(1).astype(jnp.int32)


def _rope_tables(length, n_ctx_rows):
    rows = length // GRID_W
    row = jnp.repeat(jnp.arange(rows, dtype=F32), GRID_W)
    col = jnp.tile(jnp.arange(GRID_W, dtype=F32), rows)
    n_freq = C_DQK // 4
    inv_freq = ROPE_THETA ** (-jnp.arange(n_freq, dtype=F32) / n_freq)
    ang_r = row[:, None] * inv_freq
    ang_c = col[:, None] * inv_freq
    cos = jnp.concatenate([jnp.cos(ang_r), jnp.cos(ang_r), jnp.cos(ang_c), jnp.cos(ang_c)], axis=-1)
    sin = jnp.concatenate([-jnp.sin(ang_r), jnp.sin(ang_r), -jnp.sin(ang_c), jnp.sin(ang_c)], axis=-1)
    reps = C_W // C_DQK
    cos = jnp.concatenate([jnp.ones((n_ctx_rows, C_W), F32), jnp.tile(cos, (1, reps))], axis=0)
    sin = jnp.concatenate([jnp.zeros((n_ctx_rows, C_W), F32), jnp.tile(sin, (1, reps))], axis=0)
    return cos, sin


def _hgrn_lower_bound(table, layer):
    p = jax.nn.softmax(table.astype(F32), axis=1)
    cum = jnp.cumsum(p, axis=1) - p[:, :1]
    return jnp.clip(cum[:, layer], 0.0, 1.0)


def kernel(x, c, ctx, c_ctx, ada_w, ada_b, sandwich_norms, w_in, w_out, hgrn_lower_bounds, hgrn_norm,
           mlstm_gate_bias, mlstm_norm, diff_lambdas, diff_norm, router_w, router_b, moe_w1, moe_b1,
           moe_w2, moe_b2):
    n_batch, seq, d = x.shape
    n_ctx_rows = ctx.shape[1]
    depth = w_in.shape[0]
    n_exp = router_w.shape[2]
    assert seq % ROW_TILE == 0 and n_ctx_rows % ROW_TILE == 0 and seq % GRID_W == 0
    rows_b = n_ctx_rows + seq
    tiles_b = rows_b // ROW_TILE
    ctx_tiles = n_ctx_rows // ROW_TILE
    lat_tiles = seq // ROW_TILE

    b_main = PA_W + 2 * HEADS * B_QK + 2 * B_W
    w_in_p = jnp.concatenate(
        [w_in[:, :, :b_main],
         jnp.pad(w_in[:, :, b_main:b_main + N_GATE], ((0, 0), (0, 0), (0, LANES - N_GATE))),
         w_in[:, :, b_main + N_GATE:]], axis=2).astype(BF16)
    w_out_b = w_out.astype(BF16)
    two_f = moe_b1.shape[2]
    b1p = moe_b1.reshape(depth, n_exp, two_f // PAIR_BLOCK, LANES, 2).transpose(0, 1, 2, 4, 3).reshape(
        depth, n_exp, 1, two_f)
    b2 = moe_b2[:, :, None, :]
    ada_b3 = ada_b[:, None, :]
    router_b3 = router_b[:, :, None]
    rw_t = router_w.transpose(0, 2, 1)
    rw_hi = rw_t.astype(BF16)
    router_wt = jnp.stack([rw_hi, (rw_t - rw_hi.astype(F32)).astype(BF16)], axis=1)
    gate_bias = jnp.pad(mlstm_gate_bias, ((0, 0), (0, LANES - N_GATE)))
    cos, sin = _rope_tables(seq, n_ctx_rows)

    cond_rows = (n_batch + 1 + 7) // 8 * 8
    cvec = jnp.zeros((cond_rows, d), F32).at[:n_batch].set(c).at[n_batch].set(c_ctx)

    def mod_row_all(i):
        return jnp.where(i % tiles_b < ctx_tiles, n_batch, i // tiles_b)

    xy = jnp.concatenate([ctx, x], axis=1).reshape(n_batch * rows_b, d)

    for layer in range(depth):
        last = layer == depth - 1
        mod = _adaln(cvec, ada_w, ada_b3, layer).reshape(cond_rows, 1, 6 * d)
        lb = _hgrn_lower_bound(hgrn_lower_bounds, layer)
        lam_init = 0.8 - 0.6 * math.exp(-0.3 * layer)
        lq1, lk1, lq2, lk2 = diff_lambdas[layer].astype(F32)
        lam = (jnp.exp(jnp.sum(lq1 * lk1)) - jnp.exp(jnp.sum(lq2 * lk2)) + lam_init).reshape(1)

        pa, pb, pc = _inproj(xy, mod, sandwich_norms, w_in_p, cos, sin, layer, tiles_b, ctx_tiles, n_batch)
        oa = _hgrn(pa.reshape(n_batch, rows_b, PA_W), lb, jnp.tile(hgrn_norm[layer], HEADS)[None, :], n_ctx_rows)
        ob = _mlstm(pb.reshape(n_batch, rows_b, PB_W), gate_bias[layer][None, :], mlstm_norm[layer][None, :],
                    n_ctx_rows)
        oc = _attention(pc.reshape(n_batch, rows_b, PC_OUT), lam, diff_norm[layer][None, :], n_ctx_rows,
                        lam_init, not last)

        if last:
            n_tiles = n_batch * lat_tiles
            in_tile = lambda i: (i // lat_tiles) * tiles_b + ctx_tiles + i % lat_tiles
            mod_row = lambda i: i // lat_tiles
        else:
            n_tiles = n_batch * tiles_b
            in_tile = lambda i: i
            mod_row = mod_row_all
        xn, h, route, counts = _outproj(
            oa.reshape(-1, A_W), ob.reshape(-1, B_W), oc.reshape(-1, C_W), xy, mod, sandwich_norms, w_out_b,
            router_wt, router_b3, layer, n_tiles, in_tile, lambda i: i, mod_row)

        r_max = n_tiles * ROW_TILE * TOP_K + n_exp * MOE_TILE
        pos, wts, tile_expert, tile_first, tile_valid, n_used = _dispatch_plan(route, counts, r_max)
        xs = _dispatch_rows(h, pos, r_max)
        ys = _moe(tile_expert, tile_first, tile_valid, n_used, xs, moe_w1, b1p, moe_w2, b2, layer)
        ys_by_k = [jnp.take(ys, pos[:, k], axis=0) for k in range(TOP_K)]
        xy = _ffn_residual(xn, wts, ys_by_k, mod, sandwich_norms, layer, mod_row)

    return xy.reshape(n_batch, seq, d)
```

```python
import functools
import math

import jax
import jax.numpy as jnp
from jax import lax
from jax.experimental import pallas as pl
from jax.experimental.pallas import tpu as pltpu
from jax.experimental.pallas import tpu_sc as plsc

F32 = jnp.float32
BF16 = jnp.bfloat16
HI = lax.Precision.HIGHEST

HEADS = 4
A_W = 256
B_QK = 32
B_W = 256
C_DQK = 64
C_DV = 2 * C_DQK
C_W = 512
HEAD_V = 64
N_GATE = 16
GRID_W = 64
TOP_K = 4
SWIGLU_ALPHA = 1.702
SWIGLU_LIMIT = 7.0
ROPE_THETA = 10000.0
NORM_EPS = 1e-6
MASK_NEG = -1e30
F_MIN = 1e-12

LANES = 128
ROW_TILE = 256
CHUNK = 64
MOE_TILE = 512
VMEM_LIMIT = 56 * 1024 * 1024

PA_W = 5 * A_W
PB_W = 2 * HEADS * B_QK + 2 * B_W + LANES
PC_IN = 3 * C_W
PC_OUT = 4 * C_W
W_IN_PAD = PA_W + PB_W + PC_IN


def _cparams(sem, flags=None):
    return pltpu.CompilerParams(dimension_semantics=sem, vmem_limit_bytes=VMEM_LIMIT, flags=flags)


def _nt(a, b):
    return lax.dot_general(a, b, (((1,), (1,)), ((), ())), preferred_element_type=F32)


def _tn(a, b, precision=None):
    return lax.dot_general(a, b, (((0,), (0,)), ((), ())), preferred_element_type=F32, precision=precision)


def _rms(x):
    return x * lax.rsqrt(jnp.mean(x * x, axis=-1, keepdims=True) + NORM_EPS)


def _silu(x):
    return x * jax.nn.sigmoid(x)


def _adaln_kernel(c_ref, w_ref, b_ref, o_ref):
    cond = _silu(c_ref[...])
    o_ref[...] = jnp.dot(cond, w_ref[...], preferred_element_type=F32, precision=HI) + b_ref[...]


def _adaln(cvec, ada_w, ada_b, layer):
    rows, d = cvec.shape
    return pl.pallas_call(
        _adaln_kernel,
        out_shape=jax.ShapeDtypeStruct((rows, 6 * d), F32),
        grid=(6,),
        in_specs=[pl.BlockSpec((rows, d), lambda j: (0, 0)),
                  pl.BlockSpec((None, d, d), lambda j: (layer, 0, j)),
                  pl.BlockSpec((None, 1, d), lambda j: (layer, 0, j))],
        out_specs=pl.BlockSpec((rows, d), lambda j: (0, j)),
        compiler_params=_cparams(("arbitrary",)),
        name="adaln",
    )(cvec, ada_w, ada_b)


def _inproj_kernel(x_ref, mod_ref, g_ref, w_ref, cos_ref, sin_ref, pa_ref, pb_ref, pc_ref):
    d = x_ref.shape[1]
    shift = mod_ref[0, :, 0:d]
    scale = mod_ref[0, :, d:2 * d]
    h = _rms(x_ref[...]) * g_ref[0:1, :] * (1.0 + scale) + shift
    hb = h.astype(BF16)
    pa_ref[...] = jnp.dot(hb, w_ref[:, 0:PA_W], preferred_element_type=F32)
    pb_ref[...] = jnp.dot(hb, w_ref[:, PA_W:PA_W + PB_W], preferred_element_type=F32)
    pc = jnp.dot(hb, w_ref[:, PA_W + PB_W:W_IN_PAD], preferred_element_type=F32)
    q = pc[:, 0:C_W]
    k = pc[:, C_W:2 * C_W]
    cos = cos_ref[...]
    sin = sin_ref[...]
    lane = lax.broadcasted_iota(jnp.int32, q.shape, 1)
    first = (lane % 32) < 16

    def rope(t):
        partner = jnp.where(first, pltpu.roll(t, C_W - 16, 1), pltpu.roll(t, 16, 1))
        return t * cos + partner * sin

    pc_ref[:, 0:C_W] = q.astype(BF16)
    pc_ref[:, C_W:2 * C_W] = rope(q).astype(BF16)
    pc_ref[:, 2 * C_W:3 * C_W] = rope(k).astype(BF16)
    pc_ref[:, 3 * C_W:4 * C_W] = pc[:, 2 * C_W:3 * C_W].astype(BF16)


def _inproj(xy, mod, norms, w_in_p, cos, sin, layer, tiles_per_batch, n_ctx_tiles, n_batch):
    t_all, d = xy.shape
    n_tiles = t_all // ROW_TILE

    def mod_row(i):
        return jnp.where(i % tiles_per_batch < n_ctx_tiles, n_batch, i // tiles_per_batch)

    return pl.pallas_call(
        _inproj_kernel,
        out_shape=(jax.ShapeDtypeStruct((t_all, PA_W), F32),
                   jax.ShapeDtypeStruct((t_all, PB_W), F32),
                   jax.ShapeDtypeStruct((t_all, PC_OUT), BF16)),
        grid=(n_tiles,),
        in_specs=[pl.BlockSpec((ROW_TILE, d), lambda i: (i, 0)),
                  pl.BlockSpec((1, 1, 6 * d), lambda i: (mod_row(i), 0, 0)),
                  pl.BlockSpec((None, 4, d), lambda i: (layer, 0, 0)),
                  pl.BlockSpec((None, d, W_IN_PAD), lambda i: (layer, 0, 0)),
                  pl.BlockSpec((ROW_TILE, C_W), lambda i: (i % tiles_per_batch, 0)),
                  pl.BlockSpec((ROW_TILE, C_W), lambda i: (i % tiles_per_batch, 0))],
        out_specs=(pl.BlockSpec((ROW_TILE, PA_W), lambda i: (i, 0)),
                   pl.BlockSpec((ROW_TILE, PB_W), lambda i: (i, 0)),
                   pl.BlockSpec((ROW_TILE, PC_OUT), lambda i: (i, 0))),
        compiler_params=_cparams(("arbitrary",)),
        name="inproj",
    )(xy, mod, norms, w_in_p, cos, sin)


def _hgrn_kernel(pa_ref, lb_ref, gain_ref, o_ref, st_ref, ob_ref, *, n_ctx, n_all):
    c_sz, w = CHUNK, A_W
    r_i = lax.broadcasted_iota(jnp.int32, (c_sz, c_sz), 0)
    c_i = lax.broadcasted_iota(jnp.int32, (c_sz, c_sz), 1)
    tri_incl_past = (c_i <= r_i).astype(BF16)
    tri_incl_future = (c_i >= r_i).astype(BF16)
    same_head = (lax.broadcasted_iota(jnp.int32, (w, w), 0) // HEAD_V
                 == lax.broadcasted_iota(jnp.int32, (w, w), 1) // HEAD_V)
    head_ones = same_head.astype(BF16)
    head_ones_f32 = same_head.astype(F32)
    assert c_sz == HEAD_V
    t_i = lax.broadcasted_iota(jnp.int32, (c_sz, w), 0)
    s_i = lax.broadcasted_iota(jnp.int32, (c_sz, w), 1) % c_sz
    t_row = lax.broadcasted_iota(jnp.int32, (c_sz, 1), 0)
    row8 = t_row % 8

    def grouped_rows(a, k):
        return jnp.concatenate(
            [jnp.broadcast_to(a[8 * j + k:8 * j + k + 1, :], (8, w)) for j in range(c_sz // 8)], axis=0)

    def halving_levels(rev):
        out = []
        b = c_sz // 2
        while b >= 1:
            def later(i):
                return ((i % (2 * b)) < b) if rev else ((i % (2 * b)) >= b)
            live = jnp.logical_and(t_i // (2 * b) == s_i // (2 * b),
                                   jnp.logical_and(later(t_i), jnp.logical_not(later(s_i))))
            out.append((b, live.astype(F32), jnp.where(later(t_row), 1.0, -1.0)))
            b //= 2
        return out

    self_mask = (s_i == t_i).astype(F32)
    levels_of = (halving_levels(False), halving_levels(True))

    def chunk(c, rev):
        d = 1 if rev else 0
        levels = levels_of[d]
        r0 = pl.multiple_of(c * c_sz, c_sz)
        rows = pl.ds(r0, c_sz)
        q_pre = pa_ref[0, rows, 0:A_W]
        v = pa_ref[0, rows, A_W:2 * A_W]
        f_pre = pa_ref[0, rows, 3 * A_W:4 * A_W] if rev else pa_ref[0, rows, 2 * A_W:3 * A_W]
        lb = lb_ref[1:2, :] if rev else lb_ref[0:1, :]
        q = _silu(q_pre)
        f = lb + (1.0 - lb) * jax.nn.sigmoid(f_pre)
        log_f = jnp.log(jnp.maximum(f, F_MIN))
        kk = (1.0 - lb) * jax.nn.sigmoid(-f_pre)
        tri = tri_incl_future if rev else tri_incl_past
        cum = sum(jnp.dot(tri, piece, preferred_element_type=F32) for piece in _split3(log_f))
        e = 0 if rev else c_sz - 1
        cum_end = cum[e:e + 1, :]
        vb = v.astype(BF16)

        st = st_ref[d]
        o = _nt((q * jnp.exp(cum)).astype(BF16), st.astype(BF16))
        k_end = (kk * jnp.exp(cum_end - cum)).astype(BF16)
        st_ref[d] = st * jnp.exp(cum_end) + jnp.where(same_head, _tn(vb, k_end), 0.0)

        p = self_mask * jnp.dot((q * kk).astype(BF16), head_ones, preferred_element_type=F32)
        for b, live, sign in levels:
            if b >= 8:
                ref = jnp.concatenate(
                    [jnp.broadcast_to(cum[r0 + (b if rev else b - 1):r0 + (b if rev else b - 1) + 1, :], (2 * b, w))
                     for r0 in range(0, c_sz, 2 * b)], axis=0)
            else:
                n_ref = 8 // (2 * b)
                ref = None
                for g in reversed(range(n_ref)):
                    cand = grouped_rows(cum, g * 2 * b + (b if rev else b - 1))
                    ref = cand if ref is None else jnp.where(row8 < (g + 1) * 2 * b, cand, ref)
            z = jnp.exp((cum - ref) * sign)
            qa = (q * z).astype(BF16)
            ka = (kk * z).astype(BF16)
            ka_bd = jnp.where(same_head, jnp.concatenate([ka] * HEADS, axis=0), jnp.zeros((), BF16))
            p = p + live * _nt(qa, ka_bd)
        v_bd = jnp.where(same_head, jnp.concatenate([vb] * HEADS, axis=0), jnp.zeros((), BF16))
        o = o + jnp.dot(p.astype(BF16), v_bd, preferred_element_type=F32)
        if rev:
            ob_ref[rows, :] = o
        else:
            o_ref[0, rows, :] = o

    def scan_step(n, carry):
        chunk(n, False)
        chunk(jnp.where(n < n_ctx, n_ctx - 1 - n, n_all - 1 - (n - n_ctx)), True)
        return carry

    st_ref[...] = jnp.zeros_like(st_ref)
    lax.fori_loop(0, n_all, scan_step, 0)

    def readout(c, carry):
        rows = pl.ds(pl.multiple_of(c * ROW_TILE, ROW_TILE), ROW_TILE)
        tot = o_ref[0, rows, :] + ob_ref[rows, :]
        ms = jnp.dot(tot * tot, head_ones_f32, preferred_element_type=F32, precision=HI) * (1.0 / HEAD_V)
        g = pa_ref[0, rows, 4 * A_W:5 * A_W]
        o_ref[0, rows, :] = tot * lax.rsqrt(ms + NORM_EPS) * gain_ref[...] * _silu(g)
        return carry

    lax.fori_loop(0, n_all * c_sz // ROW_TILE, readout, 0)


def _hgrn(pa3, lb, gain, n_ctx_rows):
    n_batch, rows, _ = pa3.shape
    kern = functools.partial(_hgrn_kernel, n_ctx=n_ctx_rows // CHUNK, n_all=rows // CHUNK)
    return pl.pallas_call(
        kern,
        out_shape=jax.ShapeDtypeStruct((n_batch, rows, A_W), F32),
        grid=(n_batch,),
        in_specs=[pl.BlockSpec((1, rows, PA_W), lambda b: (b, 0, 0)),
                  pl.BlockSpec((2, A_W), lambda b: (0, 0)),
                  pl.BlockSpec((1, A_W), lambda b: (0, 0))],
        out_specs=pl.BlockSpec((1, rows, A_W), lambda b: (b, 0, 0)),
        scratch_shapes=[pltpu.VMEM((2, A_W, A_W), F32), pltpu.VMEM((rows, A_W), F32)],
        compiler_params=_cparams(("arbitrary",)),
        name="hgrn2",
    )(pa3, lb, gain)


def _split3(x):
    hi = x.astype(BF16)
    rest = x - hi.astype(F32)
    mid = rest.astype(BF16)
    return hi, mid, (rest - mid.astype(F32)).astype(BF16)


def _mlstm_kernel(pb_ref, bias_ref, gain_ref, o_ref, cn_ref, ob_ref, *, n_ctx, n_all):
    assert CHUNK == HEAD_V
    c_sz, w, qk_w = CHUNK, B_W, HEADS * B_QK
    t_i = lax.broadcasted_iota(jnp.int32, (c_sz, w), 0)
    s_i = lax.broadcasted_iota(jnp.int32, (c_sz, w), 1) % c_sz
    r_i = lax.broadcasted_iota(jnp.int32, (c_sz, c_sz), 0)
    c_i = lax.broadcasted_iota(jnp.int32, (c_sz, c_sz), 1)
    diag4 = s_i == t_i
    half_lane = lax.broadcasted_iota(jnp.int32, (1, LANES), 1) < HEAD_V
    k_block = (lax.broadcasted_iota(jnp.int32, (HEADS * c_sz, qk_w), 0) // c_sz
               == lax.broadcasted_iota(jnp.int32, (HEADS * c_sz, qk_w), 1) // B_QK)
    v_block = (lax.broadcasted_iota(jnp.int32, (HEADS * c_sz, w), 0) // c_sz
               == lax.broadcasted_iota(jnp.int32, (HEADS * c_sz, w), 1) // HEAD_V)
    state_block = (lax.broadcasted_iota(jnp.int32, (qk_w, 2 * w), 0) // B_QK
                   == (lax.broadcasted_iota(jnp.int32, (qk_w, 2 * w), 1) % w) // HEAD_V)
    head_ones = v_block.astype(BF16)
    q_off, k_off, v_off, o_off, g_off = 0, qk_w, 2 * qk_w, 2 * qk_w + B_W, 2 * qk_w + 2 * B_W

    def selector(rev):
        sr = lax.broadcasted_iota(jnp.int32, (2 * LANES, 2 * w), 0)
        sc = lax.broadcasted_iota(jnp.int32, (2 * LANES, 2 * w), 1)
        i_base = 2 * HEADS if rev else 0
        want = jnp.where(sc < w, i_base + HEADS + sc // HEAD_V, LANES + i_base + (sc - w) // HEAD_V)
        return (sr == want).astype(BF16)

    def head_max(x):
        outs = []
        for col in range(w // LANES):
            xc = x[:, col * LANES:(col + 1) * LANES]
            lo = jnp.max(jnp.where(half_lane, xc, -jnp.inf), axis=-1, keepdims=True)
            hi = jnp.max(jnp.where(half_lane, -jnp.inf, xc), axis=-1, keepdims=True)
            outs.append(jnp.where(half_lane, lo, hi))
        return jnp.concatenate(outs, axis=1)

    def chunk(c, rev, m_prev, sel):
        d = 1 if rev else 0
        rows = pl.ds(pl.multiple_of(c * c_sz, c_sz), c_sz)
        qb = (pb_ref[0, rows, q_off:q_off + qk_w] * (B_QK ** -0.5)).astype(BF16)
        kb = pb_ref[0, rows, k_off:k_off + qk_w].astype(BF16)
        v = pb_ref[0, rows, v_off:v_off + B_W]
        gts = pb_ref[0, rows, g_off:g_off + LANES] + bias_ref[...]
        log_f = jnp.minimum(gts, 0.0) - jnp.log(1.0 + jnp.exp(-jnp.abs(gts)))
        mask = (s_i >= t_i) if rev else (s_i <= t_i)
        tri = ((c_i >= r_i) if rev else (c_i <= r_i)).astype(BF16)
        cum_col = sum(jnp.dot(tri, p, preferred_element_type=F32) for p in _split3(log_f))
        both = sum(jnp.dot(p, sel, preferred_element_type=F32)
                   for p in _split3(jnp.concatenate([cum_col, gts], axis=1)))
        cum_t = both[:, :w]
        ig_t = both[:, w:]
        src = jnp.sum(jnp.where(diag4, cum_t - ig_t, 0.0), axis=0, keepdims=True)
        log_d = jnp.where(mask, cum_t - src, MASK_NEG)
        log_inter = cum_t + m_prev
        m_t = jnp.maximum(log_inter, head_max(log_d))
        w_intra = jnp.where(mask, jnp.exp(log_d - m_t), 0.0)
        w_inter = jnp.exp(log_inter - m_t)

        k_bd = jnp.where(k_block, jnp.concatenate([kb] * HEADS, axis=0), jnp.zeros((), BF16))
        p = (_nt(qb, k_bd) * w_intra).astype(BF16)
        vb = v.astype(BF16)
        v_bd = jnp.where(v_block, jnp.concatenate([vb] * HEADS, axis=0), jnp.zeros((), BF16))
        cn = cn_ref[d]
        inter = jnp.dot(qb, cn.astype(BF16), preferred_element_type=F32)
        num = w_inter * inter[:, :w] + jnp.dot(p, v_bd, preferred_element_type=F32)
        den = w_inter * inter[:, w:] + jnp.dot(p, head_ones, preferred_element_type=F32)
        h_out = num / jnp.maximum(jnp.abs(den), jnp.exp(-m_t))

        e = 0 if rev else c_sz - 1
        cum_e = cum_t[e:e + 1, :]
        log_end = cum_e - cum_t + ig_t
        m_end = jnp.maximum(cum_e + m_prev, jnp.max(log_end, axis=0, keepdims=True))
        w_end = jnp.exp(log_end - m_end)
        w_carry = jnp.exp(cum_e + m_prev - m_end)
        upd = _tn(kb, jnp.concatenate([(w_end * v).astype(BF16), w_end.astype(BF16)], axis=1))
        cn_ref[d] = jnp.concatenate([w_carry, w_carry], axis=1) * cn + jnp.where(state_block, upd, 0.0)
        if rev:
            ob_ref[rows, :] = h_out
        else:
            o_ref[0, rows, :] = h_out
        return m_end

    sel_fwd = selector(False)
    sel_bwd = selector(True)

    def scan_step(n, carry):
        m_fwd, m_bwd = carry
        m_fwd = chunk(n, False, m_fwd, sel_fwd)
        m_bwd = chunk(jnp.where(n < n_ctx, n_ctx - 1 - n, n_all - 1 - (n - n_ctx)), True, m_bwd, sel_bwd)
        return m_fwd, m_bwd

    zero_m = jnp.zeros((1, w), F32)
    cn_ref[...] = jnp.zeros_like(cn_ref)
    lax.fori_loop(0, n_all, scan_step, (zero_m, zero_m), unroll=2)

    def readout(c, carry):
        rows = pl.ds(pl.multiple_of(c * ROW_TILE, ROW_TILE), ROW_TILE)
        tot = o_ref[0, rows, :] + ob_ref[rows, :]
        normed = jnp.concatenate(
            [_rms(tot[:, h * HEAD_V:(h + 1) * HEAD_V]) for h in range(HEADS)], axis=1) * gain_ref[...]
        og = pb_ref[0, rows, o_off:o_off + B_W]
        o_ref[0, rows, :] = normed * jax.nn.sigmoid(og)
        return carry

    lax.fori_loop(0, n_all * c_sz // ROW_TILE, readout, 0)


def _mlstm(pb3, bias, gain, n_ctx_rows):
    n_batch, rows, _ = pb3.shape
    kern = functools.partial(_mlstm_kernel, n_ctx=n_ctx_rows // CHUNK, n_all=rows // CHUNK)
    return pl.pallas_call(
        kern,
        out_shape=jax.ShapeDtypeStruct((n_batch, rows, B_W), F32),
        grid=(n_batch,),
        in_specs=[pl.BlockSpec((1, rows, PB_W), lambda b: (b, 0, 0)),
                  pl.BlockSpec((1, LANES), lambda b: (0, 0)),
                  pl.BlockSpec((1, B_W), lambda b: (0, 0))],
        out_specs=pl.BlockSpec((1, rows, B_W), lambda b: (b, 0, 0)),
        scratch_shapes=[pltpu.VMEM((2, HEADS * B_QK, 2 * B_W), F32), pltpu.VMEM((rows, B_W), F32)],
        compiler_params=_cparams(("arbitrary",)),
        name="mlstm",
    )(pb3, bias, gain)


def _attn_kernel(lam_ref, qp_ref, qr_ref, k_ref, v_ref, gain_ref, o_ref, *, n_ctx, q_tile0, lam_init):
    lam = lam_ref[0]
    q_tile = pl.program_id(2) + q_tile0
    lane = lax.broadcasted_iota(jnp.int32, (1, 2 * C_DQK), 1)
    scale = C_DQK ** -0.5

    def finish(parts):
        o = parts[0] - lam * parts[1]
        o_ref[0] = _rms(o) * gain_ref[...] * (1.0 - lam_init)

    def sub_query(q, j):
        return jnp.where(lane // C_DQK == j, q * scale, jnp.zeros_like(q))

    def row_max(s):
        return jnp.max(s, axis=-1, keepdims=True)

    def row_sum(s):
        return jnp.sum(s, axis=-1, keepdims=True)

    def pv(ex, v):
        return jnp.dot(ex.astype(BF16), v, preferred_element_type=F32)

    @pl.when(q_tile * ROW_TILE < n_ctx)
    def _():
        kc = k_ref[0, 0:n_ctx, :]
        vc = v_ref[0, 0:n_ctx, :]
        parts = []
        for j in range(2):
            s = _nt(sub_query(qp_ref[0], j), kc)
            ex = jnp.exp(s - row_max(s))
            parts.append(pv(ex, vc) / row_sum(ex))
        finish(parts)

    @pl.when(q_tile * ROW_TILE >= n_ctx)
    def _():
        n_all = k_ref.shape[1]
        kc = k_ref[0, 0:n_ctx, :]
        kl = k_ref[0, n_ctx:n_all, :]
        vc = v_ref[0, 0:n_ctx, :]
        vl = v_ref[0, n_ctx:n_all, :]
        parts = []
        for j in range(2):
            s_c = _nt(sub_query(qp_ref[0], j), kc)
            s_l = _nt(sub_query(qr_ref[0], j), kl)
            m = jnp.maximum(row_max(s_c), row_max(s_l))
            e_c = jnp.exp(s_c - m)
            e_l = jnp.exp(s_l - m)
            parts.append((pv(e_c, vc) + pv(e_l, vl)) / (row_sum(e_c) + row_sum(e_l)))
        finish(parts)


def _attention(pc3, lam, gain, n_ctx_rows, lam_init, with_ctx):
    n_batch, rows, _ = pc3.shape
    q_tile0 = 0 if with_ctx else n_ctx_rows // ROW_TILE
    n_q = rows // ROW_TILE - q_tile0
    hb = C_W // LANES
    kern = functools.partial(_attn_kernel, n_ctx=n_ctx_rows, q_tile0=q_tile0, lam_init=lam_init)
    grid_spec = pltpu.PrefetchScalarGridSpec(
        num_scalar_prefetch=1,
        grid=(n_batch, HEADS, n_q),
        in_specs=[pl.BlockSpec((1, ROW_TILE, LANES), lambda b, h, i, lam: (b, i + q_tile0, h)),
                  pl.BlockSpec((1, ROW_TILE, LANES), lambda b, h, i, lam: (b, i + q_tile0, hb + h)),
                  pl.BlockSpec((1, rows, LANES), lambda b, h, i, lam: (b, 0, 2 * hb + h)),
                  pl.BlockSpec((1, rows, LANES), lambda b, h, i, lam: (b, 0, 3 * hb + h)),
                  pl.BlockSpec((1, LANES), lambda b, h, i, lam: (0, 0))],
        out_specs=pl.BlockSpec((1, ROW_TILE, LANES), lambda b, h, i, lam: (b, i, h)),
    )
    return pl.pallas_call(
        kern,
        out_shape=jax.ShapeDtypeStruct((n_batch, n_q * ROW_TILE, C_W), F32),
        grid_spec=grid_spec,
        compiler_params=_cparams(("arbitrary", "arbitrary", "arbitrary")),
        name="diff_attn",
    )(lam, pc3, pc3, pc3, pc3, gain)


def _outproj_kernel(oa_ref, ob_ref, oc_ref, x_ref, mod_ref, n_ref, w_ref, rw_ref, rb_ref,
                    xn_ref, h_ref, route_ref, cnt_ref, carry_ref):
    d = x_ref.shape[1]
    n_exp = rw_ref.shape[1]

    @pl.when(pl.program_id(0) == 0)
    def _():
        carry_ref[...] = jnp.zeros_like(carry_ref)

    mix = (jnp.dot(oa_ref[...].astype(BF16), w_ref[0:A_W, :], preferred_element_type=F32)
           + jnp.dot(ob_ref[...].astype(BF16), w_ref[A_W:A_W + B_W, :], preferred_element_type=F32)
           + jnp.dot(oc_ref[...].astype(BF16), w_ref[A_W + B_W:A_W + B_W + C_W, :], preferred_element_type=F32))
    xn = x_ref[...] + mod_ref[0, :, 2 * d:3 * d] * (_rms(mix) * n_ref[1:2, :])
    xn_ref[...] = xn
    h = _rms(xn) * n_ref[2:3, :] * (1.0 + mod_ref[0, :, 4 * d:5 * d]) + mod_ref[0, :, 3 * d:4 * d]
    bits = pltpu.bitcast(h.astype(BF16).astype(F32), jnp.uint32)
    words = (bits[:, :d // 2] >> 16) | (bits[:, d // 2:] & jnp.uint32(0xFFFF0000))
    h_ref[...] = pltpu.bitcast(words, jnp.int32)

    h_hi = h.astype(BF16)
    h_mid = (h - h_hi.astype(F32)).astype(BF16)
    logits = _nt(rw_ref[0], h_hi) + _nt(rw_ref[1], h_hi) + _nt(rw_ref[0], h_mid) + rb_ref[...]
    e_sub = lax.broadcasted_iota(jnp.int32, logits.shape, 0)
    cur = logits
    picks, vals = [], []
    for _ in range(TOP_K):
        mx = jnp.max(cur, axis=0, keepdims=True)
        idx = jnp.min(jnp.where(cur == mx, e_sub, n_exp), axis=0, keepdims=True)
        hit = e_sub == idx
        cur = jnp.where(hit, -jnp.inf, cur)
        picks.append((idx, hit.astype(F32)))
        vals.append(mx)
    exps = [jnp.exp(vv - vals[0]) for vv in vals]
    total = exps[0] + exps[1] + exps[2] + exps[3]

    chosen = picks[0][1] + picks[1][1] + picks[2][1] + picks[3][1]
    tm = logits.shape[1]
    before = (lax.broadcasted_iota(jnp.int32, (tm, tm), 0) < lax.broadcasted_iota(jnp.int32, (tm, tm), 1))
    seen = jnp.dot(chosen.astype(BF16), before.astype(BF16), preferred_element_type=F32) + carry_ref[...]
    new_carry = carry_ref[...] + jnp.sum(chosen, axis=1, keepdims=True)
    carry_ref[...] = new_carry
    cnt_ref[...] = new_carry

    o_row = lax.broadcasted_iota(jnp.int32, route_ref.shape, 0)
    route = jnp.zeros(route_ref.shape, F32)
    for kk in range(TOP_K):
        idx, hit = picks[kk]
        rank = jnp.sum(hit * seen, axis=0, keepdims=True)
        route = (route + jnp.where(o_row == kk, idx.astype(F32), 0.0)
                 + jnp.where(o_row == TOP_K + kk, exps[kk] / total, 0.0)
                 + jnp.where(o_row == 2 * TOP_K + kk, rank, 0.0))
    route_ref[...] = route


ROUTE_ROWS = 16


def _outproj(oa, ob, oc, xy, mod, norms, w_out_b, router_wt, router_b, layer, n_tiles, in_tile, oc_tile, mod_row):
    d = xy.shape[1]
    n_exp = router_wt.shape[2]
    t_out = n_tiles * ROW_TILE
    xn, h, route_t, counts = pl.pallas_call(
        _outproj_kernel,
        out_shape=(jax.ShapeDtypeStruct((t_out, d), F32),
                   jax.ShapeDtypeStruct((t_out, d // 2), jnp.int32),
                   jax.ShapeDtypeStruct((n_tiles * ROUTE_ROWS, ROW_TILE), F32),
                   jax.ShapeDtypeStruct((n_exp, 1), F32)),
        grid=(n_tiles,),
        in_specs=[pl.BlockSpec((ROW_TILE, A_W), lambda i: (in_tile(i), 0)),
                  pl.BlockSpec((ROW_TILE, B_W), lambda i: (in_tile(i), 0)),
                  pl.BlockSpec((ROW_TILE, C_W), lambda i: (oc_tile(i), 0)),
                  pl.BlockSpec((ROW_TILE, d), lambda i: (in_tile(i), 0)),
                  pl.BlockSpec((1, 1, 6 * d), lambda i: (mod_row(i), 0, 0)),
                  pl.BlockSpec((None, 4, d), lambda i: (layer, 0, 0)),
                  pl.BlockSpec((None, d, d), lambda i: (layer, 0, 0)),
                  pl.BlockSpec((None, 2, n_exp, d), lambda i: (layer, 0, 0, 0)),
                  pl.BlockSpec((None, n_exp, 1), lambda i: (layer, 0, 0))],
        out_specs=(pl.BlockSpec((ROW_TILE, d), lambda i: (i, 0)),
                   pl.BlockSpec((ROW_TILE, d // 2), lambda i: (i, 0)),
                   pl.BlockSpec((ROUTE_ROWS, ROW_TILE), lambda i: (i, 0)),
                   pl.BlockSpec((n_exp, 1), lambda i: (0, 0))),
        scratch_shapes=[pltpu.VMEM((n_exp, 1), F32)],
        compiler_params=_cparams(("arbitrary",)),
        name="outproj_router",
    )(oa, ob, oc, xy, mod, norms, w_out_b, router_wt, router_b)
    return xn, h, route_t.reshape(n_tiles, ROUTE_ROWS, ROW_TILE), counts[:, 0]


PAIR_BLOCK = 2 * LANES


def _moe_kernel(te_ref, first_ref, valid_ref, nu_ref, xs_ref, w1_ref, b1_ref, w2_ref, b2_ref, ys_ref,
                w1p_ref, w2b_ref):
    i = pl.program_id(0)
    two_f = w1_ref.shape[1]
    n_blk = two_f // PAIR_BLOCK

    @pl.when(jnp.logical_and(i < nu_ref[0], first_ref[i] == 1))
    def _():
        r = lax.broadcasted_iota(jnp.int32, (PAIR_BLOCK, PAIR_BLOCK), 0)
        c = lax.broadcasted_iota(jnp.int32, (PAIR_BLOCK, PAIR_BLOCK), 1)
        perm = (r == jnp.where(c < LANES, 2 * c, 2 * (c - LANES) + 1)).astype(BF16)
        for blk in range(n_blk):
            cols = slice(blk * PAIR_BLOCK, (blk + 1) * PAIR_BLOCK)
            w1p_ref[:, cols] = jnp.dot(w1_ref[:, cols].astype(BF16), perm,
                                       preferred_element_type=F32).astype(BF16)
        w2b_ref[...] = w2_ref[...].astype(BF16)

    @pl.when(i < nu_ref[0])
    def _():
        words = pltpu.bitcast(xs_ref[...], jnp.uint32)
        x = jnp.concatenate([pltpu.bitcast(words << 16, F32),
                             pltpu.bitcast(words & jnp.uint32(0xFFFF0000), F32)], axis=1).astype(BF16)
        row = lax.broadcasted_iota(jnp.int32, (x.shape[0], 1), 0)
        x = jnp.where(row < valid_ref[i], x, jnp.zeros_like(x))
        hid = jnp.dot(x, w1p_ref[...], preferred_element_type=F32) + b1_ref[...]
        acts = []
        for blk in range(n_blk):
            glu = jnp.minimum(hid[:, blk * PAIR_BLOCK:blk * PAIR_BLOCK + LANES], SWIGLU_LIMIT)
            lin = jnp.clip(hid[:, blk * PAIR_BLOCK + LANES:(blk + 1) * PAIR_BLOCK], -SWIGLU_LIMIT, SWIGLU_LIMIT)
            acts.append((glu * jax.nn.sigmoid(SWIGLU_ALPHA * glu) * (lin + 1.0)).astype(BF16))
        y = jnp.dot(jnp.concatenate(acts, axis=1), w2b_ref[...], preferred_element_type=F32) + b2_ref[...]
        ys_ref[...] = y.astype(ys_ref.dtype)

    @pl.when(i >= nu_ref[0])
    def _():
        ys_ref[...] = jnp.zeros_like(ys_ref)


def _moe(tile_expert, tile_first, tile_valid, n_used, xs, w1, b1p, w2, b2, layer):
    r_max, half_d = xs.shape
    d = 2 * half_d
    two_f = w1.shape[3]
    f = two_f // 2
    n_tiles = r_max // MOE_TILE

    def row_tile(i, te, fi, va, nu):
        return (jnp.maximum(jnp.minimum(i, nu[0] - 1), 0), 0)

    grid_spec = pltpu.PrefetchScalarGridSpec(
        num_scalar_prefetch=4,
        grid=(n_tiles,),
        in_specs=[pl.BlockSpec((MOE_TILE, half_d), row_tile),
                  pl.BlockSpec((None, None, d, two_f), lambda i, te, fi, va, nu: (layer, te[i], 0, 0)),
                  pl.BlockSpec((None, None, 1, two_f), lambda i, te, fi, va, nu: (layer, te[i], 0, 0)),
                  pl.BlockSpec((None, None, f, d), lambda i, te, fi, va, nu: (layer, te[i], 0, 0)),
                  pl.BlockSpec((None, None, 1, d), lambda i, te, fi, va, nu: (layer, te[i], 0, 0))],
        out_specs=pl.BlockSpec((MOE_TILE, d), lambda i, te, fi, va, nu: (i, 0)),
        scratch_shapes=[pltpu.VMEM((d, two_f), BF16), pltpu.VMEM((f, d), BF16)],
    )
    return pl.pallas_call(
        _moe_kernel,
        out_shape=jax.ShapeDtypeStruct((r_max, d), BF16),
        grid_spec=grid_spec,
        compiler_params=_cparams(("arbitrary",)),
        name="moe_experts",
    )(tile_expert, tile_first, tile_valid, n_used, xs, w1, b1p, w2, b2)


SC_CORES = 2
SC_SUBCORES = 16
SC_CHUNK = 64


def _dispatch_rows(hp, pos, r_max):
    t, width = hp.shape
    workers = SC_CORES * SC_SUBCORES
    assert t % (workers * SC_CHUNK) == 0
    per_worker = t // (workers * SC_CHUNK)
    idx = pos.reshape(-1, TOP_K, ROW_TILE // SC_CHUNK, SC_CHUNK).transpose(0, 2, 1, 3).reshape(
        workers, per_worker * TOP_K, SC_CHUNK)
    mesh = plsc.VectorSubcoreMesh(core_axis_name="c", subcore_axis_name="s",
                                  num_cores=SC_CORES, num_subcores=SC_SUBCORES)

    @functools.partial(
        pl.kernel, mesh=mesh,
        out_type=jax.ShapeDtypeStruct((r_max, width), hp.dtype),
        scratch_types=[pltpu.VMEM((per_worker * TOP_K, SC_CHUNK), jnp.int32),
                       pltpu.VMEM((SC_CHUNK, width), hp.dtype)],
    )
    def scatter(hp_hbm, idx_hbm, out_hbm, idx_v, rows_v):
        wid = lax.axis_index("s") * SC_CORES + lax.axis_index("c")
        pltpu.sync_copy(idx_hbm.at[wid], idx_v)

        @pl.loop(0, per_worker)
        def _(j):
            row0 = pl.multiple_of((wid * per_worker + j) * SC_CHUNK, SC_CHUNK)
            pltpu.sync_copy(hp_hbm.at[pl.ds(row0, SC_CHUNK)], rows_v)
            for k in range(TOP_K):
                pltpu.sync_copy(rows_v, out_hbm.at[idx_v.at[j * TOP_K + k]])

    return scatter(hp, idx)


def _ffn_residual_kernel(x_ref, w_ref, y0_ref, y1_ref, y2_ref, y3_ref, mod_ref, n_ref, o_ref):
    d = x_ref.shape[1]
    wts = w_ref[...]
    ffn = None
    for k, y_ref in enumerate((y0_ref, y1_ref, y2_ref, y3_ref)):
        term = wts[:, k:k + 1] * y_ref[...].astype(F32)
        ffn = term if ffn is None else ffn + term
    o_ref[...] = x_ref[...] + mod_ref[0, :, 5 * d:6 * d] * (_rms(ffn) * n_ref[3:4, :])


def _ffn_residual(xn, wts, ys_by_k, mod, norms, layer, mod_row):
    t, d = xn.shape
    row_spec = pl.BlockSpec((ROW_TILE, d), lambda i: (i, 0))
    return pl.pallas_call(
        _ffn_residual_kernel,
        out_shape=jax.ShapeDtypeStruct((t, d), F32),
        grid=(t // ROW_TILE,),
        in_specs=[row_spec,
                  pl.BlockSpec((ROW_TILE, TOP_K), lambda i: (i, 0)),
                  row_spec, row_spec, row_spec, row_spec,
                  pl.BlockSpec((1, 1, 6 * d), lambda i: (mod_row(i), 0, 0)),
                  pl.BlockSpec((None, 4, d), lambda i: (layer, 0, 0))],
        out_specs=row_spec,
        compiler_params=_cparams(("arbitrary",)),
        name="ffn_residual",
    )(xn, wts, *ys_by_k, mod, norms)


def _dispatch_plan(route_t, counts, r_max):
    n_row_tiles = route_t.shape[0]
    idx = route_t[:, 0:TOP_K, :].astype(jnp.int32)
    wts = route_t[:, TOP_K:2 * TOP_K, :].transpose(0, 2, 1).reshape(n_row_tiles * ROW_TILE, TOP_K)
    rank = route_t[:, 2 * TOP_K:3 * TOP_K, :].astype(jnp.int32)
    cnt = counts.astype(jnp.int32)
    padded = (cnt + MOE_TILE - 1) // MOE_TILE * MOE_TILE
    ends = jnp.cumsum(padded)
    pos = (ends - padded)[idx] + rank
    n_tiles = r_max // MOE_TILE
    n_used = ends[-1] // MOE_TILE
    tile_ids = jnp.minimum(jnp.arange(n_tiles, dtype=jnp.int32), n_used - 1)
    tile_expert = jnp.sum((ends // MOE_TILE)[None, :] <= tile_ids[:, None], axis=1).astype(jnp.int32)
    tile_first = jnp.concatenate(
        [jnp.ones((1,), jnp.int32), (tile_expert[1:] != tile_expert[:-1]).astype(jnp.int32)])
    of_expert = (tile_expert[:, None] == jnp.arange(cnt.shape[0], dtype=jnp.int32)[None, :]).astype(jnp.int32)
    first_tile = jnp.sum(of_expert * ((ends - padded) // MOE_TILE)[None, :], axis=1)
    tile_valid = jnp.clip(jnp.sum(of_expert * cnt[None, :], axis=1) - (tile_ids - first_tile) * MOE_TILE,
                          0, MOE_TILE).astype(jnp.int32)
    return pos, wts, tile_expert, tile_first, tile_valid, n_used.reshape(1).astype(jnp.int32)


def _rope_tables(length, n_ctx_rows):
    rows = length // GRID_W
    row = jnp.repeat(jnp.arange(rows, dtype=F32), GRID_W)
    col = jnp.tile(jnp.arange(GRID_W, dtype=F32), rows)
    n_freq = C_DQK // 4
    inv_freq = ROPE_THETA ** (-jnp.arange(n_freq, dtype=F32) / n_freq)
    ang_r = row[:, None] * inv_freq
    ang_c = col[:, None] * inv_freq
    cos = jnp.concatenate([jnp.cos(ang_r), jnp.cos(ang_r), jnp.cos(ang_c), jnp.cos(ang_c)], axis=-1)
    sin = jnp.concatenate([-jnp.sin(ang_r), jnp.sin(ang_r), -jnp.sin(ang_c), jnp.sin(ang_c)], axis=-1)
    reps = C_W // C_DQK
    cos = jnp.concatenate([jnp.ones((n_ctx_rows, C_W), F32), jnp.tile(cos, (1, reps))], axis=0)
    sin = jnp.concatenate([jnp.zeros((n_ctx_rows, C_W), F32), jnp.tile(sin, (1, reps))], axis=0)
    return cos, sin


def _hgrn_lower_bound(table, layer):
    p = jax.nn.softmax(table.astype(F32), axis=1)
    cum = jnp.cumsum(p, axis=1) - p[:, :1]
    return jnp.clip(cum[:, layer], 0.0, 1.0)


def kernel(x, c, ctx, c_ctx, ada_w, ada_b, sandwich_norms, w_in, w_out, hgrn_lower_bounds, hgrn_norm,
           mlstm_gate_bias, mlstm_norm, diff_lambdas, diff_norm, router_w, router_b, moe_w1, moe_b1,
           moe_w2, moe_b2):
    n_batch, seq, d = x.shape
    n_ctx_rows = ctx.shape[1]
    depth = w_in.shape[0]
    n_exp = router_w.shape[2]
    assert seq % ROW_TILE == 0 and n_ctx_rows % ROW_TILE == 0 and seq % GRID_W == 0
    rows_b = n_ctx_rows + seq
    tiles_b = rows_b // ROW_TILE
    ctx_tiles = n_ctx_rows // ROW_TILE
    lat_tiles = seq // ROW_TILE

    b_main = PA_W + 2 * HEADS * B_QK + 2 * B_W
    w_in_p = jnp.concatenate(
        [w_in[:, :, :b_main],
         jnp.pad(w_in[:, :, b_main:b_main + N_GATE], ((0, 0), (0, 0), (0, LANES - N_GATE))),
         w_in[:, :, b_main + N_GATE:]], axis=2).astype(BF16)
    w_out_b = w_out.astype(BF16)
    two_f = moe_b1.shape[2]
    b1p = moe_b1.reshape(depth, n_exp, two_f // PAIR_BLOCK, LANES, 2).transpose(0, 1, 2, 4, 3).reshape(
        depth, n_exp, 1, two_f)
    b2 = moe_b2[:, :, None, :]
    ada_b3 = ada_b[:, None, :]
    router_b3 = router_b[:, :, None]
    rw_t = router_w.transpose(0, 2, 1)
    rw_hi = rw_t.astype(BF16)
    router_wt = jnp.stack([rw_hi, (rw_t - rw_hi.astype(F32)).astype(BF16)], axis=1)
    gate_bias = jnp.pad(mlstm_gate_bias, ((0, 0), (0, LANES - N_GATE)))
    cos, sin = _rope_tables(seq, n_ctx_rows)

    cond_rows = (n_batch + 1 + 7) // 8 * 8
    cvec = jnp.zeros((cond_rows, d), F32).at[:n_batch].set(c).at[n_batch].set(c_ctx)

    def mod_row_all(i):
        return jnp.where(i % tiles_b < ctx_tiles, n_batch, i // tiles_b)

    xy = jnp.concatenate([ctx, x], axis=1).reshape(n_batch * rows_b, d)

    for layer in range(depth):
        last = layer == depth - 1
        mod = _adaln(cvec, ada_w, ada_b3, layer).reshape(cond_rows, 1, 6 * d)
        lb = _hgrn_lower_bound(hgrn_lower_bounds, layer)
        lam_init = 0.8 - 0.6 * math.exp(-0.3 * layer)
        lq1, lk1, lq2, lk2 = diff_lambdas[layer].astype(F32)
        lam = (jnp.exp(jnp.sum(lq1 * lk1)) - jnp.exp(jnp.sum(lq2 * lk2)) + lam_init).reshape(1)

        pa, pb, pc = _inproj(xy, mod, sandwich_norms, w_in_p, cos, sin, layer, tiles_b, ctx_tiles, n_batch)
        oa = _hgrn(pa.reshape(n_batch, rows_b, PA_W), lb, jnp.tile(hgrn_norm[layer], HEADS)[None, :], n_ctx_rows)
        ob = _mlstm(pb.reshape(n_batch, rows_b, PB_W), gate_bias[layer][None, :], mlstm_norm[layer][None, :],
                    n_ctx_rows)
        oc = _attention(pc.reshape(n_batch, rows_b, PC_OUT), lam, diff_norm[layer][None, :], n_ctx_rows,
                        lam_init, not last)

        if last:
            n_tiles = n_batch * lat_tiles
            in_tile = lambda i: (i // lat_tiles) * tiles_b + ctx_tiles + i % lat_tiles
            mod_row = lambda i: i // lat_tiles
        else:
            n_tiles = n_batch * tiles_b
            in_tile = lambda i: i
            mod_row = mod_row_all
        xn, h, route, counts = _outproj(
            oa.reshape(-1, A_W), ob.reshape(-1, B_W), oc.reshape(-1, C_W), xy, mod, sandwich_norms, w_out_b,
            router_wt, router_b3, layer, n_tiles, in_tile, lambda i: i, mod_row)

        r_max = n_tiles * ROW_TILE * TOP_K + n_exp * MOE_TILE
        pos, wts, tile_expert, tile_first, tile_valid, n_used = _dispatch_plan(route, counts, r_max)
        xs = _dispatch_rows(h, pos, r_max)
        ys = _moe(tile_expert, tile_first, tile_valid, n_used, xs, moe_w1, b1p, moe_w2, b2, layer)
        ys_by_k = [ys.at[pos[:, k, :].reshape(-1)].get(mode="promise_in_bounds") for k in range(TOP_K)]
        xy = _ffn_residual(xn, wts, ys_by_k, mod, sandwich_norms, layer, mod_row)

    return xy.reshape(n_batch, seq, d)
```

```python
import functools
import math

import jax
import jax.numpy as jnp
from jax import lax
from jax.experimental import pallas as pl
from jax.experimental.pallas import tpu as pltpu
from jax.experimental.pallas import tpu_sc as plsc

F32 = jnp.float32
BF16 = jnp.bfloat16
HI = lax.Precision.HIGHEST

HEADS = 4
A_W = 256
B_QK = 32
B_W = 256
C_DQK = 64
C_DV = 2 * C_DQK
C_W = 512
HEAD_V = 64
N_GATE = 16
GRID_W = 64
TOP_K = 4
SWIGLU_ALPHA = 1.702
SWIGLU_LIMIT = 7.0
ROPE_THETA = 10000.0
NORM_EPS = 1e-6
MASK_NEG = -1e30
F_MIN = 1e-12

LANES = 128
ROW_TILE = 256
CHUNK = 64
MOE_TILE = 512
VMEM_LIMIT = 56 * 1024 * 1024

PA_W = 5 * A_W
PB_W = 2 * HEADS * B_QK + 2 * B_W + LANES
PC_IN = 3 * C_W
PC_OUT = 4 * C_W
W_IN_PAD = PA_W + PB_W + PC_IN


def _cparams(sem, flags=None):
    return pltpu.CompilerParams(dimension_semantics=sem, vmem_limit_bytes=VMEM_LIMIT, flags=flags)


def _nt(a, b):
    return lax.dot_general(a, b, (((1,), (1,)), ((), ())), preferred_element_type=F32)


def _tn(a, b, precision=None):
    return lax.dot_general(a, b, (((0,), (0,)), ((), ())), preferred_element_type=F32, precision=precision)


def _rms(x):
    return x * lax.rsqrt(jnp.mean(x * x, axis=-1, keepdims=True) + NORM_EPS)


def _silu(x):
    return x * jax.nn.sigmoid(x)


def _adaln_kernel(c_ref, w_ref, b_ref, o_ref):
    cond = _silu(c_ref[...])
    o_ref[...] = jnp.dot(cond, w_ref[...], preferred_element_type=F32, precision=HI) + b_ref[...]


def _adaln(cvec, ada_w, ada_b, layer):
    rows, d = cvec.shape
    return pl.pallas_call(
        _adaln_kernel,
        out_shape=jax.ShapeDtypeStruct((rows, 6 * d), F32),
        grid=(6,),
        in_specs=[pl.BlockSpec((rows, d), lambda j: (0, 0)),
                  pl.BlockSpec((None, d, d), lambda j: (layer, 0, j)),
                  pl.BlockSpec((None, 1, d), lambda j: (layer, 0, j))],
        out_specs=pl.BlockSpec((rows, d), lambda j: (0, j)),
        compiler_params=_cparams(("arbitrary",)),
        name="adaln",
    )(cvec, ada_w, ada_b)


def _inproj_kernel(x_ref, mod_ref, g_ref, w_ref, cos_ref, sin_ref, pa_ref, pb_ref, pc_ref):
    d = x_ref.shape[1]
    shift = mod_ref[0, :, 0:d]
    scale = mod_ref[0, :, d:2 * d]
    h = _rms(x_ref[...]) * g_ref[0:1, :] * (1.0 + scale) + shift
    hb = h.astype(BF16)
    pa_ref[...] = jnp.dot(hb, w_ref[:, 0:PA_W], preferred_element_type=F32)
    pb_ref[...] = jnp.dot(hb, w_ref[:, PA_W:PA_W + PB_W], preferred_element_type=F32)
    pc = jnp.dot(hb, w_ref[:, PA_W + PB_W:W_IN_PAD], preferred_element_type=F32)
    q = pc[:, 0:C_W]
    k = pc[:, C_W:2 * C_W]
    cos = cos_ref[...]
    sin = sin_ref[...]
    lane = lax.broadcasted_iota(jnp.int32, q.shape, 1)
    first = (lane % 32) < 16

    def rope(t):
        partner = jnp.where(first, pltpu.roll(t, C_W - 16, 1), pltpu.roll(t, 16, 1))
        return t * cos + partner * sin

    pc_ref[:, 0:C_W] = q.astype(BF16)
    pc_ref[:, C_W:2 * C_W] = rope(q).astype(BF16)
    pc_ref[:, 2 * C_W:3 * C_W] = rope(k).astype(BF16)
    pc_ref[:, 3 * C_W:4 * C_W] = pc[:, 2 * C_W:3 * C_W].astype(BF16)


def _inproj(xy, mod, norms, w_in_p, cos, sin, layer, tiles_per_batch, n_ctx_tiles, n_batch):
    t_all, d = xy.shape
    n_tiles = t_all // ROW_TILE

    def mod_row(i):
        return jnp.where(i % tiles_per_batch < n_ctx_tiles, n_batch, i // tiles_per_batch)

    return pl.pallas_call(
        _inproj_kernel,
        out_shape=(jax.ShapeDtypeStruct((t_all, PA_W), F32),
                   jax.ShapeDtypeStruct((t_all, PB_W), F32),
                   jax.ShapeDtypeStruct((t_all, PC_OUT), BF16)),
        grid=(n_tiles,),
        in_specs=[pl.BlockSpec((ROW_TILE, d), lambda i: (i, 0)),
                  pl.BlockSpec((1, 1, 6 * d), lambda i: (mod_row(i), 0, 0)),
                  pl.BlockSpec((None, 4, d), lambda i: (layer, 0, 0)),
                  pl.BlockSpec((None, d, W_IN_PAD), lambda i: (layer, 0, 0)),
                  pl.BlockSpec((ROW_TILE, C_W), lambda i: (i % tiles_per_batch, 0)),
                  pl.BlockSpec((ROW_TILE, C_W), lambda i: (i % tiles_per_batch, 0))],
        out_specs=(pl.BlockSpec((ROW_TILE, PA_W), lambda i: (i, 0)),
                   pl.BlockSpec((ROW_TILE, PB_W), lambda i: (i, 0)),
                   pl.BlockSpec((ROW_TILE, PC_OUT), lambda i: (i, 0))),
        compiler_params=_cparams(("arbitrary",)),
        name="inproj",
    )(xy, mod, norms, w_in_p, cos, sin)


def _hgrn_kernel(pa_ref, lb_ref, gain_ref, o_ref, st_ref, ob_ref, *, n_ctx, n_all):
    c_sz, w = CHUNK, A_W
    r_i = lax.broadcasted_iota(jnp.int32, (c_sz, c_sz), 0)
    c_i = lax.broadcasted_iota(jnp.int32, (c_sz, c_sz), 1)
    tri_incl_past = (c_i <= r_i).astype(BF16)
    tri_incl_future = (c_i >= r_i).astype(BF16)
    same_head = (lax.broadcasted_iota(jnp.int32, (w, w), 0) // HEAD_V
                 == lax.broadcasted_iota(jnp.int32, (w, w), 1) // HEAD_V)
    head_ones = same_head.astype(BF16)
    head_ones_f32 = same_head.astype(F32)
    assert c_sz == HEAD_V
    t_i = lax.broadcasted_iota(jnp.int32, (c_sz, w), 0)
    s_i = lax.broadcasted_iota(jnp.int32, (c_sz, w), 1) % c_sz
    t_row = lax.broadcasted_iota(jnp.int32, (c_sz, 1), 0)
    row8 = t_row % 8

    def grouped_rows(a, k):
        return jnp.concatenate(
            [jnp.broadcast_to(a[8 * j + k:8 * j + k + 1, :], (8, w)) for j in range(c_sz // 8)], axis=0)

    def halving_levels(rev):
        out = []
        b = c_sz // 2
        while b >= 1:
            def later(i):
                return ((i % (2 * b)) < b) if rev else ((i % (2 * b)) >= b)
            live = jnp.logical_and(t_i // (2 * b) == s_i // (2 * b),
                                   jnp.logical_and(later(t_i), jnp.logical_not(later(s_i))))
            out.append((b, live.astype(F32), jnp.where(later(t_row), 1.0, -1.0)))
            b //= 2
        return out

    self_mask = (s_i == t_i).astype(F32)
    levels_of = (halving_levels(False), halving_levels(True))

    def chunk(c, rev):
        d = 1 if rev else 0
        levels = levels_of[d]
        r0 = pl.multiple_of(c * c_sz, c_sz)
        rows = pl.ds(r0, c_sz)
        q_pre = pa_ref[0, rows, 0:A_W]
        v = pa_ref[0, rows, A_W:2 * A_W]
        f_pre = pa_ref[0, rows, 3 * A_W:4 * A_W] if rev else pa_ref[0, rows, 2 * A_W:3 * A_W]
        lb = lb_ref[1:2, :] if rev else lb_ref[0:1, :]
        q = _silu(q_pre)
        f = lb + (1.0 - lb) * jax.nn.sigmoid(f_pre)
        log_f = jnp.log(jnp.maximum(f, F_MIN))
        kk = (1.0 - lb) * jax.nn.sigmoid(-f_pre)
        tri = tri_incl_future if rev else tri_incl_past
        cum = sum(jnp.dot(tri, piece, preferred_element_type=F32) for piece in _split3(log_f))
        e = 0 if rev else c_sz - 1
        cum_end = cum[e:e + 1, :]
        vb = v.astype(BF16)

        st = st_ref[d]
        o = _nt((q * jnp.exp(cum)).astype(BF16), st.astype(BF16))
        k_end = (kk * jnp.exp(cum_end - cum)).astype(BF16)
        st_ref[d] = st * jnp.exp(cum_end) + jnp.where(same_head, _tn(vb, k_end), 0.0)

        p = self_mask * jnp.dot((q * kk).astype(BF16), head_ones, preferred_element_type=F32)
        for b, live, sign in levels:
            if b >= 8:
                ref = jnp.concatenate(
                    [jnp.broadcast_to(cum[r0 + (b if rev else b - 1):r0 + (b if rev else b - 1) + 1, :], (2 * b, w))
                     for r0 in range(0, c_sz, 2 * b)], axis=0)
            else:
                n_ref = 8 // (2 * b)
                ref = None
                for g in reversed(range(n_ref)):
                    cand = grouped_rows(cum, g * 2 * b + (b if rev else b - 1))
                    ref = cand if ref is None else jnp.where(row8 < (g + 1) * 2 * b, cand, ref)
            z = jnp.exp((cum - ref) * sign)
            qa = (q * z).astype(BF16)
            ka = (kk * z).astype(BF16)
            ka_bd = jnp.where(same_head, jnp.concatenate([ka] * HEADS, axis=0), jnp.zeros((), BF16))
            p = p + live * _nt(qa, ka_bd)
        v_bd = jnp.where(same_head, jnp.concatenate([vb] * HEADS, axis=0), jnp.zeros((), BF16))
        o = o + jnp.dot(p.astype(BF16), v_bd, preferred_element_type=F32)
        if rev:
            ob_ref[rows, :] = o
        else:
            o_ref[0, rows, :] = o

    def scan_step(n, carry):
        chunk(n, False)
        chunk(jnp.where(n < n_ctx, n_ctx - 1 - n, n_all - 1 - (n - n_ctx)), True)
        return carry

    st_ref[...] = jnp.zeros_like(st_ref)
    lax.fori_loop(0, n_all, scan_step, 0)

    def readout(c, carry):
        rows = pl.ds(pl.multiple_of(c * ROW_TILE, ROW_TILE), ROW_TILE)
        tot = o_ref[0, rows, :] + ob_ref[rows, :]
        ms = jnp.dot(tot * tot, head_ones_f32, preferred_element_type=F32, precision=HI) * (1.0 / HEAD_V)
        g = pa_ref[0, rows, 4 * A_W:5 * A_W]
        o_ref[0, rows, :] = tot * lax.rsqrt(ms + NORM_EPS) * gain_ref[...] * _silu(g)
        return carry

    lax.fori_loop(0, n_all * c_sz // ROW_TILE, readout, 0)


def _hgrn(pa3, lb, gain, n_ctx_rows):
    n_batch, rows, _ = pa3.shape
    kern = functools.partial(_hgrn_kernel, n_ctx=n_ctx_rows // CHUNK, n_all=rows // CHUNK)
    return pl.pallas_call(
        kern,
        out_shape=jax.ShapeDtypeStruct((n_batch, rows, A_W), F32),
        grid=(n_batch,),
        in_specs=[pl.BlockSpec((1, rows, PA_W), lambda b: (b, 0, 0)),
                  pl.BlockSpec((2, A_W), lambda b: (0, 0)),
                  pl.BlockSpec((1, A_W), lambda b: (0, 0))],
        out_specs=pl.BlockSpec((1, rows, A_W), lambda b: (b, 0, 0)),
        scratch_shapes=[pltpu.VMEM((2, A_W, A_W), F32), pltpu.VMEM((rows, A_W), F32)],
        compiler_params=_cparams(("arbitrary",)),
        name="hgrn2",
    )(pa3, lb, gain)


def _split3(x):
    hi = x.astype(BF16)
    rest = x - hi.astype(F32)
    mid = rest.astype(BF16)
    return hi, mid, (rest - mid.astype(F32)).astype(BF16)


def _mlstm_kernel(pb_ref, bias_ref, gain_ref, o_ref, cn_ref, ob_ref, *, n_ctx, n_all):
    assert CHUNK == HEAD_V
    c_sz, w, qk_w = CHUNK, B_W, HEADS * B_QK
    t_i = lax.broadcasted_iota(jnp.int32, (c_sz, w), 0)
    s_i = lax.broadcasted_iota(jnp.int32, (c_sz, w), 1) % c_sz
    r_i = lax.broadcasted_iota(jnp.int32, (c_sz, c_sz), 0)
    c_i = lax.broadcasted_iota(jnp.int32, (c_sz, c_sz), 1)
    diag4 = s_i == t_i
    half_lane = lax.broadcasted_iota(jnp.int32, (1, LANES), 1) < HEAD_V
    k_block = (lax.broadcasted_iota(jnp.int32, (HEADS * c_sz, qk_w), 0) // c_sz
               == lax.broadcasted_iota(jnp.int32, (HEADS * c_sz, qk_w), 1) // B_QK)
    v_block = (lax.broadcasted_iota(jnp.int32, (HEADS * c_sz, w), 0) // c_sz
               == lax.broadcasted_iota(jnp.int32, (HEADS * c_sz, w), 1) // HEAD_V)
    state_block = (lax.broadcasted_iota(jnp.int32, (qk_w, 2 * w), 0) // B_QK
                   == (lax.broadcasted_iota(jnp.int32, (qk_w, 2 * w), 1) % w) // HEAD_V)
    head_ones = v_block.astype(BF16)
    q_off, k_off, v_off, o_off, g_off = 0, qk_w, 2 * qk_w, 2 * qk_w + B_W, 2 * qk_w + 2 * B_W

    def selector(rev):
        sr = lax.broadcasted_iota(jnp.int32, (2 * LANES, 2 * w), 0)
        sc = lax.broadcasted_iota(jnp.int32, (2 * LANES, 2 * w), 1)
        i_base = 2 * HEADS if rev else 0
        want = jnp.where(sc < w, i_base + HEADS + sc // HEAD_V, LANES + i_base + (sc - w) // HEAD_V)
        return (sr == want).astype(BF16)

    def head_max(x):
        outs = []
        for col in range(w // LANES):
            xc = x[:, col * LANES:(col + 1) * LANES]
            lo = jnp.max(jnp.where(half_lane, xc, -jnp.inf), axis=-1, keepdims=True)
            hi = jnp.max(jnp.where(half_lane, -jnp.inf, xc), axis=-1, keepdims=True)
            outs.append(jnp.where(half_lane, lo, hi))
        return jnp.concatenate(outs, axis=1)

    def chunk(c, rev, m_prev, sel):
        d = 1 if rev else 0
        rows = pl.ds(pl.multiple_of(c * c_sz, c_sz), c_sz)
        qb = (pb_ref[0, rows, q_off:q_off + qk_w] * (B_QK ** -0.5)).astype(BF16)
        kb = pb_ref[0, rows, k_off:k_off + qk_w].astype(BF16)
        v = pb_ref[0, rows, v_off:v_off + B_W]
        gts = pb_ref[0, rows, g_off:g_off + LANES] + bias_ref[...]
        log_f = jnp.minimum(gts, 0.0) - jnp.log(1.0 + jnp.exp(-jnp.abs(gts)))
        mask = (s_i >= t_i) if rev else (s_i <= t_i)
        tri = ((c_i >= r_i) if rev else (c_i <= r_i)).astype(BF16)
        cum_col = sum(jnp.dot(tri, p, preferred_element_type=F32) for p in _split3(log_f))
        both = sum(jnp.dot(p, sel, preferred_element_type=F32)
                   for p in _split3(jnp.concatenate([cum_col, gts], axis=1)))
        cum_t = both[:, :w]
        ig_t = both[:, w:]
        src = jnp.sum(jnp.where(diag4, cum_t - ig_t, 0.0), axis=0, keepdims=True)
        log_d = jnp.where(mask, cum_t - src, MASK_NEG)
        log_inter = cum_t + m_prev
        m_t = jnp.maximum(log_inter, head_max(log_d))
        w_intra = jnp.where(mask, jnp.exp(log_d - m_t), 0.0)
        w_inter = jnp.exp(log_inter - m_t)

        k_bd = jnp.where(k_block, jnp.concatenate([kb] * HEADS, axis=0), jnp.zeros((), BF16))
        p = (_nt(qb, k_bd) * w_intra).astype(BF16)
        vb = v.astype(BF16)
        v_bd = jnp.where(v_block, jnp.concatenate([vb] * HEADS, axis=0), jnp.zeros((), BF16))
        cn = cn_ref[d]
        inter = jnp.dot(qb, cn.astype(BF16), preferred_element_type=F32)
        num = w_inter * inter[:, :w] + jnp.dot(p, v_bd, preferred_element_type=F32)
        den = w_inter * inter[:, w:] + jnp.dot(p, head_ones, preferred_element_type=F32)
        h_out = num / jnp.maximum(jnp.abs(den), jnp.exp(-m_t))

        e = 0 if rev else c_sz - 1
        cum_e = cum_t[e:e + 1, :]
        log_end = cum_e - cum_t + ig_t
        m_end = jnp.maximum(cum_e + m_prev, jnp.max(log_end, axis=0, keepdims=True))
        w_end = jnp.exp(log_end - m_end)
        w_carry = jnp.exp(cum_e + m_prev - m_end)
        upd = _tn(kb, jnp.concatenate([(w_end * v).astype(BF16), w_end.astype(BF16)], axis=1))
        cn_ref[d] = jnp.concatenate([w_carry, w_carry], axis=1) * cn + jnp.where(state_block, upd, 0.0)
        if rev:
            ob_ref[rows, :] = h_out
        else:
            o_ref[0, rows, :] = h_out
        return m_end

    sel_fwd = selector(False)
    sel_bwd = selector(True)

    def scan_step(n, carry):
        m_fwd, m_bwd = carry
        m_fwd = chunk(n, False, m_fwd, sel_fwd)
        m_bwd = chunk(jnp.where(n < n_ctx, n_ctx - 1 - n, n_all - 1 - (n - n_ctx)), True, m_bwd, sel_bwd)
        return m_fwd, m_bwd

    zero_m = jnp.zeros((1, w), F32)
    cn_ref[...] = jnp.zeros_like(cn_ref)
    lax.fori_loop(0, n_all, scan_step, (zero_m, zero_m), unroll=2)

    def readout(c, carry):
        rows = pl.ds(pl.multiple_of(c * ROW_TILE, ROW_TILE), ROW_TILE)
        tot = o_ref[0, rows, :] + ob_ref[rows, :]
        normed = jnp.concatenate(
            [_rms(tot[:, h * HEAD_V:(h + 1) * HEAD_V]) for h in range(HEADS)], axis=1) * gain_ref[...]
        og = pb_ref[0, rows, o_off:o_off + B_W]
        o_ref[0, rows, :] = normed * jax.nn.sigmoid(og)
        return carry

    lax.fori_loop(0, n_all * c_sz // ROW_TILE, readout, 0)


def _mlstm(pb3, bias, gain, n_ctx_rows):
    n_batch, rows, _ = pb3.shape
    kern = functools.partial(_mlstm_kernel, n_ctx=n_ctx_rows // CHUNK, n_all=rows // CHUNK)
    return pl.pallas_call(
        kern,
        out_shape=jax.ShapeDtypeStruct((n_batch, rows, B_W), F32),
        grid=(n_batch,),
        in_specs=[pl.BlockSpec((1, rows, PB_W), lambda b: (b, 0, 0)),
                  pl.BlockSpec((1, LANES), lambda b: (0, 0)),
                  pl.BlockSpec((1, B_W), lambda b: (0, 0))],
        out_specs=pl.BlockSpec((1, rows, B_W), lambda b: (b, 0, 0)),
        scratch_shapes=[pltpu.VMEM((2, HEADS * B_QK, 2 * B_W), F32), pltpu.VMEM((rows, B_W), F32)],
        compiler_params=_cparams(("arbitrary",)),
        name="mlstm",
    )(pb3, bias, gain)


def _attn_kernel(lam_ref, qp_ref, qr_ref, k_ref, v_ref, gain_ref, o_ref, *, n_ctx, q_tile0, lam_init):
    lam = lam_ref[0]
    q_tile = pl.program_id(2) + q_tile0
    lane = lax.broadcasted_iota(jnp.int32, (1, 2 * C_DQK), 1)
    scale = C_DQK ** -0.5

    def finish(parts):
        o = parts[0] - lam * parts[1]
        o_ref[0] = _rms(o) * gain_ref[...] * (1.0 - lam_init)

    def sub_query(q, j):
        return jnp.where(lane // C_DQK == j, q * scale, jnp.zeros_like(q))

    def row_max(s):
        return jnp.max(s, axis=-1, keepdims=True)

    def row_sum(s):
        return jnp.sum(s, axis=-1, keepdims=True)

    def pv(ex, v):
        return jnp.dot(ex.astype(BF16), v, preferred_element_type=F32)

    @pl.when(q_tile * ROW_TILE < n_ctx)
    def _():
        kc = k_ref[0, 0:n_ctx, :]
        vc = v_ref[0, 0:n_ctx, :]
        parts = []
        for j in range(2):
            s = _nt(sub_query(qp_ref[0], j), kc)
            ex = jnp.exp(s - row_max(s))
            parts.append(pv(ex, vc) / row_sum(ex))
        finish(parts)

    @pl.when(q_tile * ROW_TILE >= n_ctx)
    def _():
        n_all = k_ref.shape[1]
        kc = k_ref[0, 0:n_ctx, :]
        kl = k_ref[0, n_ctx:n_all, :]
        vc = v_ref[0, 0:n_ctx, :]
        vl = v_ref[0, n_ctx:n_all, :]
        parts = []
        for j in range(2):
            s_c = _nt(sub_query(qp_ref[0], j), kc)
            s_l = _nt(sub_query(qr_ref[0], j), kl)
            m = jnp.maximum(row_max(s_c), row_max(s_l))
            e_c = jnp.exp(s_c - m)
            e_l = jnp.exp(s_l - m)
            parts.append((pv(e_c, vc) + pv(e_l, vl)) / (row_sum(e_c) + row_sum(e_l)))
        finish(parts)


def _attention(pc3, lam, gain, n_ctx_rows, lam_init, with_ctx):
    n_batch, rows, _ = pc3.shape
    q_tile0 = 0 if with_ctx else n_ctx_rows // ROW_TILE
    n_q = rows // ROW_TILE - q_tile0
    hb = C_W // LANES
    kern = functools.partial(_attn_kernel, n_ctx=n_ctx_rows, q_tile0=q_tile0, lam_init=lam_init)
    grid_spec = pltpu.PrefetchScalarGridSpec(
        num_scalar_prefetch=1,
        grid=(n_batch, HEADS, n_q),
        in_specs=[pl.BlockSpec((1, ROW_TILE, LANES), lambda b, h, i, lam: (b, i + q_tile0, h)),
                  pl.BlockSpec((1, ROW_TILE, LANES), lambda b, h, i, lam: (b, i + q_tile0, hb + h)),
                  pl.BlockSpec((1, rows, LANES), lambda b, h, i, lam: (b, 0, 2 * hb + h)),
                  pl.BlockSpec((1, rows, LANES), lambda b, h, i, lam: (b, 0, 3 * hb + h)),
                  pl.BlockSpec((1, LANES), lambda b, h, i, lam: (0, 0))],
        out_specs=pl.BlockSpec((1, ROW_TILE, LANES), lambda b, h, i, lam: (b, i, h)),
    )
    return pl.pallas_call(
        kern,
        out_shape=jax.ShapeDtypeStruct((n_batch, n_q * ROW_TILE, C_W), F32),
        grid_spec=grid_spec,
        compiler_params=_cparams(("arbitrary", "arbitrary", "arbitrary")),
        name="diff_attn",
    )(lam, pc3, pc3, pc3, pc3, gain)


def _outproj_kernel(oa_ref, ob_ref, oc_ref, x_ref, mod_ref, n_ref, w_ref, rw_ref, rb_ref,
                    xn_ref, h_ref, route_ref, cnt_ref, carry_ref):
    d = x_ref.shape[1]
    n_exp = rw_ref.shape[1]

    @pl.when(pl.program_id(0) == 0)
    def _():
        carry_ref[...] = jnp.zeros_like(carry_ref)

    mix = (jnp.dot(oa_ref[...].astype(BF16), w_ref[0:A_W, :], preferred_element_type=F32)
           + jnp.dot(ob_ref[...].astype(BF16), w_ref[A_W:A_W + B_W, :], preferred_element_type=F32)
           + jnp.dot(oc_ref[...].astype(BF16), w_ref[A_W + B_W:A_W + B_W + C_W, :], preferred_element_type=F32))
    xn = x_ref[...] + mod_ref[0, :, 2 * d:3 * d] * (_rms(mix) * n_ref[1:2, :])
    xn_ref[...] = xn
    h = _rms(xn) * n_ref[2:3, :] * (1.0 + mod_ref[0, :, 4 * d:5 * d]) + mod_ref[0, :, 3 * d:4 * d]
    bits = pltpu.bitcast(h.astype(BF16).astype(F32), jnp.uint32)
    words = (bits[:, :d // 2] >> 16) | (bits[:, d // 2:] & jnp.uint32(0xFFFF0000))
    h_ref[...] = pltpu.bitcast(words, jnp.int32)

    h_hi = h.astype(BF16)
    h_mid = (h - h_hi.astype(F32)).astype(BF16)
    logits = _nt(rw_ref[0], h_hi) + _nt(rw_ref[1], h_hi) + _nt(rw_ref[0], h_mid) + rb_ref[...]
    e_sub = lax.broadcasted_iota(jnp.int32, logits.shape, 0)
    cur = logits
    picks, vals = [], []
    for _ in range(TOP_K):
        mx = jnp.max(cur, axis=0, keepdims=True)
        idx = jnp.min(jnp.where(cur == mx, e_sub, n_exp), axis=0, keepdims=True)
        hit = e_sub == idx
        cur = jnp.where(hit, -jnp.inf, cur)
        picks.append((idx, hit.astype(F32)))
        vals.append(mx)
    exps = [jnp.exp(vv - vals[0]) for vv in vals]
    total = exps[0] + exps[1] + exps[2] + exps[3]

    chosen = picks[0][1] + picks[1][1] + picks[2][1] + picks[3][1]
    tm = logits.shape[1]
    before = (lax.broadcasted_iota(jnp.int32, (tm, tm), 0) < lax.broadcasted_iota(jnp.int32, (tm, tm), 1))
    seen = jnp.dot(chosen.astype(BF16), before.astype(BF16), preferred_element_type=F32) + carry_ref[...]
    new_carry = carry_ref[...] + jnp.sum(chosen, axis=1, keepdims=True)
    carry_ref[...] = new_carry
    cnt_ref[...] = new_carry

    o_row = lax.broadcasted_iota(jnp.int32, route_ref.shape, 0)
    route = jnp.zeros(route_ref.shape, F32)
    for kk in range(TOP_K):
        idx, hit = picks[kk]
        rank = jnp.sum(hit * seen, axis=0, keepdims=True)
        route = (route + jnp.where(o_row == kk, idx.astype(F32), 0.0)
                 + jnp.where(o_row == TOP_K + kk, exps[kk] / total, 0.0)
                 + jnp.where(o_row == 2 * TOP_K + kk, rank, 0.0))
    route_ref[...] = route


ROUTE_ROWS = 16


def _outproj(oa, ob, oc, xy, mod, norms, w_out_b, router_wt, router_b, layer, n_tiles, in_tile, oc_tile, mod_row):
    d = xy.shape[1]
    n_exp = router_wt.shape[2]
    t_out = n_tiles * ROW_TILE
    xn, h, route_t, counts = pl.pallas_call(
        _outproj_kernel,
        out_shape=(jax.ShapeDtypeStruct((t_out, d), F32),
                   jax.ShapeDtypeStruct((t_out, d // 2), jnp.int32),
                   jax.ShapeDtypeStruct((n_tiles * ROUTE_ROWS, ROW_TILE), F32),
                   jax.ShapeDtypeStruct((n_exp, 1), F32)),
        grid=(n_tiles,),
        in_specs=[pl.BlockSpec((ROW_TILE, A_W), lambda i: (in_tile(i), 0)),
                  pl.BlockSpec((ROW_TILE, B_W), lambda i: (in_tile(i), 0)),
                  pl.BlockSpec((ROW_TILE, C_W), lambda i: (oc_tile(i), 0)),
                  pl.BlockSpec((ROW_TILE, d), lambda i: (in_tile(i), 0)),
                  pl.BlockSpec((1, 1, 6 * d), lambda i: (mod_row(i), 0, 0)),
                  pl.BlockSpec((None, 4, d), lambda i: (layer, 0, 0)),
                  pl.BlockSpec((None, d, d), lambda i: (layer, 0, 0)),
                  pl.BlockSpec((None, 2, n_exp, d), lambda i: (layer, 0, 0, 0)),
                  pl.BlockSpec((None, n_exp, 1), lambda i: (layer, 0, 0))],
        out_specs=(pl.BlockSpec((ROW_TILE, d), lambda i: (i, 0)),
                   pl.BlockSpec((ROW_TILE, d // 2), lambda i: (i, 0)),
                   pl.BlockSpec((ROUTE_ROWS, ROW_TILE), lambda i: (i, 0)),
                   pl.BlockSpec((n_exp, 1), lambda i: (0, 0))),
        scratch_shapes=[pltpu.VMEM((n_exp, 1), F32)],
        compiler_params=_cparams(("arbitrary",)),
        name="outproj_router",
    )(oa, ob, oc, xy, mod, norms, w_out_b, router_wt, router_b)
    return xn, h, route_t.reshape(n_tiles, ROUTE_ROWS, ROW_TILE), counts[:, 0]


PAIR_BLOCK = 2 * LANES


def _moe_kernel(te_ref, first_ref, valid_ref, nu_ref, xs_ref, w1_ref, b1_ref, w2_ref, b2_ref, ys_ref,
                w1p_ref, w2b_ref):
    i = pl.program_id(0)
    two_f = w1_ref.shape[1]
    n_blk = two_f // PAIR_BLOCK

    @pl.when(jnp.logical_and(i < nu_ref[0], first_ref[i] == 1))
    def _():
        r = lax.broadcasted_iota(jnp.int32, (PAIR_BLOCK, PAIR_BLOCK), 0)
        c = lax.broadcasted_iota(jnp.int32, (PAIR_BLOCK, PAIR_BLOCK), 1)
        perm = (r == jnp.where(c < LANES, 2 * c, 2 * (c - LANES) + 1)).astype(BF16)
        for blk in range(n_blk):
            cols = slice(blk * PAIR_BLOCK, (blk + 1) * PAIR_BLOCK)
            w1p_ref[:, cols] = jnp.dot(w1_ref[:, cols].astype(BF16), perm,
                                       preferred_element_type=F32).astype(BF16)
        w2b_ref[...] = w2_ref[...].astype(BF16)

    @pl.when(i < nu_ref[0])
    def _():
        words = pltpu.bitcast(xs_ref[...], jnp.uint32)
        x = jnp.concatenate([pltpu.bitcast(words << 16, F32),
                             pltpu.bitcast(words & jnp.uint32(0xFFFF0000), F32)], axis=1).astype(BF16)
        row = lax.broadcasted_iota(jnp.int32, (x.shape[0], 1), 0)
        x = jnp.where(row < valid_ref[i], x, jnp.zeros_like(x))
        hid = jnp.dot(x, w1p_ref[...], preferred_element_type=F32) + b1_ref[...]
        acts = []
        for blk in range(n_blk):
            glu = jnp.minimum(hid[:, blk * PAIR_BLOCK:blk * PAIR_BLOCK + LANES], SWIGLU_LIMIT)
            lin = jnp.clip(hid[:, blk * PAIR_BLOCK + LANES:(blk + 1) * PAIR_BLOCK], -SWIGLU_LIMIT, SWIGLU_LIMIT)
            acts.append((glu * jax.nn.sigmoid(SWIGLU_ALPHA * glu) * (lin + 1.0)).astype(BF16))
        y = jnp.dot(jnp.concatenate(acts, axis=1), w2b_ref[...], preferred_element_type=F32) + b2_ref[...]
        ys_ref[...] = y.astype(ys_ref.dtype)

    @pl.when(i >= nu_ref[0])
    def _():
        ys_ref[...] = jnp.zeros_like(ys_ref)


def _moe(tile_expert, tile_first, tile_valid, n_used, xs, w1, b1p, w2, b2, layer):
    r_max, half_d = xs.shape
    d = 2 * half_d
    two_f = w1.shape[3]
    f = two_f // 2
    n_tiles = r_max // MOE_TILE

    def row_tile(i, te, fi, va, nu):
        return (jnp.maximum(jnp.minimum(i, nu[0] - 1), 0), 0)

    grid_spec = pltpu.PrefetchScalarGridSpec(
        num_scalar_prefetch=4,
        grid=(n_tiles,),
        in_specs=[pl.BlockSpec((MOE_TILE, half_d), row_tile),
                  pl.BlockSpec((None, None, d, two_f), lambda i, te, fi, va, nu: (layer, te[i], 0, 0)),
                  pl.BlockSpec((None, None, 1, two_f), lambda i, te, fi, va, nu: (layer, te[i], 0, 0)),
                  pl.BlockSpec((None, None, f, d), lambda i, te, fi, va, nu: (layer, te[i], 0, 0)),
                  pl.BlockSpec((None, None, 1, d), lambda i, te, fi, va, nu: (layer, te[i], 0, 0))],
        out_specs=pl.BlockSpec((MOE_TILE, d), lambda i, te, fi, va, nu: (i, 0)),
        scratch_shapes=[pltpu.VMEM((d, two_f), BF16), pltpu.VMEM((f, d), BF16)],
    )
    return pl.pallas_call(
        _moe_kernel,
        out_shape=jax.ShapeDtypeStruct((r_max, d), BF16),
        grid_spec=grid_spec,
        compiler_params=_cparams(("arbitrary",)),
        name="moe_experts",
    )(tile_expert, tile_first, tile_valid, n_used, xs, w1, b1p, w2, b2)


SC_CORES = 2
SC_SUBCORES = 16
SC_CHUNK = 64


def _dispatch_rows(hp, pos, r_max):
    t, width = hp.shape
    workers = SC_CORES * SC_SUBCORES
    assert t % (workers * SC_CHUNK) == 0
    per_worker = t // (workers * SC_CHUNK)
    idx = pos.reshape(-1, TOP_K, ROW_TILE // SC_CHUNK, SC_CHUNK).transpose(0, 2, 1, 3).reshape(
        workers, per_worker * TOP_K, SC_CHUNK)
    mesh = plsc.VectorSubcoreMesh(core_axis_name="c", subcore_axis_name="s",
                                  num_cores=SC_CORES, num_subcores=SC_SUBCORES)

    @functools.partial(
        pl.kernel, mesh=mesh,
        out_type=jax.ShapeDtypeStruct((r_max, width), hp.dtype),
        scratch_types=[pltpu.VMEM((per_worker * TOP_K, SC_CHUNK), jnp.int32),
                       pltpu.VMEM((2, SC_CHUNK, width), hp.dtype),
                       pltpu.SemaphoreType.DMA((2,)),
                       pltpu.SemaphoreType.DMA((2,))],
    )
    def scatter(hp_hbm, idx_hbm, out_hbm, idx_v, rows_v, read_sem, write_sem):
        wid = lax.axis_index("s") * SC_CORES + lax.axis_index("c")
        pltpu.sync_copy(idx_hbm.at[wid], idx_v)

        def read(j):
            row0 = pl.multiple_of((wid * per_worker + j) * SC_CHUNK, SC_CHUNK)
            return pltpu.make_async_copy(hp_hbm.at[pl.ds(row0, SC_CHUNK)], rows_v.at[j % 2], read_sem.at[j % 2])

        def write(j, k):
            return pltpu.make_async_copy(rows_v.at[j % 2], out_hbm.at[idx_v.at[j * TOP_K + k]],
                                         write_sem.at[j % 2])

        read(0).start()
        for j in range(per_worker):
            read(j).wait()
            for k in range(TOP_K):
                write(j, k).start()
            if j + 1 < per_worker:
                if j >= 1:
                    for k in range(TOP_K):
                        write(j - 1, k).wait()
                read(j + 1).start()
        for j in range(max(per_worker - 2, 0), per_worker):
            for k in range(TOP_K):
                write(j, k).wait()

    return scatter(hp, idx)


def _ffn_residual_kernel(x_ref, w_ref, y0_ref, y1_ref, y2_ref, y3_ref, mod_ref, n_ref, o_ref):
    d = x_ref.shape[1]
    wts = w_ref[...]
    ffn = None
    for k, y_ref in enumerate((y0_ref, y1_ref, y2_ref, y3_ref)):
        term = wts[:, k:k + 1] * y_ref[...].astype(F32)
        ffn = term if ffn is None else ffn + term
    o_ref[...] = x_ref[...] + mod_ref[0, :, 5 * d:6 * d] * (_rms(ffn) * n_ref[3:4, :])


def _ffn_residual(xn, wts, ys_by_k, mod, norms, layer, mod_row):
    t, d = xn.shape
    row_spec = pl.BlockSpec((ROW_TILE, d), lambda i: (i, 0))
    return pl.pallas_call(
        _ffn_residual_kernel,
        out_shape=jax.ShapeDtypeStruct((t, d), F32),
        grid=(t // ROW_TILE,),
        in_specs=[row_spec,
                  pl.BlockSpec((ROW_TILE, TOP_K), lambda i: (i, 0)),
                  row_spec, row_spec, row_spec, row_spec,
                  pl.BlockSpec((1, 1, 6 * d), lambda i: (mod_row(i), 0, 0)),
                  pl.BlockSpec((None, 4, d), lambda i: (layer, 0, 0))],
        out_specs=row_spec,
        compiler_params=_cparams(("arbitrary",)),
        name="ffn_residual",
    )(xn, wts, *ys_by_k, mod, norms)


def _dispatch_plan(route_t, counts, r_max):
    n_row_tiles = route_t.shape[0]
    idx = route_t[:, 0:TOP_K, :].astype(jnp.int32)
    wts = route_t[:, TOP_K:2 * TOP_K, :].transpose(0, 2, 1).reshape(n_row_tiles * ROW_TILE, TOP_K)
    rank = route_t[:, 2 * TOP_K:3 * TOP_K, :].astype(jnp.int32)
    cnt = counts.astype(jnp.int32)
    padded = (cnt + MOE_TILE - 1) // MOE_TILE * MOE_TILE
    ends = jnp.cumsum(padded)
    starts = ends - padded
    pos = rank
    for e in range(cnt.shape[0]):
        pos = pos + jnp.where(idx == e, starts[e], 0)
    n_tiles = r_max // MOE_TILE
    n_used = ends[-1] // MOE_TILE
    tile_ids = jnp.minimum(jnp.arange(n_tiles, dtype=jnp.int32), n_used - 1)
    tile_expert = jnp.sum((ends // MOE_TILE)[None, :] <= tile_ids[:, None], axis=1).astype(jnp.int32)
    tile_first = jnp.concatenate(
        [jnp.ones((1,), jnp.int32), (tile_expert[1:] != tile_expert[:-1]).astype(jnp.int32)])
    of_expert = (tile_expert[:, None] == jnp.arange(cnt.shape[0], dtype=jnp.int32)[None, :]).astype(jnp.int32)
    first_tile = jnp.sum(of_expert * ((ends - padded) // MOE_TILE)[None, :], axis=1)
    tile_valid = jnp.clip(jnp.sum(of_expert * cnt[None, :], axis=1) - (tile_ids - first_tile) * MOE_TILE,
                          0, MOE_TILE).astype(jnp.int32)
    return pos, wts, tile_expert, tile_first, tile_valid, n_used.reshape(1).astype(jnp.int32)


def _rope_tables(length, n_ctx_rows):
    rows = length // GRID_W
    row = jnp.repeat(jnp.arange(rows, dtype=F32), GRID_W)
    col = jnp.tile(jnp.arange(GRID_W, dtype=F32), rows)
    n_freq = C_DQK // 4
    inv_freq = ROPE_THETA ** (-jnp.arange(n_freq, dtype=F32) / n_freq)
    ang_r = row[:, None] * inv_freq
    ang_c = col[:, None] * inv_freq
    cos = jnp.concatenate([jnp.cos(ang_r), jnp.cos(ang_r), jnp.cos(ang_c), jnp.cos(ang_c)], axis=-1)
    sin = jnp.concatenate([-jnp.sin(ang_r), jnp.sin(ang_r), -jnp.sin(ang_c), jnp.sin(ang_c)], axis=-1)
    reps = C_W // C_DQK
    cos = jnp.concatenate([jnp.ones((n_ctx_rows, C_W), F32), jnp.tile(cos, (1, reps))], axis=0)
    sin = jnp.concatenate([jnp.zeros((n_ctx_rows, C_W), F32), jnp.tile(sin, (1, reps))], axis=0)
    return cos, sin


def _hgrn_lower_bound(table, layer):
    p = jax.nn.softmax(table.astype(F32), axis=1)
    cum = jnp.cumsum(p, axis=1) - p[:, :1]
    return jnp.clip(cum[:, layer], 0.0, 1.0)


def kernel(x, c, ctx, c_ctx, ada_w, ada_b, sandwich_norms, w_in, w_out, hgrn_lower_bounds, hgrn_norm,
           mlstm_gate_bias, mlstm_norm, diff_lambdas, diff_norm, router_w, router_b, moe_w1, moe_b1,
           moe_w2, moe_b2):
    n_batch, seq, d = x.shape
    n_ctx_rows = ctx.shape[1]
    depth = w_in.shape[0]
    n_exp = router_w.shape[2]
    assert seq % ROW_TILE == 0 and n_ctx_rows % ROW_TILE == 0 and seq % GRID_W == 0
    rows_b = n_ctx_rows + seq
    tiles_b = rows_b // ROW_TILE
    ctx_tiles = n_ctx_rows // ROW_TILE
    lat_tiles = seq // ROW_TILE

    b_main = PA_W + 2 * HEADS * B_QK + 2 * B_W
    w_in_p = jnp.concatenate(
        [w_in[:, :, :b_main],
         jnp.pad(w_in[:, :, b_main:b_main + N_GATE], ((0, 0), (0, 0), (0, LANES - N_GATE))),
         w_in[:, :, b_main + N_GATE:]], axis=2).astype(BF16)
    w_out_b = w_out.astype(BF16)
    two_f = moe_b1.shape[2]
    b1p = moe_b1.reshape(depth, n_exp, two_f // PAIR_BLOCK, LANES, 2).transpose(0, 1, 2, 4, 3).reshape(
        depth, n_exp, 1, two_f)
    b2 = moe_b2[:, :, None, :]
    ada_b3 = ada_b[:, None, :]
    router_b3 = router_b[:, :, None]
    rw_t = router_w.transpose(0, 2, 1)
    rw_hi = rw_t.astype(BF16)
    router_wt = jnp.stack([rw_hi, (rw_t - rw_hi.astype(F32)).astype(BF16)], axis=1)
    gate_bias = jnp.pad(mlstm_gate_bias, ((0, 0), (0, LANES - N_GATE)))
    cos, sin = _rope_tables(seq, n_ctx_rows)

    cond_rows = (n_batch + 1 + 7) // 8 * 8
    cvec = jnp.zeros((cond_rows, d), F32).at[:n_batch].set(c).at[n_batch].set(c_ctx)

    def mod_row_all(i):
        return jnp.where(i % tiles_b < ctx_tiles, n_batch, i // tiles_b)

    xy = jnp.concatenate([ctx, x], axis=1).reshape(n_batch * rows_b, d)

    for layer in range(depth):
        last = layer == depth - 1
        mod = _adaln(cvec, ada_w, ada_b3, layer).reshape(cond_rows, 1, 6 * d)
        lb = _hgrn_lower_bound(hgrn_lower_bounds, layer)
        lam_init = 0.8 - 0.6 * math.exp(-0.3 * layer)
        lq1, lk1, lq2, lk2 = diff_lambdas[layer].astype(F32)
        lam = (jnp.exp(jnp.sum(lq1 * lk1)) - jnp.exp(jnp.sum(lq2 * lk2)) + lam_init).reshape(1)

        pa, pb, pc = _inproj(xy, mod, sandwich_norms, w_in_p, cos, sin, layer, tiles_b, ctx_tiles, n_batch)
        oa = _hgrn(pa.reshape(n_batch, rows_b, PA_W), lb, jnp.tile(hgrn_norm[layer], HEADS)[None, :], n_ctx_rows)
        ob = _mlstm(pb.reshape(n_batch, rows_b, PB_W), gate_bias[layer][None, :], mlstm_norm[layer][None, :],
                    n_ctx_rows)
        oc = _attention(pc.reshape(n_batch, rows_b, PC_OUT), lam, diff_norm[layer][None, :], n_ctx_rows,
                        lam_init, not last)

        if last:
            n_tiles = n_batch * lat_tiles
            in_tile = lambda i: (i // lat_tiles) * tiles_b + ctx_tiles + i % lat_tiles
            mod_row = lambda i: i // lat_tiles
        else:
            n_tiles = n_batch * tiles_b
            in_tile = lambda i: i
            mod_row = mod_row_all
        xn, h, route, counts = _outproj(
            oa.reshape(-1, A_W), ob.reshape(-1, B_W), oc.reshape(-1, C_W), xy, mod, sandwich_norms, w_out_b,
            router_wt, router_b3, layer, n_tiles, in_tile, lambda i: i, mod_row)

        r_max = n_tiles * ROW_TILE * TOP_K + n_exp * MOE_TILE
        pos, wts, tile_expert, tile_first, tile_valid, n_used = _dispatch_plan(route, counts, r_max)
        xs = _dispatch_rows(h, pos, r_max)
        ys = _moe(tile_expert, tile_first, tile_valid, n_used, xs, moe_w1, b1p, moe_w2, b2, layer)
        ys_by_k = [ys.at[pos[:, k, :].reshape(-1)].get(mode="promise_in_bounds") for k in range(TOP_K)]
        xy = _ffn_residual(xn, wts, ys_by_k, mod, sandwich_norms, layer, mod_row)

    return xy.reshape(n_batch, seq, d)
```

```python
import functools
import math

import jax
import jax.numpy as jnp
from jax import lax
from jax.experimental import pallas as pl
from jax.experimental.pallas import tpu as pltpu
from jax.experimental.pallas import tpu_sc as plsc

F32 = jnp.float32
BF16 = jnp.bfloat16
HI = lax.Precision.HIGHEST

HEADS = 4
A_W = 256
B_QK = 32
B_W = 256
C_DQK = 64
C_DV = 2 * C_DQK
C_W = 512
HEAD_V = 64
N_GATE = 16
GRID_W = 64
TOP_K = 4
SWIGLU_ALPHA = 1.702
SWIGLU_LIMIT = 7.0
ROPE_THETA = 10000.0
NORM_EPS = 1e-6
MASK_NEG = -1e30
F_MIN = 1e-12

LANES = 128
ROW_TILE = 256
CHUNK = 64
MOE_TILE = 512
VMEM_LIMIT = 56 * 1024 * 1024

PA_W = 5 * A_W
PB_W = 2 * HEADS * B_QK + 2 * B_W + LANES
PC_IN = 3 * C_W
PC_OUT = 4 * C_W
W_IN_PAD = PA_W + PB_W + PC_IN


def _cparams(sem, flags=None):
    return pltpu.CompilerParams(dimension_semantics=sem, vmem_limit_bytes=VMEM_LIMIT, flags=flags)


def _nt(a, b):
    return lax.dot_general(a, b, (((1,), (1,)), ((), ())), preferred_element_type=F32)


def _tn(a, b, precision=None):
    return lax.dot_general(a, b, (((0,), (0,)), ((), ())), preferred_element_type=F32, precision=precision)


def _rms(x):
    return x * lax.rsqrt(jnp.mean(x * x, axis=-1, keepdims=True) + NORM_EPS)


def _silu(x):
    return x * jax.nn.sigmoid(x)


def _pack_bf16_pairs(x):
    half = x.shape[1] // 2
    bits = pltpu.bitcast(x.astype(BF16).astype(F32), jnp.uint32)
    return pltpu.bitcast((bits[:, :half] >> 16) | (bits[:, half:] & jnp.uint32(0xFFFF0000)), jnp.int32)


def _unpack_bf16_pairs(words):
    bits = pltpu.bitcast(words, jnp.uint32)
    return jnp.concatenate([pltpu.bitcast(bits << 16, F32),
                            pltpu.bitcast(bits & jnp.uint32(0xFFFF0000), F32)], axis=1)


def _adaln_kernel(c_ref, w_ref, b_ref, o_ref):
    cond = _silu(c_ref[...])
    o_ref[...] = jnp.dot(cond, w_ref[...], preferred_element_type=F32, precision=HI) + b_ref[...]


def _adaln(cvec, ada_w, ada_b, layer):
    rows, d = cvec.shape
    return pl.pallas_call(
        _adaln_kernel,
        out_shape=jax.ShapeDtypeStruct((rows, 6 * d), F32),
        grid=(6,),
        in_specs=[pl.BlockSpec((rows, d), lambda j: (0, 0)),
                  pl.BlockSpec((None, d, d), lambda j: (layer, 0, j)),
                  pl.BlockSpec((None, 1, d), lambda j: (layer, 0, j))],
        out_specs=pl.BlockSpec((rows, d), lambda j: (0, j)),
        compiler_params=_cparams(("arbitrary",)),
        name="adaln",
    )(cvec, ada_w, ada_b)


def _inproj_kernel(x_ref, mod_ref, g_ref, w_ref, cos_ref, sin_ref, pa_ref, pb_ref, pc_ref):
    d = x_ref.shape[1]
    shift = mod_ref[0, :, 0:d]
    scale = mod_ref[0, :, d:2 * d]
    h = _rms(x_ref[...]) * g_ref[0:1, :] * (1.0 + scale) + shift
    hb = h.astype(BF16)
    pa_ref[...] = jnp.dot(hb, w_ref[:, 0:PA_W], preferred_element_type=F32)
    pb_ref[...] = jnp.dot(hb, w_ref[:, PA_W:PA_W + PB_W], preferred_element_type=F32)
    pc = jnp.dot(hb, w_ref[:, PA_W + PB_W:W_IN_PAD], preferred_element_type=F32)
    q = pc[:, 0:C_W]
    k = pc[:, C_W:2 * C_W]
    cos = cos_ref[...]
    sin = sin_ref[...]
    lane = lax.broadcasted_iota(jnp.int32, q.shape, 1)
    first = (lane % 32) < 16

    def rope(t):
        partner = jnp.where(first, pltpu.roll(t, C_W - 16, 1), pltpu.roll(t, 16, 1))
        return t * cos + partner * sin

    pc_ref[:, 0:C_W] = q.astype(BF16)
    pc_ref[:, C_W:2 * C_W] = rope(q).astype(BF16)
    pc_ref[:, 2 * C_W:3 * C_W] = rope(k).astype(BF16)
    pc_ref[:, 3 * C_W:4 * C_W] = pc[:, 2 * C_W:3 * C_W].astype(BF16)


def _inproj(xy, mod, norms, w_in_p, cos, sin, layer, tiles_per_batch, n_ctx_tiles, n_batch):
    t_all, d = xy.shape
    n_tiles = t_all // ROW_TILE

    def mod_row(i):
        return jnp.where(i % tiles_per_batch < n_ctx_tiles, n_batch, i // tiles_per_batch)

    return pl.pallas_call(
        _inproj_kernel,
        out_shape=(jax.ShapeDtypeStruct((t_all, PA_W), F32),
                   jax.ShapeDtypeStruct((t_all, PB_W), F32),
                   jax.ShapeDtypeStruct((t_all, PC_OUT), BF16)),
        grid=(n_tiles,),
        in_specs=[pl.BlockSpec((ROW_TILE, d), lambda i: (i, 0)),
                  pl.BlockSpec((1, 1, 6 * d), lambda i: (mod_row(i), 0, 0)),
                  pl.BlockSpec((None, 4, d), lambda i: (layer, 0, 0)),
                  pl.BlockSpec((None, d, W_IN_PAD), lambda i: (layer, 0, 0)),
                  pl.BlockSpec((ROW_TILE, C_W), lambda i: (i % tiles_per_batch, 0)),
                  pl.BlockSpec((ROW_TILE, C_W), lambda i: (i % tiles_per_batch, 0))],
        out_specs=(pl.BlockSpec((ROW_TILE, PA_W), lambda i: (i, 0)),
                   pl.BlockSpec((ROW_TILE, PB_W), lambda i: (i, 0)),
                   pl.BlockSpec((ROW_TILE, PC_OUT), lambda i: (i, 0))),
        compiler_params=_cparams(("arbitrary",)),
        name="inproj",
    )(xy, mod, norms, w_in_p, cos, sin)


def _hgrn_kernel(pa_ref, lb_ref, gain_ref, o_ref, st_ref, ob_ref, *, n_ctx, n_all):
    c_sz, w = CHUNK, A_W
    r_i = lax.broadcasted_iota(jnp.int32, (c_sz, c_sz), 0)
    c_i = lax.broadcasted_iota(jnp.int32, (c_sz, c_sz), 1)
    tri_incl_past = (c_i <= r_i).astype(BF16)
    tri_incl_future = (c_i >= r_i).astype(BF16)
    same_head = (lax.broadcasted_iota(jnp.int32, (w, w), 0) // HEAD_V
                 == lax.broadcasted_iota(jnp.int32, (w, w), 1) // HEAD_V)
    head_ones = same_head.astype(BF16)
    head_ones_f32 = same_head.astype(F32)
    assert c_sz == HEAD_V
    t_i = lax.broadcasted_iota(jnp.int32, (c_sz, w), 0)
    s_i = lax.broadcasted_iota(jnp.int32, (c_sz, w), 1) % c_sz
    t_row = lax.broadcasted_iota(jnp.int32, (c_sz, 1), 0)
    row8 = t_row % 8

    def grouped_rows(a, k):
        return jnp.concatenate(
            [jnp.broadcast_to(a[8 * j + k:8 * j + k + 1, :], (8, w)) for j in range(c_sz // 8)], axis=0)

    def halving_levels(rev):
        out = []
        b = c_sz // 2
        while b >= 1:
            def later(i):
                return ((i % (2 * b)) < b) if rev else ((i % (2 * b)) >= b)
            live = jnp.logical_and(t_i // (2 * b) == s_i // (2 * b),
                                   jnp.logical_and(later(t_i), jnp.logical_not(later(s_i))))
            out.append((b, live.astype(F32), jnp.where(later(t_row), 1.0, -1.0)))
            b //= 2
        return out

    self_mask = (s_i == t_i).astype(F32)
    levels_of = (halving_levels(False), halving_levels(True))

    def chunk(c, rev):
        d = 1 if rev else 0
        levels = levels_of[d]
        r0 = pl.multiple_of(c * c_sz, c_sz)
        rows = pl.ds(r0, c_sz)
        q_pre = pa_ref[0, rows, 0:A_W]
        v = pa_ref[0, rows, A_W:2 * A_W]
        f_pre = pa_ref[0, rows, 3 * A_W:4 * A_W] if rev else pa_ref[0, rows, 2 * A_W:3 * A_W]
        lb = lb_ref[1:2, :] if rev else lb_ref[0:1, :]
        q = _silu(q_pre)
        f = lb + (1.0 - lb) * jax.nn.sigmoid(f_pre)
        log_f = jnp.log(jnp.maximum(f, F_MIN))
        kk = (1.0 - lb) * jax.nn.sigmoid(-f_pre)
        tri = tri_incl_future if rev else tri_incl_past
        cum = sum(jnp.dot(tri, piece, preferred_element_type=F32) for piece in _split3(log_f))
        e = 0 if rev else c_sz - 1
        cum_end = cum[e:e + 1, :]
        vb = v.astype(BF16)

        st = st_ref[d]
        o = _nt((q * jnp.exp(cum)).astype(BF16), st.astype(BF16))
        k_end = (kk * jnp.exp(cum_end - cum)).astype(BF16)
        st_ref[d] = st * jnp.exp(cum_end) + jnp.where(same_head, _tn(vb, k_end), 0.0)

        p = self_mask * jnp.dot((q * kk).astype(BF16), head_ones, preferred_element_type=F32)
        for b, live, sign in levels:
            if b >= 8:
                ref = jnp.concatenate(
                    [jnp.broadcast_to(cum[r0 + (b if rev else b - 1):r0 + (b if rev else b - 1) + 1, :], (2 * b, w))
                     for r0 in range(0, c_sz, 2 * b)], axis=0)
            else:
                n_ref = 8 // (2 * b)
                ref = None
                for g in reversed(range(n_ref)):
                    cand = grouped_rows(cum, g * 2 * b + (b if rev else b - 1))
                    ref = cand if ref is None else jnp.where(row8 < (g + 1) * 2 * b, cand, ref)
            z = jnp.exp((cum - ref) * sign)
            qa = (q * z).astype(BF16)
            ka = (kk * z).astype(BF16)
            ka_bd = jnp.where(same_head, jnp.concatenate([ka] * HEADS, axis=0), jnp.zeros((), BF16))
            p = p + live * _nt(qa, ka_bd)
        v_bd = jnp.where(same_head, jnp.concatenate([vb] * HEADS, axis=0), jnp.zeros((), BF16))
        o = o + jnp.dot(p.astype(BF16), v_bd, preferred_element_type=F32)
        if rev:
            ob_ref[rows, :] = o
        else:
            o_ref[0, rows, :] = o

    def scan_step(n, carry):
        chunk(n, False)
        chunk(jnp.where(n < n_ctx, n_ctx - 1 - n, n_all - 1 - (n - n_ctx)), True)
        return carry

    st_ref[...] = jnp.zeros_like(st_ref)
    lax.fori_loop(0, n_all, scan_step, 0)

    def readout(c, carry):
        rows = pl.ds(pl.multiple_of(c * ROW_TILE, ROW_TILE), ROW_TILE)
        tot = o_ref[0, rows, :] + ob_ref[rows, :]
        ms = jnp.dot(tot * tot, head_ones_f32, preferred_element_type=F32, precision=HI) * (1.0 / HEAD_V)
        g = pa_ref[0, rows, 4 * A_W:5 * A_W]
        o_ref[0, rows, :] = tot * lax.rsqrt(ms + NORM_EPS) * gain_ref[...] * _silu(g)
        return carry

    lax.fori_loop(0, n_all * c_sz // ROW_TILE, readout, 0)


def _hgrn(pa3, lb, gain, n_ctx_rows):
    n_batch, rows, _ = pa3.shape
    kern = functools.partial(_hgrn_kernel, n_ctx=n_ctx_rows // CHUNK, n_all=rows // CHUNK)
    return pl.pallas_call(
        kern,
        out_shape=jax.ShapeDtypeStruct((n_batch, rows, A_W), F32),
        grid=(n_batch,),
        in_specs=[pl.BlockSpec((1, rows, PA_W), lambda b: (b, 0, 0)),
                  pl.BlockSpec((2, A_W), lambda b: (0, 0)),
                  pl.BlockSpec((1, A_W), lambda b: (0, 0))],
        out_specs=pl.BlockSpec((1, rows, A_W), lambda b: (b, 0, 0)),
        scratch_shapes=[pltpu.VMEM((2, A_W, A_W), F32), pltpu.VMEM((rows, A_W), F32)],
        compiler_params=_cparams(("arbitrary",)),
        name="hgrn2",
    )(pa3, lb, gain)


def _split3(x):
    hi = x.astype(BF16)
    rest = x - hi.astype(F32)
    mid = rest.astype(BF16)
    return hi, mid, (rest - mid.astype(F32)).astype(BF16)


def _mlstm_kernel(pb_ref, bias_ref, gain_ref, o_ref, cn_fwd_ref, cn_bwd_ref, ob_ref, *, n_ctx, n_all):
    assert CHUNK == HEAD_V
    c_sz, w, qk_w = CHUNK, B_W, HEADS * B_QK
    t_i = lax.broadcasted_iota(jnp.int32, (c_sz, w), 0)
    s_i = lax.broadcasted_iota(jnp.int32, (c_sz, w), 1) % c_sz
    r_i = lax.broadcasted_iota(jnp.int32, (c_sz, c_sz), 0)
    c_i = lax.broadcasted_iota(jnp.int32, (c_sz, c_sz), 1)
    diag4 = s_i == t_i
    half_lane = lax.broadcasted_iota(jnp.int32, (1, LANES), 1) < HEAD_V
    k_block = (lax.broadcasted_iota(jnp.int32, (HEADS * c_sz, qk_w), 0) // c_sz
               == lax.broadcasted_iota(jnp.int32, (HEADS * c_sz, qk_w), 1) // B_QK)
    v_block = (lax.broadcasted_iota(jnp.int32, (HEADS * c_sz, w), 0) // c_sz
               == lax.broadcasted_iota(jnp.int32, (HEADS * c_sz, w), 1) // HEAD_V)
    state_block = (lax.broadcasted_iota(jnp.int32, (qk_w, 2 * w), 0) // B_QK
                   == (lax.broadcasted_iota(jnp.int32, (qk_w, 2 * w), 1) % w) // HEAD_V)
    head_ones = v_block.astype(BF16)
    q_off, k_off, v_off, o_off, g_off = 0, qk_w, 2 * qk_w, 2 * qk_w + B_W, 2 * qk_w + 2 * B_W

    def selector(rev):
        sr = lax.broadcasted_iota(jnp.int32, (2 * LANES, 2 * w), 0)
        sc = lax.broadcasted_iota(jnp.int32, (2 * LANES, 2 * w), 1)
        i_base = 2 * HEADS if rev else 0
        want = jnp.where(sc < w, i_base + HEADS + sc // HEAD_V, LANES + i_base + (sc - w) // HEAD_V)
        return (sr == want).astype(BF16)

    def head_max(x):
        outs = []
        for col in range(w // LANES):
            xc = x[:, col * LANES:(col + 1) * LANES]
            lo = jnp.max(jnp.where(half_lane, xc, -jnp.inf), axis=-1, keepdims=True)
            hi = jnp.max(jnp.where(half_lane, -jnp.inf, xc), axis=-1, keepdims=True)
            outs.append(jnp.where(half_lane, lo, hi))
        return jnp.concatenate(outs, axis=1)

    def chunk(c, rev, m_prev, sel):
        d = 1 if rev else 0
        rows = pl.ds(pl.multiple_of(c * c_sz, c_sz), c_sz)
        qb = (pb_ref[0, rows, q_off:q_off + qk_w] * (B_QK ** -0.5)).astype(BF16)
        kb = pb_ref[0, rows, k_off:k_off + qk_w].astype(BF16)
        v = pb_ref[0, rows, v_off:v_off + B_W]
        gts = pb_ref[0, rows, g_off:g_off + LANES] + bias_ref[...]
        log_f = jnp.minimum(gts, 0.0) - jnp.log(1.0 + jnp.exp(-jnp.abs(gts)))
        mask = (s_i >= t_i) if rev else (s_i <= t_i)
        tri = ((c_i >= r_i) if rev else (c_i <= r_i)).astype(BF16)
        cum_col = sum(jnp.dot(tri, p, preferred_element_type=F32) for p in _split3(log_f))
        both = sum(jnp.dot(p, sel, preferred_element_type=F32)
                   for p in _split3(jnp.concatenate([cum_col, gts], axis=1)))
        cum_t = both[:, :w]
        ig_t = both[:, w:]
        src = jnp.sum(jnp.where(diag4, cum_t - ig_t, 0.0), axis=0, keepdims=True)
        log_d = jnp.where(mask, cum_t - src, MASK_NEG)
        log_inter = cum_t + m_prev
        m_t = jnp.maximum(log_inter, head_max(log_d))
        w_intra = jnp.where(mask, jnp.exp(log_d - m_t), 0.0)
        w_inter = jnp.exp(log_inter - m_t)

        k_bd = jnp.where(k_block, jnp.concatenate([kb] * HEADS, axis=0), jnp.zeros((), BF16))
        p = (_nt(qb, k_bd) * w_intra).astype(BF16)
        vb = v.astype(BF16)
        v_bd = jnp.where(v_block, jnp.concatenate([vb] * HEADS, axis=0), jnp.zeros((), BF16))
        cn_ref = cn_bwd_ref if rev else cn_fwd_ref
        cn = cn_ref[...]
        inter = jnp.dot(qb, cn.astype(BF16), preferred_element_type=F32)
        num = w_inter * inter[:, :w] + jnp.dot(p, v_bd, preferred_element_type=F32)
        den = w_inter * inter[:, w:] + jnp.dot(p, head_ones, preferred_element_type=F32)
        h_out = num / jnp.maximum(jnp.abs(den), jnp.exp(-m_t))

        e = 0 if rev else c_sz - 1
        cum_e = cum_t[e:e + 1, :]
        log_end = cum_e - cum_t + ig_t
        m_end = jnp.maximum(cum_e + m_prev, jnp.max(log_end, axis=0, keepdims=True))
        w_end = jnp.exp(log_end - m_end)
        w_carry = jnp.exp(cum_e + m_prev - m_end)
        upd = _tn(kb, jnp.concatenate([(w_end * v).astype(BF16), w_end.astype(BF16)], axis=1))
        cn_ref[...] = jnp.concatenate([w_carry, w_carry], axis=1) * cn + jnp.where(state_block, upd, 0.0)
        if rev:
            ob_ref[rows, :] = h_out
        else:
            o_ref[0, rows, :] = h_out
        return m_end

    sel_fwd = selector(False)
    sel_bwd = selector(True)

    def scan_step(n, carry):
        m_fwd, m_bwd = carry
        m_fwd = chunk(n, False, m_fwd, sel_fwd)
        m_bwd = chunk(jnp.where(n < n_ctx, n_ctx - 1 - n, n_all - 1 - (n - n_ctx)), True, m_bwd, sel_bwd)
        return m_fwd, m_bwd

    zero_m = jnp.zeros((1, w), F32)
    cn_fwd_ref[...] = jnp.zeros_like(cn_fwd_ref)
    cn_bwd_ref[...] = jnp.zeros_like(cn_bwd_ref)
    lax.fori_loop(0, n_all, scan_step, (zero_m, zero_m), unroll=2)

    def readout(c, carry):
        rows = pl.ds(pl.multiple_of(c * ROW_TILE, ROW_TILE), ROW_TILE)
        tot = o_ref[0, rows, :] + ob_ref[rows, :]
        normed = jnp.concatenate(
            [_rms(tot[:, h * HEAD_V:(h + 1) * HEAD_V]) for h in range(HEADS)], axis=1) * gain_ref[...]
        og = pb_ref[0, rows, o_off:o_off + B_W]
        o_ref[0, rows, :] = normed * jax.nn.sigmoid(og)
        return carry

    lax.fori_loop(0, n_all * c_sz // ROW_TILE, readout, 0)


def _mlstm(pb3, bias, gain, n_ctx_rows):
    n_batch, rows, _ = pb3.shape
    kern = functools.partial(_mlstm_kernel, n_ctx=n_ctx_rows // CHUNK, n_all=rows // CHUNK)
    return pl.pallas_call(
        kern,
        out_shape=jax.ShapeDtypeStruct((n_batch, rows, B_W), F32),
        grid=(n_batch,),
        in_specs=[pl.BlockSpec((1, rows, PB_W), lambda b: (b, 0, 0)),
                  pl.BlockSpec((1, LANES), lambda b: (0, 0)),
                  pl.BlockSpec((1, B_W), lambda b: (0, 0))],
        out_specs=pl.BlockSpec((1, rows, B_W), lambda b: (b, 0, 0)),
        scratch_shapes=[pltpu.VMEM((HEADS * B_QK, 2 * B_W), F32), pltpu.VMEM((HEADS * B_QK, 2 * B_W), F32),
                        pltpu.VMEM((rows, B_W), F32)],
        compiler_params=_cparams(("arbitrary",)),
        name="mlstm",
    )(pb3, bias, gain)


def _attn_kernel(lam_ref, qp_ref, qr_ref, k_ref, v_ref, gain_ref, o_ref, *, n_ctx, q_tile0, lam_init):
    lam = lam_ref[0]
    q_tile = pl.program_id(2) + q_tile0
    lane = lax.broadcasted_iota(jnp.int32, (1, 2 * C_DQK), 1)
    scale = C_DQK ** -0.5

    def finish(parts):
        o = parts[0] - lam * parts[1]
        o_ref[0] = _rms(o) * gain_ref[...] * (1.0 - lam_init)

    def sub_query(q, j):
        return jnp.where(lane // C_DQK == j, q * scale, jnp.zeros_like(q))

    def row_max(s):
        return jnp.max(s, axis=-1, keepdims=True)

    def row_sum(s):
        return jnp.sum(s, axis=-1, keepdims=True)

    def pv(ex, v):
        return jnp.dot(ex.astype(BF16), v, preferred_element_type=F32)

    @pl.when(q_tile * ROW_TILE < n_ctx)
    def _():
        kc = k_ref[0, 0:n_ctx, :]
        vc = v_ref[0, 0:n_ctx, :]
        parts = []
        for j in range(2):
            s = _nt(sub_query(qp_ref[0], j), kc)
            ex = jnp.exp(s - row_max(s))
            parts.append(pv(ex, vc) / row_sum(ex))
        finish(parts)

    @pl.when(q_tile * ROW_TILE >= n_ctx)
    def _():
        n_all = k_ref.shape[1]
        kc = k_ref[0, 0:n_ctx, :]
        kl = k_ref[0, n_ctx:n_all, :]
        vc = v_ref[0, 0:n_ctx, :]
        vl = v_ref[0, n_ctx:n_all, :]
        parts = []
        for j in range(2):
            s_c = _nt(sub_query(qp_ref[0], j), kc)
            s_l = _nt(sub_query(qr_ref[0], j), kl)
            m = jnp.maximum(row_max(s_c), row_max(s_l))
            e_c = jnp.exp(s_c - m)
            e_l = jnp.exp(s_l - m)
            parts.append((pv(e_c, vc) + pv(e_l, vl)) / (row_sum(e_c) + row_sum(e_l)))
        finish(parts)


def _attention(pc3, lam, gain, n_ctx_rows, lam_init, with_ctx):
    n_batch, rows, _ = pc3.shape
    q_tile0 = 0 if with_ctx else n_ctx_rows // ROW_TILE
    n_q = rows // ROW_TILE - q_tile0
    hb = C_W // LANES
    kern = functools.partial(_attn_kernel, n_ctx=n_ctx_rows, q_tile0=q_tile0, lam_init=lam_init)
    grid_spec = pltpu.PrefetchScalarGridSpec(
        num_scalar_prefetch=1,
        grid=(n_batch, HEADS, n_q),
        in_specs=[pl.BlockSpec((1, ROW_TILE, LANES), lambda b, h, i, lam: (b, i + q_tile0, h)),
                  pl.BlockSpec((1, ROW_TILE, LANES), lambda b, h, i, lam: (b, i + q_tile0, hb + h)),
                  pl.BlockSpec((1, rows, LANES), lambda b, h, i, lam: (b, 0, 2 * hb + h)),
                  pl.BlockSpec((1, rows, LANES), lambda b, h, i, lam: (b, 0, 3 * hb + h)),
                  pl.BlockSpec((1, LANES), lambda b, h, i, lam: (0, 0))],
        out_specs=pl.BlockSpec((1, ROW_TILE, LANES), lambda b, h, i, lam: (b, i, h)),
    )
    return pl.pallas_call(
        kern,
        out_shape=jax.ShapeDtypeStruct((n_batch, n_q * ROW_TILE, C_W), F32),
        grid_spec=grid_spec,
        compiler_params=_cparams(("arbitrary", "arbitrary", "arbitrary")),
        name="diff_attn",
    )(lam, pc3, pc3, pc3, pc3, gain)


def _outproj_kernel(oa_ref, ob_ref, oc_ref, x_ref, mod_ref, n_ref, w_ref, rw_ref, rb_ref,
                    xn_ref, h_ref, route_ref, cnt_ref, carry_ref):
    d = x_ref.shape[1]
    n_exp = rw_ref.shape[1]

    @pl.when(pl.program_id(0) == 0)
    def _():
        carry_ref[...] = jnp.zeros_like(carry_ref)

    mix = (jnp.dot(oa_ref[...].astype(BF16), w_ref[0:A_W, :], preferred_element_type=F32)
           + jnp.dot(ob_ref[...].astype(BF16), w_ref[A_W:A_W + B_W, :], preferred_element_type=F32)
           + jnp.dot(oc_ref[...].astype(BF16), w_ref[A_W + B_W:A_W + B_W + C_W, :], preferred_element_type=F32))
    xn = x_ref[...] + mod_ref[0, :, 2 * d:3 * d] * (_rms(mix) * n_ref[1:2, :])
    xn_ref[...] = xn
    h = _rms(xn) * n_ref[2:3, :] * (1.0 + mod_ref[0, :, 4 * d:5 * d]) + mod_ref[0, :, 3 * d:4 * d]
    h_ref[...] = _pack_bf16_pairs(h)

    h_hi = h.astype(BF16)
    h_mid = (h - h_hi.astype(F32)).astype(BF16)
    logits = _nt(rw_ref[0], h_hi) + _nt(rw_ref[1], h_hi) + _nt(rw_ref[0], h_mid) + rb_ref[...]
    e_sub = lax.broadcasted_iota(jnp.int32, logits.shape, 0)
    cur = logits
    picks, vals = [], []
    for _ in range(TOP_K):
        mx = jnp.max(cur, axis=0, keepdims=True)
        idx = jnp.min(jnp.where(cur == mx, e_sub, n_exp), axis=0, keepdims=True)
        hit = e_sub == idx
        cur = jnp.where(hit, -jnp.inf, cur)
        picks.append((idx, hit.astype(F32)))
        vals.append(mx)
    exps = [jnp.exp(vv - vals[0]) for vv in vals]
    total = exps[0] + exps[1] + exps[2] + exps[3]

    chosen = picks[0][1] + picks[1][1] + picks[2][1] + picks[3][1]
    tm = logits.shape[1]
    before = (lax.broadcasted_iota(jnp.int32, (tm, tm), 0) < lax.broadcasted_iota(jnp.int32, (tm, tm), 1))
    seen = jnp.dot(chosen.astype(BF16), before.astype(BF16), preferred_element_type=F32) + carry_ref[...]
    new_carry = carry_ref[...] + jnp.sum(chosen, axis=1, keepdims=True)
    carry_ref[...] = new_carry
    cnt_ref[...] = new_carry

    o_row = lax.broadcasted_iota(jnp.int32, route_ref.shape, 0)
    route = jnp.zeros(route_ref.shape, F32)
    for kk in range(TOP_K):
        idx, hit = picks[kk]
        rank = jnp.sum(hit * seen, axis=0, keepdims=True)
        route = (route + jnp.where(o_row == kk, idx.astype(F32), 0.0)
                 + jnp.where(o_row == TOP_K + kk, exps[kk] / total, 0.0)
                 + jnp.where(o_row == 2 * TOP_K + kk, rank, 0.0))
    route_ref[...] = route


ROUTE_ROWS = 16


def _outproj(oa, ob, oc, xy, mod, norms, w_out_b, router_wt, router_b, layer, n_tiles, in_tile, oc_tile, mod_row):
    d = xy.shape[1]
    n_exp = router_wt.shape[2]
    t_out = n_tiles * ROW_TILE
    xn, h, route_t, counts = pl.pallas_call(
        _outproj_kernel,
        out_shape=(jax.ShapeDtypeStruct((t_out, d), F32),
                   jax.ShapeDtypeStruct((t_out, d // 2), jnp.int32),
                   jax.ShapeDtypeStruct((n_tiles * ROUTE_ROWS, ROW_TILE), F32),
                   jax.ShapeDtypeStruct((n_exp, 1), F32)),
        grid=(n_tiles,),
        in_specs=[pl.BlockSpec((ROW_TILE, A_W), lambda i: (in_tile(i), 0)),
                  pl.BlockSpec((ROW_TILE, B_W), lambda i: (in_tile(i), 0)),
                  pl.BlockSpec((ROW_TILE, C_W), lambda i: (oc_tile(i), 0)),
                  pl.BlockSpec((ROW_TILE, d), lambda i: (in_tile(i), 0)),
                  pl.BlockSpec((1, 1, 6 * d), lambda i: (mod_row(i), 0, 0)),
                  pl.BlockSpec((None, 4, d), lambda i: (layer, 0, 0)),
                  pl.BlockSpec((None, d, d), lambda i: (layer, 0, 0)),
                  pl.BlockSpec((None, 2, n_exp, d), lambda i: (layer, 0, 0, 0)),
                  pl.BlockSpec((None, n_exp, 1), lambda i: (layer, 0, 0))],
        out_specs=(pl.BlockSpec((ROW_TILE, d), lambda i: (i, 0)),
                   pl.BlockSpec((ROW_TILE, d // 2), lambda i: (i, 0)),
                   pl.BlockSpec((ROUTE_ROWS, ROW_TILE), lambda i: (i, 0)),
                   pl.BlockSpec((n_exp, 1), lambda i: (0, 0))),
        scratch_shapes=[pltpu.VMEM((n_exp, 1), F32)],
        compiler_params=_cparams(("arbitrary",)),
        name="outproj_router",
    )(oa, ob, oc, xy, mod, norms, w_out_b, router_wt, router_b)
    return xn, h, route_t.reshape(n_tiles, ROUTE_ROWS, ROW_TILE), counts[:, 0]


PAIR_BLOCK = 2 * LANES


def _moe_kernel(te_ref, first_ref, valid_ref, nu_ref, xs_ref, w1_ref, b1_ref, w2_ref, b2_ref, ys_ref,
                w1p_ref, w2b_ref):
    i = pl.program_id(0)
    two_f = w1_ref.shape[1]
    n_blk = two_f // PAIR_BLOCK

    @pl.when(jnp.logical_and(i < nu_ref[0], first_ref[i] == 1))
    def _():
        r = lax.broadcasted_iota(jnp.int32, (PAIR_BLOCK, PAIR_BLOCK), 0)
        c = lax.broadcasted_iota(jnp.int32, (PAIR_BLOCK, PAIR_BLOCK), 1)
        perm = (r == jnp.where(c < LANES, 2 * c, 2 * (c - LANES) + 1)).astype(BF16)
        for blk in range(n_blk):
            cols = slice(blk * PAIR_BLOCK, (blk + 1) * PAIR_BLOCK)
            w1p_ref[:, cols] = jnp.dot(w1_ref[:, cols].astype(BF16), perm,
                                       preferred_element_type=F32).astype(BF16)
        w2b_ref[...] = w2_ref[...].astype(BF16)

    @pl.when(i < nu_ref[0])
    def _():
        x = _unpack_bf16_pairs(xs_ref[...]).astype(BF16)
        row = lax.broadcasted_iota(jnp.int32, (x.shape[0], 1), 0)
        x = jnp.where(row < valid_ref[i], x, jnp.zeros_like(x))
        hid = jnp.dot(x, w1p_ref[...], preferred_element_type=F32) + b1_ref[...]
        acts = []
        for blk in range(n_blk):
            glu = jnp.minimum(hid[:, blk * PAIR_BLOCK:blk * PAIR_BLOCK + LANES], SWIGLU_LIMIT)
            lin = jnp.clip(hid[:, blk * PAIR_BLOCK + LANES:(blk + 1) * PAIR_BLOCK], -SWIGLU_LIMIT, SWIGLU_LIMIT)
            acts.append((glu * jax.nn.sigmoid(SWIGLU_ALPHA * glu) * (lin + 1.0)).astype(BF16))
        y = jnp.dot(jnp.concatenate(acts, axis=1), w2b_ref[...], preferred_element_type=F32) + b2_ref[...]
        ys_ref[...] = _pack_bf16_pairs(y)

    @pl.when(i >= nu_ref[0])
    def _():
        ys_ref[...] = jnp.zeros_like(ys_ref)


def _moe(tile_expert, tile_first, tile_valid, n_used, xs, w1, b1p, w2, b2, layer):
    r_max, half_d = xs.shape
    d = 2 * half_d
    two_f = w1.shape[3]
    f = two_f // 2
    n_tiles = r_max // MOE_TILE

    def row_tile(i, te, fi, va, nu):
        return (jnp.maximum(jnp.minimum(i, nu[0] - 1), 0), 0)

    grid_spec = pltpu.PrefetchScalarGridSpec(
        num_scalar_prefetch=4,
        grid=(n_tiles,),
        in_specs=[pl.BlockSpec((MOE_TILE, half_d), row_tile),
                  pl.BlockSpec((None, None, d, two_f), lambda i, te, fi, va, nu: (layer, te[i], 0, 0)),
                  pl.BlockSpec((None, None, 1, two_f), lambda i, te, fi, va, nu: (layer, te[i], 0, 0)),
                  pl.BlockSpec((None, None, f, d), lambda i, te, fi, va, nu: (layer, te[i], 0, 0)),
                  pl.BlockSpec((None, None, 1, d), lambda i, te, fi, va, nu: (layer, te[i], 0, 0))],
        out_specs=pl.BlockSpec((MOE_TILE, half_d), lambda i, te, fi, va, nu: (i, 0)),
        scratch_shapes=[pltpu.VMEM((d, two_f), BF16), pltpu.VMEM((f, d), BF16)],
    )
    return pl.pallas_call(
        _moe_kernel,
        out_shape=jax.ShapeDtypeStruct((r_max, half_d), jnp.int32),
        grid_spec=grid_spec,
        compiler_params=_cparams(("arbitrary",)),
        name="moe_experts",
    )(tile_expert, tile_first, tile_valid, n_used, xs, w1, b1p, w2, b2)


SC_CORES = 2
SC_SUBCORES = 16
SC_CHUNK = 64


def _dispatch_rows(hp, pos, r_max):
    t, width = hp.shape
    workers = SC_CORES * SC_SUBCORES
    assert t % (workers * SC_CHUNK) == 0
    per_worker = t // (workers * SC_CHUNK)
    idx = pos.reshape(-1, TOP_K, ROW_TILE // SC_CHUNK, SC_CHUNK).transpose(0, 2, 1, 3).reshape(
        workers, per_worker * TOP_K, SC_CHUNK)
    mesh = plsc.VectorSubcoreMesh(core_axis_name="c", subcore_axis_name="s",
                                  num_cores=SC_CORES, num_subcores=SC_SUBCORES)

    @functools.partial(
        pl.kernel, mesh=mesh,
        out_type=jax.ShapeDtypeStruct((r_max, width), hp.dtype),
        scratch_types=[pltpu.VMEM((per_worker * TOP_K, SC_CHUNK), jnp.int32),
                       pltpu.VMEM((2, SC_CHUNK, width), hp.dtype),
                       pltpu.SemaphoreType.DMA((2,)),
                       pltpu.SemaphoreType.DMA((2,))],
    )
    def scatter(hp_hbm, idx_hbm, out_hbm, idx_v, rows_v, read_sem, write_sem):
        wid = lax.axis_index("s") * SC_CORES + lax.axis_index("c")
        pltpu.sync_copy(idx_hbm.at[wid], idx_v)

        def read(j):
            row0 = pl.multiple_of((wid * per_worker + j) * SC_CHUNK, SC_CHUNK)
            return pltpu.make_async_copy(hp_hbm.at[pl.ds(row0, SC_CHUNK)], rows_v.at[j % 2], read_sem.at[j % 2])

        def write(j, k):
            return pltpu.make_async_copy(rows_v.at[j % 2], out_hbm.at[idx_v.at[j * TOP_K + k]],
                                         write_sem.at[j % 2])

        read(0).start()
        for j in range(per_worker):
            read(j).wait()
            for k in range(TOP_K):
                write(j, k).start()
            if j + 1 < per_worker:
                if j >= 1:
                    for k in range(TOP_K):
                        write(j - 1, k).wait()
                read(j + 1).start()
        for j in range(max(per_worker - 2, 0), per_worker):
            for k in range(TOP_K):
                write(j, k).wait()

    return scatter(hp, idx)


def _combine_rows(ysp, pos):
    width = ysp.shape[1]
    t = pos.shape[0] * pos.shape[2]
    workers = SC_CORES * SC_SUBCORES
    assert t % (workers * SC_CHUNK) == 0
    per_worker = t // (workers * SC_CHUNK)
    units = per_worker * TOP_K
    idx = pos.reshape(-1, TOP_K, ROW_TILE // SC_CHUNK, SC_CHUNK).transpose(0, 2, 1, 3).reshape(
        workers, units, SC_CHUNK)
    mesh = plsc.VectorSubcoreMesh(core_axis_name="c", subcore_axis_name="s",
                                  num_cores=SC_CORES, num_subcores=SC_SUBCORES)

    @functools.partial(
        pl.kernel, mesh=mesh,
        out_type=jax.ShapeDtypeStruct((TOP_K, t, width), ysp.dtype),
        scratch_types=[pltpu.VMEM((units, SC_CHUNK), jnp.int32),
                       pltpu.VMEM((2, SC_CHUNK, width), ysp.dtype),
                       pltpu.SemaphoreType.DMA((2,)),
                       pltpu.SemaphoreType.DMA((2,))],
    )
    def gather(ys_hbm, idx_hbm, out_hbm, idx_v, rows_v, read_sem, write_sem):
        wid = lax.axis_index("s") * SC_CORES + lax.axis_index("c")
        pltpu.sync_copy(idx_hbm.at[wid], idx_v)

        def read(u):
            return pltpu.make_async_copy(ys_hbm.at[idx_v.at[u]], rows_v.at[u % 2], read_sem.at[u % 2])

        def write(u):
            row0 = pl.multiple_of((wid * per_worker + u // TOP_K) * SC_CHUNK, SC_CHUNK)
            return pltpu.make_async_copy(rows_v.at[u % 2], out_hbm.at[u % TOP_K, pl.ds(row0, SC_CHUNK)],
                                         write_sem.at[u % 2])

        read(0).start()
        for u in range(units):
            read(u).wait()
            write(u).start()
            if u + 1 < units:
                if u >= 1:
                    write(u - 1).wait()
                read(u + 1).start()
        for u in range(max(units - 2, 0), units):
            write(u).wait()

    return gather(ysp, idx)


def _ffn_residual_kernel(x_ref, w_ref, y0_ref, y1_ref, y2_ref, y3_ref, mod_ref, n_ref, o_ref):
    d = x_ref.shape[1]
    wts = w_ref[...]
    ffn = None
    for k, y_ref in enumerate((y0_ref, y1_ref, y2_ref, y3_ref)):
        term = wts[:, k:k + 1] * _unpack_bf16_pairs(y_ref[...])
        ffn = term if ffn is None else ffn + term
    o_ref[...] = x_ref[...] + mod_ref[0, :, 5 * d:6 * d] * (_rms(ffn) * n_ref[3:4, :])


def _ffn_residual(xn, wts, ys_by_k, mod, norms, layer, mod_row):
    t, d = xn.shape
    row_spec = pl.BlockSpec((ROW_TILE, d), lambda i: (i, 0))

    def y_spec(k):
        return pl.BlockSpec((None, ROW_TILE, d // 2), lambda i: (k, i, 0))

    return pl.pallas_call(
        _ffn_residual_kernel,
        out_shape=jax.ShapeDtypeStruct((t, d), F32),
        grid=(t // ROW_TILE,),
        in_specs=[row_spec,
                  pl.BlockSpec((ROW_TILE, TOP_K), lambda i: (i, 0)),
                  y_spec(0), y_spec(1), y_spec(2), y_spec(3),
                  pl.BlockSpec((1, 1, 6 * d), lambda i: (mod_row(i), 0, 0)),
                  pl.BlockSpec((None, 4, d), lambda i: (layer, 0, 0))],
        out_specs=row_spec,
        compiler_params=_cparams(("arbitrary",)),
        name="ffn_residual",
    )(xn, wts, ys_by_k, ys_by_k, ys_by_k, ys_by_k, mod, norms)


def _dispatch_plan(route_t, counts, r_max):
    n_row_tiles = route_t.shape[0]
    idx = route_t[:, 0:TOP_K, :].astype(jnp.int32)
    wts = route_t[:, TOP_K:2 * TOP_K, :].transpose(0, 2, 1).reshape(n_row_tiles * ROW_TILE, TOP_K)
    rank = route_t[:, 2 * TOP_K:3 * TOP_K, :].astype(jnp.int32)
    cnt = counts.astype(jnp.int32)
    padded = (cnt + MOE_TILE - 1) // MOE_TILE * MOE_TILE
    ends = jnp.cumsum(padded)
    starts = ends - padded
    pos = rank
    for e in range(cnt.shape[0]):
        pos = pos + jnp.where(idx == e, starts[e], 0)
    n_tiles = r_max // MOE_TILE
    n_used = ends[-1] // MOE_TILE
    tile_ids = jnp.minimum(jnp.arange(n_tiles, dtype=jnp.int32), n_used - 1)
    tile_expert = jnp.sum((ends // MOE_TILE)[None, :] <= tile_ids[:, None], axis=1).astype(jnp.int32)
    tile_first = jnp.concatenate(
        [jnp.ones((1,), jnp.int32), (tile_expert[1:] != tile_expert[:-1]).astype(jnp.int32)])
    of_expert = (tile_expert[:, None] == jnp.arange(cnt.shape[0], dtype=jnp.int32)[None, :]).astype(jnp.int32)
    first_tile = jnp.sum(of_expert * ((ends - padded) // MOE_TILE)[None, :], axis=1)
    tile_valid = jnp.clip(jnp.sum(of_expert * cnt[None, :], axis=1) - (tile_ids - first_tile) * MOE_TILE,
                          0, MOE_TILE).astype(jnp.int32)
    return pos, wts, tile_expert, tile_first, tile_valid, n_used.reshape(1).astype(jnp.int32)


def _rope_tables(length, n_ctx_rows):
    rows = length // GRID_W
    row = jnp.repeat(jnp.arange(rows, dtype=F32), GRID_W)
    col = jnp.tile(jnp.arange(GRID_W, dtype=F32), rows)
    n_freq = C_DQK // 4
    inv_freq = ROPE_THETA ** (-jnp.arange(n_freq, dtype=F32) / n_freq)
    ang_r = row[:, None] * inv_freq
    ang_c = col[:, None] * inv_freq
    cos = jnp.concatenate([jnp.cos(ang_r), jnp.cos(ang_r), jnp.cos(ang_c), jnp.cos(ang_c)], axis=-1)
    sin = jnp.concatenate([-jnp.sin(ang_r), jnp.sin(ang_r), -jnp.sin(ang_c), jnp.sin(ang_c)], axis=-1)
    reps = C_W // C_DQK
    cos = jnp.concatenate([jnp.ones((n_ctx_rows, C_W), F32), jnp.tile(cos, (1, reps))], axis=0)
    sin = jnp.concatenate([jnp.zeros((n_ctx_rows, C_W), F32), jnp.tile(sin, (1, reps))], axis=0)
    return cos, sin


def _hgrn_lower_bound(table, layer):
    p = jax.nn.softmax(table.astype(F32), axis=1)
    cum = jnp.cumsum(p, axis=1) - p[:, :1]
    return jnp.clip(cum[:, layer], 0.0, 1.0)


def kernel(x, c, ctx, c_ctx, ada_w, ada_b, sandwich_norms, w_in, w_out, hgrn_lower_bounds, hgrn_norm,
           mlstm_gate_bias, mlstm_norm, diff_lambdas, diff_norm, router_w, router_b, moe_w1, moe_b1,
           moe_w2, moe_b2):
    n_batch, seq, d = x.shape
    n_ctx_rows = ctx.shape[1]
    depth = w_in.shape[0]
    n_exp = router_w.shape[2]
    assert seq % ROW_TILE == 0 and n_ctx_rows % ROW_TILE == 0 and seq % GRID_W == 0
    rows_b = n_ctx_rows + seq
    tiles_b = rows_b // ROW_TILE
    ctx_tiles = n_ctx_rows // ROW_TILE
    lat_tiles = seq // ROW_TILE

    b_main = PA_W + 2 * HEADS * B_QK + 2 * B_W
    w_in_p = jnp.concatenate(
        [w_in[:, :, :b_main],
         jnp.pad(w_in[:, :, b_main:b_main + N_GATE], ((0, 0), (0, 0), (0, LANES - N_GATE))),
         w_in[:, :, b_main + N_GATE:]], axis=2).astype(BF16)
    w_out_b = w_out.astype(BF16)
    two_f = moe_b1.shape[2]
    b1p = moe_b1.reshape(depth, n_exp, two_f // PAIR_BLOCK, LANES, 2).transpose(0, 1, 2, 4, 3).reshape(
        depth, n_exp, 1, two_f)
    b2 = moe_b2[:, :, None, :]
    ada_b3 = ada_b[:, None, :]
    router_b3 = router_b[:, :, None]
    rw_t = router_w.transpose(0, 2, 1)
    rw_hi = rw_t.astype(BF16)
    router_wt = jnp.stack([rw_hi, (rw_t - rw_hi.astype(F32)).astype(BF16)], axis=1)
    gate_bias = jnp.pad(mlstm_gate_bias, ((0, 0), (0, LANES - N_GATE)))
    cos, sin = _rope_tables(seq, n_ctx_rows)

    cond_rows = (n_batch + 1 + 7) // 8 * 8
    cvec = jnp.zeros((cond_rows, d), F32).at[:n_batch].set(c).at[n_batch].set(c_ctx)

    def mod_row_all(i):
        return jnp.where(i % tiles_b < ctx_tiles, n_batch, i // tiles_b)

    xy = jnp.concatenate([ctx, x], axis=1).reshape(n_batch * rows_b, d)

    for layer in range(depth):
        last = layer == depth - 1
        mod = _adaln(cvec, ada_w, ada_b3, layer).reshape(cond_rows, 1, 6 * d)
        lb = _hgrn_lower_bound(hgrn_lower_bounds, layer)
        lam_init = 0.8 - 0.6 * math.exp(-0.3 * layer)
        lq1, lk1, lq2, lk2 = diff_lambdas[layer].astype(F32)
        lam = (jnp.exp(jnp.sum(lq1 * lk1)) - jnp.exp(jnp.sum(lq2 * lk2)) + lam_init).reshape(1)

        pa, pb, pc = _inproj(xy, mod, sandwich_norms, w_in_p, cos, sin, layer, tiles_b, ctx_tiles, n_batch)
        oa = _hgrn(pa.reshape(n_batch, rows_b, PA_W), lb, jnp.tile(hgrn_norm[layer], HEADS)[None, :], n_ctx_rows)
        ob = _mlstm(pb.reshape(n_batch, rows_b, PB_W), gate_bias[layer][None, :], mlstm_norm[layer][None, :],
                    n_ctx_rows)
        oc = _attention(pc.reshape(n_batch, rows_b, PC_OUT), lam, diff_norm[layer][None, :], n_ctx_rows,
                        lam_init, not last)

        if last:
            n_tiles = n_batch * lat_tiles
            in_tile = lambda i: (i // lat_tiles) * tiles_b + ctx_tiles + i % lat_tiles
            mod_row = lambda i: i // lat_tiles
        else:
            n_tiles = n_batch * tiles_b
            in_tile = lambda i: i
            mod_row = mod_row_all
        xn, h, route, counts = _outproj(
            oa.reshape(-1, A_W), ob.reshape(-1, B_W), oc.reshape(-1, C_W), xy, mod, sandwich_norms, w_out_b,
            router_wt, router_b3, layer, n_tiles, in_tile, lambda i: i, mod_row)

        r_max = n_tiles * ROW_TILE * TOP_K + n_exp * MOE_TILE
        pos, wts, tile_expert, tile_first, tile_valid, n_used = _dispatch_plan(route, counts, r_max)
        xs = _dispatch_rows(h, pos, r_max)
        ys = _moe(tile_expert, tile_first, tile_valid, n_used, xs, moe_w1, b1p, moe_w2, b2, layer)
        xy = _ffn_residual(xn, wts, _combine_rows(ys, pos), mod, sandwich_norms, layer, mod_row)

    return xy.reshape(n_batch, seq, d)
```

```python
import functools
import math

import jax
import jax.numpy as jnp
from jax import lax
from jax.experimental import pallas as pl
from jax.experimental.pallas import tpu as pltpu
from jax.experimental.pallas import tpu_sc as plsc

F32 = jnp.float32
BF16 = jnp.bfloat16
HI = lax.Precision.HIGHEST

HEADS = 4
A_W = 256
B_QK = 32
B_W = 256
C_DQK = 64
C_DV = 2 * C_DQK
C_W = 512
HEAD_V = 64
N_GATE = 16
GRID_W = 64
TOP_K = 4
SWIGLU_ALPHA = 1.702
SWIGLU_LIMIT = 7.0
ROPE_THETA = 10000.0
NORM_EPS = 1e-6
MASK_NEG = -1e30
F_MIN = 1e-12

LANES = 128
ROW_TILE = 256
CHUNK = 64
MOE_TILE = 512
VMEM_LIMIT = 56 * 1024 * 1024

PA_W = 5 * A_W
PB_W = 2 * HEADS * B_QK + 2 * B_W + LANES
PC_IN = 3 * C_W
PC_OUT = 4 * C_W
W_IN_PAD = PA_W + PB_W + PC_IN


def _cparams(sem, flags=None):
    return pltpu.CompilerParams(dimension_semantics=sem, vmem_limit_bytes=VMEM_LIMIT, flags=flags)


def _nt(a, b):
    return lax.dot_general(a, b, (((1,), (1,)), ((), ())), preferred_element_type=F32)


def _tn(a, b, precision=None):
    return lax.dot_general(a, b, (((0,), (0,)), ((), ())), preferred_element_type=F32, precision=precision)


def _rms(x):
    return x * lax.rsqrt(jnp.mean(x * x, axis=-1, keepdims=True) + NORM_EPS)


def _silu(x):
    return x * jax.nn.sigmoid(x)


def _pack_bf16_pairs(x):
    half = x.shape[1] // 2
    bits = pltpu.bitcast(x.astype(BF16).astype(F32), jnp.uint32)
    return pltpu.bitcast((bits[:, :half] >> 16) | (bits[:, half:] & jnp.uint32(0xFFFF0000)), jnp.int32)


def _unpack_bf16_pairs(words):
    bits = pltpu.bitcast(words, jnp.uint32)
    return jnp.concatenate([pltpu.bitcast(bits << 16, F32),
                            pltpu.bitcast(bits & jnp.uint32(0xFFFF0000), F32)], axis=1)


def _adaln_kernel(c_ref, w_ref, b_ref, o_ref):
    cond = _silu(c_ref[...])
    o_ref[...] = jnp.dot(cond, w_ref[...], preferred_element_type=F32, precision=HI) + b_ref[...]


def _adaln(cvec, ada_w, ada_b, layer):
    rows, d = cvec.shape
    return pl.pallas_call(
        _adaln_kernel,
        out_shape=jax.ShapeDtypeStruct((rows, 6 * d), F32),
        grid=(6,),
        in_specs=[pl.BlockSpec((rows, d), lambda j: (0, 0)),
                  pl.BlockSpec((None, d, d), lambda j: (layer, 0, j)),
                  pl.BlockSpec((None, 1, d), lambda j: (layer, 0, j))],
        out_specs=pl.BlockSpec((rows, d), lambda j: (0, j)),
        compiler_params=_cparams(("arbitrary",)),
        name="adaln",
    )(cvec, ada_w, ada_b)


def _inproj_kernel(x_ref, mod_ref, g_ref, w_ref, cos_ref, sin_ref, pa_ref, pb_ref, pc_ref):
    d = x_ref.shape[1]
    shift = mod_ref[0, :, 0:d]
    scale = mod_ref[0, :, d:2 * d]
    h = _rms(x_ref[...]) * g_ref[0:1, :] * (1.0 + scale) + shift
    hb = h.astype(BF16)
    pa_ref[...] = jnp.dot(hb, w_ref[:, 0:PA_W], preferred_element_type=F32)
    pb_ref[...] = jnp.dot(hb, w_ref[:, PA_W:PA_W + PB_W], preferred_element_type=F32)
    pc = jnp.dot(hb, w_ref[:, PA_W + PB_W:W_IN_PAD], preferred_element_type=F32)
    q = pc[:, 0:C_W]
    k = pc[:, C_W:2 * C_W]
    cos = cos_ref[...]
    sin = sin_ref[...]
    lane = lax.broadcasted_iota(jnp.int32, q.shape, 1)
    first = (lane % 32) < 16

    def rope(t):
        partner = jnp.where(first, pltpu.roll(t, C_W - 16, 1), pltpu.roll(t, 16, 1))
        return t * cos + partner * sin

    pc_ref[:, 0:C_W] = q.astype(BF16)
    pc_ref[:, C_W:2 * C_W] = rope(q).astype(BF16)
    pc_ref[:, 2 * C_W:3 * C_W] = rope(k).astype(BF16)
    pc_ref[:, 3 * C_W:4 * C_W] = pc[:, 2 * C_W:3 * C_W].astype(BF16)


def _inproj(xy, mod, norms, w_in_p, cos, sin, layer, tiles_per_batch, n_ctx_tiles, n_batch):
    t_all, d = xy.shape
    n_tiles = t_all // ROW_TILE

    def mod_row(i):
        return jnp.where(i % tiles_per_batch < n_ctx_tiles, n_batch, i // tiles_per_batch)

    return pl.pallas_call(
        _inproj_kernel,
        out_shape=(jax.ShapeDtypeStruct((t_all, PA_W), F32),
                   jax.ShapeDtypeStruct((t_all, PB_W), F32),
                   jax.ShapeDtypeStruct((t_all, PC_OUT), BF16)),
        grid=(n_tiles,),
        in_specs=[pl.BlockSpec((ROW_TILE, d), lambda i: (i, 0)),
                  pl.BlockSpec((1, 1, 6 * d), lambda i: (mod_row(i), 0, 0)),
                  pl.BlockSpec((None, 4, d), lambda i: (layer, 0, 0)),
                  pl.BlockSpec((None, d, W_IN_PAD), lambda i: (layer, 0, 0)),
                  pl.BlockSpec((ROW_TILE, C_W), lambda i: (i % tiles_per_batch, 0)),
                  pl.BlockSpec((ROW_TILE, C_W), lambda i: (i % tiles_per_batch, 0))],
        out_specs=(pl.BlockSpec((ROW_TILE, PA_W), lambda i: (i, 0)),
                   pl.BlockSpec((ROW_TILE, PB_W), lambda i: (i, 0)),
                   pl.BlockSpec((ROW_TILE, PC_OUT), lambda i: (i, 0))),
        compiler_params=_cparams(("arbitrary",)),
        name="inproj",
    )(xy, mod, norms, w_in_p, cos, sin)


def _hgrn_kernel(pa_ref, lb_ref, gain_ref, o_ref, st_ref, ob_ref, *, n_ctx, n_all):
    c_sz, w = CHUNK, A_W
    r_i = lax.broadcasted_iota(jnp.int32, (c_sz, c_sz), 0)
    c_i = lax.broadcasted_iota(jnp.int32, (c_sz, c_sz), 1)
    tri_incl_past = (c_i <= r_i).astype(BF16)
    tri_incl_future = (c_i >= r_i).astype(BF16)
    same_head = (lax.broadcasted_iota(jnp.int32, (w, w), 0) // HEAD_V
                 == lax.broadcasted_iota(jnp.int32, (w, w), 1) // HEAD_V)
    head_ones = same_head.astype(BF16)
    head_ones_f32 = same_head.astype(F32)
    assert c_sz == HEAD_V
    t_i = lax.broadcasted_iota(jnp.int32, (c_sz, w), 0)
    s_i = lax.broadcasted_iota(jnp.int32, (c_sz, w), 1) % c_sz
    t_row = lax.broadcasted_iota(jnp.int32, (c_sz, 1), 0)
    row8 = t_row % 8

    def grouped_rows(a, k):
        return jnp.concatenate(
            [jnp.broadcast_to(a[8 * j + k:8 * j + k + 1, :], (8, w)) for j in range(c_sz // 8)], axis=0)

    def halving_levels(rev):
        out = []
        b = c_sz // 2
        while b >= 1:
            def later(i):
                return ((i % (2 * b)) < b) if rev else ((i % (2 * b)) >= b)
            live = jnp.logical_and(t_i // (2 * b) == s_i // (2 * b),
                                   jnp.logical_and(later(t_i), jnp.logical_not(later(s_i))))
            out.append((b, live.astype(F32), jnp.where(later(t_row), 1.0, -1.0)))
            b //= 2
        return out

    def both(x_f, x_b):
        return jnp.concatenate([x_f, x_b], axis=0)

    def rows_of(row_f, row_b):
        return both(jnp.broadcast_to(row_f, (c_sz, w)), jnp.broadcast_to(row_b, (c_sz, w)))

    self_mask = both(*[(s_i == t_i).astype(F32)] * 2)
    levels = [(b, both(live_f, live_b), both(sign_f, sign_b))
              for (b, live_f, sign_f), (_, live_b, sign_b) in zip(halving_levels(False), halving_levels(True))]
    lb2 = rows_of(lb_ref[0:1, :], lb_ref[1:2, :])
    r2 = lax.broadcasted_iota(jnp.int32, (2 * c_sz, 2 * c_sz), 0)
    c2 = lax.broadcasted_iota(jnp.int32, (2 * c_sz, 2 * c_sz), 1)
    tri2 = jnp.logical_or(jnp.logical_and(r2 < c_sz, c2 <= r2),
                          jnp.logical_and(r2 >= c_sz, c2 >= r2)).astype(BF16)
    zero = jnp.zeros((), BF16)

    def boundary_rows(cum, b, rev):
        if b >= 8:
            return jnp.concatenate(
                [jnp.broadcast_to(cum[r0 + (b if rev else b - 1):r0 + (b if rev else b - 1) + 1, :], (2 * b, w))
                 for r0 in range(0, c_sz, 2 * b)], axis=0)
        ref = None
        for g in reversed(range(8 // (2 * b))):
            cand = grouped_rows(cum, g * 2 * b + (b if rev else b - 1))
            ref = cand if ref is None else jnp.where(row8 < (g + 1) * 2 * b, cand, ref)
        return ref

    def scan_step(n, carry):
        c_f = n
        c_b = jnp.where(n < n_ctx, n_ctx - 1 - n, n_all - 1 - (n - n_ctx))
        rows_f = pl.ds(pl.multiple_of(c_f * c_sz, c_sz), c_sz)
        rows_b = pl.ds(pl.multiple_of(c_b * c_sz, c_sz), c_sz)
        q_pre = both(pa_ref[0, rows_f, 0:A_W], pa_ref[0, rows_b, 0:A_W])
        v_f = pa_ref[0, rows_f, A_W:2 * A_W].astype(BF16)
        v_b = pa_ref[0, rows_b, A_W:2 * A_W].astype(BF16)
        f_pre = both(pa_ref[0, rows_f, 2 * A_W:3 * A_W], pa_ref[0, rows_b, 3 * A_W:4 * A_W])
        q = _silu(q_pre)
        f = lb2 + (1.0 - lb2) * jax.nn.sigmoid(f_pre)
        log_f = jnp.log(jnp.maximum(f, F_MIN))
        kk = (1.0 - lb2) * jax.nn.sigmoid(-f_pre)
        cum = sum(jnp.dot(tri2, piece, preferred_element_type=F32) for piece in _split3(log_f))
        end_f = cum[c_sz - 1:c_sz, :]
        end_b = cum[c_sz:c_sz + 1, :]

        st_f = st_ref[0]
        st_b = st_ref[1]
        q_in = (q * jnp.exp(cum)).astype(BF16)
        o = both(_nt(q_in[0:c_sz], st_f.astype(BF16)), _nt(q_in[c_sz:2 * c_sz], st_b.astype(BF16)))
        k_end = (kk * jnp.exp(rows_of(end_f, end_b) - cum)).astype(BF16)
        st_ref[0] = st_f * jnp.exp(end_f) + jnp.where(same_head, _tn(v_f, k_end[0:c_sz]), 0.0)
        st_ref[1] = st_b * jnp.exp(end_b) + jnp.where(same_head, _tn(v_b, k_end[c_sz:2 * c_sz]), 0.0)

        def scores(qa, ka):
            return _nt(qa, jnp.where(same_head, jnp.concatenate([ka] * HEADS, axis=0), zero))

        p = self_mask * jnp.dot((q * kk).astype(BF16), head_ones, preferred_element_type=F32)
        for b, live, sign in levels:
            ref = both(boundary_rows(cum[0:c_sz], b, False), boundary_rows(cum[c_sz:2 * c_sz], b, True))
            z = jnp.exp((cum - ref) * sign)
            qa = (q * z).astype(BF16)
            ka = (kk * z).astype(BF16)
            p = p + live * both(scores(qa[0:c_sz], ka[0:c_sz]), scores(qa[c_sz:2 * c_sz], ka[c_sz:2 * c_sz]))
        pb = p.astype(BF16)

        def values(p_half, v_half):
            return jnp.dot(p_half, jnp.where(same_head, jnp.concatenate([v_half] * HEADS, axis=0), zero),
                           preferred_element_type=F32)

        o_ref[0, rows_f, :] = o[0:c_sz] + values(pb[0:c_sz], v_f)
        ob_ref[rows_b, :] = o[c_sz:2 * c_sz] + values(pb[c_sz:2 * c_sz], v_b)
        return carry

    st_ref[...] = jnp.zeros_like(st_ref)
    lax.fori_loop(0, n_all, scan_step, 0)

    def readout(c, carry):
        rows = pl.ds(pl.multiple_of(c * ROW_TILE, ROW_TILE), ROW_TILE)
        tot = o_ref[0, rows, :] + ob_ref[rows, :]
        ms = jnp.dot(tot * tot, head_ones_f32, preferred_element_type=F32, precision=HI) * (1.0 / HEAD_V)
        g = pa_ref[0, rows, 4 * A_W:5 * A_W]
        o_ref[0, rows, :] = tot * lax.rsqrt(ms + NORM_EPS) * gain_ref[...] * _silu(g)
        return carry

    lax.fori_loop(0, n_all * c_sz // ROW_TILE, readout, 0)


def _hgrn(pa3, lb, gain, n_ctx_rows):
    n_batch, rows, _ = pa3.shape
    kern = functools.partial(_hgrn_kernel, n_ctx=n_ctx_rows // CHUNK, n_all=rows // CHUNK)
    return pl.pallas_call(
        kern,
        out_shape=jax.ShapeDtypeStruct((n_batch, rows, A_W), F32),
        grid=(n_batch,),
        in_specs=[pl.BlockSpec((1, rows, PA_W), lambda b: (b, 0, 0)),
                  pl.BlockSpec((2, A_W), lambda b: (0, 0)),
                  pl.BlockSpec((1, A_W), lambda b: (0, 0))],
        out_specs=pl.BlockSpec((1, rows, A_W), lambda b: (b, 0, 0)),
        scratch_shapes=[pltpu.VMEM((2, A_W, A_W), F32), pltpu.VMEM((rows, A_W), F32)],
        compiler_params=_cparams(("arbitrary",)),
        name="hgrn2",
    )(pa3, lb, gain)


def _split3(x):
    hi = x.astype(BF16)
    rest = x - hi.astype(F32)
    mid = rest.astype(BF16)
    return hi, mid, (rest - mid.astype(F32)).astype(BF16)


def _mlstm_kernel(pb_ref, bias_ref, gain_ref, o_ref, cn_fwd_ref, cn_bwd_ref, ob_ref, *, n_ctx, n_all):
    assert CHUNK == HEAD_V
    c_sz, w, qk_w = CHUNK, B_W, HEADS * B_QK
    t_i = lax.broadcasted_iota(jnp.int32, (c_sz, w), 0)
    s_i = lax.broadcasted_iota(jnp.int32, (c_sz, w), 1) % c_sz
    r_i = lax.broadcasted_iota(jnp.int32, (c_sz, c_sz), 0)
    c_i = lax.broadcasted_iota(jnp.int32, (c_sz, c_sz), 1)
    diag4 = s_i == t_i
    half_lane = lax.broadcasted_iota(jnp.int32, (1, LANES), 1) < HEAD_V
    k_block = (lax.broadcasted_iota(jnp.int32, (HEADS * c_sz, qk_w), 0) // c_sz
               == lax.broadcasted_iota(jnp.int32, (HEADS * c_sz, qk_w), 1) // B_QK)
    v_block = (lax.broadcasted_iota(jnp.int32, (HEADS * c_sz, w), 0) // c_sz
               == lax.broadcasted_iota(jnp.int32, (HEADS * c_sz, w), 1) // HEAD_V)
    state_block = (lax.broadcasted_iota(jnp.int32, (qk_w, 2 * w), 0) // B_QK
                   == (lax.broadcasted_iota(jnp.int32, (qk_w, 2 * w), 1) % w) // HEAD_V)
    head_ones = v_block.astype(BF16)
    q_off, k_off, v_off, o_off, g_off = 0, qk_w, 2 * qk_w, 2 * qk_w + B_W, 2 * qk_w + 2 * B_W

    def selector(rev):
        sr = lax.broadcasted_iota(jnp.int32, (2 * LANES, 2 * w), 0)
        sc = lax.broadcasted_iota(jnp.int32, (2 * LANES, 2 * w), 1)
        i_base = 2 * HEADS if rev else 0
        want = jnp.where(sc < w, i_base + HEADS + sc // HEAD_V, LANES + i_base + (sc - w) // HEAD_V)
        return (sr == want).astype(BF16)

    def head_max(x):
        outs = []
        for col in range(w // LANES):
            xc = x[:, col * LANES:(col + 1) * LANES]
            lo = jnp.max(jnp.where(half_lane, xc, -jnp.inf), axis=-1, keepdims=True)
            hi = jnp.max(jnp.where(half_lane, -jnp.inf, xc), axis=-1, keepdims=True)
            outs.append(jnp.where(half_lane, lo, hi))
        return jnp.concatenate(outs, axis=1)

    sel = selector(False)
    mask2 = jnp.concatenate([s_i <= t_i, s_i >= t_i], axis=0)
    r2 = lax.broadcasted_iota(jnp.int32, (2 * c_sz, 2 * c_sz), 0)
    c2 = lax.broadcasted_iota(jnp.int32, (2 * c_sz, 2 * c_sz), 1)
    tri2 = jnp.logical_or(jnp.logical_and(r2 < c_sz, c2 <= r2),
                          jnp.logical_and(r2 >= c_sz, c2 >= r2)).astype(BF16)

    def per_half(fn, x):
        return jnp.concatenate([jnp.broadcast_to(fn(x[0:c_sz]), (c_sz, w)),
                                jnp.broadcast_to(fn(x[c_sz:2 * c_sz]), (c_sz, w))], axis=0)

    def rows_of(row_f, row_b):
        return jnp.concatenate([jnp.broadcast_to(row_f, (c_sz, w)), jnp.broadcast_to(row_b, (c_sz, w))], axis=0)

    def load(c_f, c_b, lo, hi):
        rf = pl.ds(pl.multiple_of(c_f * c_sz, c_sz), c_sz)
        rb = pl.ds(pl.multiple_of(c_b * c_sz, c_sz), c_sz)
        return pb_ref[0, rf, lo:hi], pb_ref[0, rb, lo:hi]

    def scan_step(n, carry):
        m_f, m_b = carry
        c_f = n
        c_b = jnp.where(n < n_ctx, n_ctx - 1 - n, n_all - 1 - (n - n_ctx))
        q_f, q_b = load(c_f, c_b, q_off, q_off + qk_w)
        k_f, k_b = load(c_f, c_b, k_off, k_off + qk_w)
        v_f, v_b = load(c_f, c_b, v_off, v_off + B_W)
        g_f, g_b = load(c_f, c_b, g_off, g_off + LANES)
        gts = jnp.concatenate([g_f + bias_ref[...],
                               pltpu.roll(g_b + bias_ref[...], LANES - 2 * HEADS, 1)], axis=0)
        log_f = jnp.minimum(gts, 0.0) - jnp.log(1.0 + jnp.exp(-jnp.abs(gts)))
        cum_col = sum(jnp.dot(tri2, p, preferred_element_type=F32) for p in _split3(log_f))
        both = sum(jnp.dot(p, sel, preferred_element_type=F32)
                   for p in _split3(jnp.concatenate([cum_col, gts], axis=1)))
        cum_t = both[:, :w]
        ig_t = both[:, w:]
        src = per_half(lambda x: jnp.sum(jnp.where(diag4, x, 0.0), axis=0, keepdims=True), cum_t - ig_t)
        m_prev = rows_of(m_f, m_b)
        log_d = jnp.where(mask2, cum_t - src, MASK_NEG)
        log_inter = cum_t + m_prev
        m_t = jnp.maximum(log_inter, head_max(log_d))
        w_intra = jnp.where(mask2, jnp.exp(log_d - m_t), 0.0)
        w_inter = jnp.exp(log_inter - m_t)

        scale = B_QK ** -0.5
        qb_f, qb_b = (q_f * scale).astype(BF16), (q_b * scale).astype(BF16)
        kb_f, kb_b = k_f.astype(BF16), k_b.astype(BF16)
        zero = jnp.zeros((), BF16)

        def scores(qb, kb):
            return _nt(qb, jnp.where(k_block, jnp.concatenate([kb] * HEADS, axis=0), zero))

        p = (jnp.concatenate([scores(qb_f, kb_f), scores(qb_b, kb_b)], axis=0) * w_intra).astype(BF16)

        def values(p_half, v_half):
            v_bd = jnp.where(v_block, jnp.concatenate([v_half.astype(BF16)] * HEADS, axis=0), zero)
            return jnp.dot(p_half, v_bd, preferred_element_type=F32)

        cn_f = cn_fwd_ref[...]
        cn_b = cn_bwd_ref[...]
        inter = jnp.concatenate([jnp.dot(qb_f, cn_f.astype(BF16), preferred_element_type=F32),
                                 jnp.dot(qb_b, cn_b.astype(BF16), preferred_element_type=F32)], axis=0)
        num = w_inter * inter[:, :w] + jnp.concatenate(
            [values(p[0:c_sz], v_f), values(p[c_sz:2 * c_sz], v_b)], axis=0)
        den = w_inter * inter[:, w:] + jnp.dot(p, head_ones, preferred_element_type=F32)
        h_out = num / jnp.maximum(jnp.abs(den), jnp.exp(-m_t))
        o_ref[0, pl.ds(pl.multiple_of(c_f * c_sz, c_sz), c_sz), :] = h_out[0:c_sz]
        ob_ref[pl.ds(pl.multiple_of(c_b * c_sz, c_sz), c_sz), :] = h_out[c_sz:2 * c_sz]

        end_f = cum_t[c_sz - 1:c_sz, :]
        end_b = cum_t[c_sz:c_sz + 1, :]
        log_end = rows_of(end_f, end_b) - cum_t + ig_t
        m_end_f = jnp.maximum(end_f + m_f, jnp.max(log_end[0:c_sz], axis=0, keepdims=True))
        m_end_b = jnp.maximum(end_b + m_b, jnp.max(log_end[c_sz:2 * c_sz], axis=0, keepdims=True))
        w_end = jnp.exp(log_end - rows_of(m_end_f, m_end_b))

        def update(cn_ref, cn, kb, w_half, v_half, carry_w):
            upd = _tn(kb, jnp.concatenate([(w_half * v_half).astype(BF16), w_half.astype(BF16)], axis=1))
            cn_ref[...] = jnp.concatenate([carry_w, carry_w], axis=1) * cn + jnp.where(state_block, upd, 0.0)

        update(cn_fwd_ref, cn_f, kb_f, w_end[0:c_sz], v_f, jnp.exp(end_f + m_f - m_end_f))
        update(cn_bwd_ref, cn_b, kb_b, w_end[c_sz:2 * c_sz], v_b, jnp.exp(end_b + m_b - m_end_b))
        return m_end_f, m_end_b

    zero_m = jnp.zeros((1, w), F32)
    cn_fwd_ref[...] = jnp.zeros_like(cn_fwd_ref)
    cn_bwd_ref[...] = jnp.zeros_like(cn_bwd_ref)
    lax.fori_loop(0, n_all, scan_step, (zero_m, zero_m), unroll=2)

    def readout(c, carry):
        rows = pl.ds(pl.multiple_of(c * ROW_TILE, ROW_TILE), ROW_TILE)
        tot = o_ref[0, rows, :] + ob_ref[rows, :]
        normed = jnp.concatenate(
            [_rms(tot[:, h * HEAD_V:(h + 1) * HEAD_V]) for h in range(HEADS)], axis=1) * gain_ref[...]
        og = pb_ref[0, rows, o_off:o_off + B_W]
        o_ref[0, rows, :] = normed * jax.nn.sigmoid(og)
        return carry

    lax.fori_loop(0, n_all * c_sz // ROW_TILE, readout, 0)


def _mlstm(pb3, bias, gain, n_ctx_rows):
    n_batch, rows, _ = pb3.shape
    kern = functools.partial(_mlstm_kernel, n_ctx=n_ctx_rows // CHUNK, n_all=rows // CHUNK)
    return pl.pallas_call(
        kern,
        out_shape=jax.ShapeDtypeStruct((n_batch, rows, B_W), F32),
        grid=(n_batch,),
        in_specs=[pl.BlockSpec((1, rows, PB_W), lambda b: (b, 0, 0)),
                  pl.BlockSpec((1, LANES), lambda b: (0, 0)),
                  pl.BlockSpec((1, B_W), lambda b: (0, 0))],
        out_specs=pl.BlockSpec((1, rows, B_W), lambda b: (b, 0, 0)),
        scratch_shapes=[pltpu.VMEM((HEADS * B_QK, 2 * B_W), F32), pltpu.VMEM((HEADS * B_QK, 2 * B_W), F32),
                        pltpu.VMEM((rows, B_W), F32)],
        compiler_params=_cparams(("arbitrary",)),
        name="mlstm",
    )(pb3, bias, gain)


def _attn_kernel(lam_ref, qp_ref, qr_ref, k_ref, v_ref, gain_ref, o_ref, *, n_ctx, q_tile0, lam_init):
    lam = lam_ref[0]
    q_tile = pl.program_id(2) + q_tile0
    lane = lax.broadcasted_iota(jnp.int32, (1, 2 * C_DQK), 1)
    scale = C_DQK ** -0.5

    def finish(parts):
        o = parts[0] - lam * parts[1]
        o_ref[0] = _rms(o) * gain_ref[...] * (1.0 - lam_init)

    def sub_query(q, j):
        return jnp.where(lane // C_DQK == j, q * scale, jnp.zeros_like(q))

    def row_max(s):
        return jnp.max(s, axis=-1, keepdims=True)

    def row_sum(s):
        return jnp.sum(s, axis=-1, keepdims=True)

    def pv(ex, v):
        return jnp.dot(ex.astype(BF16), v, preferred_element_type=F32)

    @pl.when(q_tile * ROW_TILE < n_ctx)
    def _():
        kc = k_ref[0, 0:n_ctx, :]
        vc = v_ref[0, 0:n_ctx, :]
        parts = []
        for j in range(2):
            s = _nt(sub_query(qp_ref[0], j), kc)
            ex = jnp.exp(s - row_max(s))
            parts.append(pv(ex, vc) / row_sum(ex))
        finish(parts)

    @pl.when(q_tile * ROW_TILE >= n_ctx)
    def _():
        n_all = k_ref.shape[1]
        kc = k_ref[0, 0:n_ctx, :]
        kl = k_ref[0, n_ctx:n_all, :]
        vc = v_ref[0, 0:n_ctx, :]
        vl = v_ref[0, n_ctx:n_all, :]
        parts = []
        for j in range(2):
            s_c = _nt(sub_query(qp_ref[0], j), kc)
            s_l = _nt(sub_query(qr_ref[0], j), kl)
            m = jnp.maximum(row_max(s_c), row_max(s_l))
            e_c = jnp.exp(s_c - m)
            e_l = jnp.exp(s_l - m)
            parts.append((pv(e_c, vc) + pv(e_l, vl)) / (row_sum(e_c) + row_sum(e_l)))
        finish(parts)


def _attention(pc3, lam, gain, n_ctx_rows, lam_init, with_ctx):
    n_batch, rows, _ = pc3.shape
    q_tile0 = 0 if with_ctx else n_ctx_rows // ROW_TILE
    n_q = rows // ROW_TILE - q_tile0
    hb = C_W // LANES
    kern = functools.partial(_attn_kernel, n_ctx=n_ctx_rows, q_tile0=q_tile0, lam_init=lam_init)
    grid_spec = pltpu.PrefetchScalarGridSpec(
        num_scalar_prefetch=1,
        grid=(n_batch, HEADS, n_q),
        in_specs=[pl.BlockSpec((1, ROW_TILE, LANES), lambda b, h, i, lam: (b, i + q_tile0, h)),
                  pl.BlockSpec((1, ROW_TILE, LANES), lambda b, h, i, lam: (b, i + q_tile0, hb + h)),
                  pl.BlockSpec((1, rows, LANES), lambda b, h, i, lam: (b, 0, 2 * hb + h)),
                  pl.BlockSpec((1, rows, LANES), lambda b, h, i, lam: (b, 0, 3 * hb + h)),
                  pl.BlockSpec((1, LANES), lambda b, h, i, lam: (0, 0))],
        out_specs=pl.BlockSpec((1, ROW_TILE, LANES), lambda b, h, i, lam: (b, i, h)),
    )
    return pl.pallas_call(
        kern,
        out_shape=jax.ShapeDtypeStruct((n_batch, n_q * ROW_TILE, C_W), F32),
        grid_spec=grid_spec,
        compiler_params=_cparams(("arbitrary", "arbitrary", "arbitrary")),
        name="diff_attn",
    )(lam, pc3, pc3, pc3, pc3, gain)


def _outproj_kernel(oa_ref, ob_ref, oc_ref, x_ref, mod_ref, n_ref, w_ref, rw_ref, rb_ref,
                    xn_ref, h_ref, route_ref, cnt_ref, carry_ref):
    d = x_ref.shape[1]
    n_exp = rw_ref.shape[1]

    @pl.when(pl.program_id(0) == 0)
    def _():
        carry_ref[...] = jnp.zeros_like(carry_ref)

    mix = (jnp.dot(oa_ref[...].astype(BF16), w_ref[0:A_W, :], preferred_element_type=F32)
           + jnp.dot(ob_ref[...].astype(BF16), w_ref[A_W:A_W + B_W, :], preferred_element_type=F32)
           + jnp.dot(oc_ref[...].astype(BF16), w_ref[A_W + B_W:A_W + B_W + C_W, :], preferred_element_type=F32))
    xn = x_ref[...] + mod_ref[0, :, 2 * d:3 * d] * (_rms(mix) * n_ref[1:2, :])
    xn_ref[...] = xn
    h = _rms(xn) * n_ref[2:3, :] * (1.0 + mod_ref[0, :, 4 * d:5 * d]) + mod_ref[0, :, 3 * d:4 * d]
    h_ref[...] = _pack_bf16_pairs(h)

    h_hi = h.astype(BF16)
    h_mid = (h - h_hi.astype(F32)).astype(BF16)
    logits = _nt(rw_ref[0], h_hi) + _nt(rw_ref[1], h_hi) + _nt(rw_ref[0], h_mid) + rb_ref[...]
    e_sub = lax.broadcasted_iota(jnp.int32, logits.shape, 0)
    cur = logits
    picks, vals = [], []
    for _ in range(TOP_K):
        mx = jnp.max(cur, axis=0, keepdims=True)
        idx = jnp.min(jnp.where(cur == mx, e_sub, n_exp), axis=0, keepdims=True)
        hit = e_sub == idx
        cur = jnp.where(hit, -jnp.inf, cur)
        picks.append((idx, hit.astype(F32)))
        vals.append(mx)
    exps = [jnp.exp(vv - vals[0]) for vv in vals]
    total = exps[0] + exps[1] + exps[2] + exps[3]

    chosen = picks[0][1] + picks[1][1] + picks[2][1] + picks[3][1]
    tm = logits.shape[1]
    before = (lax.broadcasted_iota(jnp.int32, (tm, tm), 0) < lax.broadcasted_iota(jnp.int32, (tm, tm), 1))
    seen = jnp.dot(chosen.astype(BF16), before.astype(BF16), preferred_element_type=F32) + carry_ref[...]
    new_carry = carry_ref[...] + jnp.sum(chosen, axis=1, keepdims=True)
    carry_ref[...] = new_carry
    cnt_ref[...] = new_carry

    o_row = lax.broadcasted_iota(jnp.int32, route_ref.shape, 0)
    route = jnp.zeros(route_ref.shape, F32)
    for kk in range(TOP_K):
        idx, hit = picks[kk]
        rank = jnp.sum(hit * seen, axis=0, keepdims=True)
        route = (route + jnp.where(o_row == kk, idx.astype(F32), 0.0)
                 + jnp.where(o_row == TOP_K + kk, exps[kk] / total, 0.0)
                 + jnp.where(o_row == 2 * TOP_K + kk, rank, 0.0))
    route_ref[...] = route


ROUTE_ROWS = 16


def _outproj(oa, ob, oc, xy, mod, norms, w_out_b, router_wt, router_b, layer, n_tiles, in_tile, oc_tile, mod_row):
    d = xy.shape[1]
    n_exp = router_wt.shape[2]
    t_out = n_tiles * ROW_TILE
    xn, h, route_t, counts = pl.pallas_call(
        _outproj_kernel,
        out_shape=(jax.ShapeDtypeStruct((t_out, d), F32),
                   jax.ShapeDtypeStruct((t_out, d // 2), jnp.int32),
                   jax.ShapeDtypeStruct((n_tiles * ROUTE_ROWS, ROW_TILE), F32),
                   jax.ShapeDtypeStruct((n_exp, 1), F32)),
        grid=(n_tiles,),
        in_specs=[pl.BlockSpec((ROW_TILE, A_W), lambda i: (in_tile(i), 0)),
                  pl.BlockSpec((ROW_TILE, B_W), lambda i: (in_tile(i), 0)),
                  pl.BlockSpec((ROW_TILE, C_W), lambda i: (oc_tile(i), 0)),
                  pl.BlockSpec((ROW_TILE, d), lambda i: (in_tile(i), 0)),
                  pl.BlockSpec((1, 1, 6 * d), lambda i: (mod_row(i), 0, 0)),
                  pl.BlockSpec((None, 4, d), lambda i: (layer, 0, 0)),
                  pl.BlockSpec((None, d, d), lambda i: (layer, 0, 0)),
                  pl.BlockSpec((None, 2, n_exp, d), lambda i: (layer, 0, 0, 0)),
                  pl.BlockSpec((None, n_exp, 1), lambda i: (layer, 0, 0))],
        out_specs=(pl.BlockSpec((ROW_TILE, d), lambda i: (i, 0)),
                   pl.BlockSpec((ROW_TILE, d // 2), lambda i: (i, 0)),
                   pl.BlockSpec((ROUTE_ROWS, ROW_TILE), lambda i: (i, 0)),
                   pl.BlockSpec((n_exp, 1), lambda i: (0, 0))),
        scratch_shapes=[pltpu.VMEM((n_exp, 1), F32)],
        compiler_params=_cparams(("arbitrary",)),
        name="outproj_router",
    )(oa, ob, oc, xy, mod, norms, w_out_b, router_wt, router_b)
    return xn, h, route_t.reshape(n_tiles, ROUTE_ROWS, ROW_TILE), counts[:, 0]


PAIR_BLOCK = 2 * LANES


def _moe_kernel(te_ref, first_ref, valid_ref, nu_ref, xs_ref, w1_ref, b1_ref, w2_ref, b2_ref, ys_ref,
                w1p_ref, w2b_ref):
    i = pl.program_id(0)
    two_f = w1_ref.shape[1]
    n_blk = two_f // PAIR_BLOCK

    @pl.when(jnp.logical_and(i < nu_ref[0], first_ref[i] == 1))
    def _():
        r = lax.broadcasted_iota(jnp.int32, (PAIR_BLOCK, PAIR_BLOCK), 0)
        c = lax.broadcasted_iota(jnp.int32, (PAIR_BLOCK, PAIR_BLOCK), 1)
        perm = (r == jnp.where(c < LANES, 2 * c, 2 * (c - LANES) + 1)).astype(BF16)
        for blk in range(n_blk):
            cols = slice(blk * PAIR_BLOCK, (blk + 1) * PAIR_BLOCK)
            w1p_ref[:, cols] = jnp.dot(w1_ref[:, cols].astype(BF16), perm,
                                       preferred_element_type=F32).astype(BF16)
        w2b_ref[...] = w2_ref[...].astype(BF16)

    @pl.when(i < nu_ref[0])
    def _():
        x = _unpack_bf16_pairs(xs_ref[...]).astype(BF16)
        row = lax.broadcasted_iota(jnp.int32, (x.shape[0], 1), 0)
        x = jnp.where(row < valid_ref[i], x, jnp.zeros_like(x))
        hid = jnp.dot(x, w1p_ref[...], preferred_element_type=F32) + b1_ref[...]
        acts = []
        for blk in range(n_blk):
            glu = jnp.minimum(hid[:, blk * PAIR_BLOCK:blk * PAIR_BLOCK + LANES], SWIGLU_LIMIT)
            lin = jnp.clip(hid[:, blk * PAIR_BLOCK + LANES:(blk + 1) * PAIR_BLOCK], -SWIGLU_LIMIT, SWIGLU_LIMIT)
            acts.append((glu * jax.nn.sigmoid(SWIGLU_ALPHA * glu) * (lin + 1.0)).astype(BF16))
        y = jnp.dot(jnp.concatenate(acts, axis=1), w2b_ref[...], preferred_element_type=F32) + b2_ref[...]
        ys_ref[...] = _pack_bf16_pairs(y)

    @pl.when(i >= nu_ref[0])
    def _():
        ys_ref[...] = jnp.zeros_like(ys_ref)


def _moe(tile_expert, tile_first, tile_valid, n_used, xs, w1, b1p, w2, b2, layer):
    r_max, half_d = xs.shape
    d = 2 * half_d
    two_f = w1.shape[3]
    f = two_f // 2
    n_tiles = r_max // MOE_TILE

    def row_tile(i, te, fi, va, nu):
        return (jnp.maximum(jnp.minimum(i, nu[0] - 1), 0), 0)

    grid_spec = pltpu.PrefetchScalarGridSpec(
        num_scalar_prefetch=4,
        grid=(n_tiles,),
        in_specs=[pl.BlockSpec((MOE_TILE, half_d), row_tile),
                  pl.BlockSpec((None, None, d, two_f), lambda i, te, fi, va, nu: (layer, te[i], 0, 0)),
                  pl.BlockSpec((None, None, 1, two_f), lambda i, te, fi, va, nu: (layer, te[i], 0, 0)),
                  pl.BlockSpec((None, None, f, d), lambda i, te, fi, va, nu: (layer, te[i], 0, 0)),
                  pl.BlockSpec((None, None, 1, d), lambda i, te, fi, va, nu: (layer, te[i], 0, 0))],
        out_specs=pl.BlockSpec((MOE_TILE, half_d), lambda i, te, fi, va, nu: (i, 0)),
        scratch_shapes=[pltpu.VMEM((d, two_f), BF16), pltpu.VMEM((f, d), BF16)],
    )
    return pl.pallas_call(
        _moe_kernel,
        out_shape=jax.ShapeDtypeStruct((r_max, half_d), jnp.int32),
        grid_spec=grid_spec,
        compiler_params=_cparams(("arbitrary",)),
        name="moe_experts",
    )(tile_expert, tile_first, tile_valid, n_used, xs, w1, b1p, w2, b2)


SC_CORES = 2
SC_SUBCORES = 16
SC_CHUNK = 64


def _dispatch_rows(hp, pos, r_max):
    t, width = hp.shape
    workers = SC_CORES * SC_SUBCORES
    assert t % (workers * SC_CHUNK) == 0
    per_worker = t // (workers * SC_CHUNK)
    idx = pos.reshape(-1, TOP_K, ROW_TILE // SC_CHUNK, SC_CHUNK).transpose(0, 2, 1, 3).reshape(
        workers, per_worker * TOP_K, SC_CHUNK)
    mesh = plsc.VectorSubcoreMesh(core_axis_name="c", subcore_axis_name="s",
                                  num_cores=SC_CORES, num_subcores=SC_SUBCORES)

    @functools.partial(
        pl.kernel, mesh=mesh,
        out_type=jax.ShapeDtypeStruct((r_max, width), hp.dtype),
        scratch_types=[pltpu.VMEM((per_worker * TOP_K, SC_CHUNK), jnp.int32),
                       pltpu.VMEM((2, SC_CHUNK, width), hp.dtype),
                       pltpu.SemaphoreType.DMA((2,)),
                       pltpu.SemaphoreType.DMA((2,))],
    )
    def scatter(hp_hbm, idx_hbm, out_hbm, idx_v, rows_v, read_sem, write_sem):
        wid = lax.axis_index("s") * SC_CORES + lax.axis_index("c")
        pltpu.sync_copy(idx_hbm.at[wid], idx_v)

        def read(j):
            row0 = pl.multiple_of((wid * per_worker + j) * SC_CHUNK, SC_CHUNK)
            return pltpu.make_async_copy(hp_hbm.at[pl.ds(row0, SC_CHUNK)], rows_v.at[j % 2], read_sem.at[j % 2])

        def write(j, k):
            return pltpu.make_async_copy(rows_v.at[j % 2], out_hbm.at[idx_v.at[j * TOP_K + k]],
                                         write_sem.at[j % 2])

        read(0).start()
        for j in range(per_worker):
            read(j).wait()
            for k in range(TOP_K):
                write(j, k).start()
            if j + 1 < per_worker:
                if j >= 1:
                    for k in range(TOP_K):
                        write(j - 1, k).wait()
                read(j + 1).start()
        for j in range(max(per_worker - 2, 0), per_worker):
            for k in range(TOP_K):
                write(j, k).wait()

    return scatter(hp, idx)


def _combine_rows(ysp, pos):
    width = ysp.shape[1]
    t = pos.shape[0] * pos.shape[2]
    workers = SC_CORES * SC_SUBCORES
    assert t % (workers * SC_CHUNK) == 0
    per_worker = t // (workers * SC_CHUNK)
    units = per_worker * TOP_K
    idx = pos.reshape(-1, TOP_K, ROW_TILE // SC_CHUNK, SC_CHUNK).transpose(0, 2, 1, 3).reshape(
        workers, units, SC_CHUNK)
    mesh = plsc.VectorSubcoreMesh(core_axis_name="c", subcore_axis_name="s",
                                  num_cores=SC_CORES, num_subcores=SC_SUBCORES)

    @functools.partial(
        pl.kernel, mesh=mesh,
        out_type=jax.ShapeDtypeStruct((TOP_K, t, width), ysp.dtype),
        scratch_types=[pltpu.VMEM((units, SC_CHUNK), jnp.int32),
                       pltpu.VMEM((2, SC_CHUNK, width), ysp.dtype),
                       pltpu.SemaphoreType.DMA((2,)),
                       pltpu.SemaphoreType.DMA((2,))],
    )
    def gather(ys_hbm, idx_hbm, out_hbm, idx_v, rows_v, read_sem, write_sem):
        wid = lax.axis_index("s") * SC_CORES + lax.axis_index("c")
        pltpu.sync_copy(idx_hbm.at[wid], idx_v)

        def read(u):
            return pltpu.make_async_copy(ys_hbm.at[idx_v.at[u]], rows_v.at[u % 2], read_sem.at[u % 2])

        def write(u):
            row0 = pl.multiple_of((wid * per_worker + u // TOP_K) * SC_CHUNK, SC_CHUNK)
            return pltpu.make_async_copy(rows_v.at[u % 2], out_hbm.at[u % TOP_K, pl.ds(row0, SC_CHUNK)],
                                         write_sem.at[u % 2])

        read(0).start()
        for u in range(units):
            read(u).wait()
            write(u).start()
            if u + 1 < units:
                if u >= 1:
                    write(u - 1).wait()
                read(u + 1).start()
        for u in range(max(units - 2, 0), units):
            write(u).wait()

    return gather(ysp, idx)


def _ffn_residual_kernel(x_ref, w_ref, y0_ref, y1_ref, y2_ref, y3_ref, mod_ref, n_ref, o_ref):
    d = x_ref.shape[1]
    wts = w_ref[...]
    ffn = None
    for k, y_ref in enumerate((y0_ref, y1_ref, y2_ref, y3_ref)):
        term = wts[:, k:k + 1] * _unpack_bf16_pairs(y_ref[...])
        ffn = term if ffn is None else ffn + term
    o_ref[...] = x_ref[...] + mod_ref[0, :, 5 * d:6 * d] * (_rms(ffn) * n_ref[3:4, :])


def _ffn_residual(xn, wts, ys_by_k, mod, norms, layer, mod_row):
    t, d = xn.shape
    row_spec = pl.BlockSpec((ROW_TILE, d), lambda i: (i, 0))

    def y_spec(k):
        return pl.BlockSpec((None, ROW_TILE, d // 2), lambda i: (k, i, 0))

    return pl.pallas_call(
        _ffn_residual_kernel,
        out_shape=jax.ShapeDtypeStruct((t, d), F32),
        grid=(t // ROW_TILE,),
        in_specs=[row_spec,
                  pl.BlockSpec((ROW_TILE, TOP_K), lambda i: (i, 0)),
                  y_spec(0), y_spec(1), y_spec(2), y_spec(3),
                  pl.BlockSpec((1, 1, 6 * d), lambda i: (mod_row(i), 0, 0)),
                  pl.BlockSpec((None, 4, d), lambda i: (layer, 0, 0))],
        out_specs=row_spec,
        compiler_params=_cparams(("arbitrary",)),
        name="ffn_residual",
    )(xn, wts, ys_by_k, ys_by_k, ys_by_k, ys_by_k, mod, norms)


def _dispatch_plan(route_t, counts, r_max):
    n_row_tiles = route_t.shape[0]
    idx = route_t[:, 0:TOP_K, :].astype(jnp.int32)
    wts = route_t[:, TOP_K:2 * TOP_K, :].transpose(0, 2, 1).reshape(n_row_tiles * ROW_TILE, TOP_K)
    rank = route_t[:, 2 * TOP_K:3 * TOP_K, :].astype(jnp.int32)
    cnt = counts.astype(jnp.int32)
    padded = (cnt + MOE_TILE - 1) // MOE_TILE * MOE_TILE
    ends = jnp.cumsum(padded)
    starts = ends - padded
    pos = rank
    for e in range(cnt.shape[0]):
        pos = pos + jnp.where(idx == e, starts[e], 0)
    n_tiles = r_max // MOE_TILE
    n_used = ends[-1] // MOE_TILE
    tile_ids = jnp.minimum(jnp.arange(n_tiles, dtype=jnp.int32), n_used - 1)
    tile_expert = jnp.sum((ends // MOE_TILE)[None, :] <= tile_ids[:, None], axis=1).astype(jnp.int32)
    tile_first = jnp.concatenate(
        [jnp.ones((1,), jnp.int32), (tile_expert[1:] != tile_expert[:-1]).astype(jnp.int32)])
    of_expert = (tile_expert[:, None] == jnp.arange(cnt.shape[0], dtype=jnp.int32)[None, :]).astype(jnp.int32)
    first_tile = jnp.sum(of_expert * ((ends - padded) // MOE_TILE)[None, :], axis=1)
    tile_valid = jnp.clip(jnp.sum(of_expert * cnt[None, :], axis=1) - (tile_ids - first_tile) * MOE_TILE,
                          0, MOE_TILE).astype(jnp.int32)
    return pos, wts, tile_expert, tile_first, tile_valid, n_used.reshape(1).astype(jnp.int32)


def _rope_tables(length, n_ctx_rows):
    rows = length // GRID_W
    row = jnp.repeat(jnp.arange(rows, dtype=F32), GRID_W)
    col = jnp.tile(jnp.arange(GRID_W, dtype=F32), rows)
    n_freq = C_DQK // 4
    inv_freq = ROPE_THETA ** (-jnp.arange(n_freq, dtype=F32) / n_freq)
    ang_r = row[:, None] * inv_freq
    ang_c = col[:, None] * inv_freq
    cos = jnp.concatenate([jnp.cos(ang_r), jnp.cos(ang_r), jnp.cos(ang_c), jnp.cos(ang_c)], axis=-1)
    sin = jnp.concatenate([-jnp.sin(ang_r), jnp.sin(ang_r), -jnp.sin(ang_c), jnp.sin(ang_c)], axis=-1)
    reps = C_W // C_DQK
    cos = jnp.concatenate([jnp.ones((n_ctx_rows, C_W), F32), jnp.tile(cos, (1, reps))], axis=0)
    sin = jnp.concatenate([jnp.zeros((n_ctx_rows, C_W), F32), jnp.tile(sin, (1, reps))], axis=0)
    return cos, sin


def _hgrn_lower_bound(table, layer):
    p = jax.nn.softmax(table.astype(F32), axis=1)
    cum = jnp.cumsum(p, axis=1) - p[:, :1]
    return jnp.clip(cum[:, layer], 0.0, 1.0)


def kernel(x, c, ctx, c_ctx, ada_w, ada_b, sandwich_norms, w_in, w_out, hgrn_lower_bounds, hgrn_norm,
           mlstm_gate_bias, mlstm_norm, diff_lambdas, diff_norm, router_w, router_b, moe_w1, moe_b1,
           moe_w2, moe_b2):
    n_batch, seq, d = x.shape
    n_ctx_rows = ctx.shape[1]
    depth = w_in.shape[0]
    n_exp = router_w.shape[2]
    assert seq % ROW_TILE == 0 and n_ctx_rows % ROW_TILE == 0 and seq % GRID_W == 0
    rows_b = n_ctx_rows + seq
    tiles_b = rows_b // ROW_TILE
    ctx_tiles = n_ctx_rows // ROW_TILE
    lat_tiles = seq // ROW_TILE

    b_main = PA_W + 2 * HEADS * B_QK + 2 * B_W
    w_in_p = jnp.concatenate(
        [w_in[:, :, :b_main],
         jnp.pad(w_in[:, :, b_main:b_main + N_GATE], ((0, 0), (0, 0), (0, LANES - N_GATE))),
         w_in[:, :, b_main + N_GATE:]], axis=2).astype(BF16)
    w_out_b = w_out.astype(BF16)
    two_f = moe_b1.shape[2]
    b1p = moe_b1.reshape(depth, n_exp, two_f // PAIR_BLOCK, LANES, 2).transpose(0, 1, 2, 4, 3).reshape(
        depth, n_exp, 1, two_f)
    b2 = moe_b2[:, :, None, :]
    ada_b3 = ada_b[:, None, :]
    router_b3 = router_b[:, :, None]
    rw_t = router_w.transpose(0, 2, 1)
    rw_hi = rw_t.astype(BF16)
    router_wt = jnp.stack([rw_hi, (rw_t - rw_hi.astype(F32)).astype(BF16)], axis=1)
    gate_bias = jnp.pad(mlstm_gate_bias, ((0, 0), (0, LANES - N_GATE)))
    cos, sin = _rope_tables(seq, n_ctx_rows)

    cond_rows = (n_batch + 1 + 7) // 8 * 8
    cvec = jnp.zeros((cond_rows, d), F32).at[:n_batch].set(c).at[n_batch].set(c_ctx)

    def mod_row_all(i):
        return jnp.where(i % tiles_b < ctx_tiles, n_batch, i // tiles_b)

    xy = jnp.concatenate([ctx, x], axis=1).reshape(n_batch * rows_b, d)

    for layer in range(depth):
        last = layer == depth - 1
        mod = _adaln(cvec, ada_w, ada_b3, layer).reshape(cond_rows, 1, 6 * d)
        lb = _hgrn_lower_bound(hgrn_lower_bounds, layer)
        lam_init = 0.8 - 0.6 * math.exp(-0.3 * layer)
        lq1, lk1, lq2, lk2 = diff_lambdas[layer].astype(F32)
        lam = (jnp.exp(jnp.sum(lq1 * lk1)) - jnp.exp(jnp.sum(lq2 * lk2)) + lam_init).reshape(1)

        pa, pb, pc = _inproj(xy, mod, sandwich_norms, w_in_p, cos, sin, layer, tiles_b, ctx_tiles, n_batch)
        oa = _hgrn(pa.reshape(n_batch, rows_b, PA_W), lb, jnp.tile(hgrn_norm[layer], HEADS)[None, :], n_ctx_rows)
        ob = _mlstm(pb.reshape(n_batch, rows_b, PB_W), gate_bias[layer][None, :], mlstm_norm[layer][None, :],
                    n_ctx_rows)
        oc = _attention(pc.reshape(n_batch, rows_b, PC_OUT), lam, diff_norm[layer][None, :], n_ctx_rows,
                        lam_init, not last)

        if last:
            n_tiles = n_batch * lat_tiles
            in_tile = lambda i: (i // lat_tiles) * tiles_b + ctx_tiles + i % lat_tiles
            mod_row = lambda i: i // lat_tiles
        else:
            n_tiles = n_batch * tiles_b
            in_tile = lambda i: i
            mod_row = mod_row_all
        xn, h, route, counts = _outproj(
            oa.reshape(-1, A_W), ob.reshape(-1, B_W), oc.reshape(-1, C_W), xy, mod, sandwich_norms, w_out_b,
            router_wt, router_b3, layer, n_tiles, in_tile, lambda i: i, mod_row)

        r_max = n_tiles * ROW_TILE * TOP_K + n_exp * MOE_TILE
        pos, wts, tile_expert, tile_first, tile_valid, n_used = _dispatch_plan(route, counts, r_max)
        xs = _dispatch_rows(h, pos, r_max)
        ys = _moe(tile_expert, tile_first, tile_valid, n_used, xs, moe_w1, b1p, moe_w2, b2, layer)
        xy = _ffn_residual(xn, wts, _combine_rows(ys, pos), mod, sandwich_norms, layer, mod_row)

    return xy.reshape(n_batch, seq, d)
```

```python
import functools
import math

import jax
import jax.numpy as jnp
from jax import lax
from jax.experimental import pallas as pl
from jax.experimental.pallas import tpu as pltpu
from jax.experimental.pallas import tpu_sc as plsc

F32 = jnp.float32
BF16 = jnp.bfloat16
HI = lax.Precision.HIGHEST

HEADS = 4
A_W = 256
B_QK = 32
B_W = 256
C_DQK = 64
C_DV = 2 * C_DQK
C_W = 512
HEAD_V = 64
N_GATE = 16
GRID_W = 64
TOP_K = 4
SWIGLU_ALPHA = 1.702
SWIGLU_LIMIT = 7.0
ROPE_THETA = 10000.0
NORM_EPS = 1e-6
MASK_NEG = -1e30
F_MIN = 1e-12

LANES = 128
ROW_TILE = 256
CHUNK = 64
MOE_TILE = 512
VMEM_LIMIT = 56 * 1024 * 1024

PA_W = 5 * A_W
PB_W = 2 * HEADS * B_QK + 2 * B_W + LANES
PC_IN = 3 * C_W
PC_OUT = 4 * C_W
W_IN_PAD = PA_W + PB_W + PC_IN


def _cparams(sem, flags=None):
    return pltpu.CompilerParams(dimension_semantics=sem, vmem_limit_bytes=VMEM_LIMIT, flags=flags)


def _nt(a, b):
    return lax.dot_general(a, b, (((1,), (1,)), ((), ())), preferred_element_type=F32)


def _tn(a, b, precision=None):
    return lax.dot_general(a, b, (((0,), (0,)), ((), ())), preferred_element_type=F32, precision=precision)


def _rms(x):
    return x * lax.rsqrt(jnp.mean(x * x, axis=-1, keepdims=True) + NORM_EPS)


def _silu(x):
    return x * jax.nn.sigmoid(x)


def _pack_bf16_pairs(x):
    half = x.shape[1] // 2
    bits = pltpu.bitcast(x.astype(BF16).astype(F32), jnp.uint32)
    return pltpu.bitcast((bits[:, :half] >> 16) | (bits[:, half:] & jnp.uint32(0xFFFF0000)), jnp.int32)


def _unpack_bf16_pairs(words):
    bits = pltpu.bitcast(words, jnp.uint32)
    return jnp.concatenate([pltpu.bitcast(bits << 16, F32),
                            pltpu.bitcast(bits & jnp.uint32(0xFFFF0000), F32)], axis=1)


def _adaln_kernel(c_ref, w_ref, b_ref, o_ref):
    cond = _silu(c_ref[...])
    o_ref[...] = jnp.dot(cond, w_ref[...], preferred_element_type=F32, precision=HI) + b_ref[...]


def _adaln(cvec, ada_w, ada_b, layer):
    rows, d = cvec.shape
    return pl.pallas_call(
        _adaln_kernel,
        out_shape=jax.ShapeDtypeStruct((rows, 6 * d), F32),
        grid=(6,),
        in_specs=[pl.BlockSpec((rows, d), lambda j: (0, 0)),
                  pl.BlockSpec((None, d, d), lambda j: (layer, 0, j)),
                  pl.BlockSpec((None, 1, d), lambda j: (layer, 0, j))],
        out_specs=pl.BlockSpec((rows, d), lambda j: (0, j)),
        compiler_params=_cparams(("arbitrary",)),
        name="adaln",
    )(cvec, ada_w, ada_b)


def _inproj_kernel(x_ref, mod_ref, g_ref, w_ref, cos_ref, sin_ref, pa_ref, pb_ref, pc_ref):
    d = x_ref.shape[1]
    shift = mod_ref[0, :, 0:d]
    scale = mod_ref[0, :, d:2 * d]
    h = _rms(x_ref[...]) * g_ref[0:1, :] * (1.0 + scale) + shift
    hb = h.astype(BF16)
    pa_ref[...] = jnp.dot(hb, w_ref[:, 0:PA_W], preferred_element_type=F32)
    pb_ref[...] = jnp.dot(hb, w_ref[:, PA_W:PA_W + PB_W], preferred_element_type=F32)
    pc = jnp.dot(hb, w_ref[:, PA_W + PB_W:W_IN_PAD], preferred_element_type=F32)
    q = pc[:, 0:C_W]
    k = pc[:, C_W:2 * C_W]
    cos = cos_ref[...]
    sin = sin_ref[...]
    lane = lax.broadcasted_iota(jnp.int32, q.shape, 1)
    first = (lane % 32) < 16

    def rope(t):
        partner = jnp.where(first, pltpu.roll(t, C_W - 16, 1), pltpu.roll(t, 16, 1))
        return t * cos + partner * sin

    pc_ref[:, 0:C_W] = q.astype(BF16)
    pc_ref[:, C_W:2 * C_W] = rope(q).astype(BF16)
    pc_ref[:, 2 * C_W:3 * C_W] = rope(k).astype(BF16)
    pc_ref[:, 3 * C_W:4 * C_W] = pc[:, 2 * C_W:3 * C_W].astype(BF16)


def _inproj(xy, mod, norms, w_in_p, cos, sin, layer, tiles_per_batch, n_ctx_tiles, n_batch):
    t_all, d = xy.shape
    n_tiles = t_all // ROW_TILE

    def mod_row(i):
        return jnp.where(i % tiles_per_batch < n_ctx_tiles, n_batch, i // tiles_per_batch)

    return pl.pallas_call(
        _inproj_kernel,
        out_shape=(jax.ShapeDtypeStruct((t_all, PA_W), F32),
                   jax.ShapeDtypeStruct((t_all, PB_W), F32),
                   jax.ShapeDtypeStruct((t_all, PC_OUT), BF16)),
        grid=(n_tiles,),
        in_specs=[pl.BlockSpec((ROW_TILE, d), lambda i: (i, 0)),
                  pl.BlockSpec((1, 1, 6 * d), lambda i: (mod_row(i), 0, 0)),
                  pl.BlockSpec((None, 4, d), lambda i: (layer, 0, 0)),
                  pl.BlockSpec((None, d, W_IN_PAD), lambda i: (layer, 0, 0)),
                  pl.BlockSpec((ROW_TILE, C_W), lambda i: (i % tiles_per_batch, 0)),
                  pl.BlockSpec((ROW_TILE, C_W), lambda i: (i % tiles_per_batch, 0))],
        out_specs=(pl.BlockSpec((ROW_TILE, PA_W), lambda i: (i, 0)),
                   pl.BlockSpec((ROW_TILE, PB_W), lambda i: (i, 0)),
                   pl.BlockSpec((ROW_TILE, PC_OUT), lambda i: (i, 0))),
        compiler_params=_cparams(("arbitrary",)),
        name="inproj",
    )(xy, mod, norms, w_in_p, cos, sin)


def _hgrn_kernel(pa_ref, lb_ref, gain_ref, o_ref, st_ref, ob_ref, *, n_ctx, n_all):
    c_sz, w = CHUNK, A_W
    r_i = lax.broadcasted_iota(jnp.int32, (c_sz, c_sz), 0)
    c_i = lax.broadcasted_iota(jnp.int32, (c_sz, c_sz), 1)
    tri_incl_past = (c_i <= r_i).astype(BF16)
    tri_incl_future = (c_i >= r_i).astype(BF16)
    same_head = (lax.broadcasted_iota(jnp.int32, (w, w), 0) // HEAD_V
                 == lax.broadcasted_iota(jnp.int32, (w, w), 1) // HEAD_V)
    head_ones = same_head.astype(BF16)
    head_ones_f32 = same_head.astype(F32)
    assert c_sz == HEAD_V
    t_i = lax.broadcasted_iota(jnp.int32, (c_sz, w), 0)
    s_i = lax.broadcasted_iota(jnp.int32, (c_sz, w), 1) % c_sz
    t_row = lax.broadcasted_iota(jnp.int32, (c_sz, 1), 0)
    row8 = t_row % 8

    def grouped_rows(a, k):
        return jnp.concatenate(
            [jnp.broadcast_to(a[8 * j + k:8 * j + k + 1, :], (8, w)) for j in range(c_sz // 8)], axis=0)

    def halving_levels(rev):
        out = []
        b = c_sz // 2
        while b >= 1:
            def later(i):
                return ((i % (2 * b)) < b) if rev else ((i % (2 * b)) >= b)
            live = jnp.logical_and(t_i // (2 * b) == s_i // (2 * b),
                                   jnp.logical_and(later(t_i), jnp.logical_not(later(s_i))))
            out.append((b, live.astype(F32), jnp.where(later(t_row), 1.0, -1.0)))
            b //= 2
        return out

    def both(x_f, x_b):
        return jnp.concatenate([x_f, x_b], axis=0)

    def rows_of(row_f, row_b):
        return both(jnp.broadcast_to(row_f, (c_sz, w)), jnp.broadcast_to(row_b, (c_sz, w)))

    self_mask = both(*[(s_i == t_i).astype(F32)] * 2)
    levels = [(b, both(live_f, live_b), both(sign_f, sign_b))
              for (b, live_f, sign_f), (_, live_b, sign_b) in zip(halving_levels(False), halving_levels(True))]
    lb2 = rows_of(lb_ref[0:1, :], lb_ref[1:2, :])
    r2 = lax.broadcasted_iota(jnp.int32, (2 * c_sz, 2 * c_sz), 0)
    c2 = lax.broadcasted_iota(jnp.int32, (2 * c_sz, 2 * c_sz), 1)
    tri2 = jnp.logical_or(jnp.logical_and(r2 < c_sz, c2 <= r2),
                          jnp.logical_and(r2 >= c_sz, c2 >= r2)).astype(BF16)
    zero = jnp.zeros((), BF16)

    def boundary_rows(cum, b, rev):
        if b >= 8:
            return jnp.concatenate(
                [jnp.broadcast_to(cum[r0 + (b if rev else b - 1):r0 + (b if rev else b - 1) + 1, :], (2 * b, w))
                 for r0 in range(0, c_sz, 2 * b)], axis=0)
        ref = None
        for g in reversed(range(8 // (2 * b))):
            cand = grouped_rows(cum, g * 2 * b + (b if rev else b - 1))
            ref = cand if ref is None else jnp.where(row8 < (g + 1) * 2 * b, cand, ref)
        return ref

    def scan_step(n, carry):
        c_f = n
        c_b = jnp.where(n < n_ctx, n_ctx - 1 - n, n_all - 1 - (n - n_ctx))
        rows_f = pl.ds(pl.multiple_of(c_f * c_sz, c_sz), c_sz)
        rows_b = pl.ds(pl.multiple_of(c_b * c_sz, c_sz), c_sz)
        q_pre = both(pa_ref[0, rows_f, 0:A_W], pa_ref[0, rows_b, 0:A_W])
        v_f = pa_ref[0, rows_f, A_W:2 * A_W].astype(BF16)
        v_b = pa_ref[0, rows_b, A_W:2 * A_W].astype(BF16)
        f_pre = both(pa_ref[0, rows_f, 2 * A_W:3 * A_W], pa_ref[0, rows_b, 3 * A_W:4 * A_W])
        q = _silu(q_pre)
        f = lb2 + (1.0 - lb2) * jax.nn.sigmoid(f_pre)
        log_f = jnp.log(jnp.maximum(f, F_MIN))
        kk = (1.0 - lb2) * jax.nn.sigmoid(-f_pre)
        cum = sum(jnp.dot(tri2, piece, preferred_element_type=F32) for piece in _split3(log_f))
        end_f = cum[c_sz - 1:c_sz, :]
        end_b = cum[c_sz:c_sz + 1, :]

        st_f = st_ref[0]
        st_b = st_ref[1]
        q_in = (q * jnp.exp(cum)).astype(BF16)
        o = both(_nt(q_in[0:c_sz], st_f.astype(BF16)), _nt(q_in[c_sz:2 * c_sz], st_b.astype(BF16)))
        k_end = (kk * jnp.exp(rows_of(end_f, end_b) - cum)).astype(BF16)
        st_ref[0] = st_f * jnp.exp(end_f) + jnp.where(same_head, _tn(v_f, k_end[0:c_sz]), 0.0)
        st_ref[1] = st_b * jnp.exp(end_b) + jnp.where(same_head, _tn(v_b, k_end[c_sz:2 * c_sz]), 0.0)

        def scores(qa, ka):
            return _nt(qa, jnp.where(same_head, jnp.concatenate([ka] * HEADS, axis=0), zero))

        p = self_mask * jnp.dot((q * kk).astype(BF16), head_ones, preferred_element_type=F32)
        for b, live, sign in levels:
            ref = both(boundary_rows(cum[0:c_sz], b, False), boundary_rows(cum[c_sz:2 * c_sz], b, True))
            z = jnp.exp((cum - ref) * sign)
            qa = (q * z).astype(BF16)
            ka = (kk * z).astype(BF16)
            p = p + live * both(scores(qa[0:c_sz], ka[0:c_sz]), scores(qa[c_sz:2 * c_sz], ka[c_sz:2 * c_sz]))
        pb = p.astype(BF16)

        def values(p_half, v_half):
            return jnp.dot(p_half, jnp.where(same_head, jnp.concatenate([v_half] * HEADS, axis=0), zero),
                           preferred_element_type=F32)

        o_ref[0, rows_f, :] = o[0:c_sz] + values(pb[0:c_sz], v_f)
        ob_ref[rows_b, :] = o[c_sz:2 * c_sz] + values(pb[c_sz:2 * c_sz], v_b)
        return carry

    st_ref[...] = jnp.zeros_like(st_ref)
    lax.fori_loop(0, n_all, scan_step, 0)

    def readout(c, carry):
        rows = pl.ds(pl.multiple_of(c * ROW_TILE, ROW_TILE), ROW_TILE)
        tot = o_ref[0, rows, :] + ob_ref[rows, :]
        ms = jnp.dot(tot * tot, head_ones_f32, preferred_element_type=F32, precision=HI) * (1.0 / HEAD_V)
        g = pa_ref[0, rows, 4 * A_W:5 * A_W]
        o_ref[0, rows, :] = tot * lax.rsqrt(ms + NORM_EPS) * gain_ref[...] * _silu(g)
        return carry

    lax.fori_loop(0, n_all * c_sz // ROW_TILE, readout, 0)


def _hgrn(pa3, lb, gain, n_ctx_rows):
    n_batch, rows, _ = pa3.shape
    kern = functools.partial(_hgrn_kernel, n_ctx=n_ctx_rows // CHUNK, n_all=rows // CHUNK)
    return pl.pallas_call(
        kern,
        out_shape=jax.ShapeDtypeStruct((n_batch, rows, A_W), F32),
        grid=(n_batch,),
        in_specs=[pl.BlockSpec((1, rows, PA_W), lambda b: (b, 0, 0)),
                  pl.BlockSpec((2, A_W), lambda b: (0, 0)),
                  pl.BlockSpec((1, A_W), lambda b: (0, 0))],
        out_specs=pl.BlockSpec((1, rows, A_W), lambda b: (b, 0, 0)),
        scratch_shapes=[pltpu.VMEM((2, A_W, A_W), F32), pltpu.VMEM((rows, A_W), F32)],
        compiler_params=_cparams(("arbitrary",)),
        name="hgrn2",
    )(pa3, lb, gain)


def _split3(x):
    hi = x.astype(BF16)
    rest = x - hi.astype(F32)
    mid = rest.astype(BF16)
    return hi, mid, (rest - mid.astype(F32)).astype(BF16)


def _mlstm_kernel(pb_ref, bias_ref, gain_ref, o_ref, cn_fwd_ref, cn_bwd_ref, ob_ref, *, n_ctx, n_all):
    assert CHUNK == HEAD_V
    c_sz, w, qk_w = CHUNK, B_W, HEADS * B_QK
    t_i = lax.broadcasted_iota(jnp.int32, (c_sz, w), 0)
    s_i = lax.broadcasted_iota(jnp.int32, (c_sz, w), 1) % c_sz
    r_i = lax.broadcasted_iota(jnp.int32, (c_sz, c_sz), 0)
    c_i = lax.broadcasted_iota(jnp.int32, (c_sz, c_sz), 1)
    diag4 = s_i == t_i
    half_lane = lax.broadcasted_iota(jnp.int32, (1, LANES), 1) < HEAD_V
    k_block = (lax.broadcasted_iota(jnp.int32, (HEADS * c_sz, qk_w), 0) // c_sz
               == lax.broadcasted_iota(jnp.int32, (HEADS * c_sz, qk_w), 1) // B_QK)
    v_block = (lax.broadcasted_iota(jnp.int32, (HEADS * c_sz, w), 0) // c_sz
               == lax.broadcasted_iota(jnp.int32, (HEADS * c_sz, w), 1) // HEAD_V)
    state_block = (lax.broadcasted_iota(jnp.int32, (qk_w, 2 * w), 0) // B_QK
                   == (lax.broadcasted_iota(jnp.int32, (qk_w, 2 * w), 1) % w) // HEAD_V)
    head_ones = v_block.astype(BF16)
    q_off, k_off, v_off, o_off, g_off = 0, qk_w, 2 * qk_w, 2 * qk_w + B_W, 2 * qk_w + 2 * B_W

    def selector(rev):
        sr = lax.broadcasted_iota(jnp.int32, (2 * LANES, 2 * w), 0)
        sc = lax.broadcasted_iota(jnp.int32, (2 * LANES, 2 * w), 1)
        i_base = 2 * HEADS if rev else 0
        want = jnp.where(sc < w, i_base + HEADS + sc // HEAD_V, LANES + i_base + (sc - w) // HEAD_V)
        return (sr == want).astype(BF16)

    def head_max(x):
        outs = []
        for col in range(w // LANES):
            xc = x[:, col * LANES:(col + 1) * LANES]
            lo = jnp.max(jnp.where(half_lane, xc, -jnp.inf), axis=-1, keepdims=True)
            hi = jnp.max(jnp.where(half_lane, -jnp.inf, xc), axis=-1, keepdims=True)
            outs.append(jnp.where(half_lane, lo, hi))
        return jnp.concatenate(outs, axis=1)

    sel = selector(False)
    mask2 = jnp.concatenate([s_i <= t_i, s_i >= t_i], axis=0)
    r2 = lax.broadcasted_iota(jnp.int32, (2 * c_sz, 2 * c_sz), 0)
    c2 = lax.broadcasted_iota(jnp.int32, (2 * c_sz, 2 * c_sz), 1)
    tri2 = jnp.logical_or(jnp.logical_and(r2 < c_sz, c2 <= r2),
                          jnp.logical_and(r2 >= c_sz, c2 >= r2)).astype(BF16)

    def per_half(fn, x):
        return jnp.concatenate([jnp.broadcast_to(fn(x[0:c_sz]), (c_sz, w)),
                                jnp.broadcast_to(fn(x[c_sz:2 * c_sz]), (c_sz, w))], axis=0)

    def rows_of(row_f, row_b):
        return jnp.concatenate([jnp.broadcast_to(row_f, (c_sz, w)), jnp.broadcast_to(row_b, (c_sz, w))], axis=0)

    def load(c_f, c_b, lo, hi):
        rf = pl.ds(pl.multiple_of(c_f * c_sz, c_sz), c_sz)
        rb = pl.ds(pl.multiple_of(c_b * c_sz, c_sz), c_sz)
        return pb_ref[0, rf, lo:hi], pb_ref[0, rb, lo:hi]

    def scan_step(n, carry):
        m_f, m_b = carry
        c_f = n
        c_b = jnp.where(n < n_ctx, n_ctx - 1 - n, n_all - 1 - (n - n_ctx))
        q_f, q_b = load(c_f, c_b, q_off, q_off + qk_w)
        k_f, k_b = load(c_f, c_b, k_off, k_off + qk_w)
        v_f, v_b = load(c_f, c_b, v_off, v_off + B_W)
        g_f, g_b = load(c_f, c_b, g_off, g_off + LANES)
        gts = jnp.concatenate([g_f + bias_ref[...],
                               pltpu.roll(g_b + bias_ref[...], LANES - 2 * HEADS, 1)], axis=0)
        log_f = jnp.minimum(gts, 0.0) - jnp.log(1.0 + jnp.exp(-jnp.abs(gts)))
        cum_col = sum(jnp.dot(tri2, p, preferred_element_type=F32) for p in _split3(log_f))
        both = sum(jnp.dot(p, sel, preferred_element_type=F32)
                   for p in _split3(jnp.concatenate([cum_col, gts], axis=1)))
        cum_t = both[:, :w]
        ig_t = both[:, w:]
        src = per_half(lambda x: jnp.sum(jnp.where(diag4, x, 0.0), axis=0, keepdims=True), cum_t - ig_t)
        m_prev = rows_of(m_f, m_b)
        log_d = jnp.where(mask2, cum_t - src, MASK_NEG)
        log_inter = cum_t + m_prev
        m_t = jnp.maximum(log_inter, head_max(log_d))
        w_intra = jnp.where(mask2, jnp.exp(log_d - m_t), 0.0)
        w_inter = jnp.exp(log_inter - m_t)

        scale = B_QK ** -0.5
        qb_f, qb_b = (q_f * scale).astype(BF16), (q_b * scale).astype(BF16)
        kb_f, kb_b = k_f.astype(BF16), k_b.astype(BF16)
        zero = jnp.zeros((), BF16)

        def scores(qb, kb):
            return _nt(qb, jnp.where(k_block, jnp.concatenate([kb] * HEADS, axis=0), zero))

        p = (jnp.concatenate([scores(qb_f, kb_f), scores(qb_b, kb_b)], axis=0) * w_intra).astype(BF16)

        def values(p_half, v_half):
            v_bd = jnp.where(v_block, jnp.concatenate([v_half.astype(BF16)] * HEADS, axis=0), zero)
            return jnp.dot(p_half, v_bd, preferred_element_type=F32)

        cn_f = cn_fwd_ref[...]
        cn_b = cn_bwd_ref[...]
        inter = jnp.concatenate([jnp.dot(qb_f, cn_f.astype(BF16), preferred_element_type=F32),
                                 jnp.dot(qb_b, cn_b.astype(BF16), preferred_element_type=F32)], axis=0)
        num = w_inter * inter[:, :w] + jnp.concatenate(
            [values(p[0:c_sz], v_f), values(p[c_sz:2 * c_sz], v_b)], axis=0)
        den = w_inter * inter[:, w:] + jnp.dot(p, head_ones, preferred_element_type=F32)
        h_out = num / jnp.maximum(jnp.abs(den), jnp.exp(-m_t))
        o_ref[0, pl.ds(pl.multiple_of(c_f * c_sz, c_sz), c_sz), :] = h_out[0:c_sz]
        ob_ref[pl.ds(pl.multiple_of(c_b * c_sz, c_sz), c_sz), :] = h_out[c_sz:2 * c_sz]

        end_f = cum_t[c_sz - 1:c_sz, :]
        end_b = cum_t[c_sz:c_sz + 1, :]
        log_end = rows_of(end_f, end_b) - cum_t + ig_t
        m_end_f = jnp.maximum(end_f + m_f, jnp.max(log_end[0:c_sz], axis=0, keepdims=True))
        m_end_b = jnp.maximum(end_b + m_b, jnp.max(log_end[c_sz:2 * c_sz], axis=0, keepdims=True))
        w_end = jnp.exp(log_end - rows_of(m_end_f, m_end_b))

        def update(cn_ref, cn, kb, w_half, v_half, carry_w):
            upd = _tn(kb, jnp.concatenate([(w_half * v_half).astype(BF16), w_half.astype(BF16)], axis=1))
            cn_ref[...] = jnp.concatenate([carry_w, carry_w], axis=1) * cn + jnp.where(state_block, upd, 0.0)

        update(cn_fwd_ref, cn_f, kb_f, w_end[0:c_sz], v_f, jnp.exp(end_f + m_f - m_end_f))
        update(cn_bwd_ref, cn_b, kb_b, w_end[c_sz:2 * c_sz], v_b, jnp.exp(end_b + m_b - m_end_b))
        return m_end_f, m_end_b

    zero_m = jnp.zeros((1, w), F32)
    cn_fwd_ref[...] = jnp.zeros_like(cn_fwd_ref)
    cn_bwd_ref[...] = jnp.zeros_like(cn_bwd_ref)
    lax.fori_loop(0, n_all, scan_step, (zero_m, zero_m), unroll=2)

    def readout(c, carry):
        rows = pl.ds(pl.multiple_of(c * ROW_TILE, ROW_TILE), ROW_TILE)
        tot = o_ref[0, rows, :] + ob_ref[rows, :]
        normed = jnp.concatenate(
            [_rms(tot[:, h * HEAD_V:(h + 1) * HEAD_V]) for h in range(HEADS)], axis=1) * gain_ref[...]
        og = pb_ref[0, rows, o_off:o_off + B_W]
        o_ref[0, rows, :] = normed * jax.nn.sigmoid(og)
        return carry

    lax.fori_loop(0, n_all * c_sz // ROW_TILE, readout, 0)


def _mlstm(pb3, bias, gain, n_ctx_rows):
    n_batch, rows, _ = pb3.shape
    kern = functools.partial(_mlstm_kernel, n_ctx=n_ctx_rows // CHUNK, n_all=rows // CHUNK)
    return pl.pallas_call(
        kern,
        out_shape=jax.ShapeDtypeStruct((n_batch, rows, B_W), F32),
        grid=(n_batch,),
        in_specs=[pl.BlockSpec((1, rows, PB_W), lambda b: (b, 0, 0)),
                  pl.BlockSpec((1, LANES), lambda b: (0, 0)),
                  pl.BlockSpec((1, B_W), lambda b: (0, 0))],
        out_specs=pl.BlockSpec((1, rows, B_W), lambda b: (b, 0, 0)),
        scratch_shapes=[pltpu.VMEM((HEADS * B_QK, 2 * B_W), F32), pltpu.VMEM((HEADS * B_QK, 2 * B_W), F32),
                        pltpu.VMEM((rows, B_W), F32)],
        compiler_params=_cparams(("arbitrary",)),
        name="mlstm",
    )(pb3, bias, gain)


def _attn_kernel(lam_ref, qp_ref, qr_ref, k_ref, v_ref, gain_ref, o_ref, *, n_ctx, q_tile0, lam_init):
    lam = lam_ref[0]
    q_tile = pl.program_id(2) + q_tile0
    lane = lax.broadcasted_iota(jnp.int32, (1, 2 * C_DQK), 1)
    scale = C_DQK ** -0.5

    def finish(parts):
        o = parts[0] - lam * parts[1]
        o_ref[0] = _rms(o) * gain_ref[...] * (1.0 - lam_init)

    def sub_query(q, j):
        return jnp.where(lane // C_DQK == j, q * scale, jnp.zeros_like(q))

    def row_max(s):
        return jnp.max(s, axis=-1, keepdims=True)

    def row_sum(s):
        return jnp.sum(s, axis=-1, keepdims=True)

    def pv(ex, v):
        return jnp.dot(ex.astype(BF16), v, preferred_element_type=F32)

    @pl.when(q_tile * ROW_TILE < n_ctx)
    def _():
        kc = k_ref[0, 0:n_ctx, :]
        vc = v_ref[0, 0:n_ctx, :]
        parts = []
        for j in range(2):
            s = _nt(sub_query(qp_ref[0], j), kc)
            ex = jnp.exp(s - row_max(s))
            parts.append(pv(ex, vc) / row_sum(ex))
        finish(parts)

    @pl.when(q_tile * ROW_TILE >= n_ctx)
    def _():
        n_all = k_ref.shape[1]
        kc = k_ref[0, 0:n_ctx, :]
        kl = k_ref[0, n_ctx:n_all, :]
        vc = v_ref[0, 0:n_ctx, :]
        vl = v_ref[0, n_ctx:n_all, :]
        parts = []
        for j in range(2):
            s_c = _nt(sub_query(qp_ref[0], j), kc)
            s_l = _nt(sub_query(qr_ref[0], j), kl)
            m = jnp.maximum(row_max(s_c), row_max(s_l))
            e_c = jnp.exp(s_c - m)
            e_l = jnp.exp(s_l - m)
            parts.append((pv(e_c, vc) + pv(e_l, vl)) / (row_sum(e_c) + row_sum(e_l)))
        finish(parts)


def _attention(pc3, lam, gain, n_ctx_rows, lam_init, with_ctx):
    n_batch, rows, _ = pc3.shape
    q_tile0 = 0 if with_ctx else n_ctx_rows // ROW_TILE
    n_q = rows // ROW_TILE - q_tile0
    hb = C_W // LANES
    kern = functools.partial(_attn_kernel, n_ctx=n_ctx_rows, q_tile0=q_tile0, lam_init=lam_init)
    grid_spec = pltpu.PrefetchScalarGridSpec(
        num_scalar_prefetch=1,
        grid=(n_batch, HEADS, n_q),
        in_specs=[pl.BlockSpec((1, ROW_TILE, LANES), lambda b, h, i, lam: (b, i + q_tile0, h)),
                  pl.BlockSpec((1, ROW_TILE, LANES), lambda b, h, i, lam: (b, i + q_tile0, hb + h)),
                  pl.BlockSpec((1, rows, LANES), lambda b, h, i, lam: (b, 0, 2 * hb + h)),
                  pl.BlockSpec((1, rows, LANES), lambda b, h, i, lam: (b, 0, 3 * hb + h)),
                  pl.BlockSpec((1, LANES), lambda b, h, i, lam: (0, 0))],
        out_specs=pl.BlockSpec((1, ROW_TILE, LANES), lambda b, h, i, lam: (b, i, h)),
    )
    return pl.pallas_call(
        kern,
        out_shape=jax.ShapeDtypeStruct((n_batch, n_q * ROW_TILE, C_W), F32),
        grid_spec=grid_spec,
        compiler_params=_cparams(("arbitrary", "arbitrary", "arbitrary")),
        name="diff_attn",
    )(lam, pc3, pc3, pc3, pc3, gain)


def _outproj_kernel(oa_ref, ob_ref, oc_ref, x_ref, mod_ref, n_ref, w_ref, rw_ref, rb_ref,
                    xn_ref, h_ref, route_ref, cnt_ref, carry_ref):
    d = x_ref.shape[1]
    n_exp = rw_ref.shape[1]

    @pl.when(pl.program_id(0) == 0)
    def _():
        carry_ref[...] = jnp.zeros_like(carry_ref)

    mix = (jnp.dot(oa_ref[...].astype(BF16), w_ref[0:A_W, :], preferred_element_type=F32)
           + jnp.dot(ob_ref[...].astype(BF16), w_ref[A_W:A_W + B_W, :], preferred_element_type=F32)
           + jnp.dot(oc_ref[...].astype(BF16), w_ref[A_W + B_W:A_W + B_W + C_W, :], preferred_element_type=F32))
    xn = x_ref[...] + mod_ref[0, :, 2 * d:3 * d] * (_rms(mix) * n_ref[1:2, :])
    xn_ref[...] = xn
    h = _rms(xn) * n_ref[2:3, :] * (1.0 + mod_ref[0, :, 4 * d:5 * d]) + mod_ref[0, :, 3 * d:4 * d]
    h_ref[...] = _pack_bf16_pairs(h)

    h_hi = h.astype(BF16)
    h_mid = (h - h_hi.astype(F32)).astype(BF16)
    logits = _nt(rw_ref[0], h_hi) + _nt(rw_ref[1], h_hi) + _nt(rw_ref[0], h_mid) + rb_ref[...]
    e_sub = lax.broadcasted_iota(jnp.int32, logits.shape, 0)
    cur = logits
    picks, vals = [], []
    for _ in range(TOP_K):
        mx = jnp.max(cur, axis=0, keepdims=True)
        idx = jnp.min(jnp.where(cur == mx, e_sub, n_exp), axis=0, keepdims=True)
        hit = e_sub == idx
        cur = jnp.where(hit, -jnp.inf, cur)
        picks.append((idx, hit.astype(F32)))
        vals.append(mx)
    exps = [jnp.exp(vv - vals[0]) for vv in vals]
    total = exps[0] + exps[1] + exps[2] + exps[3]

    chosen = picks[0][1] + picks[1][1] + picks[2][1] + picks[3][1]
    tm = logits.shape[1]
    before = (lax.broadcasted_iota(jnp.int32, (tm, tm), 0) < lax.broadcasted_iota(jnp.int32, (tm, tm), 1))
    seen = jnp.dot(chosen.astype(BF16), before.astype(BF16), preferred_element_type=F32) + carry_ref[...]
    new_carry = carry_ref[...] + jnp.sum(chosen, axis=1, keepdims=True)
    carry_ref[...] = new_carry
    cnt_ref[...] = new_carry

    o_row = lax.broadcasted_iota(jnp.int32, route_ref.shape, 0)
    route = jnp.zeros(route_ref.shape, F32)
    for kk in range(TOP_K):
        idx, hit = picks[kk]
        rank = jnp.sum(hit * seen, axis=0, keepdims=True)
        route = (route + jnp.where(o_row == kk, idx.astype(F32), 0.0)
                 + jnp.where(o_row == TOP_K + kk, exps[kk] / total, 0.0)
                 + jnp.where(o_row == 2 * TOP_K + kk, rank, 0.0))
    route_ref[...] = route


ROUTE_ROWS = 16


def _outproj(oa, ob, oc, xy, mod, norms, w_out_b, router_wt, router_b, layer, n_tiles, in_tile, oc_tile, mod_row):
    d = xy.shape[1]
    n_exp = router_wt.shape[2]
    t_out = n_tiles * ROW_TILE
    xn, h, route_t, counts = pl.pallas_call(
        _outproj_kernel,
        out_shape=(jax.ShapeDtypeStruct((t_out, d), F32),
                   jax.ShapeDtypeStruct((t_out, d // 2), jnp.int32),
                   jax.ShapeDtypeStruct((n_tiles * ROUTE_ROWS, ROW_TILE), F32),
                   jax.ShapeDtypeStruct((n_exp, 1), F32)),
        grid=(n_tiles,),
        in_specs=[pl.BlockSpec((ROW_TILE, A_W), lambda i: (in_tile(i), 0)),
                  pl.BlockSpec((ROW_TILE, B_W), lambda i: (in_tile(i), 0)),
                  pl.BlockSpec((ROW_TILE, C_W), lambda i: (oc_tile(i), 0)),
                  pl.BlockSpec((ROW_TILE, d), lambda i: (in_tile(i), 0)),
                  pl.BlockSpec((1, 1, 6 * d), lambda i: (mod_row(i), 0, 0)),
                  pl.BlockSpec((None, 4, d), lambda i: (layer, 0, 0)),
                  pl.BlockSpec((None, d, d), lambda i: (layer, 0, 0)),
                  pl.BlockSpec((None, 2, n_exp, d), lambda i: (layer, 0, 0, 0)),
                  pl.BlockSpec((None, n_exp, 1), lambda i: (layer, 0, 0))],
        out_specs=(pl.BlockSpec((ROW_TILE, d), lambda i: (i, 0)),
                   pl.BlockSpec((ROW_TILE, d // 2), lambda i: (i, 0)),
                   pl.BlockSpec((ROUTE_ROWS, ROW_TILE), lambda i: (i, 0)),
                   pl.BlockSpec((n_exp, 1), lambda i: (0, 0))),
        scratch_shapes=[pltpu.VMEM((n_exp, 1), F32)],
        compiler_params=_cparams(("arbitrary",)),
        name="outproj_router",
    )(oa, ob, oc, xy, mod, norms, w_out_b, router_wt, router_b)
    return xn, h, route_t.reshape(n_tiles, ROUTE_ROWS, ROW_TILE), counts[:, 0]


PAIR_BLOCK = 2 * LANES


def _moe_kernel(te_ref, first_ref, valid_ref, nu_ref, xs_ref, w1_ref, b1_ref, w2_ref, b2_ref, ys_ref,
                w1p_ref, w2b_ref):
    i = pl.program_id(0)
    two_f = w1_ref.shape[1]
    n_blk = two_f // PAIR_BLOCK

    @pl.when(jnp.logical_and(i < nu_ref[0], first_ref[i] == 1))
    def _():
        r = lax.broadcasted_iota(jnp.int32, (PAIR_BLOCK, PAIR_BLOCK), 0)
        c = lax.broadcasted_iota(jnp.int32, (PAIR_BLOCK, PAIR_BLOCK), 1)
        perm = (r == jnp.where(c < LANES, 2 * c, 2 * (c - LANES) + 1)).astype(BF16)
        for blk in range(n_blk):
            cols = slice(blk * PAIR_BLOCK, (blk + 1) * PAIR_BLOCK)
            w1p_ref[:, cols] = jnp.dot(w1_ref[:, cols].astype(BF16), perm,
                                       preferred_element_type=F32).astype(BF16)
        w2b_ref[...] = w2_ref[...].astype(BF16)

    tile_rows = xs_ref.shape[0]
    used = i < nu_ref[0]
    valid = valid_ref[i]

    def expert_ffn(n_rows):
        x = _unpack_bf16_pairs(xs_ref[0:n_rows, :]).astype(BF16)
        row = lax.broadcasted_iota(jnp.int32, (n_rows, 1), 0)
        x = jnp.where(row < valid, x, jnp.zeros_like(x))
        hid = jnp.dot(x, w1p_ref[...], preferred_element_type=F32) + b1_ref[...]
        acts = []
        for blk in range(n_blk):
            glu = jnp.minimum(hid[:, blk * PAIR_BLOCK:blk * PAIR_BLOCK + LANES], SWIGLU_LIMIT)
            lin = jnp.clip(hid[:, blk * PAIR_BLOCK + LANES:(blk + 1) * PAIR_BLOCK], -SWIGLU_LIMIT, SWIGLU_LIMIT)
            acts.append((glu * jax.nn.sigmoid(SWIGLU_ALPHA * glu) * (lin + 1.0)).astype(BF16))
        y = jnp.dot(jnp.concatenate(acts, axis=1), w2b_ref[...], preferred_element_type=F32) + b2_ref[...]
        ys_ref[0:n_rows, :] = _pack_bf16_pairs(y)
        if n_rows < tile_rows:
            ys_ref[n_rows:tile_rows, :] = jnp.zeros((tile_rows - n_rows, ys_ref.shape[1]), ys_ref.dtype)

    @pl.when(jnp.logical_and(used, valid > tile_rows // 2))
    def _():
        expert_ffn(tile_rows)

    @pl.when(jnp.logical_and(used, valid <= tile_rows // 2))
    def _():
        expert_ffn(tile_rows // 2)

    @pl.when(jnp.logical_not(used))
    def _():
        ys_ref[...] = jnp.zeros_like(ys_ref)


def _moe(tile_expert, tile_first, tile_valid, n_used, xs, w1, b1p, w2, b2, layer):
    r_max, half_d = xs.shape
    d = 2 * half_d
    two_f = w1.shape[3]
    f = two_f // 2
    n_tiles = r_max // MOE_TILE

    def row_tile(i, te, fi, va, nu):
        return (jnp.maximum(jnp.minimum(i, nu[0] - 1), 0), 0)

    grid_spec = pltpu.PrefetchScalarGridSpec(
        num_scalar_prefetch=4,
        grid=(n_tiles,),
        in_specs=[pl.BlockSpec((MOE_TILE, half_d), row_tile),
                  pl.BlockSpec((None, None, d, two_f), lambda i, te, fi, va, nu: (layer, te[i], 0, 0)),
                  pl.BlockSpec((None, None, 1, two_f), lambda i, te, fi, va, nu: (layer, te[i], 0, 0)),
                  pl.BlockSpec((None, None, f, d), lambda i, te, fi, va, nu: (layer, te[i], 0, 0)),
                  pl.BlockSpec((None, None, 1, d), lambda i, te, fi, va, nu: (layer, te[i], 0, 0))],
        out_specs=pl.BlockSpec((MOE_TILE, half_d), lambda i, te, fi, va, nu: (i, 0)),
        scratch_shapes=[pltpu.VMEM((d, two_f), BF16), pltpu.VMEM((f, d), BF16)],
    )
    return pl.pallas_call(
        _moe_kernel,
        out_shape=jax.ShapeDtypeStruct((r_max, half_d), jnp.int32),
        grid_spec=grid_spec,
        compiler_params=_cparams(("arbitrary",)),
        name="moe_experts",
    )(tile_expert, tile_first, tile_valid, n_used, xs, w1, b1p, w2, b2)


SC_CORES = 2
SC_SUBCORES = 16
SC_CHUNK = 64


def _dispatch_rows(hp, pos, r_max):
    t, width = hp.shape
    workers = SC_CORES * SC_SUBCORES
    assert t % (workers * SC_CHUNK) == 0
    per_worker = t // (workers * SC_CHUNK)
    idx = pos.reshape(-1, TOP_K, ROW_TILE // SC_CHUNK, SC_CHUNK).transpose(0, 2, 1, 3).reshape(
        workers, per_worker * TOP_K, SC_CHUNK)
    mesh = plsc.VectorSubcoreMesh(core_axis_name="c", subcore_axis_name="s",
                                  num_cores=SC_CORES, num_subcores=SC_SUBCORES)

    @functools.partial(
        pl.kernel, mesh=mesh,
        out_type=jax.ShapeDtypeStruct((r_max, width), hp.dtype),
        scratch_types=[pltpu.VMEM((per_worker * TOP_K, SC_CHUNK), jnp.int32),
                       pltpu.VMEM((2, SC_CHUNK, width), hp.dtype),
                       pltpu.SemaphoreType.DMA((2,)),
                       pltpu.SemaphoreType.DMA((2,))],
    )
    def scatter(hp_hbm, idx_hbm, out_hbm, idx_v, rows_v, read_sem, write_sem):
        wid = lax.axis_index("s") * SC_CORES + lax.axis_index("c")
        pltpu.sync_copy(idx_hbm.at[wid], idx_v)

        def read(j):
            row0 = pl.multiple_of((wid * per_worker + j) * SC_CHUNK, SC_CHUNK)
            return pltpu.make_async_copy(hp_hbm.at[pl.ds(row0, SC_CHUNK)], rows_v.at[j % 2], read_sem.at[j % 2])

        def write(j, k):
            return pltpu.make_async_copy(rows_v.at[j % 2], out_hbm.at[idx_v.at[j * TOP_K + k]],
                                         write_sem.at[j % 2])

        read(0).start()
        for j in range(per_worker):
            read(j).wait()
            for k in range(TOP_K):
                write(j, k).start()
            if j + 1 < per_worker:
                if j >= 1:
                    for k in range(TOP_K):
                        write(j - 1, k).wait()
                read(j + 1).start()
        for j in range(max(per_worker - 2, 0), per_worker):
            for k in range(TOP_K):
                write(j, k).wait()

    return scatter(hp, idx)


def _combine_rows(ysp, pos):
    width = ysp.shape[1]
    t = pos.shape[0] * pos.shape[2]
    workers = SC_CORES * SC_SUBCORES
    assert t % (workers * SC_CHUNK) == 0
    per_worker = t // (workers * SC_CHUNK)
    units = per_worker * TOP_K
    idx = pos.reshape(-1, TOP_K, ROW_TILE // SC_CHUNK, SC_CHUNK).transpose(0, 2, 1, 3).reshape(
        workers, units, SC_CHUNK)
    mesh = plsc.VectorSubcoreMesh(core_axis_name="c", subcore_axis_name="s",
                                  num_cores=SC_CORES, num_subcores=SC_SUBCORES)

    @functools.partial(
        pl.kernel, mesh=mesh,
        out_type=jax.ShapeDtypeStruct((TOP_K, t, width), ysp.dtype),
        scratch_types=[pltpu.VMEM((units, SC_CHUNK), jnp.int32),
                       pltpu.VMEM((2, SC_CHUNK, width), ysp.dtype),
                       pltpu.SemaphoreType.DMA((2,)),
                       pltpu.SemaphoreType.DMA((2,))],
    )
    def gather(ys_hbm, idx_hbm, out_hbm, idx_v, rows_v, read_sem, write_sem):
        wid = lax.axis_index("s") * SC_CORES + lax.axis_index("c")
        pltpu.sync_copy(idx_hbm.at[wid], idx_v)

        def read(u):
            return pltpu.make_async_copy(ys_hbm.at[idx_v.at[u]], rows_v.at[u % 2], read_sem.at[u % 2])

        def write(u):
            row0 = pl.multiple_of((wid * per_worker + u // TOP_K) * SC_CHUNK, SC_CHUNK)
            return pltpu.make_async_copy(rows_v.at[u % 2], out_hbm.at[u % TOP_K, pl.ds(row0, SC_CHUNK)],
                                         write_sem.at[u % 2])

        read(0).start()
        for u in range(units):
            read(u).wait()
            write(u).start()
            if u + 1 < units:
                if u >= 1:
                    write(u - 1).wait()
                read(u + 1).start()
        for u in range(max(units - 2, 0), units):
            write(u).wait()

    return gather(ysp, idx)


def _ffn_residual_kernel(x_ref, w_ref, y0_ref, y1_ref, y2_ref, y3_ref, mod_ref, n_ref, o_ref):
    d = x_ref.shape[1]
    wts = w_ref[...]
    ffn = None
    for k, y_ref in enumerate((y0_ref, y1_ref, y2_ref, y3_ref)):
        term = wts[:, k:k + 1] * _unpack_bf16_pairs(y_ref[...])
        ffn = term if ffn is None else ffn + term
    o_ref[...] = x_ref[...] + mod_ref[0, :, 5 * d:6 * d] * (_rms(ffn) * n_ref[3:4, :])


def _ffn_residual(xn, wts, ys_by_k, mod, norms, layer, mod_row):
    t, d = xn.shape
    row_spec = pl.BlockSpec((ROW_TILE, d), lambda i: (i, 0))

    def y_spec(k):
        return pl.BlockSpec((None, ROW_TILE, d // 2), lambda i: (k, i, 0))

    return pl.pallas_call(
        _ffn_residual_kernel,
        out_shape=jax.ShapeDtypeStruct((t, d), F32),
        grid=(t // ROW_TILE,),
        in_specs=[row_spec,
                  pl.BlockSpec((ROW_TILE, TOP_K), lambda i: (i, 0)),
                  y_spec(0), y_spec(1), y_spec(2), y_spec(3),
                  pl.BlockSpec((1, 1, 6 * d), lambda i: (mod_row(i), 0, 0)),
                  pl.BlockSpec((None, 4, d), lambda i: (layer, 0, 0))],
        out_specs=row_spec,
        compiler_params=_cparams(("arbitrary",)),
        name="ffn_residual",
    )(xn, wts, ys_by_k, ys_by_k, ys_by_k, ys_by_k, mod, norms)


def _dispatch_plan(route_t, counts, r_max):
    n_row_tiles = route_t.shape[0]
    idx = route_t[:, 0:TOP_K, :].astype(jnp.int32)
    wts = route_t[:, TOP_K:2 * TOP_K, :].transpose(0, 2, 1).reshape(n_row_tiles * ROW_TILE, TOP_K)
    rank = route_t[:, 2 * TOP_K:3 * TOP_K, :].astype(jnp.int32)
    cnt = counts.astype(jnp.int32)
    padded = (cnt + MOE_TILE - 1) // MOE_TILE * MOE_TILE
    ends = jnp.cumsum(padded)
    starts = ends - padded
    pos = rank
    for e in range(cnt.shape[0]):
        pos = pos + jnp.where(idx == e, starts[e], 0)
    n_tiles = r_max // MOE_TILE
    n_used = ends[-1] // MOE_TILE
    tile_ids = jnp.minimum(jnp.arange(n_tiles, dtype=jnp.int32), n_used - 1)
    tile_expert = jnp.sum((ends // MOE_TILE)[None, :] <= tile_ids[:, None], axis=1).astype(jnp.int32)
    tile_first = jnp.concatenate(
        [jnp.ones((1,), jnp.int32), (tile_expert[1:] != tile_expert[:-1]).astype(jnp.int32)])
    of_expert = (tile_expert[:, None] == jnp.arange(cnt.shape[0], dtype=jnp.int32)[None, :]).astype(jnp.int32)
    first_tile = jnp.sum(of_expert * ((ends - padded) // MOE_TILE)[None, :], axis=1)
    tile_valid = jnp.clip(jnp.sum(of_expert * cnt[None, :], axis=1) - (tile_ids - first_tile) * MOE_TILE,
                          0, MOE_TILE).astype(jnp.int32)
    return pos, wts, tile_expert, tile_first, tile_valid, n_used.reshape(1).astype(jnp.int32)


def _rope_tables(length, n_ctx_rows):
    rows = length // GRID_W
    row = jnp.repeat(jnp.arange(rows, dtype=F32), GRID_W)
    col = jnp.tile(jnp.arange(GRID_W, dtype=F32), rows)
    n_freq = C_DQK // 4
    inv_freq = ROPE_THETA ** (-jnp.arange(n_freq, dtype=F32) / n_freq)
    ang_r = row[:, None] * inv_freq
    ang_c = col[:, None] * inv_freq
    cos = jnp.concatenate([jnp.cos(ang_r), jnp.cos(ang_r), jnp.cos(ang_c), jnp.cos(ang_c)], axis=-1)
    sin = jnp.concatenate([-jnp.sin(ang_r), jnp.sin(ang_r), -jnp.sin(ang_c), jnp.sin(ang_c)], axis=-1)
    reps = C_W // C_DQK
    cos = jnp.concatenate([jnp.ones((n_ctx_rows, C_W), F32), jnp.tile(cos, (1, reps))], axis=0)
    sin = jnp.concatenate([jnp.zeros((n_ctx_rows, C_W), F32), jnp.tile(sin, (1, reps))], axis=0)
    return cos, sin


def _hgrn_lower_bound(table, layer):
    p = jax.nn.softmax(table.astype(F32), axis=1)
    cum = jnp.cumsum(p, axis=1) - p[:, :1]
    return jnp.clip(cum[:, layer], 0.0, 1.0)


def kernel(x, c, ctx, c_ctx, ada_w, ada_b, sandwich_norms, w_in, w_out, hgrn_lower_bounds, hgrn_norm,
           mlstm_gate_bias, mlstm_norm, diff_lambdas, diff_norm, router_w, router_b, moe_w1, moe_b1,
           moe_w2, moe_b2):
    n_batch, seq, d = x.shape
    n_ctx_rows = ctx.shape[1]
    depth = w_in.shape[0]
    n_exp = router_w.shape[2]
    assert seq % ROW_TILE == 0 and n_ctx_rows % ROW_TILE == 0 and seq % GRID_W == 0
    rows_b = n_ctx_rows + seq
    tiles_b = rows_b // ROW_TILE
    ctx_tiles = n_ctx_rows // ROW_TILE
    lat_tiles = seq // ROW_TILE

    b_main = PA_W + 2 * HEADS * B_QK + 2 * B_W
    w_in_p = jnp.concatenate(
        [w_in[:, :, :b_main],
         jnp.pad(w_in[:, :, b_main:b_main + N_GATE], ((0, 0), (0, 0), (0, LANES - N_GATE))),
         w_in[:, :, b_main + N_GATE:]], axis=2).astype(BF16)
    w_out_b = w_out.astype(BF16)
    two_f = moe_b1.shape[2]
    b1p = moe_b1.reshape(depth, n_exp, two_f // PAIR_BLOCK, LANES, 2).transpose(0, 1, 2, 4, 3).reshape(
        depth, n_exp, 1, two_f)
    b2 = moe_b2[:, :, None, :]
    ada_b3 = ada_b[:, None, :]
    router_b3 = router_b[:, :, None]
    rw_t = router_w.transpose(0, 2, 1)
    rw_hi = rw_t.astype(BF16)
    router_wt = jnp.stack([rw_hi, (rw_t - rw_hi.astype(F32)).astype(BF16)], axis=1)
    gate_bias = jnp.pad(mlstm_gate_bias, ((0, 0), (0, LANES - N_GATE)))
    cos, sin = _rope_tables(seq, n_ctx_rows)

    cond_rows = (n_batch + 1 + 7) // 8 * 8
    cvec = jnp.zeros((cond_rows, d), F32).at[:n_batch].set(c).at[n_batch].set(c_ctx)

    def mod_row_all(i):
        return jnp.where(i % tiles_b < ctx_tiles, n_batch, i // tiles_b)

    xy = jnp.concatenate([ctx, x], axis=1).reshape(n_batch * rows_b, d)

    for layer in range(depth):
        last = layer == depth - 1
        mod = _adaln(cvec, ada_w, ada_b3, layer).reshape(cond_rows, 1, 6 * d)
        lb = _hgrn_lower_bound(hgrn_lower_bounds, layer)
        lam_init = 0.8 - 0.6 * math.exp(-0.3 * layer)
        lq1, lk1, lq2, lk2 = diff_lambdas[layer].astype(F32)
        lam = (jnp.exp(jnp.sum(lq1 * lk1)) - jnp.exp(jnp.sum(lq2 * lk2)) + lam_init).reshape(1)

        pa, pb, pc = _inproj(xy, mod, sandwich_norms, w_in_p, cos, sin, layer, tiles_b, ctx_tiles, n_batch)
        oa = _hgrn(pa.reshape(n_batch, rows_b, PA_W), lb, jnp.tile(hgrn_norm[layer], HEADS)[None, :], n_ctx_rows)
        ob = _mlstm(pb.reshape(n_batch, rows_b, PB_W), gate_bias[layer][None, :], mlstm_norm[layer][None, :],
                    n_ctx_rows)
        oc = _attention(pc.reshape(n_batch, rows_b, PC_OUT), lam, diff_norm[layer][None, :], n_ctx_rows,
                        lam_init, not last)

        if last:
            n_tiles = n_batch * lat_tiles
            in_tile = lambda i: (i // lat_tiles) * tiles_b + ctx_tiles + i % lat_tiles
            mod_row = lambda i: i // lat_tiles
        else:
            n_tiles = n_batch * tiles_b
            in_tile = lambda i: i
            mod_row = mod_row_all
        xn, h, route, counts = _outproj(
            oa.reshape(-1, A_W), ob.reshape(-1, B_W), oc.reshape(-1, C_W), xy, mod, sandwich_norms, w_out_b,
            router_wt, router_b3, layer, n_tiles, in_tile, lambda i: i, mod_row)

        r_max = n_tiles * ROW_TILE * TOP_K + n_exp * MOE_TILE
        pos, wts, tile_expert, tile_first, tile_valid, n_used = _dispatch_plan(route, counts, r_max)
        xs = _dispatch_rows(h, pos, r_max)
        ys = _moe(tile_expert, tile_first, tile_valid, n_used, xs, moe_w1, b1p, moe_w2, b2, layer)
        xy = _ffn_residual(xn, wts, _combine_rows(ys, pos), mod, sandwich_norms, layer, mod_row)

    return xy.reshape(n_batch, seq, d)
```

```python
import functools
import math

import jax
import jax.numpy as jnp
from jax import lax
from jax.experimental import pallas as pl
from jax.experimental.pallas import tpu as pltpu
from jax.experimental.pallas import tpu_sc as plsc

F32 = jnp.float32
BF16 = jnp.bfloat16
HI = lax.Precision.HIGHEST

HEADS = 4
A_W = 256
B_QK = 32
B_W = 256
C_DQK = 64
C_DV = 2 * C_DQK
ATTN_KEY_SCALE = C_DQK ** -0.5 * math.log2(math.e)
C_W = 512
HEAD_V = 64
N_GATE = 16
GRID_W = 64
TOP_K = 4
SWIGLU_ALPHA = 1.702
SWIGLU_LIMIT = 7.0
ROPE_THETA = 10000.0
NORM_EPS = 1e-6
MASK_NEG = -1e30
F_MIN = 1e-12

LANES = 128
ROW_TILE = 256
CHUNK = 64
MOE_TILE = 512
VMEM_LIMIT = 56 * 1024 * 1024

PA_W = 5 * A_W
PB_W = 2 * HEADS * B_QK + 2 * B_W + LANES
PC_IN = 3 * C_W
PC_OUT = 4 * C_W
W_IN_PAD = PA_W + PB_W + PC_IN


def _cparams(sem):
    return pltpu.CompilerParams(dimension_semantics=sem, vmem_limit_bytes=VMEM_LIMIT)


def _nt(a, b):
    return lax.dot_general(a, b, (((1,), (1,)), ((), ())), preferred_element_type=F32)


def _tn(a, b, precision=None):
    return lax.dot_general(a, b, (((0,), (0,)), ((), ())), preferred_element_type=F32, precision=precision)


def _rms(x):
    return x * lax.rsqrt(jnp.mean(x * x, axis=-1, keepdims=True) + NORM_EPS)


def _silu(x):
    return x * jax.nn.sigmoid(x)


def _pack_bf16_pairs(x):
    half = x.shape[1] // 2
    bits = pltpu.bitcast(x.astype(BF16).astype(F32), jnp.uint32)
    return pltpu.bitcast((bits[:, :half] >> 16) | (bits[:, half:] & jnp.uint32(0xFFFF0000)), jnp.int32)


def _unpack_bf16_pairs(words):
    bits = pltpu.bitcast(words, jnp.uint32)
    return jnp.concatenate([pltpu.bitcast(bits << 16, F32),
                            pltpu.bitcast(bits & jnp.uint32(0xFFFF0000), F32)], axis=1)


def _adaln_kernel(c_ref, w_ref, b_ref, o_ref):
    cond = _silu(c_ref[...])
    o_ref[...] = jnp.dot(cond, w_ref[...], preferred_element_type=F32, precision=HI) + b_ref[...]


def _adaln(cvec, ada_w, ada_b, layer):
    rows, d = cvec.shape
    return pl.pallas_call(
        _adaln_kernel,
        out_shape=jax.ShapeDtypeStruct((rows, 6 * d), F32),
        grid=(6,),
        in_specs=[pl.BlockSpec((rows, d), lambda j: (0, 0)),
                  pl.BlockSpec((None, d, d), lambda j: (layer, 0, j)),
                  pl.BlockSpec((None, 1, d), lambda j: (layer, 0, j))],
        out_specs=pl.BlockSpec((rows, d), lambda j: (0, j)),
        compiler_params=_cparams(("arbitrary",)),
        name="adaln",
    )(cvec, ada_w, ada_b)


def _inproj_kernel(x_ref, mod_ref, g_ref, w_ref, cos_ref, sin_ref, pa_ref, pb_ref, pc_ref):
    d = x_ref.shape[1]
    shift = mod_ref[0, :, 0:d]
    scale = mod_ref[0, :, d:2 * d]
    h = _rms(x_ref[...]) * g_ref[0:1, :] * (1.0 + scale) + shift
    hb = h.astype(BF16)
    pa_ref[...] = jnp.dot(hb, w_ref[:, 0:PA_W], preferred_element_type=F32)
    pb_ref[...] = jnp.dot(hb, w_ref[:, PA_W:PA_W + PB_W], preferred_element_type=F32)
    pc = jnp.dot(hb, w_ref[:, PA_W + PB_W:W_IN_PAD], preferred_element_type=F32)
    q = pc[:, 0:C_W]
    k = pc[:, C_W:2 * C_W] * ATTN_KEY_SCALE
    cos = cos_ref[...]
    sin = sin_ref[...]
    lane = lax.broadcasted_iota(jnp.int32, q.shape, 1)
    first = (lane % 32) < 16

    def rope(t):
        partner = jnp.where(first, pltpu.roll(t, C_W - 16, 1), pltpu.roll(t, 16, 1))
        return t * cos + partner * sin

    pc_ref[:, 0:C_W] = q.astype(BF16)
    pc_ref[:, C_W:2 * C_W] = rope(q).astype(BF16)
    pc_ref[:, 2 * C_W:3 * C_W] = rope(k).astype(BF16)
    pc_ref[:, 3 * C_W:4 * C_W] = pc[:, 2 * C_W:3 * C_W].astype(BF16)


def _inproj(xy, mod, norms, w_in_p, cos, sin, layer, tiles_per_batch, n_ctx_tiles, n_batch):
    t_all, d = xy.shape
    n_tiles = t_all // ROW_TILE

    def mod_row(i):
        return jnp.where(i % tiles_per_batch < n_ctx_tiles, n_batch, i // tiles_per_batch)

    return pl.pallas_call(
        _inproj_kernel,
        out_shape=(jax.ShapeDtypeStruct((t_all, PA_W), F32),
                   jax.ShapeDtypeStruct((t_all, PB_W), F32),
                   jax.ShapeDtypeStruct((t_all, PC_OUT), BF16)),
        grid=(n_tiles,),
        in_specs=[pl.BlockSpec((ROW_TILE, d), lambda i: (i, 0)),
                  pl.BlockSpec((1, 1, 6 * d), lambda i: (mod_row(i), 0, 0)),
                  pl.BlockSpec((None, 4, d), lambda i: (layer, 0, 0)),
                  pl.BlockSpec((None, d, W_IN_PAD), lambda i: (layer, 0, 0)),
                  pl.BlockSpec((ROW_TILE, C_W), lambda i: (i % tiles_per_batch, 0)),
                  pl.BlockSpec((ROW_TILE, C_W), lambda i: (i % tiles_per_batch, 0))],
        out_specs=(pl.BlockSpec((ROW_TILE, PA_W), lambda i: (i, 0)),
                   pl.BlockSpec((ROW_TILE, PB_W), lambda i: (i, 0)),
                   pl.BlockSpec((ROW_TILE, PC_OUT), lambda i: (i, 0))),
        compiler_params=_cparams(("arbitrary",)),
        name="inproj",
    )(xy, mod, norms, w_in_p, cos, sin)


def _hgrn_kernel(pa_ref, lb_ref, gain_ref, o_ref, st_ref, ob_ref, *, n_ctx, n_all):
    c_sz, w = CHUNK, A_W
    same_head = (lax.broadcasted_iota(jnp.int32, (w, w), 0) // HEAD_V
                 == lax.broadcasted_iota(jnp.int32, (w, w), 1) // HEAD_V)
    head_ones = same_head.astype(BF16)
    assert c_sz == HEAD_V
    t_i = lax.broadcasted_iota(jnp.int32, (c_sz, w), 0)
    s_i = lax.broadcasted_iota(jnp.int32, (c_sz, w), 1) % c_sz
    t_row = lax.broadcasted_iota(jnp.int32, (c_sz, 1), 0)
    row8 = t_row % 8

    def grouped_rows(a, k):
        return jnp.concatenate(
            [jnp.broadcast_to(a[8 * j + k:8 * j + k + 1, :], (8, w)) for j in range(c_sz // 8)], axis=0)

    def halving_levels(rev):
        out = []
        b = c_sz // 2
        while b >= 1:
            def later(i):
                return ((i % (2 * b)) < b) if rev else ((i % (2 * b)) >= b)
            live = jnp.logical_and(t_i // (2 * b) == s_i // (2 * b),
                                   jnp.logical_and(later(t_i), jnp.logical_not(later(s_i))))
            out.append((b, live.astype(F32), jnp.where(later(t_row), 1.0, -1.0)))
            b //= 2
        return out

    def both(x_f, x_b):
        return jnp.concatenate([x_f, x_b], axis=0)

    def rows_of(row_f, row_b):
        return both(jnp.broadcast_to(row_f, (c_sz, w)), jnp.broadcast_to(row_b, (c_sz, w)))

    self_mask = both(*[(s_i == t_i).astype(F32)] * 2)
    levels = [(b, both(live_f, live_b), both(sign_f, sign_b))
              for (b, live_f, sign_f), (_, live_b, sign_b) in zip(halving_levels(False), halving_levels(True))]
    lb2 = rows_of(lb_ref[0:1, :], lb_ref[1:2, :])
    r2 = lax.broadcasted_iota(jnp.int32, (2 * c_sz, 2 * c_sz), 0)
    c2 = lax.broadcasted_iota(jnp.int32, (2 * c_sz, 2 * c_sz), 1)
    tri2 = jnp.logical_or(jnp.logical_and(r2 < c_sz, c2 <= r2),
                          jnp.logical_and(r2 >= c_sz, c2 >= r2)).astype(BF16)
    zero = jnp.zeros((), BF16)

    def boundary_rows(cum, b, rev):
        if b >= 8:
            return jnp.concatenate(
                [jnp.broadcast_to(cum[r0 + (b if rev else b - 1):r0 + (b if rev else b - 1) + 1, :], (2 * b, w))
                 for r0 in range(0, c_sz, 2 * b)], axis=0)
        ref = None
        for g in reversed(range(8 // (2 * b))):
            cand = grouped_rows(cum, g * 2 * b + (b if rev else b - 1))
            ref = cand if ref is None else jnp.where(row8 < (g + 1) * 2 * b, cand, ref)
        return ref

    def scan_step(n, carry):
        c_f = n
        c_b = jnp.where(n < n_ctx, n_ctx - 1 - n, n_all - 1 - (n - n_ctx))
        rows_f = pl.ds(pl.multiple_of(c_f * c_sz, c_sz), c_sz)
        rows_b = pl.ds(pl.multiple_of(c_b * c_sz, c_sz), c_sz)
        q_pre = both(pa_ref[0, rows_f, 0:A_W], pa_ref[0, rows_b, 0:A_W])
        v_f = pa_ref[0, rows_f, A_W:2 * A_W].astype(BF16)
        v_b = pa_ref[0, rows_b, A_W:2 * A_W].astype(BF16)
        f_pre = both(pa_ref[0, rows_f, 2 * A_W:3 * A_W], pa_ref[0, rows_b, 3 * A_W:4 * A_W])
        q = _silu(q_pre)
        f = lb2 + (1.0 - lb2) * jax.nn.sigmoid(f_pre)
        log_f = jnp.log(jnp.maximum(f, F_MIN))
        kk = (1.0 - lb2) * jax.nn.sigmoid(-f_pre)
        cum = sum(jnp.dot(tri2, piece, preferred_element_type=F32) for piece in _split3(log_f))
        end_f = cum[c_sz - 1:c_sz, :]
        end_b = cum[c_sz:c_sz + 1, :]

        st_f = st_ref[0]
        st_b = st_ref[1]
        q_in = (q * jnp.exp(cum)).astype(BF16)
        o = both(_nt(q_in[0:c_sz], st_f.astype(BF16)), _nt(q_in[c_sz:2 * c_sz], st_b.astype(BF16)))
        k_end = (kk * jnp.exp(rows_of(end_f, end_b) - cum)).astype(BF16)
        st_ref[0] = st_f * jnp.exp(end_f) + jnp.where(same_head, _tn(v_f, k_end[0:c_sz]), 0.0)
        st_ref[1] = st_b * jnp.exp(end_b) + jnp.where(same_head, _tn(v_b, k_end[c_sz:2 * c_sz]), 0.0)

        def scores(qa, ka):
            return _nt(qa, jnp.where(same_head, jnp.concatenate([ka] * HEADS, axis=0), zero))

        p = self_mask * jnp.dot((q * kk).astype(BF16), head_ones, preferred_element_type=F32)
        for b, live, sign in levels:
            ref = both(boundary_rows(cum[0:c_sz], b, False), boundary_rows(cum[c_sz:2 * c_sz], b, True))
            z = jnp.exp((cum - ref) * sign)
            qa = (q * z).astype(BF16)
            ka = (kk * z).astype(BF16)
            p = p + live * both(scores(qa[0:c_sz], ka[0:c_sz]), scores(qa[c_sz:2 * c_sz], ka[c_sz:2 * c_sz]))
        pb = p.astype(BF16)

        def values(p_half, v_half):
            return jnp.dot(p_half, jnp.where(same_head, jnp.concatenate([v_half] * HEADS, axis=0), zero),
                           preferred_element_type=F32)

        o_ref[0, rows_f, :] = o[0:c_sz] + values(pb[0:c_sz], v_f)
        ob_ref[rows_b, :] = o[c_sz:2 * c_sz] + values(pb[c_sz:2 * c_sz], v_b)
        return carry

    st_ref[...] = jnp.zeros_like(st_ref)
    lax.fori_loop(0, n_all, scan_step, 0)

    def readout(c, carry):
        rows = pl.ds(pl.multiple_of(c * ROW_TILE, ROW_TILE), ROW_TILE)
        tot = o_ref[0, rows, :] + ob_ref[rows, :]
        ms = sum(jnp.dot(piece, head_ones, preferred_element_type=F32)
                 for piece in _split3(tot * tot)) * (1.0 / HEAD_V)
        g = pa_ref[0, rows, 4 * A_W:5 * A_W]
        o_ref[0, rows, :] = tot * lax.rsqrt(ms + NORM_EPS) * gain_ref[...] * _silu(g)
        return carry

    lax.fori_loop(0, n_all * c_sz // ROW_TILE, readout, 0)


def _hgrn(pa3, lb, gain, n_ctx_rows):
    n_batch, rows, _ = pa3.shape
    kern = functools.partial(_hgrn_kernel, n_ctx=n_ctx_rows // CHUNK, n_all=rows // CHUNK)
    return pl.pallas_call(
        kern,
        out_shape=jax.ShapeDtypeStruct((n_batch, rows, A_W), F32),
        grid=(n_batch,),
        in_specs=[pl.BlockSpec((1, rows, PA_W), lambda b: (b, 0, 0)),
                  pl.BlockSpec((2, A_W), lambda b: (0, 0)),
                  pl.BlockSpec((1, A_W), lambda b: (0, 0))],
        out_specs=pl.BlockSpec((1, rows, A_W), lambda b: (b, 0, 0)),
        scratch_shapes=[pltpu.VMEM((2, A_W, A_W), F32), pltpu.VMEM((rows, A_W), F32)],
        compiler_params=_cparams(("arbitrary",)),
        name="hgrn2",
    )(pa3, lb, gain)


def _split3(x):
    hi = x.astype(BF16)
    rest = x - hi.astype(F32)
    mid = rest.astype(BF16)
    return hi, mid, (rest - mid.astype(F32)).astype(BF16)


def _mlstm_kernel(pb_ref, bias_ref, gain_ref, o_ref, cn_fwd_ref, cn_bwd_ref, ob_ref, *, n_ctx, n_all):
    assert CHUNK == HEAD_V
    c_sz, w, qk_w = CHUNK, B_W, HEADS * B_QK
    t_i = lax.broadcasted_iota(jnp.int32, (c_sz, w), 0)
    s_i = lax.broadcasted_iota(jnp.int32, (c_sz, w), 1) % c_sz
    diag4 = s_i == t_i
    half_lane = lax.broadcasted_iota(jnp.int32, (1, LANES), 1) < HEAD_V
    k_block = (lax.broadcasted_iota(jnp.int32, (HEADS * c_sz, qk_w), 0) // c_sz
               == lax.broadcasted_iota(jnp.int32, (HEADS * c_sz, qk_w), 1) // B_QK)
    v_block = (lax.broadcasted_iota(jnp.int32, (HEADS * c_sz, w), 0) // c_sz
               == lax.broadcasted_iota(jnp.int32, (HEADS * c_sz, w), 1) // HEAD_V)
    state_block = (lax.broadcasted_iota(jnp.int32, (qk_w, 2 * w), 0) // B_QK
                   == (lax.broadcasted_iota(jnp.int32, (qk_w, 2 * w), 1) % w) // HEAD_V)
    head_ones = v_block.astype(BF16)
    q_off, k_off, v_off, o_off, g_off = 0, qk_w, 2 * qk_w, 2 * qk_w + B_W, 2 * qk_w + 2 * B_W

    sel_r = lax.broadcasted_iota(jnp.int32, (2 * LANES, 2 * w), 0)
    sel_c = lax.broadcasted_iota(jnp.int32, (2 * LANES, 2 * w), 1)
    sel = (sel_r == jnp.where(sel_c < w, HEADS + sel_c // HEAD_V, LANES + (sel_c - w) // HEAD_V)).astype(BF16)

    def head_max(x):
        outs = []
        for col in range(w // LANES):
            xc = x[:, col * LANES:(col + 1) * LANES]
            lo = jnp.max(jnp.where(half_lane, xc, -jnp.inf), axis=-1, keepdims=True)
            hi = jnp.max(jnp.where(half_lane, -jnp.inf, xc), axis=-1, keepdims=True)
            outs.append(jnp.where(half_lane, lo, hi))
        return jnp.concatenate(outs, axis=1)

    mask2 =jnp.concatenate([s_i <= t_i, s_i >= t_i], axis=0)
    r2 = lax.broadcasted_iota(jnp.int32, (2 * c_sz, 2 * c_sz), 0)
    c2 = lax.broadcasted_iota(jnp.int32, (2 * c_sz, 2 * c_sz), 1)
    tri2 = jnp.logical_or(jnp.logical_and(r2 < c_sz, c2 <= r2),
                          jnp.logical_and(r2 >= c_sz, c2 >= r2)).astype(BF16)

    def per_half(fn, x):
        return jnp.concatenate([jnp.broadcast_to(fn(x[0:c_sz]), (c_sz, w)),
                                jnp.broadcast_to(fn(x[c_sz:2 * c_sz]), (c_sz, w))], axis=0)

    def rows_of(row_f, row_b):
        return jnp.concatenate([jnp.broadcast_to(row_f, (c_sz, w)), jnp.broadcast_to(row_b, (c_sz, w))], axis=0)

    def load(c_f, c_b, lo, hi):
        rf = pl.ds(pl.multiple_of(c_f * c_sz, c_sz), c_sz)
        rb = pl.ds(pl.multiple_of(c_b * c_sz, c_sz), c_sz)
        return pb_ref[0, rf, lo:hi], pb_ref[0, rb, lo:hi]

    def scan_step(n, carry):
        m_f, m_b = carry
        c_f = n
        c_b = jnp.where(n < n_ctx, n_ctx - 1 - n, n_all - 1 - (n - n_ctx))
        q_f, q_b = load(c_f, c_b, q_off, q_off + qk_w)
        k_f, k_b = load(c_f, c_b, k_off, k_off + qk_w)
        v_f, v_b = load(c_f, c_b, v_off, v_off + B_W)
        g_f, g_b = load(c_f, c_b, g_off, g_off + LANES)
        gts = jnp.concatenate([g_f + bias_ref[...],
                               pltpu.roll(g_b + bias_ref[...], LANES - 2 * HEADS, 1)], axis=0)
        log_f = jnp.minimum(gts, 0.0) - jnp.log(1.0 + jnp.exp(-jnp.abs(gts)))
        cum_col = sum(jnp.dot(tri2, p, preferred_element_type=F32) for p in _split3(log_f))
        both = sum(jnp.dot(p, sel, preferred_element_type=F32)
                   for p in _split3(jnp.concatenate([cum_col, gts], axis=1)))
        cum_t = both[:, :w]
        ig_t = both[:, w:]
        src = per_half(lambda x: jnp.sum(jnp.where(diag4, x, 0.0), axis=0, keepdims=True), cum_t - ig_t)
        m_prev = rows_of(m_f, m_b)
        log_d = jnp.where(mask2, cum_t - src, MASK_NEG)
        log_inter = cum_t + m_prev
        m_t = jnp.maximum(log_inter, head_max(log_d))
        w_intra = jnp.where(mask2, jnp.exp(log_d - m_t), 0.0)
        w_inter = jnp.exp(log_inter - m_t)

        scale = B_QK ** -0.5
        qb_f, qb_b = (q_f * scale).astype(BF16), (q_b * scale).astype(BF16)
        kb_f, kb_b = k_f.astype(BF16), k_b.astype(BF16)
        zero = jnp.zeros((), BF16)

        def scores(qb, kb):
            return _nt(qb, jnp.where(k_block, jnp.concatenate([kb] * HEADS, axis=0), zero))

        p = (jnp.concatenate([scores(qb_f, kb_f), scores(qb_b, kb_b)], axis=0) * w_intra).astype(BF16)

        def values(p_half, v_half):
            v_bd = jnp.where(v_block, jnp.concatenate([v_half.astype(BF16)] * HEADS, axis=0), zero)
            return jnp.dot(p_half, v_bd, preferred_element_type=F32)

        cn_f = cn_fwd_ref[...]
        cn_b = cn_bwd_ref[...]
        inter = jnp.concatenate([jnp.dot(qb_f, cn_f.astype(BF16), preferred_element_type=F32),
                                 jnp.dot(qb_b, cn_b.astype(BF16), preferred_element_type=F32)], axis=0)
        num = w_inter * inter[:, :w] + jnp.concatenate(
            [values(p[0:c_sz], v_f), values(p[c_sz:2 * c_sz], v_b)], axis=0)
        den = w_inter * inter[:, w:] + jnp.dot(p, head_ones, preferred_element_type=F32)
        h_out = num / jnp.maximum(jnp.abs(den), jnp.exp(-m_t))
        o_ref[0, pl.ds(pl.multiple_of(c_f * c_sz, c_sz), c_sz), :] = h_out[0:c_sz]
        ob_ref[pl.ds(pl.multiple_of(c_b * c_sz, c_sz), c_sz), :] = h_out[c_sz:2 * c_sz]

        end_f = cum_t[c_sz - 1:c_sz, :]
        end_b = cum_t[c_sz:c_sz + 1, :]
        log_end = rows_of(end_f, end_b) - cum_t + ig_t
        m_end_f = jnp.maximum(end_f + m_f, jnp.max(log_end[0:c_sz], axis=0, keepdims=True))
        m_end_b = jnp.maximum(end_b + m_b, jnp.max(log_end[c_sz:2 * c_sz], axis=0, keepdims=True))
        w_end = jnp.exp(log_end - rows_of(m_end_f, m_end_b))

        def update(cn_ref, cn, kb, w_half, v_half, carry_w):
            upd = _tn(kb, jnp.concatenate([(w_half * v_half).astype(BF16), w_half.astype(BF16)], axis=1))
            cn_ref[...] = jnp.concatenate([carry_w, carry_w], axis=1) * cn + jnp.where(state_block, upd, 0.0)

        update(cn_fwd_ref, cn_f, kb_f, w_end[0:c_sz], v_f, jnp.exp(end_f + m_f - m_end_f))
        update(cn_bwd_ref, cn_b, kb_b, w_end[c_sz:2 * c_sz], v_b, jnp.exp(end_b + m_b - m_end_b))
        return m_end_f, m_end_b

    zero_m = jnp.zeros((1, w), F32)
    cn_fwd_ref[...] = jnp.zeros_like(cn_fwd_ref)
    cn_bwd_ref[...] = jnp.zeros_like(cn_bwd_ref)
    lax.fori_loop(0, n_all, scan_step, (zero_m, zero_m), unroll=2)

    def readout(c, carry):
        rows = pl.ds(pl.multiple_of(c * ROW_TILE, ROW_TILE), ROW_TILE)
        tot = o_ref[0, rows, :] + ob_ref[rows, :]
        normed = jnp.concatenate(
            [_rms(tot[:, h * HEAD_V:(h + 1) * HEAD_V]) for h in range(HEADS)], axis=1) * gain_ref[...]
        og = pb_ref[0, rows, o_off:o_off + B_W]
        o_ref[0, rows, :] = normed * jax.nn.sigmoid(og)
        return carry

    lax.fori_loop(0, n_all * c_sz // ROW_TILE, readout, 0)


def _mlstm(pb3, bias, gain, n_ctx_rows):
    n_batch, rows, _ = pb3.shape
    kern = functools.partial(_mlstm_kernel, n_ctx=n_ctx_rows // CHUNK, n_all=rows // CHUNK)
    return pl.pallas_call(
        kern,
        out_shape=jax.ShapeDtypeStruct((n_batch, rows, B_W), F32),
        grid=(n_batch,),
        in_specs=[pl.BlockSpec((1, rows, PB_W), lambda b: (b, 0, 0)),
                  pl.BlockSpec((1, LANES), lambda b: (0, 0)),
                  pl.BlockSpec((1, B_W), lambda b: (0, 0))],
        out_specs=pl.BlockSpec((1, rows, B_W), lambda b: (b, 0, 0)),
        scratch_shapes=[pltpu.VMEM((HEADS * B_QK, 2 * B_W), F32), pltpu.VMEM((HEADS * B_QK, 2 * B_W), F32),
                        pltpu.VMEM((rows, B_W), F32)],
        compiler_params=_cparams(("arbitrary",)),
        name="mlstm",
    )(pb3, bias, gain)


def _attn_kernel(lam_ref, qp_ref, qr_ref, k_ref, v_ref, gain_ref, o_ref, *, n_ctx, q_tile0, lam_init):
    lam = lam_ref[0]
    q_tile = pl.program_id(2) + q_tile0
    lane = lax.broadcasted_iota(jnp.int32, (1, 2 * C_DQK), 1)

    def finish(parts):
        o = parts[0] - lam * parts[1]
        o_ref[0] = _rms(o) * gain_ref[...] * (1.0 - lam_init)

    def sub_query(q, j):
        return jnp.where(lane // C_DQK == j, q, jnp.zeros_like(q))

    def row_max(s):
        return jnp.max(s, axis=-1, keepdims=True)

    def row_sum(s):
        return jnp.sum(s, axis=-1, keepdims=True)

    def pv(ex, v):
        return jnp.dot(ex.astype(BF16), v, preferred_element_type=F32)

    @pl.when(q_tile * ROW_TILE < n_ctx)
    def _():
        kc = k_ref[0, 0:n_ctx, :]
        vc = v_ref[0, 0:n_ctx, :]
        parts = []
        for j in range(2):
            s = _nt(sub_query(qp_ref[0], j), kc)
            ex = jnp.exp2(s - row_max(s))
            parts.append(pv(ex, vc) / row_sum(ex))
        finish(parts)

    @pl.when(q_tile * ROW_TILE >= n_ctx)
    def _():
        n_all = k_ref.shape[1]
        kc = k_ref[0, 0:n_ctx, :]
        kl = k_ref[0, n_ctx:n_all, :]
        vc = v_ref[0, 0:n_ctx, :]
        vl = v_ref[0, n_ctx:n_all, :]
        parts = []
        for j in range(2):
            s_c = _nt(sub_query(qp_ref[0], j), kc)
            s_l = _nt(sub_query(qr_ref[0], j), kl)
            m = jnp.maximum(row_max(s_c), row_max(s_l))
            e_c = jnp.exp2(s_c - m)
            e_l = jnp.exp2(s_l - m)
            parts.append((pv(e_c, vc) + pv(e_l, vl)) / (row_sum(e_c) + row_sum(e_l)))
        finish(parts)


def _attention(pc3, lam, gain, n_ctx_rows, lam_init, with_ctx):
    n_batch, rows, _ = pc3.shape
    q_tile0 = 0 if with_ctx else n_ctx_rows // ROW_TILE
    n_q = rows // ROW_TILE - q_tile0
    hb = C_W // LANES
    kern = functools.partial(_attn_kernel, n_ctx=n_ctx_rows, q_tile0=q_tile0, lam_init=lam_init)
    grid_spec = pltpu.PrefetchScalarGridSpec(
        num_scalar_prefetch=1,
        grid=(n_batch, HEADS, n_q),
        in_specs=[pl.BlockSpec((1, ROW_TILE, LANES), lambda b, h, i, lam: (b, i + q_tile0, h)),
                  pl.BlockSpec((1, ROW_TILE, LANES), lambda b, h, i, lam: (b, i + q_tile0, hb + h)),
                  pl.BlockSpec((1, rows, LANES), lambda b, h, i, lam: (b, 0, 2 * hb + h)),
                  pl.BlockSpec((1, rows, LANES), lambda b, h, i, lam: (b, 0, 3 * hb + h)),
                  pl.BlockSpec((1, LANES), lambda b, h, i, lam: (0, 0))],
        out_specs=pl.BlockSpec((1, ROW_TILE, LANES), lambda b, h, i, lam: (b, i, h)),
    )
    return pl.pallas_call(
        kern,
        out_shape=jax.ShapeDtypeStruct((n_batch, n_q * ROW_TILE, C_W), F32),
        grid_spec=grid_spec,
        compiler_params=_cparams(("arbitrary", "arbitrary", "arbitrary")),
        name="diff_attn",
    )(lam, pc3, pc3, pc3, pc3, gain)


def _outproj_kernel(oa_ref, ob_ref, oc_ref, x_ref, mod_ref, n_ref, w_ref, rw_ref, rb_ref,
                    xn_ref, h_ref, route_ref, cnt_ref, carry_ref):
    d = x_ref.shape[1]
    n_exp = rw_ref.shape[1]

    @pl.when(pl.program_id(0) == 0)
    def _():
        carry_ref[...] = jnp.zeros_like(carry_ref)

    mix = (jnp.dot(oa_ref[...].astype(BF16), w_ref[0:A_W, :], preferred_element_type=F32)
           + jnp.dot(ob_ref[...].astype(BF16), w_ref[A_W:A_W + B_W, :], preferred_element_type=F32)
           + jnp.dot(oc_ref[...].astype(BF16), w_ref[A_W + B_W:A_W + B_W + C_W, :], preferred_element_type=F32))
    xn = x_ref[...] + mod_ref[0, :, 2 * d:3 * d] * (_rms(mix) * n_ref[1:2, :])
    xn_ref[...] = xn
    h = _rms(xn) * n_ref[2:3, :] * (1.0 + mod_ref[0, :, 4 * d:5 * d]) + mod_ref[0, :, 3 * d:4 * d]
    h_ref[...] = _pack_bf16_pairs(h)

    h_hi = h.astype(BF16)
    h_mid = (h - h_hi.astype(F32)).astype(BF16)
    logits = _nt(rw_ref[0], h_hi) + _nt(rw_ref[1], h_hi) + _nt(rw_ref[0], h_mid) + rb_ref[...]
    e_sub = lax.broadcasted_iota(jnp.int32, logits.shape, 0)
    cur = logits
    picks, vals = [], []
    for _ in range(TOP_K):
        mx = jnp.max(cur, axis=0, keepdims=True)
        idx = jnp.min(jnp.where(cur == mx, e_sub, n_exp), axis=0, keepdims=True)
        hit = e_sub == idx
        cur = jnp.where(hit, -jnp.inf, cur)
        picks.append((idx, hit.astype(F32)))
        vals.append(mx)
    exps = [jnp.exp(vv - vals[0]) for vv in vals]
    total = exps[0] + exps[1] + exps[2] + exps[3]

    chosen = picks[0][1] + picks[1][1] + picks[2][1] + picks[3][1]
    tm = logits.shape[1]
    before = (lax.broadcasted_iota(jnp.int32, (tm, tm), 0) < lax.broadcasted_iota(jnp.int32, (tm, tm), 1))
    seen = jnp.dot(chosen.astype(BF16), before.astype(BF16), preferred_element_type=F32) + carry_ref[...]
    new_carry = carry_ref[...] + jnp.sum(chosen, axis=1, keepdims=True)
    carry_ref[...] = new_carry
    cnt_ref[...] = new_carry

    o_row = lax.broadcasted_iota(jnp.int32, route_ref.shape, 0)
    route = jnp.zeros(route_ref.shape, F32)
    for kk in range(TOP_K):
        idx, hit = picks[kk]
        rank = jnp.sum(hit * seen, axis=0, keepdims=True)
        route = (route + jnp.where(o_row == kk, idx.astype(F32), 0.0)
                 + jnp.where(o_row == TOP_K + kk, exps[kk] / total, 0.0)
                 + jnp.where(o_row == 2 * TOP_K + kk, rank, 0.0))
    route_ref[...] = route


ROUTE_ROWS = 16


def _outproj(oa, ob, oc, xy, mod, norms, w_out_b, router_wt, router_b, layer, n_tiles, in_tile, oc_tile, mod_row):
    d = xy.shape[1]
    n_exp = router_wt.shape[2]
    t_out = n_tiles * ROW_TILE
    xn, h, route_t, counts = pl.pallas_call(
        _outproj_kernel,
        out_shape=(jax.ShapeDtypeStruct((t_out, d), F32),
                   jax.ShapeDtypeStruct((t_out, d // 2), jnp.int32),
                   jax.ShapeDtypeStruct((n_tiles * ROUTE_ROWS, ROW_TILE), F32),
                   jax.ShapeDtypeStruct((n_exp, 1), F32)),
        grid=(n_tiles,),
        in_specs=[pl.BlockSpec((ROW_TILE, A_W), lambda i: (in_tile(i), 0)),
                  pl.BlockSpec((ROW_TILE, B_W), lambda i: (in_tile(i), 0)),
                  pl.BlockSpec((ROW_TILE, C_W), lambda i: (oc_tile(i), 0)),
                  pl.BlockSpec((ROW_TILE, d), lambda i: (in_tile(i), 0)),
                  pl.BlockSpec((1, 1, 6 * d), lambda i: (mod_row(i), 0, 0)),
                  pl.BlockSpec((None, 4, d), lambda i: (layer, 0, 0)),
                  pl.BlockSpec((None, d, d), lambda i: (layer, 0, 0)),
                  pl.BlockSpec((None, 2, n_exp, d), lambda i: (layer, 0, 0, 0)),
                  pl.BlockSpec((None, n_exp, 1), lambda i: (layer, 0, 0))],
        out_specs=(pl.BlockSpec((ROW_TILE, d), lambda i: (i, 0)),
                   pl.BlockSpec((ROW_TILE, d // 2), lambda i: (i, 0)),
                   pl.BlockSpec((ROUTE_ROWS, ROW_TILE), lambda i: (i, 0)),
                   pl.BlockSpec((n_exp, 1), lambda i: (0, 0))),
        scratch_shapes=[pltpu.VMEM((n_exp, 1), F32)],
        compiler_params=_cparams(("arbitrary",)),
        name="outproj_router",
    )(oa, ob, oc, xy, mod, norms, w_out_b, router_wt, router_b)
    return xn, h, route_t.reshape(n_tiles, ROUTE_ROWS, ROW_TILE), counts[:, 0]


PAIR_BLOCK = 2 * LANES


def _moe_kernel(te_ref, first_ref, valid_ref, nu_ref, xs_ref, w1_ref, b1_ref, w2_ref, b2_ref, ys_ref,
                w1p_ref, w2b_ref):
    i = pl.program_id(0)
    two_f = w1_ref.shape[1]
    n_blk = two_f // PAIR_BLOCK

    @pl.when(jnp.logical_and(i < nu_ref[0], first_ref[i] == 1))
    def _():
        r = lax.broadcasted_iota(jnp.int32, (PAIR_BLOCK, PAIR_BLOCK), 0)
        c = lax.broadcasted_iota(jnp.int32, (PAIR_BLOCK, PAIR_BLOCK), 1)
        perm = (r == jnp.where(c < LANES, 2 * c, 2 * (c - LANES) + 1)).astype(BF16)
        for blk in range(n_blk):
            cols = slice(blk * PAIR_BLOCK, (blk + 1) * PAIR_BLOCK)
            w1p_ref[:, cols] = jnp.dot(w1_ref[:, cols].astype(BF16), perm,
                                       preferred_element_type=F32).astype(BF16)
        w2b_ref[...] = w2_ref[...].astype(BF16)

    tile_rows = xs_ref.shape[0]
    used = i < nu_ref[0]
    valid = valid_ref[i]

    def expert_ffn(n_rows):
        x = _unpack_bf16_pairs(xs_ref[0:n_rows, :]).astype(BF16)
        row = lax.broadcasted_iota(jnp.int32, (n_rows, 1), 0)
        x = jnp.where(row < valid, x, jnp.zeros_like(x))
        hid = jnp.dot(x, w1p_ref[...], preferred_element_type=F32) + b1_ref[...]
        acts = []
        for blk in range(n_blk):
            glu = jnp.minimum(hid[:, blk * PAIR_BLOCK:blk * PAIR_BLOCK + LANES], SWIGLU_LIMIT)
            lin = jnp.clip(hid[:, blk * PAIR_BLOCK + LANES:(blk + 1) * PAIR_BLOCK], -SWIGLU_LIMIT, SWIGLU_LIMIT)
            acts.append((glu * jax.nn.sigmoid(SWIGLU_ALPHA * glu) * (lin + 1.0)).astype(BF16))
        y = jnp.dot(jnp.concatenate(acts, axis=1), w2b_ref[...], preferred_element_type=F32) + b2_ref[...]
        ys_ref[0:n_rows, :] = _pack_bf16_pairs(y)
        if n_rows < tile_rows:
            ys_ref[n_rows:tile_rows, :] = jnp.zeros((tile_rows - n_rows, ys_ref.shape[1]), ys_ref.dtype)

    @pl.when(jnp.logical_and(used, valid > tile_rows // 2))
    def _():
        expert_ffn(tile_rows)

    @pl.when(jnp.logical_and(used, valid <= tile_rows // 2))
    def _():
        expert_ffn(tile_rows // 2)

    @pl.when(jnp.logical_not(used))
    def _():
        ys_ref[...] = jnp.zeros_like(ys_ref)


def _moe(tile_expert, tile_first, tile_valid, n_used, xs, w1, b1p, w2, b2, layer):
    r_max, half_d = xs.shape
    d = 2 * half_d
    two_f = w1.shape[3]
    f = two_f // 2
    n_tiles = r_max // MOE_TILE

    def row_tile(i, te, fi, va, nu):
        return (jnp.maximum(jnp.minimum(i, nu[0] - 1), 0), 0)

    grid_spec = pltpu.PrefetchScalarGridSpec(
        num_scalar_prefetch=4,
        grid=(n_tiles,),
        in_specs=[pl.BlockSpec((MOE_TILE, half_d), row_tile),
                  pl.BlockSpec((None, None, d, two_f), lambda i, te, fi, va, nu: (layer, te[i], 0, 0)),
                  pl.BlockSpec((None, None, 1, two_f), lambda i, te, fi, va, nu: (layer, te[i], 0, 0)),
                  pl.BlockSpec((None, None, f, d), lambda i, te, fi, va, nu: (layer, te[i], 0, 0)),
                  pl.BlockSpec((None, None, 1, d), lambda i, te, fi, va, nu: (layer, te[i], 0, 0))],
        out_specs=pl.BlockSpec((MOE_TILE, half_d), lambda i, te, fi, va, nu: (i, 0)),
        scratch_shapes=[pltpu.VMEM((d, two_f), BF16), pltpu.VMEM((f, d), BF16)],
    )
    return pl.pallas_call(
        _moe_kernel,
        out_shape=jax.ShapeDtypeStruct((r_max, half_d), jnp.int32),
        grid_spec=grid_spec,
        compiler_params=_cparams(("arbitrary",)),
        name="moe_experts",
    )(tile_expert, tile_first, tile_valid, n_used, xs, w1, b1p, w2, b2)


SC_CORES = 2
SC_SUBCORES = 16
SC_CHUNK = 64


def _dispatch_rows(hp, pos, r_max):
    t, width = hp.shape
    workers = SC_CORES * SC_SUBCORES
    assert t % (workers * SC_CHUNK) == 0
    per_worker = t // (workers * SC_CHUNK)
    idx = pos.reshape(-1, TOP_K, ROW_TILE // SC_CHUNK, SC_CHUNK).transpose(0, 2, 1, 3).reshape(
        workers, per_worker * TOP_K, SC_CHUNK)
    mesh = plsc.VectorSubcoreMesh(core_axis_name="c", subcore_axis_name="s",
                                  num_cores=SC_CORES, num_subcores=SC_SUBCORES)

    @functools.partial(
        pl.kernel, mesh=mesh,
        out_type=jax.ShapeDtypeStruct((r_max, width), hp.dtype),
        scratch_types=[pltpu.VMEM((per_worker * TOP_K, SC_CHUNK), jnp.int32),
                       pltpu.VMEM((2, SC_CHUNK, width), hp.dtype),
                       pltpu.SemaphoreType.DMA((2,)),
                       pltpu.SemaphoreType.DMA((2,))],
    )
    def scatter(hp_hbm, idx_hbm, out_hbm, idx_v, rows_v, read_sem, write_sem):
        wid = lax.axis_index("s") * SC_CORES + lax.axis_index("c")
        pltpu.sync_copy(idx_hbm.at[wid], idx_v)

        def read(j):
            row0 = pl.multiple_of((wid * per_worker + j) * SC_CHUNK, SC_CHUNK)
            return pltpu.make_async_copy(hp_hbm.at[pl.ds(row0, SC_CHUNK)], rows_v.at[j % 2], read_sem.at[j % 2])

        def write(j, k):
            return pltpu.make_async_copy(rows_v.at[j % 2], out_hbm.at[idx_v.at[j * TOP_K + k]],
                                         write_sem.at[j % 2])

        read(0).start()
        for j in range(per_worker):
            read(j).wait()
            for k in range(TOP_K):
                write(j, k).start()
            if j + 1 < per_worker:
                if j >= 1:
                    for k in range(TOP_K):
                        write(j - 1, k).wait()
                read(j + 1).start()
        for j in range(max(per_worker - 2, 0), per_worker):
            for k in range(TOP_K):
                write(j, k).wait()

    return scatter(hp, idx)


def _combine_rows(ysp, pos):
    width = ysp.shape[1]
    t = pos.shape[0] * pos.shape[2]
    workers = SC_CORES * SC_SUBCORES
    assert t % (workers * SC_CHUNK) == 0
    per_worker = t // (workers * SC_CHUNK)
    units = per_worker * TOP_K
    idx = pos.reshape(-1, TOP_K, ROW_TILE // SC_CHUNK, SC_CHUNK).transpose(0, 2, 1, 3).reshape(
        workers, units, SC_CHUNK)
    mesh = plsc.VectorSubcoreMesh(core_axis_name="c", subcore_axis_name="s",
                                  num_cores=SC_CORES, num_subcores=SC_SUBCORES)

    @functools.partial(
        pl.kernel, mesh=mesh,
        out_type=jax.ShapeDtypeStruct((TOP_K, t, width), ysp.dtype),
        scratch_types=[pltpu.VMEM((units, SC_CHUNK), jnp.int32),
                       pltpu.VMEM((2, SC_CHUNK, width), ysp.dtype),
                       pltpu.SemaphoreType.DMA((2,)),
                       pltpu.SemaphoreType.DMA((2,))],
    )
    def gather(ys_hbm, idx_hbm, out_hbm, idx_v, rows_v, read_sem, write_sem):
        wid = lax.axis_index("s") * SC_CORES + lax.axis_index("c")
        pltpu.sync_copy(idx_hbm.at[wid], idx_v)

        def read(u):
            return pltpu.make_async_copy(ys_hbm.at[idx_v.at[u]], rows_v.at[u % 2], read_sem.at[u % 2])

        def write(u):
            row0 = pl.multiple_of((wid * per_worker + u // TOP_K) * SC_CHUNK, SC_CHUNK)
            return pltpu.make_async_copy(rows_v.at[u % 2], out_hbm.at[u % TOP_K, pl.ds(row0, SC_CHUNK)],
                                         write_sem.at[u % 2])

        read(0).start()
        for u in range(units):
            read(u).wait()
            write(u).start()
            if u + 1 < units:
                if u >= 1:
                    write(u - 1).wait()
                read(u + 1).start()
        for u in range(max(units - 2, 0), units):
            write(u).wait()

    return gather(ysp, idx)


def _ffn_residual_kernel(x_ref, w_ref, y0_ref, y1_ref, y2_ref, y3_ref, mod_ref, n_ref, o_ref):
    d = x_ref.shape[1]
    wts = w_ref[...]
    ffn = None
    for k, y_ref in enumerate((y0_ref, y1_ref, y2_ref, y3_ref)):
        term = wts[:, k:k + 1] * _unpack_bf16_pairs(y_ref[...])
        ffn = term if ffn is None else ffn + term
    o_ref[...] = x_ref[...] + mod_ref[0, :, 5 * d:6 * d] * (_rms(ffn) * n_ref[3:4, :])


def _ffn_residual(xn, wts, ys_by_k, mod, norms, layer, mod_row):
    t, d = xn.shape
    row_spec = pl.BlockSpec((ROW_TILE, d), lambda i: (i, 0))

    def y_spec(k):
        return pl.BlockSpec((None, ROW_TILE, d // 2), lambda i: (k, i, 0))

    return pl.pallas_call(
        _ffn_residual_kernel,
        out_shape=jax.ShapeDtypeStruct((t, d), F32),
        grid=(t // ROW_TILE,),
        in_specs=[row_spec,
                  pl.BlockSpec((ROW_TILE, TOP_K), lambda i: (i, 0)),
                  y_spec(0), y_spec(1), y_spec(2), y_spec(3),
                  pl.BlockSpec((1, 1, 6 * d), lambda i: (mod_row(i), 0, 0)),
                  pl.BlockSpec((None, 4, d), lambda i: (layer, 0, 0))],
        out_specs=row_spec,
        compiler_params=_cparams(("arbitrary",)),
        name="ffn_residual",
    )(xn, wts, ys_by_k, ys_by_k, ys_by_k, ys_by_k, mod, norms)


def _dispatch_plan(route_t, counts, r_max):
    n_row_tiles = route_t.shape[0]
    idx = route_t[:, 0:TOP_K, :].astype(jnp.int32)
    wts = route_t[:, TOP_K:2 * TOP_K, :].transpose(0, 2, 1).reshape(n_row_tiles * ROW_TILE, TOP_K)
    rank = route_t[:, 2 * TOP_K:3 * TOP_K, :].astype(jnp.int32)
    cnt = counts.astype(jnp.int32)
    padded = (cnt + MOE_TILE - 1) // MOE_TILE * MOE_TILE
    ends = jnp.cumsum(padded)
    starts = ends - padded
    pos = rank
    for e in range(cnt.shape[0]):
        pos = pos + jnp.where(idx == e, starts[e], 0)
    n_tiles = r_max // MOE_TILE
    n_used = ends[-1] // MOE_TILE
    tile_ids = jnp.minimum(jnp.arange(n_tiles, dtype=jnp.int32), n_used - 1)
    tile_expert = jnp.sum((ends // MOE_TILE)[None, :] <= tile_ids[:, None], axis=1).astype(jnp.int32)
    tile_first = jnp.concatenate(
        [jnp.ones((1,), jnp.int32), (tile_expert[1:] != tile_expert[:-1]).astype(jnp.int32)])
    of_expert = (tile_expert[:, None] == jnp.arange(cnt.shape[0], dtype=jnp.int32)[None, :]).astype(jnp.int32)
    first_tile = jnp.sum(of_expert * ((ends - padded) // MOE_TILE)[None, :], axis=1)
    tile_valid = jnp.clip(jnp.sum(of_expert * cnt[None, :], axis=1) - (tile_ids - first_tile) * MOE_TILE,
                          0, MOE_TILE).astype(jnp.int32)
    return pos, wts, tile_expert, tile_first, tile_valid, n_used.reshape(1).astype(jnp.int32)


def _rope_tables(length, n_ctx_rows):
    rows = length // GRID_W
    row = jnp.repeat(jnp.arange(rows, dtype=F32), GRID_W)
    col = jnp.tile(jnp.arange(GRID_W, dtype=F32), rows)
    n_freq = C_DQK // 4
    inv_freq = ROPE_THETA ** (-jnp.arange(n_freq, dtype=F32) / n_freq)
    ang_r = row[:, None] * inv_freq
    ang_c = col[:, None] * inv_freq
    cos = jnp.concatenate([jnp.cos(ang_r), jnp.cos(ang_r), jnp.cos(ang_c), jnp.cos(ang_c)], axis=-1)
    sin = jnp.concatenate([-jnp.sin(ang_r), jnp.sin(ang_r), -jnp.sin(ang_c), jnp.sin(ang_c)], axis=-1)
    reps = C_W // C_DQK
    cos = jnp.concatenate([jnp.ones((n_ctx_rows, C_W), F32), jnp.tile(cos, (1, reps))], axis=0)
    sin = jnp.concatenate([jnp.zeros((n_ctx_rows, C_W), F32), jnp.tile(sin, (1, reps))], axis=0)
    return cos, sin


def _hgrn_lower_bound(table, layer):
    p = jax.nn.softmax(table.astype(F32), axis=1)
    cum = jnp.cumsum(p, axis=1) - p[:, :1]
    return jnp.clip(cum[:, layer], 0.0, 1.0)


def kernel(x, c, ctx, c_ctx, ada_w, ada_b, sandwich_norms, w_in, w_out, hgrn_lower_bounds, hgrn_norm,
           mlstm_gate_bias, mlstm_norm, diff_lambdas, diff_norm, router_w, router_b, moe_w1, moe_b1,
           moe_w2, moe_b2):
    n_batch, seq, d = x.shape
    n_ctx_rows = ctx.shape[1]
    depth = w_in.shape[0]
    n_exp = router_w.shape[2]
    assert seq % ROW_TILE == 0 and n_ctx_rows % ROW_TILE == 0 and seq % GRID_W == 0
    rows_b = n_ctx_rows + seq
    tiles_b = rows_b // ROW_TILE
    ctx_tiles = n_ctx_rows // ROW_TILE
    lat_tiles = seq // ROW_TILE

    b_main = PA_W + 2 * HEADS * B_QK + 2 * B_W
    w_in_p = jnp.concatenate(
        [w_in[:, :, :b_main],
         jnp.pad(w_in[:, :, b_main:b_main + N_GATE], ((0, 0), (0, 0), (0, LANES - N_GATE))),
         w_in[:, :, b_main + N_GATE:]], axis=2).astype(BF16)
    w_out_b = w_out.astype(BF16)
    two_f = moe_b1.shape[2]
    b1p = moe_b1.reshape(depth, n_exp, two_f // PAIR_BLOCK, LANES, 2).transpose(0, 1, 2, 4, 3).reshape(
        depth, n_exp, 1, two_f)
    b2 = moe_b2[:, :, None, :]
    ada_b3 = ada_b[:, None, :]
    router_b3 = router_b[:, :, None]
    rw_t = router_w.transpose(0, 2, 1)
    rw_hi = rw_t.astype(BF16)
    router_wt = jnp.stack([rw_hi, (rw_t - rw_hi.astype(F32)).astype(BF16)], axis=1)
    gate_bias = jnp.pad(mlstm_gate_bias, ((0, 0), (0, LANES - N_GATE)))
    cos, sin = _rope_tables(seq, n_ctx_rows)

    cond_rows = (n_batch + 1 + 7) // 8 * 8
    cvec = jnp.zeros((cond_rows, d), F32).at[:n_batch].set(c).at[n_batch].set(c_ctx)

    def mod_row_all(i):
        return jnp.where(i % tiles_b < ctx_tiles, n_batch, i // tiles_b)

    xy = jnp.concatenate([ctx, x], axis=1).reshape(n_batch * rows_b, d)

    for layer in range(depth):
        last = layer == depth - 1
        mod = _adaln(cvec, ada_w, ada_b3, layer).reshape(cond_rows, 1, 6 * d)
        lb = _hgrn_lower_bound(hgrn_lower_bounds, layer)
        lam_init = 0.8 - 0.6 * math.exp(-0.3 * layer)
        lq1, lk1, lq2, lk2 = diff_lambdas[layer].astype(F32)
        lam = (jnp.exp(jnp.sum(lq1 * lk1)) - jnp.exp(jnp.sum(lq2 * lk2)) + lam_init).reshape(1)

        pa, pb, pc = _inproj(xy, mod, sandwich_norms, w_in_p, cos, sin, layer, tiles_b, ctx_tiles, n_batch)
        oa = _hgrn(pa.reshape(n_batch, rows_b, PA_W), lb, jnp.tile(hgrn_norm[layer], HEADS)[None, :], n_ctx_rows)
        ob = _mlstm(pb.reshape(n_batch, rows_b, PB_W), gate_bias[layer][None, :], mlstm_norm[layer][None, :],
                    n_ctx_rows)
        oc = _attention(pc.reshape(n_batch, rows_b, PC_OUT), lam, diff_norm[layer][None, :], n_ctx_rows,
                        lam_init, not last)

        if last:
            n_tiles = n_batch * lat_tiles
            in_tile = lambda i: (i // lat_tiles) * tiles_b + ctx_tiles + i % lat_tiles
            mod_row = lambda i: i // lat_tiles
        else:
            n_tiles = n_batch * tiles_b
            in_tile = lambda i: i
            mod_row = mod_row_all
        xn, h, route, counts = _outproj(
            oa.reshape(-1, A_W), ob.reshape(-1, B_W), oc.reshape(-1, C_W), xy, mod, sandwich_norms, w_out_b,
            router_wt, router_b3, layer, n_tiles, in_tile, lambda i: i, mod_row)

        r_max = n_tiles * ROW_TILE * TOP_K + n_exp * MOE_TILE
        pos, wts, tile_expert, tile_first, tile_valid, n_used = _dispatch_plan(route, counts, r_max)
        xs = _dispatch_rows(h, pos, r_max)
        ys = _moe(tile_expert, tile_first, tile_valid, n_used, xs, moe_w1, b1p, moe_w2, b2, layer)
        xy = _ffn_residual(xn, wts, _combine_rows(ys, pos), mod, sandwich_norms, layer, mod_row)

    return xy.reshape(n_batch, seq, d)
```

```python
import functools
import math

import jax
import jax.numpy as jnp
from jax import lax
from jax.experimental import pallas as pl
from jax.experimental.pallas import tpu as pltpu
from jax.experimental.pallas import tpu_sc as plsc

F32 = jnp.float32
BF16 = jnp.bfloat16
HI = lax.Precision.HIGHEST

HEADS = 4
A_W = 256
B_QK = 32
B_W = 256
C_DQK = 64
C_DV = 2 * C_DQK
ATTN_KEY_SCALE = C_DQK ** -0.5 * math.log2(math.e)
C_W = 512
HEAD_V = 64
N_GATE = 16
GRID_W = 64
TOP_K = 4
SWIGLU_ALPHA = 1.702
SWIGLU_LIMIT = 7.0
ROPE_THETA = 10000.0
NORM_EPS = 1e-6
MASK_NEG = -1e30
F_MIN = 1e-12

LANES = 128
ROW_TILE = 256
CHUNK = 64
MOE_TILE = 512
VMEM_LIMIT = 56 * 1024 * 1024

PA_W = 5 * A_W
PB_W = 2 * HEADS * B_QK + 2 * B_W + LANES
PC_IN = 3 * C_W
PC_OUT = 4 * C_W
W_IN_PAD = PA_W + PB_W + PC_IN


def _cparams(sem):
    return pltpu.CompilerParams(dimension_semantics=sem, vmem_limit_bytes=VMEM_LIMIT)


def _nt(a, b):
    return lax.dot_general(a, b, (((1,), (1,)), ((), ())), preferred_element_type=F32)


def _tn(a, b, precision=None):
    return lax.dot_general(a, b, (((0,), (0,)), ((), ())), preferred_element_type=F32, precision=precision)


def _rms(x):
    return x * lax.rsqrt(jnp.mean(x * x, axis=-1, keepdims=True) + NORM_EPS)


def _silu(x):
    return x * jax.nn.sigmoid(x)


def _pack_bf16_pairs(x):
    half = x.shape[1] // 2
    bits = pltpu.bitcast(x.astype(BF16).astype(F32), jnp.uint32)
    return pltpu.bitcast((bits[:, :half] >> 16) | (bits[:, half:] & jnp.uint32(0xFFFF0000)), jnp.int32)


def _unpack_bf16_pairs(words):
    bits = pltpu.bitcast(words, jnp.uint32)
    return jnp.concatenate([pltpu.bitcast(bits << 16, F32),
                            pltpu.bitcast(bits & jnp.uint32(0xFFFF0000), F32)], axis=1)


def _adaln_kernel(c_ref, w_ref, b_ref, o_ref):
    cond = _silu(c_ref[...])
    o_ref[...] = jnp.dot(cond, w_ref[...], preferred_element_type=F32, precision=HI) + b_ref[...]


def _adaln(cvec, ada_w, ada_b, layer):
    rows, d = cvec.shape
    return pl.pallas_call(
        _adaln_kernel,
        out_shape=jax.ShapeDtypeStruct((rows, 6 * d), F32),
        grid=(6,),
        in_specs=[pl.BlockSpec((rows, d), lambda j: (0, 0)),
                  pl.BlockSpec((None, d, d), lambda j: (layer, 0, j)),
                  pl.BlockSpec((None, 1, d), lambda j: (layer, 0, j))],
        out_specs=pl.BlockSpec((rows, d), lambda j: (0, j)),
        compiler_params=_cparams(("arbitrary",)),
        name="adaln",
    )(cvec, ada_w, ada_b)


def _inproj_kernel(x_ref, mod_ref, g_ref, w_ref, cos_ref, sin_ref, pa_ref, pb_ref, pc_ref):
    d = x_ref.shape[1]
    shift = mod_ref[0, :, 0:d]
    scale = mod_ref[0, :, d:2 * d]
    h = _rms(x_ref[...]) * g_ref[0:1, :] * (1.0 + scale) + shift
    hb = h.astype(BF16)
    pa_ref[...] = jnp.dot(hb, w_ref[:, 0:PA_W], preferred_element_type=F32)
    pb_ref[...] = jnp.dot(hb, w_ref[:, PA_W:PA_W + PB_W], preferred_element_type=F32)
    pc = jnp.dot(hb, w_ref[:, PA_W + PB_W:W_IN_PAD], preferred_element_type=F32)
    q = pc[:, 0:C_W]
    k = pc[:, C_W:2 * C_W] * ATTN_KEY_SCALE
    cos = cos_ref[...]
    sin = sin_ref[...]
    lane = lax.broadcasted_iota(jnp.int32, q.shape, 1)
    first = (lane % 32) < 16

    def rope(t):
        partner = jnp.where(first, pltpu.roll(t, C_W - 16, 1), pltpu.roll(t, 16, 1))
        return t * cos + partner * sin

    pc_ref[:, 0:C_W] = q.astype(BF16)
    pc_ref[:, C_W:2 * C_W] = rope(q).astype(BF16)
    pc_ref[:, 2 * C_W:3 * C_W] = rope(k).astype(BF16)
    pc_ref[:, 3 * C_W:4 * C_W] = pc[:, 2 * C_W:3 * C_W].astype(BF16)


def _inproj(xy, mod, norms, w_in_p, cos, sin, layer, tiles_per_batch, n_ctx_tiles, n_batch):
    t_all, d = xy.shape
    n_tiles = t_all // ROW_TILE

    def mod_row(i):
        return jnp.where(i % tiles_per_batch < n_ctx_tiles, n_batch, i // tiles_per_batch)

    return pl.pallas_call(
        _inproj_kernel,
        out_shape=(jax.ShapeDtypeStruct((t_all, PA_W), F32),
                   jax.ShapeDtypeStruct((t_all, PB_W), F32),
                   jax.ShapeDtypeStruct((t_all, PC_OUT), BF16)),
        grid=(n_tiles,),
        in_specs=[pl.BlockSpec((ROW_TILE, d), lambda i: (i, 0)),
                  pl.BlockSpec((1, 1, 6 * d), lambda i: (mod_row(i), 0, 0)),
                  pl.BlockSpec((None, 4, d), lambda i: (layer, 0, 0)),
                  pl.BlockSpec((None, d, W_IN_PAD), lambda i: (layer, 0, 0)),
                  pl.BlockSpec((ROW_TILE, C_W), lambda i: (i % tiles_per_batch, 0)),
                  pl.BlockSpec((ROW_TILE, C_W), lambda i: (i % tiles_per_batch, 0))],
        out_specs=(pl.BlockSpec((ROW_TILE, PA_W), lambda i: (i, 0)),
                   pl.BlockSpec((ROW_TILE, PB_W), lambda i: (i, 0)),
                   pl.BlockSpec((ROW_TILE, PC_OUT), lambda i: (i, 0))),
        compiler_params=_cparams(("arbitrary",)),
        name="inproj",
    )(xy, mod, norms, w_in_p, cos, sin)


def _hgrn_kernel(pa_ref, lb_ref, gain_ref, o_ref, st_ref, ob_ref, *, n_ctx, n_all):
    c_sz, w = CHUNK, A_W
    same_head = (lax.broadcasted_iota(jnp.int32, (w, w), 0) // HEAD_V
                 == lax.broadcasted_iota(jnp.int32, (w, w), 1) // HEAD_V)
    head_ones = same_head.astype(BF16)
    assert c_sz == HEAD_V
    t_i = lax.broadcasted_iota(jnp.int32, (c_sz, w), 0)
    s_i = lax.broadcasted_iota(jnp.int32, (c_sz, w), 1) % c_sz
    t_row = lax.broadcasted_iota(jnp.int32, (c_sz, 1), 0)
    row8 = t_row % 8

    def grouped_rows(a, k):
        return jnp.concatenate(
            [jnp.broadcast_to(a[8 * j + k:8 * j + k + 1, :], (8, w)) for j in range(c_sz // 8)], axis=0)

    def halving_levels(rev):
        out = []
        b = c_sz // 2
        while b >= 1:
            def later(i):
                return ((i % (2 * b)) < b) if rev else ((i % (2 * b)) >= b)
            live = jnp.logical_and(t_i // (2 * b) == s_i // (2 * b),
                                   jnp.logical_and(later(t_i), jnp.logical_not(later(s_i))))
            out.append((b, live.astype(F32), jnp.where(later(t_row), 1.0, -1.0)))
            b //= 2
        return out

    def both(x_f, x_b):
        return jnp.concatenate([x_f, x_b], axis=0)

    def rows_of(row_f, row_b):
        return both(jnp.broadcast_to(row_f, (c_sz, w)), jnp.broadcast_to(row_b, (c_sz, w)))

    self_mask = both(*[(s_i == t_i).astype(F32)] * 2)
    levels = [(b, both(live_f, live_b), both(sign_f, sign_b))
              for (b, live_f, sign_f), (_, live_b, sign_b) in zip(halving_levels(False), halving_levels(True))]
    lb2 = rows_of(lb_ref[0:1, :], lb_ref[1:2, :])
    r2 = lax.broadcasted_iota(jnp.int32, (2 * c_sz, 2 * c_sz), 0)
    c2 = lax.broadcasted_iota(jnp.int32, (2 * c_sz, 2 * c_sz), 1)
    tri2 = jnp.logical_or(jnp.logical_and(r2 < c_sz, c2 <= r2),
                          jnp.logical_and(r2 >= c_sz, c2 >= r2)).astype(BF16)
    zero = jnp.zeros((), BF16)

    def boundary_rows(cum, b, rev):
        if b >= 8:
            return jnp.concatenate(
                [jnp.broadcast_to(cum[r0 + (b if rev else b - 1):r0 + (b if rev else b - 1) + 1, :], (2 * b, w))
                 for r0 in range(0, c_sz, 2 * b)], axis=0)
        ref = None
        for g in reversed(range(8 // (2 * b))):
            cand = grouped_rows(cum, g * 2 * b + (b if rev else b - 1))
            ref = cand if ref is None else jnp.where(row8 < (g + 1) * 2 * b, cand, ref)
        return ref

    def scan_step(n, carry):
        c_f = n
        c_b = jnp.where(n < n_ctx, n_ctx - 1 - n, n_all - 1 - (n - n_ctx))
        rows_f = pl.ds(pl.multiple_of(c_f * c_sz, c_sz), c_sz)
        rows_b = pl.ds(pl.multiple_of(c_b * c_sz, c_sz), c_sz)
        q_pre = both(pa_ref[0, rows_f, 0:A_W], pa_ref[0, rows_b, 0:A_W])
        v_f = pa_ref[0, rows_f, A_W:2 * A_W].astype(BF16)
        v_b = pa_ref[0, rows_b, A_W:2 * A_W].astype(BF16)
        f_pre = both(pa_ref[0, rows_f, 2 * A_W:3 * A_W], pa_ref[0, rows_b, 3 * A_W:4 * A_W])
        q = _silu(q_pre)
        f = lb2 + (1.0 - lb2) * jax.nn.sigmoid(f_pre)
        log_f = jnp.log(jnp.maximum(f, F_MIN))
        kk = (1.0 - lb2) * jax.nn.sigmoid(-f_pre)
        cum = sum(jnp.dot(tri2, piece, preferred_element_type=F32) for piece in _split3(log_f))
        end_f = cum[c_sz - 1:c_sz, :]
        end_b = cum[c_sz:c_sz + 1, :]

        st_f = st_ref[0]
        st_b = st_ref[1]
        q_in = (q * jnp.exp(cum)).astype(BF16)
        o = both(_nt(q_in[0:c_sz], st_f.astype(BF16)), _nt(q_in[c_sz:2 * c_sz], st_b.astype(BF16)))
        k_end = (kk * jnp.exp(rows_of(end_f, end_b) - cum)).astype(BF16)
        st_ref[0] = st_f * jnp.exp(end_f) + jnp.where(same_head, _tn(v_f, k_end[0:c_sz]), 0.0)
        st_ref[1] = st_b * jnp.exp(end_b) + jnp.where(same_head, _tn(v_b, k_end[c_sz:2 * c_sz]), 0.0)

        def scores(qa, ka):
            return _nt(qa, jnp.where(same_head, jnp.concatenate([ka] * HEADS, axis=0), zero))

        p = self_mask * jnp.dot((q * kk).astype(BF16), head_ones, preferred_element_type=F32)
        for b, live, sign in levels:
            ref = both(boundary_rows(cum[0:c_sz], b, False), boundary_rows(cum[c_sz:2 * c_sz], b, True))
            z = jnp.exp((cum - ref) * sign)
            qa = (q * z).astype(BF16)
            ka = (kk * z).astype(BF16)
            p = p + live * both(scores(qa[0:c_sz], ka[0:c_sz]), scores(qa[c_sz:2 * c_sz], ka[c_sz:2 * c_sz]))
        pb = p.astype(BF16)

        def values(p_half, v_half):
            return jnp.dot(p_half, jnp.where(same_head, jnp.concatenate([v_half] * HEADS, axis=0), zero),
                           preferred_element_type=F32)

        o_ref[0, rows_f, :] = o[0:c_sz] + values(pb[0:c_sz], v_f)
        ob_ref[rows_b, :] = o[c_sz:2 * c_sz] + values(pb[c_sz:2 * c_sz], v_b)
        return carry

    st_ref[...] = jnp.zeros_like(st_ref)
    lax.fori_loop(0, n_all, scan_step, 0)

    def readout(c, carry):
        rows = pl.ds(pl.multiple_of(c * ROW_TILE, ROW_TILE), ROW_TILE)
        tot = o_ref[0, rows, :] + ob_ref[rows, :]
        ms = sum(jnp.dot(piece, head_ones, preferred_element_type=F32)
                 for piece in _split3(tot * tot)) * (1.0 / HEAD_V)
        g = pa_ref[0, rows, 4 * A_W:5 * A_W]
        o_ref[0, rows, :] = tot * lax.rsqrt(ms + NORM_EPS) * gain_ref[...] * _silu(g)
        return carry

    lax.fori_loop(0, n_all * c_sz // ROW_TILE, readout, 0)


def _hgrn(pa3, lb, gain, n_ctx_rows):
    n_batch, rows, _ = pa3.shape
    kern = functools.partial(_hgrn_kernel, n_ctx=n_ctx_rows // CHUNK, n_all=rows // CHUNK)
    return pl.pallas_call(
        kern,
        out_shape=jax.ShapeDtypeStruct((n_batch, rows, A_W), F32),
        grid=(n_batch,),
        in_specs=[pl.BlockSpec((1, rows, PA_W), lambda b: (b, 0, 0)),
                  pl.BlockSpec((2, A_W), lambda b: (0, 0)),
                  pl.BlockSpec((1, A_W), lambda b: (0, 0))],
        out_specs=pl.BlockSpec((1, rows, A_W), lambda b: (b, 0, 0)),
        scratch_shapes=[pltpu.VMEM((2, A_W, A_W), F32), pltpu.VMEM((rows, A_W), F32)],
        compiler_params=_cparams(("arbitrary",)),
        name="hgrn2",
    )(pa3, lb, gain)


def _split3(x):
    hi = x.astype(BF16)
    rest = x - hi.astype(F32)
    mid = rest.astype(BF16)
    return hi, mid, (rest - mid.astype(F32)).astype(BF16)


def _mlstm_kernel(pb_ref, bias_ref, gain_ref, o_ref, cn_fwd_ref, cn_bwd_ref, ob_ref, *, n_ctx, n_all):
    assert CHUNK == HEAD_V
    c_sz, w, qk_w = CHUNK, B_W, HEADS * B_QK
    t_i = lax.broadcasted_iota(jnp.int32, (c_sz, w), 0)
    s_i = lax.broadcasted_iota(jnp.int32, (c_sz, w), 1) % c_sz
    diag4 = s_i == t_i
    half_lane = lax.broadcasted_iota(jnp.int32, (1, LANES), 1) < HEAD_V
    k_block = (lax.broadcasted_iota(jnp.int32, (HEADS * c_sz, qk_w), 0) // c_sz
               == lax.broadcasted_iota(jnp.int32, (HEADS * c_sz, qk_w), 1) // B_QK)
    v_block = (lax.broadcasted_iota(jnp.int32, (HEADS * c_sz, w), 0) // c_sz
               == lax.broadcasted_iota(jnp.int32, (HEADS * c_sz, w), 1) // HEAD_V)
    state_block = (lax.broadcasted_iota(jnp.int32, (qk_w, 2 * w), 0) // B_QK
                   == (lax.broadcasted_iota(jnp.int32, (qk_w, 2 * w), 1) % w) // HEAD_V)
    head_ones = v_block.astype(BF16)
    q_off, k_off, v_off, o_off, g_off = 0, qk_w, 2 * qk_w, 2 * qk_w + B_W, 2 * qk_w + 2 * B_W

    sel_r = lax.broadcasted_iota(jnp.int32, (2 * LANES, 2 * w), 0)
    sel_c = lax.broadcasted_iota(jnp.int32, (2 * LANES, 2 * w), 1)
    sel = (sel_r == jnp.where(sel_c < w, HEADS + sel_c // HEAD_V, LANES + (sel_c - w) // HEAD_V)).astype(BF16)

    def head_max(x):
        outs = []
        for col in range(w // LANES):
            xc = x[:, col * LANES:(col + 1) * LANES]
            lo = jnp.max(jnp.where(half_lane, xc, -jnp.inf), axis=-1, keepdims=True)
            hi = jnp.max(jnp.where(half_lane, -jnp.inf, xc), axis=-1, keepdims=True)
            outs.append(jnp.where(half_lane, lo, hi))
        return jnp.concatenate(outs, axis=1)

    mask2 =jnp.concatenate([s_i <= t_i, s_i >= t_i], axis=0)
    r2 = lax.broadcasted_iota(jnp.int32, (2 * c_sz, 2 * c_sz), 0)
    c2 = lax.broadcasted_iota(jnp.int32, (2 * c_sz, 2 * c_sz), 1)
    tri2 = jnp.logical_or(jnp.logical_and(r2 < c_sz, c2 <= r2),
                          jnp.logical_and(r2 >= c_sz, c2 >= r2)).astype(BF16)

    def per_half(fn, x):
        return jnp.concatenate([jnp.broadcast_to(fn(x[0:c_sz]), (c_sz, w)),
                                jnp.broadcast_to(fn(x[c_sz:2 * c_sz]), (c_sz, w))], axis=0)

    def rows_of(row_f, row_b):
        return jnp.concatenate([jnp.broadcast_to(row_f, (c_sz, w)), jnp.broadcast_to(row_b, (c_sz, w))], axis=0)

    def load(c_f, c_b, lo, hi):
        rf = pl.ds(pl.multiple_of(c_f * c_sz, c_sz), c_sz)
        rb = pl.ds(pl.multiple_of(c_b * c_sz, c_sz), c_sz)
        return pb_ref[0, rf, lo:hi], pb_ref[0, rb, lo:hi]

    def scan_step(n, carry):
        m_f, m_b = carry
        c_f = n
        c_b = jnp.where(n < n_ctx, n_ctx - 1 - n, n_all - 1 - (n - n_ctx))
        q_f, q_b = load(c_f, c_b, q_off, q_off + qk_w)
        k_f, k_b = load(c_f, c_b, k_off, k_off + qk_w)
        v_f, v_b = load(c_f, c_b, v_off, v_off + B_W)
        g_f, g_b = load(c_f, c_b, g_off, g_off + LANES)
        gts = jnp.concatenate([g_f + bias_ref[...],
                               pltpu.roll(g_b + bias_ref[...], LANES - 2 * HEADS, 1)], axis=0)
        log_f = jnp.minimum(gts, 0.0) - jnp.log(1.0 + jnp.exp(-jnp.abs(gts)))
        cum_col = sum(jnp.dot(tri2, p, preferred_element_type=F32) for p in _split3(log_f))
        both = sum(jnp.dot(p, sel, preferred_element_type=F32)
                   for p in _split3(jnp.concatenate([cum_col, gts], axis=1)))
        cum_t = both[:, :w]
        ig_t = both[:, w:]
        src = per_half(lambda x: jnp.sum(jnp.where(diag4, x, 0.0), axis=0, keepdims=True), cum_t - ig_t)
        m_prev = rows_of(m_f, m_b)
        log_d = jnp.where(mask2, cum_t - src, MASK_NEG)
        log_inter = cum_t + m_prev
        m_t = jnp.maximum(log_inter, head_max(log_d))
        w_intra = jnp.where(mask2, jnp.exp(log_d - m_t), 0.0)
        w_inter = jnp.exp(log_inter - m_t)

        scale = B_QK ** -0.5
        qb_f, qb_b = (q_f * scale).astype(BF16), (q_b * scale).astype(BF16)
        kb_f, kb_b = k_f.astype(BF16), k_b.astype(BF16)
        zero = jnp.zeros((), BF16)

        def scores(qb, kb):
            return _nt(qb, jnp.where(k_block, jnp.concatenate([kb] * HEADS, axis=0), zero))

        p = (jnp.concatenate([scores(qb_f, kb_f), scores(qb_b, kb_b)], axis=0) * w_intra).astype(BF16)

        def values(p_half, v_half):
            v_bd = jnp.where(v_block, jnp.concatenate([v_half.astype(BF16)] * HEADS, axis=0), zero)
            return jnp.dot(p_half, v_bd, preferred_element_type=F32)

        cn_f = cn_fwd_ref[...]
        cn_b = cn_bwd_ref[...]
        inter = jnp.concatenate([jnp.dot(qb_f, cn_f.astype(BF16), preferred_element_type=F32),
                                 jnp.dot(qb_b, cn_b.astype(BF16), preferred_element_type=F32)], axis=0)
        num = w_inter * inter[:, :w] + jnp.concatenate(
            [values(p[0:c_sz], v_f), values(p[c_sz:2 * c_sz], v_b)], axis=0)
        den = w_inter * inter[:, w:] + jnp.dot(p, head_ones, preferred_element_type=F32)
        h_out = num / jnp.maximum(jnp.abs(den), jnp.exp(-m_t))
        o_ref[0, pl.ds(pl.multiple_of(c_f * c_sz, c_sz), c_sz), :] = h_out[0:c_sz]
        ob_ref[pl.ds(pl.multiple_of(c_b * c_sz, c_sz), c_sz), :] = h_out[c_sz:2 * c_sz]

        end_f = cum_t[c_sz - 1:c_sz, :]
        end_b = cum_t[c_sz:c_sz + 1, :]
        log_end = rows_of(end_f, end_b) - cum_t + ig_t
        m_end_f = jnp.maximum(end_f + m_f, jnp.max(log_end[0:c_sz], axis=0, keepdims=True))
        m_end_b = jnp.maximum(end_b + m_b, jnp.max(log_end[c_sz:2 * c_sz], axis=0, keepdims=True))
        w_end = jnp.exp(log_end - rows_of(m_end_f, m_end_b))

        def update(cn_ref, cn, kb, w_half, v_half, carry_w):
            upd = _tn(kb, jnp.concatenate([(w_half * v_half).astype(BF16), w_half.astype(BF16)], axis=1))
            cn_ref[...] = jnp.concatenate([carry_w, carry_w], axis=1) * cn + jnp.where(state_block, upd, 0.0)

        update(cn_fwd_ref, cn_f, kb_f, w_end[0:c_sz], v_f, jnp.exp(end_f + m_f - m_end_f))
        update(cn_bwd_ref, cn_b, kb_b, w_end[c_sz:2 * c_sz], v_b, jnp.exp(end_b + m_b - m_end_b))
        return m_end_f, m_end_b

    zero_m = jnp.zeros((1, w), F32)
    cn_fwd_ref[...] = jnp.zeros_like(cn_fwd_ref)
    cn_bwd_ref[...] = jnp.zeros_like(cn_bwd_ref)
    lax.fori_loop(0, n_all, scan_step, (zero_m, zero_m), unroll=2)

    def readout(c, carry):
        rows = pl.ds(pl.multiple_of(c * ROW_TILE, ROW_TILE), ROW_TILE)
        tot = o_ref[0, rows, :] + ob_ref[rows, :]
        normed = jnp.concatenate(
            [_rms(tot[:, h * HEAD_V:(h + 1) * HEAD_V]) for h in range(HEADS)], axis=1) * gain_ref[...]
        og = pb_ref[0, rows, o_off:o_off + B_W]
        o_ref[0, rows, :] = normed * jax.nn.sigmoid(og)
        return carry

    lax.fori_loop(0, n_all * c_sz // ROW_TILE, readout, 0)


def _mlstm(pb3, bias, gain, n_ctx_rows):
    n_batch, rows, _ = pb3.shape
    kern = functools.partial(_mlstm_kernel, n_ctx=n_ctx_rows // CHUNK, n_all=rows // CHUNK)
    return pl.pallas_call(
        kern,
        out_shape=jax.ShapeDtypeStruct((n_batch, rows, B_W), F32),
        grid=(n_batch,),
        in_specs=[pl.BlockSpec((1, rows, PB_W), lambda b: (b, 0, 0)),
                  pl.BlockSpec((1, LANES), lambda b: (0, 0)),
                  pl.BlockSpec((1, B_W), lambda b: (0, 0))],
        out_specs=pl.BlockSpec((1, rows, B_W), lambda b: (b, 0, 0)),
        scratch_shapes=[pltpu.VMEM((HEADS * B_QK, 2 * B_W), F32), pltpu.VMEM((HEADS * B_QK, 2 * B_W), F32),
                        pltpu.VMEM((rows, B_W), F32)],
        compiler_params=_cparams(("arbitrary",)),
        name="mlstm",
    )(pb3, bias, gain)


def _attn_kernel(lam_ref, qp_ref, qr_ref, k_ref, v_ref, gain_ref, o_ref, *, n_ctx, q_tile0, lam_init):
    lam = lam_ref[0]
    q_tile = pl.program_id(2) + q_tile0
    lane = lax.broadcasted_iota(jnp.int32, (1, 2 * C_DQK), 1)

    def finish(parts):
        o = parts[0] - lam * parts[1]
        o_ref[0] = _rms(o) * gain_ref[...] * (1.0 - lam_init)

    def sub_query(q, j):
        return jnp.where(lane // C_DQK == j, q, jnp.zeros_like(q))

    def row_max(s):
        return jnp.max(s, axis=-1, keepdims=True)

    def row_sum(s):
        return jnp.sum(s, axis=-1, keepdims=True)

    def pv(ex, v):
        return jnp.dot(ex.astype(BF16), v, preferred_element_type=F32)

    @pl.when(q_tile * ROW_TILE < n_ctx)
    def _():
        kc = k_ref[0, 0:n_ctx, :]
        vc = v_ref[0, 0:n_ctx, :]
        parts = []
        for j in range(2):
            s = _nt(sub_query(qp_ref[0], j), kc)
            ex = jnp.exp2(s - row_max(s))
            parts.append(pv(ex, vc) / row_sum(ex))
        finish(parts)

    @pl.when(q_tile * ROW_TILE >= n_ctx)
    def _():
        n_all = k_ref.shape[1]
        kc = k_ref[0, 0:n_ctx, :]
        kl = k_ref[0, n_ctx:n_all, :]
        vc = v_ref[0, 0:n_ctx, :]
        vl = v_ref[0, n_ctx:n_all, :]
        parts = []
        for j in range(2):
            s_c = _nt(sub_query(qp_ref[0], j), kc)
            s_l = _nt(sub_query(qr_ref[0], j), kl)
            m = jnp.maximum(row_max(s_c), row_max(s_l))
            e_c = jnp.exp2(s_c - m)
            e_l = jnp.exp2(s_l - m)
            parts.append((pv(e_c, vc) + pv(e_l, vl)) / (row_sum(e_c) + row_sum(e_l)))
        finish(parts)


def _attention(pc3, lam, gain, n_ctx_rows, lam_init, with_ctx):
    n_batch, rows, _ = pc3.shape
    q_tile0 = 0 if with_ctx else n_ctx_rows // ROW_TILE
    n_q = rows // ROW_TILE - q_tile0
    hb = C_W // LANES
    kern = functools.partial(_attn_kernel, n_ctx=n_ctx_rows, q_tile0=q_tile0, lam_init=lam_init)
    grid_spec = pltpu.PrefetchScalarGridSpec(
        num_scalar_prefetch=1,
        grid=(n_batch, HEADS, n_q),
        in_specs=[pl.BlockSpec((1, ROW_TILE, LANES), lambda b, h, i, lam: (b, i + q_tile0, h)),
                  pl.BlockSpec((1, ROW_TILE, LANES), lambda b, h, i, lam: (b, i + q_tile0, hb + h)),
                  pl.BlockSpec((1, rows, LANES), lambda b, h, i, lam: (b, 0, 2 * hb + h)),
                  pl.BlockSpec((1, rows, LANES), lambda b, h, i, lam: (b, 0, 3 * hb + h)),
                  pl.BlockSpec((1, LANES), lambda b, h, i, lam: (0, 0))],
        out_specs=pl.BlockSpec((1, ROW_TILE, LANES), lambda b, h, i, lam: (b, i, h)),
    )
    return pl.pallas_call(
        kern,
        out_shape=jax.ShapeDtypeStruct((n_batch, n_q * ROW_TILE, C_W), F32),
        grid_spec=grid_spec,
        compiler_params=_cparams(("arbitrary", "arbitrary", "arbitrary")),
        name="diff_attn",
    )(lam, pc3, pc3, pc3, pc3, gain)


def _outproj_kernel(oa_ref, ob_ref, oc_ref, x_ref, mod_ref, n_ref, w_ref, rw_ref, rb_ref,
                    xn_ref, h_ref, route_ref, cnt_ref, carry_ref):
    d = x_ref.shape[1]
    n_exp = rw_ref.shape[1]

    @pl.when(pl.program_id(0) == 0)
    def _():
        carry_ref[...] = jnp.zeros_like(carry_ref)

    mix = (jnp.dot(oa_ref[...].astype(BF16), w_ref[0:A_W, :], preferred_element_type=F32)
           + jnp.dot(ob_ref[...].astype(BF16), w_ref[A_W:A_W + B_W, :], preferred_element_type=F32)
           + jnp.dot(oc_ref[...].astype(BF16), w_ref[A_W + B_W:A_W + B_W + C_W, :], preferred_element_type=F32))
    xn = x_ref[...] + mod_ref[0, :, 2 * d:3 * d] * (_rms(mix) * n_ref[1:2, :])
    xn_ref[...] = xn
    h = _rms(xn) * n_ref[2:3, :] * (1.0 + mod_ref[0, :, 4 * d:5 * d]) + mod_ref[0, :, 3 * d:4 * d]
    h_ref[...] = _pack_bf16_pairs(h)

    h_hi = h.astype(BF16)
    h_mid = (h - h_hi.astype(F32)).astype(BF16)
    logits = _nt(rw_ref[0], h_hi) + _nt(rw_ref[1], h_hi) + _nt(rw_ref[0], h_mid) + rb_ref[...]
    e_sub = lax.broadcasted_iota(jnp.int32, logits.shape, 0)
    cur = logits
    picks, vals = [], []
    for _ in range(TOP_K):
        mx = jnp.max(cur, axis=0, keepdims=True)
        idx = jnp.min(jnp.where(cur == mx, e_sub, n_exp), axis=0, keepdims=True)
        hit = e_sub == idx
        cur = jnp.where(hit, -jnp.inf, cur)
        picks.append((idx, hit.astype(F32)))
        vals.append(mx)
    exps = [jnp.exp(vv - vals[0]) for vv in vals]
    total = exps[0] + exps[1] + exps[2] + exps[3]

    chosen = picks[0][1] + picks[1][1] + picks[2][1] + picks[3][1]
    tm = logits.shape[1]
    before = (lax.broadcasted_iota(jnp.int32, (tm, tm), 0) < lax.broadcasted_iota(jnp.int32, (tm, tm), 1))
    seen = jnp.dot(chosen.astype(BF16), before.astype(BF16), preferred_element_type=F32) + carry_ref[...]
    new_carry = carry_ref[...] + jnp.sum(chosen, axis=1, keepdims=True)
    carry_ref[...] = new_carry
    cnt_ref[...] = new_carry

    o_row = lax.broadcasted_iota(jnp.int32, route_ref.shape, 0)
    route = jnp.zeros(route_ref.shape, F32)
    for kk in range(TOP_K):
        idx, hit = picks[kk]
        rank = jnp.sum(hit * seen, axis=0, keepdims=True)
        route = (route + jnp.where(o_row == kk, idx.astype(F32), 0.0)
                 + jnp.where(o_row == TOP_K + kk, exps[kk] / total, 0.0)
                 + jnp.where(o_row == 2 * TOP_K + kk, rank, 0.0))
    route_ref[...] = route


ROUTE_ROWS = 16


def _outproj(oa, ob, oc, xy, mod, norms, w_out_b, router_wt, router_b, layer, n_tiles, in_tile, oc_tile, mod_row):
    d = xy.shape[1]
    n_exp = router_wt.shape[2]
    t_out = n_tiles * ROW_TILE
    xn, h, route_t, counts = pl.pallas_call(
        _outproj_kernel,
        out_shape=(jax.ShapeDtypeStruct((t_out, d), F32),
                   jax.ShapeDtypeStruct((t_out, d // 2), jnp.int32),
                   jax.ShapeDtypeStruct((n_tiles * ROUTE_ROWS, ROW_TILE), F32),
                   jax.ShapeDtypeStruct((n_exp, 1), F32)),
        grid=(n_tiles,),
        in_specs=[pl.BlockSpec((ROW_TILE, A_W), lambda i: (in_tile(i), 0)),
                  pl.BlockSpec((ROW_TILE, B_W), lambda i: (in_tile(i), 0)),
                  pl.BlockSpec((ROW_TILE, C_W), lambda i: (oc_tile(i), 0)),
                  pl.BlockSpec((ROW_TILE, d), lambda i: (in_tile(i), 0)),
                  pl.BlockSpec((1, 1, 6 * d), lambda i: (mod_row(i), 0, 0)),
                  pl.BlockSpec((None, 4, d), lambda i: (layer, 0, 0)),
                  pl.BlockSpec((None, d, d), lambda i: (layer, 0, 0)),
                  pl.BlockSpec((None, 2, n_exp, d), lambda i: (layer, 0, 0, 0)),
                  pl.BlockSpec((None, n_exp, 1), lambda i: (layer, 0, 0))],
        out_specs=(pl.BlockSpec((ROW_TILE, d), lambda i: (i, 0)),
                   pl.BlockSpec((ROW_TILE, d // 2), lambda i: (i, 0)),
                   pl.BlockSpec((ROUTE_ROWS, ROW_TILE), lambda i: (i, 0)),
                   pl.BlockSpec((n_exp, 1), lambda i: (0, 0))),
        scratch_shapes=[pltpu.VMEM((n_exp, 1), F32)],
        compiler_params=_cparams(("arbitrary",)),
        name="outproj_router",
    )(oa, ob, oc, xy, mod, norms, w_out_b, router_wt, router_b)
    return xn, h, route_t.reshape(n_tiles, ROUTE_ROWS, ROW_TILE), counts[:, 0]


PAIR_BLOCK = 2 * LANES


def _moe_kernel(te_ref, first_ref, valid_ref, nu_ref, slot_ref, next_ref, xs_ref, w1_hbm, b1_ref, w2_hbm, b2_ref,
                ys_ref, w1p_ref, w2b_ref, w1_buf, w2_buf, w_sem, *, layer):
    i = pl.program_id(0)
    two_f = w1_buf.shape[2]
    n_blk = two_f // PAIR_BLOCK

    def fetch(expert, slot):
        return (pltpu.make_async_copy(w1_hbm.at[layer, expert], w1_buf.at[slot], w_sem.at[0, slot]),
                pltpu.make_async_copy(w2_hbm.at[layer, expert], w2_buf.at[slot], w_sem.at[1, slot]))

    @pl.when(i == 0)
    def _():
        for copy in fetch(te_ref[0], 0):
            copy.start()

    @pl.when(jnp.logical_and(i < nu_ref[0], first_ref[i] == 1))
    def _():
        slot = slot_ref[i]
        for copy in fetch(te_ref[i], slot):
            copy.wait()
        r = lax.broadcasted_iota(jnp.int32, (PAIR_BLOCK, PAIR_BLOCK), 0)
        c = lax.broadcasted_iota(jnp.int32, (PAIR_BLOCK, PAIR_BLOCK), 1)
        perm = (r == jnp.where(c < LANES, 2 * c, 2 * (c - LANES) + 1)).astype(BF16)
        for blk in range(n_blk):
            cols = slice(blk * PAIR_BLOCK, (blk + 1) * PAIR_BLOCK)
            w1p_ref[:, cols] = jnp.dot(w1_buf[slot, :, cols].astype(BF16), perm,
                                       preferred_element_type=F32).astype(BF16)
        w2b_ref[...] = w2_buf[slot].astype(BF16)

        @pl.when(next_ref[i] >= 0)
        def _():
            for copy in fetch(next_ref[i], 1 - slot):
                copy.start()

    tile_rows = xs_ref.shape[0]
    used = i < nu_ref[0]
    valid = valid_ref[i]

    def expert_ffn(n_rows):
        x = _unpack_bf16_pairs(xs_ref[0:n_rows, :]).astype(BF16)
        row = lax.broadcasted_iota(jnp.int32, (n_rows, 1), 0)
        x = jnp.where(row < valid, x, jnp.zeros_like(x))
        hid = jnp.dot(x, w1p_ref[...], preferred_element_type=F32) + b1_ref[...]
        acts = []
        for blk in range(n_blk):
            glu = jnp.minimum(hid[:, blk * PAIR_BLOCK:blk * PAIR_BLOCK + LANES], SWIGLU_LIMIT)
            lin = jnp.clip(hid[:, blk * PAIR_BLOCK + LANES:(blk + 1) * PAIR_BLOCK], -SWIGLU_LIMIT, SWIGLU_LIMIT)
            acts.append((glu * jax.nn.sigmoid(SWIGLU_ALPHA * glu) * (lin + 1.0)).astype(BF16))
        y = jnp.dot(jnp.concatenate(acts, axis=1), w2b_ref[...], preferred_element_type=F32) + b2_ref[...]
        ys_ref[0:n_rows, :] = _pack_bf16_pairs(y)
        if n_rows < tile_rows:
            ys_ref[n_rows:tile_rows, :] = jnp.zeros((tile_rows - n_rows, ys_ref.shape[1]), ys_ref.dtype)

    @pl.when(jnp.logical_and(used, valid > tile_rows // 2))
    def _():
        expert_ffn(tile_rows)

    @pl.when(jnp.logical_and(used, valid <= tile_rows // 2))
    def _():
        expert_ffn(tile_rows // 2)

    @pl.when(jnp.logical_not(used))
    def _():
        ys_ref[...] = jnp.zeros_like(ys_ref)


def _moe(plan, xs, w1, b1p, w2, b2, layer):
    r_max, half_d = xs.shape
    d = 2 * half_d
    two_f = w1.shape[3]
    f = two_f // 2
    n_tiles = r_max // MOE_TILE

    def row_tile(i, te, fi, va, nu, sl, nx):
        return (jnp.maximum(jnp.minimum(i, nu[0] - 1), 0), 0)

    def bias_block(i, te, fi, va, nu, sl, nx):
        return (layer, te[i], 0, 0)

    grid_spec = pltpu.PrefetchScalarGridSpec(
        num_scalar_prefetch=6,
        grid=(n_tiles,),
        in_specs=[pl.BlockSpec((MOE_TILE, half_d), row_tile),
                  pl.BlockSpec(memory_space=pl.ANY),
                  pl.BlockSpec((None, None, 1, two_f), bias_block),
                  pl.BlockSpec(memory_space=pl.ANY),
                  pl.BlockSpec((None, None, 1, d), bias_block)],
        out_specs=pl.BlockSpec((MOE_TILE, half_d), lambda i, te, fi, va, nu, sl, nx: (i, 0)),
        scratch_shapes=[pltpu.VMEM((d, two_f), BF16), pltpu.VMEM((f, d), BF16),
                        pltpu.VMEM((2, d, two_f), F32), pltpu.VMEM((2, f, d), F32),
                        pltpu.SemaphoreType.DMA((2, 2))],
    )
    return pl.pallas_call(
        functools.partial(_moe_kernel, layer=layer),
        out_shape=jax.ShapeDtypeStruct((r_max, half_d), jnp.int32),
        grid_spec=grid_spec,
        compiler_params=_cparams(("arbitrary",)),
        name="moe_experts",
    )(*plan, xs, w1, b1p, w2, b2)


SC_CORES = 2
SC_SUBCORES = 16
SC_CHUNK = 64


def _dispatch_rows(hp, pos, r_max):
    t, width = hp.shape
    workers = SC_CORES * SC_SUBCORES
    assert t % (workers * SC_CHUNK) == 0
    per_worker = t // (workers * SC_CHUNK)
    idx = pos.reshape(-1, TOP_K, ROW_TILE // SC_CHUNK, SC_CHUNK).transpose(0, 2, 1, 3).reshape(
        workers, per_worker * TOP_K, SC_CHUNK)
    mesh = plsc.VectorSubcoreMesh(core_axis_name="c", subcore_axis_name="s",
                                  num_cores=SC_CORES, num_subcores=SC_SUBCORES)

    @functools.partial(
        pl.kernel, mesh=mesh,
        out_type=jax.ShapeDtypeStruct((r_max, width), hp.dtype),
        scratch_types=[pltpu.VMEM((per_worker * TOP_K, SC_CHUNK), jnp.int32),
                       pltpu.VMEM((2, SC_CHUNK, width), hp.dtype),
                       pltpu.SemaphoreType.DMA((2,)),
                       pltpu.SemaphoreType.DMA((2,))],
    )
    def scatter(hp_hbm, idx_hbm, out_hbm, idx_v, rows_v, read_sem, write_sem):
        wid = lax.axis_index("s") * SC_CORES + lax.axis_index("c")
        pltpu.sync_copy(idx_hbm.at[wid], idx_v)

        def read(j):
            row0 = pl.multiple_of((wid * per_worker + j) * SC_CHUNK, SC_CHUNK)
            return pltpu.make_async_copy(hp_hbm.at[pl.ds(row0, SC_CHUNK)], rows_v.at[j % 2], read_sem.at[j % 2])

        def write(j, k):
            return pltpu.make_async_copy(rows_v.at[j % 2], out_hbm.at[idx_v.at[j * TOP_K + k]],
                                         write_sem.at[j % 2])

        read(0).start()
        for j in range(per_worker):
            read(j).wait()
            for k in range(TOP_K):
                write(j, k).start()
            if j + 1 < per_worker:
                if j >= 1:
                    for k in range(TOP_K):
                        write(j - 1, k).wait()
                read(j + 1).start()
        for j in range(max(per_worker - 2, 0), per_worker):
            for k in range(TOP_K):
                write(j, k).wait()

    return scatter(hp, idx)


def _combine_rows(ysp, pos):
    width = ysp.shape[1]
    t = pos.shape[0] * pos.shape[2]
    workers = SC_CORES * SC_SUBCORES
    assert t % (workers * SC_CHUNK) == 0
    per_worker = t // (workers * SC_CHUNK)
    units = per_worker * TOP_K
    idx = pos.reshape(-1, TOP_K, ROW_TILE // SC_CHUNK, SC_CHUNK).transpose(0, 2, 1, 3).reshape(
        workers, units, SC_CHUNK)
    mesh = plsc.VectorSubcoreMesh(core_axis_name="c", subcore_axis_name="s",
                                  num_cores=SC_CORES, num_subcores=SC_SUBCORES)

    @functools.partial(
        pl.kernel, mesh=mesh,
        out_type=jax.ShapeDtypeStruct((TOP_K, t, width), ysp.dtype),
        scratch_types=[pltpu.VMEM((units, SC_CHUNK), jnp.int32),
                       pltpu.VMEM((2, SC_CHUNK, width), ysp.dtype),
                       pltpu.SemaphoreType.DMA((2,)),
                       pltpu.SemaphoreType.DMA((2,))],
    )
    def gather(ys_hbm, idx_hbm, out_hbm, idx_v, rows_v, read_sem, write_sem):
        wid = lax.axis_index("s") * SC_CORES + lax.axis_index("c")
        pltpu.sync_copy(idx_hbm.at[wid], idx_v)

        def read(u):
            return pltpu.make_async_copy(ys_hbm.at[idx_v.at[u]], rows_v.at[u % 2], read_sem.at[u % 2])

        def write(u):
            row0 = pl.multiple_of((wid * per_worker + u // TOP_K) * SC_CHUNK, SC_CHUNK)
            return pltpu.make_async_copy(rows_v.at[u % 2], out_hbm.at[u % TOP_K, pl.ds(row0, SC_CHUNK)],
                                         write_sem.at[u % 2])

        read(0).start()
        for u in range(units):
            read(u).wait()
            write(u).start()
            if u + 1 < units:
                if u >= 1:
                    write(u - 1).wait()
                read(u + 1).start()
        for u in range(max(units - 2, 0), units):
            write(u).wait()

    return gather(ysp, idx)


def _ffn_residual_kernel(x_ref, w_ref, y0_ref, y1_ref, y2_ref, y3_ref, mod_ref, n_ref, o_ref):
    d = x_ref.shape[1]
    wts = w_ref[...]
    ffn = None
    for k, y_ref in enumerate((y0_ref, y1_ref, y2_ref, y3_ref)):
        term = wts[:, k:k + 1] * _unpack_bf16_pairs(y_ref[...])
        ffn = term if ffn is None else ffn + term
    o_ref[...] = x_ref[...] + mod_ref[0, :, 5 * d:6 * d] * (_rms(ffn) * n_ref[3:4, :])


def _ffn_residual(xn, wts, ys_by_k, mod, norms, layer, mod_row):
    t, d = xn.shape
    row_spec = pl.BlockSpec((ROW_TILE, d), lambda i: (i, 0))

    def y_spec(k):
        return pl.BlockSpec((None, ROW_TILE, d // 2), lambda i: (k, i, 0))

    return pl.pallas_call(
        _ffn_residual_kernel,
        out_shape=jax.ShapeDtypeStruct((t, d), F32),
        grid=(t // ROW_TILE,),
        in_specs=[row_spec,
                  pl.BlockSpec((ROW_TILE, TOP_K), lambda i: (i, 0)),
                  y_spec(0), y_spec(1), y_spec(2), y_spec(3),
                  pl.BlockSpec((1, 1, 6 * d), lambda i: (mod_row(i), 0, 0)),
                  pl.BlockSpec((None, 4, d), lambda i: (layer, 0, 0))],
        out_specs=row_spec,
        compiler_params=_cparams(("arbitrary",)),
        name="ffn_residual",
    )(xn, wts, ys_by_k, ys_by_k, ys_by_k, ys_by_k, mod, norms)


def _dispatch_plan(route_t, counts, r_max):
    n_row_tiles = route_t.shape[0]
    idx = route_t[:, 0:TOP_K, :].astype(jnp.int32)
    wts = route_t[:, TOP_K:2 * TOP_K, :].transpose(0, 2, 1).reshape(n_row_tiles * ROW_TILE, TOP_K)
    rank = route_t[:, 2 * TOP_K:3 * TOP_K, :].astype(jnp.int32)
    cnt = counts.astype(jnp.int32)
    padded = (cnt + MOE_TILE - 1) // MOE_TILE * MOE_TILE
    ends = jnp.cumsum(padded)
    starts = ends - padded
    pos = rank
    for e in range(cnt.shape[0]):
        pos = pos + jnp.where(idx == e, starts[e], 0)
    n_tiles = r_max // MOE_TILE
    n_used = ends[-1] // MOE_TILE
    tile_ids = jnp.minimum(jnp.arange(n_tiles, dtype=jnp.int32), n_used - 1)
    tile_expert = jnp.sum((ends // MOE_TILE)[None, :] <= tile_ids[:, None], axis=1).astype(jnp.int32)
    tile_first = jnp.concatenate(
        [jnp.ones((1,), jnp.int32), (tile_expert[1:] != tile_expert[:-1]).astype(jnp.int32)])
    of_expert = (tile_expert[:, None] == jnp.arange(cnt.shape[0], dtype=jnp.int32)[None, :]).astype(jnp.int32)
    first_tile = jnp.sum(of_expert * ((ends - padded) // MOE_TILE)[None, :], axis=1)
    tile_valid = jnp.clip(jnp.sum(of_expert * cnt[None, :], axis=1) - (tile_ids - first_tile) * MOE_TILE,
                          0, MOE_TILE).astype(jnp.int32)
    experts = jnp.arange(cnt.shape[0], dtype=jnp.int32)
    has_rows = cnt > 0
    tile_slot = (jnp.sum(of_expert * (jnp.cumsum(has_rows.astype(jnp.int32)) - 1)[None, :], axis=1) % 2).astype(jnp.int32)
    later_with_rows = jnp.logical_and(has_rows[None, :], experts[None, :] > experts[:, None])
    next_expert = jnp.min(jnp.where(later_with_rows, experts[None, :], cnt.shape[0]), axis=1)
    next_expert = jnp.where(next_expert < cnt.shape[0], next_expert, -1)
    tile_next = jnp.sum(of_expert * next_expert[None, :], axis=1).astype(jnp.int32)
    plan = (tile_expert, tile_first, tile_valid, n_used.reshape(1).astype(jnp.int32), tile_slot, tile_next)
    return pos, wts, plan


def _rope_tables(length, n_ctx_rows):
    rows = length // GRID_W
    row = jnp.repeat(jnp.arange(rows, dtype=F32), GRID_W)
    col = jnp.tile(jnp.arange(GRID_W, dtype=F32), rows)
    n_freq = C_DQK // 4
    inv_freq = ROPE_THETA ** (-jnp.arange(n_freq, dtype=F32) / n_freq)
    ang_r = row[:, None] * inv_freq
    ang_c = col[:, None] * inv_freq
    cos = jnp.concatenate([jnp.cos(ang_r), jnp.cos(ang_r), jnp.cos(ang_c), jnp.cos(ang_c)], axis=-1)
    sin = jnp.concatenate([-jnp.sin(ang_r), jnp.sin(ang_r), -jnp.sin(ang_c), jnp.sin(ang_c)], axis=-1)
    reps = C_W // C_DQK
    cos = jnp.concatenate([jnp.ones((n_ctx_rows, C_W), F32), jnp.tile(cos, (1, reps))], axis=0)
    sin = jnp.concatenate([jnp.zeros((n_ctx_rows, C_W), F32), jnp.tile(sin, (1, reps))], axis=0)
    return cos, sin


def _hgrn_lower_bound(table, layer):
    p = jax.nn.softmax(table.astype(F32), axis=1)
    cum = jnp.cumsum(p, axis=1) - p[:, :1]
    return jnp.clip(cum[:, layer], 0.0, 1.0)


def kernel(x, c, ctx, c_ctx, ada_w, ada_b, sandwich_norms, w_in, w_out, hgrn_lower_bounds, hgrn_norm,
           mlstm_gate_bias, mlstm_norm, diff_lambdas, diff_norm, router_w, router_b, moe_w1, moe_b1,
           moe_w2, moe_b2):
    n_batch, seq, d = x.shape
    n_ctx_rows = ctx.shape[1]
    depth = w_in.shape[0]
    n_exp = router_w.shape[2]
    assert seq % ROW_TILE == 0 and n_ctx_rows % ROW_TILE == 0 and seq % GRID_W == 0
    rows_b = n_ctx_rows + seq
    tiles_b = rows_b // ROW_TILE
    ctx_tiles = n_ctx_rows // ROW_TILE
    lat_tiles = seq // ROW_TILE

    b_main = PA_W + 2 * HEADS * B_QK + 2 * B_W
    w_in_p = jnp.concatenate(
        [w_in[:, :, :b_main],
         jnp.pad(w_in[:, :, b_main:b_main + N_GATE], ((0, 0), (0, 0), (0, LANES - N_GATE))),
         w_in[:, :, b_main + N_GATE:]], axis=2).astype(BF16)
    w_out_b = w_out.astype(BF16)
    two_f = moe_b1.shape[2]
    b1p = moe_b1.reshape(depth, n_exp, two_f // PAIR_BLOCK, LANES, 2).transpose(0, 1, 2, 4, 3).reshape(
        depth, n_exp, 1, two_f)
    b2 = moe_b2[:, :, None, :]
    ada_b3 = ada_b[:, None, :]
    router_b3 = router_b[:, :, None]
    rw_t = router_w.transpose(0, 2, 1)
    rw_hi = rw_t.astype(BF16)
    router_wt = jnp.stack([rw_hi, (rw_t - rw_hi.astype(F32)).astype(BF16)], axis=1)
    gate_bias = jnp.pad(mlstm_gate_bias, ((0, 0), (0, LANES - N_GATE)))
    cos, sin = _rope_tables(seq, n_ctx_rows)

    cond_rows = (n_batch + 1 + 7) // 8 * 8
    cvec = jnp.zeros((cond_rows, d), F32).at[:n_batch].set(c).at[n_batch].set(c_ctx)

    def mod_row_all(i):
        return jnp.where(i % tiles_b < ctx_tiles, n_batch, i // tiles_b)

    xy = jnp.concatenate([ctx, x], axis=1).reshape(n_batch * rows_b, d)

    for layer in range(depth):
        last = layer == depth - 1
        mod = _adaln(cvec, ada_w, ada_b3, layer).reshape(cond_rows, 1, 6 * d)
        lb = _hgrn_lower_bound(hgrn_lower_bounds, layer)
        lam_init = 0.8 - 0.6 * math.exp(-0.3 * layer)
        lq1, lk1, lq2, lk2 = diff_lambdas[layer].astype(F32)
        lam = (jnp.exp(jnp.sum(lq1 * lk1)) - jnp.exp(jnp.sum(lq2 * lk2)) + lam_init).reshape(1)

        pa, pb, pc = _inproj(xy, mod, sandwich_norms, w_in_p, cos, sin, layer, tiles_b, ctx_tiles, n_batch)
        oa = _hgrn(pa.reshape(n_batch, rows_b, PA_W), lb, jnp.tile(hgrn_norm[layer], HEADS)[None, :], n_ctx_rows)
        ob = _mlstm(pb.reshape(n_batch, rows_b, PB_W), gate_bias[layer][None, :], mlstm_norm[layer][None, :],
                    n_ctx_rows)
        oc = _attention(pc.reshape(n_batch, rows_b, PC_OUT), lam, diff_norm[layer][None, :], n_ctx_rows,
                        lam_init, not last)

        if last:
            n_tiles = n_batch * lat_tiles
            in_tile = lambda i: (i // lat_tiles) * tiles_b + ctx_tiles + i % lat_tiles
            mod_row = lambda i: i // lat_tiles
        else:
            n_tiles = n_batch * tiles_b
            in_tile = lambda i: i
            mod_row = mod_row_all
        xn, h, route, counts = _outproj(
            oa.reshape(-1, A_W), ob.reshape(-1, B_W), oc.reshape(-1, C_W), xy, mod, sandwich_norms, w_out_b,
            router_wt, router_b3, layer, n_tiles, in_tile, lambda i: i, mod_row)

        r_max = n_tiles * ROW_TILE * TOP_K + n_exp * MOE_TILE
        pos, wts, plan = _dispatch_plan(route, counts, r_max)
        xs = _dispatch_rows(h, pos, r_max)
        ys = _moe(plan, xs, moe_w1, b1p, moe_w2, b2, layer)
        xy = _ffn_residual(xn, wts, _combine_rows(ys, pos), mod, sandwich_norms, layer, mod_row)

    return xy.reshape(n_batch, seq, d)
```

```python
import functools
import math

import jax
import jax.numpy as jnp
from jax import lax
from jax.experimental import pallas as pl
from jax.experimental.pallas import tpu as pltpu
from jax.experimental.pallas import tpu_sc as plsc

F32 = jnp.float32
BF16 = jnp.bfloat16
HI = lax.Precision.HIGHEST

HEADS = 4
A_W = 256
B_QK = 32
B_W = 256
C_DQK = 64
C_DV = 2 * C_DQK
ATTN_KEY_SCALE = C_DQK ** -0.5 * math.log2(math.e)
C_W = 512
HEAD_V = 64
N_GATE = 16
GRID_W = 64
TOP_K = 4
SWIGLU_ALPHA = 1.702
SWIGLU_LIMIT = 7.0
ROPE_THETA = 10000.0
NORM_EPS = 1e-6
MASK_NEG = -1e30
F_MIN = 1e-12

LANES = 128
ROW_TILE = 256
CHUNK = 64
MOE_TILE = 512
ATTN_HEADS_PER_STEP = 2
VMEM_LIMIT = 56 * 1024 * 1024

PA_W = 5 * A_W
PB_W = 2 * HEADS * B_QK + 2 * B_W + LANES
PC_IN = 3 * C_W
PC_OUT = 4 * C_W
W_IN_PAD = PA_W + PB_W + PC_IN


def _cparams(sem):
    return pltpu.CompilerParams(dimension_semantics=sem, vmem_limit_bytes=VMEM_LIMIT)


def _nt(a, b):
    return lax.dot_general(a, b, (((1,), (1,)), ((), ())), preferred_element_type=F32)


def _tn(a, b, precision=None):
    return lax.dot_general(a, b, (((0,), (0,)), ((), ())), preferred_element_type=F32, precision=precision)


def _rms(x):
    return x * lax.rsqrt(jnp.mean(x * x, axis=-1, keepdims=True) + NORM_EPS)


def _silu(x):
    return x * jax.nn.sigmoid(x)


def _pack_bf16_pairs(x):
    half = x.shape[1] // 2
    bits = pltpu.bitcast(x.astype(BF16).astype(F32), jnp.uint32)
    return pltpu.bitcast((bits[:, :half] >> 16) | (bits[:, half:] & jnp.uint32(0xFFFF0000)), jnp.int32)


def _unpack_bf16_pairs(words):
    bits = pltpu.bitcast(words, jnp.uint32)
    return jnp.concatenate([pltpu.bitcast(bits << 16, F32),
                            pltpu.bitcast(bits & jnp.uint32(0xFFFF0000), F32)], axis=1)


def _adaln_kernel(c_ref, w_ref, b_ref, o_ref):
    cond = _silu(c_ref[...])
    o_ref[...] = jnp.dot(cond, w_ref[...], preferred_element_type=F32, precision=HI) + b_ref[...]


def _adaln(cvec, ada_w, ada_b, layer):
    rows, d = cvec.shape
    return pl.pallas_call(
        _adaln_kernel,
        out_shape=jax.ShapeDtypeStruct((rows, 6 * d), F32),
        grid=(6,),
        in_specs=[pl.BlockSpec((rows, d), lambda j: (0, 0)),
                  pl.BlockSpec((None, d, d), lambda j: (layer, 0, j)),
                  pl.BlockSpec((None, 1, d), lambda j: (layer, 0, j))],
        out_specs=pl.BlockSpec((rows, d), lambda j: (0, j)),
        compiler_params=_cparams(("arbitrary",)),
        name="adaln",
    )(cvec, ada_w, ada_b)


def _inproj_kernel(x_ref, mod_ref, g_ref, w_ref, cos_ref, sin_ref, pa_ref, pb_ref, pc_ref):
    d = x_ref.shape[1]
    shift = mod_ref[0, :, 0:d]
    scale = mod_ref[0, :, d:2 * d]
    h = _rms(x_ref[...]) * g_ref[0:1, :] * (1.0 + scale) + shift
    hb = h.astype(BF16)
    pa_ref[...] = jnp.dot(hb, w_ref[:, 0:PA_W], preferred_element_type=F32)
    pb_ref[...] = jnp.dot(hb, w_ref[:, PA_W:PA_W + PB_W], preferred_element_type=F32)
    pc = jnp.dot(hb, w_ref[:, PA_W + PB_W:W_IN_PAD], preferred_element_type=F32)
    q = pc[:, 0:C_W]
    k = pc[:, C_W:2 * C_W] * ATTN_KEY_SCALE
    cos = cos_ref[...]
    sin = sin_ref[...]
    lane = lax.broadcasted_iota(jnp.int32, q.shape, 1)
    first = (lane % 32) < 16

    def rope(t):
        partner = jnp.where(first, pltpu.roll(t, C_W - 16, 1), pltpu.roll(t, 16, 1))
        return t * cos + partner * sin

    pc_ref[:, 0:C_W] = q.astype(BF16)
    pc_ref[:, C_W:2 * C_W] = rope(q).astype(BF16)
    pc_ref[:, 2 * C_W:3 * C_W] = rope(k).astype(BF16)
    pc_ref[:, 3 * C_W:4 * C_W] = pc[:, 2 * C_W:3 * C_W].astype(BF16)


def _inproj(xy, mod, norms, w_in_p, cos, sin, layer, tiles_per_batch, n_ctx_tiles, n_batch):
    t_all, d = xy.shape
    n_tiles = t_all // ROW_TILE

    def mod_row(i):
        return jnp.where(i % tiles_per_batch < n_ctx_tiles, n_batch, i // tiles_per_batch)

    return pl.pallas_call(
        _inproj_kernel,
        out_shape=(jax.ShapeDtypeStruct((t_all, PA_W), F32),
                   jax.ShapeDtypeStruct((t_all, PB_W), F32),
                   jax.ShapeDtypeStruct((t_all, PC_OUT), BF16)),
        grid=(n_tiles,),
        in_specs=[pl.BlockSpec((ROW_TILE, d), lambda i: (i, 0)),
                  pl.BlockSpec((1, 1, 6 * d), lambda i: (mod_row(i), 0, 0)),
                  pl.BlockSpec((None, 4, d), lambda i: (layer, 0, 0)),
                  pl.BlockSpec((None, d, W_IN_PAD), lambda i: (layer, 0, 0)),
                  pl.BlockSpec((ROW_TILE, C_W), lambda i: (i % tiles_per_batch, 0)),
                  pl.BlockSpec((ROW_TILE, C_W), lambda i: (i % tiles_per_batch, 0))],
        out_specs=(pl.BlockSpec((ROW_TILE, PA_W), lambda i: (i, 0)),
                   pl.BlockSpec((ROW_TILE, PB_W), lambda i: (i, 0)),
                   pl.BlockSpec((ROW_TILE, PC_OUT), lambda i: (i, 0))),
        compiler_params=_cparams(("arbitrary",)),
        name="inproj",
    )(xy, mod, norms, w_in_p, cos, sin)


def _hgrn_kernel(pa_ref, lb_ref, gain_ref, o_ref, st_ref, ob_ref, *, n_ctx, n_all):
    c_sz, w = CHUNK, A_W
    same_head = (lax.broadcasted_iota(jnp.int32, (w, w), 0) // HEAD_V
                 == lax.broadcasted_iota(jnp.int32, (w, w), 1) // HEAD_V)
    head_ones = same_head.astype(BF16)
    assert c_sz == HEAD_V
    t_i = lax.broadcasted_iota(jnp.int32, (c_sz, w), 0)
    s_i = lax.broadcasted_iota(jnp.int32, (c_sz, w), 1) % c_sz
    t_row = lax.broadcasted_iota(jnp.int32, (c_sz, 1), 0)
    row8 = t_row % 8

    def grouped_rows(a, k):
        return jnp.concatenate(
            [jnp.broadcast_to(a[8 * j + k:8 * j + k + 1, :], (8, w)) for j in range(c_sz // 8)], axis=0)

    def halving_levels(rev):
        out = []
        b = c_sz // 2
        while b >= 1:
            def later(i):
                return ((i % (2 * b)) < b) if rev else ((i % (2 * b)) >= b)
            live = jnp.logical_and(t_i // (2 * b) == s_i // (2 * b),
                                   jnp.logical_and(later(t_i), jnp.logical_not(later(s_i))))
            out.append((b, live.astype(F32), jnp.where(later(t_row), 1.0, -1.0)))
            b //= 2
        return out

    def both(x_f, x_b):
        return jnp.concatenate([x_f, x_b], axis=0)

    def rows_of(row_f, row_b):
        return both(jnp.broadcast_to(row_f, (c_sz, w)), jnp.broadcast_to(row_b, (c_sz, w)))

    self_mask = both(*[(s_i == t_i).astype(F32)] * 2)
    levels = [(b, both(live_f, live_b), both(sign_f, sign_b))
              for (b, live_f, sign_f), (_, live_b, sign_b) in zip(halving_levels(False), halving_levels(True))]
    lb2 = rows_of(lb_ref[0:1, :], lb_ref[1:2, :])
    r2 = lax.broadcasted_iota(jnp.int32, (2 * c_sz, 2 * c_sz), 0)
    c2 = lax.broadcasted_iota(jnp.int32, (2 * c_sz, 2 * c_sz), 1)
    tri2 = jnp.logical_or(jnp.logical_and(r2 < c_sz, c2 <= r2),
                          jnp.logical_and(r2 >= c_sz, c2 >= r2)).astype(BF16)
    zero = jnp.zeros((), BF16)

    def boundary_rows(cum, b, rev):
        if b >= 8:
            return jnp.concatenate(
                [jnp.broadcast_to(cum[r0 + (b if rev else b - 1):r0 + (b if rev else b - 1) + 1, :], (2 * b, w))
                 for r0 in range(0, c_sz, 2 * b)], axis=0)
        ref = None
        for g in reversed(range(8 // (2 * b))):
            cand = grouped_rows(cum, g * 2 * b + (b if rev else b - 1))
            ref = cand if ref is None else jnp.where(row8 < (g + 1) * 2 * b, cand, ref)
        return ref

    def scan_step(n, carry):
        c_f = n
        c_b = jnp.where(n < n_ctx, n_ctx - 1 - n, n_all - 1 - (n - n_ctx))
        rows_f = pl.ds(pl.multiple_of(c_f * c_sz, c_sz), c_sz)
        rows_b = pl.ds(pl.multiple_of(c_b * c_sz, c_sz), c_sz)
        q_pre = both(pa_ref[0, rows_f, 0:A_W], pa_ref[0, rows_b, 0:A_W])
        v_f = pa_ref[0, rows_f, A_W:2 * A_W].astype(BF16)
        v_b = pa_ref[0, rows_b, A_W:2 * A_W].astype(BF16)
        f_pre = both(pa_ref[0, rows_f, 2 * A_W:3 * A_W], pa_ref[0, rows_b, 3 * A_W:4 * A_W])
        q = _silu(q_pre)
        f = lb2 + (1.0 - lb2) * jax.nn.sigmoid(f_pre)
        log_f = jnp.log(jnp.maximum(f, F_MIN))
        kk = (1.0 - lb2) * jax.nn.sigmoid(-f_pre)
        cum = sum(jnp.dot(tri2, piece, preferred_element_type=F32) for piece in _split3(log_f))
        end_f = cum[c_sz - 1:c_sz, :]
        end_b = cum[c_sz:c_sz + 1, :]

        st_f = st_ref[0]
        st_b = st_ref[1]
        q_in = (q * jnp.exp(cum)).astype(BF16)
        o = both(_nt(q_in[0:c_sz], st_f.astype(BF16)), _nt(q_in[c_sz:2 * c_sz], st_b.astype(BF16)))
        k_end = (kk * jnp.exp(rows_of(end_f, end_b) - cum)).astype(BF16)
        st_ref[0] = st_f * jnp.exp(end_f) + jnp.where(same_head, _tn(v_f, k_end[0:c_sz]), 0.0)
        st_ref[1] = st_b * jnp.exp(end_b) + jnp.where(same_head, _tn(v_b, k_end[c_sz:2 * c_sz]), 0.0)

        def scores(qa, ka):
            return _nt(qa, jnp.where(same_head, jnp.concatenate([ka] * HEADS, axis=0), zero))

        p = self_mask * jnp.dot((q * kk).astype(BF16), head_ones, preferred_element_type=F32)
        for b, live, sign in levels:
            ref = both(boundary_rows(cum[0:c_sz], b, False), boundary_rows(cum[c_sz:2 * c_sz], b, True))
            z = jnp.exp((cum - ref) * sign)
            qa = (q * z).astype(BF16)
            ka = (kk * z).astype(BF16)
            p = p + live * both(scores(qa[0:c_sz], ka[0:c_sz]), scores(qa[c_sz:2 * c_sz], ka[c_sz:2 * c_sz]))
        pb = p.astype(BF16)

        def values(p_half, v_half):
            return jnp.dot(p_half, jnp.where(same_head, jnp.concatenate([v_half] * HEADS, axis=0), zero),
                           preferred_element_type=F32)

        o_ref[0, rows_f, :] = o[0:c_sz] + values(pb[0:c_sz], v_f)
        ob_ref[rows_b, :] = o[c_sz:2 * c_sz] + values(pb[c_sz:2 * c_sz], v_b)
        return carry

    st_ref[...] = jnp.zeros_like(st_ref)
    lax.fori_loop(0, n_all, scan_step, 0)

    def readout(c, carry):
        rows = pl.ds(pl.multiple_of(c * ROW_TILE, ROW_TILE), ROW_TILE)
        tot = o_ref[0, rows, :] + ob_ref[rows, :]
        ms = sum(jnp.dot(piece, head_ones, preferred_element_type=F32)
                 for piece in _split3(tot * tot)) * (1.0 / HEAD_V)
        g = pa_ref[0, rows, 4 * A_W:5 * A_W]
        o_ref[0, rows, :] = tot * lax.rsqrt(ms + NORM_EPS) * gain_ref[...] * _silu(g)
        return carry

    lax.fori_loop(0, n_all * c_sz // ROW_TILE, readout, 0)


def _hgrn(pa3, lb, gain, n_ctx_rows):
    n_batch, rows, _ = pa3.shape
    kern = functools.partial(_hgrn_kernel, n_ctx=n_ctx_rows // CHUNK, n_all=rows // CHUNK)
    return pl.pallas_call(
        kern,
        out_shape=jax.ShapeDtypeStruct((n_batch, rows, A_W), F32),
        grid=(n_batch,),
        in_specs=[pl.BlockSpec((1, rows, PA_W), lambda b: (b, 0, 0)),
                  pl.BlockSpec((2, A_W), lambda b: (0, 0)),
                  pl.BlockSpec((1, A_W), lambda b: (0, 0))],
        out_specs=pl.BlockSpec((1, rows, A_W), lambda b: (b, 0, 0)),
        scratch_shapes=[pltpu.VMEM((2, A_W, A_W), F32), pltpu.VMEM((rows, A_W), F32)],
        compiler_params=_cparams(("arbitrary",)),
        name="hgrn2",
    )(pa3, lb, gain)


def _split3(x):
    hi = x.astype(BF16)
    rest = x - hi.astype(F32)
    mid = rest.astype(BF16)
    return hi, mid, (rest - mid.astype(F32)).astype(BF16)


def _mlstm_kernel(pb_ref, bias_ref, gain_ref, o_ref, cn_fwd_ref, cn_bwd_ref, ob_ref, *, n_ctx, n_all):
    assert CHUNK == HEAD_V
    c_sz, w, qk_w = CHUNK, B_W, HEADS * B_QK
    t_i = lax.broadcasted_iota(jnp.int32, (c_sz, w), 0)
    s_i = lax.broadcasted_iota(jnp.int32, (c_sz, w), 1) % c_sz
    diag4 = s_i == t_i
    half_lane = lax.broadcasted_iota(jnp.int32, (1, LANES), 1) < HEAD_V
    k_block = (lax.broadcasted_iota(jnp.int32, (HEADS * c_sz, qk_w), 0) // c_sz
               == lax.broadcasted_iota(jnp.int32, (HEADS * c_sz, qk_w), 1) // B_QK)
    v_block = (lax.broadcasted_iota(jnp.int32, (HEADS * c_sz, w), 0) // c_sz
               == lax.broadcasted_iota(jnp.int32, (HEADS * c_sz, w), 1) // HEAD_V)
    state_block = (lax.broadcasted_iota(jnp.int32, (qk_w, 2 * w), 0) // B_QK
                   == (lax.broadcasted_iota(jnp.int32, (qk_w, 2 * w), 1) % w) // HEAD_V)
    head_ones = v_block.astype(BF16)
    q_off, k_off, v_off, o_off, g_off = 0, qk_w, 2 * qk_w, 2 * qk_w + B_W, 2 * qk_w + 2 * B_W

    sel_r = lax.broadcasted_iota(jnp.int32, (2 * LANES, 2 * w), 0)
    sel_c = lax.broadcasted_iota(jnp.int32, (2 * LANES, 2 * w), 1)
    sel = (sel_r == jnp.where(sel_c < w, HEADS + sel_c // HEAD_V, LANES + (sel_c - w) // HEAD_V)).astype(BF16)

    def head_max(x):
        outs = []
        for col in range(w // LANES):
            xc = x[:, col * LANES:(col + 1) * LANES]
            lo = jnp.max(jnp.where(half_lane, xc, -jnp.inf), axis=-1, keepdims=True)
            hi = jnp.max(jnp.where(half_lane, -jnp.inf, xc), axis=-1, keepdims=True)
            outs.append(jnp.where(half_lane, lo, hi))
        return jnp.concatenate(outs, axis=1)

    mask2 =jnp.concatenate([s_i <= t_i, s_i >= t_i], axis=0)
    r2 = lax.broadcasted_iota(jnp.int32, (2 * c_sz, 2 * c_sz), 0)
    c2 = lax.broadcasted_iota(jnp.int32, (2 * c_sz, 2 * c_sz), 1)
    tri2 = jnp.logical_or(jnp.logical_and(r2 < c_sz, c2 <= r2),
                          jnp.logical_and(r2 >= c_sz, c2 >= r2)).astype(BF16)

    def per_half(fn, x):
        return jnp.concatenate([jnp.broadcast_to(fn(x[0:c_sz]), (c_sz, w)),
                                jnp.broadcast_to(fn(x[c_sz:2 * c_sz]), (c_sz, w))], axis=0)

    def rows_of(row_f, row_b):
        return jnp.concatenate([jnp.broadcast_to(row_f, (c_sz, w)), jnp.broadcast_to(row_b, (c_sz, w))], axis=0)

    def load(c_f, c_b, lo, hi):
        rf = pl.ds(pl.multiple_of(c_f * c_sz, c_sz), c_sz)
        rb = pl.ds(pl.multiple_of(c_b * c_sz, c_sz), c_sz)
        return pb_ref[0, rf, lo:hi], pb_ref[0, rb, lo:hi]

    def scan_step(n, carry):
        m_f, m_b = carry
        c_f = n
        c_b = jnp.where(n < n_ctx, n_ctx - 1 - n, n_all - 1 - (n - n_ctx))
        q_f, q_b = load(c_f, c_b, q_off, q_off + qk_w)
        k_f, k_b = load(c_f, c_b, k_off, k_off + qk_w)
        v_f, v_b = load(c_f, c_b, v_off, v_off + B_W)
        g_f, g_b = load(c_f, c_b, g_off, g_off + LANES)
        gts = jnp.concatenate([g_f + bias_ref[...],
                               pltpu.roll(g_b + bias_ref[...], LANES - 2 * HEADS, 1)], axis=0)
        log_f = jnp.minimum(gts, 0.0) - jnp.log(1.0 + jnp.exp(-jnp.abs(gts)))
        cum_col = sum(jnp.dot(tri2, p, preferred_element_type=F32) for p in _split3(log_f))
        both = sum(jnp.dot(p, sel, preferred_element_type=F32)
                   for p in _split3(jnp.concatenate([cum_col, gts], axis=1)))
        cum_t = both[:, :w]
        ig_t = both[:, w:]
        src = per_half(lambda x: jnp.sum(jnp.where(diag4, x, 0.0), axis=0, keepdims=True), cum_t - ig_t)
        m_prev = rows_of(m_f, m_b)
        log_d = jnp.where(mask2, cum_t - src, MASK_NEG)
        log_inter = cum_t + m_prev
        m_t = jnp.maximum(log_inter, head_max(log_d))
        w_intra = jnp.where(mask2, jnp.exp(log_d - m_t), 0.0)
        w_inter = jnp.exp(log_inter - m_t)

        scale = B_QK ** -0.5
        qb_f, qb_b = (q_f * scale).astype(BF16), (q_b * scale).astype(BF16)
        kb_f, kb_b = k_f.astype(BF16), k_b.astype(BF16)
        zero = jnp.zeros((), BF16)

        def scores(qb, kb):
            return _nt(qb, jnp.where(k_block, jnp.concatenate([kb] * HEADS, axis=0), zero))

        p = (jnp.concatenate([scores(qb_f, kb_f), scores(qb_b, kb_b)], axis=0) * w_intra).astype(BF16)

        def values(p_half, v_half):
            v_bd = jnp.where(v_block, jnp.concatenate([v_half.astype(BF16)] * HEADS, axis=0), zero)
            return jnp.dot(p_half, v_bd, preferred_element_type=F32)

        cn_f = cn_fwd_ref[...]
        cn_b = cn_bwd_ref[...]
        inter = jnp.concatenate([jnp.dot(qb_f, cn_f.astype(BF16), preferred_element_type=F32),
                                 jnp.dot(qb_b, cn_b.astype(BF16), preferred_element_type=F32)], axis=0)
        num = w_inter * inter[:, :w] + jnp.concatenate(
            [values(p[0:c_sz], v_f), values(p[c_sz:2 * c_sz], v_b)], axis=0)
        den = w_inter * inter[:, w:] + jnp.dot(p, head_ones, preferred_element_type=F32)
        h_out = num / jnp.maximum(jnp.abs(den), jnp.exp(-m_t))
        o_ref[0, pl.ds(pl.multiple_of(c_f * c_sz, c_sz), c_sz), :] = h_out[0:c_sz]
        ob_ref[pl.ds(pl.multiple_of(c_b * c_sz, c_sz), c_sz), :] = h_out[c_sz:2 * c_sz]

        end_f = cum_t[c_sz - 1:c_sz, :]
        end_b = cum_t[c_sz:c_sz + 1, :]
        log_end = rows_of(end_f, end_b) - cum_t + ig_t
        m_end_f = jnp.maximum(end_f + m_f, jnp.max(log_end[0:c_sz], axis=0, keepdims=True))
        m_end_b = jnp.maximum(end_b + m_b, jnp.max(log_end[c_sz:2 * c_sz], axis=0, keepdims=True))
        w_end = jnp.exp(log_end - rows_of(m_end_f, m_end_b))

        def update(cn_ref, cn, kb, w_half, v_half, carry_w):
            upd = _tn(kb, jnp.concatenate([(w_half * v_half).astype(BF16), w_half.astype(BF16)], axis=1))
            cn_ref[...] = jnp.concatenate([carry_w, carry_w], axis=1) * cn + jnp.where(state_block, upd, 0.0)

        update(cn_fwd_ref, cn_f, kb_f, w_end[0:c_sz], v_f, jnp.exp(end_f + m_f - m_end_f))
        update(cn_bwd_ref, cn_b, kb_b, w_end[c_sz:2 * c_sz], v_b, jnp.exp(end_b + m_b - m_end_b))
        return m_end_f, m_end_b

    zero_m = jnp.zeros((1, w), F32)
    cn_fwd_ref[...] = jnp.zeros_like(cn_fwd_ref)
    cn_bwd_ref[...] = jnp.zeros_like(cn_bwd_ref)
    lax.fori_loop(0, n_all, scan_step, (zero_m, zero_m), unroll=2)

    def readout(c, carry):
        rows = pl.ds(pl.multiple_of(c * ROW_TILE, ROW_TILE), ROW_TILE)
        tot = o_ref[0, rows, :] + ob_ref[rows, :]
        normed = jnp.concatenate(
            [_rms(tot[:, h * HEAD_V:(h + 1) * HEAD_V]) for h in range(HEADS)], axis=1) * gain_ref[...]
        og = pb_ref[0, rows, o_off:o_off + B_W]
        o_ref[0, rows, :] = normed * jax.nn.sigmoid(og)
        return carry

    lax.fori_loop(0, n_all * c_sz // ROW_TILE, readout, 0)


def _mlstm(pb3, bias, gain, n_ctx_rows):
    n_batch, rows, _ = pb3.shape
    kern = functools.partial(_mlstm_kernel, n_ctx=n_ctx_rows // CHUNK, n_all=rows // CHUNK)
    return pl.pallas_call(
        kern,
        out_shape=jax.ShapeDtypeStruct((n_batch, rows, B_W), F32),
        grid=(n_batch,),
        in_specs=[pl.BlockSpec((1, rows, PB_W), lambda b: (b, 0, 0)),
                  pl.BlockSpec((1, LANES), lambda b: (0, 0)),
                  pl.BlockSpec((1, B_W), lambda b: (0, 0))],
        out_specs=pl.BlockSpec((1, rows, B_W), lambda b: (b, 0, 0)),
        scratch_shapes=[pltpu.VMEM((HEADS * B_QK, 2 * B_W), F32), pltpu.VMEM((HEADS * B_QK, 2 * B_W), F32),
                        pltpu.VMEM((rows, B_W), F32)],
        compiler_params=_cparams(("arbitrary",)),
        name="mlstm",
    )(pb3, bias, gain)


def _attn_kernel(lam_ref, qp_ref, qr_ref, k_ref, v_ref, gain_ref, o_ref, *, n_ctx, q_tile0, lam_init):
    lam = lam_ref[0]
    q_tile = pl.program_id(2) + q_tile0
    lane = lax.broadcasted_iota(jnp.int32, (1, 2 * C_DQK), 1)

    n_all = k_ref.shape[1]

    def finish(head, parts):
        o = parts[0] - lam * parts[1]
        o_ref[0, :, head] = _rms(o) * gain_ref[...] * (1.0 - lam_init)

    def sub_query(q, j):
        return jnp.where(lane // C_DQK == j, q, jnp.zeros_like(q))

    def row_max(s):
        return jnp.max(s, axis=-1, keepdims=True)

    def row_sum(s):
        return jnp.sum(s, axis=-1, keepdims=True)

    def pv(ex, v):
        return jnp.dot(ex.astype(BF16), v, preferred_element_type=F32)

    for hh in range(ATTN_HEADS_PER_STEP):
        head = slice(hh * C_DV, (hh + 1) * C_DV)

        @pl.when(q_tile * ROW_TILE < n_ctx)
        def _():
            kc = k_ref[0, 0:n_ctx, head]
            vc = v_ref[0, 0:n_ctx, head]
            parts = []
            for j in range(2):
                s = _nt(sub_query(qp_ref[0, :, head], j), kc)
                ex = jnp.exp2(s - row_max(s))
                parts.append(pv(ex, vc) / row_sum(ex))
            finish(head, parts)

        @pl.when(q_tile * ROW_TILE >= n_ctx)
        def _():
            kc = k_ref[0, 0:n_ctx, head]
            kl = k_ref[0, n_ctx:n_all, head]
            vc = v_ref[0, 0:n_ctx, head]
            vl = v_ref[0, n_ctx:n_all, head]
            parts = []
            for j in range(2):
                s_c = _nt(sub_query(qp_ref[0, :, head], j), kc)
                s_l = _nt(sub_query(qr_ref[0, :, head], j), kl)
                m = jnp.maximum(row_max(s_c), row_max(s_l))
                e_c = jnp.exp2(s_c - m)
                e_l = jnp.exp2(s_l - m)
                parts.append((pv(e_c, vc) + pv(e_l, vl)) / (row_sum(e_c) + row_sum(e_l)))
            finish(head, parts)


def _attention(pc3, lam, gain, n_ctx_rows, lam_init, with_ctx):
    n_batch, rows, _ = pc3.shape
    q_tile0 = 0 if with_ctx else n_ctx_rows // ROW_TILE
    n_q = rows // ROW_TILE - q_tile0
    gw = ATTN_HEADS_PER_STEP * C_DV
    hb = C_W // gw
    kern = functools.partial(_attn_kernel, n_ctx=n_ctx_rows, q_tile0=q_tile0, lam_init=lam_init)
    grid_spec = pltpu.PrefetchScalarGridSpec(
        num_scalar_prefetch=1,
        grid=(n_batch, hb, n_q),
        in_specs=[pl.BlockSpec((1, ROW_TILE, gw), lambda b, h, i, lam: (b, i + q_tile0, h)),
                  pl.BlockSpec((1, ROW_TILE, gw), lambda b, h, i, lam: (b, i + q_tile0, hb + h)),
                  pl.BlockSpec((1, rows, gw), lambda b, h, i, lam: (b, 0, 2 * hb + h)),
                  pl.BlockSpec((1, rows, gw), lambda b, h, i, lam: (b, 0, 3 * hb + h)),
                  pl.BlockSpec((1, C_DV), lambda b, h, i, lam: (0, 0))],
        out_specs=pl.BlockSpec((1, ROW_TILE, gw), lambda b, h, i, lam: (b, i, h)),
    )
    return pl.pallas_call(
        kern,
        out_shape=jax.ShapeDtypeStruct((n_batch, n_q * ROW_TILE, C_W), F32),
        grid_spec=grid_spec,
        compiler_params=_cparams(("arbitrary", "arbitrary", "arbitrary")),
        name="diff_attn",
    )(lam, pc3, pc3, pc3, pc3, gain)


def _outproj_kernel(oa_ref, ob_ref, oc_ref, x_ref, mod_ref, n_ref, w_ref, rw_ref, rb_ref,
                    xn_ref, h_ref, route_ref, cnt_ref, carry_ref):
    d = x_ref.shape[1]
    n_exp = rw_ref.shape[1]

    @pl.when(pl.program_id(0) == 0)
    def _():
        carry_ref[...] = jnp.zeros_like(carry_ref)

    mix = (jnp.dot(oa_ref[...].astype(BF16), w_ref[0:A_W, :], preferred_element_type=F32)
           + jnp.dot(ob_ref[...].astype(BF16), w_ref[A_W:A_W + B_W, :], preferred_element_type=F32)
           + jnp.dot(oc_ref[...].astype(BF16), w_ref[A_W + B_W:A_W + B_W + C_W, :], preferred_element_type=F32))
    xn = x_ref[...] + mod_ref[0, :, 2 * d:3 * d] * (_rms(mix) * n_ref[1:2, :])
    xn_ref[...] = xn
    h = _rms(xn) * n_ref[2:3, :] * (1.0 + mod_ref[0, :, 4 * d:5 * d]) + mod_ref[0, :, 3 * d:4 * d]
    h_ref[...] = _pack_bf16_pairs(h)

    h_hi = h.astype(BF16)
    h_mid = (h - h_hi.astype(F32)).astype(BF16)
    logits = _nt(rw_ref[0], h_hi) + _nt(rw_ref[1], h_hi) + _nt(rw_ref[0], h_mid) + rb_ref[...]
    e_sub = lax.broadcasted_iota(jnp.int32, logits.shape, 0)
    cur = logits
    picks, vals = [], []
    for _ in range(TOP_K):
        mx = jnp.max(cur, axis=0, keepdims=True)
        idx = jnp.min(jnp.where(cur == mx, e_sub, n_exp), axis=0, keepdims=True)
        hit = e_sub == idx
        cur = jnp.where(hit, -jnp.inf, cur)
        picks.append((idx, hit.astype(F32)))
        vals.append(mx)
    exps = [jnp.exp(vv - vals[0]) for vv in vals]
    total = exps[0] + exps[1] + exps[2] + exps[3]

    chosen = picks[0][1] + picks[1][1] + picks[2][1] + picks[3][1]
    tm = logits.shape[1]
    before = (lax.broadcasted_iota(jnp.int32, (tm, tm), 0) < lax.broadcasted_iota(jnp.int32, (tm, tm), 1))
    seen = jnp.dot(chosen.astype(BF16), before.astype(BF16), preferred_element_type=F32) + carry_ref[...]
    new_carry = carry_ref[...] + jnp.sum(chosen, axis=1, keepdims=True)
    carry_ref[...] = new_carry
    cnt_ref[...] = new_carry

    o_row = lax.broadcasted_iota(jnp.int32, route_ref.shape, 0)
    route = jnp.zeros(route_ref.shape, F32)
    for kk in range(TOP_K):
        idx, hit = picks[kk]
        rank = jnp.sum(hit * seen, axis=0, keepdims=True)
        route = (route + jnp.where(o_row == kk, idx.astype(F32), 0.0)
                 + jnp.where(o_row == TOP_K + kk, exps[kk] / total, 0.0)
                 + jnp.where(o_row == 2 * TOP_K + kk, rank, 0.0))
    route_ref[...] = route


ROUTE_ROWS = 16


def _outproj(oa, ob, oc, xy, mod, norms, w_out_b, router_wt, router_b, layer, n_tiles, in_tile, oc_tile, mod_row):
    d = xy.shape[1]
    n_exp = router_wt.shape[2]
    t_out = n_tiles * ROW_TILE
    xn, h, route_t, counts = pl.pallas_call(
        _outproj_kernel,
        out_shape=(jax.ShapeDtypeStruct((t_out, d), F32),
                   jax.ShapeDtypeStruct((t_out, d // 2), jnp.int32),
                   jax.ShapeDtypeStruct((n_tiles * ROUTE_ROWS, ROW_TILE), F32),
                   jax.ShapeDtypeStruct((n_exp, 1), F32)),
        grid=(n_tiles,),
        in_specs=[pl.BlockSpec((ROW_TILE, A_W), lambda i: (in_tile(i), 0)),
                  pl.BlockSpec((ROW_TILE, B_W), lambda i: (in_tile(i), 0)),
                  pl.BlockSpec((ROW_TILE, C_W), lambda i: (oc_tile(i), 0)),
                  pl.BlockSpec((ROW_TILE, d), lambda i: (in_tile(i), 0)),
                  pl.BlockSpec((1, 1, 6 * d), lambda i: (mod_row(i), 0, 0)),
                  pl.BlockSpec((None, 4, d), lambda i: (layer, 0, 0)),
                  pl.BlockSpec((None, d, d), lambda i: (layer, 0, 0)),
                  pl.BlockSpec((None, 2, n_exp, d), lambda i: (layer, 0, 0, 0)),
                  pl.BlockSpec((None, n_exp, 1), lambda i: (layer, 0, 0))],
        out_specs=(pl.BlockSpec((ROW_TILE, d), lambda i: (i, 0)),
                   pl.BlockSpec((ROW_TILE, d // 2), lambda i: (i, 0)),
                   pl.BlockSpec((ROUTE_ROWS, ROW_TILE), lambda i: (i, 0)),
                   pl.BlockSpec((n_exp, 1), lambda i: (0, 0))),
        scratch_shapes=[pltpu.VMEM((n_exp, 1), F32)],
        compiler_params=_cparams(("arbitrary",)),
        name="outproj_router",
    )(oa, ob, oc, xy, mod, norms, w_out_b, router_wt, router_b)
    return xn, h, route_t.reshape(n_tiles, ROUTE_ROWS, ROW_TILE), counts[:, 0]


PAIR_BLOCK = 2 * LANES


def _moe_kernel(te_ref, first_ref, valid_ref, nu_ref, slot_ref, next_ref, xs_ref, w1_hbm, b1_ref, w2_hbm, b2_ref,
                ys_ref, w1p_ref, w2b_ref, w1_buf, w2_buf, w_sem, *, layer):
    i = pl.program_id(0)
    two_f = w1_buf.shape[2]
    n_blk = two_f // PAIR_BLOCK

    def fetch(expert, slot):
        return (pltpu.make_async_copy(w1_hbm.at[layer, expert], w1_buf.at[slot], w_sem.at[0, slot]),
                pltpu.make_async_copy(w2_hbm.at[layer, expert], w2_buf.at[slot], w_sem.at[1, slot]))

    @pl.when(i == 0)
    def _():
        for copy in fetch(te_ref[0], 0):
            copy.start()

    @pl.when(jnp.logical_and(i < nu_ref[0], first_ref[i] == 1))
    def _():
        slot = slot_ref[i]
        for copy in fetch(te_ref[i], slot):
            copy.wait()
        r = lax.broadcasted_iota(jnp.int32, (PAIR_BLOCK, PAIR_BLOCK), 0)
        c = lax.broadcasted_iota(jnp.int32, (PAIR_BLOCK, PAIR_BLOCK), 1)
        perm = (r == jnp.where(c < LANES, 2 * c, 2 * (c - LANES) + 1)).astype(BF16)
        for blk in range(n_blk):
            cols = slice(blk * PAIR_BLOCK, (blk + 1) * PAIR_BLOCK)
            w1p_ref[:, cols] = jnp.dot(w1_buf[slot, :, cols].astype(BF16), perm,
                                       preferred_element_type=F32).astype(BF16)
        w2b_ref[...] = w2_buf[slot].astype(BF16)

        @pl.when(next_ref[i] >= 0)
        def _():
            for copy in fetch(next_ref[i], 1 - slot):
                copy.start()

    tile_rows = xs_ref.shape[0]
    used = i < nu_ref[0]
    valid = valid_ref[i]

    def expert_ffn(n_rows):
        x = _unpack_bf16_pairs(xs_ref[0:n_rows, :]).astype(BF16)
        row = lax.broadcasted_iota(jnp.int32, (n_rows, 1), 0)
        x = jnp.where(row < valid, x, jnp.zeros_like(x))
        hid = jnp.dot(x, w1p_ref[...], preferred_element_type=F32) + b1_ref[...]
        acts = []
        for blk in range(n_blk):
            glu = jnp.minimum(hid[:, blk * PAIR_BLOCK:blk * PAIR_BLOCK + LANES], SWIGLU_LIMIT)
            lin = jnp.clip(hid[:, blk * PAIR_BLOCK + LANES:(blk + 1) * PAIR_BLOCK], -SWIGLU_LIMIT, SWIGLU_LIMIT)
            acts.append((glu * jax.nn.sigmoid(SWIGLU_ALPHA * glu) * (lin + 1.0)).astype(BF16))
        y = jnp.dot(jnp.concatenate(acts, axis=1), w2b_ref[...], preferred_element_type=F32) + b2_ref[...]
        ys_ref[0:n_rows, :] = _pack_bf16_pairs(y)
        if n_rows < tile_rows:
            ys_ref[n_rows:tile_rows, :] = jnp.zeros((tile_rows - n_rows, ys_ref.shape[1]), ys_ref.dtype)

    @pl.when(jnp.logical_and(used, valid > tile_rows // 2))
    def _():
        expert_ffn(tile_rows)

    @pl.when(jnp.logical_and(used, valid <= tile_rows // 2))
    def _():
        expert_ffn(tile_rows // 2)

    @pl.when(jnp.logical_not(used))
    def _():
        ys_ref[...] = jnp.zeros_like(ys_ref)


def _moe(plan, xs, w1, b1p, w2, b2, layer):
    r_max, half_d = xs.shape
    d = 2 * half_d
    two_f = w1.shape[3]
    f = two_f // 2
    n_tiles = r_max // MOE_TILE

    def row_tile(i, te, fi, va, nu, sl, nx):
        return (jnp.maximum(jnp.minimum(i, nu[0] - 1), 0), 0)

    def bias_block(i, te, fi, va, nu, sl, nx):
        return (layer, te[i], 0, 0)

    grid_spec = pltpu.PrefetchScalarGridSpec(
        num_scalar_prefetch=6,
        grid=(n_tiles,),
        in_specs=[pl.BlockSpec((MOE_TILE, half_d), row_tile),
                  pl.BlockSpec(memory_space=pl.ANY),
                  pl.BlockSpec((None, None, 1, two_f), bias_block),
                  pl.BlockSpec(memory_space=pl.ANY),
                  pl.BlockSpec((None, None, 1, d), bias_block)],
        out_specs=pl.BlockSpec((MOE_TILE, half_d), lambda i, te, fi, va, nu, sl, nx: (i, 0)),
        scratch_shapes=[pltpu.VMEM((d, two_f), BF16), pltpu.VMEM((f, d), BF16),
                        pltpu.VMEM((2, d, two_f), F32), pltpu.VMEM((2, f, d), F32),
                        pltpu.SemaphoreType.DMA((2, 2))],
    )
    return pl.pallas_call(
        functools.partial(_moe_kernel, layer=layer),
        out_shape=jax.ShapeDtypeStruct((r_max, half_d), jnp.int32),
        grid_spec=grid_spec,
        compiler_params=_cparams(("arbitrary",)),
        name="moe_experts",
    )(*plan, xs, w1, b1p, w2, b2)


SC_CORES = 2
SC_SUBCORES = 16
SC_CHUNK = 64


def _dispatch_rows(hp, pos, r_max):
    t, width = hp.shape
    workers = SC_CORES * SC_SUBCORES
    assert t % (workers * SC_CHUNK) == 0
    per_worker = t // (workers * SC_CHUNK)
    idx = pos.reshape(-1, TOP_K, ROW_TILE // SC_CHUNK, SC_CHUNK).transpose(0, 2, 1, 3).reshape(
        workers, per_worker * TOP_K, SC_CHUNK)
    mesh = plsc.VectorSubcoreMesh(core_axis_name="c", subcore_axis_name="s",
                                  num_cores=SC_CORES, num_subcores=SC_SUBCORES)

    @functools.partial(
        pl.kernel, mesh=mesh,
        out_type=jax.ShapeDtypeStruct((r_max, width), hp.dtype),
        scratch_types=[pltpu.VMEM((per_worker * TOP_K, SC_CHUNK), jnp.int32),
                       pltpu.VMEM((2, SC_CHUNK, width), hp.dtype),
                       pltpu.SemaphoreType.DMA((2,)),
                       pltpu.SemaphoreType.DMA((2,))],
    )
    def scatter(hp_hbm, idx_hbm, out_hbm, idx_v, rows_v, read_sem, write_sem):
        wid = lax.axis_index("s") * SC_CORES + lax.axis_index("c")
        pltpu.sync_copy(idx_hbm.at[wid], idx_v)

        def read(j):
            row0 = pl.multiple_of((wid * per_worker + j) * SC_CHUNK, SC_CHUNK)
            return pltpu.make_async_copy(hp_hbm.at[pl.ds(row0, SC_CHUNK)], rows_v.at[j % 2], read_sem.at[j % 2])

        def write(j, k):
            return pltpu.make_async_copy(rows_v.at[j % 2], out_hbm.at[idx_v.at[j * TOP_K + k]],
                                         write_sem.at[j % 2])

        read(0).start()
        for j in range(per_worker):
            read(j).wait()
            for k in range(TOP_K):
                write(j, k).start()
            if j + 1 < per_worker:
                if j >= 1:
                    for k in range(TOP_K):
                        write(j - 1, k).wait()
                read(j + 1).start()
        for j in range(max(per_worker - 2, 0), per_worker):
            for k in range(TOP_K):
                write(j, k).wait()

    return scatter(hp, idx)


def _combine_rows(ysp, pos):
    width = ysp.shape[1]
    t = pos.shape[0] * pos.shape[2]
    workers = SC_CORES * SC_SUBCORES
    assert t % (workers * SC_CHUNK) == 0
    per_worker = t // (workers * SC_CHUNK)
    units = per_worker * TOP_K
    idx = pos.reshape(-1, TOP_K, ROW_TILE // SC_CHUNK, SC_CHUNK).transpose(0, 2, 1, 3).reshape(
        workers, units, SC_CHUNK)
    mesh = plsc.VectorSubcoreMesh(core_axis_name="c", subcore_axis_name="s",
                                  num_cores=SC_CORES, num_subcores=SC_SUBCORES)

    @functools.partial(
        pl.kernel, mesh=mesh,
        out_type=jax.ShapeDtypeStruct((TOP_K, t, width), ysp.dtype),
        scratch_types=[pltpu.VMEM((units, SC_CHUNK), jnp.int32),
                       pltpu.VMEM((2, SC_CHUNK, width), ysp.dtype),
                       pltpu.SemaphoreType.DMA((2,)),
                       pltpu.SemaphoreType.DMA((2,))],
    )
    def gather(ys_hbm, idx_hbm, out_hbm, idx_v, rows_v, read_sem, write_sem):
        wid = lax.axis_index("s") * SC_CORES + lax.axis_index("c")
        pltpu.sync_copy(idx_hbm.at[wid], idx_v)

        def read(u):
            return pltpu.make_async_copy(ys_hbm.at[idx_v.at[u]], rows_v.at[u % 2], read_sem.at[u % 2])

        def write(u):
            row0 = pl.multiple_of((wid * per_worker + u // TOP_K) * SC_CHUNK, SC_CHUNK)
            return pltpu.make_async_copy(rows_v.at[u % 2], out_hbm.at[u % TOP_K, pl.ds(row0, SC_CHUNK)],
                                         write_sem.at[u % 2])

        read(0).start()
        for u in range(units):
            read(u).wait()
            write(u).start()
            if u + 1 < units:
                if u >= 1:
                    write(u - 1).wait()
                read(u + 1).start()
        for u in range(max(units - 2, 0), units):
            write(u).wait()

    return gather(ysp, idx)


def _ffn_residual_kernel(x_ref, w_ref, y0_ref, y1_ref, y2_ref, y3_ref, mod_ref, n_ref, o_ref):
    d = x_ref.shape[1]
    wts = w_ref[...]
    ffn = None
    for k, y_ref in enumerate((y0_ref, y1_ref, y2_ref, y3_ref)):
        term = wts[:, k:k + 1] * _unpack_bf16_pairs(y_ref[...])
        ffn = term if ffn is None else ffn + term
    o_ref[...] = x_ref[...] + mod_ref[0, :, 5 * d:6 * d] * (_rms(ffn) * n_ref[3:4, :])


def _ffn_residual(xn, wts, ys_by_k, mod, norms, layer, mod_row):
    t, d = xn.shape
    row_spec = pl.BlockSpec((ROW_TILE, d), lambda i: (i, 0))

    def y_spec(k):
        return pl.BlockSpec((None, ROW_TILE, d // 2), lambda i: (k, i, 0))

    return pl.pallas_call(
        _ffn_residual_kernel,
        out_shape=jax.ShapeDtypeStruct((t, d), F32),
        grid=(t // ROW_TILE,),
        in_specs=[row_spec,
                  pl.BlockSpec((ROW_TILE, TOP_K), lambda i: (i, 0)),
                  y_spec(0), y_spec(1), y_spec(2), y_spec(3),
                  pl.BlockSpec((1, 1, 6 * d), lambda i: (mod_row(i), 0, 0)),
                  pl.BlockSpec((None, 4, d), lambda i: (layer, 0, 0))],
        out_specs=row_spec,
        compiler_params=_cparams(("arbitrary",)),
        name="ffn_residual",
    )(xn, wts, ys_by_k, ys_by_k, ys_by_k, ys_by_k, mod, norms)


def _dispatch_plan(route_t, counts, r_max):
    n_row_tiles = route_t.shape[0]
    idx = route_t[:, 0:TOP_K, :].astype(jnp.int32)
    wts = route_t[:, TOP_K:2 * TOP_K, :].transpose(0, 2, 1).reshape(n_row_tiles * ROW_TILE, TOP_K)
    rank = route_t[:, 2 * TOP_K:3 * TOP_K, :].astype(jnp.int32)
    cnt = counts.astype(jnp.int32)
    padded = (cnt + MOE_TILE - 1) // MOE_TILE * MOE_TILE
    ends = jnp.cumsum(padded)
    starts = ends - padded
    pos = rank
    for e in range(cnt.shape[0]):
        pos = pos + jnp.where(idx == e, starts[e], 0)
    n_tiles = r_max // MOE_TILE
    n_used = ends[-1] // MOE_TILE
    tile_ids = jnp.minimum(jnp.arange(n_tiles, dtype=jnp.int32), n_used - 1)
    tile_expert = jnp.sum((ends // MOE_TILE)[None, :] <= tile_ids[:, None], axis=1).astype(jnp.int32)
    tile_first = jnp.concatenate(
        [jnp.ones((1,), jnp.int32), (tile_expert[1:] != tile_expert[:-1]).astype(jnp.int32)])
    of_expert = (tile_expert[:, None] == jnp.arange(cnt.shape[0], dtype=jnp.int32)[None, :]).astype(jnp.int32)
    first_tile = jnp.sum(of_expert * ((ends - padded) // MOE_TILE)[None, :], axis=1)
    tile_valid = jnp.clip(jnp.sum(of_expert * cnt[None, :], axis=1) - (tile_ids - first_tile) * MOE_TILE,
                          0, MOE_TILE).astype(jnp.int32)
    experts = jnp.arange(cnt.shape[0], dtype=jnp.int32)
    has_rows = cnt > 0
    tile_slot = (jnp.sum(of_expert * (jnp.cumsum(has_rows.astype(jnp.int32)) - 1)[None, :], axis=1) % 2).astype(jnp.int32)
    later_with_rows = jnp.logical_and(has_rows[None, :], experts[None, :] > experts[:, None])
    next_expert = jnp.min(jnp.where(later_with_rows, experts[None, :], cnt.shape[0]), axis=1)
    next_expert = jnp.where(next_expert < cnt.shape[0], next_expert, -1)
    tile_next = jnp.sum(of_expert * next_expert[None, :], axis=1).astype(jnp.int32)
    plan = (tile_expert, tile_first, tile_valid, n_used.reshape(1).astype(jnp.int32), tile_slot, tile_next)
    return pos, wts, plan


def _rope_tables(length, n_ctx_rows):
    rows = length // GRID_W
    row = jnp.repeat(jnp.arange(rows, dtype=F32), GRID_W)
    col = jnp.tile(jnp.arange(GRID_W, dtype=F32), rows)
    n_freq = C_DQK // 4
    inv_freq = ROPE_THETA ** (-jnp.arange(n_freq, dtype=F32) / n_freq)
    ang_r = row[:, None] * inv_freq
    ang_c = col[:, None] * inv_freq
    cos = jnp.concatenate([jnp.cos(ang_r), jnp.cos(ang_r), jnp.cos(ang_c), jnp.cos(ang_c)], axis=-1)
    sin = jnp.concatenate([-jnp.sin(ang_r), jnp.sin(ang_r), -jnp.sin(ang_c), jnp.sin(ang_c)], axis=-1)
    reps = C_W // C_DQK
    cos = jnp.concatenate([jnp.ones((n_ctx_rows, C_W), F32), jnp.tile(cos, (1, reps))], axis=0)
    sin = jnp.concatenate([jnp.zeros((n_ctx_rows, C_W), F32), jnp.tile(sin, (1, reps))], axis=0)
    return cos, sin


def _hgrn_lower_bound(table, layer):
    p = jax.nn.softmax(table.astype(F32), axis=1)
    cum = jnp.cumsum(p, axis=1) - p[:, :1]
    return jnp.clip(cum[:, layer], 0.0, 1.0)


def kernel(x, c, ctx, c_ctx, ada_w, ada_b, sandwich_norms, w_in, w_out, hgrn_lower_bounds, hgrn_norm,
           mlstm_gate_bias, mlstm_norm, diff_lambdas, diff_norm, router_w, router_b, moe_w1, moe_b1,
           moe_w2, moe_b2):
    n_batch, seq, d = x.shape
    n_ctx_rows = ctx.shape[1]
    depth = w_in.shape[0]
    n_exp = router_w.shape[2]
    assert seq % ROW_TILE == 0 and n_ctx_rows % ROW_TILE == 0 and seq % GRID_W == 0
    rows_b = n_ctx_rows + seq
    tiles_b = rows_b // ROW_TILE
    ctx_tiles = n_ctx_rows // ROW_TILE
    lat_tiles = seq // ROW_TILE

    b_main = PA_W + 2 * HEADS * B_QK + 2 * B_W
    w_in_p = jnp.concatenate(
        [w_in[:, :, :b_main],
         jnp.pad(w_in[:, :, b_main:b_main + N_GATE], ((0, 0), (0, 0), (0, LANES - N_GATE))),
         w_in[:, :, b_main + N_GATE:]], axis=2).astype(BF16)
    w_out_b = w_out.astype(BF16)
    two_f = moe_b1.shape[2]
    b1p = moe_b1.reshape(depth, n_exp, two_f // PAIR_BLOCK, LANES, 2).transpose(0, 1, 2, 4, 3).reshape(
        depth, n_exp, 1, two_f)
    b2 = moe_b2[:, :, None, :]
    ada_b3 = ada_b[:, None, :]
    router_b3 = router_b[:, :, None]
    rw_t = router_w.transpose(0, 2, 1)
    rw_hi = rw_t.astype(BF16)
    router_wt = jnp.stack([rw_hi, (rw_t - rw_hi.astype(F32)).astype(BF16)], axis=1)
    gate_bias = jnp.pad(mlstm_gate_bias, ((0, 0), (0, LANES - N_GATE)))
    cos, sin = _rope_tables(seq, n_ctx_rows)

    cond_rows = (n_batch + 1 + 7) // 8 * 8
    cvec = jnp.zeros((cond_rows, d), F32).at[:n_batch].set(c).at[n_batch].set(c_ctx)

    def mod_row_all(i):
        return jnp.where(i % tiles_b < ctx_tiles, n_batch, i // tiles_b)

    xy = jnp.concatenate([ctx, x], axis=1).reshape(n_batch * rows_b, d)

    for layer in range(depth):
        last = layer == depth - 1
        mod = _adaln(cvec, ada_w, ada_b3, layer).reshape(cond_rows, 1, 6 * d)
        lb = _hgrn_lower_bound(hgrn_lower_bounds, layer)
        lam_init = 0.8 - 0.6 * math.exp(-0.3 * layer)
        lq1, lk1, lq2, lk2 = diff_lambdas[layer].astype(F32)
        lam = (jnp.exp(jnp.sum(lq1 * lk1)) - jnp.exp(jnp.sum(lq2 * lk2)) + lam_init).reshape(1)

        pa, pb, pc = _inproj(xy, mod, sandwich_norms, w_in_p, cos, sin, layer, tiles_b, ctx_tiles, n_batch)
        oa = _hgrn(pa.reshape(n_batch, rows_b, PA_W), lb, jnp.tile(hgrn_norm[layer], HEADS)[None, :], n_ctx_rows)
        ob = _mlstm(pb.reshape(n_batch, rows_b, PB_W), gate_bias[layer][None, :], mlstm_norm[layer][None, :],
                    n_ctx_rows)
        oc = _attention(pc.reshape(n_batch, rows_b, PC_OUT), lam, diff_norm[layer][None, :], n_ctx_rows,
                        lam_init, not last)

        if last:
            n_tiles = n_batch * lat_tiles
            in_tile = lambda i: (i // lat_tiles) * tiles_b + ctx_tiles + i % lat_tiles
            mod_row = lambda i: i // lat_tiles
        else:
            n_tiles = n_batch * tiles_b
            in_tile = lambda i: i
            mod_row = mod_row_all
        xn, h, route, counts = _outproj(
            oa.reshape(-1, A_W), ob.reshape(-1, B_W), oc.reshape(-1, C_W), xy, mod, sandwich_norms, w_out_b,
            router_wt, router_b3, layer, n_tiles, in_tile, lambda i: i, mod_row)

        r_max = n_tiles * ROW_TILE * TOP_K + n_exp * MOE_TILE
        pos, wts, plan = _dispatch_plan(route, counts, r_max)
        xs = _dispatch_rows(h, pos, r_max)
        ys = _moe(plan, xs, moe_w1, b1p, moe_w2, b2, layer)
        xy = _ffn_residual(xn, wts, _combine_rows(ys, pos), mod, sandwich_norms, layer, mod_row)

    return xy.reshape(n_batch, seq, d)
```

```python
import functools
import math

import jax
import jax.numpy as jnp
from jax import lax
from jax.experimental import pallas as pl
from jax.experimental.pallas import tpu as pltpu
from jax.experimental.pallas import tpu_sc as plsc

F32 = jnp.float32
BF16 = jnp.bfloat16
HI = lax.Precision.HIGHEST

HEADS = 4
A_W = 256
B_QK = 32
B_W = 256
C_DQK = 64
C_DV = 2 * C_DQK
ATTN_KEY_SCALE = C_DQK ** -0.5 * math.log2(math.e)
C_W = 512
HEAD_V = 64
N_GATE = 16
GRID_W = 64
TOP_K = 4
SWIGLU_ALPHA = 1.702
SWIGLU_LIMIT = 7.0
ROPE_THETA = 10000.0
NORM_EPS = 1e-6
MASK_NEG = -1e30
F_MIN = 1e-12

LANES = 128
ROW_TILE = 256
CHUNK = 64
MOE_TILE = 512
ATTN_HEADS_PER_STEP = 2
VMEM_LIMIT = 56 * 1024 * 1024

PA_W = 5 * A_W
PB_W = 2 * HEADS * B_QK + 2 * B_W + LANES
PC_IN = 3 * C_W
PC_OUT = 4 * C_W
W_IN_PAD = PA_W + PB_W + PC_IN


def _cparams(sem):
    return pltpu.CompilerParams(dimension_semantics=sem, vmem_limit_bytes=VMEM_LIMIT)


def _nt(a, b):
    return lax.dot_general(a, b, (((1,), (1,)), ((), ())), preferred_element_type=F32)


def _tn(a, b, precision=None):
    return lax.dot_general(a, b, (((0,), (0,)), ((), ())), preferred_element_type=F32, precision=precision)


def _rms(x):
    return x * lax.rsqrt(jnp.mean(x * x, axis=-1, keepdims=True) + NORM_EPS)


def _silu(x):
    return x * jax.nn.sigmoid(x)


def _pack_bf16_pairs(x):
    half = x.shape[1] // 2
    bits = pltpu.bitcast(x.astype(BF16).astype(F32), jnp.uint32)
    return pltpu.bitcast((bits[:, :half] >> 16) | (bits[:, half:] & jnp.uint32(0xFFFF0000)), jnp.int32)


def _unpack_bf16_pairs(words):
    bits = pltpu.bitcast(words, jnp.uint32)
    return jnp.concatenate([pltpu.bitcast(bits << 16, F32),
                            pltpu.bitcast(bits & jnp.uint32(0xFFFF0000), F32)], axis=1)


def _adaln_kernel(c_ref, w_ref, b_ref, o_ref):
    cond = _silu(c_ref[...])
    o_ref[...] = jnp.dot(cond, w_ref[...], preferred_element_type=F32, precision=HI) + b_ref[...]


def _adaln(cvec, ada_w, ada_b, layer):
    rows, d = cvec.shape
    return pl.pallas_call(
        _adaln_kernel,
        out_shape=jax.ShapeDtypeStruct((rows, 6 * d), F32),
        grid=(6,),
        in_specs=[pl.BlockSpec((rows, d), lambda j: (0, 0)),
                  pl.BlockSpec((None, d, d), lambda j: (layer, 0, j)),
                  pl.BlockSpec((None, 1, d), lambda j: (layer, 0, j))],
        out_specs=pl.BlockSpec((rows, d), lambda j: (0, j)),
        compiler_params=_cparams(("arbitrary",)),
        name="adaln",
    )(cvec, ada_w, ada_b)


def _inproj_kernel(x_ref, mod_ref, g_ref, w_ref, cos_ref, sin_ref, pa_ref, pb_ref, pc_ref):
    d = x_ref.shape[1]
    shift = mod_ref[0, :, 0:d]
    scale = mod_ref[0, :, d:2 * d]
    h = _rms(x_ref[...]) * g_ref[0:1, :] * (1.0 + scale) + shift
    hb = h.astype(BF16)
    pa_ref[...] = jnp.dot(hb, w_ref[:, 0:PA_W], preferred_element_type=F32)
    pb_ref[...] = jnp.dot(hb, w_ref[:, PA_W:PA_W + PB_W], preferred_element_type=F32)
    pc = jnp.dot(hb, w_ref[:, PA_W + PB_W:W_IN_PAD], preferred_element_type=F32)
    q = pc[:, 0:C_W]
    k = pc[:, C_W:2 * C_W] * ATTN_KEY_SCALE
    cos = cos_ref[...]
    sin = sin_ref[...]
    lane = lax.broadcasted_iota(jnp.int32, q.shape, 1)
    first = (lane % 32) < 16

    def rope(t):
        partner = jnp.where(first, pltpu.roll(t, C_W - 16, 1), pltpu.roll(t, 16, 1))
        return t * cos + partner * sin

    pc_ref[:, 0:C_W] = q.astype(BF16)
    pc_ref[:, C_W:2 * C_W] = rope(q).astype(BF16)
    pc_ref[:, 2 * C_W:3 * C_W] = rope(k).astype(BF16)
    pc_ref[:, 3 * C_W:4 * C_W] = pc[:, 2 * C_W:3 * C_W].astype(BF16)


def _inproj(xy, mod, norms, w_in_p, cos, sin, layer, tiles_per_batch, n_ctx_tiles, n_batch):
    t_all, d = xy.shape
    n_tiles = t_all // ROW_TILE

    def mod_row(i):
        return jnp.where(i % tiles_per_batch < n_ctx_tiles, n_batch, i // tiles_per_batch)

    return pl.pallas_call(
        _inproj_kernel,
        out_shape=(jax.ShapeDtypeStruct((t_all, PA_W), F32),
                   jax.ShapeDtypeStruct((t_all, PB_W), F32),
                   jax.ShapeDtypeStruct((t_all, PC_OUT), BF16)),
        grid=(n_tiles,),
        in_specs=[pl.BlockSpec((ROW_TILE, d), lambda i: (i, 0)),
                  pl.BlockSpec((1, 1, 6 * d), lambda i: (mod_row(i), 0, 0)),
                  pl.BlockSpec((None, 4, d), lambda i: (layer, 0, 0)),
                  pl.BlockSpec((None, d, W_IN_PAD), lambda i: (layer, 0, 0)),
                  pl.BlockSpec((ROW_TILE, C_W), lambda i: (i % tiles_per_batch, 0)),
                  pl.BlockSpec((ROW_TILE, C_W), lambda i: (i % tiles_per_batch, 0))],
        out_specs=(pl.BlockSpec((ROW_TILE, PA_W), lambda i: (i, 0)),
                   pl.BlockSpec((ROW_TILE, PB_W), lambda i: (i, 0)),
                   pl.BlockSpec((ROW_TILE, PC_OUT), lambda i: (i, 0))),
        compiler_params=_cparams(("arbitrary",)),
        name="inproj",
    )(xy, mod, norms, w_in_p, cos, sin)


def _hgrn_kernel(pa_ref, lb_ref, gain_ref, o_ref, st_ref, ob_ref, *, n_ctx, n_all):
    c_sz, w = CHUNK, A_W
    same_head = (lax.broadcasted_iota(jnp.int32, (w, w), 0) // HEAD_V
                 == lax.broadcasted_iota(jnp.int32, (w, w), 1) // HEAD_V)
    head_ones = same_head.astype(BF16)
    assert c_sz == HEAD_V
    t_i = lax.broadcasted_iota(jnp.int32, (c_sz, w), 0)
    s_i = lax.broadcasted_iota(jnp.int32, (c_sz, w), 1) % c_sz
    t_row = lax.broadcasted_iota(jnp.int32, (c_sz, 1), 0)
    row8 = t_row % 8

    def grouped_rows(a, k):
        return jnp.concatenate(
            [jnp.broadcast_to(a[8 * j + k:8 * j + k + 1, :], (8, w)) for j in range(c_sz // 8)], axis=0)

    def halving_levels(rev):
        out = []
        b = c_sz // 2
        while b >= 1:
            def later(i):
                return ((i % (2 * b)) < b) if rev else ((i % (2 * b)) >= b)
            live = jnp.logical_and(t_i // (2 * b) == s_i // (2 * b),
                                   jnp.logical_and(later(t_i), jnp.logical_not(later(s_i))))
            out.append((b, live.astype(F32), jnp.where(later(t_row), 1.0, -1.0)))
            b //= 2
        return out

    def both(x_f, x_b):
        return jnp.concatenate([x_f, x_b], axis=0)

    def rows_of(row_f, row_b):
        return both(jnp.broadcast_to(row_f, (c_sz, w)), jnp.broadcast_to(row_b, (c_sz, w)))

    self_mask = both(*[(s_i == t_i).astype(F32)] * 2)
    levels = [(b, both(live_f, live_b), both(sign_f, sign_b))
              for (b, live_f, sign_f), (_, live_b, sign_b) in zip(halving_levels(False), halving_levels(True))]
    lb2 = rows_of(lb_ref[0:1, :], lb_ref[1:2, :])
    r2 = lax.broadcasted_iota(jnp.int32, (2 * c_sz, 2 * c_sz), 0)
    c2 = lax.broadcasted_iota(jnp.int32, (2 * c_sz, 2 * c_sz), 1)
    tri2 = jnp.logical_or(jnp.logical_and(r2 < c_sz, c2 <= r2),
                          jnp.logical_and(r2 >= c_sz, c2 >= r2)).astype(BF16)
    zero = jnp.zeros((), BF16)

    def boundary_rows(cum, b, rev):
        if b >= 8:
            return jnp.concatenate(
                [jnp.broadcast_to(cum[r0 + (b if rev else b - 1):r0 + (b if rev else b - 1) + 1, :], (2 * b, w))
                 for r0 in range(0, c_sz, 2 * b)], axis=0)
        ref = None
        for g in reversed(range(8 // (2 * b))):
            cand = grouped_rows(cum, g * 2 * b + (b if rev else b - 1))
            ref = cand if ref is None else jnp.where(row8 < (g + 1) * 2 * b, cand, ref)
        return ref

    def scan_step(n, carry):
        c_f = n
        c_b = jnp.where(n < n_ctx, n_ctx - 1 - n, n_all - 1 - (n - n_ctx))
        rows_f = pl.ds(pl.multiple_of(c_f * c_sz, c_sz), c_sz)
        rows_b = pl.ds(pl.multiple_of(c_b * c_sz, c_sz), c_sz)
        q_pre = both(pa_ref[0, rows_f, 0:A_W], pa_ref[0, rows_b, 0:A_W])
        v_f = pa_ref[0, rows_f, A_W:2 * A_W].astype(BF16)
        v_b = pa_ref[0, rows_b, A_W:2 * A_W].astype(BF16)
        f_pre = both(pa_ref[0, rows_f, 2 * A_W:3 * A_W], pa_ref[0, rows_b, 3 * A_W:4 * A_W])
        q = _silu(q_pre)
        f = lb2 + (1.0 - lb2) * jax.nn.sigmoid(f_pre)
        log_f = jnp.log(jnp.maximum(f, F_MIN))
        kk = (1.0 - lb2) * jax.nn.sigmoid(-f_pre)
        cum = sum(jnp.dot(tri2, piece, preferred_element_type=F32) for piece in _split3(log_f))
        end_f = cum[c_sz - 1:c_sz, :]
        end_b = cum[c_sz:c_sz + 1, :]

        st_f = st_ref[0]
        st_b = st_ref[1]
        q_in = (q * jnp.exp(cum)).astype(BF16)
        o = both(_nt(q_in[0:c_sz], st_f.astype(BF16)), _nt(q_in[c_sz:2 * c_sz], st_b.astype(BF16)))
        k_end = (kk * jnp.exp(rows_of(end_f, end_b) - cum)).astype(BF16)
        st_ref[0] = st_f * jnp.exp(end_f) + jnp.where(same_head, _tn(v_f, k_end[0:c_sz]), 0.0)
        st_ref[1] = st_b * jnp.exp(end_b) + jnp.where(same_head, _tn(v_b, k_end[c_sz:2 * c_sz]), 0.0)

        def scores(qa, ka):
            return _nt(qa, jnp.where(same_head, jnp.concatenate([ka] * HEADS, axis=0), zero))

        refs = [both(boundary_rows(cum[0:c_sz], b, False), boundary_rows(cum[c_sz:2 * c_sz], b, True))
                for b, _, _ in levels]
        zs = [jnp.exp((cum - ref) * sign)
              for ref, (_, _, sign) in zip(refs, levels)]
        qas = [(q * z).astype(BF16) for z in zs]
        kas = [(kk * z).astype(BF16) for z in zs]
        prods = [both(scores(qa[0:c_sz], ka[0:c_sz]), scores(qa[c_sz:2 * c_sz], ka[c_sz:2 * c_sz]))
                 for qa, ka in zip(qas, kas)]
        p = self_mask * jnp.dot((q * kk).astype(BF16), head_ones, preferred_element_type=F32)
        for prod, (_, live, _) in zip(prods, levels):
            p = p + live * prod
        pb = p.astype(BF16)

        def values(p_half, v_half):
            return jnp.dot(p_half, jnp.where(same_head, jnp.concatenate([v_half] * HEADS, axis=0), zero),
                           preferred_element_type=F32)

        o_ref[0, rows_f, :] = o[0:c_sz] + values(pb[0:c_sz], v_f)
        ob_ref[rows_b, :] = o[c_sz:2 * c_sz] + values(pb[c_sz:2 * c_sz], v_b)
        return carry

    st_ref[...] = jnp.zeros_like(st_ref)
    lax.fori_loop(0, n_all, scan_step, 0)

    def readout(c, carry):
        rows = pl.ds(pl.multiple_of(c * ROW_TILE, ROW_TILE), ROW_TILE)
        tot = o_ref[0, rows, :] + ob_ref[rows, :]
        ms = sum(jnp.dot(piece, head_ones, preferred_element_type=F32)
                 for piece in _split3(tot * tot)) * (1.0 / HEAD_V)
        g = pa_ref[0, rows, 4 * A_W:5 * A_W]
        o_ref[0, rows, :] = tot * lax.rsqrt(ms + NORM_EPS) * gain_ref[...] * _silu(g)
        return carry

    lax.fori_loop(0, n_all * c_sz // ROW_TILE, readout, 0)


def _hgrn(pa3, lb, gain, n_ctx_rows):
    n_batch, rows, _ = pa3.shape
    kern = functools.partial(_hgrn_kernel, n_ctx=n_ctx_rows // CHUNK, n_all=rows // CHUNK)
    return pl.pallas_call(
        kern,
        out_shape=jax.ShapeDtypeStruct((n_batch, rows, A_W), F32),
        grid=(n_batch,),
        in_specs=[pl.BlockSpec((1, rows, PA_W), lambda b: (b, 0, 0)),
                  pl.BlockSpec((2, A_W), lambda b: (0, 0)),
                  pl.BlockSpec((1, A_W), lambda b: (0, 0))],
        out_specs=pl.BlockSpec((1, rows, A_W), lambda b: (b, 0, 0)),
        scratch_shapes=[pltpu.VMEM((2, A_W, A_W), F32), pltpu.VMEM((rows, A_W), F32)],
        compiler_params=_cparams(("arbitrary",)),
        name="hgrn2",
    )(pa3, lb, gain)


def _split3(x):
    hi = x.astype(BF16)
    rest = x - hi.astype(F32)
    mid = rest.astype(BF16)
    return hi, mid, (rest - mid.astype(F32)).astype(BF16)


def _mlstm_kernel(pb_ref, bias_ref, gain_ref, o_ref, cn_fwd_ref, cn_bwd_ref, ob_ref, *, n_ctx, n_all):
    assert CHUNK == HEAD_V
    c_sz, w, qk_w = CHUNK, B_W, HEADS * B_QK
    t_i = lax.broadcasted_iota(jnp.int32, (c_sz, w), 0)
    s_i = lax.broadcasted_iota(jnp.int32, (c_sz, w), 1) % c_sz
    diag4 = s_i == t_i
    half_lane = lax.broadcasted_iota(jnp.int32, (1, LANES), 1) < HEAD_V
    k_block = (lax.broadcasted_iota(jnp.int32, (HEADS * c_sz, qk_w), 0) // c_sz
               == lax.broadcasted_iota(jnp.int32, (HEADS * c_sz, qk_w), 1) // B_QK)
    v_block = (lax.broadcasted_iota(jnp.int32, (HEADS * c_sz, w), 0) // c_sz
               == lax.broadcasted_iota(jnp.int32, (HEADS * c_sz, w), 1) // HEAD_V)
    state_block = (lax.broadcasted_iota(jnp.int32, (qk_w, 2 * w), 0) // B_QK
                   == (lax.broadcasted_iota(jnp.int32, (qk_w, 2 * w), 1) % w) // HEAD_V)
    head_ones = v_block.astype(BF16)
    q_off, k_off, v_off, o_off, g_off = 0, qk_w, 2 * qk_w, 2 * qk_w + B_W, 2 * qk_w + 2 * B_W

    sel_r = lax.broadcasted_iota(jnp.int32, (2 * LANES, 2 * w), 0)
    sel_c = lax.broadcasted_iota(jnp.int32, (2 * LANES, 2 * w), 1)
    sel = (sel_r == jnp.where(sel_c < w, HEADS + sel_c // HEAD_V, LANES + (sel_c - w) // HEAD_V)).astype(BF16)

    def head_max(x):
        outs = []
        for col in range(w // LANES):
            xc = x[:, col * LANES:(col + 1) * LANES]
            lo = jnp.max(jnp.where(half_lane, xc, -jnp.inf), axis=-1, keepdims=True)
            hi = jnp.max(jnp.where(half_lane, -jnp.inf, xc), axis=-1, keepdims=True)
            outs.append(jnp.where(half_lane, lo, hi))
        return jnp.concatenate(outs, axis=1)

    mask2 =jnp.concatenate([s_i <= t_i, s_i >= t_i], axis=0)
    r2 = lax.broadcasted_iota(jnp.int32, (2 * c_sz, 2 * c_sz), 0)
    c2 = lax.broadcasted_iota(jnp.int32, (2 * c_sz, 2 * c_sz), 1)
    tri2 = jnp.logical_or(jnp.logical_and(r2 < c_sz, c2 <= r2),
                          jnp.logical_and(r2 >= c_sz, c2 >= r2)).astype(BF16)

    def per_half(fn, x):
        return jnp.concatenate([jnp.broadcast_to(fn(x[0:c_sz]), (c_sz, w)),
                                jnp.broadcast_to(fn(x[c_sz:2 * c_sz]), (c_sz, w))], axis=0)

    def rows_of(row_f, row_b):
        return jnp.concatenate([jnp.broadcast_to(row_f, (c_sz, w)), jnp.broadcast_to(row_b, (c_sz, w))], axis=0)

    def load(c_f, c_b, lo, hi):
        rf = pl.ds(pl.multiple_of(c_f * c_sz, c_sz), c_sz)
        rb = pl.ds(pl.multiple_of(c_b * c_sz, c_sz), c_sz)
        return pb_ref[0, rf, lo:hi], pb_ref[0, rb, lo:hi]

    def scan_step(n, carry):
        m_f, m_b = carry
        c_f = n
        c_b = jnp.where(n < n_ctx, n_ctx - 1 - n, n_all - 1 - (n - n_ctx))
        q_f, q_b = load(c_f, c_b, q_off, q_off + qk_w)
        k_f, k_b = load(c_f, c_b, k_off, k_off + qk_w)
        v_f, v_b = load(c_f, c_b, v_off, v_off + B_W)
        g_f, g_b = load(c_f, c_b, g_off, g_off + LANES)
        gts = jnp.concatenate([g_f + bias_ref[...],
                               pltpu.roll(g_b + bias_ref[...], LANES - 2 * HEADS, 1)], axis=0)
        log_f = jnp.minimum(gts, 0.0) - jnp.log(1.0 + jnp.exp(-jnp.abs(gts)))
        cum_col = sum(jnp.dot(tri2, p, preferred_element_type=F32) for p in _split3(log_f))
        both = sum(jnp.dot(p, sel, preferred_element_type=F32)
                   for p in _split3(jnp.concatenate([cum_col, gts], axis=1)))
        cum_t = both[:, :w]
        ig_t = both[:, w:]
        src = per_half(lambda x: jnp.sum(jnp.where(diag4, x, 0.0), axis=0, keepdims=True), cum_t - ig_t)
        m_prev = rows_of(m_f, m_b)
        log_d = jnp.where(mask2, cum_t - src, MASK_NEG)
        log_inter = cum_t + m_prev
        m_t = jnp.maximum(log_inter, head_max(log_d))
        w_intra = jnp.where(mask2, jnp.exp(log_d - m_t), 0.0)
        w_inter = jnp.exp(log_inter - m_t)

        scale = B_QK ** -0.5
        qb_f, qb_b = (q_f * scale).astype(BF16), (q_b * scale).astype(BF16)
        kb_f, kb_b = k_f.astype(BF16), k_b.astype(BF16)
        zero = jnp.zeros((), BF16)

        def scores(qb, kb):
            return _nt(qb, jnp.where(k_block, jnp.concatenate([kb] * HEADS, axis=0), zero))

        p = (jnp.concatenate([scores(qb_f, kb_f), scores(qb_b, kb_b)], axis=0) * w_intra).astype(BF16)

        def values(p_half, v_half):
            v_bd = jnp.where(v_block, jnp.concatenate([v_half.astype(BF16)] * HEADS, axis=0), zero)
            return jnp.dot(p_half, v_bd, preferred_element_type=F32)

        cn_f = cn_fwd_ref[...]
        cn_b = cn_bwd_ref[...]
        inter = jnp.concatenate([jnp.dot(qb_f, cn_f.astype(BF16), preferred_element_type=F32),
                                 jnp.dot(qb_b, cn_b.astype(BF16), preferred_element_type=F32)], axis=0)
        num = w_inter * inter[:, :w] + jnp.concatenate(
            [values(p[0:c_sz], v_f), values(p[c_sz:2 * c_sz], v_b)], axis=0)
        den = w_inter * inter[:, w:] + jnp.dot(p, head_ones, preferred_element_type=F32)
        h_out = num / jnp.maximum(jnp.abs(den), jnp.exp(-m_t))
        o_ref[0, pl.ds(pl.multiple_of(c_f * c_sz, c_sz), c_sz), :] = h_out[0:c_sz]
        ob_ref[pl.ds(pl.multiple_of(c_b * c_sz, c_sz), c_sz), :] = h_out[c_sz:2 * c_sz]

        end_f = cum_t[c_sz - 1:c_sz, :]
        end_b = cum_t[c_sz:c_sz + 1, :]
        log_end = rows_of(end_f, end_b) - cum_t + ig_t
        m_end_f = jnp.maximum(end_f + m_f, jnp.max(log_end[0:c_sz], axis=0, keepdims=True))
        m_end_b = jnp.maximum(end_b + m_b, jnp.max(log_end[c_sz:2 * c_sz], axis=0, keepdims=True))
        w_end = jnp.exp(log_end - rows_of(m_end_f, m_end_b))

        def update(cn_ref, cn, kb, w_half, v_half, carry_w):
            upd = _tn(kb, jnp.concatenate([(w_half * v_half).astype(BF16), w_half.astype(BF16)], axis=1))
            cn_ref[...] = jnp.concatenate([carry_w, carry_w], axis=1) * cn + jnp.where(state_block, upd, 0.0)

        update(cn_fwd_ref, cn_f, kb_f, w_end[0:c_sz], v_f, jnp.exp(end_f + m_f - m_end_f))
        update(cn_bwd_ref, cn_b, kb_b, w_end[c_sz:2 * c_sz], v_b, jnp.exp(end_b + m_b - m_end_b))
        return m_end_f, m_end_b

    zero_m = jnp.zeros((1, w), F32)
    cn_fwd_ref[...] = jnp.zeros_like(cn_fwd_ref)
    cn_bwd_ref[...] = jnp.zeros_like(cn_bwd_ref)
    lax.fori_loop(0, n_all, scan_step, (zero_m, zero_m), unroll=2)

    def readout(c, carry):
        rows = pl.ds(pl.multiple_of(c * ROW_TILE, ROW_TILE), ROW_TILE)
        tot = o_ref[0, rows, :] + ob_ref[rows, :]
        normed = jnp.concatenate(
            [_rms(tot[:, h * HEAD_V:(h + 1) * HEAD_V]) for h in range(HEADS)], axis=1) * gain_ref[...]
        og = pb_ref[0, rows, o_off:o_off + B_W]
        o_ref[0, rows, :] = normed * jax.nn.sigmoid(og)
        return carry

    lax.fori_loop(0, n_all * c_sz // ROW_TILE, readout, 0)


def _mlstm(pb3, bias, gain, n_ctx_rows):
    n_batch, rows, _ = pb3.shape
    kern = functools.partial(_mlstm_kernel, n_ctx=n_ctx_rows // CHUNK, n_all=rows // CHUNK)
    return pl.pallas_call(
        kern,
        out_shape=jax.ShapeDtypeStruct((n_batch, rows, B_W), F32),
        grid=(n_batch,),
        in_specs=[pl.BlockSpec((1, rows, PB_W), lambda b: (b, 0, 0)),
                  pl.BlockSpec((1, LANES), lambda b: (0, 0)),
                  pl.BlockSpec((1, B_W), lambda b: (0, 0))],
        out_specs=pl.BlockSpec((1, rows, B_W), lambda b: (b, 0, 0)),
        scratch_shapes=[pltpu.VMEM((HEADS * B_QK, 2 * B_W), F32), pltpu.VMEM((HEADS * B_QK, 2 * B_W), F32),
                        pltpu.VMEM((rows, B_W), F32)],
        compiler_params=_cparams(("arbitrary",)),
        name="mlstm",
    )(pb3, bias, gain)


def _attn_kernel(lam_ref, qp_ref, qr_ref, k_ref, v_ref, gain_ref, o_ref, *, n_ctx, q_tile0, lam_init):
    lam = lam_ref[0]
    q_tile = pl.program_id(2) + q_tile0
    lane = lax.broadcasted_iota(jnp.int32, (1, 2 * C_DQK), 1)

    n_all = k_ref.shape[1]

    def finish(head, parts):
        o = parts[0] - lam * parts[1]
        o_ref[0, :, head] = _rms(o) * gain_ref[...] * (1.0 - lam_init)

    def sub_query(q, j):
        return jnp.where(lane // C_DQK == j, q, jnp.zeros_like(q))

    def row_max(s):
        return jnp.max(s, axis=-1, keepdims=True)

    def row_sum(s):
        return jnp.sum(s, axis=-1, keepdims=True)

    def pv(ex, v):
        return jnp.dot(ex.astype(BF16), v, preferred_element_type=F32)

    heads = [slice(hh * C_DV, (hh + 1) * C_DV) for hh in range(ATTN_HEADS_PER_STEP)]

    def attend(pieces):
        problems = [(head, j) for head in heads for j in range(2)]
        scores = [[_nt(sub_query(q_ref[0, :, head], j), k_ref[0, k0:k1, head]) for q_ref, k0, k1 in pieces]
                  for head, j in problems]
        maxes = [functools.reduce(jnp.maximum, [row_max(s) for s in ss]) for ss in scores]
        exps = [[jnp.exp2(s - m) for s in ss] for ss, m in zip(scores, maxes)]
        outs = [sum(pv(e, v_ref[0, k0:k1, head]) for e, (_, k0, k1) in zip(es, pieces))
                / sum(row_sum(e) for e in es) for es, (head, _) in zip(exps, problems)]
        for n, head in enumerate(heads):
            finish(head, outs[2 * n:2 * n + 2])

    @pl.when(q_tile * ROW_TILE < n_ctx)
    def _():
        attend([(qp_ref, 0, n_ctx)])

    @pl.when(q_tile * ROW_TILE >= n_ctx)
    def _():
        attend([(qp_ref, 0, n_ctx), (qr_ref, n_ctx, n_all)])


def _attention(pc3, lam, gain, n_ctx_rows, lam_init, with_ctx):
    n_batch, rows, _ = pc3.shape
    q_tile0 = 0 if with_ctx else n_ctx_rows // ROW_TILE
    n_q = rows // ROW_TILE - q_tile0
    gw = ATTN_HEADS_PER_STEP * C_DV
    hb = C_W // gw
    kern = functools.partial(_attn_kernel, n_ctx=n_ctx_rows, q_tile0=q_tile0, lam_init=lam_init)
    grid_spec = pltpu.PrefetchScalarGridSpec(
        num_scalar_prefetch=1,
        grid=(n_batch, hb, n_q),
        in_specs=[pl.BlockSpec((1, ROW_TILE, gw), lambda b, h, i, lam: (b, i + q_tile0, h)),
                  pl.BlockSpec((1, ROW_TILE, gw), lambda b, h, i, lam: (b, i + q_tile0, hb + h)),
                  pl.BlockSpec((1, rows, gw), lambda b, h, i, lam: (b, 0, 2 * hb + h)),
                  pl.BlockSpec((1, rows, gw), lambda b, h, i, lam: (b, 0, 3 * hb + h)),
                  pl.BlockSpec((1, C_DV), lambda b, h, i, lam: (0, 0))],
        out_specs=pl.BlockSpec((1, ROW_TILE, gw), lambda b, h, i, lam: (b, i, h)),
    )
    return pl.pallas_call(
        kern,
        out_shape=jax.ShapeDtypeStruct((n_batch, n_q * ROW_TILE, C_W), F32),
        grid_spec=grid_spec,
        compiler_params=_cparams(("arbitrary", "arbitrary", "arbitrary")),
        name="diff_attn",
    )(lam, pc3, pc3, pc3, pc3, gain)


def _outproj_kernel(oa_ref, ob_ref, oc_ref, x_ref, mod_ref, n_ref, w_ref, rw_ref, rb_ref,
                    xn_ref, h_ref, route_ref, cnt_ref, carry_ref):
    d = x_ref.shape[1]
    n_exp = rw_ref.shape[1]

    @pl.when(pl.program_id(0) == 0)
    def _():
        carry_ref[...] = jnp.zeros_like(carry_ref)

    mix = (jnp.dot(oa_ref[...].astype(BF16), w_ref[0:A_W, :], preferred_element_type=F32)
           + jnp.dot(ob_ref[...].astype(BF16), w_ref[A_W:A_W + B_W, :], preferred_element_type=F32)
           + jnp.dot(oc_ref[...].astype(BF16), w_ref[A_W + B_W:A_W + B_W + C_W, :], preferred_element_type=F32))
    xn = x_ref[...] + mod_ref[0, :, 2 * d:3 * d] * (_rms(mix) * n_ref[1:2, :])
    xn_ref[...] = xn
    h = _rms(xn) * n_ref[2:3, :] * (1.0 + mod_ref[0, :, 4 * d:5 * d]) + mod_ref[0, :, 3 * d:4 * d]
    h_ref[...] = _pack_bf16_pairs(h)

    h_hi = h.astype(BF16)
    h_mid = (h - h_hi.astype(F32)).astype(BF16)
    logits = _nt(rw_ref[0], h_hi) + _nt(rw_ref[1], h_hi) + _nt(rw_ref[0], h_mid) + rb_ref[...]
    e_sub = lax.broadcasted_iota(jnp.int32, logits.shape, 0)
    cur = logits
    picks, vals = [], []
    for _ in range(TOP_K):
        mx = jnp.max(cur, axis=0, keepdims=True)
        idx = jnp.min(jnp.where(cur == mx, e_sub, n_exp), axis=0, keepdims=True)
        hit = e_sub == idx
        cur = jnp.where(hit, -jnp.inf, cur)
        picks.append((idx, hit.astype(F32)))
        vals.append(mx)
    exps = [jnp.exp(vv - vals[0]) for vv in vals]
    total = exps[0] + exps[1] + exps[2] + exps[3]

    chosen = picks[0][1] + picks[1][1] + picks[2][1] + picks[3][1]
    tm = logits.shape[1]
    before = (lax.broadcasted_iota(jnp.int32, (tm, tm), 0) < lax.broadcasted_iota(jnp.int32, (tm, tm), 1))
    seen = jnp.dot(chosen.astype(BF16), before.astype(BF16), preferred_element_type=F32) + carry_ref[...]
    new_carry = carry_ref[...] + jnp.sum(chosen, axis=1, keepdims=True)
    carry_ref[...] = new_carry
    cnt_ref[...] = new_carry

    o_row = lax.broadcasted_iota(jnp.int32, route_ref.shape, 0)
    route = jnp.zeros(route_ref.shape, F32)
    for kk in range(TOP_K):
        idx, hit = picks[kk]
        rank = jnp.sum(hit * seen, axis=0, keepdims=True)
        route = (route + jnp.where(o_row == kk, idx.astype(F32), 0.0)
                 + jnp.where(o_row == TOP_K + kk, exps[kk] / total, 0.0)
                 + jnp.where(o_row == 2 * TOP_K + kk, rank, 0.0))
    route_ref[...] = route


ROUTE_ROWS = 16


def _outproj(oa, ob, oc, xy, mod, norms, w_out_b, router_wt, router_b, layer, n_tiles, in_tile, oc_tile, mod_row):
    d = xy.shape[1]
    n_exp = router_wt.shape[2]
    t_out = n_tiles * ROW_TILE
    xn, h, route_t, counts = pl.pallas_call(
        _outproj_kernel,
        out_shape=(jax.ShapeDtypeStruct((t_out, d), F32),
                   jax.ShapeDtypeStruct((t_out, d // 2), jnp.int32),
                   jax.ShapeDtypeStruct((n_tiles * ROUTE_ROWS, ROW_TILE), F32),
                   jax.ShapeDtypeStruct((n_exp, 1), F32)),
        grid=(n_tiles,),
        in_specs=[pl.BlockSpec((ROW_TILE, A_W), lambda i: (in_tile(i), 0)),
                  pl.BlockSpec((ROW_TILE, B_W), lambda i: (in_tile(i), 0)),
                  pl.BlockSpec((ROW_TILE, C_W), lambda i: (oc_tile(i), 0)),
                  pl.BlockSpec((ROW_TILE, d), lambda i: (in_tile(i), 0)),
                  pl.BlockSpec((1, 1, 6 * d), lambda i: (mod_row(i), 0, 0)),
                  pl.BlockSpec((None, 4, d), lambda i: (layer, 0, 0)),
                  pl.BlockSpec((None, d, d), lambda i: (layer, 0, 0)),
                  pl.BlockSpec((None, 2, n_exp, d), lambda i: (layer, 0, 0, 0)),
                  pl.BlockSpec((None, n_exp, 1), lambda i: (layer, 0, 0))],
        out_specs=(pl.BlockSpec((ROW_TILE, d), lambda i: (i, 0)),
                   pl.BlockSpec((ROW_TILE, d // 2), lambda i: (i, 0)),
                   pl.BlockSpec((ROUTE_ROWS, ROW_TILE), lambda i: (i, 0)),
                   pl.BlockSpec((n_exp, 1), lambda i: (0, 0))),
        scratch_shapes=[pltpu.VMEM((n_exp, 1), F32)],
        compiler_params=_cparams(("arbitrary",)),
        name="outproj_router",
    )(oa, ob, oc, xy, mod, norms, w_out_b, router_wt, router_b)
    return xn, h, route_t.reshape(n_tiles, ROUTE_ROWS, ROW_TILE), counts[:, 0]


PAIR_BLOCK = 2 * LANES


def _moe_kernel(te_ref, first_ref, valid_ref, nu_ref, slot_ref, next_ref, xs_ref, w1_hbm, b1_ref, w2_hbm, b2_ref,
                ys_ref, w1p_ref, w2b_ref, w1_buf, w2_buf, w_sem, *, layer):
    i = pl.program_id(0)
    two_f = w1_buf.shape[2]
    n_blk = two_f // PAIR_BLOCK

    def fetch(expert, slot):
        return (pltpu.make_async_copy(w1_hbm.at[layer, expert], w1_buf.at[slot], w_sem.at[0, slot]),
                pltpu.make_async_copy(w2_hbm.at[layer, expert], w2_buf.at[slot], w_sem.at[1, slot]))

    @pl.when(i == 0)
    def _():
        for copy in fetch(te_ref[0], 0):
            copy.start()

    @pl.when(jnp.logical_and(i < nu_ref[0], first_ref[i] == 1))
    def _():
        slot = slot_ref[i]
        for copy in fetch(te_ref[i], slot):
            copy.wait()
        r = lax.broadcasted_iota(jnp.int32, (PAIR_BLOCK, PAIR_BLOCK), 0)
        c = lax.broadcasted_iota(jnp.int32, (PAIR_BLOCK, PAIR_BLOCK), 1)
        perm = (r == jnp.where(c < LANES, 2 * c, 2 * (c - LANES) + 1)).astype(BF16)
        for blk in range(n_blk):
            cols = slice(blk * PAIR_BLOCK, (blk + 1) * PAIR_BLOCK)
            w1p_ref[:, cols] = jnp.dot(w1_buf[slot, :, cols].astype(BF16), perm,
                                       preferred_element_type=F32).astype(BF16)
        w2b_ref[...] = w2_buf[slot].astype(BF16)

        @pl.when(next_ref[i] >= 0)
        def _():
            for copy in fetch(next_ref[i], 1 - slot):
                copy.start()

    tile_rows = xs_ref.shape[0]
    used = i < nu_ref[0]
    valid = valid_ref[i]

    def expert_ffn(n_rows):
        x = _unpack_bf16_pairs(xs_ref[0:n_rows, :]).astype(BF16)
        row = lax.broadcasted_iota(jnp.int32, (n_rows, 1), 0)
        x = jnp.where(row < valid, x, jnp.zeros_like(x))
        hid = jnp.dot(x, w1p_ref[...], preferred_element_type=F32) + b1_ref[...]
        acts = []
        for blk in range(n_blk):
            glu = jnp.minimum(hid[:, blk * PAIR_BLOCK:blk * PAIR_BLOCK + LANES], SWIGLU_LIMIT)
            lin = jnp.clip(hid[:, blk * PAIR_BLOCK + LANES:(blk + 1) * PAIR_BLOCK], -SWIGLU_LIMIT, SWIGLU_LIMIT)
            acts.append((glu * jax.nn.sigmoid(SWIGLU_ALPHA * glu) * (lin + 1.0)).astype(BF16))
        y = jnp.dot(jnp.concatenate(acts, axis=1), w2b_ref[...], preferred_element_type=F32) + b2_ref[...]
        ys_ref[0:n_rows, :] = _pack_bf16_pairs(y)
        if n_rows < tile_rows:
            ys_ref[n_rows:tile_rows, :] = jnp.zeros((tile_rows - n_rows, ys_ref.shape[1]), ys_ref.dtype)

    @pl.when(jnp.logical_and(used, valid > tile_rows // 2))
    def _():
        expert_ffn(tile_rows)

    @pl.when(jnp.logical_and(used, valid <= tile_rows // 2))
    def _():
        expert_ffn(tile_rows // 2)

    @pl.when(jnp.logical_not(used))
    def _():
        ys_ref[...] = jnp.zeros_like(ys_ref)


def _moe(plan, xs, w1, b1p, w2, b2, layer):
    r_max, half_d = xs.shape
    d = 2 * half_d
    two_f = w1.shape[3]
    f = two_f // 2
    n_tiles = r_max // MOE_TILE

    def row_tile(i, te, fi, va, nu, sl, nx):
        return (jnp.maximum(jnp.minimum(i, nu[0] - 1), 0), 0)

    def bias_block(i, te, fi, va, nu, sl, nx):
        return (layer, te[i], 0, 0)

    grid_spec = pltpu.PrefetchScalarGridSpec(
        num_scalar_prefetch=6,
        grid=(n_tiles,),
        in_specs=[pl.BlockSpec((MOE_TILE, half_d), row_tile),
                  pl.BlockSpec(memory_space=pl.ANY),
                  pl.BlockSpec((None, None, 1, two_f), bias_block),
                  pl.BlockSpec(memory_space=pl.ANY),
                  pl.BlockSpec((None, None, 1, d), bias_block)],
        out_specs=pl.BlockSpec((MOE_TILE, half_d), lambda i, te, fi, va, nu, sl, nx: (i, 0)),
        scratch_shapes=[pltpu.VMEM((d, two_f), BF16), pltpu.VMEM((f, d), BF16),
                        pltpu.VMEM((2, d, two_f), F32), pltpu.VMEM((2, f, d), F32),
                        pltpu.SemaphoreType.DMA((2, 2))],
    )
    return pl.pallas_call(
        functools.partial(_moe_kernel, layer=layer),
        out_shape=jax.ShapeDtypeStruct((r_max, half_d), jnp.int32),
        grid_spec=grid_spec,
        compiler_params=_cparams(("arbitrary",)),
        name="moe_experts",
    )(*plan, xs, w1, b1p, w2, b2)


SC_CORES = 2
SC_SUBCORES = 16
SC_CHUNK = 64


def _dispatch_rows(hp, pos, r_max):
    t, width = hp.shape
    workers = SC_CORES * SC_SUBCORES
    assert t % (workers * SC_CHUNK) == 0
    per_worker = t // (workers * SC_CHUNK)
    idx = pos.reshape(-1, TOP_K, ROW_TILE // SC_CHUNK, SC_CHUNK).transpose(0, 2, 1, 3).reshape(
        workers, per_worker * TOP_K, SC_CHUNK)
    mesh = plsc.VectorSubcoreMesh(core_axis_name="c", subcore_axis_name="s",
                                  num_cores=SC_CORES, num_subcores=SC_SUBCORES)

    @functools.partial(
        pl.kernel, mesh=mesh,
        out_type=jax.ShapeDtypeStruct((r_max, width), hp.dtype),
        scratch_types=[pltpu.VMEM((per_worker * TOP_K, SC_CHUNK), jnp.int32),
                       pltpu.VMEM((2, SC_CHUNK, width), hp.dtype),
                       pltpu.SemaphoreType.DMA((2,)),
                       pltpu.SemaphoreType.DMA((2,))],
    )
    def scatter(hp_hbm, idx_hbm, out_hbm, idx_v, rows_v, read_sem, write_sem):
        wid = lax.axis_index("s") * SC_CORES + lax.axis_index("c")
        pltpu.sync_copy(idx_hbm.at[wid], idx_v)

        def read(j):
            row0 = pl.multiple_of((wid * per_worker + j) * SC_CHUNK, SC_CHUNK)
            return pltpu.make_async_copy(hp_hbm.at[pl.ds(row0, SC_CHUNK)], rows_v.at[j % 2], read_sem.at[j % 2])

        def write(j, k):
            return pltpu.make_async_copy(rows_v.at[j % 2], out_hbm.at[idx_v.at[j * TOP_K + k]],
                                         write_sem.at[j % 2])

        read(0).start()
        for j in range(per_worker):
            read(j).wait()
            for k in range(TOP_K):
                write(j, k).start()
            if j + 1 < per_worker:
                if j >= 1:
                    for k in range(TOP_K):
                        write(j - 1, k).wait()
                read(j + 1).start()
        for j in range(max(per_worker - 2, 0), per_worker):
            for k in range(TOP_K):
                write(j, k).wait()

    return scatter(hp, idx)


def _combine_rows(ysp, pos):
    width = ysp.shape[1]
    t = pos.shape[0] * pos.shape[2]
    workers = SC_CORES * SC_SUBCORES
    assert t % (workers * SC_CHUNK) == 0
    per_worker = t // (workers * SC_CHUNK)
    units = per_worker * TOP_K
    idx = pos.reshape(-1, TOP_K, ROW_TILE // SC_CHUNK, SC_CHUNK).transpose(0, 2, 1, 3).reshape(
        workers, units, SC_CHUNK)
    mesh = plsc.VectorSubcoreMesh(core_axis_name="c", subcore_axis_name="s",
                                  num_cores=SC_CORES, num_subcores=SC_SUBCORES)

    @functools.partial(
        pl.kernel, mesh=mesh,
        out_type=jax.ShapeDtypeStruct((TOP_K, t, width), ysp.dtype),
        scratch_types=[pltpu.VMEM((units, SC_CHUNK), jnp.int32),
                       pltpu.VMEM((2, SC_CHUNK, width), ysp.dtype),
                       pltpu.SemaphoreType.DMA((2,)),
                       pltpu.SemaphoreType.DMA((2,))],
    )
    def gather(ys_hbm, idx_hbm, out_hbm, idx_v, rows_v, read_sem, write_sem):
        wid = lax.axis_index("s") * SC_CORES + lax.axis_index("c")
        pltpu.sync_copy(idx_hbm.at[wid], idx_v)

        def read(u):
            return pltpu.make_async_copy(ys_hbm.at[idx_v.at[u]], rows_v.at[u % 2], read_sem.at[u % 2])

        def write(u):
            row0 = pl.multiple_of((wid * per_worker + u // TOP_K) * SC_CHUNK, SC_CHUNK)
            return pltpu.make_async_copy(rows_v.at[u % 2], out_hbm.at[u % TOP_K, pl.ds(row0, SC_CHUNK)],
                                         write_sem.at[u % 2])

        read(0).start()
        for u in range(units):
            read(u).wait()
            write(u).start()
            if u + 1 < units:
                if u >= 1:
                    write(u - 1).wait()
                read(u + 1).start()
        for u in range(max(units - 2, 0), units):
            write(u).wait()

    return gather(ysp, idx)


def _ffn_residual_kernel(x_ref, w_ref, y0_ref, y1_ref, y2_ref, y3_ref, mod_ref, n_ref, o_ref):
    d = x_ref.shape[1]
    wts = w_ref[...]
    ffn = None
    for k, y_ref in enumerate((y0_ref, y1_ref, y2_ref, y3_ref)):
        term = wts[:, k:k + 1] * _unpack_bf16_pairs(y_ref[...])
        ffn = term if ffn is None else ffn + term
    o_ref[...] = x_ref[...] + mod_ref[0, :, 5 * d:6 * d] * (_rms(ffn) * n_ref[3:4, :])


def _ffn_residual(xn, wts, ys_by_k, mod, norms, layer, mod_row):
    t, d = xn.shape
    row_spec = pl.BlockSpec((ROW_TILE, d), lambda i: (i, 0))

    def y_spec(k):
        return pl.BlockSpec((None, ROW_TILE, d // 2), lambda i: (k, i, 0))

    return pl.pallas_call(
        _ffn_residual_kernel,
        out_shape=jax.ShapeDtypeStruct((t, d), F32),
        grid=(t // ROW_TILE,),
        in_specs=[row_spec,
                  pl.BlockSpec((ROW_TILE, TOP_K), lambda i: (i, 0)),
                  y_spec(0), y_spec(1), y_spec(2), y_spec(3),
                  pl.BlockSpec((1, 1, 6 * d), lambda i: (mod_row(i), 0, 0)),
                  pl.BlockSpec((None, 4, d), lambda i: (layer, 0, 0))],
        out_specs=row_spec,
        compiler_params=_cparams(("arbitrary",)),
        name="ffn_residual",
    )(xn, wts, ys_by_k, ys_by_k, ys_by_k, ys_by_k, mod, norms)


def _dispatch_plan(route_t, counts, r_max):
    n_row_tiles = route_t.shape[0]
    idx = route_t[:, 0:TOP_K, :].astype(jnp.int32)
    wts = route_t[:, TOP_K:2 * TOP_K, :].transpose(0, 2, 1).reshape(n_row_tiles * ROW_TILE, TOP_K)
    rank = route_t[:, 2 * TOP_K:3 * TOP_K, :].astype(jnp.int32)
    cnt = counts.astype(jnp.int32)
    padded = (cnt + MOE_TILE - 1) // MOE_TILE * MOE_TILE
    ends = jnp.cumsum(padded)
    starts = ends - padded
    pos = rank
    for e in range(cnt.shape[0]):
        pos = pos + jnp.where(idx == e, starts[e], 0)
    n_tiles = r_max // MOE_TILE
    n_used = ends[-1] // MOE_TILE
    tile_ids = jnp.minimum(jnp.arange(n_tiles, dtype=jnp.int32), n_used - 1)
    tile_expert = jnp.sum((ends // MOE_TILE)[None, :] <= tile_ids[:, None], axis=1).astype(jnp.int32)
    tile_first = jnp.concatenate(
        [jnp.ones((1,), jnp.int32), (tile_expert[1:] != tile_expert[:-1]).astype(jnp.int32)])
    of_expert = (tile_expert[:, None] == jnp.arange(cnt.shape[0], dtype=jnp.int32)[None, :]).astype(jnp.int32)
    first_tile = jnp.sum(of_expert * ((ends - padded) // MOE_TILE)[None, :], axis=1)
    tile_valid = jnp.clip(jnp.sum(of_expert * cnt[None, :], axis=1) - (tile_ids - first_tile) * MOE_TILE,
                          0, MOE_TILE).astype(jnp.int32)
    experts = jnp.arange(cnt.shape[0], dtype=jnp.int32)
    has_rows = cnt > 0
    tile_slot = (jnp.sum(of_expert * (jnp.cumsum(has_rows.astype(jnp.int32)) - 1)[None, :], axis=1) % 2).astype(jnp.int32)
    later_with_rows = jnp.logical_and(has_rows[None, :], experts[None, :] > experts[:, None])
    next_expert = jnp.min(jnp.where(later_with_rows, experts[None, :], cnt.shape[0]), axis=1)
    next_expert = jnp.where(next_expert < cnt.shape[0], next_expert, -1)
    tile_next = jnp.sum(of_expert * next_expert[None, :], axis=1).astype(jnp.int32)
    plan = (tile_expert, tile_first, tile_valid, n_used.reshape(1).astype(jnp.int32), tile_slot, tile_next)
    return pos, wts, plan


def _rope_tables(length, n_ctx_rows):
    rows = length // GRID_W
    row = jnp.repeat(jnp.arange(rows, dtype=F32), GRID_W)
    col = jnp.tile(jnp.arange(GRID_W, dtype=F32), rows)
    n_freq = C_DQK // 4
    inv_freq = ROPE_THETA ** (-jnp.arange(n_freq, dtype=F32) / n_freq)
    ang_r = row[:, None] * inv_freq
    ang_c = col[:, None] * inv_freq
    cos = jnp.concatenate([jnp.cos(ang_r), jnp.cos(ang_r), jnp.cos(ang_c), jnp.cos(ang_c)], axis=-1)
    sin = jnp.concatenate([-jnp.sin(ang_r), jnp.sin(ang_r), -jnp.sin(ang_c), jnp.sin(ang_c)], axis=-1)
    reps = C_W // C_DQK
    cos = jnp.concatenate([jnp.ones((n_ctx_rows, C_W), F32), jnp.tile(cos, (1, reps))], axis=0)
    sin = jnp.concatenate([jnp.zeros((n_ctx_rows, C_W), F32), jnp.tile(sin, (1, reps))], axis=0)
    return cos, sin


def _hgrn_lower_bound(table, layer):
    p = jax.nn.softmax(table.astype(F32), axis=1)
    cum = jnp.cumsum(p, axis=1) - p[:, :1]
    return jnp.clip(cum[:, layer], 0.0, 1.0)


def kernel(x, c, ctx, c_ctx, ada_w, ada_b, sandwich_norms, w_in, w_out, hgrn_lower_bounds, hgrn_norm,
           mlstm_gate_bias, mlstm_norm, diff_lambdas, diff_norm, router_w, router_b, moe_w1, moe_b1,
           moe_w2, moe_b2):
    n_batch, seq, d = x.shape
    n_ctx_rows = ctx.shape[1]
    depth = w_in.shape[0]
    n_exp = router_w.shape[2]
    assert seq % ROW_TILE == 0 and n_ctx_rows % ROW_TILE == 0 and seq % GRID_W == 0
    rows_b = n_ctx_rows + seq
    tiles_b = rows_b // ROW_TILE
    ctx_tiles = n_ctx_rows // ROW_TILE
    lat_tiles = seq // ROW_TILE

    b_main = PA_W + 2 * HEADS * B_QK + 2 * B_W
    w_in_p = jnp.concatenate(
        [w_in[:, :, :b_main],
         jnp.pad(w_in[:, :, b_main:b_main + N_GATE], ((0, 0), (0, 0), (0, LANES - N_GATE))),
         w_in[:, :, b_main + N_GATE:]], axis=2).astype(BF16)
    w_out_b = w_out.astype(BF16)
    two_f = moe_b1.shape[2]
    b1p = moe_b1.reshape(depth, n_exp, two_f // PAIR_BLOCK, LANES, 2).transpose(0, 1, 2, 4, 3).reshape(
        depth, n_exp, 1, two_f)
    b2 = moe_b2[:, :, None, :]
    ada_b3 = ada_b[:, None, :]
    router_b3 = router_b[:, :, None]
    rw_t = router_w.transpose(0, 2, 1)
    rw_hi = rw_t.astype(BF16)
    router_wt = jnp.stack([rw_hi, (rw_t - rw_hi.astype(F32)).astype(BF16)], axis=1)
    gate_bias = jnp.pad(mlstm_gate_bias, ((0, 0), (0, LANES - N_GATE)))
    cos, sin = _rope_tables(seq, n_ctx_rows)

    cond_rows = (n_batch + 1 + 7) // 8 * 8
    cvec = jnp.zeros((cond_rows, d), F32).at[:n_batch].set(c).at[n_batch].set(c_ctx)

    def mod_row_all(i):
        return jnp.where(i % tiles_b < ctx_tiles, n_batch, i // tiles_b)

    xy = jnp.concatenate([ctx, x], axis=1).reshape(n_batch * rows_b, d)

    for layer in range(depth):
        last = layer == depth - 1
        mod = _adaln(cvec, ada_w, ada_b3, layer).reshape(cond_rows, 1, 6 * d)
        lb = _hgrn_lower_bound(hgrn_lower_bounds, layer)
        lam_init = 0.8 - 0.6 * math.exp(-0.3 * layer)
        lq1, lk1, lq2, lk2 = diff_lambdas[layer].astype(F32)
        lam = (jnp.exp(jnp.sum(lq1 * lk1)) - jnp.exp(jnp.sum(lq2 * lk2)) + lam_init).reshape(1)

        pa, pb, pc = _inproj(xy, mod, sandwich_norms, w_in_p, cos, sin, layer, tiles_b, ctx_tiles, n_batch)
        oa = _hgrn(pa.reshape(n_batch, rows_b, PA_W), lb, jnp.tile(hgrn_norm[layer], HEADS)[None, :], n_ctx_rows)
        ob = _mlstm(pb.reshape(n_batch, rows_b, PB_W), gate_bias[layer][None, :], mlstm_norm[layer][None, :],
                    n_ctx_rows)
        oc = _attention(pc.reshape(n_batch, rows_b, PC_OUT), lam, diff_norm[layer][None, :], n_ctx_rows,
                        lam_init, not last)

        if last:
            n_tiles = n_batch * lat_tiles
            in_tile = lambda i: (i // lat_tiles) * tiles_b + ctx_tiles + i % lat_tiles
            mod_row = lambda i: i // lat_tiles
        else:
            n_tiles = n_batch * tiles_b
            in_tile = lambda i: i
            mod_row = mod_row_all
        xn, h, route, counts = _outproj(
            oa.reshape(-1, A_W), ob.reshape(-1, B_W), oc.reshape(-1, C_W), xy, mod, sandwich_norms, w_out_b,
            router_wt, router_b3, layer, n_tiles, in_tile, lambda i: i, mod_row)

        r_max = n_tiles * ROW_TILE * TOP_K + n_exp * MOE_TILE
        pos, wts, plan = _dispatch_plan(route, counts, r_max)
        xs = _dispatch_rows(h, pos, r_max)
        ys = _moe(plan, xs, moe_w1, b1p, moe_w2, b2, layer)
        xy = _ffn_residual(xn, wts, _combine_rows(ys, pos), mod, sandwich_norms, layer, mod_row)

    return xy.reshape(n_batch, seq, d)
```

```python
import functools
import math

import jax
import jax.numpy as jnp
from jax import lax
from jax.experimental import pallas as pl
from jax.experimental.pallas import tpu as pltpu
from jax.experimental.pallas import tpu_sc as plsc

F32 = jnp.float32
BF16 = jnp.bfloat16
HI = lax.Precision.HIGHEST

HEADS = 4
A_W = 256
B_QK = 32
B_W = 256
C_DQK = 64
C_DV = 2 * C_DQK
ATTN_KEY_SCALE = C_DQK ** -0.5 * math.log2(math.e)
C_W = 512
HEAD_V = 64
N_GATE = 16
GRID_W = 64
TOP_K = 4
SWIGLU_ALPHA = 1.702
SWIGLU_LIMIT = 7.0
ROPE_THETA = 10000.0
NORM_EPS = 1e-6
MASK_NEG = -1e30
F_MIN = 1e-12

LANES = 128
ROW_TILE = 256
CHUNK = 64
SCAN_BATCHES = 2
MOE_TILE = 512
ATTN_HEADS_PER_STEP = 2
VMEM_LIMIT = 56 * 1024 * 1024

PA_W = 5 * A_W
PB_W = 2 * HEADS * B_QK + 2 * B_W + LANES
PC_IN = 3 * C_W
PC_OUT = 4 * C_W
W_IN_PAD = PA_W + PB_W + PC_IN


def _cparams(sem):
    return pltpu.CompilerParams(dimension_semantics=sem, vmem_limit_bytes=VMEM_LIMIT)


def _nt(a, b):
    return lax.dot_general(a, b, (((1,), (1,)), ((), ())), preferred_element_type=F32)


def _tn(a, b, precision=None):
    return lax.dot_general(a, b, (((0,), (0,)), ((), ())), preferred_element_type=F32, precision=precision)


def _rms(x):
    return x * lax.rsqrt(jnp.mean(x * x, axis=-1, keepdims=True) + NORM_EPS)


def _silu(x):
    return x * jax.nn.sigmoid(x)


def _pack_bf16_pairs(x):
    half = x.shape[1] // 2
    bits = pltpu.bitcast(x.astype(BF16).astype(F32), jnp.uint32)
    return pltpu.bitcast((bits[:, :half] >> 16) | (bits[:, half:] & jnp.uint32(0xFFFF0000)), jnp.int32)


def _unpack_bf16_pairs(words):
    bits = pltpu.bitcast(words, jnp.uint32)
    return jnp.concatenate([pltpu.bitcast(bits << 16, F32),
                            pltpu.bitcast(bits & jnp.uint32(0xFFFF0000), F32)], axis=1)


def _adaln_kernel(c_ref, w_ref, b_ref, o_ref):
    cond = _silu(c_ref[...])
    o_ref[...] = jnp.dot(cond, w_ref[...], preferred_element_type=F32, precision=HI) + b_ref[...]


def _adaln(cvec, ada_w, ada_b, layer):
    rows, d = cvec.shape
    return pl.pallas_call(
        _adaln_kernel,
        out_shape=jax.ShapeDtypeStruct((rows, 6 * d), F32),
        grid=(6,),
        in_specs=[pl.BlockSpec((rows, d), lambda j: (0, 0)),
                  pl.BlockSpec((None, d, d), lambda j: (layer, 0, j)),
                  pl.BlockSpec((None, 1, d), lambda j: (layer, 0, j))],
        out_specs=pl.BlockSpec((rows, d), lambda j: (0, j)),
        compiler_params=_cparams(("arbitrary",)),
        name="adaln",
    )(cvec, ada_w, ada_b)


def _inproj_kernel(x_ref, mod_ref, g_ref, w_ref, cos_ref, sin_ref, pa_ref, pb_ref, pc_ref):
    d = x_ref.shape[1]
    shift = mod_ref[0, :, 0:d]
    scale = mod_ref[0, :, d:2 * d]
    h = _rms(x_ref[...]) * g_ref[0:1, :] * (1.0 + scale) + shift
    hb = h.astype(BF16)
    pa_ref[...] = jnp.dot(hb, w_ref[:, 0:PA_W], preferred_element_type=F32)
    pb_ref[...] = jnp.dot(hb, w_ref[:, PA_W:PA_W + PB_W], preferred_element_type=F32)
    pc = jnp.dot(hb, w_ref[:, PA_W + PB_W:W_IN_PAD], preferred_element_type=F32)
    q = pc[:, 0:C_W]
    k = pc[:, C_W:2 * C_W] * ATTN_KEY_SCALE
    cos = cos_ref[...]
    sin = sin_ref[...]
    lane = lax.broadcasted_iota(jnp.int32, q.shape, 1)
    first = (lane % 32) < 16

    def rope(t):
        partner = jnp.where(first, pltpu.roll(t, C_W - 16, 1), pltpu.roll(t, 16, 1))
        return t * cos + partner * sin

    pc_ref[:, 0:C_W] = q.astype(BF16)
    pc_ref[:, C_W:2 * C_W] = rope(q).astype(BF16)
    pc_ref[:, 2 * C_W:3 * C_W] = rope(k).astype(BF16)
    pc_ref[:, 3 * C_W:4 * C_W] = pc[:, 2 * C_W:3 * C_W].astype(BF16)


def _inproj(xy, mod, norms, w_in_p, cos, sin, layer, tiles_per_batch, n_ctx_tiles, n_batch):
    t_all, d = xy.shape
    n_tiles = t_all // ROW_TILE

    def mod_row(i):
        return jnp.where(i % tiles_per_batch < n_ctx_tiles, n_batch, i // tiles_per_batch)

    return pl.pallas_call(
        _inproj_kernel,
        out_shape=(jax.ShapeDtypeStruct((t_all, PA_W), F32),
                   jax.ShapeDtypeStruct((t_all, PB_W), F32),
                   jax.ShapeDtypeStruct((t_all, PC_OUT), BF16)),
        grid=(n_tiles,),
        in_specs=[pl.BlockSpec((ROW_TILE, d), lambda i: (i, 0)),
                  pl.BlockSpec((1, 1, 6 * d), lambda i: (mod_row(i), 0, 0)),
                  pl.BlockSpec((None, 4, d), lambda i: (layer, 0, 0)),
                  pl.BlockSpec((None, d, W_IN_PAD), lambda i: (layer, 0, 0)),
                  pl.BlockSpec((ROW_TILE, C_W), lambda i: (i % tiles_per_batch, 0)),
                  pl.BlockSpec((ROW_TILE, C_W), lambda i: (i % tiles_per_batch, 0))],
        out_specs=(pl.BlockSpec((ROW_TILE, PA_W), lambda i: (i, 0)),
                   pl.BlockSpec((ROW_TILE, PB_W), lambda i: (i, 0)),
                   pl.BlockSpec((ROW_TILE, PC_OUT), lambda i: (i, 0))),
        compiler_params=_cparams(("arbitrary",)),
        name="inproj",
    )(xy, mod, norms, w_in_p, cos, sin)


def _hgrn_kernel(pa_ref, lb_ref, gain_ref, o_ref, st_ref, ob_ref, *, n_ctx, n_all):
    c_sz, w = CHUNK, A_W
    same_head = (lax.broadcasted_iota(jnp.int32, (w, w), 0) // HEAD_V
                 == lax.broadcasted_iota(jnp.int32, (w, w), 1) // HEAD_V)
    head_ones = same_head.astype(BF16)
    assert c_sz == HEAD_V
    t_i = lax.broadcasted_iota(jnp.int32, (c_sz, w), 0)
    s_i = lax.broadcasted_iota(jnp.int32, (c_sz, w), 1) % c_sz
    t_row = lax.broadcasted_iota(jnp.int32, (c_sz, 1), 0)
    row8 = t_row % 8

    def grouped_rows(a, k):
        return jnp.concatenate(
            [jnp.broadcast_to(a[8 * j + k:8 * j + k + 1, :], (8, w)) for j in range(c_sz // 8)], axis=0)

    def halving_levels(rev):
        out = []
        b = c_sz // 2
        while b >= 1:
            def later(i):
                return ((i % (2 * b)) < b) if rev else ((i % (2 * b)) >= b)
            live = jnp.logical_and(t_i // (2 * b) == s_i // (2 * b),
                                   jnp.logical_and(later(t_i), jnp.logical_not(later(s_i))))
            out.append((b, live.astype(F32), jnp.where(later(t_row), 1.0, -1.0)))
            b //= 2
        return out

    def both(x_f, x_b):
        return jnp.concatenate([x_f, x_b], axis=0)

    def rows_of(row_f, row_b):
        return both(jnp.broadcast_to(row_f, (c_sz, w)), jnp.broadcast_to(row_b, (c_sz, w)))

    self_mask = both(*[(s_i == t_i).astype(F32)] * 2)
    levels = [(b, both(live_f, live_b), both(sign_f, sign_b))
              for (b, live_f, sign_f), (_, live_b, sign_b) in zip(halving_levels(False), halving_levels(True))]
    lb2 = rows_of(lb_ref[0:1, :], lb_ref[1:2, :])
    r2 = lax.broadcasted_iota(jnp.int32, (2 * c_sz, 2 * c_sz), 0)
    c2 = lax.broadcasted_iota(jnp.int32, (2 * c_sz, 2 * c_sz), 1)
    tri2 = jnp.logical_or(jnp.logical_and(r2 < c_sz, c2 <= r2),
                          jnp.logical_and(r2 >= c_sz, c2 >= r2)).astype(BF16)
    zero = jnp.zeros((), BF16)

    def boundary_rows(cum, b, rev):
        if b >= 8:
            return jnp.concatenate(
                [jnp.broadcast_to(cum[r0 + (b if rev else b - 1):r0 + (b if rev else b - 1) + 1, :], (2 * b, w))
                 for r0 in range(0, c_sz, 2 * b)], axis=0)
        ref = None
        for g in reversed(range(8 // (2 * b))):
            cand = grouped_rows(cum, g * 2 * b + (b if rev else b - 1))
            ref = cand if ref is None else jnp.where(row8 < (g + 1) * 2 * b, cand, ref)
        return ref

    def scan_step(n, carry):
        c_f = n
        c_b = jnp.where(n < n_ctx, n_ctx - 1 - n, n_all - 1 - (n - n_ctx))
        rows_f = pl.ds(pl.multiple_of(c_f * c_sz, c_sz), c_sz)
        rows_b = pl.ds(pl.multiple_of(c_b * c_sz, c_sz), c_sz)
        q_pre = both(pa_ref[0, rows_f, 0:A_W], pa_ref[0, rows_b, 0:A_W])
        v_f = pa_ref[0, rows_f, A_W:2 * A_W].astype(BF16)
        v_b = pa_ref[0, rows_b, A_W:2 * A_W].astype(BF16)
        f_pre = both(pa_ref[0, rows_f, 2 * A_W:3 * A_W], pa_ref[0, rows_b, 3 * A_W:4 * A_W])
        q = _silu(q_pre)
        f = lb2 + (1.0 - lb2) * jax.nn.sigmoid(f_pre)
        log_f = jnp.log(jnp.maximum(f, F_MIN))
        kk = (1.0 - lb2) * jax.nn.sigmoid(-f_pre)
        cum = sum(jnp.dot(tri2, piece, preferred_element_type=F32) for piece in _split3(log_f))
        end_f = cum[c_sz - 1:c_sz, :]
        end_b = cum[c_sz:c_sz + 1, :]

        st_f = st_ref[0]
        st_b = st_ref[1]
        q_in = (q * jnp.exp(cum)).astype(BF16)
        o = both(_nt(q_in[0:c_sz], st_f.astype(BF16)), _nt(q_in[c_sz:2 * c_sz], st_b.astype(BF16)))
        k_end = (kk * jnp.exp(rows_of(end_f, end_b) - cum)).astype(BF16)
        st_ref[0] = st_f * jnp.exp(end_f) + jnp.where(same_head, _tn(v_f, k_end[0:c_sz]), 0.0)
        st_ref[1] = st_b * jnp.exp(end_b) + jnp.where(same_head, _tn(v_b, k_end[c_sz:2 * c_sz]), 0.0)

        def scores(qa, ka):
            return _nt(qa, jnp.where(same_head, jnp.concatenate([ka] * HEADS, axis=0), zero))

        refs = [both(boundary_rows(cum[0:c_sz], b, False), boundary_rows(cum[c_sz:2 * c_sz], b, True))
                for b, _, _ in levels]
        zs = [jnp.exp((cum - ref) * sign)
              for ref, (_, _, sign) in zip(refs, levels)]
        qas = [(q * z).astype(BF16) for z in zs]
        kas = [(kk * z).astype(BF16) for z in zs]
        prods = [both(scores(qa[0:c_sz], ka[0:c_sz]), scores(qa[c_sz:2 * c_sz], ka[c_sz:2 * c_sz]))
                 for qa, ka in zip(qas, kas)]
        p = self_mask * jnp.dot((q * kk).astype(BF16), head_ones, preferred_element_type=F32)
        for prod, (_, live, _) in zip(prods, levels):
            p = p + live * prod
        pb = p.astype(BF16)

        def values(p_half, v_half):
            return jnp.dot(p_half, jnp.where(same_head, jnp.concatenate([v_half] * HEADS, axis=0), zero),
                           preferred_element_type=F32)

        o_ref[0, rows_f, :] = o[0:c_sz] + values(pb[0:c_sz], v_f)
        ob_ref[rows_b, :] = o[c_sz:2 * c_sz] + values(pb[c_sz:2 * c_sz], v_b)
        return carry

    st_ref[...] = jnp.zeros_like(st_ref)
    lax.fori_loop(0, n_all, scan_step, 0)

    def readout(c, carry):
        rows = pl.ds(pl.multiple_of(c * ROW_TILE, ROW_TILE), ROW_TILE)
        tot = o_ref[0, rows, :] + ob_ref[rows, :]
        ms = sum(jnp.dot(piece, head_ones, preferred_element_type=F32)
                 for piece in _split3(tot * tot)) * (1.0 / HEAD_V)
        g = pa_ref[0, rows, 4 * A_W:5 * A_W]
        o_ref[0, rows, :] = tot * lax.rsqrt(ms + NORM_EPS) * gain_ref[...] * _silu(g)
        return carry

    lax.fori_loop(0, n_all * c_sz // ROW_TILE, readout, 0)


def _hgrn(pa3, lb, gain, n_ctx_rows):
    n_batch, rows, _ = pa3.shape
    kern = functools.partial(_hgrn_kernel, n_ctx=n_ctx_rows // CHUNK, n_all=rows // CHUNK)
    return pl.pallas_call(
        kern,
        out_shape=jax.ShapeDtypeStruct((n_batch, rows, A_W), F32),
        grid=(n_batch,),
        in_specs=[pl.BlockSpec((1, rows, PA_W), lambda b: (b, 0, 0)),
                  pl.BlockSpec((2, A_W), lambda b: (0, 0)),
                  pl.BlockSpec((1, A_W), lambda b: (0, 0))],
        out_specs=pl.BlockSpec((1, rows, A_W), lambda b: (b, 0, 0)),
        scratch_shapes=[pltpu.VMEM((2, A_W, A_W), F32), pltpu.VMEM((rows, A_W), F32)],
        compiler_params=_cparams(("arbitrary",)),
        name="hgrn2",
    )(pa3, lb, gain)


def _split3(x):
    hi = x.astype(BF16)
    rest = x - hi.astype(F32)
    mid = rest.astype(BF16)
    return hi, mid, (rest - mid.astype(F32)).astype(BF16)


def _mlstm_kernel(pb_ref, bias_ref, gain_ref, o_ref, cn_ref, ob_ref, *, n_ctx, n_all, n_b):
    assert CHUNK == HEAD_V
    c_sz, w, qk_w = CHUNK, B_W, HEADS * B_QK
    t_i = lax.broadcasted_iota(jnp.int32, (c_sz, w), 0)
    s_i = lax.broadcasted_iota(jnp.int32, (c_sz, w), 1) % c_sz
    diag4 = s_i == t_i
    half_lane = lax.broadcasted_iota(jnp.int32, (1, LANES), 1) < HEAD_V
    k_block = (lax.broadcasted_iota(jnp.int32, (HEADS * c_sz, qk_w), 0) // c_sz
               == lax.broadcasted_iota(jnp.int32, (HEADS * c_sz, qk_w), 1) // B_QK)
    v_block = (lax.broadcasted_iota(jnp.int32, (HEADS * c_sz, w), 0) // c_sz
               == lax.broadcasted_iota(jnp.int32, (HEADS * c_sz, w), 1) // HEAD_V)
    state_block = (lax.broadcasted_iota(jnp.int32, (qk_w, 2 * w), 0) // B_QK
                   == (lax.broadcasted_iota(jnp.int32, (qk_w, 2 * w), 1) % w) // HEAD_V)
    head_ones = v_block.astype(BF16)
    q_off, k_off, v_off, o_off, g_off = 0, qk_w, 2 * qk_w, 2 * qk_w + B_W, 2 * qk_w + 2 * B_W

    sel_r = lax.broadcasted_iota(jnp.int32, (2 * LANES, 2 * w), 0)
    sel_c = lax.broadcasted_iota(jnp.int32, (2 * LANES, 2 * w), 1)
    sel = (sel_r == jnp.where(sel_c < w, HEADS + sel_c // HEAD_V, LANES + (sel_c - w) // HEAD_V)).astype(BF16)

    def head_max(x):
        outs = []
        for col in range(w // LANES):
            xc = x[:, col * LANES:(col + 1) * LANES]
            lo = jnp.max(jnp.where(half_lane, xc, -jnp.inf), axis=-1, keepdims=True)
            hi = jnp.max(jnp.where(half_lane, -jnp.inf, xc), axis=-1, keepdims=True)
            outs.append(jnp.where(half_lane, lo, hi))
        return jnp.concatenate(outs, axis=1)

    n_seg = 2 * n_b
    seg_rev = [seg % 2 == 1 for seg in range(n_seg)]

    def seg_rows(seg):
        return slice(seg * c_sz, (seg + 1) * c_sz)

    mask = jnp.concatenate([(s_i >= t_i) if rev else (s_i <= t_i) for rev in seg_rev], axis=0)
    r_all = lax.broadcasted_iota(jnp.int32, (n_seg * c_sz, n_seg * c_sz), 0)
    c_all = lax.broadcasted_iota(jnp.int32, (n_seg * c_sz, n_seg * c_sz), 1)
    same_seg = r_all // c_sz == c_all // c_sz
    rev_row = (r_all // c_sz) % 2 == 1
    tri = jnp.logical_and(same_seg, jnp.logical_or(
        jnp.logical_and(jnp.logical_not(rev_row), c_all <= r_all),
        jnp.logical_and(rev_row, c_all >= r_all))).astype(BF16)

    def per_seg(fn, x):
        return jnp.concatenate([jnp.broadcast_to(fn(x[seg_rows(seg)]), (c_sz, w)) for seg in range(n_seg)],
                               axis=0)

    def rows_of(rows):
        return jnp.concatenate([jnp.broadcast_to(row, (c_sz, w)) for row in rows], axis=0)

    def scan_step(n, m_prev_rows):
        c_f = n
        c_b = jnp.where(n < n_ctx, n_ctx - 1 - n, n_all - 1 - (n - n_ctx))
        chunk_rows = [pl.ds(pl.multiple_of((c_b if rev else c_f) * c_sz, c_sz), c_sz) for rev in seg_rev]

        def load(lo, hi):
            return [pb_ref[seg // 2, chunk_rows[seg], lo:hi] for seg in range(n_seg)]

        q_l = load(q_off, q_off + qk_w)
        k_l = load(k_off, k_off + qk_w)
        v_l = load(v_off, v_off + B_W)
        g_l = load(g_off, g_off + LANES)
        gts = jnp.concatenate(
            [pltpu.roll(g + bias_ref[...], LANES - 2 * HEADS, 1) if rev else g + bias_ref[...]
             for g, rev in zip(g_l, seg_rev)], axis=0)
        log_f = jnp.minimum(gts, 0.0) - jnp.log(1.0 + jnp.exp(-jnp.abs(gts)))
        cum_col = sum(jnp.dot(tri, p, preferred_element_type=F32) for p in _split3(log_f))
        both = sum(jnp.dot(p, sel, preferred_element_type=F32)
                   for p in _split3(jnp.concatenate([cum_col, gts], axis=1)))
        cum_t = both[:, :w]
        ig_t = both[:, w:]
        src = per_seg(lambda x: jnp.sum(jnp.where(diag4, x, 0.0), axis=0, keepdims=True), cum_t - ig_t)
        m_prev = rows_of(m_prev_rows)
        log_d = jnp.where(mask, cum_t - src, MASK_NEG)
        log_inter = cum_t + m_prev
        m_t = jnp.maximum(log_inter, head_max(log_d))
        w_intra = jnp.where(mask, jnp.exp(log_d - m_t), 0.0)
        w_inter = jnp.exp(log_inter - m_t)

        scale = B_QK ** -0.5
        qb_l = [(q * scale).astype(BF16) for q in q_l]
        kb_l = [k.astype(BF16) for k in k_l]
        zero = jnp.zeros((), BF16)
        scores = [_nt(qb, jnp.where(k_block, jnp.concatenate([kb] * HEADS, axis=0), zero))
                  for qb, kb in zip(qb_l, kb_l)]
        p = (jnp.concatenate(scores, axis=0) * w_intra).astype(BF16)

        cn_l = [cn_ref[seg] for seg in range(n_seg)]
        inter = jnp.concatenate([jnp.dot(qb, cn.astype(BF16), preferred_element_type=F32)
                                 for qb, cn in zip(qb_l, cn_l)], axis=0)
        values = [jnp.dot(p[seg_rows(seg)],
                          jnp.where(v_block, jnp.concatenate([v_l[seg].astype(BF16)] * HEADS, axis=0), zero),
                          preferred_element_type=F32) for seg in range(n_seg)]
        num = w_inter * inter[:, :w] + jnp.concatenate(values, axis=0)
        den = w_inter * inter[:, w:] + jnp.dot(p, head_ones, preferred_element_type=F32)
        h_out = num / jnp.maximum(jnp.abs(den), jnp.exp(-m_t))
        for seg, rev in enumerate(seg_rev):
            if rev:
                ob_ref[seg // 2, chunk_rows[seg], :] = h_out[seg_rows(seg)]
            else:
                o_ref[seg // 2, chunk_rows[seg], :] = h_out[seg_rows(seg)]

        ends = [cum_t[seg * c_sz + (0 if rev else c_sz - 1):seg * c_sz + (0 if rev else c_sz - 1) + 1, :]
                for seg, rev in enumerate(seg_rev)]
        log_end = rows_of(ends) - cum_t + ig_t
        m_ends = [jnp.maximum(end + m_old, jnp.max(log_end[seg_rows(seg)], axis=0, keepdims=True))
                  for seg, (end, m_old) in enumerate(zip(ends, m_prev_rows))]
        w_end = jnp.exp(log_end - rows_of(m_ends))
        for seg in range(n_seg):
            w_seg = w_end[seg_rows(seg)]
            upd = _tn(kb_l[seg], jnp.concatenate([(w_seg * v_l[seg]).astype(BF16), w_seg.astype(BF16)], axis=1))
            carry_w = jnp.exp(ends[seg] + m_prev_rows[seg] - m_ends[seg])
            cn_ref[seg] = (jnp.concatenate([carry_w, carry_w], axis=1) * cn_l[seg]
                           + jnp.where(state_block, upd, 0.0))
        return tuple(m_ends)

    cn_ref[...] = jnp.zeros_like(cn_ref)
    lax.fori_loop(0, n_all, scan_step, tuple(jnp.zeros((1, w), F32) for _ in range(n_seg)))

    def readout(c, carry):
        rows = pl.ds(pl.multiple_of(c * ROW_TILE, ROW_TILE), ROW_TILE)
        for b in range(n_b):
            tot = o_ref[b, rows, :] + ob_ref[b, rows, :]
            normed = jnp.concatenate(
                [_rms(tot[:, h * HEAD_V:(h + 1) * HEAD_V]) for h in range(HEADS)], axis=1) * gain_ref[...]
            og = pb_ref[b, rows, o_off:o_off + B_W]
            o_ref[b, rows, :] = normed * jax.nn.sigmoid(og)
        return carry

    lax.fori_loop(0, n_all * c_sz // ROW_TILE, readout, 0)


def _mlstm(pb3, bias, gain, n_ctx_rows):
    n_batch, rows, _ = pb3.shape
    n_b = SCAN_BATCHES if n_batch % SCAN_BATCHES == 0 else 1
    kern = functools.partial(_mlstm_kernel, n_ctx=n_ctx_rows // CHUNK, n_all=rows // CHUNK, n_b=n_b)
    return pl.pallas_call(
        kern,
        out_shape=jax.ShapeDtypeStruct((n_batch, rows, B_W), F32),
        grid=(n_batch // n_b,),
        in_specs=[pl.BlockSpec((n_b, rows, PB_W), lambda b: (b, 0, 0), pipeline_mode=pl.Buffered(1)),
                  pl.BlockSpec((1, LANES), lambda b: (0, 0)),
                  pl.BlockSpec((1, B_W), lambda b: (0, 0))],
        out_specs=pl.BlockSpec((n_b, rows, B_W), lambda b: (b, 0, 0)),
        scratch_shapes=[pltpu.VMEM((2 * n_b, HEADS * B_QK, 2 * B_W), F32), pltpu.VMEM((n_b, rows, B_W), F32)],
        compiler_params=_cparams(("arbitrary",)),
        name="mlstm",
    )(pb3, bias, gain)


def _attn_kernel(lam_ref, qp_ref, qr_ref, k_ref, v_ref, gain_ref, o_ref, *, n_ctx, q_tile0, lam_init):
    lam = lam_ref[0]
    q_tile = pl.program_id(2) + q_tile0
    lane = lax.broadcasted_iota(jnp.int32, (1, 2 * C_DQK), 1)

    n_all = k_ref.shape[1]

    def finish(head, parts):
        o = parts[0] - lam * parts[1]
        o_ref[0, :, head] = _rms(o) * gain_ref[...] * (1.0 - lam_init)

    def sub_query(q, j):
        return jnp.where(lane // C_DQK == j, q, jnp.zeros_like(q))

    def row_max(s):
        return jnp.max(s, axis=-1, keepdims=True)

    def row_sum(s):
        return jnp.sum(s, axis=-1, keepdims=True)

    def pv(ex, v):
        return jnp.dot(ex.astype(BF16), v, preferred_element_type=F32)

    heads = [slice(hh * C_DV, (hh + 1) * C_DV) for hh in range(ATTN_HEADS_PER_STEP)]

    def attend(pieces):
        problems = [(head, j) for head in heads for j in range(2)]
        scores = [[_nt(sub_query(q_ref[0, :, head], j), k_ref[0, k0:k1, head]) for q_ref, k0, k1 in pieces]
                  for head, j in problems]
        maxes = [functools.reduce(jnp.maximum, [row_max(s) for s in ss]) for ss in scores]
        exps = [[jnp.exp2(s - m) for s in ss] for ss, m in zip(scores, maxes)]
        outs = [sum(pv(e, v_ref[0, k0:k1, head]) for e, (_, k0, k1) in zip(es, pieces))
                / sum(row_sum(e) for e in es) for es, (head, _) in zip(exps, problems)]
        for n, head in enumerate(heads):
            finish(head, outs[2 * n:2 * n + 2])

    @pl.when(q_tile * ROW_TILE < n_ctx)
    def _():
        attend([(qp_ref, 0, n_ctx)])

    @pl.when(q_tile * ROW_TILE >= n_ctx)
    def _():
        attend([(qp_ref, 0, n_ctx), (qr_ref, n_ctx, n_all)])


def _attention(pc3, lam, gain, n_ctx_rows, lam_init, with_ctx):
    n_batch, rows, _ = pc3.shape
    q_tile0 = 0 if with_ctx else n_ctx_rows // ROW_TILE
    n_q = rows // ROW_TILE - q_tile0
    gw = ATTN_HEADS_PER_STEP * C_DV
    hb = C_W // gw
    kern = functools.partial(_attn_kernel, n_ctx=n_ctx_rows, q_tile0=q_tile0, lam_init=lam_init)
    grid_spec = pltpu.PrefetchScalarGridSpec(
        num_scalar_prefetch=1,
        grid=(n_batch, hb, n_q),
        in_specs=[pl.BlockSpec((1, ROW_TILE, gw), lambda b, h, i, lam: (b, i + q_tile0, h)),
                  pl.BlockSpec((1, ROW_TILE, gw), lambda b, h, i, lam: (b, i + q_tile0, hb + h)),
                  pl.BlockSpec((1, rows, gw), lambda b, h, i, lam: (b, 0, 2 * hb + h)),
                  pl.BlockSpec((1, rows, gw), lambda b, h, i, lam: (b, 0, 3 * hb + h)),
                  pl.BlockSpec((1, C_DV), lambda b, h, i, lam: (0, 0))],
        out_specs=pl.BlockSpec((1, ROW_TILE, gw), lambda b, h, i, lam: (b, i, h)),
    )
    return pl.pallas_call(
        kern,
        out_shape=jax.ShapeDtypeStruct((n_batch, n_q * ROW_TILE, C_W), F32),
        grid_spec=grid_spec,
        compiler_params=_cparams(("arbitrary", "arbitrary", "arbitrary")),
        name="diff_attn",
    )(lam, pc3, pc3, pc3, pc3, gain)


def _outproj_kernel(oa_ref, ob_ref, oc_ref, x_ref, mod_ref, n_ref, w_ref, rw_ref, rb_ref,
                    xn_ref, h_ref, route_ref, cnt_ref, carry_ref):
    d = x_ref.shape[1]
    n_exp = rw_ref.shape[1]

    @pl.when(pl.program_id(0) == 0)
    def _():
        carry_ref[...] = jnp.zeros_like(carry_ref)

    mix = (jnp.dot(oa_ref[...].astype(BF16), w_ref[0:A_W, :], preferred_element_type=F32)
           + jnp.dot(ob_ref[...].astype(BF16), w_ref[A_W:A_W + B_W, :], preferred_element_type=F32)
           + jnp.dot(oc_ref[...].astype(BF16), w_ref[A_W + B_W:A_W + B_W + C_W, :], preferred_element_type=F32))
    xn = x_ref[...] + mod_ref[0, :, 2 * d:3 * d] * (_rms(mix) * n_ref[1:2, :])
    xn_ref[...] = xn
    h = _rms(xn) * n_ref[2:3, :] * (1.0 + mod_ref[0, :, 4 * d:5 * d]) + mod_ref[0, :, 3 * d:4 * d]
    h_ref[...] = _pack_bf16_pairs(h)

    h_hi = h.astype(BF16)
    h_mid = (h - h_hi.astype(F32)).astype(BF16)
    logits = _nt(rw_ref[0], h_hi) + _nt(rw_ref[1], h_hi) + _nt(rw_ref[0], h_mid) + rb_ref[...]
    e_sub = lax.broadcasted_iota(jnp.int32, logits.shape, 0)
    cur = logits
    picks, vals = [], []
    for _ in range(TOP_K):
        mx = jnp.max(cur, axis=0, keepdims=True)
        idx = jnp.min(jnp.where(cur == mx, e_sub, n_exp), axis=0, keepdims=True)
        hit = e_sub == idx
        cur = jnp.where(hit, -jnp.inf, cur)
        picks.append((idx, hit.astype(F32)))
        vals.append(mx)
    exps = [jnp.exp(vv - vals[0]) for vv in vals]
    total = exps[0] + exps[1] + exps[2] + exps[3]

    chosen = picks[0][1] + picks[1][1] + picks[2][1] + picks[3][1]
    tm = logits.shape[1]
    before = (lax.broadcasted_iota(jnp.int32, (tm, tm), 0) < lax.broadcasted_iota(jnp.int32, (tm, tm), 1))
    seen = jnp.dot(chosen.astype(BF16), before.astype(BF16), preferred_element_type=F32) + carry_ref[...]
    new_carry = carry_ref[...] + jnp.sum(chosen, axis=1, keepdims=True)
    carry_ref[...] = new_carry
    cnt_ref[...] = new_carry

    o_row = lax.broadcasted_iota(jnp.int32, route_ref.shape, 0)
    route = jnp.zeros(route_ref.shape, F32)
    for kk in range(TOP_K):
        idx, hit = picks[kk]
        rank = jnp.sum(hit * seen, axis=0, keepdims=True)
        route = (route + jnp.where(o_row == kk, idx.astype(F32), 0.0)
                 + jnp.where(o_row == TOP_K + kk, exps[kk] / total, 0.0)
                 + jnp.where(o_row == 2 * TOP_K + kk, rank, 0.0))
    route_ref[...] = route


ROUTE_ROWS = 16


def _outproj(oa, ob, oc, xy, mod, norms, w_out_b, router_wt, router_b, layer, n_tiles, in_tile, oc_tile, mod_row):
    d = xy.shape[1]
    n_exp = router_wt.shape[2]
    t_out = n_tiles * ROW_TILE
    xn, h, route_t, counts = pl.pallas_call(
        _outproj_kernel,
        out_shape=(jax.ShapeDtypeStruct((t_out, d), F32),
                   jax.ShapeDtypeStruct((t_out, d // 2), jnp.int32),
                   jax.ShapeDtypeStruct((n_tiles * ROUTE_ROWS, ROW_TILE), F32),
                   jax.ShapeDtypeStruct((n_exp, 1), F32)),
        grid=(n_tiles,),
        in_specs=[pl.BlockSpec((ROW_TILE, A_W), lambda i: (in_tile(i), 0)),
                  pl.BlockSpec((ROW_TILE, B_W), lambda i: (in_tile(i), 0)),
                  pl.BlockSpec((ROW_TILE, C_W), lambda i: (oc_tile(i), 0)),
                  pl.BlockSpec((ROW_TILE, d), lambda i: (in_tile(i), 0)),
                  pl.BlockSpec((1, 1, 6 * d), lambda i: (mod_row(i), 0, 0)),
                  pl.BlockSpec((None, 4, d), lambda i: (layer, 0, 0)),
                  pl.BlockSpec((None, d, d), lambda i: (layer, 0, 0)),
                  pl.BlockSpec((None, 2, n_exp, d), lambda i: (layer, 0, 0, 0)),
                  pl.BlockSpec((None, n_exp, 1), lambda i: (layer, 0, 0))],
        out_specs=(pl.BlockSpec((ROW_TILE, d), lambda i: (i, 0)),
                   pl.BlockSpec((ROW_TILE, d // 2), lambda i: (i, 0)),
                   pl.BlockSpec((ROUTE_ROWS, ROW_TILE), lambda i: (i, 0)),
                   pl.BlockSpec((n_exp, 1), lambda i: (0, 0))),
        scratch_shapes=[pltpu.VMEM((n_exp, 1), F32)],
        compiler_params=_cparams(("arbitrary",)),
        name="outproj_router",
    )(oa, ob, oc, xy, mod, norms, w_out_b, router_wt, router_b)
    return xn, h, route_t.reshape(n_tiles, ROUTE_ROWS, ROW_TILE), counts[:, 0]


PAIR_BLOCK = 2 * LANES


def _moe_kernel(te_ref, first_ref, valid_ref, nu_ref, slot_ref, next_ref, xs_ref, w1_hbm, b1_ref, w2_hbm, b2_ref,
                ys_ref, w1p_ref, w2b_ref, w1_buf, w2_buf, w_sem, *, layer):
    i = pl.program_id(0)
    two_f = w1_buf.shape[2]
    n_blk = two_f // PAIR_BLOCK

    def fetch(expert, slot):
        return (pltpu.make_async_copy(w1_hbm.at[layer, expert], w1_buf.at[slot], w_sem.at[0, slot]),
                pltpu.make_async_copy(w2_hbm.at[layer, expert], w2_buf.at[slot], w_sem.at[1, slot]))

    @pl.when(i == 0)
    def _():
        for copy in fetch(te_ref[0], 0):
            copy.start()

    @pl.when(jnp.logical_and(i < nu_ref[0], first_ref[i] == 1))
    def _():
        slot = slot_ref[i]
        for copy in fetch(te_ref[i], slot):
            copy.wait()
        r = lax.broadcasted_iota(jnp.int32, (PAIR_BLOCK, PAIR_BLOCK), 0)
        c = lax.broadcasted_iota(jnp.int32, (PAIR_BLOCK, PAIR_BLOCK), 1)
        perm = (r == jnp.where(c < LANES, 2 * c, 2 * (c - LANES) + 1)).astype(BF16)
        for blk in range(n_blk):
            cols = slice(blk * PAIR_BLOCK, (blk + 1) * PAIR_BLOCK)
            w1p_ref[:, cols] = jnp.dot(w1_buf[slot, :, cols].astype(BF16), perm,
                                       preferred_element_type=F32).astype(BF16)
        w2b_ref[...] = w2_buf[slot].astype(BF16)

        @pl.when(next_ref[i] >= 0)
        def _():
            for copy in fetch(next_ref[i], 1 - slot):
                copy.start()

    tile_rows = xs_ref.shape[0]
    used = i < nu_ref[0]
    valid = valid_ref[i]

    def expert_ffn(n_rows):
        x = _unpack_bf16_pairs(xs_ref[0:n_rows, :]).astype(BF16)
        row = lax.broadcasted_iota(jnp.int32, (n_rows, 1), 0)
        x = jnp.where(row < valid, x, jnp.zeros_like(x))
        hid = jnp.dot(x, w1p_ref[...], preferred_element_type=F32) + b1_ref[...]
        acts = []
        for blk in range(n_blk):
            glu = jnp.minimum(hid[:, blk * PAIR_BLOCK:blk * PAIR_BLOCK + LANES], SWIGLU_LIMIT)
            lin = jnp.clip(hid[:, blk * PAIR_BLOCK + LANES:(blk + 1) * PAIR_BLOCK], -SWIGLU_LIMIT, SWIGLU_LIMIT)
            acts.append((glu * jax.nn.sigmoid(SWIGLU_ALPHA * glu) * (lin + 1.0)).astype(BF16))
        y = jnp.dot(jnp.concatenate(acts, axis=1), w2b_ref[...], preferred_element_type=F32) + b2_ref[...]
        ys_ref[0:n_rows, :] = _pack_bf16_pairs(y)
        if n_rows < tile_rows:
            ys_ref[n_rows:tile_rows, :] = jnp.zeros((tile_rows - n_rows, ys_ref.shape[1]), ys_ref.dtype)

    @pl.when(jnp.logical_and(used, valid > tile_rows // 2))
    def _():
        expert_ffn(tile_rows)

    @pl.when(jnp.logical_and(used, valid <= tile_rows // 2))
    def _():
        expert_ffn(tile_rows // 2)

    @pl.when(jnp.logical_not(used))
    def _():
        ys_ref[...] = jnp.zeros_like(ys_ref)


def _moe(plan, xs, w1, b1p, w2, b2, layer):
    r_max, half_d = xs.shape
    d = 2 * half_d
    two_f = w1.shape[3]
    f = two_f // 2
    n_tiles = r_max // MOE_TILE

    def row_tile(i, te, fi, va, nu, sl, nx):
        return (jnp.maximum(jnp.minimum(i, nu[0] - 1), 0), 0)

    def bias_block(i, te, fi, va, nu, sl, nx):
        return (layer, te[i], 0, 0)

    grid_spec = pltpu.PrefetchScalarGridSpec(
        num_scalar_prefetch=6,
        grid=(n_tiles,),
        in_specs=[pl.BlockSpec((MOE_TILE, half_d), row_tile),
                  pl.BlockSpec(memory_space=pl.ANY),
                  pl.BlockSpec((None, None, 1, two_f), bias_block),
                  pl.BlockSpec(memory_space=pl.ANY),
                  pl.BlockSpec((None, None, 1, d), bias_block)],
        out_specs=pl.BlockSpec((MOE_TILE, half_d), lambda i, te, fi, va, nu, sl, nx: (i, 0)),
        scratch_shapes=[pltpu.VMEM((d, two_f), BF16), pltpu.VMEM((f, d), BF16),
                        pltpu.VMEM((2, d, two_f), F32), pltpu.VMEM((2, f, d), F32),
                        pltpu.SemaphoreType.DMA((2, 2))],
    )
    return pl.pallas_call(
        functools.partial(_moe_kernel, layer=layer),
        out_shape=jax.ShapeDtypeStruct((r_max, half_d), jnp.int32),
        grid_spec=grid_spec,
        compiler_params=_cparams(("arbitrary",)),
        name="moe_experts",
    )(*plan, xs, w1, b1p, w2, b2)


SC_CORES = 2
SC_SUBCORES = 16
SC_CHUNK = 64


def _dispatch_rows(hp, pos, r_max):
    t, width = hp.shape
    workers = SC_CORES * SC_SUBCORES
    assert t % (workers * SC_CHUNK) == 0
    per_worker = t // (workers * SC_CHUNK)
    idx = pos.reshape(-1, TOP_K, ROW_TILE // SC_CHUNK, SC_CHUNK).transpose(0, 2, 1, 3).reshape(
        workers, per_worker * TOP_K, SC_CHUNK)
    mesh = plsc.VectorSubcoreMesh(core_axis_name="c", subcore_axis_name="s",
                                  num_cores=SC_CORES, num_subcores=SC_SUBCORES)

    @functools.partial(
        pl.kernel, mesh=mesh,
        out_type=jax.ShapeDtypeStruct((r_max, width), hp.dtype),
        scratch_types=[pltpu.VMEM((per_worker * TOP_K, SC_CHUNK), jnp.int32),
                       pltpu.VMEM((2, SC_CHUNK, width), hp.dtype),
                       pltpu.SemaphoreType.DMA((2,)),
                       pltpu.SemaphoreType.DMA((2,))],
    )
    def scatter(hp_hbm, idx_hbm, out_hbm, idx_v, rows_v, read_sem, write_sem):
        wid = lax.axis_index("s") * SC_CORES + lax.axis_index("c")
        pltpu.sync_copy(idx_hbm.at[wid], idx_v)

        def read(j):
            row0 = pl.multiple_of((wid * per_worker + j) * SC_CHUNK, SC_CHUNK)
            return pltpu.make_async_copy(hp_hbm.at[pl.ds(row0, SC_CHUNK)], rows_v.at[j % 2], read_sem.at[j % 2])

        def write(j, k):
            return pltpu.make_async_copy(rows_v.at[j % 2], out_hbm.at[idx_v.at[j * TOP_K + k]],
                                         write_sem.at[j % 2])

        read(0).start()
        for j in range(per_worker):
            read(j).wait()
            for k in range(TOP_K):
                write(j, k).start()
            if j + 1 < per_worker:
                if j >= 1:
                    for k in range(TOP_K):
                        write(j - 1, k).wait()
                read(j + 1).start()
        for j in range(max(per_worker - 2, 0), per_worker):
            for k in range(TOP_K):
                write(j, k).wait()

    return scatter(hp, idx)


def _combine_rows(ysp, pos):
    width = ysp.shape[1]
    t = pos.shape[0] * pos.shape[2]
    workers = SC_CORES * SC_SUBCORES
    assert t % (workers * SC_CHUNK) == 0
    per_worker = t // (workers * SC_CHUNK)
    units = per_worker * TOP_K
    idx = pos.reshape(-1, TOP_K, ROW_TILE // SC_CHUNK, SC_CHUNK).transpose(0, 2, 1, 3).reshape(
        workers, units, SC_CHUNK)
    mesh = plsc.VectorSubcoreMesh(core_axis_name="c", subcore_axis_name="s",
                                  num_cores=SC_CORES, num_subcores=SC_SUBCORES)

    @functools.partial(
        pl.kernel, mesh=mesh,
        out_type=jax.ShapeDtypeStruct((TOP_K, t, width), ysp.dtype),
        scratch_types=[pltpu.VMEM((units, SC_CHUNK), jnp.int32),
                       pltpu.VMEM((2, SC_CHUNK, width), ysp.dtype),
                       pltpu.SemaphoreType.DMA((2,)),
                       pltpu.SemaphoreType.DMA((2,))],
    )
    def gather(ys_hbm, idx_hbm, out_hbm, idx_v, rows_v, read_sem, write_sem):
        wid = lax.axis_index("s") * SC_CORES + lax.axis_index("c")
        pltpu.sync_copy(idx_hbm.at[wid], idx_v)

        def read(u):
            return pltpu.make_async_copy(ys_hbm.at[idx_v.at[u]], rows_v.at[u % 2], read_sem.at[u % 2])

        def write(u):
            row0 = pl.multiple_of((wid * per_worker + u // TOP_K) * SC_CHUNK, SC_CHUNK)
            return pltpu.make_async_copy(rows_v.at[u % 2], out_hbm.at[u % TOP_K, pl.ds(row0, SC_CHUNK)],
                                         write_sem.at[u % 2])

        read(0).start()
        for u in range(units):
            read(u).wait()
            write(u).start()
            if u + 1 < units:
                if u >= 1:
                    write(u - 1).wait()
                read(u + 1).start()
        for u in range(max(units - 2, 0), units):
            write(u).wait()

    return gather(ysp, idx)


def _ffn_residual_kernel(x_ref, w_ref, y0_ref, y1_ref, y2_ref, y3_ref, mod_ref, n_ref, o_ref):
    d = x_ref.shape[1]
    wts = w_ref[...]
    ffn = None
    for k, y_ref in enumerate((y0_ref, y1_ref, y2_ref, y3_ref)):
        term = wts[:, k:k + 1] * _unpack_bf16_pairs(y_ref[...])
        ffn = term if ffn is None else ffn + term
    o_ref[...] = x_ref[...] + mod_ref[0, :, 5 * d:6 * d] * (_rms(ffn) * n_ref[3:4, :])


def _ffn_residual(xn, wts, ys_by_k, mod, norms, layer, mod_row):
    t, d = xn.shape
    row_spec = pl.BlockSpec((ROW_TILE, d), lambda i: (i, 0))

    def y_spec(k):
        return pl.BlockSpec((None, ROW_TILE, d // 2), lambda i: (k, i, 0))

    return pl.pallas_call(
        _ffn_residual_kernel,
        out_shape=jax.ShapeDtypeStruct((t, d), F32),
        grid=(t // ROW_TILE,),
        in_specs=[row_spec,
                  pl.BlockSpec((ROW_TILE, TOP_K), lambda i: (i, 0)),
                  y_spec(0), y_spec(1), y_spec(2), y_spec(3),
                  pl.BlockSpec((1, 1, 6 * d), lambda i: (mod_row(i), 0, 0)),
                  pl.BlockSpec((None, 4, d), lambda i: (layer, 0, 0))],
        out_specs=row_spec,
        compiler_params=_cparams(("arbitrary",)),
        name="ffn_residual",
    )(xn, wts, ys_by_k, ys_by_k, ys_by_k, ys_by_k, mod, norms)


def _dispatch_plan(route_t, counts, r_max):
    n_row_tiles = route_t.shape[0]
    idx = route_t[:, 0:TOP_K, :].astype(jnp.int32)
    wts = route_t[:, TOP_K:2 * TOP_K, :].transpose(0, 2, 1).reshape(n_row_tiles * ROW_TILE, TOP_K)
    rank = route_t[:, 2 * TOP_K:3 * TOP_K, :].astype(jnp.int32)
    cnt = counts.astype(jnp.int32)
    padded = (cnt + MOE_TILE - 1) // MOE_TILE * MOE_TILE
    ends = jnp.cumsum(padded)
    starts = ends - padded
    pos = rank
    for e in range(cnt.shape[0]):
        pos = pos + jnp.where(idx == e, starts[e], 0)
    n_tiles = r_max // MOE_TILE
    n_used = ends[-1] // MOE_TILE
    tile_ids = jnp.minimum(jnp.arange(n_tiles, dtype=jnp.int32), n_used - 1)
    tile_expert = jnp.sum((ends // MOE_TILE)[None, :] <= tile_ids[:, None], axis=1).astype(jnp.int32)
    tile_first = jnp.concatenate(
        [jnp.ones((1,), jnp.int32), (tile_expert[1:] != tile_expert[:-1]).astype(jnp.int32)])
    of_expert = (tile_expert[:, None] == jnp.arange(cnt.shape[0], dtype=jnp.int32)[None, :]).astype(jnp.int32)
    first_tile = jnp.sum(of_expert * ((ends - padded) // MOE_TILE)[None, :], axis=1)
    tile_valid = jnp.clip(jnp.sum(of_expert * cnt[None, :], axis=1) - (tile_ids - first_tile) * MOE_TILE,
                          0, MOE_TILE).astype(jnp.int32)
    experts = jnp.arange(cnt.shape[0], dtype=jnp.int32)
    has_rows = cnt > 0
    tile_slot = (jnp.sum(of_expert * (jnp.cumsum(has_rows.astype(jnp.int32)) - 1)[None, :], axis=1) % 2).astype(jnp.int32)
    later_with_rows = jnp.logical_and(has_rows[None, :], experts[None, :] > experts[:, None])
    next_expert = jnp.min(jnp.where(later_with_rows, experts[None, :], cnt.shape[0]), axis=1)
    next_expert = jnp.where(next_expert < cnt.shape[0], next_expert, -1)
    tile_next = jnp.sum(of_expert * next_expert[None, :], axis=1).astype(jnp.int32)
    plan = (tile_expert, tile_first, tile_valid, n_used.reshape(1).astype(jnp.int32), tile_slot, tile_next)
    return pos, wts, plan


def _rope_tables(length, n_ctx_rows):
    rows = length // GRID_W
    row = jnp.repeat(jnp.arange(rows, dtype=F32), GRID_W)
    col = jnp.tile(jnp.arange(GRID_W, dtype=F32), rows)
    n_freq = C_DQK // 4
    inv_freq = ROPE_THETA ** (-jnp.arange(n_freq, dtype=F32) / n_freq)
    ang_r = row[:, None] * inv_freq
    ang_c = col[:, None] * inv_freq
    cos = jnp.concatenate([jnp.cos(ang_r), jnp.cos(ang_r), jnp.cos(ang_c), jnp.cos(ang_c)], axis=-1)
    sin = jnp.concatenate([-jnp.sin(ang_r), jnp.sin(ang_r), -jnp.sin(ang_c), jnp.sin(ang_c)], axis=-1)
    reps = C_W // C_DQK
    cos = jnp.concatenate([jnp.ones((n_ctx_rows, C_W), F32), jnp.tile(cos, (1, reps))], axis=0)
    sin = jnp.concatenate([jnp.zeros((n_ctx_rows, C_W), F32), jnp.tile(sin, (1, reps))], axis=0)
    return cos, sin


def _hgrn_lower_bound(table, layer):
    p = jax.nn.softmax(table.astype(F32), axis=1)
    cum = jnp.cumsum(p, axis=1) - p[:, :1]
    return jnp.clip(cum[:, layer], 0.0, 1.0)


def kernel(x, c, ctx, c_ctx, ada_w, ada_b, sandwich_norms, w_in, w_out, hgrn_lower_bounds, hgrn_norm,
           mlstm_gate_bias, mlstm_norm, diff_lambdas, diff_norm, router_w, router_b, moe_w1, moe_b1,
           moe_w2, moe_b2):
    n_batch, seq, d = x.shape
    n_ctx_rows = ctx.shape[1]
    depth = w_in.shape[0]
    n_exp = router_w.shape[2]
    assert seq % ROW_TILE == 0 and n_ctx_rows % ROW_TILE == 0 and seq % GRID_W == 0
    rows_b = n_ctx_rows + seq
    tiles_b = rows_b // ROW_TILE
    ctx_tiles = n_ctx_rows // ROW_TILE
    lat_tiles = seq // ROW_TILE

    b_main = PA_W + 2 * HEADS * B_QK + 2 * B_W
    w_in_p = jnp.concatenate(
        [w_in[:, :, :b_main],
         jnp.pad(w_in[:, :, b_main:b_main + N_GATE], ((0, 0), (0, 0), (0, LANES - N_GATE))),
         w_in[:, :, b_main + N_GATE:]], axis=2).astype(BF16)
    w_out_b = w_out.astype(BF16)
    two_f = moe_b1.shape[2]
    b1p = moe_b1.reshape(depth, n_exp, two_f // PAIR_BLOCK, LANES, 2).transpose(0, 1, 2, 4, 3).reshape(
        depth, n_exp, 1, two_f)
    b2 = moe_b2[:, :, None, :]
    ada_b3 = ada_b[:, None, :]
    router_b3 = router_b[:, :, None]
    rw_t = router_w.transpose(0, 2, 1)
    rw_hi = rw_t.astype(BF16)
    router_wt = jnp.stack([rw_hi, (rw_t - rw_hi.astype(F32)).astype(BF16)], axis=1)
    gate_bias = jnp.pad(mlstm_gate_bias, ((0, 0), (0, LANES - N_GATE)))
    cos, sin = _rope_tables(seq, n_ctx_rows)

    cond_rows = (n_batch + 1 + 7) // 8 * 8
    cvec = jnp.zeros((cond_rows, d), F32).at[:n_batch].set(c).at[n_batch].set(c_ctx)

    def mod_row_all(i):
        return jnp.where(i % tiles_b < ctx_tiles, n_batch, i // tiles_b)

    xy = jnp.concatenate([ctx, x], axis=1).reshape(n_batch * rows_b, d)

    for layer in range(depth):
        last = layer == depth - 1
        mod = _adaln(cvec, ada_w, ada_b3, layer).reshape(cond_rows, 1, 6 * d)
        lb = _hgrn_lower_bound(hgrn_lower_bounds, layer)
        lam_init = 0.8 - 0.6 * math.exp(-0.3 * layer)
        lq1, lk1, lq2, lk2 = diff_lambdas[layer].astype(F32)
        lam = (jnp.exp(jnp.sum(lq1 * lk1)) - jnp.exp(jnp.sum(lq2 * lk2)) + lam_init).reshape(1)

        pa, pb, pc = _inproj(xy, mod, sandwich_norms, w_in_p, cos, sin, layer, tiles_b, ctx_tiles, n_batch)
        oa = _hgrn(pa.reshape(n_batch, rows_b, PA_W), lb, jnp.tile(hgrn_norm[layer], HEADS)[None, :], n_ctx_rows)
        ob = _mlstm(pb.reshape(n_batch, rows_b, PB_W), gate_bias[layer][None, :], mlstm_norm[layer][None, :],
                    n_ctx_rows)
        oc = _attention(pc.reshape(n_batch, rows_b, PC_OUT), lam, diff_norm[layer][None, :], n_ctx_rows,
                        lam_init, not last)

        if last:
            n_tiles = n_batch * lat_tiles
            in_tile = lambda i: (i // lat_tiles) * tiles_b + ctx_tiles + i % lat_tiles
            mod_row = lambda i: i // lat_tiles
        else:
            n_tiles = n_batch * tiles_b
            in_tile = lambda i: i
            mod_row = mod_row_all
        xn, h, route, counts = _outproj(
            oa.reshape(-1, A_W), ob.reshape(-1, B_W), oc.reshape(-1, C_W), xy, mod, sandwich_norms, w_out_b,
            router_wt, router_b3, layer, n_tiles, in_tile, lambda i: i, mod_row)

        r_max = n_tiles * ROW_TILE * TOP_K + n_exp * MOE_TILE
        pos, wts, plan = _dispatch_plan(route, counts, r_max)
        xs = _dispatch_rows(h, pos, r_max)
        ys = _moe(plan, xs, moe_w1, b1p, moe_w2, b2, layer)
        xy = _ffn_residual(xn, wts, _combine_rows(ys, pos), mod, sandwich_norms, layer, mod_row)

    return xy.reshape(n_batch, seq, d)
```

```python
import functools
import math

import jax
import jax.numpy as jnp
from jax import lax
from jax.experimental import pallas as pl
from jax.experimental.pallas import tpu as pltpu
from jax.experimental.pallas import tpu_sc as plsc

F32 = jnp.float32
BF16 = jnp.bfloat16
HI = lax.Precision.HIGHEST

HEADS = 4
A_W = 256
B_QK = 32
B_W = 256
C_DQK = 64
C_DV = 2 * C_DQK
ATTN_KEY_SCALE = C_DQK ** -0.5 * math.log2(math.e)
C_W = 512
HEAD_V = 64
N_GATE = 16
GRID_W = 64
TOP_K = 4
SWIGLU_ALPHA = 1.702
SWIGLU_LIMIT = 7.0
ROPE_THETA = 10000.0
NORM_EPS = 1e-6
MASK_NEG = -1e30
F_MIN = 1e-12

LANES = 128
ROW_TILE = 256
CHUNK = 64
SCAN_BATCHES = 2
MOE_TILE = 512
ATTN_HEADS_PER_STEP = 2
VMEM_LIMIT = 56 * 1024 * 1024

PA_W = 5 * A_W
PB_W = 2 * HEADS * B_QK + 2 * B_W + LANES
PC_IN = 3 * C_W
PC_OUT = 4 * C_W
W_IN_PAD = PA_W + PB_W + PC_IN


def _cparams(sem):
    return pltpu.CompilerParams(dimension_semantics=sem, vmem_limit_bytes=VMEM_LIMIT)


def _nt(a, b):
    return lax.dot_general(a, b, (((1,), (1,)), ((), ())), preferred_element_type=F32)


def _tn(a, b, precision=None):
    return lax.dot_general(a, b, (((0,), (0,)), ((), ())), preferred_element_type=F32, precision=precision)


def _rms(x):
    return x * lax.rsqrt(jnp.mean(x * x, axis=-1, keepdims=True) + NORM_EPS)


def _silu(x):
    return x * jax.nn.sigmoid(x)


def _pack_bf16_pairs(x):
    half = x.shape[1] // 2
    bits = pltpu.bitcast(x.astype(BF16).astype(F32), jnp.uint32)
    return pltpu.bitcast((bits[:, :half] >> 16) | (bits[:, half:] & jnp.uint32(0xFFFF0000)), jnp.int32)


def _unpack_bf16_pairs(words):
    bits = pltpu.bitcast(words, jnp.uint32)
    return jnp.concatenate([pltpu.bitcast(bits << 16, F32),
                            pltpu.bitcast(bits & jnp.uint32(0xFFFF0000), F32)], axis=1)


def _adaln_kernel(c_ref, w_ref, b_ref, o_ref):
    cond = _silu(c_ref[...])
    o_ref[...] = jnp.dot(cond, w_ref[...], preferred_element_type=F32, precision=HI) + b_ref[...]


def _adaln(cvec, ada_w, ada_b, layer):
    rows, d = cvec.shape
    return pl.pallas_call(
        _adaln_kernel,
        out_shape=jax.ShapeDtypeStruct((rows, 6 * d), F32),
        grid=(6,),
        in_specs=[pl.BlockSpec((rows, d), lambda j: (0, 0)),
                  pl.BlockSpec((None, d, d), lambda j: (layer, 0, j)),
                  pl.BlockSpec((None, 1, d), lambda j: (layer, 0, j))],
        out_specs=pl.BlockSpec((rows, d), lambda j: (0, j)),
        compiler_params=_cparams(("arbitrary",)),
        name="adaln",
    )(cvec, ada_w, ada_b)


def _inproj_kernel(x_ref, mod_ref, g_ref, w_ref, cos_ref, sin_ref, pa_ref, pb_ref, pc_ref):
    d = x_ref.shape[1]
    shift = mod_ref[0, :, 0:d]
    scale = mod_ref[0, :, d:2 * d]
    h = _rms(x_ref[...]) * g_ref[0:1, :] * (1.0 + scale) + shift
    hb = h.astype(BF16)
    pa_ref[...] = jnp.dot(hb, w_ref[:, 0:PA_W], preferred_element_type=F32)
    pb_ref[...] = jnp.dot(hb, w_ref[:, PA_W:PA_W + PB_W], preferred_element_type=F32)
    pc = jnp.dot(hb, w_ref[:, PA_W + PB_W:W_IN_PAD], preferred_element_type=F32)
    q = pc[:, 0:C_W]
    k = pc[:, C_W:2 * C_W] * ATTN_KEY_SCALE
    cos = cos_ref[...]
    sin = sin_ref[...]
    lane = lax.broadcasted_iota(jnp.int32, q.shape, 1)
    first = (lane % 32) < 16

    def rope(t):
        partner = jnp.where(first, pltpu.roll(t, C_W - 16, 1), pltpu.roll(t, 16, 1))
        return t * cos + partner * sin

    pc_ref[:, 0:C_W] = q.astype(BF16)
    pc_ref[:, C_W:2 * C_W] = rope(q).astype(BF16)
    pc_ref[:, 2 * C_W:3 * C_W] = rope(k).astype(BF16)
    pc_ref[:, 3 * C_W:4 * C_W] = pc[:, 2 * C_W:3 * C_W].astype(BF16)


def _inproj(xy, mod, norms, w_in_p, cos, sin, layer, tiles_per_batch, n_ctx_tiles, n_batch):
    t_all, d = xy.shape
    n_tiles = t_all // ROW_TILE

    def mod_row(i):
        return jnp.where(i % tiles_per_batch < n_ctx_tiles, n_batch, i // tiles_per_batch)

    return pl.pallas_call(
        _inproj_kernel,
        out_shape=(jax.ShapeDtypeStruct((t_all, PA_W), F32),
                   jax.ShapeDtypeStruct((t_all, PB_W), F32),
                   jax.ShapeDtypeStruct((t_all, PC_OUT), BF16)),
        grid=(n_tiles,),
        in_specs=[pl.BlockSpec((ROW_TILE, d), lambda i: (i, 0)),
                  pl.BlockSpec((1, 1, 6 * d), lambda i: (mod_row(i), 0, 0)),
                  pl.BlockSpec((None, 4, d), lambda i: (layer, 0, 0)),
                  pl.BlockSpec((None, d, W_IN_PAD), lambda i: (layer, 0, 0)),
                  pl.BlockSpec((ROW_TILE, C_W), lambda i: (i % tiles_per_batch, 0)),
                  pl.BlockSpec((ROW_TILE, C_W), lambda i: (i % tiles_per_batch, 0))],
        out_specs=(pl.BlockSpec((ROW_TILE, PA_W), lambda i: (i, 0)),
                   pl.BlockSpec((ROW_TILE, PB_W), lambda i: (i, 0)),
                   pl.BlockSpec((ROW_TILE, PC_OUT), lambda i: (i, 0))),
        compiler_params=_cparams(("arbitrary",)),
        name="inproj",
    )(xy, mod, norms, w_in_p, cos, sin)


def _hgrn_kernel(pa_ref, lb_ref, gain_ref, o_ref, st_ref, ob_ref, *, n_ctx, n_all, n_b):
    c_sz, w = CHUNK, A_W
    same_head = (lax.broadcasted_iota(jnp.int32, (w, w), 0) // HEAD_V
                 == lax.broadcasted_iota(jnp.int32, (w, w), 1) // HEAD_V)
    head_ones = same_head.astype(BF16)
    assert c_sz == HEAD_V
    t_i = lax.broadcasted_iota(jnp.int32, (c_sz, w), 0)
    s_i = lax.broadcasted_iota(jnp.int32, (c_sz, w), 1) % c_sz
    t_row = lax.broadcasted_iota(jnp.int32, (c_sz, 1), 0)
    row8 = t_row % 8

    def grouped_rows(a, k):
        return jnp.concatenate(
            [jnp.broadcast_to(a[8 * j + k:8 * j + k + 1, :], (8, w)) for j in range(c_sz // 8)], axis=0)

    def halving_levels(rev):
        out = []
        b = c_sz // 2
        while b >= 1:
            def later(i):
                return ((i % (2 * b)) < b) if rev else ((i % (2 * b)) >= b)
            live = jnp.logical_and(t_i // (2 * b) == s_i // (2 * b),
                                   jnp.logical_and(later(t_i), jnp.logical_not(later(s_i))))
            out.append((b, live.astype(F32), jnp.where(later(t_row), 1.0, -1.0)))
            b //= 2
        return out

    n_seg = 2 * n_b
    seg_rev = [seg % 2 == 1 for seg in range(n_seg)]

    def seg_rows(seg):
        return slice(seg * c_sz, (seg + 1) * c_sz)

    def stack(parts):
        return jnp.concatenate(parts, axis=0)

    def rows_of(rows):
        return stack([jnp.broadcast_to(row, (c_sz, w)) for row in rows])

    self_mask = stack([(s_i == t_i).astype(F32)] * n_seg)
    levels = [(b, stack([live_b if rev else live_f for rev in seg_rev]),
               stack([sign_b if rev else sign_f for rev in seg_rev]))
              for (b, live_f, sign_f), (_, live_b, sign_b) in zip(halving_levels(False), halving_levels(True))]
    lb_all = rows_of([lb_ref[1:2, :] if rev else lb_ref[0:1, :] for rev in seg_rev])
    r_all = lax.broadcasted_iota(jnp.int32, (n_seg * c_sz, n_seg * c_sz), 0)
    c_all = lax.broadcasted_iota(jnp.int32, (n_seg * c_sz, n_seg * c_sz), 1)
    rev_row = (r_all // c_sz) % 2 == 1
    tri = jnp.logical_and(r_all // c_sz == c_all // c_sz, jnp.logical_or(
        jnp.logical_and(jnp.logical_not(rev_row), c_all <= r_all),
        jnp.logical_and(rev_row, c_all >= r_all))).astype(BF16)
    zero = jnp.zeros((), BF16)

    def boundary_rows(cum, b, rev):
        if b >= 8:
            return jnp.concatenate(
                [jnp.broadcast_to(cum[r0 + (b if rev else b - 1):r0 + (b if rev else b - 1) + 1, :], (2 * b, w))
                 for r0 in range(0, c_sz, 2 * b)], axis=0)
        ref = None
        for g in reversed(range(8 // (2 * b))):
            cand = grouped_rows(cum, g * 2 * b + (b if rev else b - 1))
            ref = cand if ref is None else jnp.where(row8 < (g + 1) * 2 * b, cand, ref)
        return ref

    def scan_step(n, carry):
        c_f = n
        c_b = jnp.where(n < n_ctx, n_ctx - 1 - n, n_all - 1 - (n - n_ctx))
        chunk_rows = [pl.ds(pl.multiple_of((c_b if rev else c_f) * c_sz, c_sz), c_sz) for rev in seg_rev]
        q_pre = stack([pa_ref[seg // 2, chunk_rows[seg], 0:A_W] for seg in range(n_seg)])
        v_l = [pa_ref[seg // 2, chunk_rows[seg], A_W:2 * A_W].astype(BF16) for seg in range(n_seg)]
        f_pre = stack([pa_ref[seg // 2, chunk_rows[seg], (3 if rev else 2) * A_W:(4 if rev else 3) * A_W]
                       for seg, rev in enumerate(seg_rev)])
        q = _silu(q_pre)
        f = lb_all + (1.0 - lb_all) * jax.nn.sigmoid(f_pre)
        log_f = jnp.log(jnp.maximum(f, F_MIN))
        kk = (1.0 - lb_all) * jax.nn.sigmoid(-f_pre)
        cum = sum(jnp.dot(tri, piece, preferred_element_type=F32) for piece in _split3(log_f))
        ends = [cum[seg * c_sz + (0 if rev else c_sz - 1):seg * c_sz + (0 if rev else c_sz - 1) + 1, :]
                for seg, rev in enumerate(seg_rev)]

        st_l = [st_ref[seg] for seg in range(n_seg)]
        q_in = (q * jnp.exp(cum)).astype(BF16)
        o = stack([_nt(q_in[seg_rows(seg)], st_l[seg].astype(BF16)) for seg in range(n_seg)])
        k_end = (kk * jnp.exp(rows_of(ends) - cum)).astype(BF16)
        for seg in range(n_seg):
            st_ref[seg] = (st_l[seg] * jnp.exp(ends[seg])
                           + jnp.where(same_head, _tn(v_l[seg], k_end[seg_rows(seg)]), 0.0))

        def scores(qa, ka):
            return _nt(qa, jnp.where(same_head, jnp.concatenate([ka] * HEADS, axis=0), zero))

        refs = [stack([boundary_rows(cum[seg_rows(seg)], b, rev) for seg, rev in enumerate(seg_rev)])
                for b, _, _ in levels]
        zs = [jnp.exp((cum - ref) * sign)
              for ref, (_, _, sign) in zip(refs, levels)]
        qas = [(q * z).astype(BF16) for z in zs]
        kas = [(kk * z).astype(BF16) for z in zs]
        prods = [stack([scores(qa[seg_rows(seg)], ka[seg_rows(seg)]) for seg in range(n_seg)])
                 for qa, ka in zip(qas, kas)]
        p = self_mask * jnp.dot((q * kk).astype(BF16), head_ones, preferred_element_type=F32)
        for prod, (_, live, _) in zip(prods, levels):
            p = p + live * prod
        pb = p.astype(BF16)

        for seg, rev in enumerate(seg_rev):
            out = o[seg_rows(seg)] + jnp.dot(
                pb[seg_rows(seg)], jnp.where(same_head, jnp.concatenate([v_l[seg]] * HEADS, axis=0), zero),
                preferred_element_type=F32)
            if rev:
                ob_ref[seg // 2, chunk_rows[seg], :] = out
            else:
                o_ref[seg // 2, chunk_rows[seg], :] = out
        return carry

    st_ref[...] = jnp.zeros_like(st_ref)
    lax.fori_loop(0, n_all, scan_step, 0)

    def readout(c, carry):
        rows = pl.ds(pl.multiple_of(c * ROW_TILE, ROW_TILE), ROW_TILE)
        for b in range(n_b):
            tot = o_ref[b, rows, :] + ob_ref[b, rows, :]
            ms = sum(jnp.dot(piece, head_ones, preferred_element_type=F32)
                     for piece in _split3(tot * tot)) * (1.0 / HEAD_V)
            g = pa_ref[b, rows, 4 * A_W:5 * A_W]
            o_ref[b, rows, :] = tot * lax.rsqrt(ms + NORM_EPS) * gain_ref[...] * _silu(g)
        return carry

    lax.fori_loop(0, n_all * c_sz // ROW_TILE, readout, 0)


def _hgrn(pa3, lb, gain, n_ctx_rows):
    n_batch, rows, _ = pa3.shape
    n_b = SCAN_BATCHES if n_batch % SCAN_BATCHES == 0 else 1
    kern = functools.partial(_hgrn_kernel, n_ctx=n_ctx_rows // CHUNK, n_all=rows // CHUNK, n_b=n_b)
    return pl.pallas_call(
        kern,
        out_shape=jax.ShapeDtypeStruct((n_batch, rows, A_W), F32),
        grid=(n_batch // n_b,),
        in_specs=[pl.BlockSpec((n_b, rows, PA_W), lambda b: (b, 0, 0), pipeline_mode=pl.Buffered(1)),
                  pl.BlockSpec((2, A_W), lambda b: (0, 0)),
                  pl.BlockSpec((1, A_W), lambda b: (0, 0))],
        out_specs=pl.BlockSpec((n_b, rows, A_W), lambda b: (b, 0, 0)),
        scratch_shapes=[pltpu.VMEM((2 * n_b, A_W, A_W), F32), pltpu.VMEM((n_b, rows, A_W), F32)],
        compiler_params=_cparams(("arbitrary",)),
        name="hgrn2",
    )(pa3, lb, gain)


def _split3(x):
    hi = x.astype(BF16)
    rest = x - hi.astype(F32)
    mid = rest.astype(BF16)
    return hi, mid, (rest - mid.astype(F32)).astype(BF16)


def _mlstm_kernel(pb_ref, bias_ref, gain_ref, o_ref, cn_ref, ob_ref, *, n_ctx, n_all, n_b):
    assert CHUNK == HEAD_V
    c_sz, w, qk_w = CHUNK, B_W, HEADS * B_QK
    t_i = lax.broadcasted_iota(jnp.int32, (c_sz, w), 0)
    s_i = lax.broadcasted_iota(jnp.int32, (c_sz, w), 1) % c_sz
    diag4 = s_i == t_i
    half_lane = lax.broadcasted_iota(jnp.int32, (1, LANES), 1) < HEAD_V
    k_block = (lax.broadcasted_iota(jnp.int32, (HEADS * c_sz, qk_w), 0) // c_sz
               == lax.broadcasted_iota(jnp.int32, (HEADS * c_sz, qk_w), 1) // B_QK)
    v_block = (lax.broadcasted_iota(jnp.int32, (HEADS * c_sz, w), 0) // c_sz
               == lax.broadcasted_iota(jnp.int32, (HEADS * c_sz, w), 1) // HEAD_V)
    state_block = (lax.broadcasted_iota(jnp.int32, (qk_w, 2 * w), 0) // B_QK
                   == (lax.broadcasted_iota(jnp.int32, (qk_w, 2 * w), 1) % w) // HEAD_V)
    head_ones = v_block.astype(BF16)
    q_off, k_off, v_off, o_off, g_off = 0, qk_w, 2 * qk_w, 2 * qk_w + B_W, 2 * qk_w + 2 * B_W

    sel_r = lax.broadcasted_iota(jnp.int32, (2 * LANES, 2 * w), 0)
    sel_c = lax.broadcasted_iota(jnp.int32, (2 * LANES, 2 * w), 1)
    sel = (sel_r == jnp.where(sel_c < w, HEADS + sel_c // HEAD_V, LANES + (sel_c - w) // HEAD_V)).astype(BF16)

    def head_max(x):
        outs = []
        for col in range(w // LANES):
            xc = x[:, col * LANES:(col + 1) * LANES]
            lo = jnp.max(jnp.where(half_lane, xc, -jnp.inf), axis=-1, keepdims=True)
            hi = jnp.max(jnp.where(half_lane, -jnp.inf, xc), axis=-1, keepdims=True)
            outs.append(jnp.where(half_lane, lo, hi))
        return jnp.concatenate(outs, axis=1)

    n_seg = 2 * n_b
    seg_rev = [seg % 2 == 1 for seg in range(n_seg)]

    def seg_rows(seg):
        return slice(seg * c_sz, (seg + 1) * c_sz)

    mask = jnp.concatenate([(s_i >= t_i) if rev else (s_i <= t_i) for rev in seg_rev], axis=0)
    r_all = lax.broadcasted_iota(jnp.int32, (n_seg * c_sz, n_seg * c_sz), 0)
    c_all = lax.broadcasted_iota(jnp.int32, (n_seg * c_sz, n_seg * c_sz), 1)
    same_seg = r_all // c_sz == c_all // c_sz
    rev_row = (r_all // c_sz) % 2 == 1
    tri = jnp.logical_and(same_seg, jnp.logical_or(
        jnp.logical_and(jnp.logical_not(rev_row), c_all <= r_all),
        jnp.logical_and(rev_row, c_all >= r_all))).astype(BF16)

    def per_seg(fn, x):
        return jnp.concatenate([jnp.broadcast_to(fn(x[seg_rows(seg)]), (c_sz, w)) for seg in range(n_seg)],
                               axis=0)

    def rows_of(rows):
        return jnp.concatenate([jnp.broadcast_to(row, (c_sz, w)) for row in rows], axis=0)

    def scan_step(n, m_prev_rows):
        c_f = n
        c_b = jnp.where(n < n_ctx, n_ctx - 1 - n, n_all - 1 - (n - n_ctx))
        chunk_rows = [pl.ds(pl.multiple_of((c_b if rev else c_f) * c_sz, c_sz), c_sz) for rev in seg_rev]

        def load(lo, hi):
            return [pb_ref[seg // 2, chunk_rows[seg], lo:hi] for seg in range(n_seg)]

        q_l = load(q_off, q_off + qk_w)
        k_l = load(k_off, k_off + qk_w)
        v_l = load(v_off, v_off + B_W)
        g_l = load(g_off, g_off + LANES)
        gts = jnp.concatenate(
            [pltpu.roll(g + bias_ref[...], LANES - 2 * HEADS, 1) if rev else g + bias_ref[...]
             for g, rev in zip(g_l, seg_rev)], axis=0)
        log_f = jnp.minimum(gts, 0.0) - jnp.log(1.0 + jnp.exp(-jnp.abs(gts)))
        cum_col = sum(jnp.dot(tri, p, preferred_element_type=F32) for p in _split3(log_f))
        both = sum(jnp.dot(p, sel, preferred_element_type=F32)
                   for p in _split3(jnp.concatenate([cum_col, gts], axis=1)))
        cum_t = both[:, :w]
        ig_t = both[:, w:]
        src = per_seg(lambda x: jnp.sum(jnp.where(diag4, x, 0.0), axis=0, keepdims=True), cum_t - ig_t)
        m_prev = rows_of(m_prev_rows)
        log_d = jnp.where(mask, cum_t - src, MASK_NEG)
        log_inter = cum_t + m_prev
        m_t = jnp.maximum(log_inter, head_max(log_d))
        w_intra = jnp.where(mask, jnp.exp(log_d - m_t), 0.0)
        w_inter = jnp.exp(log_inter - m_t)

        scale = B_QK ** -0.5
        qb_l = [(q * scale).astype(BF16) for q in q_l]
        kb_l = [k.astype(BF16) for k in k_l]
        zero = jnp.zeros((), BF16)
        scores = [_nt(qb, jnp.where(k_block, jnp.concatenate([kb] * HEADS, axis=0), zero))
                  for qb, kb in zip(qb_l, kb_l)]
        p = (jnp.concatenate(scores, axis=0) * w_intra).astype(BF16)

        cn_l = [cn_ref[seg] for seg in range(n_seg)]
        inter = jnp.concatenate([jnp.dot(qb, cn.astype(BF16), preferred_element_type=F32)
                                 for qb, cn in zip(qb_l, cn_l)], axis=0)
        values = [jnp.dot(p[seg_rows(seg)],
                          jnp.where(v_block, jnp.concatenate([v_l[seg].astype(BF16)] * HEADS, axis=0), zero),
                          preferred_element_type=F32) for seg in range(n_seg)]
        num = w_inter * inter[:, :w] + jnp.concatenate(values, axis=0)
        den = w_inter * inter[:, w:] + jnp.dot(p, head_ones, preferred_element_type=F32)
        h_out = num / jnp.maximum(jnp.abs(den), jnp.exp(-m_t))
        for seg, rev in enumerate(seg_rev):
            if rev:
                ob_ref[seg // 2, chunk_rows[seg], :] = h_out[seg_rows(seg)]
            else:
                o_ref[seg // 2, chunk_rows[seg], :] = h_out[seg_rows(seg)]

        ends = [cum_t[seg * c_sz + (0 if rev else c_sz - 1):seg * c_sz + (0 if rev else c_sz - 1) + 1, :]
                for seg, rev in enumerate(seg_rev)]
        log_end = rows_of(ends) - cum_t + ig_t
        m_ends = [jnp.maximum(end + m_old, jnp.max(log_end[seg_rows(seg)], axis=0, keepdims=True))
                  for seg, (end, m_old) in enumerate(zip(ends, m_prev_rows))]
        w_end = jnp.exp(log_end - rows_of(m_ends))
        for seg in range(n_seg):
            w_seg = w_end[seg_rows(seg)]
            upd = _tn(kb_l[seg], jnp.concatenate([(w_seg * v_l[seg]).astype(BF16), w_seg.astype(BF16)], axis=1))
            carry_w = jnp.exp(ends[seg] + m_prev_rows[seg] - m_ends[seg])
            cn_ref[seg] = (jnp.concatenate([carry_w, carry_w], axis=1) * cn_l[seg]
                           + jnp.where(state_block, upd, 0.0))
        return tuple(m_ends)

    cn_ref[...] = jnp.zeros_like(cn_ref)
    lax.fori_loop(0, n_all, scan_step, tuple(jnp.zeros((1, w), F32) for _ in range(n_seg)))

    def readout(c, carry):
        rows = pl.ds(pl.multiple_of(c * ROW_TILE, ROW_TILE), ROW_TILE)
        for b in range(n_b):
            tot = o_ref[b, rows, :] + ob_ref[b, rows, :]
            normed = jnp.concatenate(
                [_rms(tot[:, h * HEAD_V:(h + 1) * HEAD_V]) for h in range(HEADS)], axis=1) * gain_ref[...]
            og = pb_ref[b, rows, o_off:o_off + B_W]
            o_ref[b, rows, :] = normed * jax.nn.sigmoid(og)
        return carry

    lax.fori_loop(0, n_all * c_sz // ROW_TILE, readout, 0)


def _mlstm(pb3, bias, gain, n_ctx_rows):
    n_batch, rows, _ = pb3.shape
    n_b = SCAN_BATCHES if n_batch % SCAN_BATCHES == 0 else 1
    kern = functools.partial(_mlstm_kernel, n_ctx=n_ctx_rows // CHUNK, n_all=rows // CHUNK, n_b=n_b)
    return pl.pallas_call(
        kern,
        out_shape=jax.ShapeDtypeStruct((n_batch, rows, B_W), F32),
        grid=(n_batch // n_b,),
        in_specs=[pl.BlockSpec((n_b, rows, PB_W), lambda b: (b, 0, 0), pipeline_mode=pl.Buffered(1)),
                  pl.BlockSpec((1, LANES), lambda b: (0, 0)),
                  pl.BlockSpec((1, B_W), lambda b: (0, 0))],
        out_specs=pl.BlockSpec((n_b, rows, B_W), lambda b: (b, 0, 0)),
        scratch_shapes=[pltpu.VMEM((2 * n_b, HEADS * B_QK, 2 * B_W), F32), pltpu.VMEM((n_b, rows, B_W), F32)],
        compiler_params=_cparams(("arbitrary",)),
        name="mlstm",
    )(pb3, bias, gain)


def _attn_kernel(lam_ref, qp_ref, qr_ref, k_ref, v_ref, gain_ref, o_ref, *, n_ctx, q_tile0, lam_init):
    lam = lam_ref[0]
    q_tile = pl.program_id(2) + q_tile0
    lane = lax.broadcasted_iota(jnp.int32, (1, 2 * C_DQK), 1)

    n_all = k_ref.shape[1]

    def finish(head, parts):
        o = parts[0] - lam * parts[1]
        o_ref[0, :, head] = _rms(o) * gain_ref[...] * (1.0 - lam_init)

    def sub_query(q, j):
        return jnp.where(lane // C_DQK == j, q, jnp.zeros_like(q))

    def row_max(s):
        return jnp.max(s, axis=-1, keepdims=True)

    def row_sum(s):
        return jnp.sum(s, axis=-1, keepdims=True)

    def pv(ex, v):
        return jnp.dot(ex.astype(BF16), v, preferred_element_type=F32)

    heads = [slice(hh * C_DV, (hh + 1) * C_DV) for hh in range(ATTN_HEADS_PER_STEP)]

    def attend(pieces):
        problems = [(head, j) for head in heads for j in range(2)]
        scores = [[_nt(sub_query(q_ref[0, :, head], j), k_ref[0, k0:k1, head]) for q_ref, k0, k1 in pieces]
                  for head, j in problems]
        maxes = [functools.reduce(jnp.maximum, [row_max(s) for s in ss]) for ss in scores]
        exps = [[jnp.exp2(s - m) for s in ss] for ss, m in zip(scores, maxes)]
        outs = [sum(pv(e, v_ref[0, k0:k1, head]) for e, (_, k0, k1) in zip(es, pieces))
                / sum(row_sum(e) for e in es) for es, (head, _) in zip(exps, problems)]
        for n, head in enumerate(heads):
            finish(head, outs[2 * n:2 * n + 2])

    @pl.when(q_tile * ROW_TILE < n_ctx)
    def _():
        attend([(qp_ref, 0, n_ctx)])

    @pl.when(q_tile * ROW_TILE >= n_ctx)
    def _():
        attend([(qp_ref, 0, n_ctx), (qr_ref, n_ctx, n_all)])


def _attention(pc3, lam, gain, n_ctx_rows, lam_init, with_ctx):
    n_batch, rows, _ = pc3.shape
    q_tile0 = 0 if with_ctx else n_ctx_rows // ROW_TILE
    n_q = rows // ROW_TILE - q_tile0
    gw = ATTN_HEADS_PER_STEP * C_DV
    hb = C_W // gw
    kern = functools.partial(_attn_kernel, n_ctx=n_ctx_rows, q_tile0=q_tile0, lam_init=lam_init)
    grid_spec = pltpu.PrefetchScalarGridSpec(
        num_scalar_prefetch=1,
        grid=(n_batch, hb, n_q),
        in_specs=[pl.BlockSpec((1, ROW_TILE, gw), lambda b, h, i, lam: (b, i + q_tile0, h)),
                  pl.BlockSpec((1, ROW_TILE, gw), lambda b, h, i, lam: (b, i + q_tile0, hb + h)),
                  pl.BlockSpec((1, rows, gw), lambda b, h, i, lam: (b, 0, 2 * hb + h)),
                  pl.BlockSpec((1, rows, gw), lambda b, h, i, lam: (b, 0, 3 * hb + h)),
                  pl.BlockSpec((1, C_DV), lambda b, h, i, lam: (0, 0))],
        out_specs=pl.BlockSpec((1, ROW_TILE, gw), lambda b, h, i, lam: (b, i, h)),
    )
    return pl.pallas_call(
        kern,
        out_shape=jax.ShapeDtypeStruct((n_batch, n_q * ROW_TILE, C_W), F32),
        grid_spec=grid_spec,
        compiler_params=_cparams(("arbitrary", "arbitrary", "arbitrary")),
        name="diff_attn",
    )(lam, pc3, pc3, pc3, pc3, gain)


def _outproj_kernel(oa_ref, ob_ref, oc_ref, x_ref, mod_ref, n_ref, w_ref, rw_ref, rb_ref,
                    xn_ref, h_ref, route_ref, cnt_ref, carry_ref):
    d = x_ref.shape[1]
    n_exp = rw_ref.shape[1]

    @pl.when(pl.program_id(0) == 0)
    def _():
        carry_ref[...] = jnp.zeros_like(carry_ref)

    mix = (jnp.dot(oa_ref[...].astype(BF16), w_ref[0:A_W, :], preferred_element_type=F32)
           + jnp.dot(ob_ref[...].astype(BF16), w_ref[A_W:A_W + B_W, :], preferred_element_type=F32)
           + jnp.dot(oc_ref[...].astype(BF16), w_ref[A_W + B_W:A_W + B_W + C_W, :], preferred_element_type=F32))
    xn = x_ref[...] + mod_ref[0, :, 2 * d:3 * d] * (_rms(mix) * n_ref[1:2, :])
    xn_ref[...] = xn
    h = _rms(xn) * n_ref[2:3, :] * (1.0 + mod_ref[0, :, 4 * d:5 * d]) + mod_ref[0, :, 3 * d:4 * d]
    h_ref[...] = _pack_bf16_pairs(h)

    h_hi = h.astype(BF16)
    h_mid = (h - h_hi.astype(F32)).astype(BF16)
    logits = _nt(rw_ref[0], h_hi) + _nt(rw_ref[1], h_hi) + _nt(rw_ref[0], h_mid) + rb_ref[...]
    e_sub = lax.broadcasted_iota(jnp.int32, logits.shape, 0)
    cur = logits
    picks, vals = [], []
    for _ in range(TOP_K):
        mx = jnp.max(cur, axis=0, keepdims=True)
        idx = jnp.min(jnp.where(cur == mx, e_sub, n_exp), axis=0, keepdims=True)
        hit = e_sub == idx
        cur = jnp.where(hit, -jnp.inf, cur)
        picks.append((idx, hit.astype(F32)))
        vals.append(mx)
    exps = [jnp.exp(vv - vals[0]) for vv in vals]
    total = exps[0] + exps[1] + exps[2] + exps[3]

    chosen = picks[0][1] + picks[1][1] + picks[2][1] + picks[3][1]
    tm = logits.shape[1]
    before = (lax.broadcasted_iota(jnp.int32, (tm, tm), 0) < lax.broadcasted_iota(jnp.int32, (tm, tm), 1))
    seen = jnp.dot(chosen.astype(BF16), before.astype(BF16), preferred_element_type=F32) + carry_ref[...]
    new_carry = carry_ref[...] + jnp.sum(chosen, axis=1, keepdims=True)
    carry_ref[...] = new_carry
    cnt_ref[...] = new_carry

    o_row = lax.broadcasted_iota(jnp.int32, route_ref.shape, 0)
    route = jnp.zeros(route_ref.shape, F32)
    for kk in range(TOP_K):
        idx, hit = picks[kk]
        rank = jnp.sum(hit * seen, axis=0, keepdims=True)
        route = (route + jnp.where(o_row == kk, idx.astype(F32), 0.0)
                 + jnp.where(o_row == TOP_K + kk, exps[kk] / total, 0.0)
                 + jnp.where(o_row == 2 * TOP_K + kk, rank, 0.0))
    route_ref[...] = route


ROUTE_ROWS = 16


def _outproj(oa, ob, oc, xy, mod, norms, w_out_b, router_wt, router_b, layer, n_tiles, in_tile, oc_tile, mod_row):
    d = xy.shape[1]
    n_exp = router_wt.shape[2]
    t_out = n_tiles * ROW_TILE
    xn, h, route_t, counts = pl.pallas_call(
        _outproj_kernel,
        out_shape=(jax.ShapeDtypeStruct((t_out, d), F32),
                   jax.ShapeDtypeStruct((t_out, d // 2), jnp.int32),
                   jax.ShapeDtypeStruct((n_tiles * ROUTE_ROWS, ROW_TILE), F32),
                   jax.ShapeDtypeStruct((n_exp, 1), F32)),
        grid=(n_tiles,),
        in_specs=[pl.BlockSpec((ROW_TILE, A_W), lambda i: (in_tile(i), 0)),
                  pl.BlockSpec((ROW_TILE, B_W), lambda i: (in_tile(i), 0)),
                  pl.BlockSpec((ROW_TILE, C_W), lambda i: (oc_tile(i), 0)),
                  pl.BlockSpec((ROW_TILE, d), lambda i: (in_tile(i), 0)),
                  pl.BlockSpec((1, 1, 6 * d), lambda i: (mod_row(i), 0, 0)),
                  pl.BlockSpec((None, 4, d), lambda i: (layer, 0, 0)),
                  pl.BlockSpec((None, d, d), lambda i: (layer, 0, 0)),
                  pl.BlockSpec((None, 2, n_exp, d), lambda i: (layer, 0, 0, 0)),
                  pl.BlockSpec((None, n_exp, 1), lambda i: (layer, 0, 0))],
        out_specs=(pl.BlockSpec((ROW_TILE, d), lambda i: (i, 0)),
                   pl.BlockSpec((ROW_TILE, d // 2), lambda i: (i, 0)),
                   pl.BlockSpec((ROUTE_ROWS, ROW_TILE), lambda i: (i, 0)),
                   pl.BlockSpec((n_exp, 1), lambda i: (0, 0))),
        scratch_shapes=[pltpu.VMEM((n_exp, 1), F32)],
        compiler_params=_cparams(("arbitrary",)),
        name="outproj_router",
    )(oa, ob, oc, xy, mod, norms, w_out_b, router_wt, router_b)
    return xn, h, route_t.reshape(n_tiles, ROUTE_ROWS, ROW_TILE), counts[:, 0]


PAIR_BLOCK = 2 * LANES


def _moe_kernel(te_ref, first_ref, valid_ref, nu_ref, slot_ref, next_ref, xs_ref, w1_hbm, b1_ref, w2_hbm, b2_ref,
                ys_ref, w1p_ref, w2b_ref, w1_buf, w2_buf, w_sem, *, layer):
    i = pl.program_id(0)
    two_f = w1_buf.shape[2]
    n_blk = two_f // PAIR_BLOCK

    def fetch(expert, slot):
        return (pltpu.make_async_copy(w1_hbm.at[layer, expert], w1_buf.at[slot], w_sem.at[0, slot]),
                pltpu.make_async_copy(w2_hbm.at[layer, expert], w2_buf.at[slot], w_sem.at[1, slot]))

    @pl.when(i == 0)
    def _():
        for copy in fetch(te_ref[0], 0):
            copy.start()

    @pl.when(jnp.logical_and(i < nu_ref[0], first_ref[i] == 1))
    def _():
        slot = slot_ref[i]
        for copy in fetch(te_ref[i], slot):
            copy.wait()
        r = lax.broadcasted_iota(jnp.int32, (PAIR_BLOCK, PAIR_BLOCK), 0)
        c = lax.broadcasted_iota(jnp.int32, (PAIR_BLOCK, PAIR_BLOCK), 1)
        perm = (r == jnp.where(c < LANES, 2 * c, 2 * (c - LANES) + 1)).astype(BF16)
        for blk in range(n_blk):
            cols = slice(blk * PAIR_BLOCK, (blk + 1) * PAIR_BLOCK)
            w1p_ref[:, cols] = jnp.dot(w1_buf[slot, :, cols].astype(BF16), perm,
                                       preferred_element_type=F32).astype(BF16)
        w2b_ref[...] = w2_buf[slot].astype(BF16)

        @pl.when(next_ref[i] >= 0)
        def _():
            for copy in fetch(next_ref[i], 1 - slot):
                copy.start()

    tile_rows = xs_ref.shape[0]
    used = i < nu_ref[0]
    valid = valid_ref[i]

    def expert_ffn(n_rows):
        x = _unpack_bf16_pairs(xs_ref[0:n_rows, :]).astype(BF16)
        row = lax.broadcasted_iota(jnp.int32, (n_rows, 1), 0)
        x = jnp.where(row < valid, x, jnp.zeros_like(x))
        hid = jnp.dot(x, w1p_ref[...], preferred_element_type=F32) + b1_ref[...]
        acts = []
        for blk in range(n_blk):
            glu = jnp.minimum(hid[:, blk * PAIR_BLOCK:blk * PAIR_BLOCK + LANES], SWIGLU_LIMIT)
            lin = jnp.clip(hid[:, blk * PAIR_BLOCK + LANES:(blk + 1) * PAIR_BLOCK], -SWIGLU_LIMIT, SWIGLU_LIMIT)
            acts.append((glu * jax.nn.sigmoid(SWIGLU_ALPHA * glu) * (lin + 1.0)).astype(BF16))
        y = jnp.dot(jnp.concatenate(acts, axis=1), w2b_ref[...], preferred_element_type=F32) + b2_ref[...]
        ys_ref[0:n_rows, :] = _pack_bf16_pairs(y)
        if n_rows < tile_rows:
            ys_ref[n_rows:tile_rows, :] = jnp.zeros((tile_rows - n_rows, ys_ref.shape[1]), ys_ref.dtype)

    @pl.when(jnp.logical_and(used, valid > tile_rows // 2))
    def _():
        expert_ffn(tile_rows)

    @pl.when(jnp.logical_and(used, valid <= tile_rows // 2))
    def _():
        expert_ffn(tile_rows // 2)

    @pl.when(jnp.logical_not(used))
    def _():
        ys_ref[...] = jnp.zeros_like(ys_ref)


def _moe(plan, xs, w1, b1p, w2, b2, layer):
    r_max, half_d = xs.shape
    d = 2 * half_d
    two_f = w1.shape[3]
    f = two_f // 2
    n_tiles = r_max // MOE_TILE

    def row_tile(i, te, fi, va, nu, sl, nx):
        return (jnp.maximum(jnp.minimum(i, nu[0] - 1), 0), 0)

    def bias_block(i, te, fi, va, nu, sl, nx):
        return (layer, te[i], 0, 0)

    grid_spec = pltpu.PrefetchScalarGridSpec(
        num_scalar_prefetch=6,
        grid=(n_tiles,),
        in_specs=[pl.BlockSpec((MOE_TILE, half_d), row_tile),
                  pl.BlockSpec(memory_space=pl.ANY),
                  pl.BlockSpec((None, None, 1, two_f), bias_block),
                  pl.BlockSpec(memory_space=pl.ANY),
                  pl.BlockSpec((None, None, 1, d), bias_block)],
        out_specs=pl.BlockSpec((MOE_TILE, half_d), lambda i, te, fi, va, nu, sl, nx: (i, 0)),
        scratch_shapes=[pltpu.VMEM((d, two_f), BF16), pltpu.VMEM((f, d), BF16),
                        pltpu.VMEM((2, d, two_f), F32), pltpu.VMEM((2, f, d), F32),
                        pltpu.SemaphoreType.DMA((2, 2))],
    )
    return pl.pallas_call(
        functools.partial(_moe_kernel, layer=layer),
        out_shape=jax.ShapeDtypeStruct((r_max, half_d), jnp.int32),
        grid_spec=grid_spec,
        compiler_params=_cparams(("arbitrary",)),
        name="moe_experts",
    )(*plan, xs, w1, b1p, w2, b2)


SC_CORES = 2
SC_SUBCORES = 16
SC_CHUNK = 64


def _dispatch_rows(hp, pos, r_max):
    t, width = hp.shape
    workers = SC_CORES * SC_SUBCORES
    assert t % (workers * SC_CHUNK) == 0
    per_worker = t // (workers * SC_CHUNK)
    idx = pos.reshape(-1, TOP_K, ROW_TILE // SC_CHUNK, SC_CHUNK).transpose(0, 2, 1, 3).reshape(
        workers, per_worker * TOP_K, SC_CHUNK)
    mesh = plsc.VectorSubcoreMesh(core_axis_name="c", subcore_axis_name="s",
                                  num_cores=SC_CORES, num_subcores=SC_SUBCORES)

    @functools.partial(
        pl.kernel, mesh=mesh,
        out_type=jax.ShapeDtypeStruct((r_max, width), hp.dtype),
        scratch_types=[pltpu.VMEM((per_worker * TOP_K, SC_CHUNK), jnp.int32),
                       pltpu.VMEM((2, SC_CHUNK, width), hp.dtype),
                       pltpu.SemaphoreType.DMA((2,)),
                       pltpu.SemaphoreType.DMA((2,))],
    )
    def scatter(hp_hbm, idx_hbm, out_hbm, idx_v, rows_v, read_sem, write_sem):
        wid = lax.axis_index("s") * SC_CORES + lax.axis_index("c")
        pltpu.sync_copy(idx_hbm.at[wid], idx_v)

        def read(j):
            row0 = pl.multiple_of((wid * per_worker + j) * SC_CHUNK, SC_CHUNK)
            return pltpu.make_async_copy(hp_hbm.at[pl.ds(row0, SC_CHUNK)], rows_v.at[j % 2], read_sem.at[j % 2])

        def write(j, k):
            return pltpu.make_async_copy(rows_v.at[j % 2], out_hbm.at[idx_v.at[j * TOP_K + k]],
                                         write_sem.at[j % 2])

        read(0).start()
        for j in range(per_worker):
            read(j).wait()
            for k in range(TOP_K):
                write(j, k).start()
            if j + 1 < per_worker:
                if j >= 1:
                    for k in range(TOP_K):
                        write(j - 1, k).wait()
                read(j + 1).start()
        for j in range(max(per_worker - 2, 0), per_worker):
            for k in range(TOP_K):
                write(j, k).wait()

    return scatter(hp, idx)


def _combine_rows(ysp, pos):
    width = ysp.shape[1]
    t = pos.shape[0] * pos.shape[2]
    workers = SC_CORES * SC_SUBCORES
    assert t % (workers * SC_CHUNK) == 0
    per_worker = t // (workers * SC_CHUNK)
    units = per_worker * TOP_K
    idx = pos.reshape(-1, TOP_K, ROW_TILE // SC_CHUNK, SC_CHUNK).transpose(0, 2, 1, 3).reshape(
        workers, units, SC_CHUNK)
    mesh = plsc.VectorSubcoreMesh(core_axis_name="c", subcore_axis_name="s",
                                  num_cores=SC_CORES, num_subcores=SC_SUBCORES)

    @functools.partial(
        pl.kernel, mesh=mesh,
        out_type=jax.ShapeDtypeStruct((TOP_K, t, width), ysp.dtype),
        scratch_types=[pltpu.VMEM((units, SC_CHUNK), jnp.int32),
                       pltpu.VMEM((2, SC_CHUNK, width), ysp.dtype),
                       pltpu.SemaphoreType.DMA((2,)),
                       pltpu.SemaphoreType.DMA((2,))],
    )
    def gather(ys_hbm, idx_hbm, out_hbm, idx_v, rows_v, read_sem, write_sem):
        wid = lax.axis_index("s") * SC_CORES + lax.axis_index("c")
        pltpu.sync_copy(idx_hbm.at[wid], idx_v)

        def read(u):
            return pltpu.make_async_copy(ys_hbm.at[idx_v.at[u]], rows_v.at[u % 2], read_sem.at[u % 2])

        def write(u):
            row0 = pl.multiple_of((wid * per_worker + u // TOP_K) * SC_CHUNK, SC_CHUNK)
            return pltpu.make_async_copy(rows_v.at[u % 2], out_hbm.at[u % TOP_K, pl.ds(row0, SC_CHUNK)],
                                         write_sem.at[u % 2])

        read(0).start()
        for u in range(units):
            read(u).wait()
            write(u).start()
            if u + 1 < units:
                if u >= 1:
                    write(u - 1).wait()
                read(u + 1).start()
        for u in range(max(units - 2, 0), units):
            write(u).wait()

    return gather(ysp, idx)


def _ffn_residual_kernel(x_ref, w_ref, y0_ref, y1_ref, y2_ref, y3_ref, mod_ref, n_ref, o_ref):
    d = x_ref.shape[1]
    wts = w_ref[...]
    ffn = None
    for k, y_ref in enumerate((y0_ref, y1_ref, y2_ref, y3_ref)):
        term = wts[:, k:k + 1] * _unpack_bf16_pairs(y_ref[...])
        ffn = term if ffn is None else ffn + term
    o_ref[...] = x_ref[...] + mod_ref[0, :, 5 * d:6 * d] * (_rms(ffn) * n_ref[3:4, :])


def _ffn_residual(xn, wts, ys_by_k, mod, norms, layer, mod_row):
    t, d = xn.shape
    row_spec = pl.BlockSpec((ROW_TILE, d), lambda i: (i, 0))

    def y_spec(k):
        return pl.BlockSpec((None, ROW_TILE, d // 2), lambda i: (k, i, 0))

    return pl.pallas_call(
        _ffn_residual_kernel,
        out_shape=jax.ShapeDtypeStruct((t, d), F32),
        grid=(t // ROW_TILE,),
        in_specs=[row_spec,
                  pl.BlockSpec((ROW_TILE, TOP_K), lambda i: (i, 0)),
                  y_spec(0), y_spec(1), y_spec(2), y_spec(3),
                  pl.BlockSpec((1, 1, 6 * d), lambda i: (mod_row(i), 0, 0)),
                  pl.BlockSpec((None, 4, d), lambda i: (layer, 0, 0))],
        out_specs=row_spec,
        compiler_params=_cparams(("arbitrary",)),
        name="ffn_residual",
    )(xn, wts, ys_by_k, ys_by_k, ys_by_k, ys_by_k, mod, norms)


def _dispatch_plan(route_t, counts, r_max):
    n_row_tiles = route_t.shape[0]
    idx = route_t[:, 0:TOP_K, :].astype(jnp.int32)
    wts = route_t[:, TOP_K:2 * TOP_K, :].transpose(0, 2, 1).reshape(n_row_tiles * ROW_TILE, TOP_K)
    rank = route_t[:, 2 * TOP_K:3 * TOP_K, :].astype(jnp.int32)
    cnt = counts.astype(jnp.int32)
    padded = (cnt + MOE_TILE - 1) // MOE_TILE * MOE_TILE
    ends = jnp.cumsum(padded)
    starts = ends - padded
    pos = rank
    for e in range(cnt.shape[0]):
        pos = pos + jnp.where(idx == e, starts[e], 0)
    n_tiles = r_max // MOE_TILE
    n_used = ends[-1] // MOE_TILE
    tile_ids = jnp.minimum(jnp.arange(n_tiles, dtype=jnp.int32), n_used - 1)
    tile_expert = jnp.sum((ends // MOE_TILE)[None, :] <= tile_ids[:, None], axis=1).astype(jnp.int32)
    tile_first = jnp.concatenate(
        [jnp.ones((1,), jnp.int32), (tile_expert[1:] != tile_expert[:-1]).astype(jnp.int32)])
    of_expert = (tile_expert[:, None] == jnp.arange(cnt.shape[0], dtype=jnp.int32)[None, :]).astype(jnp.int32)
    first_tile = jnp.sum(of_expert * ((ends - padded) // MOE_TILE)[None, :], axis=1)
    tile_valid = jnp.clip(jnp.sum(of_expert * cnt[None, :], axis=1) - (tile_ids - first_tile) * MOE_TILE,
                          0, MOE_TILE).astype(jnp.int32)
    experts = jnp.arange(cnt.shape[0], dtype=jnp.int32)
    has_rows = cnt > 0
    tile_slot = (jnp.sum(of_expert * (jnp.cumsum(has_rows.astype(jnp.int32)) - 1)[None, :], axis=1) % 2).astype(jnp.int32)
    later_with_rows = jnp.logical_and(has_rows[None, :], experts[None, :] > experts[:, None])
    next_expert = jnp.min(jnp.where(later_with_rows, experts[None, :], cnt.shape[0]), axis=1)
    next_expert = jnp.where(next_expert < cnt.shape[0], next_expert, -1)
    tile_next = jnp.sum(of_expert * next_expert[None, :], axis=1).astype(jnp.int32)
    plan = (tile_expert, tile_first, tile_valid, n_used.reshape(1).astype(jnp.int32), tile_slot, tile_next)
    return pos, wts, plan


def _rope_tables(length, n_ctx_rows):
    rows = length // GRID_W
    row = jnp.repeat(jnp.arange(rows, dtype=F32), GRID_W)
    col = jnp.tile(jnp.arange(GRID_W, dtype=F32), rows)
    n_freq = C_DQK // 4
    inv_freq = ROPE_THETA ** (-jnp.arange(n_freq, dtype=F32) / n_freq)
    ang_r = row[:, None] * inv_freq
    ang_c = col[:, None] * inv_freq
    cos = jnp.concatenate([jnp.cos(ang_r), jnp.cos(ang_r), jnp.cos(ang_c), jnp.cos(ang_c)], axis=-1)
    sin = jnp.concatenate([-jnp.sin(ang_r), jnp.sin(ang_r), -jnp.sin(ang_c), jnp.sin(ang_c)], axis=-1)
    reps = C_W // C_DQK
    cos = jnp.concatenate([jnp.ones((n_ctx_rows, C_W), F32), jnp.tile(cos, (1, reps))], axis=0)
    sin = jnp.concatenate([jnp.zeros((n_ctx_rows, C_W), F32), jnp.tile(sin, (1, reps))], axis=0)
    return cos, sin


def _hgrn_lower_bound(table, layer):
    p = jax.nn.softmax(table.astype(F32), axis=1)
    cum = jnp.cumsum(p, axis=1) - p[:, :1]
    return jnp.clip(cum[:, layer], 0.0, 1.0)


def kernel(x, c, ctx, c_ctx, ada_w, ada_b, sandwich_norms, w_in, w_out, hgrn_lower_bounds, hgrn_norm,
           mlstm_gate_bias, mlstm_norm, diff_lambdas, diff_norm, router_w, router_b, moe_w1, moe_b1,
           moe_w2, moe_b2):
    n_batch, seq, d = x.shape
    n_ctx_rows = ctx.shape[1]
    depth = w_in.shape[0]
    n_exp = router_w.shape[2]
    assert seq % ROW_TILE == 0 and n_ctx_rows % ROW_TILE == 0 and seq % GRID_W == 0
    rows_b = n_ctx_rows + seq
    tiles_b = rows_b // ROW_TILE
    ctx_tiles = n_ctx_rows // ROW_TILE
    lat_tiles = seq // ROW_TILE

    b_main = PA_W + 2 * HEADS * B_QK + 2 * B_W
    w_in_p = jnp.concatenate(
        [w_in[:, :, :b_main],
         jnp.pad(w_in[:, :, b_main:b_main + N_GATE], ((0, 0), (0, 0), (0, LANES - N_GATE))),
         w_in[:, :, b_main + N_GATE:]], axis=2).astype(BF16)
    w_out_b = w_out.astype(BF16)
    two_f = moe_b1.shape[2]
    b1p = moe_b1.reshape(depth, n_exp, two_f // PAIR_BLOCK, LANES, 2).transpose(0, 1, 2, 4, 3).reshape(
        depth, n_exp, 1, two_f)
    b2 = moe_b2[:, :, None, :]
    ada_b3 = ada_b[:, None, :]
    router_b3 = router_b[:, :, None]
    rw_t = router_w.transpose(0, 2, 1)
    rw_hi = rw_t.astype(BF16)
    router_wt = jnp.stack([rw_hi, (rw_t - rw_hi.astype(F32)).astype(BF16)], axis=1)
    gate_bias = jnp.pad(mlstm_gate_bias, ((0, 0), (0, LANES - N_GATE)))
    cos, sin = _rope_tables(seq, n_ctx_rows)

    cond_rows = (n_batch + 1 + 7) // 8 * 8
    cvec = jnp.zeros((cond_rows, d), F32).at[:n_batch].set(c).at[n_batch].set(c_ctx)

    def mod_row_all(i):
        return jnp.where(i % tiles_b < ctx_tiles, n_batch, i // tiles_b)

    xy = jnp.concatenate([ctx, x], axis=1).reshape(n_batch * rows_b, d)

    for layer in range(depth):
        last = layer == depth - 1
        mod = _adaln(cvec, ada_w, ada_b3, layer).reshape(cond_rows, 1, 6 * d)
        lb = _hgrn_lower_bound(hgrn_lower_bounds, layer)
        lam_init = 0.8 - 0.6 * math.exp(-0.3 * layer)
        lq1, lk1, lq2, lk2 = diff_lambdas[layer].astype(F32)
        lam = (jnp.exp(jnp.sum(lq1 * lk1)) - jnp.exp(jnp.sum(lq2 * lk2)) + lam_init).reshape(1)

        pa, pb, pc = _inproj(xy, mod, sandwich_norms, w_in_p, cos, sin, layer, tiles_b, ctx_tiles, n_batch)
        oa = _hgrn(pa.reshape(n_batch, rows_b, PA_W), lb, jnp.tile(hgrn_norm[layer], HEADS)[None, :], n_ctx_rows)
        ob = _mlstm(pb.reshape(n_batch, rows_b, PB_W), gate_bias[layer][None, :], mlstm_norm[layer][None, :],
                    n_ctx_rows)
        oc = _attention(pc.reshape(n_batch, rows_b, PC_OUT), lam, diff_norm[layer][None, :], n_ctx_rows,
                        lam_init, not last)

        if last:
            n_tiles = n_batch * lat_tiles
            in_tile = lambda i: (i // lat_tiles) * tiles_b + ctx_tiles + i % lat_tiles
            mod_row = lambda i: i // lat_tiles
        else:
            n_tiles = n_batch * tiles_b
            in_tile = lambda i: i
            mod_row = mod_row_all
        xn, h, route, counts = _outproj(
            oa.reshape(-1, A_W), ob.reshape(-1, B_W), oc.reshape(-1, C_W), xy, mod, sandwich_norms, w_out_b,
            router_wt, router_b3, layer, n_tiles, in_tile, lambda i: i, mod_row)

        r_max = n_tiles * ROW_TILE * TOP_K + n_exp * MOE_TILE
        pos, wts, plan = _dispatch_plan(route, counts, r_max)
        xs = _dispatch_rows(h, pos, r_max)
        ys = _moe(plan, xs, moe_w1, b1p, moe_w2, b2, layer)
        xy = _ffn_residual(xn, wts, _combine_rows(ys, pos), mod, sandwich_norms, layer, mod_row)

    return xy.reshape(n_batch, seq, d)
```

```python
import functools
import math

import jax
import jax.numpy as jnp
from jax import lax
from jax.experimental import pallas as pl
from jax.experimental.pallas import tpu as pltpu
from jax.experimental.pallas import tpu_sc as plsc

F32 = jnp.float32
BF16 = jnp.bfloat16
HI = lax.Precision.HIGHEST

HEADS = 4
A_W = 256
B_QK = 32
B_W = 256
C_DQK = 64
C_DV = 2 * C_DQK
ATTN_KEY_SCALE = C_DQK ** -0.5 * math.log2(math.e)
C_W = 512
HEAD_V = 64
N_GATE = 16
GRID_W = 64
TOP_K = 4
SWIGLU_ALPHA = 1.702
SWIGLU_LIMIT = 7.0
ROPE_THETA = 10000.0
NORM_EPS = 1e-6
MASK_NEG = -1e30
F_MIN = 1e-12

LANES = 128
ROW_TILE = 256
CHUNK = 64
SCAN_BATCHES = 2
ROW_GROUPS = 2
MOE_TILE = 512
ATTN_HEADS_PER_STEP = 2
VMEM_LIMIT = 56 * 1024 * 1024

PA_W = 5 * A_W
PB_W = 2 * HEADS * B_QK + 2 * B_W + LANES
PC_IN = 3 * C_W
PC_OUT = 4 * C_W
W_IN_PAD = PA_W + PB_W + PC_IN


def _cparams(sem):
    return pltpu.CompilerParams(dimension_semantics=sem, vmem_limit_bytes=VMEM_LIMIT)


def _nt(a, b):
    return lax.dot_general(a, b, (((1,), (1,)), ((), ())), preferred_element_type=F32)


def _tn(a, b, precision=None):
    return lax.dot_general(a, b, (((0,), (0,)), ((), ())), preferred_element_type=F32, precision=precision)


def _rms(x):
    return x * lax.rsqrt(jnp.mean(x * x, axis=-1, keepdims=True) + NORM_EPS)


def _silu(x):
    return x * jax.nn.sigmoid(x)


def _pack_bf16_pairs(x):
    half = x.shape[1] // 2
    bits = pltpu.bitcast(x.astype(BF16).astype(F32), jnp.uint32)
    return pltpu.bitcast((bits[:, :half] >> 16) | (bits[:, half:] & jnp.uint32(0xFFFF0000)), jnp.int32)


def _unpack_bf16_pairs(words):
    bits = pltpu.bitcast(words, jnp.uint32)
    return jnp.concatenate([pltpu.bitcast(bits << 16, F32),
                            pltpu.bitcast(bits & jnp.uint32(0xFFFF0000), F32)], axis=1)


def _adaln_kernel(c_ref, w_ref, b_ref, o_ref):
    cond = _silu(c_ref[...])
    o_ref[...] = jnp.dot(cond, w_ref[...], preferred_element_type=F32, precision=HI) + b_ref[...]


def _adaln(cvec, ada_w, ada_b, layer):
    rows, d = cvec.shape
    return pl.pallas_call(
        _adaln_kernel,
        out_shape=jax.ShapeDtypeStruct((rows, 6 * d), F32),
        grid=(6,),
        in_specs=[pl.BlockSpec((rows, d), lambda j: (0, 0)),
                  pl.BlockSpec((None, d, d), lambda j: (layer, 0, j)),
                  pl.BlockSpec((None, 1, d), lambda j: (layer, 0, j))],
        out_specs=pl.BlockSpec((rows, d), lambda j: (0, j)),
        compiler_params=_cparams(("arbitrary",)),
        name="adaln",
    )(cvec, ada_w, ada_b)


def _inproj_kernel(x_ref, mod_ref, g_ref, w_ref, cos_ref, sin_ref, pa_ref, pb_ref, pc_ref):
    d = x_ref.shape[1]
    shift = mod_ref[0, :, 0:d]
    scale = mod_ref[0, :, d:2 * d]
    h = _rms(x_ref[...]) * g_ref[0:1, :] * (1.0 + scale) + shift
    hb = h.astype(BF16)
    pa_ref[...] = jnp.dot(hb, w_ref[:, 0:PA_W], preferred_element_type=F32)
    pb_ref[...] = jnp.dot(hb, w_ref[:, PA_W:PA_W + PB_W], preferred_element_type=F32)
    pc = jnp.dot(hb, w_ref[:, PA_W + PB_W:W_IN_PAD], preferred_element_type=F32)
    q = pc[:, 0:C_W]
    k = pc[:, C_W:2 * C_W] * ATTN_KEY_SCALE
    cos = cos_ref[...]
    sin = sin_ref[...]
    lane = lax.broadcasted_iota(jnp.int32, q.shape, 1)
    first = (lane % 32) < 16

    def rope(t):
        partner = jnp.where(first, pltpu.roll(t, C_W - 16, 1), pltpu.roll(t, 16, 1))
        return t * cos + partner * sin

    pc_ref[:, 0:C_W] = q.astype(BF16)
    pc_ref[:, C_W:2 * C_W] = rope(q).astype(BF16)
    pc_ref[:, 2 * C_W:3 * C_W] = rope(k).astype(BF16)
    pc_ref[:, 3 * C_W:4 * C_W] = pc[:, 2 * C_W:3 * C_W].astype(BF16)


def _inproj(xy, mod, norms, w_in_p, cos, sin, layer, tiles_per_batch, n_ctx_tiles, n_batch):
    t_all, d = xy.shape
    n_tiles = t_all // ROW_TILE

    def mod_row(i):
        return jnp.where(i % tiles_per_batch < n_ctx_tiles, n_batch, i // tiles_per_batch)

    return pl.pallas_call(
        _inproj_kernel,
        out_shape=(jax.ShapeDtypeStruct((t_all, PA_W), F32),
                   jax.ShapeDtypeStruct((t_all, PB_W), F32),
                   jax.ShapeDtypeStruct((t_all, PC_OUT), BF16)),
        grid=(n_tiles,),
        in_specs=[pl.BlockSpec((ROW_TILE, d), lambda i: (i, 0)),
                  pl.BlockSpec((1, 1, 6 * d), lambda i: (mod_row(i), 0, 0)),
                  pl.BlockSpec((None, 4, d), lambda i: (layer, 0, 0)),
                  pl.BlockSpec((None, d, W_IN_PAD), lambda i: (layer, 0, 0)),
                  pl.BlockSpec((ROW_TILE, C_W), lambda i: (i % tiles_per_batch, 0)),
                  pl.BlockSpec((ROW_TILE, C_W), lambda i: (i % tiles_per_batch, 0))],
        out_specs=(pl.BlockSpec((ROW_TILE, PA_W), lambda i: (i, 0)),
                   pl.BlockSpec((ROW_TILE, PB_W), lambda i: (i, 0)),
                   pl.BlockSpec((ROW_TILE, PC_OUT), lambda i: (i, 0))),
        compiler_params=_cparams(("arbitrary",)),
        name="inproj",
    )(xy, mod, norms, w_in_p, cos, sin)


def _hgrn_kernel(pa_ref, lb_ref, gain_ref, o_ref, st_ref, ob_ref, *, n_ctx, n_all, n_b):
    c_sz, w = CHUNK, A_W
    same_head = (lax.broadcasted_iota(jnp.int32, (w, w), 0) // HEAD_V
                 == lax.broadcasted_iota(jnp.int32, (w, w), 1) // HEAD_V)
    head_ones = same_head.astype(BF16)
    assert c_sz == HEAD_V
    t_i = lax.broadcasted_iota(jnp.int32, (c_sz, w), 0)
    s_i = lax.broadcasted_iota(jnp.int32, (c_sz, w), 1) % c_sz
    t_row = lax.broadcasted_iota(jnp.int32, (c_sz, 1), 0)
    row8 = t_row % 8

    def grouped_rows(a, k):
        return jnp.concatenate(
            [jnp.broadcast_to(a[8 * j + k:8 * j + k + 1, :], (8, w)) for j in range(c_sz // 8)], axis=0)

    def halving_levels(rev):
        out = []
        b = c_sz // 2
        while b >= 1:
            def later(i):
                return ((i % (2 * b)) < b) if rev else ((i % (2 * b)) >= b)
            live = jnp.logical_and(t_i // (2 * b) == s_i // (2 * b),
                                   jnp.logical_and(later(t_i), jnp.logical_not(later(s_i))))
            out.append((b, live.astype(F32), jnp.where(later(t_row), 1.0, -1.0)))
            b //= 2
        return out

    n_seg = 2 * n_b
    seg_rev = [seg % 2 == 1 for seg in range(n_seg)]

    def seg_rows(seg):
        return slice(seg * c_sz, (seg + 1) * c_sz)

    def stack(parts):
        return jnp.concatenate(parts, axis=0)

    def rows_of(rows):
        return stack([jnp.broadcast_to(row, (c_sz, w)) for row in rows])

    self_mask = stack([(s_i == t_i).astype(F32)] * n_seg)
    levels = [(b, stack([live_b if rev else live_f for rev in seg_rev]),
               stack([sign_b if rev else sign_f for rev in seg_rev]))
              for (b, live_f, sign_f), (_, live_b, sign_b) in zip(halving_levels(False), halving_levels(True))]
    lb_all = rows_of([lb_ref[1:2, :] if rev else lb_ref[0:1, :] for rev in seg_rev])
    r_all = lax.broadcasted_iota(jnp.int32, (n_seg * c_sz, n_seg * c_sz), 0)
    c_all = lax.broadcasted_iota(jnp.int32, (n_seg * c_sz, n_seg * c_sz), 1)
    rev_row = (r_all // c_sz) % 2 == 1
    tri = jnp.logical_and(r_all // c_sz == c_all // c_sz, jnp.logical_or(
        jnp.logical_and(jnp.logical_not(rev_row), c_all <= r_all),
        jnp.logical_and(rev_row, c_all >= r_all))).astype(BF16)
    zero = jnp.zeros((), BF16)

    def boundary_rows(cum, b, rev):
        if b >= 8:
            return jnp.concatenate(
                [jnp.broadcast_to(cum[r0 + (b if rev else b - 1):r0 + (b if rev else b - 1) + 1, :], (2 * b, w))
                 for r0 in range(0, c_sz, 2 * b)], axis=0)
        ref = None
        for g in reversed(range(8 // (2 * b))):
            cand = grouped_rows(cum, g * 2 * b + (b if rev else b - 1))
            ref = cand if ref is None else jnp.where(row8 < (g + 1) * 2 * b, cand, ref)
        return ref

    def scan_step(n, carry):
        c_f = n
        c_b = jnp.where(n < n_ctx, n_ctx - 1 - n, n_all - 1 - (n - n_ctx))
        chunk_rows = [pl.ds(pl.multiple_of((c_b if rev else c_f) * c_sz, c_sz), c_sz) for rev in seg_rev]
        q_pre = stack([pa_ref[seg // 2, chunk_rows[seg], 0:A_W] for seg in range(n_seg)])
        v_l = [pa_ref[seg // 2, chunk_rows[seg], A_W:2 * A_W].astype(BF16) for seg in range(n_seg)]
        f_pre = stack([pa_ref[seg // 2, chunk_rows[seg], (3 if rev else 2) * A_W:(4 if rev else 3) * A_W]
                       for seg, rev in enumerate(seg_rev)])
        q = _silu(q_pre)
        f = lb_all + (1.0 - lb_all) * jax.nn.sigmoid(f_pre)
        log_f = jnp.log(jnp.maximum(f, F_MIN))
        kk = (1.0 - lb_all) * jax.nn.sigmoid(-f_pre)
        cum = sum(jnp.dot(tri, piece, preferred_element_type=F32) for piece in _split3(log_f))
        ends = [cum[seg * c_sz + (0 if rev else c_sz - 1):seg * c_sz + (0 if rev else c_sz - 1) + 1, :]
                for seg, rev in enumerate(seg_rev)]

        st_l = [st_ref[seg] for seg in range(n_seg)]
        q_in = (q * jnp.exp(cum)).astype(BF16)
        o = stack([_nt(q_in[seg_rows(seg)], st_l[seg].astype(BF16)) for seg in range(n_seg)])
        k_end = (kk * jnp.exp(rows_of(ends) - cum)).astype(BF16)
        for seg in range(n_seg):
            st_ref[seg] = (st_l[seg] * jnp.exp(ends[seg])
                           + jnp.where(same_head, _tn(v_l[seg], k_end[seg_rows(seg)]), 0.0))

        def scores(qa, ka):
            return _nt(qa, jnp.where(same_head, jnp.concatenate([ka] * HEADS, axis=0), zero))

        refs = [stack([boundary_rows(cum[seg_rows(seg)], b, rev) for seg, rev in enumerate(seg_rev)])
                for b, _, _ in levels]
        zs = [jnp.exp((cum - ref) * sign)
              for ref, (_, _, sign) in zip(refs, levels)]
        qas = [(q * z).astype(BF16) for z in zs]
        kas = [(kk * z).astype(BF16) for z in zs]
        prods = [stack([scores(qa[seg_rows(seg)], ka[seg_rows(seg)]) for seg in range(n_seg)])
                 for qa, ka in zip(qas, kas)]
        p = self_mask * jnp.dot((q * kk).astype(BF16), head_ones, preferred_element_type=F32)
        for prod, (_, live, _) in zip(prods, levels):
            p = p + live * prod
        pb = p.astype(BF16)

        for seg, rev in enumerate(seg_rev):
            out = o[seg_rows(seg)] + jnp.dot(
                pb[seg_rows(seg)], jnp.where(same_head, jnp.concatenate([v_l[seg]] * HEADS, axis=0), zero),
                preferred_element_type=F32)
            if rev:
                ob_ref[seg // 2, chunk_rows[seg], :] = out
            else:
                o_ref[seg // 2, chunk_rows[seg], :] = out
        return carry

    st_ref[...] = jnp.zeros_like(st_ref)
    lax.fori_loop(0, n_all, scan_step, 0)

    def readout(c, carry):
        rows = pl.ds(pl.multiple_of(c * ROW_TILE, ROW_TILE), ROW_TILE)
        for b in range(n_b):
            tot = o_ref[b, rows, :] + ob_ref[b, rows, :]
            ms = sum(jnp.dot(piece, head_ones, preferred_element_type=F32)
                     for piece in _split3(tot * tot)) * (1.0 / HEAD_V)
            g = pa_ref[b, rows, 4 * A_W:5 * A_W]
            o_ref[b, rows, :] = tot * lax.rsqrt(ms + NORM_EPS) * gain_ref[...] * _silu(g)
        return carry

    lax.fori_loop(0, n_all * c_sz // ROW_TILE, readout, 0)


def _hgrn(pa3, lb, gain, n_ctx_rows):
    n_batch, rows, _ = pa3.shape
    n_b = SCAN_BATCHES if n_batch % SCAN_BATCHES == 0 else 1
    kern = functools.partial(_hgrn_kernel, n_ctx=n_ctx_rows // CHUNK, n_all=rows // CHUNK, n_b=n_b)
    return pl.pallas_call(
        kern,
        out_shape=jax.ShapeDtypeStruct((n_batch, rows, A_W), F32),
        grid=(n_batch // n_b,),
        in_specs=[pl.BlockSpec((n_b, rows, PA_W), lambda b: (b, 0, 0), pipeline_mode=pl.Buffered(1)),
                  pl.BlockSpec((2, A_W), lambda b: (0, 0)),
                  pl.BlockSpec((1, A_W), lambda b: (0, 0))],
        out_specs=pl.BlockSpec((n_b, rows, A_W), lambda b: (b, 0, 0)),
        scratch_shapes=[pltpu.VMEM((2 * n_b, A_W, A_W), F32), pltpu.VMEM((n_b, rows, A_W), F32)],
        compiler_params=_cparams(("arbitrary",)),
        name="hgrn2",
    )(pa3, lb, gain)


def _split3(x):
    hi = x.astype(BF16)
    rest = x - hi.astype(F32)
    mid = rest.astype(BF16)
    return hi, mid, (rest - mid.astype(F32)).astype(BF16)


def _mlstm_kernel(pb_ref, bias_ref, gain_ref, o_ref, cn_ref, ob_ref, *, n_ctx, n_all, n_b):
    assert CHUNK == HEAD_V
    c_sz, w, qk_w = CHUNK, B_W, HEADS * B_QK
    t_i = lax.broadcasted_iota(jnp.int32, (c_sz, w), 0)
    s_i = lax.broadcasted_iota(jnp.int32, (c_sz, w), 1) % c_sz
    diag4 = s_i == t_i
    half_lane = lax.broadcasted_iota(jnp.int32, (1, LANES), 1) < HEAD_V
    k_block = (lax.broadcasted_iota(jnp.int32, (HEADS * c_sz, qk_w), 0) // c_sz
               == lax.broadcasted_iota(jnp.int32, (HEADS * c_sz, qk_w), 1) // B_QK)
    v_block = (lax.broadcasted_iota(jnp.int32, (HEADS * c_sz, w), 0) // c_sz
               == lax.broadcasted_iota(jnp.int32, (HEADS * c_sz, w), 1) // HEAD_V)
    state_block = (lax.broadcasted_iota(jnp.int32, (qk_w, 2 * w), 0) // B_QK
                   == (lax.broadcasted_iota(jnp.int32, (qk_w, 2 * w), 1) % w) // HEAD_V)
    head_ones = v_block.astype(BF16)
    q_off, k_off, v_off, o_off, g_off = 0, qk_w, 2 * qk_w, 2 * qk_w + B_W, 2 * qk_w + 2 * B_W

    sel_r = lax.broadcasted_iota(jnp.int32, (2 * LANES, 2 * w), 0)
    sel_c = lax.broadcasted_iota(jnp.int32, (2 * LANES, 2 * w), 1)
    sel = (sel_r == jnp.where(sel_c < w, HEADS + sel_c // HEAD_V, LANES + (sel_c - w) // HEAD_V)).astype(BF16)

    def head_max(x):
        outs = []
        for col in range(w // LANES):
            xc = x[:, col * LANES:(col + 1) * LANES]
            lo = jnp.max(jnp.where(half_lane, xc, -jnp.inf), axis=-1, keepdims=True)
            hi = jnp.max(jnp.where(half_lane, -jnp.inf, xc), axis=-1, keepdims=True)
            outs.append(jnp.where(half_lane, lo, hi))
        return jnp.concatenate(outs, axis=1)

    n_seg = 2 * n_b
    seg_rev = [seg % 2 == 1 for seg in range(n_seg)]

    def seg_rows(seg):
        return slice(seg * c_sz, (seg + 1) * c_sz)

    mask = jnp.concatenate([(s_i >= t_i) if rev else (s_i <= t_i) for rev in seg_rev], axis=0)
    r_all = lax.broadcasted_iota(jnp.int32, (n_seg * c_sz, n_seg * c_sz), 0)
    c_all = lax.broadcasted_iota(jnp.int32, (n_seg * c_sz, n_seg * c_sz), 1)
    same_seg = r_all // c_sz == c_all // c_sz
    rev_row = (r_all // c_sz) % 2 == 1
    tri = jnp.logical_and(same_seg, jnp.logical_or(
        jnp.logical_and(jnp.logical_not(rev_row), c_all <= r_all),
        jnp.logical_and(rev_row, c_all >= r_all))).astype(BF16)

    def per_seg(fn, x):
        return jnp.concatenate([jnp.broadcast_to(fn(x[seg_rows(seg)]), (c_sz, w)) for seg in range(n_seg)],
                               axis=0)

    def rows_of(rows):
        return jnp.concatenate([jnp.broadcast_to(row, (c_sz, w)) for row in rows], axis=0)

    def scan_step(n, m_prev_rows):
        c_f = n
        c_b = jnp.where(n < n_ctx, n_ctx - 1 - n, n_all - 1 - (n - n_ctx))
        chunk_rows = [pl.ds(pl.multiple_of((c_b if rev else c_f) * c_sz, c_sz), c_sz) for rev in seg_rev]

        def load(lo, hi):
            return [pb_ref[seg // 2, chunk_rows[seg], lo:hi] for seg in range(n_seg)]

        q_l = load(q_off, q_off + qk_w)
        k_l = load(k_off, k_off + qk_w)
        v_l = load(v_off, v_off + B_W)
        g_l = load(g_off, g_off + LANES)
        gts = jnp.concatenate(
            [pltpu.roll(g + bias_ref[...], LANES - 2 * HEADS, 1) if rev else g + bias_ref[...]
             for g, rev in zip(g_l, seg_rev)], axis=0)
        log_f = jnp.minimum(gts, 0.0) - jnp.log(1.0 + jnp.exp(-jnp.abs(gts)))
        cum_col = sum(jnp.dot(tri, p, preferred_element_type=F32) for p in _split3(log_f))
        both = sum(jnp.dot(p, sel, preferred_element_type=F32)
                   for p in _split3(jnp.concatenate([cum_col, gts], axis=1)))
        cum_t = both[:, :w]
        ig_t = both[:, w:]
        src = per_seg(lambda x: jnp.sum(jnp.where(diag4, x, 0.0), axis=0, keepdims=True), cum_t - ig_t)
        m_prev = rows_of(m_prev_rows)
        log_d = jnp.where(mask, cum_t - src, MASK_NEG)
        log_inter = cum_t + m_prev
        m_t = jnp.maximum(log_inter, head_max(log_d))
        w_intra = jnp.where(mask, jnp.exp(log_d - m_t), 0.0)
        w_inter = jnp.exp(log_inter - m_t)

        scale = B_QK ** -0.5
        qb_l = [(q * scale).astype(BF16) for q in q_l]
        kb_l = [k.astype(BF16) for k in k_l]
        zero = jnp.zeros((), BF16)
        scores = [_nt(qb, jnp.where(k_block, jnp.concatenate([kb] * HEADS, axis=0), zero))
                  for qb, kb in zip(qb_l, kb_l)]
        p = (jnp.concatenate(scores, axis=0) * w_intra).astype(BF16)

        cn_l = [cn_ref[seg] for seg in range(n_seg)]
        inter = jnp.concatenate([jnp.dot(qb, cn.astype(BF16), preferred_element_type=F32)
                                 for qb, cn in zip(qb_l, cn_l)], axis=0)
        values = [jnp.dot(p[seg_rows(seg)],
                          jnp.where(v_block, jnp.concatenate([v_l[seg].astype(BF16)] * HEADS, axis=0), zero),
                          preferred_element_type=F32) for seg in range(n_seg)]
        num = w_inter * inter[:, :w] + jnp.concatenate(values, axis=0)
        den = w_inter * inter[:, w:] + jnp.dot(p, head_ones, preferred_element_type=F32)
        h_out = num / jnp.maximum(jnp.abs(den), jnp.exp(-m_t))
        for seg, rev in enumerate(seg_rev):
            if rev:
                ob_ref[seg // 2, chunk_rows[seg], :] = h_out[seg_rows(seg)]
            else:
                o_ref[seg // 2, chunk_rows[seg], :] = h_out[seg_rows(seg)]

        ends = [cum_t[seg * c_sz + (0 if rev else c_sz - 1):seg * c_sz + (0 if rev else c_sz - 1) + 1, :]
                for seg, rev in enumerate(seg_rev)]
        log_end = rows_of(ends) - cum_t + ig_t
        m_ends = [jnp.maximum(end + m_old, jnp.max(log_end[seg_rows(seg)], axis=0, keepdims=True))
                  for seg, (end, m_old) in enumerate(zip(ends, m_prev_rows))]
        w_end = jnp.exp(log_end - rows_of(m_ends))
        for seg in range(n_seg):
            w_seg = w_end[seg_rows(seg)]
            upd = _tn(kb_l[seg], jnp.concatenate([(w_seg * v_l[seg]).astype(BF16), w_seg.astype(BF16)], axis=1))
            carry_w = jnp.exp(ends[seg] + m_prev_rows[seg] - m_ends[seg])
            cn_ref[seg] = (jnp.concatenate([carry_w, carry_w], axis=1) * cn_l[seg]
                           + jnp.where(state_block, upd, 0.0))
        return tuple(m_ends)

    cn_ref[...] = jnp.zeros_like(cn_ref)
    lax.fori_loop(0, n_all, scan_step, tuple(jnp.zeros((1, w), F32) for _ in range(n_seg)))

    def readout(c, carry):
        rows = pl.ds(pl.multiple_of(c * ROW_TILE, ROW_TILE), ROW_TILE)
        for b in range(n_b):
            tot = o_ref[b, rows, :] + ob_ref[b, rows, :]
            normed = jnp.concatenate(
                [_rms(tot[:, h * HEAD_V:(h + 1) * HEAD_V]) for h in range(HEADS)], axis=1) * gain_ref[...]
            og = pb_ref[b, rows, o_off:o_off + B_W]
            o_ref[b, rows, :] = normed * jax.nn.sigmoid(og)
        return carry

    lax.fori_loop(0, n_all * c_sz // ROW_TILE, readout, 0)


def _mlstm(pb3, bias, gain, n_ctx_rows):
    n_batch, rows, _ = pb3.shape
    n_b = SCAN_BATCHES if n_batch % SCAN_BATCHES == 0 else 1
    kern = functools.partial(_mlstm_kernel, n_ctx=n_ctx_rows // CHUNK, n_all=rows // CHUNK, n_b=n_b)
    return pl.pallas_call(
        kern,
        out_shape=jax.ShapeDtypeStruct((n_batch, rows, B_W), F32),
        grid=(n_batch // n_b,),
        in_specs=[pl.BlockSpec((n_b, rows, PB_W), lambda b: (b, 0, 0), pipeline_mode=pl.Buffered(1)),
                  pl.BlockSpec((1, LANES), lambda b: (0, 0)),
                  pl.BlockSpec((1, B_W), lambda b: (0, 0))],
        out_specs=pl.BlockSpec((n_b, rows, B_W), lambda b: (b, 0, 0)),
        scratch_shapes=[pltpu.VMEM((2 * n_b, HEADS * B_QK, 2 * B_W), F32), pltpu.VMEM((n_b, rows, B_W), F32)],
        compiler_params=_cparams(("arbitrary",)),
        name="mlstm",
    )(pb3, bias, gain)


def _attn_kernel(lam_ref, qp_ref, qr_ref, k_ref, v_ref, gain_ref, o_ref, *, n_ctx, q_tile0, lam_init):
    lam = lam_ref[0]
    q_tile = pl.program_id(2) + q_tile0
    lane = lax.broadcasted_iota(jnp.int32, (1, 2 * C_DQK), 1)

    n_all = k_ref.shape[1]

    def finish(head, parts):
        o = parts[0] - lam * parts[1]
        o_ref[0, :, head] = _rms(o) * gain_ref[...] * (1.0 - lam_init)

    def sub_query(q, j):
        return jnp.where(lane // C_DQK == j, q, jnp.zeros_like(q))

    def row_max(s):
        return jnp.max(s, axis=-1, keepdims=True)

    def row_sum(s):
        return jnp.sum(s, axis=-1, keepdims=True)

    def pv(ex, v):
        return jnp.dot(ex.astype(BF16), v, preferred_element_type=F32)

    heads = [slice(hh * C_DV, (hh + 1) * C_DV) for hh in range(ATTN_HEADS_PER_STEP)]

    def attend(pieces):
        problems = [(head, j) for head in heads for j in range(2)]
        scores = [[_nt(sub_query(q_ref[0, :, head], j), k_ref[0, k0:k1, head]) for q_ref, k0, k1 in pieces]
                  for head, j in problems]
        maxes = [functools.reduce(jnp.maximum, [row_max(s) for s in ss]) for ss in scores]
        exps = [[jnp.exp2(s - m) for s in ss] for ss, m in zip(scores, maxes)]
        outs = [sum(pv(e, v_ref[0, k0:k1, head]) for e, (_, k0, k1) in zip(es, pieces))
                / sum(row_sum(e) for e in es) for es, (head, _) in zip(exps, problems)]
        for n, head in enumerate(heads):
            finish(head, outs[2 * n:2 * n + 2])

    @pl.when(q_tile * ROW_TILE < n_ctx)
    def _():
        attend([(qp_ref, 0, n_ctx)])

    @pl.when(q_tile * ROW_TILE >= n_ctx)
    def _():
        attend([(qp_ref, 0, n_ctx), (qr_ref, n_ctx, n_all)])


def _attention(pc3, lam, gain, n_ctx_rows, lam_init, with_ctx):
    n_batch, rows, _ = pc3.shape
    q_tile0 = 0 if with_ctx else n_ctx_rows // ROW_TILE
    n_q = rows // ROW_TILE - q_tile0
    gw = ATTN_HEADS_PER_STEP * C_DV
    hb = C_W // gw
    kern = functools.partial(_attn_kernel, n_ctx=n_ctx_rows, q_tile0=q_tile0, lam_init=lam_init)
    grid_spec = pltpu.PrefetchScalarGridSpec(
        num_scalar_prefetch=1,
        grid=(n_batch, hb, n_q),
        in_specs=[pl.BlockSpec((1, ROW_TILE, gw), lambda b, h, i, lam: (b, i + q_tile0, h)),
                  pl.BlockSpec((1, ROW_TILE, gw), lambda b, h, i, lam: (b, i + q_tile0, hb + h)),
                  pl.BlockSpec((1, rows, gw), lambda b, h, i, lam: (b, 0, 2 * hb + h)),
                  pl.BlockSpec((1, rows, gw), lambda b, h, i, lam: (b, 0, 3 * hb + h)),
                  pl.BlockSpec((1, C_DV), lambda b, h, i, lam: (0, 0))],
        out_specs=pl.BlockSpec((1, ROW_TILE, gw), lambda b, h, i, lam: (b, i, h)),
    )
    return pl.pallas_call(
        kern,
        out_shape=jax.ShapeDtypeStruct((n_batch, n_q * ROW_TILE, C_W), F32),
        grid_spec=grid_spec,
        compiler_params=_cparams(("arbitrary", "arbitrary", "arbitrary")),
        name="diff_attn",
    )(lam, pc3, pc3, pc3, pc3, gain)


def _outproj_kernel(oa_ref, ob_ref, oc_ref, x_ref, mod_ref, n_ref, w_ref, rw_ref, rb_ref,
                    xn_ref, h_ref, route_ref, cnt_ref, carry_ref):
    d = x_ref.shape[1]
    n_exp = rw_ref.shape[1]

    @pl.when(pl.program_id(0) == 0)
    def _():
        carry_ref[...] = jnp.zeros_like(carry_ref)

    n_rows = x_ref.shape[0]
    halves = [slice(g * n_rows // ROW_GROUPS, (g + 1) * n_rows // ROW_GROUPS) for g in range(ROW_GROUPS)]
    mixes = [jnp.dot(oa_ref[hs, :].astype(BF16), w_ref[0:A_W, :], preferred_element_type=F32)
             + jnp.dot(ob_ref[hs, :].astype(BF16), w_ref[A_W:A_W + B_W, :], preferred_element_type=F32)
             + jnp.dot(oc_ref[hs, :].astype(BF16), w_ref[A_W + B_W:A_W + B_W + C_W, :],
                       preferred_element_type=F32) for hs in halves]
    xns = [x_ref[hs, :] + mod_ref[0, :, 2 * d:3 * d] * (_rms(mix) * n_ref[1:2, :])
           for hs, mix in zip(halves, mixes)]
    hiddens = [_rms(xn) * n_ref[2:3, :] * (1.0 + mod_ref[0, :, 4 * d:5 * d]) + mod_ref[0, :, 3 * d:4 * d]
               for xn in xns]
    for hs, xn, h in zip(halves, xns, hiddens):
        xn_ref[hs, :] = xn
        h_ref[hs, :] = _pack_bf16_pairs(h)

    h_his = [h.astype(BF16) for h in hiddens]
    h_mids = [(h - h_hi.astype(F32)).astype(BF16) for h, h_hi in zip(hiddens, h_his)]
    logits = jnp.concatenate([_nt(rw_ref[0], h_hi) + _nt(rw_ref[1], h_hi) + _nt(rw_ref[0], h_mid)
                              for h_hi, h_mid in zip(h_his, h_mids)], axis=1) + rb_ref[...]
    e_sub = lax.broadcasted_iota(jnp.int32, logits.shape, 0)
    cur = logits
    picks, vals = [], []
    for _ in range(TOP_K):
        mx = jnp.max(cur, axis=0, keepdims=True)
        idx = jnp.min(jnp.where(cur == mx, e_sub, n_exp), axis=0, keepdims=True)
        hit = e_sub == idx
        cur = jnp.where(hit, -jnp.inf, cur)
        picks.append((idx, hit.astype(F32)))
        vals.append(mx)
    exps = [jnp.exp(vv - vals[0]) for vv in vals]
    total = exps[0] + exps[1] + exps[2] + exps[3]

    chosen = picks[0][1] + picks[1][1] + picks[2][1] + picks[3][1]
    tm = logits.shape[1]
    before = (lax.broadcasted_iota(jnp.int32, (tm, tm), 0) < lax.broadcasted_iota(jnp.int32, (tm, tm), 1))
    seen = jnp.dot(chosen.astype(BF16), before.astype(BF16), preferred_element_type=F32) + carry_ref[...]
    new_carry = carry_ref[...] + jnp.sum(chosen, axis=1, keepdims=True)
    carry_ref[...] = new_carry
    cnt_ref[...] = new_carry

    o_row = lax.broadcasted_iota(jnp.int32, route_ref.shape, 0)
    route = jnp.zeros(route_ref.shape, F32)
    for kk in range(TOP_K):
        idx, hit = picks[kk]
        rank = jnp.sum(hit * seen, axis=0, keepdims=True)
        route = (route + jnp.where(o_row == kk, idx.astype(F32), 0.0)
                 + jnp.where(o_row == TOP_K + kk, exps[kk] / total, 0.0)
                 + jnp.where(o_row == 2 * TOP_K + kk, rank, 0.0))
    route_ref[...] = route


ROUTE_ROWS = 16


def _outproj(oa, ob, oc, xy, mod, norms, w_out_b, router_wt, router_b, layer, n_tiles, in_tile, oc_tile, mod_row):
    d = xy.shape[1]
    n_exp = router_wt.shape[2]
    t_out = n_tiles * ROW_TILE
    xn, h, route_t, counts = pl.pallas_call(
        _outproj_kernel,
        out_shape=(jax.ShapeDtypeStruct((t_out, d), F32),
                   jax.ShapeDtypeStruct((t_out, d // 2), jnp.int32),
                   jax.ShapeDtypeStruct((n_tiles * ROUTE_ROWS, ROW_TILE), F32),
                   jax.ShapeDtypeStruct((n_exp, 1), F32)),
        grid=(n_tiles,),
        in_specs=[pl.BlockSpec((ROW_TILE, A_W), lambda i: (in_tile(i), 0)),
                  pl.BlockSpec((ROW_TILE, B_W), lambda i: (in_tile(i), 0)),
                  pl.BlockSpec((ROW_TILE, C_W), lambda i: (oc_tile(i), 0)),
                  pl.BlockSpec((ROW_TILE, d), lambda i: (in_tile(i), 0)),
                  pl.BlockSpec((1, 1, 6 * d), lambda i: (mod_row(i), 0, 0)),
                  pl.BlockSpec((None, 4, d), lambda i: (layer, 0, 0)),
                  pl.BlockSpec((None, d, d), lambda i: (layer, 0, 0)),
                  pl.BlockSpec((None, 2, n_exp, d), lambda i: (layer, 0, 0, 0)),
                  pl.BlockSpec((None, n_exp, 1), lambda i: (layer, 0, 0))],
        out_specs=(pl.BlockSpec((ROW_TILE, d), lambda i: (i, 0)),
                   pl.BlockSpec((ROW_TILE, d // 2), lambda i: (i, 0)),
                   pl.BlockSpec((ROUTE_ROWS, ROW_TILE), lambda i: (i, 0)),
                   pl.BlockSpec((n_exp, 1), lambda i: (0, 0))),
        scratch_shapes=[pltpu.VMEM((n_exp, 1), F32)],
        compiler_params=_cparams(("arbitrary",)),
        name="outproj_router",
    )(oa, ob, oc, xy, mod, norms, w_out_b, router_wt, router_b)
    return xn, h, route_t.reshape(n_tiles, ROUTE_ROWS, ROW_TILE), counts[:, 0]


PAIR_BLOCK = 2 * LANES


def _moe_kernel(te_ref, first_ref, valid_ref, nu_ref, slot_ref, next_ref, xs_ref, w1_hbm, b1_ref, w2_hbm, b2_ref,
                ys_ref, w1p_ref, w2b_ref, w1_buf, w2_buf, w_sem, *, layer):
    i = pl.program_id(0)
    two_f = w1_buf.shape[2]
    n_blk = two_f // PAIR_BLOCK

    def fetch(expert, slot):
        return (pltpu.make_async_copy(w1_hbm.at[layer, expert], w1_buf.at[slot], w_sem.at[0, slot]),
                pltpu.make_async_copy(w2_hbm.at[layer, expert], w2_buf.at[slot], w_sem.at[1, slot]))

    @pl.when(i == 0)
    def _():
        for copy in fetch(te_ref[0], 0):
            copy.start()

    @pl.when(jnp.logical_and(i < nu_ref[0], first_ref[i] == 1))
    def _():
        slot = slot_ref[i]
        for copy in fetch(te_ref[i], slot):
            copy.wait()
        r = lax.broadcasted_iota(jnp.int32, (PAIR_BLOCK, PAIR_BLOCK), 0)
        c = lax.broadcasted_iota(jnp.int32, (PAIR_BLOCK, PAIR_BLOCK), 1)
        perm = (r == jnp.where(c < LANES, 2 * c, 2 * (c - LANES) + 1)).astype(BF16)
        for blk in range(n_blk):
            cols = slice(blk * PAIR_BLOCK, (blk + 1) * PAIR_BLOCK)
            w1p_ref[:, cols] = jnp.dot(w1_buf[slot, :, cols].astype(BF16), perm,
                                       preferred_element_type=F32).astype(BF16)
        w2b_ref[...] = w2_buf[slot].astype(BF16)

        @pl.when(next_ref[i] >= 0)
        def _():
            for copy in fetch(next_ref[i], 1 - slot):
                copy.start()

    tile_rows = xs_ref.shape[0]
    used = i < nu_ref[0]
    valid = valid_ref[i]

    def expert_ffn(n_rows):
        x = _unpack_bf16_pairs(xs_ref[0:n_rows, :]).astype(BF16)
        row = lax.broadcasted_iota(jnp.int32, (n_rows, 1), 0)
        x = jnp.where(row < valid, x, jnp.zeros_like(x))
        hid = jnp.dot(x, w1p_ref[...], preferred_element_type=F32) + b1_ref[...]
        acts = []
        for blk in range(n_blk):
            glu = jnp.minimum(hid[:, blk * PAIR_BLOCK:blk * PAIR_BLOCK + LANES], SWIGLU_LIMIT)
            lin = jnp.clip(hid[:, blk * PAIR_BLOCK + LANES:(blk + 1) * PAIR_BLOCK], -SWIGLU_LIMIT, SWIGLU_LIMIT)
            acts.append((glu * jax.nn.sigmoid(SWIGLU_ALPHA * glu) * (lin + 1.0)).astype(BF16))
        y = jnp.dot(jnp.concatenate(acts, axis=1), w2b_ref[...], preferred_element_type=F32) + b2_ref[...]
        ys_ref[0:n_rows, :] = _pack_bf16_pairs(y)
        if n_rows < tile_rows:
            ys_ref[n_rows:tile_rows, :] = jnp.zeros((tile_rows - n_rows, ys_ref.shape[1]), ys_ref.dtype)

    @pl.when(jnp.logical_and(used, valid > tile_rows // 2))
    def _():
        expert_ffn(tile_rows)

    @pl.when(jnp.logical_and(used, valid <= tile_rows // 2))
    def _():
        expert_ffn(tile_rows // 2)

    @pl.when(jnp.logical_not(used))
    def _():
        ys_ref[...] = jnp.zeros_like(ys_ref)


def _moe(plan, xs, w1, b1p, w2, b2, layer):
    r_max, half_d = xs.shape
    d = 2 * half_d
    two_f = w1.shape[3]
    f = two_f // 2
    n_tiles = r_max // MOE_TILE

    def row_tile(i, te, fi, va, nu, sl, nx):
        return (jnp.maximum(jnp.minimum(i, nu[0] - 1), 0), 0)

    def bias_block(i, te, fi, va, nu, sl, nx):
        return (layer, te[i], 0, 0)

    grid_spec = pltpu.PrefetchScalarGridSpec(
        num_scalar_prefetch=6,
        grid=(n_tiles,),
        in_specs=[pl.BlockSpec((MOE_TILE, half_d), row_tile),
                  pl.BlockSpec(memory_space=pl.ANY),
                  pl.BlockSpec((None, None, 1, two_f), bias_block),
                  pl.BlockSpec(memory_space=pl.ANY),
                  pl.BlockSpec((None, None, 1, d), bias_block)],
        out_specs=pl.BlockSpec((MOE_TILE, half_d), lambda i, te, fi, va, nu, sl, nx: (i, 0)),
        scratch_shapes=[pltpu.VMEM((d, two_f), BF16), pltpu.VMEM((f, d), BF16),
                        pltpu.VMEM((2, d, two_f), F32), pltpu.VMEM((2, f, d), F32),
                        pltpu.SemaphoreType.DMA((2, 2))],
    )
    return pl.pallas_call(
        functools.partial(_moe_kernel, layer=layer),
        out_shape=jax.ShapeDtypeStruct((r_max, half_d), jnp.int32),
        grid_spec=grid_spec,
        compiler_params=_cparams(("arbitrary",)),
        name="moe_experts",
    )(*plan, xs, w1, b1p, w2, b2)


SC_CORES = 2
SC_SUBCORES = 16
SC_CHUNK = 64


def _dispatch_rows(hp, pos, r_max):
    t, width = hp.shape
    workers = SC_CORES * SC_SUBCORES
    assert t % (workers * SC_CHUNK) == 0
    per_worker = t // (workers * SC_CHUNK)
    idx = pos.reshape(-1, TOP_K, ROW_TILE // SC_CHUNK, SC_CHUNK).transpose(0, 2, 1, 3).reshape(
        workers, per_worker * TOP_K, SC_CHUNK)
    mesh = plsc.VectorSubcoreMesh(core_axis_name="c", subcore_axis_name="s",
                                  num_cores=SC_CORES, num_subcores=SC_SUBCORES)

    @functools.partial(
        pl.kernel, mesh=mesh,
        out_type=jax.ShapeDtypeStruct((r_max, width), hp.dtype),
        scratch_types=[pltpu.VMEM((per_worker * TOP_K, SC_CHUNK), jnp.int32),
                       pltpu.VMEM((2, SC_CHUNK, width), hp.dtype),
                       pltpu.SemaphoreType.DMA((2,)),
                       pltpu.SemaphoreType.DMA((2,))],
    )
    def scatter(hp_hbm, idx_hbm, out_hbm, idx_v, rows_v, read_sem, write_sem):
        wid = lax.axis_index("s") * SC_CORES + lax.axis_index("c")
        pltpu.sync_copy(idx_hbm.at[wid], idx_v)

        def read(j):
            row0 = pl.multiple_of((wid * per_worker + j) * SC_CHUNK, SC_CHUNK)
            return pltpu.make_async_copy(hp_hbm.at[pl.ds(row0, SC_CHUNK)], rows_v.at[j % 2], read_sem.at[j % 2])

        def write(j, k):
            return pltpu.make_async_copy(rows_v.at[j % 2], out_hbm.at[idx_v.at[j * TOP_K + k]],
                                         write_sem.at[j % 2])

        read(0).start()
        for j in range(per_worker):
            read(j).wait()
            for k in range(TOP_K):
                write(j, k).start()
            if j + 1 < per_worker:
                if j >= 1:
                    for k in range(TOP_K):
                        write(j - 1, k).wait()
                read(j + 1).start()
        for j in range(max(per_worker - 2, 0), per_worker):
            for k in range(TOP_K):
                write(j, k).wait()

    return scatter(hp, idx)


def _combine_rows(ysp, pos):
    width = ysp.shape[1]
    t = pos.shape[0] * pos.shape[2]
    workers = SC_CORES * SC_SUBCORES
    assert t % (workers * SC_CHUNK) == 0
    per_worker = t // (workers * SC_CHUNK)
    units = per_worker * TOP_K
    idx = pos.reshape(-1, TOP_K, ROW_TILE // SC_CHUNK, SC_CHUNK).transpose(0, 2, 1, 3).reshape(
        workers, units, SC_CHUNK)
    mesh = plsc.VectorSubcoreMesh(core_axis_name="c", subcore_axis_name="s",
                                  num_cores=SC_CORES, num_subcores=SC_SUBCORES)

    @functools.partial(
        pl.kernel, mesh=mesh,
        out_type=jax.ShapeDtypeStruct((TOP_K, t, width), ysp.dtype),
        scratch_types=[pltpu.VMEM((units, SC_CHUNK), jnp.int32),
                       pltpu.VMEM((2, SC_CHUNK, width), ysp.dtype),
                       pltpu.SemaphoreType.DMA((2,)),
                       pltpu.SemaphoreType.DMA((2,))],
    )
    def gather(ys_hbm, idx_hbm, out_hbm, idx_v, rows_v, read_sem, write_sem):
        wid = lax.axis_index("s") * SC_CORES + lax.axis_index("c")
        pltpu.sync_copy(idx_hbm.at[wid], idx_v)

        def read(u):
            return pltpu.make_async_copy(ys_hbm.at[idx_v.at[u]], rows_v.at[u % 2], read_sem.at[u % 2])

        def write(u):
            row0 = pl.multiple_of((wid * per_worker + u // TOP_K) * SC_CHUNK, SC_CHUNK)
            return pltpu.make_async_copy(rows_v.at[u % 2], out_hbm.at[u % TOP_K, pl.ds(row0, SC_CHUNK)],
                                         write_sem.at[u % 2])

        read(0).start()
        for u in range(units):
            read(u).wait()
            write(u).start()
            if u + 1 < units:
                if u >= 1:
                    write(u - 1).wait()
                read(u + 1).start()
        for u in range(max(units - 2, 0), units):
            write(u).wait()

    return gather(ysp, idx)


def _ffn_residual_kernel(x_ref, w_ref, y0_ref, y1_ref, y2_ref, y3_ref, mod_ref, n_ref, o_ref):
    d = x_ref.shape[1]
    wts = w_ref[...]
    ffn = None
    for k, y_ref in enumerate((y0_ref, y1_ref, y2_ref, y3_ref)):
        term = wts[:, k:k + 1] * _unpack_bf16_pairs(y_ref[...])
        ffn = term if ffn is None else ffn + term
    o_ref[...] = x_ref[...] + mod_ref[0, :, 5 * d:6 * d] * (_rms(ffn) * n_ref[3:4, :])


def _ffn_residual(xn, wts, ys_by_k, mod, norms, layer, mod_row):
    t, d = xn.shape
    row_spec = pl.BlockSpec((ROW_TILE, d), lambda i: (i, 0))

    def y_spec(k):
        return pl.BlockSpec((None, ROW_TILE, d // 2), lambda i: (k, i, 0))

    return pl.pallas_call(
        _ffn_residual_kernel,
        out_shape=jax.ShapeDtypeStruct((t, d), F32),
        grid=(t // ROW_TILE,),
        in_specs=[row_spec,
                  pl.BlockSpec((ROW_TILE, TOP_K), lambda i: (i, 0)),
                  y_spec(0), y_spec(1), y_spec(2), y_spec(3),
                  pl.BlockSpec((1, 1, 6 * d), lambda i: (mod_row(i), 0, 0)),
                  pl.BlockSpec((None, 4, d), lambda i: (layer, 0, 0))],
        out_specs=row_spec,
        compiler_params=_cparams(("arbitrary",)),
        name="ffn_residual",
    )(xn, wts, ys_by_k, ys_by_k, ys_by_k, ys_by_k, mod, norms)


def _dispatch_plan(route_t, counts, r_max):
    n_row_tiles = route_t.shape[0]
    idx = route_t[:, 0:TOP_K, :].astype(jnp.int32)
    wts = route_t[:, TOP_K:2 * TOP_K, :].transpose(0, 2, 1).reshape(n_row_tiles * ROW_TILE, TOP_K)
    rank = route_t[:, 2 * TOP_K:3 * TOP_K, :].astype(jnp.int32)
    cnt = counts.astype(jnp.int32)
    padded = (cnt + MOE_TILE - 1) // MOE_TILE * MOE_TILE
    ends = jnp.cumsum(padded)
    starts = ends - padded
    pos = rank
    for e in range(cnt.shape[0]):
        pos = pos + jnp.where(idx == e, starts[e], 0)
    n_tiles = r_max // MOE_TILE
    n_used = ends[-1] // MOE_TILE
    tile_ids = jnp.minimum(jnp.arange(n_tiles, dtype=jnp.int32), n_used - 1)
    tile_expert = jnp.sum((ends // MOE_TILE)[None, :] <= tile_ids[:, None], axis=1).astype(jnp.int32)
    tile_first = jnp.concatenate(
        [jnp.ones((1,), jnp.int32), (tile_expert[1:] != tile_expert[:-1]).astype(jnp.int32)])
    of_expert = (tile_expert[:, None] == jnp.arange(cnt.shape[0], dtype=jnp.int32)[None, :]).astype(jnp.int32)
    first_tile = jnp.sum(of_expert * ((ends - padded) // MOE_TILE)[None, :], axis=1)
    tile_valid = jnp.clip(jnp.sum(of_expert * cnt[None, :], axis=1) - (tile_ids - first_tile) * MOE_TILE,
                          0, MOE_TILE).astype(jnp.int32)
    experts = jnp.arange(cnt.shape[0], dtype=jnp.int32)
    has_rows = cnt > 0
    tile_slot = (jnp.sum(of_expert * (jnp.cumsum(has_rows.astype(jnp.int32)) - 1)[None, :], axis=1) % 2).astype(jnp.int32)
    later_with_rows = jnp.logical_and(has_rows[None, :], experts[None, :] > experts[:, None])
    next_expert = jnp.min(jnp.where(later_with_rows, experts[None, :], cnt.shape[0]), axis=1)
    next_expert = jnp.where(next_expert < cnt.shape[0], next_expert, -1)
    tile_next = jnp.sum(of_expert * next_expert[None, :], axis=1).astype(jnp.int32)
    plan = (tile_expert, tile_first, tile_valid, n_used.reshape(1).astype(jnp.int32), tile_slot, tile_next)
    return pos, wts, plan


def _rope_tables(length, n_ctx_rows):
    rows = length // GRID_W
    row = jnp.repeat(jnp.arange(rows, dtype=F32), GRID_W)
    col = jnp.tile(jnp.arange(GRID_W, dtype=F32), rows)
    n_freq = C_DQK // 4
    inv_freq = ROPE_THETA ** (-jnp.arange(n_freq, dtype=F32) / n_freq)
    ang_r = row[:, None] * inv_freq
    ang_c = col[:, None] * inv_freq
    cos = jnp.concatenate([jnp.cos(ang_r), jnp.cos(ang_r), jnp.cos(ang_c), jnp.cos(ang_c)], axis=-1)
    sin = jnp.concatenate([-jnp.sin(ang_r), jnp.sin(ang_r), -jnp.sin(ang_c), jnp.sin(ang_c)], axis=-1)
    reps = C_W // C_DQK
    cos = jnp.concatenate([jnp.ones((n_ctx_rows, C_W), F32), jnp.tile(cos, (1, reps))], axis=0)
    sin = jnp.concatenate([jnp.zeros((n_ctx_rows, C_W), F32), jnp.tile(sin, (1, reps))], axis=0)
    return cos, sin


def _hgrn_lower_bound(table, layer):
    p = jax.nn.softmax(table.astype(F32), axis=1)
    cum = jnp.cumsum(p, axis=1) - p[:, :1]
    return jnp.clip(cum[:, layer], 0.0, 1.0)


def kernel(x, c, ctx, c_ctx, ada_w, ada_b, sandwich_norms, w_in, w_out, hgrn_lower_bounds, hgrn_norm,
           mlstm_gate_bias, mlstm_norm, diff_lambdas, diff_norm, router_w, router_b, moe_w1, moe_b1,
           moe_w2, moe_b2):
    n_batch, seq, d = x.shape
    n_ctx_rows = ctx.shape[1]
    depth = w_in.shape[0]
    n_exp = router_w.shape[2]
    assert seq % ROW_TILE == 0 and n_ctx_rows % ROW_TILE == 0 and seq % GRID_W == 0
    rows_b = n_ctx_rows + seq
    tiles_b = rows_b // ROW_TILE
    ctx_tiles = n_ctx_rows // ROW_TILE
    lat_tiles = seq // ROW_TILE

    b_main = PA_W + 2 * HEADS * B_QK + 2 * B_W
    w_in_p = jnp.concatenate(
        [w_in[:, :, :b_main],
         jnp.pad(w_in[:, :, b_main:b_main + N_GATE], ((0, 0), (0, 0), (0, LANES - N_GATE))),
         w_in[:, :, b_main + N_GATE:]], axis=2).astype(BF16)
    w_out_b = w_out.astype(BF16)
    two_f = moe_b1.shape[2]
    b1p = moe_b1.reshape(depth, n_exp, two_f // PAIR_BLOCK, LANES, 2).transpose(0, 1, 2, 4, 3).reshape(
        depth, n_exp, 1, two_f)
    b2 = moe_b2[:, :, None, :]
    ada_b3 = ada_b[:, None, :]
    router_b3 = router_b[:, :, None]
    rw_t = router_w.transpose(0, 2, 1)
    rw_hi = rw_t.astype(BF16)
    router_wt = jnp.stack([rw_hi, (rw_t - rw_hi.astype(F32)).astype(BF16)], axis=1)
    gate_bias = jnp.pad(mlstm_gate_bias, ((0, 0), (0, LANES - N_GATE)))
    cos, sin = _rope_tables(seq, n_ctx_rows)

    cond_rows = (n_batch + 1 + 7) // 8 * 8
    cvec = jnp.zeros((cond_rows, d), F32).at[:n_batch].set(c).at[n_batch].set(c_ctx)

    def mod_row_all(i):
        return jnp.where(i % tiles_b < ctx_tiles, n_batch, i // tiles_b)

    xy = jnp.concatenate([ctx, x], axis=1).reshape(n_batch * rows_b, d)

    for layer in range(depth):
        last = layer == depth - 1
        mod = _adaln(cvec, ada_w, ada_b3, layer).reshape(cond_rows, 1, 6 * d)
        lb = _hgrn_lower_bound(hgrn_lower_bounds, layer)
        lam_init = 0.8 - 0.6 * math.exp(-0.3 * layer)
        lq1, lk1, lq2, lk2 = diff_lambdas[layer].astype(F32)
        lam = (jnp.exp(jnp.sum(lq1 * lk1)) - jnp.exp(jnp.sum(lq2 * lk2)) + lam_init).reshape(1)

        pa, pb, pc = _inproj(xy, mod, sandwich_norms, w_in_p, cos, sin, layer, tiles_b, ctx_tiles, n_batch)
        oa = _hgrn(pa.reshape(n_batch, rows_b, PA_W), lb, jnp.tile(hgrn_norm[layer], HEADS)[None, :], n_ctx_rows)
        ob = _mlstm(pb.reshape(n_batch, rows_b, PB_W), gate_bias[layer][None, :], mlstm_norm[layer][None, :],
                    n_ctx_rows)
        oc = _attention(pc.reshape(n_batch, rows_b, PC_OUT), lam, diff_norm[layer][None, :], n_ctx_rows,
                        lam_init, not last)

        if last:
            n_tiles = n_batch * lat_tiles
            in_tile = lambda i: (i // lat_tiles) * tiles_b + ctx_tiles + i % lat_tiles
            mod_row = lambda i: i // lat_tiles
        else:
            n_tiles = n_batch * tiles_b
            in_tile = lambda i: i
            mod_row = mod_row_all
        xn, h, route, counts = _outproj(
            oa.reshape(-1, A_W), ob.reshape(-1, B_W), oc.reshape(-1, C_W), xy, mod, sandwich_norms, w_out_b,
            router_wt, router_b3, layer, n_tiles, in_tile, lambda i: i, mod_row)

        r_max = n_tiles * ROW_TILE * TOP_K + n_exp * MOE_TILE
        pos, wts, plan = _dispatch_plan(route, counts, r_max)
        xs = _dispatch_rows(h, pos, r_max)
        ys = _moe(plan, xs, moe_w1, b1p, moe_w2, b2, layer)
        xy = _ffn_residual(xn, wts, _combine_rows(ys, pos), mod, sandwich_norms, layer, mod_row)

    return xy.reshape(n_batch, seq, d)
```

```python
import functools
import math

import jax
import jax.numpy as jnp
from jax import lax
from jax.experimental import pallas as pl
from jax.experimental.pallas import tpu as pltpu
from jax.experimental.pallas import tpu_sc as plsc

F32 = jnp.float32
BF16 = jnp.bfloat16
HI = lax.Precision.HIGHEST

HEADS = 4
A_W = 256
B_QK = 32
B_W = 256
C_DQK = 64
C_DV = 2 * C_DQK
ATTN_KEY_SCALE = C_DQK ** -0.5 * math.log2(math.e)
C_W = 512
HEAD_V = 64
N_GATE = 16
GRID_W = 64
TOP_K = 4
SWIGLU_ALPHA = 1.702
SWIGLU_LIMIT = 7.0
ROPE_THETA = 10000.0
NORM_EPS = 1e-6
MASK_NEG = -1e30
F_MIN = 1e-12

LANES = 128
ROW_TILE = 256
CHUNK = 64
SCAN_BATCHES = 2
ROW_GROUPS = 2
MOE_TILE = 512
ATTN_HEADS_PER_STEP = 2
VMEM_LIMIT = 56 * 1024 * 1024

PA_W = 5 * A_W
PB_W = 2 * HEADS * B_QK + 2 * B_W + LANES
PC_IN = 3 * C_W
PC_OUT = 4 * C_W
W_IN_PAD = PA_W + PB_W + PC_IN


def _cparams(sem):
    return pltpu.CompilerParams(dimension_semantics=sem, vmem_limit_bytes=VMEM_LIMIT)


def _nt(a, b):
    return lax.dot_general(a, b, (((1,), (1,)), ((), ())), preferred_element_type=F32)


def _tn(a, b, precision=None):
    return lax.dot_general(a, b, (((0,), (0,)), ((), ())), preferred_element_type=F32, precision=precision)


def _rms(x):
    return x * lax.rsqrt(jnp.mean(x * x, axis=-1, keepdims=True) + NORM_EPS)


def _silu(x):
    return x * jax.nn.sigmoid(x)


def _pack_bf16_pairs(x):
    half = x.shape[1] // 2
    bits = pltpu.bitcast(x.astype(BF16).astype(F32), jnp.uint32)
    return pltpu.bitcast((bits[:, :half] >> 16) | (bits[:, half:] & jnp.uint32(0xFFFF0000)), jnp.int32)


def _unpack_bf16_pairs(words):
    bits = pltpu.bitcast(words, jnp.uint32)
    return jnp.concatenate([pltpu.bitcast(bits << 16, F32),
                            pltpu.bitcast(bits & jnp.uint32(0xFFFF0000), F32)], axis=1)


def _adaln_kernel(c_ref, w_ref, b_ref, o_ref):
    cond = _silu(c_ref[...])
    o_ref[...] = jnp.dot(cond, w_ref[...], preferred_element_type=F32, precision=HI) + b_ref[...]


def _adaln(cvec, ada_w, ada_b, layer):
    rows, d = cvec.shape
    return pl.pallas_call(
        _adaln_kernel,
        out_shape=jax.ShapeDtypeStruct((rows, 6 * d), F32),
        grid=(6,),
        in_specs=[pl.BlockSpec((rows, d), lambda j: (0, 0)),
                  pl.BlockSpec((None, d, d), lambda j: (layer, 0, j)),
                  pl.BlockSpec((None, 1, d), lambda j: (layer, 0, j))],
        out_specs=pl.BlockSpec((rows, d), lambda j: (0, j)),
        compiler_params=_cparams(("arbitrary",)),
        name="adaln",
    )(cvec, ada_w, ada_b)


def _inproj_kernel(x_ref, mod_ref, g_ref, w_ref, cos_ref, sin_ref, pa_ref, pb_ref, pc_ref):
    d = x_ref.shape[1]
    shift = mod_ref[0, :, 0:d]
    scale = mod_ref[0, :, d:2 * d]
    h = _rms(x_ref[...]) * g_ref[0:1, :] * (1.0 + scale) + shift
    hb = h.astype(BF16)
    pc = jnp.dot(hb, w_ref[:, PA_W + PB_W:W_IN_PAD], preferred_element_type=F32)
    pa_ref[...] = jnp.dot(hb, w_ref[:, 0:PA_W], preferred_element_type=F32)
    pb_ref[...] = jnp.dot(hb, w_ref[:, PA_W:PA_W + PB_W], preferred_element_type=F32)
    q = pc[:, 0:C_W]
    k = pc[:, C_W:2 * C_W] * ATTN_KEY_SCALE
    cos = cos_ref[...]
    sin = sin_ref[...]
    lane = lax.broadcasted_iota(jnp.int32, q.shape, 1)
    first = (lane % 32) < 16

    def rope(t):
        partner = jnp.where(first, pltpu.roll(t, C_W - 16, 1), pltpu.roll(t, 16, 1))
        return t * cos + partner * sin

    pc_ref[:, 0:C_W] = q.astype(BF16)
    pc_ref[:, C_W:2 * C_W] = rope(q).astype(BF16)
    pc_ref[:, 2 * C_W:3 * C_W] = rope(k).astype(BF16)
    pc_ref[:, 3 * C_W:4 * C_W] = pc[:, 2 * C_W:3 * C_W].astype(BF16)


def _inproj(xy, mod, norms, w_in_p, cos, sin, layer, tiles_per_batch, n_ctx_tiles, n_batch):
    t_all, d = xy.shape
    n_tiles = t_all // ROW_TILE

    def mod_row(i):
        return jnp.where(i % tiles_per_batch < n_ctx_tiles, n_batch, i // tiles_per_batch)

    return pl.pallas_call(
        _inproj_kernel,
        out_shape=(jax.ShapeDtypeStruct((t_all, PA_W), F32),
                   jax.ShapeDtypeStruct((t_all, PB_W), F32),
                   jax.ShapeDtypeStruct((t_all, PC_OUT), BF16)),
        grid=(n_tiles,),
        in_specs=[pl.BlockSpec((ROW_TILE, d), lambda i: (i, 0)),
                  pl.BlockSpec((1, 1, 6 * d), lambda i: (mod_row(i), 0, 0)),
                  pl.BlockSpec((None, 4, d), lambda i: (layer, 0, 0)),
                  pl.BlockSpec((None, d, W_IN_PAD), lambda i: (layer, 0, 0)),
                  pl.BlockSpec((ROW_TILE, C_W), lambda i: (i % tiles_per_batch, 0)),
                  pl.BlockSpec((ROW_TILE, C_W), lambda i: (i % tiles_per_batch, 0))],
        out_specs=(pl.BlockSpec((ROW_TILE, PA_W), lambda i: (i, 0)),
                   pl.BlockSpec((ROW_TILE, PB_W), lambda i: (i, 0)),
                   pl.BlockSpec((ROW_TILE, PC_OUT), lambda i: (i, 0))),
        compiler_params=_cparams(("arbitrary",)),
        name="inproj",
    )(xy, mod, norms, w_in_p, cos, sin)


def _hgrn_kernel(pa_ref, lb_ref, gain_ref, o_ref, st_ref, ob_ref, *, n_ctx, n_all, n_b):
    c_sz, w = CHUNK, A_W
    same_head = (lax.broadcasted_iota(jnp.int32, (w, w), 0) // HEAD_V
                 == lax.broadcasted_iota(jnp.int32, (w, w), 1) // HEAD_V)
    head_ones = same_head.astype(BF16)
    assert c_sz == HEAD_V
    t_i = lax.broadcasted_iota(jnp.int32, (c_sz, w), 0)
    s_i = lax.broadcasted_iota(jnp.int32, (c_sz, w), 1) % c_sz
    t_row = lax.broadcasted_iota(jnp.int32, (c_sz, 1), 0)
    row8 = t_row % 8

    def grouped_rows(a, k):
        return jnp.concatenate(
            [jnp.broadcast_to(a[8 * j + k:8 * j + k + 1, :], (8, w)) for j in range(c_sz // 8)], axis=0)

    def halving_levels(rev):
        out = []
        b = c_sz // 2
        while b >= 1:
            def later(i):
                return ((i % (2 * b)) < b) if rev else ((i % (2 * b)) >= b)
            live = jnp.logical_and(t_i // (2 * b) == s_i // (2 * b),
                                   jnp.logical_and(later(t_i), jnp.logical_not(later(s_i))))
            out.append((b, live.astype(F32), jnp.where(later(t_row), 1.0, -1.0)))
            b //= 2
        return out

    n_seg = 2 * n_b
    seg_rev = [seg % 2 == 1 for seg in range(n_seg)]

    def seg_rows(seg):
        return slice(seg * c_sz, (seg + 1) * c_sz)

    def stack(parts):
        return jnp.concatenate(parts, axis=0)

    def rows_of(rows):
        return stack([jnp.broadcast_to(row, (c_sz, w)) for row in rows])

    self_mask = stack([(s_i == t_i).astype(F32)] * n_seg)
    levels = [(b, stack([live_b if rev else live_f for rev in seg_rev]),
               stack([sign_b if rev else sign_f for rev in seg_rev]))
              for (b, live_f, sign_f), (_, live_b, sign_b) in zip(halving_levels(False), halving_levels(True))]
    lb_all = rows_of([lb_ref[1:2, :] if rev else lb_ref[0:1, :] for rev in seg_rev])
    r_all = lax.broadcasted_iota(jnp.int32, (n_seg * c_sz, n_seg * c_sz), 0)
    c_all = lax.broadcasted_iota(jnp.int32, (n_seg * c_sz, n_seg * c_sz), 1)
    rev_row = (r_all // c_sz) % 2 == 1
    tri = jnp.logical_and(r_all // c_sz == c_all // c_sz, jnp.logical_or(
        jnp.logical_and(jnp.logical_not(rev_row), c_all <= r_all),
        jnp.logical_and(rev_row, c_all >= r_all))).astype(BF16)
    zero = jnp.zeros((), BF16)

    def boundary_rows(cum, b, rev):
        if b >= 8:
            return jnp.concatenate(
                [jnp.broadcast_to(cum[r0 + (b if rev else b - 1):r0 + (b if rev else b - 1) + 1, :], (2 * b, w))
                 for r0 in range(0, c_sz, 2 * b)], axis=0)
        ref = None
        for g in reversed(range(8 // (2 * b))):
            cand = grouped_rows(cum, g * 2 * b + (b if rev else b - 1))
            ref = cand if ref is None else jnp.where(row8 < (g + 1) * 2 * b, cand, ref)
        return ref

    def scan_step(n, carry):
        c_f = n
        c_b = jnp.where(n < n_ctx, n_ctx - 1 - n, n_all - 1 - (n - n_ctx))
        chunk_rows = [pl.ds(pl.multiple_of((c_b if rev else c_f) * c_sz, c_sz), c_sz) for rev in seg_rev]
        q_pre = stack([pa_ref[seg // 2, chunk_rows[seg], 0:A_W] for seg in range(n_seg)])
        v_l = [pa_ref[seg // 2, chunk_rows[seg], A_W:2 * A_W].astype(BF16) for seg in range(n_seg)]
        f_pre = stack([pa_ref[seg // 2, chunk_rows[seg], (3 if rev else 2) * A_W:(4 if rev else 3) * A_W]
                       for seg, rev in enumerate(seg_rev)])
        q = _silu(q_pre)
        f = lb_all + (1.0 - lb_all) * jax.nn.sigmoid(f_pre)
        log_f = jnp.log(jnp.maximum(f, F_MIN))
        kk = (1.0 - lb_all) * jax.nn.sigmoid(-f_pre)
        cum = sum(jnp.dot(tri, piece, preferred_element_type=F32) for piece in _split3(log_f))
        ends = [cum[seg * c_sz + (0 if rev else c_sz - 1):seg * c_sz + (0 if rev else c_sz - 1) + 1, :]
                for seg, rev in enumerate(seg_rev)]

        st_l = [st_ref[seg] for seg in range(n_seg)]
        q_in = (q * jnp.exp(cum)).astype(BF16)
        o = stack([_nt(q_in[seg_rows(seg)], st_l[seg].astype(BF16)) for seg in range(n_seg)])
        k_end = (kk * jnp.exp(rows_of(ends) - cum)).astype(BF16)
        for seg in range(n_seg):
            st_ref[seg] = (st_l[seg] * jnp.exp(ends[seg])
                           + jnp.where(same_head, _tn(v_l[seg], k_end[seg_rows(seg)]), 0.0))

        def scores(qa, ka):
            return _nt(qa, jnp.where(same_head, jnp.concatenate([ka] * HEADS, axis=0), zero))

        refs = [stack([boundary_rows(cum[seg_rows(seg)], b, rev) for seg, rev in enumerate(seg_rev)])
                for b, _, _ in levels]
        zs = [jnp.exp((cum - ref) * sign)
              for ref, (_, _, sign) in zip(refs, levels)]
        qas = [(q * z).astype(BF16) for z in zs]
        kas = [(kk * z).astype(BF16) for z in zs]
        prods = [stack([scores(qa[seg_rows(seg)], ka[seg_rows(seg)]) for seg in range(n_seg)])
                 for qa, ka in zip(qas, kas)]
        p = self_mask * jnp.dot((q * kk).astype(BF16), head_ones, preferred_element_type=F32)
        for prod, (_, live, _) in zip(prods, levels):
            p = p + live * prod
        pb = p.astype(BF16)

        for seg, rev in enumerate(seg_rev):
            out = o[seg_rows(seg)] + jnp.dot(
                pb[seg_rows(seg)], jnp.where(same_head, jnp.concatenate([v_l[seg]] * HEADS, axis=0), zero),
                preferred_element_type=F32)
            if rev:
                ob_ref[seg // 2, chunk_rows[seg], :] = out
            else:
                o_ref[seg // 2, chunk_rows[seg], :] = out
        return carry

    st_ref[...] = jnp.zeros_like(st_ref)
    lax.fori_loop(0, n_all, scan_step, 0)

    def readout(c, carry):
        rows = pl.ds(pl.multiple_of(c * ROW_TILE, ROW_TILE), ROW_TILE)
        for b in range(n_b):
            tot = o_ref[b, rows, :] + ob_ref[b, rows, :]
            ms = sum(jnp.dot(piece, head_ones, preferred_element_type=F32)
                     for piece in _split3(tot * tot)) * (1.0 / HEAD_V)
            g = pa_ref[b, rows, 4 * A_W:5 * A_W]
            o_ref[b, rows, :] = tot * lax.rsqrt(ms + NORM_EPS) * gain_ref[...] * _silu(g)
        return carry

    lax.fori_loop(0, n_all * c_sz // ROW_TILE, readout, 0)


def _hgrn(pa3, lb, gain, n_ctx_rows):
    n_batch, rows, _ = pa3.shape
    n_b = SCAN_BATCHES if n_batch % SCAN_BATCHES == 0 else 1
    kern = functools.partial(_hgrn_kernel, n_ctx=n_ctx_rows // CHUNK, n_all=rows // CHUNK, n_b=n_b)
    return pl.pallas_call(
        kern,
        out_shape=jax.ShapeDtypeStruct((n_batch, rows, A_W), F32),
        grid=(n_batch // n_b,),
        in_specs=[pl.BlockSpec((n_b, rows, PA_W), lambda b: (b, 0, 0), pipeline_mode=pl.Buffered(1)),
                  pl.BlockSpec((2, A_W), lambda b: (0, 0)),
                  pl.BlockSpec((1, A_W), lambda b: (0, 0))],
        out_specs=pl.BlockSpec((n_b, rows, A_W), lambda b: (b, 0, 0)),
        scratch_shapes=[pltpu.VMEM((2 * n_b, A_W, A_W), F32), pltpu.VMEM((n_b, rows, A_W), F32)],
        compiler_params=_cparams(("arbitrary",)),
        name="hgrn2",
    )(pa3, lb, gain)


def _split3(x):
    hi = x.astype(BF16)
    rest = x - hi.astype(F32)
    mid = rest.astype(BF16)
    return hi, mid, (rest - mid.astype(F32)).astype(BF16)


def _mlstm_kernel(pb_ref, bias_ref, gain_ref, o_ref, cn_ref, ob_ref, *, n_ctx, n_all, n_b):
    assert CHUNK == HEAD_V
    c_sz, w, qk_w = CHUNK, B_W, HEADS * B_QK
    t_i = lax.broadcasted_iota(jnp.int32, (c_sz, w), 0)
    s_i = lax.broadcasted_iota(jnp.int32, (c_sz, w), 1) % c_sz
    diag4 = s_i == t_i
    half_lane = lax.broadcasted_iota(jnp.int32, (1, LANES), 1) < HEAD_V
    k_block = (lax.broadcasted_iota(jnp.int32, (HEADS * c_sz, qk_w), 0) // c_sz
               == lax.broadcasted_iota(jnp.int32, (HEADS * c_sz, qk_w), 1) // B_QK)
    v_block = (lax.broadcasted_iota(jnp.int32, (HEADS * c_sz, w), 0) // c_sz
               == lax.broadcasted_iota(jnp.int32, (HEADS * c_sz, w), 1) // HEAD_V)
    state_block = (lax.broadcasted_iota(jnp.int32, (qk_w, 2 * w), 0) // B_QK
                   == (lax.broadcasted_iota(jnp.int32, (qk_w, 2 * w), 1) % w) // HEAD_V)
    head_ones = v_block.astype(BF16)
    q_off, k_off, v_off, o_off, g_off = 0, qk_w, 2 * qk_w, 2 * qk_w + B_W, 2 * qk_w + 2 * B_W

    sel_r = lax.broadcasted_iota(jnp.int32, (2 * LANES, 2 * w), 0)
    sel_c = lax.broadcasted_iota(jnp.int32, (2 * LANES, 2 * w), 1)
    sel = (sel_r == jnp.where(sel_c < w, HEADS + sel_c // HEAD_V, LANES + (sel_c - w) // HEAD_V)).astype(BF16)

    def head_max(x):
        outs = []
        for col in range(w // LANES):
            xc = x[:, col * LANES:(col + 1) * LANES]
            lo = jnp.max(jnp.where(half_lane, xc, -jnp.inf), axis=-1, keepdims=True)
            hi = jnp.max(jnp.where(half_lane, -jnp.inf, xc), axis=-1, keepdims=True)
            outs.append(jnp.where(half_lane, lo, hi))
        return jnp.concatenate(outs, axis=1)

    n_seg = 2 * n_b
    seg_rev = [seg % 2 == 1 for seg in range(n_seg)]

    def seg_rows(seg):
        return slice(seg * c_sz, (seg + 1) * c_sz)

    mask = jnp.concatenate([(s_i >= t_i) if rev else (s_i <= t_i) for rev in seg_rev], axis=0)
    r_all = lax.broadcasted_iota(jnp.int32, (n_seg * c_sz, n_seg * c_sz), 0)
    c_all = lax.broadcasted_iota(jnp.int32, (n_seg * c_sz, n_seg * c_sz), 1)
    same_seg = r_all // c_sz == c_all // c_sz
    rev_row = (r_all // c_sz) % 2 == 1
    tri = jnp.logical_and(same_seg, jnp.logical_or(
        jnp.logical_and(jnp.logical_not(rev_row), c_all <= r_all),
        jnp.logical_and(rev_row, c_all >= r_all))).astype(BF16)

    def per_seg(fn, x):
        return jnp.concatenate([jnp.broadcast_to(fn(x[seg_rows(seg)]), (c_sz, w)) for seg in range(n_seg)],
                               axis=0)

    def rows_of(rows):
        return jnp.concatenate([jnp.broadcast_to(row, (c_sz, w)) for row in rows], axis=0)

    def scan_step(n, m_prev_rows):
        c_f = n
        c_b = jnp.where(n < n_ctx, n_ctx - 1 - n, n_all - 1 - (n - n_ctx))
        chunk_rows = [pl.ds(pl.multiple_of((c_b if rev else c_f) * c_sz, c_sz), c_sz) for rev in seg_rev]

        def load(lo, hi):
            return [pb_ref[seg // 2, chunk_rows[seg], lo:hi] for seg in range(n_seg)]

        q_l = load(q_off, q_off + qk_w)
        k_l = load(k_off, k_off + qk_w)
        v_l = load(v_off, v_off + B_W)
        g_l = load(g_off, g_off + LANES)
        gts = jnp.concatenate(
            [pltpu.roll(g + bias_ref[...], LANES - 2 * HEADS, 1) if rev else g + bias_ref[...]
             for g, rev in zip(g_l, seg_rev)], axis=0)
        log_f = jnp.minimum(gts, 0.0) - jnp.log(1.0 + jnp.exp(-jnp.abs(gts)))
        cum_col = sum(jnp.dot(tri, p, preferred_element_type=F32) for p in _split3(log_f))
        both = sum(jnp.dot(p, sel, preferred_element_type=F32)
                   for p in _split3(jnp.concatenate([cum_col, gts], axis=1)))
        cum_t = both[:, :w]
        ig_t = both[:, w:]
        src = per_seg(lambda x: jnp.sum(jnp.where(diag4, x, 0.0), axis=0, keepdims=True), cum_t - ig_t)
        m_prev = rows_of(m_prev_rows)
        log_d = jnp.where(mask, cum_t - src, MASK_NEG)
        log_inter = cum_t + m_prev
        m_t = jnp.maximum(log_inter, head_max(log_d))
        w_intra = jnp.where(mask, jnp.exp(log_d - m_t), 0.0)
        w_inter = jnp.exp(log_inter - m_t)

        scale = B_QK ** -0.5
        qb_l = [(q * scale).astype(BF16) for q in q_l]
        kb_l = [k.astype(BF16) for k in k_l]
        zero = jnp.zeros((), BF16)
        scores = [_nt(qb, jnp.where(k_block, jnp.concatenate([kb] * HEADS, axis=0), zero))
                  for qb, kb in zip(qb_l, kb_l)]
        p = (jnp.concatenate(scores, axis=0) * w_intra).astype(BF16)

        cn_l = [cn_ref[seg] for seg in range(n_seg)]
        inter = jnp.concatenate([jnp.dot(qb, cn.astype(BF16), preferred_element_type=F32)
                                 for qb, cn in zip(qb_l, cn_l)], axis=0)
        values = [jnp.dot(p[seg_rows(seg)],
                          jnp.where(v_block, jnp.concatenate([v_l[seg].astype(BF16)] * HEADS, axis=0), zero),
                          preferred_element_type=F32) for seg in range(n_seg)]
        num = w_inter * inter[:, :w] + jnp.concatenate(values, axis=0)
        den = w_inter * inter[:, w:] + jnp.dot(p, head_ones, preferred_element_type=F32)
        h_out = num / jnp.maximum(jnp.abs(den), jnp.exp(-m_t))
        for seg, rev in enumerate(seg_rev):
            if rev:
                ob_ref[seg // 2, chunk_rows[seg], :] = h_out[seg_rows(seg)]
            else:
                o_ref[seg // 2, chunk_rows[seg], :] = h_out[seg_rows(seg)]

        ends = [cum_t[seg * c_sz + (0 if rev else c_sz - 1):seg * c_sz + (0 if rev else c_sz - 1) + 1, :]
                for seg, rev in enumerate(seg_rev)]
        log_end = rows_of(ends) - cum_t + ig_t
        m_ends = [jnp.maximum(end + m_old, jnp.max(log_end[seg_rows(seg)], axis=0, keepdims=True))
                  for seg, (end, m_old) in enumerate(zip(ends, m_prev_rows))]
        w_end = jnp.exp(log_end - rows_of(m_ends))
        for seg in range(n_seg):
            w_seg = w_end[seg_rows(seg)]
            upd = _tn(kb_l[seg], jnp.concatenate([(w_seg * v_l[seg]).astype(BF16), w_seg.astype(BF16)], axis=1))
            carry_w = jnp.exp(ends[seg] + m_prev_rows[seg] - m_ends[seg])
            cn_ref[seg] = (jnp.concatenate([carry_w, carry_w], axis=1) * cn_l[seg]
                           + jnp.where(state_block, upd, 0.0))
        return tuple(m_ends)

    cn_ref[...] = jnp.zeros_like(cn_ref)
    lax.fori_loop(0, n_all, scan_step, tuple(jnp.zeros((1, w), F32) for _ in range(n_seg)))

    def readout(c, carry):
        rows = pl.ds(pl.multiple_of(c * ROW_TILE, ROW_TILE), ROW_TILE)
        for b in range(n_b):
            tot = o_ref[b, rows, :] + ob_ref[b, rows, :]
            normed = jnp.concatenate(
                [_rms(tot[:, h * HEAD_V:(h + 1) * HEAD_V]) for h in range(HEADS)], axis=1) * gain_ref[...]
            og = pb_ref[b, rows, o_off:o_off + B_W]
            o_ref[b, rows, :] = normed * jax.nn.sigmoid(og)
        return carry

    lax.fori_loop(0, n_all * c_sz // ROW_TILE, readout, 0)


def _mlstm(pb3, bias, gain, n_ctx_rows):
    n_batch, rows, _ = pb3.shape
    n_b = SCAN_BATCHES if n_batch % SCAN_BATCHES == 0 else 1
    kern = functools.partial(_mlstm_kernel, n_ctx=n_ctx_rows // CHUNK, n_all=rows // CHUNK, n_b=n_b)
    return pl.pallas_call(
        kern,
        out_shape=jax.ShapeDtypeStruct((n_batch, rows, B_W), F32),
        grid=(n_batch // n_b,),
        in_specs=[pl.BlockSpec((n_b, rows, PB_W), lambda b: (b, 0, 0), pipeline_mode=pl.Buffered(1)),
                  pl.BlockSpec((1, LANES), lambda b: (0, 0)),
                  pl.BlockSpec((1, B_W), lambda b: (0, 0))],
        out_specs=pl.BlockSpec((n_b, rows, B_W), lambda b: (b, 0, 0)),
        scratch_shapes=[pltpu.VMEM((2 * n_b, HEADS * B_QK, 2 * B_W), F32), pltpu.VMEM((n_b, rows, B_W), F32)],
        compiler_params=_cparams(("arbitrary",)),
        name="mlstm",
    )(pb3, bias, gain)


def _attn_kernel(lam_ref, qp_ref, qr_ref, k_ref, v_ref, gain_ref, o_ref, *, n_ctx, q_tile0, lam_init):
    lam = lam_ref[0]
    q_tile = pl.program_id(2) + q_tile0
    lane = lax.broadcasted_iota(jnp.int32, (1, 2 * C_DQK), 1)

    n_all = k_ref.shape[1]

    def finish(head, parts):
        o = parts[0] - lam * parts[1]
        o_ref[0, :, head] = _rms(o) * gain_ref[...] * (1.0 - lam_init)

    def sub_query(q, j):
        return jnp.where(lane // C_DQK == j, q, jnp.zeros_like(q))

    def row_max(s):
        return jnp.max(s, axis=-1, keepdims=True)

    def row_sum(s):
        return jnp.sum(s, axis=-1, keepdims=True)

    def pv(ex, v):
        return jnp.dot(ex.astype(BF16), v, preferred_element_type=F32)

    heads = [slice(hh * C_DV, (hh + 1) * C_DV) for hh in range(ATTN_HEADS_PER_STEP)]

    def attend(pieces):
        problems = [(head, j) for head in heads for j in range(2)]
        scores = [[_nt(sub_query(q_ref[0, :, head], j), k_ref[0, k0:k1, head]) for q_ref, k0, k1 in pieces]
                  for head, j in problems]
        maxes = [functools.reduce(jnp.maximum, [row_max(s) for s in ss]) for ss in scores]
        exps = [[jnp.exp2(s - m) for s in ss] for ss, m in zip(scores, maxes)]
        outs = [sum(pv(e, v_ref[0, k0:k1, head]) for e, (_, k0, k1) in zip(es, pieces))
                / sum(row_sum(e) for e in es) for es, (head, _) in zip(exps, problems)]
        for n, head in enumerate(heads):
            finish(head, outs[2 * n:2 * n + 2])

    @pl.when(q_tile * ROW_TILE < n_ctx)
    def _():
        attend([(qp_ref, 0, n_ctx)])

    @pl.when(q_tile * ROW_TILE >= n_ctx)
    def _():
        attend([(qp_ref, 0, n_ctx), (qr_ref, n_ctx, n_all)])


def _attention(pc3, lam, gain, n_ctx_rows, lam_init, with_ctx):
    n_batch, rows, _ = pc3.shape
    q_tile0 = 0 if with_ctx else n_ctx_rows // ROW_TILE
    n_q = rows // ROW_TILE - q_tile0
    gw = ATTN_HEADS_PER_STEP * C_DV
    hb = C_W // gw
    kern = functools.partial(_attn_kernel, n_ctx=n_ctx_rows, q_tile0=q_tile0, lam_init=lam_init)
    grid_spec = pltpu.PrefetchScalarGridSpec(
        num_scalar_prefetch=1,
        grid=(n_batch, hb, n_q),
        in_specs=[pl.BlockSpec((1, ROW_TILE, gw), lambda b, h, i, lam: (b, i + q_tile0, h)),
                  pl.BlockSpec((1, ROW_TILE, gw), lambda b, h, i, lam: (b, i + q_tile0, hb + h)),
                  pl.BlockSpec((1, rows, gw), lambda b, h, i, lam: (b, 0, 2 * hb + h)),
                  pl.BlockSpec((1, rows, gw), lambda b, h, i, lam: (b, 0, 3 * hb + h)),
                  pl.BlockSpec((1, C_DV), lambda b, h, i, lam: (0, 0))],
        out_specs=pl.BlockSpec((1, ROW_TILE, gw), lambda b, h, i, lam: (b, i, h)),
    )
    return pl.pallas_call(
        kern,
        out_shape=jax.ShapeDtypeStruct((n_batch, n_q * ROW_TILE, C_W), F32),
        grid_spec=grid_spec,
        compiler_params=_cparams(("arbitrary", "arbitrary", "arbitrary")),
        name="diff_attn",
    )(lam, pc3, pc3, pc3, pc3, gain)


def _outproj_kernel(oa_ref, ob_ref, oc_ref, x_ref, mod_ref, n_ref, w_ref, rw_ref, rb_ref,
                    xn_ref, h_ref, route_ref, cnt_ref, carry_ref):
    d = x_ref.shape[1]
    n_exp = rw_ref.shape[1]

    @pl.when(pl.program_id(0) == 0)
    def _():
        carry_ref[...] = jnp.zeros_like(carry_ref)

    n_rows = x_ref.shape[0]
    halves = [slice(g * n_rows // ROW_GROUPS, (g + 1) * n_rows // ROW_GROUPS) for g in range(ROW_GROUPS)]
    mixes = [jnp.dot(oa_ref[hs, :].astype(BF16), w_ref[0:A_W, :], preferred_element_type=F32)
             + jnp.dot(ob_ref[hs, :].astype(BF16), w_ref[A_W:A_W + B_W, :], preferred_element_type=F32)
             + jnp.dot(oc_ref[hs, :].astype(BF16), w_ref[A_W + B_W:A_W + B_W + C_W, :],
                       preferred_element_type=F32) for hs in halves]
    xns = [x_ref[hs, :] + mod_ref[0, :, 2 * d:3 * d] * (_rms(mix) * n_ref[1:2, :])
           for hs, mix in zip(halves, mixes)]
    hiddens = [_rms(xn) * n_ref[2:3, :] * (1.0 + mod_ref[0, :, 4 * d:5 * d]) + mod_ref[0, :, 3 * d:4 * d]
               for xn in xns]
    for hs, xn, h in zip(halves, xns, hiddens):
        xn_ref[hs, :] = xn
        h_ref[hs, :] = _pack_bf16_pairs(h)

    h_his = [h.astype(BF16) for h in hiddens]
    h_mids = [(h - h_hi.astype(F32)).astype(BF16) for h, h_hi in zip(hiddens, h_his)]
    logits = jnp.concatenate([_nt(rw_ref[0], h_hi) + _nt(rw_ref[1], h_hi) + _nt(rw_ref[0], h_mid)
                              for h_hi, h_mid in zip(h_his, h_mids)], axis=1) + rb_ref[...]
    e_sub = lax.broadcasted_iota(jnp.int32, logits.shape, 0)
    cur = logits
    picks, vals = [], []
    for _ in range(TOP_K):
        mx = jnp.max(cur, axis=0, keepdims=True)
        idx = jnp.min(jnp.where(cur == mx, e_sub, n_exp), axis=0, keepdims=True)
        hit = e_sub == idx
        cur = jnp.where(hit, -jnp.inf, cur)
        picks.append((idx, hit.astype(F32)))
        vals.append(mx)
    exps = [jnp.exp(vv - vals[0]) for vv in vals]
    total = exps[0] + exps[1] + exps[2] + exps[3]

    chosen = picks[0][1] + picks[1][1] + picks[2][1] + picks[3][1]
    tm = logits.shape[1]
    before = (lax.broadcasted_iota(jnp.int32, (tm, tm), 0) < lax.broadcasted_iota(jnp.int32, (tm, tm), 1))
    seen = jnp.dot(chosen.astype(BF16), before.astype(BF16), preferred_element_type=F32) + carry_ref[...]
    new_carry = carry_ref[...] + jnp.sum(chosen, axis=1, keepdims=True)
    carry_ref[...] = new_carry
    cnt_ref[...] = new_carry

    o_row = lax.broadcasted_iota(jnp.int32, route_ref.shape, 0)
    route = jnp.zeros(route_ref.shape, F32)
    for kk in range(TOP_K):
        idx, hit = picks[kk]
        rank = jnp.sum(hit * seen, axis=0, keepdims=True)
        route = (route + jnp.where(o_row == kk, idx.astype(F32), 0.0)
                 + jnp.where(o_row == TOP_K + kk, exps[kk] / total, 0.0)
                 + jnp.where(o_row == 2 * TOP_K + kk, rank, 0.0))
    route_ref[...] = route


ROUTE_ROWS = 16


def _outproj(oa, ob, oc, xy, mod, norms, w_out_b, router_wt, router_b, layer, n_tiles, in_tile, oc_tile, mod_row):
    d = xy.shape[1]
    n_exp = router_wt.shape[2]
    t_out = n_tiles * ROW_TILE
    xn, h, route_t, counts = pl.pallas_call(
        _outproj_kernel,
        out_shape=(jax.ShapeDtypeStruct((t_out, d), F32),
                   jax.ShapeDtypeStruct((t_out, d // 2), jnp.int32),
                   jax.ShapeDtypeStruct((n_tiles * ROUTE_ROWS, ROW_TILE), F32),
                   jax.ShapeDtypeStruct((n_exp, 1), F32)),
        grid=(n_tiles,),
        in_specs=[pl.BlockSpec((ROW_TILE, A_W), lambda i: (in_tile(i), 0)),
                  pl.BlockSpec((ROW_TILE, B_W), lambda i: (in_tile(i), 0)),
                  pl.BlockSpec((ROW_TILE, C_W), lambda i: (oc_tile(i), 0)),
                  pl.BlockSpec((ROW_TILE, d), lambda i: (in_tile(i), 0)),
                  pl.BlockSpec((1, 1, 6 * d), lambda i: (mod_row(i), 0, 0)),
                  pl.BlockSpec((None, 4, d), lambda i: (layer, 0, 0)),
                  pl.BlockSpec((None, d, d), lambda i: (layer, 0, 0)),
                  pl.BlockSpec((None, 2, n_exp, d), lambda i: (layer, 0, 0, 0)),
                  pl.BlockSpec((None, n_exp, 1), lambda i: (layer, 0, 0))],
        out_specs=(pl.BlockSpec((ROW_TILE, d), lambda i: (i, 0)),
                   pl.BlockSpec((ROW_TILE, d // 2), lambda i: (i, 0)),
                   pl.BlockSpec((ROUTE_ROWS, ROW_TILE), lambda i: (i, 0)),
                   pl.BlockSpec((n_exp, 1), lambda i: (0, 0))),
        scratch_shapes=[pltpu.VMEM((n_exp, 1), F32)],
        compiler_params=_cparams(("arbitrary",)),
        name="outproj_router",
    )(oa, ob, oc, xy, mod, norms, w_out_b, router_wt, router_b)
    return xn, h, route_t.reshape(n_tiles, ROUTE_ROWS, ROW_TILE), counts[:, 0]


PAIR_BLOCK = 2 * LANES


def _moe_kernel(te_ref, first_ref, valid_ref, nu_ref, slot_ref, next_ref, xs_ref, w1_hbm, b1_ref, w2_hbm, b2_ref,
                ys_ref, w1p_ref, w2b_ref, w1_buf, w2_buf, w_sem, *, layer):
    i = pl.program_id(0)
    two_f = w1_buf.shape[2]
    n_blk = two_f // PAIR_BLOCK

    def fetch(expert, slot):
        return (pltpu.make_async_copy(w1_hbm.at[layer, expert], w1_buf.at[slot], w_sem.at[0, slot]),
                pltpu.make_async_copy(w2_hbm.at[layer, expert], w2_buf.at[slot], w_sem.at[1, slot]))

    @pl.when(i == 0)
    def _():
        for copy in fetch(te_ref[0], 0):
            copy.start()

    @pl.when(jnp.logical_and(i < nu_ref[0], first_ref[i] == 1))
    def _():
        slot = slot_ref[i]
        for copy in fetch(te_ref[i], slot):
            copy.wait()
        r = lax.broadcasted_iota(jnp.int32, (PAIR_BLOCK, PAIR_BLOCK), 0)
        c = lax.broadcasted_iota(jnp.int32, (PAIR_BLOCK, PAIR_BLOCK), 1)
        perm = (r == jnp.where(c < LANES, 2 * c, 2 * (c - LANES) + 1)).astype(BF16)
        for blk in range(n_blk):
            cols = slice(blk * PAIR_BLOCK, (blk + 1) * PAIR_BLOCK)
            w1p_ref[:, cols] = jnp.dot(w1_buf[slot, :, cols].astype(BF16), perm,
                                       preferred_element_type=F32).astype(BF16)
        w2b_ref[...] = w2_buf[slot].astype(BF16)

        @pl.when(next_ref[i] >= 0)
        def _():
            for copy in fetch(next_ref[i], 1 - slot):
                copy.start()

    tile_rows = xs_ref.shape[0]
    used = i < nu_ref[0]
    valid = valid_ref[i]

    def expert_ffn(n_rows):
        x = _unpack_bf16_pairs(xs_ref[0:n_rows, :]).astype(BF16)
        row = lax.broadcasted_iota(jnp.int32, (n_rows, 1), 0)
        x = jnp.where(row < valid, x, jnp.zeros_like(x))
        hid = jnp.dot(x, w1p_ref[...], preferred_element_type=F32) + b1_ref[...]
        acts = []
        for blk in range(n_blk):
            glu = jnp.minimum(hid[:, blk * PAIR_BLOCK:blk * PAIR_BLOCK + LANES], SWIGLU_LIMIT)
            lin = jnp.clip(hid[:, blk * PAIR_BLOCK + LANES:(blk + 1) * PAIR_BLOCK], -SWIGLU_LIMIT, SWIGLU_LIMIT)
            acts.append((glu * jax.nn.sigmoid(SWIGLU_ALPHA * glu) * (lin + 1.0)).astype(BF16))
        y = jnp.dot(jnp.concatenate(acts, axis=1), w2b_ref[...], preferred_element_type=F32) + b2_ref[...]
        ys_ref[0:n_rows, :] = _pack_bf16_pairs(y)
        if n_rows < tile_rows:
            ys_ref[n_rows:tile_rows, :] = jnp.zeros((tile_rows - n_rows, ys_ref.shape[1]), ys_ref.dtype)

    @pl.when(jnp.logical_and(used, valid > tile_rows // 2))
    def _():
        expert_ffn(tile_rows)

    @pl.when(jnp.logical_and(used, valid <= tile_rows // 2))
    def _():
        expert_ffn(tile_rows // 2)

    @pl.when(jnp.logical_not(used))
    def _():
        ys_ref[...] = jnp.zeros_like(ys_ref)


def _moe(plan, xs, w1, b1p, w2, b2, layer):
    r_max, half_d = xs.shape
    d = 2 * half_d
    two_f = w1.shape[3]
    f = two_f // 2
    n_tiles = r_max // MOE_TILE

    def row_tile(i, te, fi, va, nu, sl, nx):
        return (jnp.maximum(jnp.minimum(i, nu[0] - 1), 0), 0)

    def bias_block(i, te, fi, va, nu, sl, nx):
        return (layer, te[i], 0, 0)

    grid_spec = pltpu.PrefetchScalarGridSpec(
        num_scalar_prefetch=6,
        grid=(n_tiles,),
        in_specs=[pl.BlockSpec((MOE_TILE, half_d), row_tile),
                  pl.BlockSpec(memory_space=pl.ANY),
                  pl.BlockSpec((None, None, 1, two_f), bias_block),
                  pl.BlockSpec(memory_space=pl.ANY),
                  pl.BlockSpec((None, None, 1, d), bias_block)],
        out_specs=pl.BlockSpec((MOE_TILE, half_d), lambda i, te, fi, va, nu, sl, nx: (i, 0)),
        scratch_shapes=[pltpu.VMEM((d, two_f), BF16), pltpu.VMEM((f, d), BF16),
                        pltpu.VMEM((2, d, two_f), F32), pltpu.VMEM((2, f, d), F32),
                        pltpu.SemaphoreType.DMA((2, 2))],
    )
    return pl.pallas_call(
        functools.partial(_moe_kernel, layer=layer),
        out_shape=jax.ShapeDtypeStruct((r_max, half_d), jnp.int32),
        grid_spec=grid_spec,
        compiler_params=_cparams(("arbitrary",)),
        name="moe_experts",
    )(*plan, xs, w1, b1p, w2, b2)


SC_CORES = 2
SC_SUBCORES = 16
SC_CHUNK = 64


def _dispatch_rows(hp, pos, r_max):
    t, width = hp.shape
    workers = SC_CORES * SC_SUBCORES
    assert t % (workers * SC_CHUNK) == 0
    per_worker = t // (workers * SC_CHUNK)
    idx = pos.reshape(-1, TOP_K, ROW_TILE // SC_CHUNK, SC_CHUNK).transpose(0, 2, 1, 3).reshape(
        workers, per_worker * TOP_K, SC_CHUNK)
    mesh = plsc.VectorSubcoreMesh(core_axis_name="c", subcore_axis_name="s",
                                  num_cores=SC_CORES, num_subcores=SC_SUBCORES)

    @functools.partial(
        pl.kernel, mesh=mesh,
        out_type=jax.ShapeDtypeStruct((r_max, width), hp.dtype),
        scratch_types=[pltpu.VMEM((per_worker * TOP_K, SC_CHUNK), jnp.int32),
                       pltpu.VMEM((2, SC_CHUNK, width), hp.dtype),
                       pltpu.SemaphoreType.DMA((2,)),
                       pltpu.SemaphoreType.DMA((2,))],
    )
    def scatter(hp_hbm, idx_hbm, out_hbm, idx_v, rows_v, read_sem, write_sem):
        wid = lax.axis_index("s") * SC_CORES + lax.axis_index("c")
        pltpu.sync_copy(idx_hbm.at[wid], idx_v)

        def read(j):
            row0 = pl.multiple_of((wid * per_worker + j) * SC_CHUNK, SC_CHUNK)
            return pltpu.make_async_copy(hp_hbm.at[pl.ds(row0, SC_CHUNK)], rows_v.at[j % 2], read_sem.at[j % 2])

        def write(j, k):
            return pltpu.make_async_copy(rows_v.at[j % 2], out_hbm.at[idx_v.at[j * TOP_K + k]],
                                         write_sem.at[j % 2])

        read(0).start()
        for j in range(per_worker):
            read(j).wait()
            for k in range(TOP_K):
                write(j, k).start()
            if j + 1 < per_worker:
                if j >= 1:
                    for k in range(TOP_K):
                        write(j - 1, k).wait()
                read(j + 1).start()
        for j in range(max(per_worker - 2, 0), per_worker):
            for k in range(TOP_K):
                write(j, k).wait()

    return scatter(hp, idx)


def _combine_rows(ysp, pos):
    width = ysp.shape[1]
    t = pos.shape[0] * pos.shape[2]
    workers = SC_CORES * SC_SUBCORES
    assert t % (workers * SC_CHUNK) == 0
    per_worker = t // (workers * SC_CHUNK)
    units = per_worker * TOP_K
    idx = pos.reshape(-1, TOP_K, ROW_TILE // SC_CHUNK, SC_CHUNK).transpose(0, 2, 1, 3).reshape(
        workers, units, SC_CHUNK)
    mesh = plsc.VectorSubcoreMesh(core_axis_name="c", subcore_axis_name="s",
                                  num_cores=SC_CORES, num_subcores=SC_SUBCORES)

    @functools.partial(
        pl.kernel, mesh=mesh,
        out_type=jax.ShapeDtypeStruct((TOP_K, t, width), ysp.dtype),
        scratch_types=[pltpu.VMEM((units, SC_CHUNK), jnp.int32),
                       pltpu.VMEM((2, SC_CHUNK, width), ysp.dtype),
                       pltpu.SemaphoreType.DMA((2,)),
                       pltpu.SemaphoreType.DMA((2,))],
    )
    def gather(ys_hbm, idx_hbm, out_hbm, idx_v, rows_v, read_sem, write_sem):
        wid = lax.axis_index("s") * SC_CORES + lax.axis_index("c")
        pltpu.sync_copy(idx_hbm.at[wid], idx_v)

        def read(u):
            return pltpu.make_async_copy(ys_hbm.at[idx_v.at[u]], rows_v.at[u % 2], read_sem.at[u % 2])

        def write(u):
            row0 = pl.multiple_of((wid * per_worker + u // TOP_K) * SC_CHUNK, SC_CHUNK)
            return pltpu.make_async_copy(rows_v.at[u % 2], out_hbm.at[u % TOP_K, pl.ds(row0, SC_CHUNK)],
                                         write_sem.at[u % 2])

        read(0).start()
        for u in range(units):
            read(u).wait()
            write(u).start()
            if u + 1 < units:
                if u >= 1:
                    write(u - 1).wait()
                read(u + 1).start()
        for u in range(max(units - 2, 0), units):
            write(u).wait()

    return gather(ysp, idx)


def _ffn_residual_kernel(x_ref, w_ref, y0_ref, y1_ref, y2_ref, y3_ref, mod_ref, n_ref, o_ref):
    d = x_ref.shape[1]
    wts = w_ref[...]
    ffn = None
    for k, y_ref in enumerate((y0_ref, y1_ref, y2_ref, y3_ref)):
        term = wts[:, k:k + 1] * _unpack_bf16_pairs(y_ref[...])
        ffn = term if ffn is None else ffn + term
    o_ref[...] = x_ref[...] + mod_ref[0, :, 5 * d:6 * d] * (_rms(ffn) * n_ref[3:4, :])


def _ffn_residual(xn, wts, ys_by_k, mod, norms, layer, mod_row):
    t, d = xn.shape
    row_spec = pl.BlockSpec((ROW_TILE, d), lambda i: (i, 0))

    def y_spec(k):
        return pl.BlockSpec((None, ROW_TILE, d // 2), lambda i: (k, i, 0))

    return pl.pallas_call(
        _ffn_residual_kernel,
        out_shape=jax.ShapeDtypeStruct((t, d), F32),
        grid=(t // ROW_TILE,),
        in_specs=[row_spec,
                  pl.BlockSpec((ROW_TILE, TOP_K), lambda i: (i, 0)),
                  y_spec(0), y_spec(1), y_spec(2), y_spec(3),
                  pl.BlockSpec((1, 1, 6 * d), lambda i: (mod_row(i), 0, 0)),
                  pl.BlockSpec((None, 4, d), lambda i: (layer, 0, 0))],
        out_specs=row_spec,
        compiler_params=_cparams(("arbitrary",)),
        name="ffn_residual",
    )(xn, wts, ys_by_k, ys_by_k, ys_by_k, ys_by_k, mod, norms)


def _dispatch_plan(route_t, counts, r_max):
    n_row_tiles = route_t.shape[0]
    idx = route_t[:, 0:TOP_K, :].astype(jnp.int32)
    wts = route_t[:, TOP_K:2 * TOP_K, :].transpose(0, 2, 1).reshape(n_row_tiles * ROW_TILE, TOP_K)
    rank = route_t[:, 2 * TOP_K:3 * TOP_K, :].astype(jnp.int32)
    cnt = counts.astype(jnp.int32)
    padded = (cnt + MOE_TILE - 1) // MOE_TILE * MOE_TILE
    ends = jnp.cumsum(padded)
    starts = ends - padded
    pos = rank
    for e in range(cnt.shape[0]):
        pos = pos + jnp.where(idx == e, starts[e], 0)
    n_tiles = r_max // MOE_TILE
    n_used = ends[-1] // MOE_TILE
    tile_ids = jnp.minimum(jnp.arange(n_tiles, dtype=jnp.int32), n_used - 1)
    tile_expert = jnp.sum((ends // MOE_TILE)[None, :] <= tile_ids[:, None], axis=1).astype(jnp.int32)
    tile_first = jnp.concatenate(
        [jnp.ones((1,), jnp.int32), (tile_expert[1:] != tile_expert[:-1]).astype(jnp.int32)])
    of_expert = (tile_expert[:, None] == jnp.arange(cnt.shape[0], dtype=jnp.int32)[None, :]).astype(jnp.int32)
    first_tile = jnp.sum(of_expert * ((ends - padded) // MOE_TILE)[None, :], axis=1)
    tile_valid = jnp.clip(jnp.sum(of_expert * cnt[None, :], axis=1) - (tile_ids - first_tile) * MOE_TILE,
                          0, MOE_TILE).astype(jnp.int32)
    experts = jnp.arange(cnt.shape[0], dtype=jnp.int32)
    has_rows = cnt > 0
    tile_slot = (jnp.sum(of_expert * (jnp.cumsum(has_rows.astype(jnp.int32)) - 1)[None, :], axis=1) % 2).astype(jnp.int32)
    later_with_rows = jnp.logical_and(has_rows[None, :], experts[None, :] > experts[:, None])
    next_expert = jnp.min(jnp.where(later_with_rows, experts[None, :], cnt.shape[0]), axis=1)
    next_expert = jnp.where(next_expert < cnt.shape[0], next_expert, -1)
    tile_next = jnp.sum(of_expert * next_expert[None, :], axis=1).astype(jnp.int32)
    plan = (tile_expert, tile_first, tile_valid, n_used.reshape(1).astype(jnp.int32), tile_slot, tile_next)
    return pos, wts, plan


def _rope_tables(length, n_ctx_rows):
    rows = length // GRID_W
    row = jnp.repeat(jnp.arange(rows, dtype=F32), GRID_W)
    col = jnp.tile(jnp.arange(GRID_W, dtype=F32), rows)
    n_freq = C_DQK // 4
    inv_freq = ROPE_THETA ** (-jnp.arange(n_freq, dtype=F32) / n_freq)
    ang_r = row[:, None] * inv_freq
    ang_c = col[:, None] * inv_freq
    cos = jnp.concatenate([jnp.cos(ang_r), jnp.cos(ang_r), jnp.cos(ang_c), jnp.cos(ang_c)], axis=-1)
    sin = jnp.concatenate([-jnp.sin(ang_r), jnp.sin(ang_r), -jnp.sin(ang_c), jnp.sin(ang_c)], axis=-1)
    reps = C_W // C_DQK
    cos = jnp.concatenate([jnp.ones((n_ctx_rows, C_W), F32), jnp.tile(cos, (1, reps))], axis=0)
    sin = jnp.concatenate([jnp.zeros((n_ctx_rows, C_W), F32), jnp.tile(sin, (1, reps))], axis=0)
    return cos, sin


def _hgrn_lower_bound(table, layer):
    p = jax.nn.softmax(table.astype(F32), axis=1)
    cum = jnp.cumsum(p, axis=1) - p[:, :1]
    return jnp.clip(cum[:, layer], 0.0, 1.0)


def kernel(x, c, ctx, c_ctx, ada_w, ada_b, sandwich_norms, w_in, w_out, hgrn_lower_bounds, hgrn_norm,
           mlstm_gate_bias, mlstm_norm, diff_lambdas, diff_norm, router_w, router_b, moe_w1, moe_b1,
           moe_w2, moe_b2):
    n_batch, seq, d = x.shape
    n_ctx_rows = ctx.shape[1]
    depth = w_in.shape[0]
    n_exp = router_w.shape[2]
    assert seq % ROW_TILE == 0 and n_ctx_rows % ROW_TILE == 0 and seq % GRID_W == 0
    rows_b = n_ctx_rows + seq
    tiles_b = rows_b // ROW_TILE
    ctx_tiles = n_ctx_rows // ROW_TILE
    lat_tiles = seq // ROW_TILE

    b_main = PA_W + 2 * HEADS * B_QK + 2 * B_W
    w_in_p = jnp.concatenate(
        [w_in[:, :, :b_main],
         jnp.pad(w_in[:, :, b_main:b_main + N_GATE], ((0, 0), (0, 0), (0, LANES - N_GATE))),
         w_in[:, :, b_main + N_GATE:]], axis=2).astype(BF16)
    w_out_b = w_out.astype(BF16)
    two_f = moe_b1.shape[2]
    b1p = moe_b1.reshape(depth, n_exp, two_f // PAIR_BLOCK, LANES, 2).transpose(0, 1, 2, 4, 3).reshape(
        depth, n_exp, 1, two_f)
    b2 = moe_b2[:, :, None, :]
    ada_b3 = ada_b[:, None, :]
    router_b3 = router_b[:, :, None]
    rw_t = router_w.transpose(0, 2, 1)
    rw_hi = rw_t.astype(BF16)
    router_wt = jnp.stack([rw_hi, (rw_t - rw_hi.astype(F32)).astype(BF16)], axis=1)
    gate_bias = jnp.pad(mlstm_gate_bias, ((0, 0), (0, LANES - N_GATE)))
    cos, sin = _rope_tables(seq, n_ctx_rows)

    cond_rows = (n_batch + 1 + 7) // 8 * 8
    cvec = jnp.zeros((cond_rows, d), F32).at[:n_batch].set(c).at[n_batch].set(c_ctx)

    def mod_row_all(i):
        return jnp.where(i % tiles_b < ctx_tiles, n_batch, i // tiles_b)

    xy = jnp.concatenate([ctx, x], axis=1).reshape(n_batch * rows_b, d)

    for layer in range(depth):
        last = layer == depth - 1
        mod = _adaln(cvec, ada_w, ada_b3, layer).reshape(cond_rows, 1, 6 * d)
        lb = _hgrn_lower_bound(hgrn_lower_bounds, layer)
        lam_init = 0.8 - 0.6 * math.exp(-0.3 * layer)
        lq1, lk1, lq2, lk2 = diff_lambdas[layer].astype(F32)
        lam = (jnp.exp(jnp.sum(lq1 * lk1)) - jnp.exp(jnp.sum(lq2 * lk2)) + lam_init).reshape(1)

        pa, pb, pc = _inproj(xy, mod, sandwich_norms, w_in_p, cos, sin, layer, tiles_b, ctx_tiles, n_batch)
        oa = _hgrn(pa.reshape(n_batch, rows_b, PA_W), lb, jnp.tile(hgrn_norm[layer], HEADS)[None, :], n_ctx_rows)
        ob = _mlstm(pb.reshape(n_batch, rows_b, PB_W), gate_bias[layer][None, :], mlstm_norm[layer][None, :],
                    n_ctx_rows)
        oc = _attention(pc.reshape(n_batch, rows_b, PC_OUT), lam, diff_norm[layer][None, :], n_ctx_rows,
                        lam_init, not last)

        if last:
            n_tiles = n_batch * lat_tiles
            in_tile = lambda i: (i // lat_tiles) * tiles_b + ctx_tiles + i % lat_tiles
            mod_row = lambda i: i // lat_tiles
        else:
            n_tiles = n_batch * tiles_b
            in_tile = lambda i: i
            mod_row = mod_row_all
        xn, h, route, counts = _outproj(
            oa.reshape(-1, A_W), ob.reshape(-1, B_W), oc.reshape(-1, C_W), xy, mod, sandwich_norms, w_out_b,
            router_wt, router_b3, layer, n_tiles, in_tile, lambda i: i, mod_row)

        r_max = n_tiles * ROW_TILE * TOP_K + n_exp * MOE_TILE
        pos, wts, plan = _dispatch_plan(route, counts, r_max)
        xs = _dispatch_rows(h, pos, r_max)
        ys = _moe(plan, xs, moe_w1, b1p, moe_w2, b2, layer)
        xy = _ffn_residual(xn, wts, _combine_rows(ys, pos), mod, sandwich_norms, layer, mod_row)

    return xy.reshape(n_batch, seq, d)
```

```python
import functools
import math

import jax
import jax.numpy as jnp
from jax import lax
from jax.experimental import pallas as pl
from jax.experimental.pallas import tpu as pltpu
from jax.experimental.pallas import tpu_sc as plsc

F32 = jnp.float32
BF16 = jnp.bfloat16
HI = lax.Precision.HIGHEST

HEADS = 4
A_W = 256
B_QK = 32
B_W = 256
C_DQK = 64
C_DV = 2 * C_DQK
ATTN_KEY_SCALE = C_DQK ** -0.5 * math.log2(math.e)
C_W = 512
HEAD_V = 64
N_GATE = 16
GRID_W = 64
TOP_K = 4
SWIGLU_ALPHA = 1.702
SWIGLU_LIMIT = 7.0
ROPE_THETA = 10000.0
NORM_EPS = 1e-6
MASK_NEG = -1e30
F_MIN = 1e-12

LANES = 128
ROW_TILE = 256
CHUNK = 64
SCAN_BATCHES = 2
ROW_GROUPS = 2
MOE_TILE = 512
ATTN_HEADS_PER_STEP = 2
VMEM_LIMIT = 56 * 1024 * 1024

PA_W = 5 * A_W
PB_W = 2 * HEADS * B_QK + 2 * B_W + LANES
PC_IN = 3 * C_W
PC_OUT = 4 * C_W
W_IN_PAD = PA_W + PB_W + PC_IN


def _cparams(sem):
    return pltpu.CompilerParams(dimension_semantics=sem, vmem_limit_bytes=VMEM_LIMIT)


def _nt(a, b):
    return lax.dot_general(a, b, (((1,), (1,)), ((), ())), preferred_element_type=F32)


def _tn(a, b, precision=None):
    return lax.dot_general(a, b, (((0,), (0,)), ((), ())), preferred_element_type=F32, precision=precision)


def _rms(x):
    return x * lax.rsqrt(jnp.mean(x * x, axis=-1, keepdims=True) + NORM_EPS)


def _silu(x):
    return x * jax.nn.sigmoid(x)


def _pack_bf16_pairs(x):
    half = x.shape[1] // 2
    bits = pltpu.bitcast(x.astype(BF16).astype(F32), jnp.uint32)
    return pltpu.bitcast((bits[:, :half] >> 16) | (bits[:, half:] & jnp.uint32(0xFFFF0000)), jnp.int32)


def _unpack_bf16_pairs(words):
    bits = pltpu.bitcast(words, jnp.uint32)
    return jnp.concatenate([pltpu.bitcast(bits << 16, F32),
                            pltpu.bitcast(bits & jnp.uint32(0xFFFF0000), F32)], axis=1)


def _adaln_kernel(c_ref, w_ref, b_ref, o_ref):
    cond = _silu(c_ref[...])
    o_ref[...] = jnp.dot(cond, w_ref[...], preferred_element_type=F32, precision=HI) + b_ref[...]


def _adaln(cvec, ada_w, ada_b, layer):
    rows, d = cvec.shape
    return pl.pallas_call(
        _adaln_kernel,
        out_shape=jax.ShapeDtypeStruct((rows, 6 * d), F32),
        grid=(6,),
        in_specs=[pl.BlockSpec((rows, d), lambda j: (0, 0)),
                  pl.BlockSpec((None, d, d), lambda j: (layer, 0, j)),
                  pl.BlockSpec((None, 1, d), lambda j: (layer, 0, j))],
        out_specs=pl.BlockSpec((rows, d), lambda j: (0, j)),
        compiler_params=_cparams(("arbitrary",)),
        name="adaln",
    )(cvec, ada_w, ada_b)


def _inproj_kernel(x_ref, mod_ref, g_ref, w_ref, cos_ref, sin_ref, pa_ref, pb_ref, pc_ref):
    d = x_ref.shape[1]
    shift = mod_ref[0, :, 0:d]
    scale = mod_ref[0, :, d:2 * d]
    h = _rms(x_ref[...]) * g_ref[0:1, :] * (1.0 + scale) + shift
    hb = h.astype(BF16)
    pc = jnp.dot(hb, w_ref[:, PA_W + PB_W:W_IN_PAD], preferred_element_type=F32)
    pa_ref[...] = jnp.dot(hb, w_ref[:, 0:PA_W], preferred_element_type=F32)
    pb_ref[...] = jnp.dot(hb, w_ref[:, PA_W:PA_W + PB_W], preferred_element_type=F32)
    q = pc[:, 0:C_W]
    k = pc[:, C_W:2 * C_W] * ATTN_KEY_SCALE
    cos = cos_ref[...]
    sin = sin_ref[...]
    lane = lax.broadcasted_iota(jnp.int32, q.shape, 1)
    first = (lane % 32) < 16

    def rope(t):
        partner = jnp.where(first, pltpu.roll(t, C_W - 16, 1), pltpu.roll(t, 16, 1))
        return t * cos + partner * sin

    pc_ref[:, 0:C_W] = q.astype(BF16)
    pc_ref[:, C_W:2 * C_W] = rope(q).astype(BF16)
    pc_ref[:, 2 * C_W:3 * C_W] = rope(k).astype(BF16)
    pc_ref[:, 3 * C_W:4 * C_W] = pc[:, 2 * C_W:3 * C_W].astype(BF16)


def _inproj(xy, mod, norms, w_in_p, cos, sin, layer, tiles_per_batch, n_ctx_tiles, n_batch):
    t_all, d = xy.shape
    n_tiles = t_all // ROW_TILE

    def mod_row(i):
        return jnp.where(i % tiles_per_batch < n_ctx_tiles, n_batch, i // tiles_per_batch)

    return pl.pallas_call(
        _inproj_kernel,
        out_shape=(jax.ShapeDtypeStruct((t_all, PA_W), F32),
                   jax.ShapeDtypeStruct((t_all, PB_W), F32),
                   jax.ShapeDtypeStruct((t_all, PC_OUT), BF16)),
        grid=(n_tiles,),
        in_specs=[pl.BlockSpec((ROW_TILE, d), lambda i: (i, 0)),
                  pl.BlockSpec((1, 1, 6 * d), lambda i: (mod_row(i), 0, 0)),
                  pl.BlockSpec((None, 4, d), lambda i: (layer, 0, 0)),
                  pl.BlockSpec((None, d, W_IN_PAD), lambda i: (layer, 0, 0)),
                  pl.BlockSpec((ROW_TILE, C_W), lambda i: (i % tiles_per_batch, 0)),
                  pl.BlockSpec((ROW_TILE, C_W), lambda i: (i % tiles_per_batch, 0))],
        out_specs=(pl.BlockSpec((ROW_TILE, PA_W), lambda i: (i, 0)),
                   pl.BlockSpec((ROW_TILE, PB_W), lambda i: (i, 0)),
                   pl.BlockSpec((ROW_TILE, PC_OUT), lambda i: (i, 0))),
        compiler_params=_cparams(("arbitrary",)),
        name="inproj",
    )(xy, mod, norms, w_in_p, cos, sin)


def _hgrn_kernel(pa_ref, lb_ref, gain_ref, o_ref, st_ref, ob_ref, *, n_ctx, n_all, n_b):
    c_sz, w = CHUNK, A_W
    same_head = (lax.broadcasted_iota(jnp.int32, (w, w), 0) // HEAD_V
                 == lax.broadcasted_iota(jnp.int32, (w, w), 1) // HEAD_V)
    head_ones = same_head.astype(BF16)
    assert c_sz == HEAD_V
    t_i = lax.broadcasted_iota(jnp.int32, (c_sz, w), 0)
    s_i = lax.broadcasted_iota(jnp.int32, (c_sz, w), 1) % c_sz
    t_row = lax.broadcasted_iota(jnp.int32, (c_sz, 1), 0)
    row8 = t_row % 8

    def grouped_rows(a, k):
        return jnp.concatenate(
            [jnp.broadcast_to(a[8 * j + k:8 * j + k + 1, :], (8, w)) for j in range(c_sz // 8)], axis=0)

    def halving_levels(rev):
        out = []
        b = c_sz // 2
        while b >= 1:
            def later(i):
                return ((i % (2 * b)) < b) if rev else ((i % (2 * b)) >= b)
            live = jnp.logical_and(t_i // (2 * b) == s_i // (2 * b),
                                   jnp.logical_and(later(t_i), jnp.logical_not(later(s_i))))
            out.append((b, live.astype(F32), jnp.where(later(t_row), 1.0, -1.0)))
            b //= 2
        return out

    n_seg = 2 * n_b
    seg_rev = [seg % 2 == 1 for seg in range(n_seg)]

    def seg_rows(seg):
        return slice(seg * c_sz, (seg + 1) * c_sz)

    def stack(parts):
        return jnp.concatenate(parts, axis=0)

    def rows_of(rows):
        return stack([jnp.broadcast_to(row, (c_sz, w)) for row in rows])

    self_mask = stack([(s_i == t_i).astype(F32)] * n_seg)
    levels = [(b, stack([live_b if rev else live_f for rev in seg_rev]),
               stack([sign_b if rev else sign_f for rev in seg_rev]))
              for (b, live_f, sign_f), (_, live_b, sign_b) in zip(halving_levels(False), halving_levels(True))]
    lb_all = rows_of([lb_ref[1:2, :] if rev else lb_ref[0:1, :] for rev in seg_rev])
    r_all = lax.broadcasted_iota(jnp.int32, (n_seg * c_sz, n_seg * c_sz), 0)
    c_all = lax.broadcasted_iota(jnp.int32, (n_seg * c_sz, n_seg * c_sz), 1)
    rev_row = (r_all // c_sz) % 2 == 1
    tri = jnp.logical_and(r_all // c_sz == c_all // c_sz, jnp.logical_or(
        jnp.logical_and(jnp.logical_not(rev_row), c_all <= r_all),
        jnp.logical_and(rev_row, c_all >= r_all))).astype(BF16)
    zero = jnp.zeros((), BF16)

    def boundary_rows(cum, b, rev):
        if b >= 8:
            return jnp.concatenate(
                [jnp.broadcast_to(cum[r0 + (b if rev else b - 1):r0 + (b if rev else b - 1) + 1, :], (2 * b, w))
                 for r0 in range(0, c_sz, 2 * b)], axis=0)
        ref = None
        for g in reversed(range(8 // (2 * b))):
            cand = grouped_rows(cum, g * 2 * b + (b if rev else b - 1))
            ref = cand if ref is None else jnp.where(row8 < (g + 1) * 2 * b, cand, ref)
        return ref

    def scan_step(n, carry):
        c_f = n
        c_b = jnp.where(n < n_ctx, n_ctx - 1 - n, n_all - 1 - (n - n_ctx))
        chunk_rows = [pl.ds(pl.multiple_of((c_b if rev else c_f) * c_sz, c_sz), c_sz) for rev in seg_rev]
        q_pre = stack([pa_ref[seg // 2, chunk_rows[seg], 0:A_W] for seg in range(n_seg)])
        v_l = [pa_ref[seg // 2, chunk_rows[seg], A_W:2 * A_W].astype(BF16) for seg in range(n_seg)]
        f_pre = stack([pa_ref[seg // 2, chunk_rows[seg], (3 if rev else 2) * A_W:(4 if rev else 3) * A_W]
                       for seg, rev in enumerate(seg_rev)])
        q = _silu(q_pre)
        f = lb_all + (1.0 - lb_all) * jax.nn.sigmoid(f_pre)
        log_f = jnp.log(jnp.maximum(f, F_MIN))
        kk = (1.0 - lb_all) * jax.nn.sigmoid(-f_pre)
        cum = sum(jnp.dot(tri, piece, preferred_element_type=F32) for piece in _split3(log_f))
        ends = [cum[seg * c_sz + (0 if rev else c_sz - 1):seg * c_sz + (0 if rev else c_sz - 1) + 1, :]
                for seg, rev in enumerate(seg_rev)]

        st_l = [st_ref[seg] for seg in range(n_seg)]
        q_in = (q * jnp.exp(cum)).astype(BF16)
        o = stack([_nt(q_in[seg_rows(seg)], st_l[seg].astype(BF16)) for seg in range(n_seg)])
        k_end = (kk * jnp.exp(rows_of(ends) - cum)).astype(BF16)
        for seg in range(n_seg):
            st_ref[seg] = (st_l[seg] * jnp.exp(ends[seg])
                           + jnp.where(same_head, _tn(v_l[seg], k_end[seg_rows(seg)]), 0.0))

        def scores(qa, ka):
            return _nt(qa, jnp.where(same_head, jnp.concatenate([ka] * HEADS, axis=0), zero))

        refs = [stack([boundary_rows(cum[seg_rows(seg)], b, rev) for seg, rev in enumerate(seg_rev)])
                for b, _, _ in levels]
        zs = [jnp.exp((cum - ref) * sign)
              for ref, (_, _, sign) in zip(refs, levels)]
        qas = [(q * z).astype(BF16) for z in zs]
        kas = [(kk * z).astype(BF16) for z in zs]
        prods = [stack([scores(qa[seg_rows(seg)], ka[seg_rows(seg)]) for seg in range(n_seg)])
                 for qa, ka in zip(qas, kas)]
        p = self_mask * jnp.dot((q * kk).astype(BF16), head_ones, preferred_element_type=F32)
        for prod, (_, live, _) in zip(prods, levels):
            p = p + live * prod
        pb = p.astype(BF16)

        for seg, rev in enumerate(seg_rev):
            out = o[seg_rows(seg)] + jnp.dot(
                pb[seg_rows(seg)], jnp.where(same_head, jnp.concatenate([v_l[seg]] * HEADS, axis=0), zero),
                preferred_element_type=F32)
            if rev:
                ob_ref[seg // 2, chunk_rows[seg], :] = out
            else:
                o_ref[seg // 2, chunk_rows[seg], :] = out
        return carry

    st_ref[...] = jnp.zeros_like(st_ref)
    lax.fori_loop(0, n_all, scan_step, 0)

    def readout(c, carry):
        rows = pl.ds(pl.multiple_of(c * ROW_TILE, ROW_TILE), ROW_TILE)
        for b in range(n_b):
            tot = o_ref[b, rows, :] + ob_ref[b, rows, :]
            ms = sum(jnp.dot(piece, head_ones, preferred_element_type=F32)
                     for piece in _split3(tot * tot)) * (1.0 / HEAD_V)
            g = pa_ref[b, rows, 4 * A_W:5 * A_W]
            o_ref[b, rows, :] = tot * lax.rsqrt(ms + NORM_EPS) * gain_ref[...] * _silu(g)
        return carry

    lax.fori_loop(0, n_all * c_sz // ROW_TILE, readout, 0)


def _hgrn(pa3, lb, gain, n_ctx_rows):
    n_batch, rows, _ = pa3.shape
    n_b = SCAN_BATCHES if n_batch % SCAN_BATCHES == 0 else 1
    kern = functools.partial(_hgrn_kernel, n_ctx=n_ctx_rows // CHUNK, n_all=rows // CHUNK, n_b=n_b)
    return pl.pallas_call(
        kern,
        out_shape=jax.ShapeDtypeStruct((n_batch, rows, A_W), F32),
        grid=(n_batch // n_b,),
        in_specs=[pl.BlockSpec((n_b, rows, PA_W), lambda b: (b, 0, 0), pipeline_mode=pl.Buffered(1)),
                  pl.BlockSpec((2, A_W), lambda b: (0, 0)),
                  pl.BlockSpec((1, A_W), lambda b: (0, 0))],
        out_specs=pl.BlockSpec((n_b, rows, A_W), lambda b: (b, 0, 0)),
        scratch_shapes=[pltpu.VMEM((2 * n_b, A_W, A_W), F32), pltpu.VMEM((n_b, rows, A_W), F32)],
        compiler_params=_cparams(("arbitrary",)),
        name="hgrn2",
    )(pa3, lb, gain)


def _split3(x):
    hi = x.astype(BF16)
    rest = x - hi.astype(F32)
    mid = rest.astype(BF16)
    return hi, mid, (rest - mid.astype(F32)).astype(BF16)


def _mlstm_kernel(pb_ref, bias_ref, gain_ref, o_ref, cn_ref, ob_ref, *, n_ctx, n_all, n_b):
    assert CHUNK == HEAD_V
    c_sz, w, qk_w = CHUNK, B_W, HEADS * B_QK
    t_i = lax.broadcasted_iota(jnp.int32, (c_sz, w), 0)
    s_i = lax.broadcasted_iota(jnp.int32, (c_sz, w), 1) % c_sz
    diag4 = s_i == t_i
    half_lane = lax.broadcasted_iota(jnp.int32, (1, LANES), 1) < HEAD_V
    k_block = (lax.broadcasted_iota(jnp.int32, (HEADS * c_sz, qk_w), 0) // c_sz
               == lax.broadcasted_iota(jnp.int32, (HEADS * c_sz, qk_w), 1) // B_QK)
    v_block = (lax.broadcasted_iota(jnp.int32, (HEADS * c_sz, w), 0) // c_sz
               == lax.broadcasted_iota(jnp.int32, (HEADS * c_sz, w), 1) // HEAD_V)
    state_block = (lax.broadcasted_iota(jnp.int32, (qk_w, 2 * w), 0) // B_QK
                   == (lax.broadcasted_iota(jnp.int32, (qk_w, 2 * w), 1) % w) // HEAD_V)
    head_ones = v_block.astype(BF16)
    q_off, k_off, v_off, o_off, g_off = 0, qk_w, 2 * qk_w, 2 * qk_w + B_W, 2 * qk_w + 2 * B_W

    sel_r = lax.broadcasted_iota(jnp.int32, (2 * LANES, 2 * w), 0)
    sel_c = lax.broadcasted_iota(jnp.int32, (2 * LANES, 2 * w), 1)
    sel = (sel_r == jnp.where(sel_c < w, HEADS + sel_c // HEAD_V, LANES + (sel_c - w) // HEAD_V)).astype(BF16)

    def head_max(x):
        outs = []
        for col in range(w // LANES):
            xc = x[:, col * LANES:(col + 1) * LANES]
            lo = jnp.max(jnp.where(half_lane, xc, -jnp.inf), axis=-1, keepdims=True)
            hi = jnp.max(jnp.where(half_lane, -jnp.inf, xc), axis=-1, keepdims=True)
            outs.append(jnp.where(half_lane, lo, hi))
        return jnp.concatenate(outs, axis=1)

    n_seg = 2 * n_b
    seg_rev = [seg % 2 == 1 for seg in range(n_seg)]

    def seg_rows(seg):
        return slice(seg * c_sz, (seg + 1) * c_sz)

    mask = jnp.concatenate([(s_i >= t_i) if rev else (s_i <= t_i) for rev in seg_rev], axis=0)
    r_all = lax.broadcasted_iota(jnp.int32, (n_seg * c_sz, n_seg * c_sz), 0)
    c_all = lax.broadcasted_iota(jnp.int32, (n_seg * c_sz, n_seg * c_sz), 1)
    same_seg = r_all // c_sz == c_all // c_sz
    rev_row = (r_all // c_sz) % 2 == 1
    tri = jnp.logical_and(same_seg, jnp.logical_or(
        jnp.logical_and(jnp.logical_not(rev_row), c_all <= r_all),
        jnp.logical_and(rev_row, c_all >= r_all))).astype(BF16)

    def per_seg(fn, x):
        return jnp.concatenate([jnp.broadcast_to(fn(x[seg_rows(seg)]), (c_sz, w)) for seg in range(n_seg)],
                               axis=0)

    def rows_of(rows):
        return jnp.concatenate([jnp.broadcast_to(row, (c_sz, w)) for row in rows], axis=0)

    def scan_step(n, m_prev_rows):
        c_f = n
        c_b = jnp.where(n < n_ctx, n_ctx - 1 - n, n_all - 1 - (n - n_ctx))
        chunk_rows = [pl.ds(pl.multiple_of((c_b if rev else c_f) * c_sz, c_sz), c_sz) for rev in seg_rev]

        def load(lo, hi):
            return [pb_ref[seg // 2, chunk_rows[seg], lo:hi] for seg in range(n_seg)]

        q_l = load(q_off, q_off + qk_w)
        k_l = load(k_off, k_off + qk_w)
        v_l = load(v_off, v_off + B_W)
        g_l = load(g_off, g_off + LANES)
        gts = jnp.concatenate(
            [pltpu.roll(g + bias_ref[...], LANES - 2 * HEADS, 1) if rev else g + bias_ref[...]
             for g, rev in zip(g_l, seg_rev)], axis=0)
        log_f = jnp.minimum(gts, 0.0) - jnp.log(1.0 + jnp.exp(-jnp.abs(gts)))
        cum_col = sum(jnp.dot(tri, p, preferred_element_type=F32) for p in _split3(log_f))
        both = sum(jnp.dot(p, sel, preferred_element_type=F32)
                   for p in _split3(jnp.concatenate([cum_col, gts], axis=1)))
        cum_t = both[:, :w]
        ig_t = both[:, w:]
        src = per_seg(lambda x: jnp.sum(jnp.where(diag4, x, 0.0), axis=0, keepdims=True), cum_t - ig_t)
        m_prev = rows_of(m_prev_rows)
        log_d = jnp.where(mask, cum_t - src, MASK_NEG)
        log_inter = cum_t + m_prev
        m_t = jnp.maximum(log_inter, head_max(log_d))
        w_intra = jnp.where(mask, jnp.exp(log_d - m_t), 0.0)
        w_inter = jnp.exp(log_inter - m_t)

        scale = B_QK ** -0.5
        qb_l = [(q * scale).astype(BF16) for q in q_l]
        kb_l = [k.astype(BF16) for k in k_l]
        zero = jnp.zeros((), BF16)
        scores = [_nt(qb, jnp.where(k_block, jnp.concatenate([kb] * HEADS, axis=0), zero))
                  for qb, kb in zip(qb_l, kb_l)]
        p = (jnp.concatenate(scores, axis=0) * w_intra).astype(BF16)

        cn_l = [cn_ref[seg] for seg in range(n_seg)]
        inter = jnp.concatenate([jnp.dot(qb, cn.astype(BF16), preferred_element_type=F32)
                                 for qb, cn in zip(qb_l, cn_l)], axis=0)
        values = [jnp.dot(p[seg_rows(seg)],
                          jnp.where(v_block, jnp.concatenate([v_l[seg].astype(BF16)] * HEADS, axis=0), zero),
                          preferred_element_type=F32) for seg in range(n_seg)]
        num = w_inter * inter[:, :w] + jnp.concatenate(values, axis=0)
        den = w_inter * inter[:, w:] + jnp.dot(p, head_ones, preferred_element_type=F32)
        h_out = num / jnp.maximum(jnp.abs(den), jnp.exp(-m_t))
        for seg, rev in enumerate(seg_rev):
            if rev:
                ob_ref[seg // 2, chunk_rows[seg], :] = h_out[seg_rows(seg)]
            else:
                o_ref[seg // 2, chunk_rows[seg], :] = h_out[seg_rows(seg)]

        ends = [cum_t[seg * c_sz + (0 if rev else c_sz - 1):seg * c_sz + (0 if rev else c_sz - 1) + 1, :]
                for seg, rev in enumerate(seg_rev)]
        log_end = rows_of(ends) - cum_t + ig_t
        m_ends = [jnp.maximum(end + m_old, jnp.max(log_end[seg_rows(seg)], axis=0, keepdims=True))
                  for seg, (end, m_old) in enumerate(zip(ends, m_prev_rows))]
        w_end = jnp.exp(log_end - rows_of(m_ends))
        for seg in range(n_seg):
            w_seg = w_end[seg_rows(seg)]
            upd = _tn(kb_l[seg], jnp.concatenate([(w_seg * v_l[seg]).astype(BF16), w_seg.astype(BF16)], axis=1))
            carry_w = jnp.exp(ends[seg] + m_prev_rows[seg] - m_ends[seg])
            cn_ref[seg] = (jnp.concatenate([carry_w, carry_w], axis=1) * cn_l[seg]
                           + jnp.where(state_block, upd, 0.0))
        return tuple(m_ends)

    cn_ref[...] = jnp.zeros_like(cn_ref)
    lax.fori_loop(0, n_all, scan_step, tuple(jnp.zeros((1, w), F32) for _ in range(n_seg)))

    def readout(c, carry):
        rows = pl.ds(pl.multiple_of(c * ROW_TILE, ROW_TILE), ROW_TILE)
        for b in range(n_b):
            tot = o_ref[b, rows, :] + ob_ref[b, rows, :]
            normed = jnp.concatenate(
                [_rms(tot[:, h * HEAD_V:(h + 1) * HEAD_V]) for h in range(HEADS)], axis=1) * gain_ref[...]
            og = pb_ref[b, rows, o_off:o_off + B_W]
            o_ref[b, rows, :] = normed * jax.nn.sigmoid(og)
        return carry

    lax.fori_loop(0, n_all * c_sz // ROW_TILE, readout, 0)


def _mlstm(pb3, bias, gain, n_ctx_rows):
    n_batch, rows, _ = pb3.shape
    n_b = SCAN_BATCHES if n_batch % SCAN_BATCHES == 0 else 1
    kern = functools.partial(_mlstm_kernel, n_ctx=n_ctx_rows // CHUNK, n_all=rows // CHUNK, n_b=n_b)
    return pl.pallas_call(
        kern,
        out_shape=jax.ShapeDtypeStruct((n_batch, rows, B_W), F32),
        grid=(n_batch // n_b,),
        in_specs=[pl.BlockSpec((n_b, rows, PB_W), lambda b: (b, 0, 0)),
                  pl.BlockSpec((1, LANES), lambda b: (0, 0)),
                  pl.BlockSpec((1, B_W), lambda b: (0, 0))],
        out_specs=pl.BlockSpec((n_b, rows, B_W), lambda b: (b, 0, 0)),
        scratch_shapes=[pltpu.VMEM((2 * n_b, HEADS * B_QK, 2 * B_W), F32), pltpu.VMEM((n_b, rows, B_W), F32)],
        compiler_params=_cparams(("arbitrary",)),
        name="mlstm",
    )(pb3, bias, gain)


def _attn_kernel(lam_ref, qp_ref, qr_ref, k_ref, v_ref, gain_ref, o_ref, *, n_ctx, q_tile0, lam_init):
    lam = lam_ref[0]
    q_tile = pl.program_id(2) + q_tile0
    lane = lax.broadcasted_iota(jnp.int32, (1, 2 * C_DQK), 1)

    n_all = k_ref.shape[1]

    def finish(head, parts):
        o = parts[0] - lam * parts[1]
        o_ref[0, :, head] = _rms(o) * gain_ref[...] * (1.0 - lam_init)

    def sub_query(q, j):
        return jnp.where(lane // C_DQK == j, q, jnp.zeros_like(q))

    def row_max(s):
        return jnp.max(s, axis=-1, keepdims=True)

    def row_sum(s):
        return jnp.sum(s, axis=-1, keepdims=True)

    def pv(ex, v):
        return jnp.dot(ex.astype(BF16), v, preferred_element_type=F32)

    heads = [slice(hh * C_DV, (hh + 1) * C_DV) for hh in range(ATTN_HEADS_PER_STEP)]

    def attend(pieces):
        problems = [(head, j) for head in heads for j in range(2)]
        scores = [[_nt(sub_query(q_ref[0, :, head], j), k_ref[0, k0:k1, head]) for q_ref, k0, k1 in pieces]
                  for head, j in problems]
        maxes = [functools.reduce(jnp.maximum, [row_max(s) for s in ss]) for ss in scores]
        exps = [[jnp.exp2(s - m) for s in ss] for ss, m in zip(scores, maxes)]
        outs = [sum(pv(e, v_ref[0, k0:k1, head]) for e, (_, k0, k1) in zip(es, pieces))
                / sum(row_sum(e) for e in es) for es, (head, _) in zip(exps, problems)]
        for n, head in enumerate(heads):
            finish(head, outs[2 * n:2 * n + 2])

    @pl.when(q_tile * ROW_TILE < n_ctx)
    def _():
        attend([(qp_ref, 0, n_ctx)])

    @pl.when(q_tile * ROW_TILE >= n_ctx)
    def _():
        attend([(qp_ref, 0, n_ctx), (qr_ref, n_ctx, n_all)])


def _attention(pc3, lam, gain, n_ctx_rows, lam_init, with_ctx):
    n_batch, rows, _ = pc3.shape
    q_tile0 = 0 if with_ctx else n_ctx_rows // ROW_TILE
    n_q = rows // ROW_TILE - q_tile0
    gw = ATTN_HEADS_PER_STEP * C_DV
    hb = C_W // gw
    kern = functools.partial(_attn_kernel, n_ctx=n_ctx_rows, q_tile0=q_tile0, lam_init=lam_init)
    grid_spec = pltpu.PrefetchScalarGridSpec(
        num_scalar_prefetch=1,
        grid=(n_batch, hb, n_q),
        in_specs=[pl.BlockSpec((1, ROW_TILE, gw), lambda b, h, i, lam: (b, i + q_tile0, h)),
                  pl.BlockSpec((1, ROW_TILE, gw), lambda b, h, i, lam: (b, i + q_tile0, hb + h)),
                  pl.BlockSpec((1, rows, gw), lambda b, h, i, lam: (b, 0, 2 * hb + h)),
                  pl.BlockSpec((1, rows, gw), lambda b, h, i, lam: (b, 0, 3 * hb + h)),
                  pl.BlockSpec((1, C_DV), lambda b, h, i, lam: (0, 0))],
        out_specs=pl.BlockSpec((1, ROW_TILE, gw), lambda b, h, i, lam: (b, i, h)),
    )
    return pl.pallas_call(
        kern,
        out_shape=jax.ShapeDtypeStruct((n_batch, n_q * ROW_TILE, C_W), F32),
        grid_spec=grid_spec,
        compiler_params=_cparams(("arbitrary", "arbitrary", "arbitrary")),
        name="diff_attn",
    )(lam, pc3, pc3, pc3, pc3, gain)


def _outproj_kernel(oa_ref, ob_ref, oc_ref, x_ref, mod_ref, n_ref, w_ref, rw_ref, rb_ref,
                    xn_ref, h_ref, route_ref, cnt_ref, carry_ref):
    d = x_ref.shape[1]
    n_exp = rw_ref.shape[1]

    @pl.when(pl.program_id(0) == 0)
    def _():
        carry_ref[...] = jnp.zeros_like(carry_ref)

    n_rows = x_ref.shape[0]
    halves = [slice(g * n_rows // ROW_GROUPS, (g + 1) * n_rows // ROW_GROUPS) for g in range(ROW_GROUPS)]
    mixes = [jnp.dot(oa_ref[hs, :].astype(BF16), w_ref[0:A_W, :], preferred_element_type=F32)
             + jnp.dot(ob_ref[hs, :].astype(BF16), w_ref[A_W:A_W + B_W, :], preferred_element_type=F32)
             + jnp.dot(oc_ref[hs, :].astype(BF16), w_ref[A_W + B_W:A_W + B_W + C_W, :],
                       preferred_element_type=F32) for hs in halves]
    xns = [x_ref[hs, :] + mod_ref[0, :, 2 * d:3 * d] * (_rms(mix) * n_ref[1:2, :])
           for hs, mix in zip(halves, mixes)]
    hiddens = [_rms(xn) * n_ref[2:3, :] * (1.0 + mod_ref[0, :, 4 * d:5 * d]) + mod_ref[0, :, 3 * d:4 * d]
               for xn in xns]
    for hs, xn, h in zip(halves, xns, hiddens):
        xn_ref[hs, :] = xn
        h_ref[hs, :] = _pack_bf16_pairs(h)

    h_his = [h.astype(BF16) for h in hiddens]
    h_mids = [(h - h_hi.astype(F32)).astype(BF16) for h, h_hi in zip(hiddens, h_his)]
    logits = jnp.concatenate([_nt(rw_ref[0], h_hi) + _nt(rw_ref[1], h_hi) + _nt(rw_ref[0], h_mid)
                              for h_hi, h_mid in zip(h_his, h_mids)], axis=1) + rb_ref[...]
    e_sub = lax.broadcasted_iota(jnp.int32, logits.shape, 0)
    cur = logits
    picks, vals = [], []
    for _ in range(TOP_K):
        mx = jnp.max(cur, axis=0, keepdims=True)
        idx = jnp.min(jnp.where(cur == mx, e_sub, n_exp), axis=0, keepdims=True)
        hit = e_sub == idx
        cur = jnp.where(hit, -jnp.inf, cur)
        picks.append((idx, hit.astype(F32)))
        vals.append(mx)
    exps = [jnp.exp(vv - vals[0]) for vv in vals]
    total = exps[0] + exps[1] + exps[2] + exps[3]

    chosen = picks[0][1] + picks[1][1] + picks[2][1] + picks[3][1]
    tm = logits.shape[1]
    before = (lax.broadcasted_iota(jnp.int32, (tm, tm), 0) < lax.broadcasted_iota(jnp.int32, (tm, tm), 1))
    seen = jnp.dot(chosen.astype(BF16), before.astype(BF16), preferred_element_type=F32) + carry_ref[...]
    new_carry = carry_ref[...] + jnp.sum(chosen, axis=1, keepdims=True)
    carry_ref[...] = new_carry
    cnt_ref[...] = new_carry

    o_row = lax.broadcasted_iota(jnp.int32, route_ref.shape, 0)
    route = jnp.zeros(route_ref.shape, F32)
    for kk in range(TOP_K):
        idx, hit = picks[kk]
        rank = jnp.sum(hit * seen, axis=0, keepdims=True)
        route = (route + jnp.where(o_row == kk, idx.astype(F32), 0.0)
                 + jnp.where(o_row == TOP_K + kk, exps[kk] / total, 0.0)
                 + jnp.where(o_row == 2 * TOP_K + kk, rank, 0.0))
    route_ref[...] = route


ROUTE_ROWS = 16


def _outproj(oa, ob, oc, xy, mod, norms, w_out_b, router_wt, router_b, layer, n_tiles, in_tile, oc_tile, mod_row):
    d = xy.shape[1]
    n_exp = router_wt.shape[2]
    t_out = n_tiles * ROW_TILE
    xn, h, route_t, counts = pl.pallas_call(
        _outproj_kernel,
        out_shape=(jax.ShapeDtypeStruct((t_out, d), F32),
                   jax.ShapeDtypeStruct((t_out, d // 2), jnp.int32),
                   jax.ShapeDtypeStruct((n_tiles * ROUTE_ROWS, ROW_TILE), F32),
                   jax.ShapeDtypeStruct((n_exp, 1), F32)),
        grid=(n_tiles,),
        in_specs=[pl.BlockSpec((ROW_TILE, A_W), lambda i: (in_tile(i), 0)),
                  pl.BlockSpec((ROW_TILE, B_W), lambda i: (in_tile(i), 0)),
                  pl.BlockSpec((ROW_TILE, C_W), lambda i: (oc_tile(i), 0)),
                  pl.BlockSpec((ROW_TILE, d), lambda i: (in_tile(i), 0)),
                  pl.BlockSpec((1, 1, 6 * d), lambda i: (mod_row(i), 0, 0)),
                  pl.BlockSpec((None, 4, d), lambda i: (layer, 0, 0)),
                  pl.BlockSpec((None, d, d), lambda i: (layer, 0, 0)),
                  pl.BlockSpec((None, 2, n_exp, d), lambda i: (layer, 0, 0, 0)),
                  pl.BlockSpec((None, n_exp, 1), lambda i: (layer, 0, 0))],
        out_specs=(pl.BlockSpec((ROW_TILE, d), lambda i: (i, 0)),
                   pl.BlockSpec((ROW_TILE, d // 2), lambda i: (i, 0)),
                   pl.BlockSpec((ROUTE_ROWS, ROW_TILE), lambda i: (i, 0)),
                   pl.BlockSpec((n_exp, 1), lambda i: (0, 0))),
        scratch_shapes=[pltpu.VMEM((n_exp, 1), F32)],
        compiler_params=_cparams(("arbitrary",)),
        name="outproj_router",
    )(oa, ob, oc, xy, mod, norms, w_out_b, router_wt, router_b)
    return xn, h, route_t.reshape(n_tiles, ROUTE_ROWS, ROW_TILE), counts[:, 0]


PAIR_BLOCK = 2 * LANES


def _moe_kernel(te_ref, first_ref, valid_ref, nu_ref, slot_ref, next_ref, xs_ref, w1_hbm, b1_ref, w2_hbm, b2_ref,
                ys_ref, w1p_ref, w2b_ref, w1_buf, w2_buf, w_sem, *, layer):
    i = pl.program_id(0)
    two_f = w1_buf.shape[2]
    n_blk = two_f // PAIR_BLOCK

    def fetch(expert, slot):
        return (pltpu.make_async_copy(w1_hbm.at[layer, expert], w1_buf.at[slot], w_sem.at[0, slot]),
                pltpu.make_async_copy(w2_hbm.at[layer, expert], w2_buf.at[slot], w_sem.at[1, slot]))

    @pl.when(i == 0)
    def _():
        for copy in fetch(te_ref[0], 0):
            copy.start()

    @pl.when(jnp.logical_and(i < nu_ref[0], first_ref[i] == 1))
    def _():
        slot = slot_ref[i]
        for copy in fetch(te_ref[i], slot):
            copy.wait()
        r = lax.broadcasted_iota(jnp.int32, (PAIR_BLOCK, PAIR_BLOCK), 0)
        c = lax.broadcasted_iota(jnp.int32, (PAIR_BLOCK, PAIR_BLOCK), 1)
        perm = (r == jnp.where(c < LANES, 2 * c, 2 * (c - LANES) + 1)).astype(BF16)
        for blk in range(n_blk):
            cols = slice(blk * PAIR_BLOCK, (blk + 1) * PAIR_BLOCK)
            w1p_ref[:, cols] = jnp.dot(w1_buf[slot, :, cols].astype(BF16), perm,
                                       preferred_element_type=F32).astype(BF16)
        w2b_ref[...] = w2_buf[slot].astype(BF16)

        @pl.when(next_ref[i] >= 0)
        def _():
            for copy in fetch(next_ref[i], 1 - slot):
                copy.start()

    tile_rows = xs_ref.shape[0]
    used = i < nu_ref[0]
    valid = valid_ref[i]

    def expert_ffn(n_rows):
        x = _unpack_bf16_pairs(xs_ref[0:n_rows, :]).astype(BF16)
        row = lax.broadcasted_iota(jnp.int32, (n_rows, 1), 0)
        x = jnp.where(row < valid, x, jnp.zeros_like(x))
        hid = jnp.dot(x, w1p_ref[...], preferred_element_type=F32) + b1_ref[...]
        acts = []
        for blk in range(n_blk):
            glu = jnp.minimum(hid[:, blk * PAIR_BLOCK:blk * PAIR_BLOCK + LANES], SWIGLU_LIMIT)
            lin = jnp.clip(hid[:, blk * PAIR_BLOCK + LANES:(blk + 1) * PAIR_BLOCK], -SWIGLU_LIMIT, SWIGLU_LIMIT)
            acts.append((glu * jax.nn.sigmoid(SWIGLU_ALPHA * glu) * (lin + 1.0)).astype(BF16))
        y = jnp.dot(jnp.concatenate(acts, axis=1), w2b_ref[...], preferred_element_type=F32) + b2_ref[...]
        ys_ref[0:n_rows, :] = _pack_bf16_pairs(y)
        if n_rows < tile_rows:
            ys_ref[n_rows:tile_rows, :] = jnp.zeros((tile_rows - n_rows, ys_ref.shape[1]), ys_ref.dtype)

    @pl.when(jnp.logical_and(used, valid > tile_rows // 2))
    def _():
        expert_ffn(tile_rows)

    @pl.when(jnp.logical_and(used, valid <= tile_rows // 2))
    def _():
        expert_ffn(tile_rows // 2)

    @pl.when(jnp.logical_not(used))
    def _():
        ys_ref[...] = jnp.zeros_like(ys_ref)


def _moe(plan, xs, w1, b1p, w2, b2, layer):
    r_max, half_d = xs.shape
    d = 2 * half_d
    two_f = w1.shape[3]
    f = two_f // 2
    n_tiles = r_max // MOE_TILE

    def row_tile(i, te, fi, va, nu, sl, nx):
        return (jnp.maximum(jnp.minimum(i, nu[0] - 1), 0), 0)

    def bias_block(i, te, fi, va, nu, sl, nx):
        return (layer, te[i], 0, 0)

    grid_spec = pltpu.PrefetchScalarGridSpec(
        num_scalar_prefetch=6,
        grid=(n_tiles,),
        in_specs=[pl.BlockSpec((MOE_TILE, half_d), row_tile),
                  pl.BlockSpec(memory_space=pl.ANY),
                  pl.BlockSpec((None, None, 1, two_f), bias_block),
                  pl.BlockSpec(memory_space=pl.ANY),
                  pl.BlockSpec((None, None, 1, d), bias_block)],
        out_specs=pl.BlockSpec((MOE_TILE, half_d), lambda i, te, fi, va, nu, sl, nx: (i, 0)),
        scratch_shapes=[pltpu.VMEM((d, two_f), BF16), pltpu.VMEM((f, d), BF16),
                        pltpu.VMEM((2, d, two_f), F32), pltpu.VMEM((2, f, d), F32),
                        pltpu.SemaphoreType.DMA((2, 2))],
    )
    return pl.pallas_call(
        functools.partial(_moe_kernel, layer=layer),
        out_shape=jax.ShapeDtypeStruct((r_max, half_d), jnp.int32),
        grid_spec=grid_spec,
        compiler_params=_cparams(("arbitrary",)),
        name="moe_experts",
    )(*plan, xs, w1, b1p, w2, b2)


SC_CORES = 2
SC_SUBCORES = 16
SC_CHUNK = 64


def _dispatch_rows(hp, pos, r_max):
    t, width = hp.shape
    workers = SC_CORES * SC_SUBCORES
    assert t % (workers * SC_CHUNK) == 0
    per_worker = t // (workers * SC_CHUNK)
    idx = pos.reshape(-1, TOP_K, ROW_TILE // SC_CHUNK, SC_CHUNK).transpose(0, 2, 1, 3).reshape(
        workers, per_worker * TOP_K, SC_CHUNK)
    mesh = plsc.VectorSubcoreMesh(core_axis_name="c", subcore_axis_name="s",
                                  num_cores=SC_CORES, num_subcores=SC_SUBCORES)

    @functools.partial(
        pl.kernel, mesh=mesh,
        out_type=jax.ShapeDtypeStruct((r_max, width), hp.dtype),
        scratch_types=[pltpu.VMEM((per_worker * TOP_K, SC_CHUNK), jnp.int32),
                       pltpu.VMEM((2, SC_CHUNK, width), hp.dtype),
                       pltpu.SemaphoreType.DMA((2,)),
                       pltpu.SemaphoreType.DMA((2,))],
    )
    def scatter(hp_hbm, idx_hbm, out_hbm, idx_v, rows_v, read_sem, write_sem):
        wid = lax.axis_index("s") * SC_CORES + lax.axis_index("c")
        pltpu.sync_copy(idx_hbm.at[wid], idx_v)

        def read(j):
            row0 = pl.multiple_of((wid * per_worker + j) * SC_CHUNK, SC_CHUNK)
            return pltpu.make_async_copy(hp_hbm.at[pl.ds(row0, SC_CHUNK)], rows_v.at[j % 2], read_sem.at[j % 2])

        def write(j, k):
            return pltpu.make_async_copy(rows_v.at[j % 2], out_hbm.at[idx_v.at[j * TOP_K + k]],
                                         write_sem.at[j % 2])

        read(0).start()
        for j in range(per_worker):
            read(j).wait()
            for k in range(TOP_K):
                write(j, k).start()
            if j + 1 < per_worker:
                if j >= 1:
                    for k in range(TOP_K):
                        write(j - 1, k).wait()
                read(j + 1).start()
        for j in range(max(per_worker - 2, 0), per_worker):
            for k in range(TOP_K):
                write(j, k).wait()

    return scatter(hp, idx)


def _combine_rows(ysp, pos):
    width = ysp.shape[1]
    t = pos.shape[0] * pos.shape[2]
    workers = SC_CORES * SC_SUBCORES
    assert t % (workers * SC_CHUNK) == 0
    per_worker = t // (workers * SC_CHUNK)
    units = per_worker * TOP_K
    idx = pos.reshape(-1, TOP_K, ROW_TILE // SC_CHUNK, SC_CHUNK).transpose(0, 2, 1, 3).reshape(
        workers, units, SC_CHUNK)
    mesh = plsc.VectorSubcoreMesh(core_axis_name="c", subcore_axis_name="s",
                                  num_cores=SC_CORES, num_subcores=SC_SUBCORES)

    @functools.partial(
        pl.kernel, mesh=mesh,
        out_type=jax.ShapeDtypeStruct((TOP_K, t, width), ysp.dtype),
        scratch_types=[pltpu.VMEM((units, SC_CHUNK), jnp.int32),
                       pltpu.VMEM((2, SC_CHUNK, width), ysp.dtype),
                       pltpu.SemaphoreType.DMA((2,)),
                       pltpu.SemaphoreType.DMA((2,))],
    )
    def gather(ys_hbm, idx_hbm, out_hbm, idx_v, rows_v, read_sem, write_sem):
        wid = lax.axis_index("s") * SC_CORES + lax.axis_index("c")
        pltpu.sync_copy(idx_hbm.at[wid], idx_v)

        def read(u):
            return pltpu.make_async_copy(ys_hbm.at[idx_v.at[u]], rows_v.at[u % 2], read_sem.at[u % 2])

        def write(u):
            row0 = pl.multiple_of((wid * per_worker + u // TOP_K) * SC_CHUNK, SC_CHUNK)
            return pltpu.make_async_copy(rows_v.at[u % 2], out_hbm.at[u % TOP_K, pl.ds(row0, SC_CHUNK)],
                                         write_sem.at[u % 2])

        read(0).start()
        for u in range(units):
            read(u).wait()
            write(u).start()
            if u + 1 < units:
                if u >= 1:
                    write(u - 1).wait()
                read(u + 1).start()
        for u in range(max(units - 2, 0), units):
            write(u).wait()

    return gather(ysp, idx)


def _ffn_residual_kernel(x_ref, w_ref, y0_ref, y1_ref, y2_ref, y3_ref, mod_ref, n_ref, o_ref):
    d = x_ref.shape[1]
    wts = w_ref[...]
    ffn = None
    for k, y_ref in enumerate((y0_ref, y1_ref, y2_ref, y3_ref)):
        term = wts[:, k:k + 1] * _unpack_bf16_pairs(y_ref[...])
        ffn = term if ffn is None else ffn + term
    o_ref[...] = x_ref[...] + mod_ref[0, :, 5 * d:6 * d] * (_rms(ffn) * n_ref[3:4, :])


def _ffn_residual(xn, wts, ys_by_k, mod, norms, layer, mod_row):
    t, d = xn.shape
    row_spec = pl.BlockSpec((ROW_TILE, d), lambda i: (i, 0))

    def y_spec(k):
        return pl.BlockSpec((None, ROW_TILE, d // 2), lambda i: (k, i, 0))

    return pl.pallas_call(
        _ffn_residual_kernel,
        out_shape=jax.ShapeDtypeStruct((t, d), F32),
        grid=(t // ROW_TILE,),
        in_specs=[row_spec,
                  pl.BlockSpec((ROW_TILE, TOP_K), lambda i: (i, 0)),
                  y_spec(0), y_spec(1), y_spec(2), y_spec(3),
                  pl.BlockSpec((1, 1, 6 * d), lambda i: (mod_row(i), 0, 0)),
                  pl.BlockSpec((None, 4, d), lambda i: (layer, 0, 0))],
        out_specs=row_spec,
        compiler_params=_cparams(("arbitrary",)),
        name="ffn_residual",
    )(xn, wts, ys_by_k, ys_by_k, ys_by_k, ys_by_k, mod, norms)


def _dispatch_plan(route_t, counts, r_max):
    n_row_tiles = route_t.shape[0]
    idx = route_t[:, 0:TOP_K, :].astype(jnp.int32)
    wts = route_t[:, TOP_K:2 * TOP_K, :].transpose(0, 2, 1).reshape(n_row_tiles * ROW_TILE, TOP_K)
    rank = route_t[:, 2 * TOP_K:3 * TOP_K, :].astype(jnp.int32)
    cnt = counts.astype(jnp.int32)
    padded = (cnt + MOE_TILE - 1) // MOE_TILE * MOE_TILE
    ends = jnp.cumsum(padded)
    starts = ends - padded
    pos = rank
    for e in range(cnt.shape[0]):
        pos = pos + jnp.where(idx == e, starts[e], 0)
    n_tiles = r_max // MOE_TILE
    n_used = ends[-1] // MOE_TILE
    tile_ids = jnp.minimum(jnp.arange(n_tiles, dtype=jnp.int32), n_used - 1)
    tile_expert = jnp.sum((ends // MOE_TILE)[None, :] <= tile_ids[:, None], axis=1).astype(jnp.int32)
    tile_first = jnp.concatenate(
        [jnp.ones((1,), jnp.int32), (tile_expert[1:] != tile_expert[:-1]).astype(jnp.int32)])
    of_expert = (tile_expert[:, None] == jnp.arange(cnt.shape[0], dtype=jnp.int32)[None, :]).astype(jnp.int32)
    first_tile = jnp.sum(of_expert * ((ends - padded) // MOE_TILE)[None, :], axis=1)
    tile_valid = jnp.clip(jnp.sum(of_expert * cnt[None, :], axis=1) - (tile_ids - first_tile) * MOE_TILE,
                          0, MOE_TILE).astype(jnp.int32)
    experts = jnp.arange(cnt.shape[0], dtype=jnp.int32)
    has_rows = cnt > 0
    tile_slot = (jnp.sum(of_expert * (jnp.cumsum(has_rows.astype(jnp.int32)) - 1)[None, :], axis=1) % 2).astype(jnp.int32)
    later_with_rows = jnp.logical_and(has_rows[None, :], experts[None, :] > experts[:, None])
    next_expert = jnp.min(jnp.where(later_with_rows, experts[None, :], cnt.shape[0]), axis=1)
    next_expert = jnp.where(next_expert < cnt.shape[0], next_expert, -1)
    tile_next = jnp.sum(of_expert * next_expert[None, :], axis=1).astype(jnp.int32)
    plan = (tile_expert, tile_first, tile_valid, n_used.reshape(1).astype(jnp.int32), tile_slot, tile_next)
    return pos, wts, plan


def _rope_tables(length, n_ctx_rows):
    rows = length // GRID_W
    row = jnp.repeat(jnp.arange(rows, dtype=F32), GRID_W)
    col = jnp.tile(jnp.arange(GRID_W, dtype=F32), rows)
    n_freq = C_DQK // 4
    inv_freq = ROPE_THETA ** (-jnp.arange(n_freq, dtype=F32) / n_freq)
    ang_r = row[:, None] * inv_freq
    ang_c = col[:, None] * inv_freq
    cos = jnp.concatenate([jnp.cos(ang_r), jnp.cos(ang_r), jnp.cos(ang_c), jnp.cos(ang_c)], axis=-1)
    sin = jnp.concatenate([-jnp.sin(ang_r), jnp.sin(ang_r), -jnp.sin(ang_c), jnp.sin(ang_c)], axis=-1)
    reps = C_W // C_DQK
    cos = jnp.concatenate([jnp.ones((n_ctx_rows, C_W), F32), jnp.tile(cos, (1, reps))], axis=0)
    sin = jnp.concatenate([jnp.zeros((n_ctx_rows, C_W), F32), jnp.tile(sin, (1, reps))], axis=0)
    return cos, sin


def _hgrn_lower_bound(table, layer):
    p = jax.nn.softmax(table.astype(F32), axis=1)
    cum = jnp.cumsum(p, axis=1) - p[:, :1]
    return jnp.clip(cum[:, layer], 0.0, 1.0)


def kernel(x, c, ctx, c_ctx, ada_w, ada_b, sandwich_norms, w_in, w_out, hgrn_lower_bounds, hgrn_norm,
           mlstm_gate_bias, mlstm_norm, diff_lambdas, diff_norm, router_w, router_b, moe_w1, moe_b1,
           moe_w2, moe_b2):
    n_batch, seq, d = x.shape
    n_ctx_rows = ctx.shape[1]
    depth = w_in.shape[0]
    n_exp = router_w.shape[2]
    assert seq % ROW_TILE == 0 and n_ctx_rows % ROW_TILE == 0 and seq % GRID_W == 0
    rows_b = n_ctx_rows + seq
    tiles_b = rows_b // ROW_TILE
    ctx_tiles = n_ctx_rows // ROW_TILE
    lat_tiles = seq // ROW_TILE

    b_main = PA_W + 2 * HEADS * B_QK + 2 * B_W
    w_in_p = jnp.concatenate(
        [w_in[:, :, :b_main],
         jnp.pad(w_in[:, :, b_main:b_main + N_GATE], ((0, 0), (0, 0), (0, LANES - N_GATE))),
         w_in[:, :, b_main + N_GATE:]], axis=2).astype(BF16)
    w_out_b = w_out.astype(BF16)
    two_f = moe_b1.shape[2]
    b1p = moe_b1.reshape(depth, n_exp, two_f // PAIR_BLOCK, LANES, 2).transpose(0, 1, 2, 4, 3).reshape(
        depth, n_exp, 1, two_f)
    b2 = moe_b2[:, :, None, :]
    ada_b3 = ada_b[:, None, :]
    router_b3 = router_b[:, :, None]
    rw_t = router_w.transpose(0, 2, 1)
    rw_hi = rw_t.astype(BF16)
    router_wt = jnp.stack([rw_hi, (rw_t - rw_hi.astype(F32)).astype(BF16)], axis=1)
    gate_bias = jnp.pad(mlstm_gate_bias, ((0, 0), (0, LANES - N_GATE)))
    cos, sin = _rope_tables(seq, n_ctx_rows)

    cond_rows = (n_batch + 1 + 7) // 8 * 8
    cvec = jnp.zeros((cond_rows, d), F32).at[:n_batch].set(c).at[n_batch].set(c_ctx)

    def mod_row_all(i):
        return jnp.where(i % tiles_b < ctx_tiles, n_batch, i // tiles_b)

    xy = jnp.concatenate([ctx, x], axis=1).reshape(n_batch * rows_b, d)

    for layer in range(depth):
        last = layer == depth - 1
        mod = _adaln(cvec, ada_w, ada_b3, layer).reshape(cond_rows, 1, 6 * d)
        lb = _hgrn_lower_bound(hgrn_lower_bounds, layer)
        lam_init = 0.8 - 0.6 * math.exp(-0.3 * layer)
        lq1, lk1, lq2, lk2 = diff_lambdas[layer].astype(F32)
        lam = (jnp.exp(jnp.sum(lq1 * lk1)) - jnp.exp(jnp.sum(lq2 * lk2)) + lam_init).reshape(1)

        pa, pb, pc = _inproj(xy, mod, sandwich_norms, w_in_p, cos, sin, layer, tiles_b, ctx_tiles, n_batch)
        oa = _hgrn(pa.reshape(n_batch, rows_b, PA_W), lb, jnp.tile(hgrn_norm[layer], HEADS)[None, :], n_ctx_rows)
        ob = _mlstm(pb.reshape(n_batch, rows_b, PB_W), gate_bias[layer][None, :], mlstm_norm[layer][None, :],
                    n_ctx_rows)
        oc = _attention(pc.reshape(n_batch, rows_b, PC_OUT), lam, diff_norm[layer][None, :], n_ctx_rows,
                        lam_init, not last)

        if last:
            n_tiles = n_batch * lat_tiles
            in_tile = lambda i: (i // lat_tiles) * tiles_b + ctx_tiles + i % lat_tiles
            mod_row = lambda i: i // lat_tiles
        else:
            n_tiles = n_batch * tiles_b
            in_tile = lambda i: i
            mod_row = mod_row_all
        xn, h, route, counts = _outproj(
            oa.reshape(-1, A_W), ob.reshape(-1, B_W), oc.reshape(-1, C_W), xy, mod, sandwich_norms, w_out_b,
            router_wt, router_b3, layer, n_tiles, in_tile, lambda i: i, mod_row)

        r_max = n_tiles * ROW_TILE * TOP_K + n_exp * MOE_TILE
        pos, wts, plan = _dispatch_plan(route, counts, r_max)
        xs = _dispatch_rows(h, pos, r_max)
        ys = _moe(plan, xs, moe_w1, b1p, moe_w2, b2, layer)
        xy = _ffn_residual(xn, wts, _combine_rows(ys, pos), mod, sandwich_norms, layer, mod_row)

    return xy.reshape(n_batch, seq, d)
```

```python
import functools
import math

import jax
import jax.numpy as jnp
from jax import lax
from jax.experimental import pallas as pl
from jax.experimental.pallas import tpu as pltpu
from jax.experimental.pallas import tpu_sc as plsc

F32 = jnp.float32
BF16 = jnp.bfloat16
HI = lax.Precision.HIGHEST

HEADS = 4
A_W = 256
B_QK = 32
B_W = 256
C_DQK = 64
C_DV = 2 * C_DQK
ATTN_KEY_SCALE = C_DQK ** -0.5 * math.log2(math.e)
C_W = 512
HEAD_V = 64
N_GATE = 16
GRID_W = 64
TOP_K = 4
SWIGLU_ALPHA = 1.702
SWIGLU_LIMIT = 7.0
ROPE_THETA = 10000.0
NORM_EPS = 1e-6
MASK_NEG = -1e30
F_MIN = 1e-12

LANES = 128
ROW_TILE = 256
CHUNK = 64
SCAN_BATCHES = 2
ROW_GROUPS = 2
MOE_TILE = 512
ATTN_HEADS_PER_STEP = 2
VMEM_LIMIT = 56 * 1024 * 1024

PA_W = 5 * A_W
PB_W = 2 * HEADS * B_QK + 2 * B_W + LANES
PC_IN = 3 * C_W
PC_OUT = 4 * C_W
W_IN_PAD = PA_W + PB_W + PC_IN


def _cparams(sem):
    return pltpu.CompilerParams(dimension_semantics=sem, vmem_limit_bytes=VMEM_LIMIT)


def _nt(a, b):
    return lax.dot_general(a, b, (((1,), (1,)), ((), ())), preferred_element_type=F32)


def _tn(a, b, precision=None):
    return lax.dot_general(a, b, (((0,), (0,)), ((), ())), preferred_element_type=F32, precision=precision)


def _rms(x):
    return x * lax.rsqrt(jnp.mean(x * x, axis=-1, keepdims=True) + NORM_EPS)


def _silu(x):
    return x * jax.nn.sigmoid(x)


def _pack_bf16_pairs(x):
    half = x.shape[1] // 2
    bits = pltpu.bitcast(x.astype(BF16).astype(F32), jnp.uint32)
    return pltpu.bitcast((bits[:, :half] >> 16) | (bits[:, half:] & jnp.uint32(0xFFFF0000)), jnp.int32)


def _unpack_bf16_pairs(words):
    bits = pltpu.bitcast(words, jnp.uint32)
    return jnp.concatenate([pltpu.bitcast(bits << 16, F32),
                            pltpu.bitcast(bits & jnp.uint32(0xFFFF0000), F32)], axis=1)


def _adaln_kernel(c_ref, w_ref, b_ref, o_ref):
    cond = _silu(c_ref[...])
    o_ref[...] = jnp.dot(cond, w_ref[...], preferred_element_type=F32, precision=HI) + b_ref[...]


def _adaln(cvec, ada_w, ada_b, layer):
    rows, d = cvec.shape
    return pl.pallas_call(
        _adaln_kernel,
        out_shape=jax.ShapeDtypeStruct((rows, 6 * d), F32),
        grid=(6,),
        in_specs=[pl.BlockSpec((rows, d), lambda j: (0, 0)),
                  pl.BlockSpec((None, d, d), lambda j: (layer, 0, j)),
                  pl.BlockSpec((None, 1, d), lambda j: (layer, 0, j))],
        out_specs=pl.BlockSpec((rows, d), lambda j: (0, j)),
        compiler_params=_cparams(("arbitrary",)),
        name="adaln",
    )(cvec, ada_w, ada_b)


def _inproj_kernel(x_ref, mod_ref, g_ref, w_ref, cos_ref, sin_ref, pa_ref, pb_ref, pc_ref):
    d = x_ref.shape[1]
    shift = mod_ref[0, :, 0:d]
    scale = mod_ref[0, :, d:2 * d]
    h = _rms(x_ref[...]) * g_ref[0:1, :] * (1.0 + scale) + shift
    hb = h.astype(BF16)
    pc = jnp.dot(hb, w_ref[:, PA_W + PB_W:W_IN_PAD], preferred_element_type=F32)
    pa_ref[...] = jnp.dot(hb, w_ref[:, 0:PA_W], preferred_element_type=F32)
    pb_ref[...] = jnp.dot(hb, w_ref[:, PA_W:PA_W + PB_W], preferred_element_type=F32)
    q = pc[:, 0:C_W]
    k = pc[:, C_W:2 * C_W] * ATTN_KEY_SCALE
    cos = cos_ref[...]
    sin = sin_ref[...]
    lane = lax.broadcasted_iota(jnp.int32, q.shape, 1)
    first = (lane % 32) < 16

    def rope(t):
        partner = jnp.where(first, pltpu.roll(t, C_W - 16, 1), pltpu.roll(t, 16, 1))
        return t * cos + partner * sin

    pc_ref[:, 0:C_W] = q.astype(BF16)
    pc_ref[:, C_W:2 * C_W] = rope(q).astype(BF16)
    pc_ref[:, 2 * C_W:3 * C_W] = rope(k).astype(BF16)
    pc_ref[:, 3 * C_W:4 * C_W] = pc[:, 2 * C_W:3 * C_W].astype(BF16)


def _inproj(xy, mod, norms, w_in_p, cos, sin, layer, tiles_per_batch, n_ctx_tiles, n_batch):
    t_all, d = xy.shape
    n_tiles = t_all // ROW_TILE

    def mod_row(i):
        return jnp.where(i % tiles_per_batch < n_ctx_tiles, n_batch, i // tiles_per_batch)

    return pl.pallas_call(
        _inproj_kernel,
        out_shape=(jax.ShapeDtypeStruct((t_all, PA_W), F32),
                   jax.ShapeDtypeStruct((t_all, PB_W), F32),
                   jax.ShapeDtypeStruct((t_all, PC_OUT), BF16)),
        grid=(n_tiles,),
        in_specs=[pl.BlockSpec((ROW_TILE, d), lambda i: (i, 0)),
                  pl.BlockSpec((1, 1, 6 * d), lambda i: (mod_row(i), 0, 0)),
                  pl.BlockSpec((None, 4, d), lambda i: (layer, 0, 0)),
                  pl.BlockSpec((None, d, W_IN_PAD), lambda i: (layer, 0, 0)),
                  pl.BlockSpec((ROW_TILE, C_W), lambda i: (i % tiles_per_batch, 0)),
                  pl.BlockSpec((ROW_TILE, C_W), lambda i: (i % tiles_per_batch, 0))],
        out_specs=(pl.BlockSpec((ROW_TILE, PA_W), lambda i: (i, 0)),
                   pl.BlockSpec((ROW_TILE, PB_W), lambda i: (i, 0)),
                   pl.BlockSpec((ROW_TILE, PC_OUT), lambda i: (i, 0))),
        compiler_params=_cparams(("arbitrary",)),
        name="inproj",
    )(xy, mod, norms, w_in_p, cos, sin)


def _hgrn_kernel(pa_ref, lb_ref, gain_ref, o_ref, st_ref, ob_ref, *, n_ctx, n_all, n_b):
    c_sz, w = CHUNK, A_W
    same_head = (lax.broadcasted_iota(jnp.int32, (w, w), 0) // HEAD_V
                 == lax.broadcasted_iota(jnp.int32, (w, w), 1) // HEAD_V)
    head_ones = same_head.astype(BF16)
    assert c_sz == HEAD_V
    t_i = lax.broadcasted_iota(jnp.int32, (c_sz, w), 0)
    s_i = lax.broadcasted_iota(jnp.int32, (c_sz, w), 1) % c_sz
    t_row = lax.broadcasted_iota(jnp.int32, (c_sz, 1), 0)
    row8 = t_row % 8

    def grouped_rows(a, k):
        return jnp.concatenate(
            [jnp.broadcast_to(a[8 * j + k:8 * j + k + 1, :], (8, w)) for j in range(c_sz // 8)], axis=0)

    def halving_levels(rev):
        out = []
        b = c_sz // 2
        while b >= 1:
            def later(i):
                return ((i % (2 * b)) < b) if rev else ((i % (2 * b)) >= b)
            live = jnp.logical_and(t_i // (2 * b) == s_i // (2 * b),
                                   jnp.logical_and(later(t_i), jnp.logical_not(later(s_i))))
            out.append((b, live.astype(F32), jnp.where(later(t_row), 1.0, -1.0)))
            b //= 2
        return out

    n_seg = 2 * n_b
    seg_rev = [seg % 2 == 1 for seg in range(n_seg)]

    def seg_rows(seg):
        return slice(seg * c_sz, (seg + 1) * c_sz)

    def stack(parts):
        return jnp.concatenate(parts, axis=0)

    def rows_of(rows):
        return stack([jnp.broadcast_to(row, (c_sz, w)) for row in rows])

    self_mask = stack([(s_i == t_i).astype(F32)] * n_seg)
    levels = [(b, stack([live_b if rev else live_f for rev in seg_rev]),
               stack([sign_b if rev else sign_f for rev in seg_rev]))
              for (b, live_f, sign_f), (_, live_b, sign_b) in zip(halving_levels(False), halving_levels(True))]
    lb_all = rows_of([lb_ref[1:2, :] if rev else lb_ref[0:1, :] for rev in seg_rev])
    r_all = lax.broadcasted_iota(jnp.int32, (n_seg * c_sz, n_seg * c_sz), 0)
    c_all = lax.broadcasted_iota(jnp.int32, (n_seg * c_sz, n_seg * c_sz), 1)
    rev_row = (r_all // c_sz) % 2 == 1
    tri = jnp.logical_and(r_all // c_sz == c_all // c_sz, jnp.logical_or(
        jnp.logical_and(jnp.logical_not(rev_row), c_all <= r_all),
        jnp.logical_and(rev_row, c_all >= r_all))).astype(BF16)
    zero = jnp.zeros((), BF16)

    def boundary_rows(cum, b, rev):
        if b >= 8:
            return jnp.concatenate(
                [jnp.broadcast_to(cum[r0 + (b if rev else b - 1):r0 + (b if rev else b - 1) + 1, :], (2 * b, w))
                 for r0 in range(0, c_sz, 2 * b)], axis=0)
        ref = None
        for g in reversed(range(8 // (2 * b))):
            cand = grouped_rows(cum, g * 2 * b + (b if rev else b - 1))
            ref = cand if ref is None else jnp.where(row8 < (g + 1) * 2 * b, cand, ref)
        return ref

    def scan_step(n, carry):
        c_f = n
        c_b = jnp.where(n < n_ctx, n_ctx - 1 - n, n_all - 1 - (n - n_ctx))
        chunk_rows = [pl.ds(pl.multiple_of((c_b if rev else c_f) * c_sz, c_sz), c_sz) for rev in seg_rev]
        q_pre = stack([pa_ref[seg // 2, chunk_rows[seg], 0:A_W] for seg in range(n_seg)])
        v_l = [pa_ref[seg // 2, chunk_rows[seg], A_W:2 * A_W].astype(BF16) for seg in range(n_seg)]
        f_pre = stack([pa_ref[seg // 2, chunk_rows[seg], (3 if rev else 2) * A_W:(4 if rev else 3) * A_W]
                       for seg, rev in enumerate(seg_rev)])
        q = _silu(q_pre)
        f = lb_all + (1.0 - lb_all) * jax.nn.sigmoid(f_pre)
        log_f = jnp.log(jnp.maximum(f, F_MIN))
        kk = (1.0 - lb_all) * jax.nn.sigmoid(-f_pre)
        cum = sum(jnp.dot(tri, piece, preferred_element_type=F32) for piece in _split3(log_f))
        ends = [cum[seg * c_sz + (0 if rev else c_sz - 1):seg * c_sz + (0 if rev else c_sz - 1) + 1, :]
                for seg, rev in enumerate(seg_rev)]

        st_l = [st_ref[seg] for seg in range(n_seg)]
        q_in = (q * jnp.exp(cum)).astype(BF16)
        o = stack([_nt(q_in[seg_rows(seg)], st_l[seg].astype(BF16)) for seg in range(n_seg)])
        k_end = (kk * jnp.exp(rows_of(ends) - cum)).astype(BF16)
        for seg in range(n_seg):
            st_ref[seg] = (st_l[seg] * jnp.exp(ends[seg])
                           + jnp.where(same_head, _tn(v_l[seg], k_end[seg_rows(seg)]), 0.0))

        def scores(qa, ka):
            return _nt(qa, jnp.where(same_head, jnp.concatenate([ka] * HEADS, axis=0), zero))

        refs = [stack([boundary_rows(cum[seg_rows(seg)], b, rev) for seg, rev in enumerate(seg_rev)])
                for b, _, _ in levels]
        zs = [jnp.exp((cum - ref) * sign)
              for ref, (_, _, sign) in zip(refs, levels)]
        qas = [(q * z).astype(BF16) for z in zs]
        kas = [(kk * z).astype(BF16) for z in zs]
        prods = [stack([scores(qa[seg_rows(seg)], ka[seg_rows(seg)]) for seg in range(n_seg)])
                 for qa, ka in zip(qas, kas)]
        p = self_mask * jnp.dot((q * kk).astype(BF16), head_ones, preferred_element_type=F32)
        for prod, (_, live, _) in zip(prods, levels):
            p = p + live * prod
        pb = p.astype(BF16)

        for seg, rev in enumerate(seg_rev):
            out = o[seg_rows(seg)] + jnp.dot(
                pb[seg_rows(seg)], jnp.where(same_head, jnp.concatenate([v_l[seg]] * HEADS, axis=0), zero),
                preferred_element_type=F32)
            if rev:
                ob_ref[seg // 2, chunk_rows[seg], :] = out
            else:
                o_ref[seg // 2, chunk_rows[seg], :] = out
        return carry

    st_ref[...] = jnp.zeros_like(st_ref)
    lax.fori_loop(0, n_all, scan_step, 0, unroll=2)

    def readout(c, carry):
        rows = pl.ds(pl.multiple_of(c * ROW_TILE, ROW_TILE), ROW_TILE)
        for b in range(n_b):
            tot = o_ref[b, rows, :] + ob_ref[b, rows, :]
            ms = sum(jnp.dot(piece, head_ones, preferred_element_type=F32)
                     for piece in _split3(tot * tot)) * (1.0 / HEAD_V)
            g = pa_ref[b, rows, 4 * A_W:5 * A_W]
            o_ref[b, rows, :] = tot * lax.rsqrt(ms + NORM_EPS) * gain_ref[...] * _silu(g)
        return carry

    lax.fori_loop(0, n_all * c_sz // ROW_TILE, readout, 0)


def _hgrn(pa3, lb, gain, n_ctx_rows):
    n_batch, rows, _ = pa3.shape
    n_b = SCAN_BATCHES if n_batch % SCAN_BATCHES == 0 else 1
    kern = functools.partial(_hgrn_kernel, n_ctx=n_ctx_rows // CHUNK, n_all=rows // CHUNK, n_b=n_b)
    return pl.pallas_call(
        kern,
        out_shape=jax.ShapeDtypeStruct((n_batch, rows, A_W), F32),
        grid=(n_batch // n_b,),
        in_specs=[pl.BlockSpec((n_b, rows, PA_W), lambda b: (b, 0, 0), pipeline_mode=pl.Buffered(1)),
                  pl.BlockSpec((2, A_W), lambda b: (0, 0)),
                  pl.BlockSpec((1, A_W), lambda b: (0, 0))],
        out_specs=pl.BlockSpec((n_b, rows, A_W), lambda b: (b, 0, 0)),
        scratch_shapes=[pltpu.VMEM((2 * n_b, A_W, A_W), F32), pltpu.VMEM((n_b, rows, A_W), F32)],
        compiler_params=_cparams(("arbitrary",)),
        name="hgrn2",
    )(pa3, lb, gain)


def _split3(x):
    hi = x.astype(BF16)
    rest = x - hi.astype(F32)
    mid = rest.astype(BF16)
    return hi, mid, (rest - mid.astype(F32)).astype(BF16)


def _mlstm_kernel(pb_ref, bias_ref, gain_ref, o_ref, cn_ref, ob_ref, *, n_ctx, n_all, n_b):
    assert CHUNK == HEAD_V
    c_sz, w, qk_w = CHUNK, B_W, HEADS * B_QK
    t_i = lax.broadcasted_iota(jnp.int32, (c_sz, w), 0)
    s_i = lax.broadcasted_iota(jnp.int32, (c_sz, w), 1) % c_sz
    diag4 = s_i == t_i
    half_lane = lax.broadcasted_iota(jnp.int32, (1, LANES), 1) < HEAD_V
    k_block = (lax.broadcasted_iota(jnp.int32, (HEADS * c_sz, qk_w), 0) // c_sz
               == lax.broadcasted_iota(jnp.int32, (HEADS * c_sz, qk_w), 1) // B_QK)
    v_block = (lax.broadcasted_iota(jnp.int32, (HEADS * c_sz, w), 0) // c_sz
               == lax.broadcasted_iota(jnp.int32, (HEADS * c_sz, w), 1) // HEAD_V)
    state_block = (lax.broadcasted_iota(jnp.int32, (qk_w, 2 * w), 0) // B_QK
                   == (lax.broadcasted_iota(jnp.int32, (qk_w, 2 * w), 1) % w) // HEAD_V)
    head_ones = v_block.astype(BF16)
    q_off, k_off, v_off, o_off, g_off = 0, qk_w, 2 * qk_w, 2 * qk_w + B_W, 2 * qk_w + 2 * B_W

    sel_r = lax.broadcasted_iota(jnp.int32, (2 * LANES, 2 * w), 0)
    sel_c = lax.broadcasted_iota(jnp.int32, (2 * LANES, 2 * w), 1)
    sel = (sel_r == jnp.where(sel_c < w, HEADS + sel_c // HEAD_V, LANES + (sel_c - w) // HEAD_V)).astype(BF16)

    def head_max(x):
        outs = []
        for col in range(w // LANES):
            xc = x[:, col * LANES:(col + 1) * LANES]
            lo = jnp.max(jnp.where(half_lane, xc, -jnp.inf), axis=-1, keepdims=True)
            hi = jnp.max(jnp.where(half_lane, -jnp.inf, xc), axis=-1, keepdims=True)
            outs.append(jnp.where(half_lane, lo, hi))
        return jnp.concatenate(outs, axis=1)

    n_seg = 2 * n_b
    seg_rev = [seg % 2 == 1 for seg in range(n_seg)]

    def seg_rows(seg):
        return slice(seg * c_sz, (seg + 1) * c_sz)

    mask = jnp.concatenate([(s_i >= t_i) if rev else (s_i <= t_i) for rev in seg_rev], axis=0)
    r_all = lax.broadcasted_iota(jnp.int32, (n_seg * c_sz, n_seg * c_sz), 0)
    c_all = lax.broadcasted_iota(jnp.int32, (n_seg * c_sz, n_seg * c_sz), 1)
    same_seg = r_all // c_sz == c_all // c_sz
    rev_row = (r_all // c_sz) % 2 == 1
    tri = jnp.logical_and(same_seg, jnp.logical_or(
        jnp.logical_and(jnp.logical_not(rev_row), c_all <= r_all),
        jnp.logical_and(rev_row, c_all >= r_all))).astype(BF16)

    def per_seg(fn, x):
        return jnp.concatenate([jnp.broadcast_to(fn(x[seg_rows(seg)]), (c_sz, w)) for seg in range(n_seg)],
                               axis=0)

    def rows_of(rows):
        return jnp.concatenate([jnp.broadcast_to(row, (c_sz, w)) for row in rows], axis=0)

    def scan_step(n, m_prev_rows):
        c_f = n
        c_b = jnp.where(n < n_ctx, n_ctx - 1 - n, n_all - 1 - (n - n_ctx))
        chunk_rows = [pl.ds(pl.multiple_of((c_b if rev else c_f) * c_sz, c_sz), c_sz) for rev in seg_rev]

        def load(lo, hi):
            return [pb_ref[seg // 2, chunk_rows[seg], lo:hi] for seg in range(n_seg)]

        q_l = load(q_off, q_off + qk_w)
        k_l = load(k_off, k_off + qk_w)
        v_l = load(v_off, v_off + B_W)
        g_l = load(g_off, g_off + LANES)
        gts = jnp.concatenate(
            [pltpu.roll(g + bias_ref[...], LANES - 2 * HEADS, 1) if rev else g + bias_ref[...]
             for g, rev in zip(g_l, seg_rev)], axis=0)
        log_f = jnp.minimum(gts, 0.0) - jnp.log(1.0 + jnp.exp(-jnp.abs(gts)))
        cum_col = sum(jnp.dot(tri, p, preferred_element_type=F32) for p in _split3(log_f))
        both = sum(jnp.dot(p, sel, preferred_element_type=F32)
                   for p in _split3(jnp.concatenate([cum_col, gts], axis=1)))
        cum_t = both[:, :w]
        ig_t = both[:, w:]
        src = per_seg(lambda x: jnp.sum(jnp.where(diag4, x, 0.0), axis=0, keepdims=True), cum_t - ig_t)
        m_prev = rows_of(m_prev_rows)
        log_d = jnp.where(mask, cum_t - src, MASK_NEG)
        log_inter = cum_t + m_prev
        m_t = jnp.maximum(log_inter, head_max(log_d))
        w_intra = jnp.where(mask, jnp.exp(log_d - m_t), 0.0)
        w_inter = jnp.exp(log_inter - m_t)

        scale = B_QK ** -0.5
        qb_l = [(q * scale).astype(BF16) for q in q_l]
        kb_l = [k.astype(BF16) for k in k_l]
        zero = jnp.zeros((), BF16)
        scores = [_nt(qb, jnp.where(k_block, jnp.concatenate([kb] * HEADS, axis=0), zero))
                  for qb, kb in zip(qb_l, kb_l)]
        p = (jnp.concatenate(scores, axis=0) * w_intra).astype(BF16)

        cn_l = [cn_ref[seg] for seg in range(n_seg)]
        inter = jnp.concatenate([jnp.dot(qb, cn.astype(BF16), preferred_element_type=F32)
                                 for qb, cn in zip(qb_l, cn_l)], axis=0)
        values = [jnp.dot(p[seg_rows(seg)],
                          jnp.where(v_block, jnp.concatenate([v_l[seg].astype(BF16)] * HEADS, axis=0), zero),
                          preferred_element_type=F32) for seg in range(n_seg)]
        num = w_inter * inter[:, :w] + jnp.concatenate(values, axis=0)
        den = w_inter * inter[:, w:] + jnp.dot(p, head_ones, preferred_element_type=F32)
        h_out = num / jnp.maximum(jnp.abs(den), jnp.exp(-m_t))
        for seg, rev in enumerate(seg_rev):
            if rev:
                ob_ref[seg // 2, chunk_rows[seg], :] = h_out[seg_rows(seg)]
            else:
                o_ref[seg // 2, chunk_rows[seg], :] = h_out[seg_rows(seg)]

        ends = [cum_t[seg * c_sz + (0 if rev else c_sz - 1):seg * c_sz + (0 if rev else c_sz - 1) + 1, :]
                for seg, rev in enumerate(seg_rev)]
        log_end = rows_of(ends) - cum_t + ig_t
        m_ends = [jnp.maximum(end + m_old, jnp.max(log_end[seg_rows(seg)], axis=0, keepdims=True))
                  for seg, (end, m_old) in enumerate(zip(ends, m_prev_rows))]
        w_end = jnp.exp(log_end - rows_of(m_ends))
        for seg in range(n_seg):
            w_seg = w_end[seg_rows(seg)]
            upd = _tn(kb_l[seg], jnp.concatenate([(w_seg * v_l[seg]).astype(BF16), w_seg.astype(BF16)], axis=1))
            carry_w = jnp.exp(ends[seg] + m_prev_rows[seg] - m_ends[seg])
            cn_ref[seg] = (jnp.concatenate([carry_w, carry_w], axis=1) * cn_l[seg]
                           + jnp.where(state_block, upd, 0.0))
        return tuple(m_ends)

    cn_ref[...] = jnp.zeros_like(cn_ref)
    lax.fori_loop(0, n_all, scan_step, tuple(jnp.zeros((1, w), F32) for _ in range(n_seg)), unroll=2)

    def readout(c, carry):
        rows = pl.ds(pl.multiple_of(c * ROW_TILE, ROW_TILE), ROW_TILE)
        for b in range(n_b):
            tot = o_ref[b, rows, :] + ob_ref[b, rows, :]
            normed = jnp.concatenate(
                [_rms(tot[:, h * HEAD_V:(h + 1) * HEAD_V]) for h in range(HEADS)], axis=1) * gain_ref[...]
            og = pb_ref[b, rows, o_off:o_off + B_W]
            o_ref[b, rows, :] = normed * jax.nn.sigmoid(og)
        return carry

    lax.fori_loop(0, n_all * c_sz // ROW_TILE, readout, 0)


def _mlstm(pb3, bias, gain, n_ctx_rows):
    n_batch, rows, _ = pb3.shape
    n_b = SCAN_BATCHES if n_batch % SCAN_BATCHES == 0 else 1
    kern = functools.partial(_mlstm_kernel, n_ctx=n_ctx_rows // CHUNK, n_all=rows // CHUNK, n_b=n_b)
    return pl.pallas_call(
        kern,
        out_shape=jax.ShapeDtypeStruct((n_batch, rows, B_W), F32),
        grid=(n_batch // n_b,),
        in_specs=[pl.BlockSpec((n_b, rows, PB_W), lambda b: (b, 0, 0)),
                  pl.BlockSpec((1, LANES), lambda b: (0, 0)),
                  pl.BlockSpec((1, B_W), lambda b: (0, 0))],
        out_specs=pl.BlockSpec((n_b, rows, B_W), lambda b: (b, 0, 0)),
        scratch_shapes=[pltpu.VMEM((2 * n_b, HEADS * B_QK, 2 * B_W), F32), pltpu.VMEM((n_b, rows, B_W), F32)],
        compiler_params=_cparams(("arbitrary",)),
        name="mlstm",
    )(pb3, bias, gain)


def _attn_kernel(lam_ref, qp_ref, qr_ref, k_ref, v_ref, gain_ref, o_ref, *, n_ctx, q_tile0, lam_init):
    lam = lam_ref[0]
    q_tile = pl.program_id(2) + q_tile0
    lane = lax.broadcasted_iota(jnp.int32, (1, 2 * C_DQK), 1)

    n_all = k_ref.shape[1]

    def finish(head, parts):
        o = parts[0] - lam * parts[1]
        o_ref[0, :, head] = _rms(o) * gain_ref[...] * (1.0 - lam_init)

    def sub_query(q, j):
        return jnp.where(lane // C_DQK == j, q, jnp.zeros_like(q))

    def row_max(s):
        return jnp.max(s, axis=-1, keepdims=True)

    def row_sum(s):
        return jnp.sum(s, axis=-1, keepdims=True)

    def pv(ex, v):
        return jnp.dot(ex.astype(BF16), v, preferred_element_type=F32)

    heads = [slice(hh * C_DV, (hh + 1) * C_DV) for hh in range(ATTN_HEADS_PER_STEP)]

    def attend(pieces):
        problems = [(head, j) for head in heads for j in range(2)]
        scores = [[_nt(sub_query(q_ref[0, :, head], j), k_ref[0, k0:k1, head]) for q_ref, k0, k1 in pieces]
                  for head, j in problems]
        maxes = [functools.reduce(jnp.maximum, [row_max(s) for s in ss]) for ss in scores]
        exps = [[jnp.exp2(s - m) for s in ss] for ss, m in zip(scores, maxes)]
        outs = [sum(pv(e, v_ref[0, k0:k1, head]) for e, (_, k0, k1) in zip(es, pieces))
                / sum(row_sum(e) for e in es) for es, (head, _) in zip(exps, problems)]
        for n, head in enumerate(heads):
            finish(head, outs[2 * n:2 * n + 2])

    @pl.when(q_tile * ROW_TILE < n_ctx)
    def _():
        attend([(qp_ref, 0, n_ctx)])

    @pl.when(q_tile * ROW_TILE >= n_ctx)
    def _():
        attend([(qp_ref, 0, n_ctx), (qr_ref, n_ctx, n_all)])


def _attention(pc3, lam, gain, n_ctx_rows, lam_init, with_ctx):
    n_batch, rows, _ = pc3.shape
    q_tile0 = 0 if with_ctx else n_ctx_rows // ROW_TILE
    n_q = rows // ROW_TILE - q_tile0
    gw = ATTN_HEADS_PER_STEP * C_DV
    hb = C_W // gw
    kern = functools.partial(_attn_kernel, n_ctx=n_ctx_rows, q_tile0=q_tile0, lam_init=lam_init)
    grid_spec = pltpu.PrefetchScalarGridSpec(
        num_scalar_prefetch=1,
        grid=(n_batch, hb, n_q),
        in_specs=[pl.BlockSpec((1, ROW_TILE, gw), lambda b, h, i, lam: (b, i + q_tile0, h)),
                  pl.BlockSpec((1, ROW_TILE, gw), lambda b, h, i, lam: (b, i + q_tile0, hb + h)),
                  pl.BlockSpec((1, rows, gw), lambda b, h, i, lam: (b, 0, 2 * hb + h)),
                  pl.BlockSpec((1, rows, gw), lambda b, h, i, lam: (b, 0, 3 * hb + h)),
                  pl.BlockSpec((1, C_DV), lambda b, h, i, lam: (0, 0))],
        out_specs=pl.BlockSpec((1, ROW_TILE, gw), lambda b, h, i, lam: (b, i, h)),
    )
    return pl.pallas_call(
        kern,
        out_shape=jax.ShapeDtypeStruct((n_batch, n_q * ROW_TILE, C_W), F32),
        grid_spec=grid_spec,
        compiler_params=_cparams(("arbitrary", "arbitrary", "arbitrary")),
        name="diff_attn",
    )(lam, pc3, pc3, pc3, pc3, gain)


def _outproj_kernel(oa_ref, ob_ref, oc_ref, x_ref, mod_ref, n_ref, w_ref, rw_ref, rb_ref,
                    xn_ref, h_ref, route_ref, cnt_ref, carry_ref):
    d = x_ref.shape[1]
    n_exp = rw_ref.shape[1]

    @pl.when(pl.program_id(0) == 0)
    def _():
        carry_ref[...] = jnp.zeros_like(carry_ref)

    n_rows = x_ref.shape[0]
    halves = [slice(g * n_rows // ROW_GROUPS, (g + 1) * n_rows // ROW_GROUPS) for g in range(ROW_GROUPS)]
    mixes = [jnp.dot(oa_ref[hs, :].astype(BF16), w_ref[0:A_W, :], preferred_element_type=F32)
             + jnp.dot(ob_ref[hs, :].astype(BF16), w_ref[A_W:A_W + B_W, :], preferred_element_type=F32)
             + jnp.dot(oc_ref[hs, :].astype(BF16), w_ref[A_W + B_W:A_W + B_W + C_W, :],
                       preferred_element_type=F32) for hs in halves]
    xns = [x_ref[hs, :] + mod_ref[0, :, 2 * d:3 * d] * (_rms(mix) * n_ref[1:2, :])
           for hs, mix in zip(halves, mixes)]
    hiddens = [_rms(xn) * n_ref[2:3, :] * (1.0 + mod_ref[0, :, 4 * d:5 * d]) + mod_ref[0, :, 3 * d:4 * d]
               for xn in xns]
    for hs, xn, h in zip(halves, xns, hiddens):
        xn_ref[hs, :] = xn
        h_ref[hs, :] = _pack_bf16_pairs(h)

    h_his = [h.astype(BF16) for h in hiddens]
    h_mids = [(h - h_hi.astype(F32)).astype(BF16) for h, h_hi in zip(hiddens, h_his)]
    logits = jnp.concatenate([_nt(rw_ref[0], h_hi) + _nt(rw_ref[1], h_hi) + _nt(rw_ref[0], h_mid)
                              for h_hi, h_mid in zip(h_his, h_mids)], axis=1) + rb_ref[...]
    e_sub = lax.broadcasted_iota(jnp.int32, logits.shape, 0)
    cur = logits
    picks, vals = [], []
    for _ in range(TOP_K):
        mx = jnp.max(cur, axis=0, keepdims=True)
        idx = jnp.min(jnp.where(cur == mx, e_sub, n_exp), axis=0, keepdims=True)
        hit = e_sub == idx
        cur = jnp.where(hit, -jnp.inf, cur)
        picks.append((idx, hit.astype(F32)))
        vals.append(mx)
    exps = [jnp.exp(vv - vals[0]) for vv in vals]
    total = exps[0] + exps[1] + exps[2] + exps[3]

    chosen = picks[0][1] + picks[1][1] + picks[2][1] + picks[3][1]
    tm = logits.shape[1]
    before = (lax.broadcasted_iota(jnp.int32, (tm, tm), 0) < lax.broadcasted_iota(jnp.int32, (tm, tm), 1))
    seen = jnp.dot(chosen.astype(BF16), before.astype(BF16), preferred_element_type=F32) + carry_ref[...]
    new_carry = carry_ref[...] + jnp.sum(chosen, axis=1, keepdims=True)
    carry_ref[...] = new_carry
    cnt_ref[...] = new_carry

    o_row = lax.broadcasted_iota(jnp.int32, route_ref.shape, 0)
    route = jnp.zeros(route_ref.shape, F32)
    for kk in range(TOP_K):
        idx, hit = picks[kk]
        rank = jnp.sum(hit * seen, axis=0, keepdims=True)
        route = (route + jnp.where(o_row == kk, idx.astype(F32), 0.0)
                 + jnp.where(o_row == TOP_K + kk, exps[kk] / total, 0.0)
                 + jnp.where(o_row == 2 * TOP_K + kk, rank, 0.0))
    route_ref[...] = route


ROUTE_ROWS = 16


def _outproj(oa, ob, oc, xy, mod, norms, w_out_b, router_wt, router_b, layer, n_tiles, in_tile, oc_tile, mod_row):
    d = xy.shape[1]
    n_exp = router_wt.shape[2]
    t_out = n_tiles * ROW_TILE
    xn, h, route_t, counts = pl.pallas_call(
        _outproj_kernel,
        out_shape=(jax.ShapeDtypeStruct((t_out, d), F32),
                   jax.ShapeDtypeStruct((t_out, d // 2), jnp.int32),
                   jax.ShapeDtypeStruct((n_tiles * ROUTE_ROWS, ROW_TILE), F32),
                   jax.ShapeDtypeStruct((n_exp, 1), F32)),
        grid=(n_tiles,),
        in_specs=[pl.BlockSpec((ROW_TILE, A_W), lambda i: (in_tile(i), 0)),
                  pl.BlockSpec((ROW_TILE, B_W), lambda i: (in_tile(i), 0)),
                  pl.BlockSpec((ROW_TILE, C_W), lambda i: (oc_tile(i), 0)),
                  pl.BlockSpec((ROW_TILE, d), lambda i: (in_tile(i), 0)),
                  pl.BlockSpec((1, 1, 6 * d), lambda i: (mod_row(i), 0, 0)),
                  pl.BlockSpec((None, 4, d), lambda i: (layer, 0, 0)),
                  pl.BlockSpec((None, d, d), lambda i: (layer, 0, 0)),
                  pl.BlockSpec((None, 2, n_exp, d), lambda i: (layer, 0, 0, 0)),
                  pl.BlockSpec((None, n_exp, 1), lambda i: (layer, 0, 0))],
        out_specs=(pl.BlockSpec((ROW_TILE, d), lambda i: (i, 0)),
                   pl.BlockSpec((ROW_TILE, d // 2), lambda i: (i, 0)),
                   pl.BlockSpec((ROUTE_ROWS, ROW_TILE), lambda i: (i, 0)),
                   pl.BlockSpec((n_exp, 1), lambda i: (0, 0))),
        scratch_shapes=[pltpu.VMEM((n_exp, 1), F32)],
        compiler_params=_cparams(("arbitrary",)),
        name="outproj_router",
    )(oa, ob, oc, xy, mod, norms, w_out_b, router_wt, router_b)
    return xn, h, route_t.reshape(n_tiles, ROUTE_ROWS, ROW_TILE), counts[:, 0]


PAIR_BLOCK = 2 * LANES


def _moe_kernel(te_ref, first_ref, valid_ref, nu_ref, slot_ref, next_ref, xs_ref, w1_hbm, b1_ref, w2_hbm, b2_ref,
                ys_ref, w1p_ref, w2b_ref, w1_buf, w2_buf, w_sem, *, layer):
    i = pl.program_id(0)
    two_f = w1_buf.shape[2]
    n_blk = two_f // PAIR_BLOCK

    def fetch(expert, slot):
        return (pltpu.make_async_copy(w1_hbm.at[layer, expert], w1_buf.at[slot], w_sem.at[0, slot]),
                pltpu.make_async_copy(w2_hbm.at[layer, expert], w2_buf.at[slot], w_sem.at[1, slot]))

    @pl.when(i == 0)
    def _():
        for copy in fetch(te_ref[0], 0):
            copy.start()

    @pl.when(jnp.logical_and(i < nu_ref[0], first_ref[i] == 1))
    def _():
        slot = slot_ref[i]
        for copy in fetch(te_ref[i], slot):
            copy.wait()
        r = lax.broadcasted_iota(jnp.int32, (PAIR_BLOCK, PAIR_BLOCK), 0)
        c = lax.broadcasted_iota(jnp.int32, (PAIR_BLOCK, PAIR_BLOCK), 1)
        perm = (r == jnp.where(c < LANES, 2 * c, 2 * (c - LANES) + 1)).astype(BF16)
        for blk in range(n_blk):
            cols = slice(blk * PAIR_BLOCK, (blk + 1) * PAIR_BLOCK)
            w1p_ref[:, cols] = jnp.dot(w1_buf[slot, :, cols].astype(BF16), perm,
                                       preferred_element_type=F32).astype(BF16)
        w2b_ref[...] = w2_buf[slot].astype(BF16)

        @pl.when(next_ref[i] >= 0)
        def _():
            for copy in fetch(next_ref[i], 1 - slot):
                copy.start()

    tile_rows = xs_ref.shape[0]
    used = i < nu_ref[0]
    valid = valid_ref[i]

    def expert_ffn(n_rows):
        x = _unpack_bf16_pairs(xs_ref[0:n_rows, :]).astype(BF16)
        row = lax.broadcasted_iota(jnp.int32, (n_rows, 1), 0)
        x = jnp.where(row < valid, x, jnp.zeros_like(x))
        hid = jnp.dot(x, w1p_ref[...], preferred_element_type=F32) + b1_ref[...]
        acts = []
        for blk in range(n_blk):
            glu = jnp.minimum(hid[:, blk * PAIR_BLOCK:blk * PAIR_BLOCK + LANES], SWIGLU_LIMIT)
            lin = jnp.clip(hid[:, blk * PAIR_BLOCK + LANES:(blk + 1) * PAIR_BLOCK], -SWIGLU_LIMIT, SWIGLU_LIMIT)
            acts.append((glu * jax.nn.sigmoid(SWIGLU_ALPHA * glu) * (lin + 1.0)).astype(BF16))
        y = jnp.dot(jnp.concatenate(acts, axis=1), w2b_ref[...], preferred_element_type=F32) + b2_ref[...]
        ys_ref[0:n_rows, :] = _pack_bf16_pairs(y)
        if n_rows < tile_rows:
            ys_ref[n_rows:tile_rows, :] = jnp.zeros((tile_rows - n_rows, ys_ref.shape[1]), ys_ref.dtype)

    @pl.when(jnp.logical_and(used, valid > tile_rows // 2))
    def _():
        expert_ffn(tile_rows)

    @pl.when(jnp.logical_and(used, valid <= tile_rows // 2))
    def _():
        expert_ffn(tile_rows // 2)

    @pl.when(jnp.logical_not(used))
    def _():
        ys_ref[...] = jnp.zeros_like(ys_ref)


def _moe(plan, xs, w1, b1p, w2, b2, layer):
    r_max, half_d = xs.shape
    d = 2 * half_d
    two_f = w1.shape[3]
    f = two_f // 2
    n_tiles = r_max // MOE_TILE

    def row_tile(i, te, fi, va, nu, sl, nx):
        return (jnp.maximum(jnp.minimum(i, nu[0] - 1), 0), 0)

    def bias_block(i, te, fi, va, nu, sl, nx):
        return (layer, te[i], 0, 0)

    grid_spec = pltpu.PrefetchScalarGridSpec(
        num_scalar_prefetch=6,
        grid=(n_tiles,),
        in_specs=[pl.BlockSpec((MOE_TILE, half_d), row_tile),
                  pl.BlockSpec(memory_space=pl.ANY),
                  pl.BlockSpec((None, None, 1, two_f), bias_block),
                  pl.BlockSpec(memory_space=pl.ANY),
                  pl.BlockSpec((None, None, 1, d), bias_block)],
        out_specs=pl.BlockSpec((MOE_TILE, half_d), lambda i, te, fi, va, nu, sl, nx: (i, 0)),
        scratch_shapes=[pltpu.VMEM((d, two_f), BF16), pltpu.VMEM((f, d), BF16),
                        pltpu.VMEM((2, d, two_f), F32), pltpu.VMEM((2, f, d), F32),
                        pltpu.SemaphoreType.DMA((2, 2))],
    )
    return pl.pallas_call(
        functools.partial(_moe_kernel, layer=layer),
        out_shape=jax.ShapeDtypeStruct((r_max, half_d), jnp.int32),
        grid_spec=grid_spec,
        compiler_params=_cparams(("arbitrary",)),
        name="moe_experts",
    )(*plan, xs, w1, b1p, w2, b2)


SC_CORES = 2
SC_SUBCORES = 16
SC_CHUNK = 64


def _dispatch_rows(hp, pos, r_max):
    t, width = hp.shape
    workers = SC_CORES * SC_SUBCORES
    assert t % (workers * SC_CHUNK) == 0
    per_worker = t // (workers * SC_CHUNK)
    idx = pos.reshape(-1, TOP_K, ROW_TILE // SC_CHUNK, SC_CHUNK).transpose(0, 2, 1, 3).reshape(
        workers, per_worker * TOP_K, SC_CHUNK)
    mesh = plsc.VectorSubcoreMesh(core_axis_name="c", subcore_axis_name="s",
                                  num_cores=SC_CORES, num_subcores=SC_SUBCORES)

    @functools.partial(
        pl.kernel, mesh=mesh,
        out_type=jax.ShapeDtypeStruct((r_max, width), hp.dtype),
        scratch_types=[pltpu.VMEM((per_worker * TOP_K, SC_CHUNK), jnp.int32),
                       pltpu.VMEM((2, SC_CHUNK, width), hp.dtype),
                       pltpu.SemaphoreType.DMA((2,)),
                       pltpu.SemaphoreType.DMA((2,))],
    )
    def scatter(hp_hbm, idx_hbm, out_hbm, idx_v, rows_v, read_sem, write_sem):
        wid = lax.axis_index("s") * SC_CORES + lax.axis_index("c")
        pltpu.sync_copy(idx_hbm.at[wid], idx_v)

        def read(j):
            row0 = pl.multiple_of((wid * per_worker + j) * SC_CHUNK, SC_CHUNK)
            return pltpu.make_async_copy(hp_hbm.at[pl.ds(row0, SC_CHUNK)], rows_v.at[j % 2], read_sem.at[j % 2])

        def write(j, k):
            return pltpu.make_async_copy(rows_v.at[j % 2], out_hbm.at[idx_v.at[j * TOP_K + k]],
                                         write_sem.at[j % 2])

        read(0).start()
        for j in range(per_worker):
            read(j).wait()
            for k in range(TOP_K):
                write(j, k).start()
            if j + 1 < per_worker:
                if j >= 1:
                    for k in range(TOP_K):
                        write(j - 1, k).wait()
                read(j + 1).start()
        for j in range(max(per_worker - 2, 0), per_worker):
            for k in range(TOP_K):
                write(j, k).wait()

    return scatter(hp, idx)


def _combine_rows(ysp, pos):
    width = ysp.shape[1]
    t = pos.shape[0] * pos.shape[2]
    workers = SC_CORES * SC_SUBCORES
    assert t % (workers * SC_CHUNK) == 0
    per_worker = t // (workers * SC_CHUNK)
    units = per_worker * TOP_K
    idx = pos.reshape(-1, TOP_K, ROW_TILE // SC_CHUNK, SC_CHUNK).transpose(0, 2, 1, 3).reshape(
        workers, units, SC_CHUNK)
    mesh = plsc.VectorSubcoreMesh(core_axis_name="c", subcore_axis_name="s",
                                  num_cores=SC_CORES, num_subcores=SC_SUBCORES)

    @functools.partial(
        pl.kernel, mesh=mesh,
        out_type=jax.ShapeDtypeStruct((TOP_K, t, width), ysp.dtype),
        scratch_types=[pltpu.VMEM((units, SC_CHUNK), jnp.int32),
                       pltpu.VMEM((2, SC_CHUNK, width), ysp.dtype),
                       pltpu.SemaphoreType.DMA((2,)),
                       pltpu.SemaphoreType.DMA((2,))],
    )
    def gather(ys_hbm, idx_hbm, out_hbm, idx_v, rows_v, read_sem, write_sem):
        wid = lax.axis_index("s") * SC_CORES + lax.axis_index("c")
        pltpu.sync_copy(idx_hbm.at[wid], idx_v)

        def read(u):
            return pltpu.make_async_copy(ys_hbm.at[idx_v.at[u]], rows_v.at[u % 2], read_sem.at[u % 2])

        def write(u):
            row0 = pl.multiple_of((wid * per_worker + u // TOP_K) * SC_CHUNK, SC_CHUNK)
            return pltpu.make_async_copy(rows_v.at[u % 2], out_hbm.at[u % TOP_K, pl.ds(row0, SC_CHUNK)],
                                         write_sem.at[u % 2])

        read(0).start()
        for u in range(units):
            read(u).wait()
            write(u).start()
            if u + 1 < units:
                if u >= 1:
                    write(u - 1).wait()
                read(u + 1).start()
        for u in range(max(units - 2, 0), units):
            write(u).wait()

    return gather(ysp, idx)


def _ffn_residual_kernel(x_ref, w_ref, y0_ref, y1_ref, y2_ref, y3_ref, mod_ref, n_ref, o_ref):
    d = x_ref.shape[1]
    wts = w_ref[...]
    ffn = None
    for k, y_ref in enumerate((y0_ref, y1_ref, y2_ref, y3_ref)):
        term = wts[:, k:k + 1] * _unpack_bf16_pairs(y_ref[...])
        ffn = term if ffn is None else ffn + term
    o_ref[...] = x_ref[...] + mod_ref[0, :, 5 * d:6 * d] * (_rms(ffn) * n_ref[3:4, :])


def _ffn_residual(xn, wts, ys_by_k, mod, norms, layer, mod_row):
    t, d = xn.shape
    row_spec = pl.BlockSpec((ROW_TILE, d), lambda i: (i, 0))

    def y_spec(k):
        return pl.BlockSpec((None, ROW_TILE, d // 2), lambda i: (k, i, 0))

    return pl.pallas_call(
        _ffn_residual_kernel,
        out_shape=jax.ShapeDtypeStruct((t, d), F32),
        grid=(t // ROW_TILE,),
        in_specs=[row_spec,
                  pl.BlockSpec((ROW_TILE, TOP_K), lambda i: (i, 0)),
                  y_spec(0), y_spec(1), y_spec(2), y_spec(3),
                  pl.BlockSpec((1, 1, 6 * d), lambda i: (mod_row(i), 0, 0)),
                  pl.BlockSpec((None, 4, d), lambda i: (layer, 0, 0))],
        out_specs=row_spec,
        compiler_params=_cparams(("arbitrary",)),
        name="ffn_residual",
    )(xn, wts, ys_by_k, ys_by_k, ys_by_k, ys_by_k, mod, norms)


def _dispatch_plan(route_t, counts, r_max):
    n_row_tiles = route_t.shape[0]
    idx = route_t[:, 0:TOP_K, :].astype(jnp.int32)
    wts = route_t[:, TOP_K:2 * TOP_K, :].transpose(0, 2, 1).reshape(n_row_tiles * ROW_TILE, TOP_K)
    rank = route_t[:, 2 * TOP_K:3 * TOP_K, :].astype(jnp.int32)
    cnt = counts.astype(jnp.int32)
    padded = (cnt + MOE_TILE - 1) // MOE_TILE * MOE_TILE
    ends = jnp.cumsum(padded)
    starts = ends - padded
    pos = rank
    for e in range(cnt.shape[0]):
        pos = pos + jnp.where(idx == e, starts[e], 0)
    n_tiles = r_max // MOE_TILE
    n_used = ends[-1] // MOE_TILE
    tile_ids = jnp.minimum(jnp.arange(n_tiles, dtype=jnp.int32), n_used - 1)
    tile_expert = jnp.sum((ends // MOE_TILE)[None, :] <= tile_ids[:, None], axis=1).astype(jnp.int32)
    tile_first = jnp.concatenate(
        [jnp.ones((1,), jnp.int32), (tile_expert[1:] != tile_expert[:-1]).astype(jnp.int32)])
    of_expert = (tile_expert[:, None] == jnp.arange(cnt.shape[0], dtype=jnp.int32)[None, :]).astype(jnp.int32)
    first_tile = jnp.sum(of_expert * ((ends - padded) // MOE_TILE)[None, :], axis=1)
    tile_valid = jnp.clip(jnp.sum(of_expert * cnt[None, :], axis=1) - (tile_ids - first_tile) * MOE_TILE,
                          0, MOE_TILE).astype(jnp.int32)
    experts = jnp.arange(cnt.shape[0], dtype=jnp.int32)
    has_rows = cnt > 0
    tile_slot = (jnp.sum(of_expert * (jnp.cumsum(has_rows.astype(jnp.int32)) - 1)[None, :], axis=1) % 2).astype(jnp.int32)
    later_with_rows = jnp.logical_and(has_rows[None, :], experts[None, :] > experts[:, None])
    next_expert = jnp.min(jnp.where(later_with_rows, experts[None, :], cnt.shape[0]), axis=1)
    next_expert = jnp.where(next_expert < cnt.shape[0], next_expert, -1)
    tile_next = jnp.sum(of_expert * next_expert[None, :], axis=1).astype(jnp.int32)
    plan = (tile_expert, tile_first, tile_valid, n_used.reshape(1).astype(jnp.int32), tile_slot, tile_next)
    return pos, wts, plan


def _rope_tables(length, n_ctx_rows):
    rows = length // GRID_W
    row = jnp.repeat(jnp.arange(rows, dtype=F32), GRID_W)
    col = jnp.tile(jnp.arange(GRID_W, dtype=F32), rows)
    n_freq = C_DQK // 4
    inv_freq = ROPE_THETA ** (-jnp.arange(n_freq, dtype=F32) / n_freq)
    ang_r = row[:, None] * inv_freq
    ang_c = col[:, None] * inv_freq
    cos = jnp.concatenate([jnp.cos(ang_r), jnp.cos(ang_r), jnp.cos(ang_c), jnp.cos(ang_c)], axis=-1)
    sin = jnp.concatenate([-jnp.sin(ang_r), jnp.sin(ang_r), -jnp.sin(ang_c), jnp.sin(ang_c)], axis=-1)
    reps = C_W // C_DQK
    cos = jnp.concatenate([jnp.ones((n_ctx_rows, C_W), F32), jnp.tile(cos, (1, reps))], axis=0)
    sin = jnp.concatenate([jnp.zeros((n_ctx_rows, C_W), F32), jnp.tile(sin, (1, reps))], axis=0)
    return cos, sin


def _hgrn_lower_bound(table, layer):
    p = jax.nn.softmax(table.astype(F32), axis=1)
    cum = jnp.cumsum(p, axis=1) - p[:, :1]
    return jnp.clip(cum[:, layer], 0.0, 1.0)


def kernel(x, c, ctx, c_ctx, ada_w, ada_b, sandwich_norms, w_in, w_out, hgrn_lower_bounds, hgrn_norm,
           mlstm_gate_bias, mlstm_norm, diff_lambdas, diff_norm, router_w, router_b, moe_w1, moe_b1,
           moe_w2, moe_b2):
    n_batch, seq, d = x.shape
    n_ctx_rows = ctx.shape[1]
    depth = w_in.shape[0]
    n_exp = router_w.shape[2]
    assert seq % ROW_TILE == 0 and n_ctx_rows % ROW_TILE == 0 and seq % GRID_W == 0
    rows_b = n_ctx_rows + seq
    tiles_b = rows_b // ROW_TILE
    ctx_tiles = n_ctx_rows // ROW_TILE
    lat_tiles = seq // ROW_TILE

    b_main = PA_W + 2 * HEADS * B_QK + 2 * B_W
    w_in_p = jnp.concatenate(
        [w_in[:, :, :b_main],
         jnp.pad(w_in[:, :, b_main:b_main + N_GATE], ((0, 0), (0, 0), (0, LANES - N_GATE))),
         w_in[:, :, b_main + N_GATE:]], axis=2).astype(BF16)
    w_out_b = w_out.astype(BF16)
    two_f = moe_b1.shape[2]
    b1p = moe_b1.reshape(depth, n_exp, two_f // PAIR_BLOCK, LANES, 2).transpose(0, 1, 2, 4, 3).reshape(
        depth, n_exp, 1, two_f)
    b2 = moe_b2[:, :, None, :]
    ada_b3 = ada_b[:, None, :]
    router_b3 = router_b[:, :, None]
    rw_t = router_w.transpose(0, 2, 1)
    rw_hi = rw_t.astype(BF16)
    router_wt = jnp.stack([rw_hi, (rw_t - rw_hi.astype(F32)).astype(BF16)], axis=1)
    gate_bias = jnp.pad(mlstm_gate_bias, ((0, 0), (0, LANES - N_GATE)))
    cos, sin = _rope_tables(seq, n_ctx_rows)

    cond_rows = (n_batch + 1 + 7) // 8 * 8
    cvec = jnp.zeros((cond_rows, d), F32).at[:n_batch].set(c).at[n_batch].set(c_ctx)

    def mod_row_all(i):
        return jnp.where(i % tiles_b < ctx_tiles, n_batch, i // tiles_b)

    xy = jnp.concatenate([ctx, x], axis=1).reshape(n_batch * rows_b, d)

    for layer in range(depth):
        last = layer == depth - 1
        mod = _adaln(cvec, ada_w, ada_b3, layer).reshape(cond_rows, 1, 6 * d)
        lb = _hgrn_lower_bound(hgrn_lower_bounds, layer)
        lam_init = 0.8 - 0.6 * math.exp(-0.3 * layer)
        lq1, lk1, lq2, lk2 = diff_lambdas[layer].astype(F32)
        lam = (jnp.exp(jnp.sum(lq1 * lk1)) - jnp.exp(jnp.sum(lq2 * lk2)) + lam_init).reshape(1)

        pa, pb, pc = _inproj(xy, mod, sandwich_norms, w_in_p, cos, sin, layer, tiles_b, ctx_tiles, n_batch)
        oa = _hgrn(pa.reshape(n_batch, rows_b, PA_W), lb, jnp.tile(hgrn_norm[layer], HEADS)[None, :], n_ctx_rows)
        ob = _mlstm(pb.reshape(n_batch, rows_b, PB_W), gate_bias[layer][None, :], mlstm_norm[layer][None, :],
                    n_ctx_rows)
        oc = _attention(pc.reshape(n_batch, rows_b, PC_OUT), lam, diff_norm[layer][None, :], n_ctx_rows,
                        lam_init, not last)

        if last:
            n_tiles = n_batch * lat_tiles
            in_tile = lambda i: (i // lat_tiles) * tiles_b + ctx_tiles + i % lat_tiles
            mod_row = lambda i: i // lat_tiles
        else:
            n_tiles = n_batch * tiles_b
            in_tile = lambda i: i
            mod_row = mod_row_all
        xn, h, route, counts = _outproj(
            oa.reshape(-1, A_W), ob.reshape(-1, B_W), oc.reshape(-1, C_W), xy, mod, sandwich_norms, w_out_b,
            router_wt, router_b3, layer, n_tiles, in_tile, lambda i: i, mod_row)

        r_max = n_tiles * ROW_TILE * TOP_K + n_exp * MOE_TILE
        pos, wts, plan = _dispatch_plan(route, counts, r_max)
        xs = _dispatch_rows(h, pos, r_max)
        ys = _moe(plan, xs, moe_w1, b1p, moe_w2, b2, layer)
        xy = _ffn_residual(xn, wts, _combine_rows(ys, pos), mod, sandwich_norms, layer, mod_row)

    return xy.reshape(n_batch, seq, d)
```

```python
import functools
import math

import jax
import jax.numpy as jnp
from jax import lax
from jax.experimental import pallas as pl
from jax.experimental.pallas import tpu as pltpu
from jax.experimental.pallas import tpu_sc as plsc

F32 = jnp.float32
BF16 = jnp.bfloat16
HI = lax.Precision.HIGHEST

HEADS = 4
A_W = 256
B_QK = 32
B_W = 256
C_DQK = 64
C_DV = 2 * C_DQK
ATTN_KEY_SCALE = C_DQK ** -0.5 * math.log2(math.e)
C_W = 512
HEAD_V = 64
N_GATE = 16
GRID_W = 64
TOP_K = 4
SWIGLU_ALPHA = 1.702
SWIGLU_LIMIT = 7.0
ROPE_THETA = 10000.0
NORM_EPS = 1e-6
MASK_NEG = -1e30
F_MIN = 1e-12

LANES = 128
ROW_TILE = 256
CHUNK = 64
SCAN_BATCHES = 2
ROW_GROUPS = 2
MOE_TILE = 512
ATTN_HEADS_PER_STEP = 2
VMEM_LIMIT = 56 * 1024 * 1024

PA_W = 5 * A_W
PB_W = 2 * HEADS * B_QK + 2 * B_W + LANES
PC_IN = 3 * C_W
PC_OUT = 4 * C_W
W_IN_PAD = PA_W + PB_W + PC_IN


def _cparams(sem):
    return pltpu.CompilerParams(dimension_semantics=sem, vmem_limit_bytes=VMEM_LIMIT)


def _nt(a, b):
    return lax.dot_general(a, b, (((1,), (1,)), ((), ())), preferred_element_type=F32)


def _tn(a, b, precision=None):
    return lax.dot_general(a, b, (((0,), (0,)), ((), ())), preferred_element_type=F32, precision=precision)


def _rms(x):
    return x * lax.rsqrt(jnp.mean(x * x, axis=-1, keepdims=True) + NORM_EPS)


def _silu(x):
    return x * jax.nn.sigmoid(x)


def _pack_bf16_pairs(x):
    half = x.shape[1] // 2
    bits = pltpu.bitcast(x.astype(BF16).astype(F32), jnp.uint32)
    return pltpu.bitcast((bits[:, :half] >> 16) | (bits[:, half:] & jnp.uint32(0xFFFF0000)), jnp.int32)


def _unpack_bf16_pairs(words):
    bits = pltpu.bitcast(words, jnp.uint32)
    return jnp.concatenate([pltpu.bitcast(bits << 16, F32),
                            pltpu.bitcast(bits & jnp.uint32(0xFFFF0000), F32)], axis=1)


def _adaln_kernel(c_ref, w_ref, b_ref, o_ref):
    cond = _silu(c_ref[...])
    o_ref[...] = jnp.dot(cond, w_ref[...], preferred_element_type=F32, precision=HI) + b_ref[...]


def _adaln(cvec, ada_w, ada_b):
    rows, d = cvec.shape
    depth = ada_w.shape[0]
    return pl.pallas_call(
        _adaln_kernel,
        out_shape=jax.ShapeDtypeStruct((depth, rows, 6 * d), F32),
        grid=(depth, 6),
        in_specs=[pl.BlockSpec((rows, d), lambda l, j: (0, 0)),
                  pl.BlockSpec((None, d, d), lambda l, j: (l, 0, j)),
                  pl.BlockSpec((None, 1, d), lambda l, j: (l, 0, j))],
        out_specs=pl.BlockSpec((None, rows, d), lambda l, j: (l, 0, j)),
        compiler_params=_cparams(("arbitrary", "arbitrary")),
        name="adaln",
    )(cvec, ada_w, ada_b)


def _inproj_kernel(x_ref, mod_ref, g_ref, w_ref, cos_ref, sin_ref, pa_ref, pb_ref, pc_ref):
    d = x_ref.shape[1]
    shift = mod_ref[0, :, 0:d]
    scale = mod_ref[0, :, d:2 * d]
    h = _rms(x_ref[...]) * g_ref[0:1, :] * (1.0 + scale) + shift
    hb = h.astype(BF16)
    pc = jnp.dot(hb, w_ref[:, PA_W + PB_W:W_IN_PAD], preferred_element_type=F32)
    pa_ref[...] = jnp.dot(hb, w_ref[:, 0:PA_W], preferred_element_type=F32)
    pb_ref[...] = jnp.dot(hb, w_ref[:, PA_W:PA_W + PB_W], preferred_element_type=F32)
    q = pc[:, 0:C_W]
    k = pc[:, C_W:2 * C_W] * ATTN_KEY_SCALE
    cos = cos_ref[...]
    sin = sin_ref[...]
    lane = lax.broadcasted_iota(jnp.int32, q.shape, 1)
    first = (lane % 32) < 16

    def rope(t):
        partner = jnp.where(first, pltpu.roll(t, C_W - 16, 1), pltpu.roll(t, 16, 1))
        return t * cos + partner * sin

    pc_ref[:, 0:C_W] = q.astype(BF16)
    pc_ref[:, C_W:2 * C_W] = rope(q).astype(BF16)
    pc_ref[:, 2 * C_W:3 * C_W] = rope(k).astype(BF16)
    pc_ref[:, 3 * C_W:4 * C_W] = pc[:, 2 * C_W:3 * C_W].astype(BF16)


def _inproj(xy, mod, norms, w_in_p, cos, sin, layer, tiles_per_batch, n_ctx_tiles, n_batch):
    t_all, d = xy.shape
    n_tiles = t_all // ROW_TILE

    def mod_row(i):
        return jnp.where(i % tiles_per_batch < n_ctx_tiles, n_batch, i // tiles_per_batch)

    return pl.pallas_call(
        _inproj_kernel,
        out_shape=(jax.ShapeDtypeStruct((t_all, PA_W), F32),
                   jax.ShapeDtypeStruct((t_all, PB_W), F32),
                   jax.ShapeDtypeStruct((t_all, PC_OUT), BF16)),
        grid=(n_tiles,),
        in_specs=[pl.BlockSpec((ROW_TILE, d), lambda i: (i, 0)),
                  pl.BlockSpec((1, 1, 6 * d), lambda i: (mod_row(i), 0, 0)),
                  pl.BlockSpec((None, 4, d), lambda i: (layer, 0, 0)),
                  pl.BlockSpec((None, d, W_IN_PAD), lambda i: (layer, 0, 0)),
                  pl.BlockSpec((ROW_TILE, C_W), lambda i: (i % tiles_per_batch, 0)),
                  pl.BlockSpec((ROW_TILE, C_W), lambda i: (i % tiles_per_batch, 0))],
        out_specs=(pl.BlockSpec((ROW_TILE, PA_W), lambda i: (i, 0)),
                   pl.BlockSpec((ROW_TILE, PB_W), lambda i: (i, 0)),
                   pl.BlockSpec((ROW_TILE, PC_OUT), lambda i: (i, 0))),
        compiler_params=_cparams(("arbitrary",)),
        name="inproj",
    )(xy, mod, norms, w_in_p, cos, sin)


def _hgrn_kernel(pa_ref, lb_ref, gain_ref, o_ref, st_ref, ob_ref, *, n_ctx, n_all, n_b):
    c_sz, w = CHUNK, A_W
    same_head = (lax.broadcasted_iota(jnp.int32, (w, w), 0) // HEAD_V
                 == lax.broadcasted_iota(jnp.int32, (w, w), 1) // HEAD_V)
    head_ones = same_head.astype(BF16)
    assert c_sz == HEAD_V
    t_i = lax.broadcasted_iota(jnp.int32, (c_sz, w), 0)
    s_i = lax.broadcasted_iota(jnp.int32, (c_sz, w), 1) % c_sz
    t_row = lax.broadcasted_iota(jnp.int32, (c_sz, 1), 0)
    row8 = t_row % 8

    def grouped_rows(a, k):
        return jnp.concatenate(
            [jnp.broadcast_to(a[8 * j + k:8 * j + k + 1, :], (8, w)) for j in range(c_sz // 8)], axis=0)

    def halving_levels(rev):
        out = []
        b = c_sz // 2
        while b >= 1:
            def later(i):
                return ((i % (2 * b)) < b) if rev else ((i % (2 * b)) >= b)
            live = jnp.logical_and(t_i // (2 * b) == s_i // (2 * b),
                                   jnp.logical_and(later(t_i), jnp.logical_not(later(s_i))))
            out.append((b, live.astype(F32), jnp.where(later(t_row), 1.0, -1.0)))
            b //= 2
        return out

    n_seg = 2 * n_b
    seg_rev = [seg % 2 == 1 for seg in range(n_seg)]

    def seg_rows(seg):
        return slice(seg * c_sz, (seg + 1) * c_sz)

    def stack(parts):
        return jnp.concatenate(parts, axis=0)

    def rows_of(rows):
        return stack([jnp.broadcast_to(row, (c_sz, w)) for row in rows])

    self_mask = stack([(s_i == t_i).astype(F32)] * n_seg)
    levels = [(b, stack([live_b if rev else live_f for rev in seg_rev]),
               stack([sign_b if rev else sign_f for rev in seg_rev]))
              for (b, live_f, sign_f), (_, live_b, sign_b) in zip(halving_levels(False), halving_levels(True))]
    lb_all = rows_of([lb_ref[1:2, :] if rev else lb_ref[0:1, :] for rev in seg_rev])
    r_all = lax.broadcasted_iota(jnp.int32, (n_seg * c_sz, n_seg * c_sz), 0)
    c_all = lax.broadcasted_iota(jnp.int32, (n_seg * c_sz, n_seg * c_sz), 1)
    rev_row = (r_all // c_sz) % 2 == 1
    tri = jnp.logical_and(r_all // c_sz == c_all // c_sz, jnp.logical_or(
        jnp.logical_and(jnp.logical_not(rev_row), c_all <= r_all),
        jnp.logical_and(rev_row, c_all >= r_all))).astype(BF16)
    zero = jnp.zeros((), BF16)

    def boundary_rows(cum, b, rev):
        if b >= 8:
            return jnp.concatenate(
                [jnp.broadcast_to(cum[r0 + (b if rev else b - 1):r0 + (b if rev else b - 1) + 1, :], (2 * b, w))
                 for r0 in range(0, c_sz, 2 * b)], axis=0)
        ref = None
        for g in reversed(range(8 // (2 * b))):
            cand = grouped_rows(cum, g * 2 * b + (b if rev else b - 1))
            ref = cand if ref is None else jnp.where(row8 < (g + 1) * 2 * b, cand, ref)
        return ref

    def scan_step(n, carry):
        c_f = n
        c_b = jnp.where(n < n_ctx, n_ctx - 1 - n, n_all - 1 - (n - n_ctx))
        chunk_rows = [pl.ds(pl.multiple_of((c_b if rev else c_f) * c_sz, c_sz), c_sz) for rev in seg_rev]
        q_pre = stack([pa_ref[seg // 2, chunk_rows[seg], 0:A_W] for seg in range(n_seg)])
        v_l = [pa_ref[seg // 2, chunk_rows[seg], A_W:2 * A_W].astype(BF16) for seg in range(n_seg)]
        f_pre = stack([pa_ref[seg // 2, chunk_rows[seg], (3 if rev else 2) * A_W:(4 if rev else 3) * A_W]
                       for seg, rev in enumerate(seg_rev)])
        q = _silu(q_pre)
        f = lb_all + (1.0 - lb_all) * jax.nn.sigmoid(f_pre)
        log_f = jnp.log(jnp.maximum(f, F_MIN))
        kk = (1.0 - lb_all) * jax.nn.sigmoid(-f_pre)
        cum = sum(jnp.dot(tri, piece, preferred_element_type=F32) for piece in _split3(log_f))
        ends = [cum[seg * c_sz + (0 if rev else c_sz - 1):seg * c_sz + (0 if rev else c_sz - 1) + 1, :]
                for seg, rev in enumerate(seg_rev)]

        st_l = [st_ref[seg] for seg in range(n_seg)]
        q_in = (q * jnp.exp(cum)).astype(BF16)
        o = stack([_nt(q_in[seg_rows(seg)], st_l[seg].astype(BF16)) for seg in range(n_seg)])
        k_end = (kk * jnp.exp(rows_of(ends) - cum)).astype(BF16)
        for seg in range(n_seg):
            st_ref[seg] = (st_l[seg] * jnp.exp(ends[seg])
                           + jnp.where(same_head, _tn(v_l[seg], k_end[seg_rows(seg)]), 0.0))

        def scores(qa, ka):
            return _nt(qa, jnp.where(same_head, jnp.concatenate([ka] * HEADS, axis=0), zero))

        refs = [stack([boundary_rows(cum[seg_rows(seg)], b, rev) for seg, rev in enumerate(seg_rev)])
                for b, _, _ in levels]
        zs = [jnp.exp((cum - ref) * sign)
              for ref, (_, _, sign) in zip(refs, levels)]
        qas = [(q * z).astype(BF16) for z in zs]
        kas = [(kk * z).astype(BF16) for z in zs]
        prods = [stack([scores(qa[seg_rows(seg)], ka[seg_rows(seg)]) for seg in range(n_seg)])
                 for qa, ka in zip(qas, kas)]
        p = self_mask * jnp.dot((q * kk).astype(BF16), head_ones, preferred_element_type=F32)
        for prod, (_, live, _) in zip(prods, levels):
            p = p + live * prod
        pb = p.astype(BF16)

        for seg, rev in enumerate(seg_rev):
            out = o[seg_rows(seg)] + jnp.dot(
                pb[seg_rows(seg)], jnp.where(same_head, jnp.concatenate([v_l[seg]] * HEADS, axis=0), zero),
                preferred_element_type=F32)
            if rev:
                ob_ref[seg // 2, chunk_rows[seg], :] = out
            else:
                o_ref[seg // 2, chunk_rows[seg], :] = out
        return carry

    st_ref[...] = jnp.zeros_like(st_ref)
    lax.fori_loop(0, n_all, scan_step, 0, unroll=2)

    def readout(c, carry):
        rows = pl.ds(pl.multiple_of(c * ROW_TILE, ROW_TILE), ROW_TILE)
        for b in range(n_b):
            tot = o_ref[b, rows, :] + ob_ref[b, rows, :]
            ms = sum(jnp.dot(piece, head_ones, preferred_element_type=F32)
                     for piece in _split3(tot * tot)) * (1.0 / HEAD_V)
            g = pa_ref[b, rows, 4 * A_W:5 * A_W]
            o_ref[b, rows, :] = tot * lax.rsqrt(ms + NORM_EPS) * gain_ref[...] * _silu(g)
        return carry

    lax.fori_loop(0, n_all * c_sz // ROW_TILE, readout, 0)


def _hgrn(pa3, lb, gain, n_ctx_rows):
    n_batch, rows, _ = pa3.shape
    n_b = SCAN_BATCHES if n_batch % SCAN_BATCHES == 0 else 1
    kern = functools.partial(_hgrn_kernel, n_ctx=n_ctx_rows // CHUNK, n_all=rows // CHUNK, n_b=n_b)
    return pl.pallas_call(
        kern,
        out_shape=jax.ShapeDtypeStruct((n_batch, rows, A_W), F32),
        grid=(n_batch // n_b,),
        in_specs=[pl.BlockSpec((n_b, rows, PA_W), lambda b: (b, 0, 0), pipeline_mode=pl.Buffered(1)),
                  pl.BlockSpec((2, A_W), lambda b: (0, 0)),
                  pl.BlockSpec((1, A_W), lambda b: (0, 0))],
        out_specs=pl.BlockSpec((n_b, rows, A_W), lambda b: (b, 0, 0)),
        scratch_shapes=[pltpu.VMEM((2 * n_b, A_W, A_W), F32), pltpu.VMEM((n_b, rows, A_W), F32)],
        compiler_params=_cparams(("arbitrary",)),
        name="hgrn2",
    )(pa3, lb, gain)


def _split3(x):
    hi = x.astype(BF16)
    rest = x - hi.astype(F32)
    mid = rest.astype(BF16)
    return hi, mid, (rest - mid.astype(F32)).astype(BF16)


def _mlstm_kernel(pb_ref, bias_ref, gain_ref, o_ref, cn_ref, ob_ref, *, n_ctx, n_all, n_b):
    assert CHUNK == HEAD_V
    c_sz, w, qk_w = CHUNK, B_W, HEADS * B_QK
    t_i = lax.broadcasted_iota(jnp.int32, (c_sz, w), 0)
    s_i = lax.broadcasted_iota(jnp.int32, (c_sz, w), 1) % c_sz
    diag4 = s_i == t_i
    half_lane = lax.broadcasted_iota(jnp.int32, (1, LANES), 1) < HEAD_V
    k_block = (lax.broadcasted_iota(jnp.int32, (HEADS * c_sz, qk_w), 0) // c_sz
               == lax.broadcasted_iota(jnp.int32, (HEADS * c_sz, qk_w), 1) // B_QK)
    v_block = (lax.broadcasted_iota(jnp.int32, (HEADS * c_sz, w), 0) // c_sz
               == lax.broadcasted_iota(jnp.int32, (HEADS * c_sz, w), 1) // HEAD_V)
    state_block = (lax.broadcasted_iota(jnp.int32, (qk_w, 2 * w), 0) // B_QK
                   == (lax.broadcasted_iota(jnp.int32, (qk_w, 2 * w), 1) % w) // HEAD_V)
    head_ones = v_block.astype(BF16)
    q_off, k_off, v_off, o_off, g_off = 0, qk_w, 2 * qk_w, 2 * qk_w + B_W, 2 * qk_w + 2 * B_W

    sel_r = lax.broadcasted_iota(jnp.int32, (2 * LANES, 2 * w), 0)
    sel_c = lax.broadcasted_iota(jnp.int32, (2 * LANES, 2 * w), 1)
    sel = (sel_r == jnp.where(sel_c < w, HEADS + sel_c // HEAD_V, LANES + (sel_c - w) // HEAD_V)).astype(BF16)

    def head_max(x):
        outs = []
        for col in range(w // LANES):
            xc = x[:, col * LANES:(col + 1) * LANES]
            lo = jnp.max(jnp.where(half_lane, xc, -jnp.inf), axis=-1, keepdims=True)
            hi = jnp.max(jnp.where(half_lane, -jnp.inf, xc), axis=-1, keepdims=True)
            outs.append(jnp.where(half_lane, lo, hi))
        return jnp.concatenate(outs, axis=1)

    n_seg = 2 * n_b
    seg_rev = [seg % 2 == 1 for seg in range(n_seg)]

    def seg_rows(seg):
        return slice(seg * c_sz, (seg + 1) * c_sz)

    mask = jnp.concatenate([(s_i >= t_i) if rev else (s_i <= t_i) for rev in seg_rev], axis=0)
    r_all = lax.broadcasted_iota(jnp.int32, (n_seg * c_sz, n_seg * c_sz), 0)
    c_all = lax.broadcasted_iota(jnp.int32, (n_seg * c_sz, n_seg * c_sz), 1)
    same_seg = r_all // c_sz == c_all // c_sz
    rev_row = (r_all // c_sz) % 2 == 1
    tri = jnp.logical_and(same_seg, jnp.logical_or(
        jnp.logical_and(jnp.logical_not(rev_row), c_all <= r_all),
        jnp.logical_and(rev_row, c_all >= r_all))).astype(BF16)

    def per_seg(fn, x):
        return jnp.concatenate([jnp.broadcast_to(fn(x[seg_rows(seg)]), (c_sz, w)) for seg in range(n_seg)],
                               axis=0)

    def rows_of(rows):
        return jnp.concatenate([jnp.broadcast_to(row, (c_sz, w)) for row in rows], axis=0)

    def scan_step(n, m_prev_rows):
        c_f = n
        c_b = jnp.where(n < n_ctx, n_ctx - 1 - n, n_all - 1 - (n - n_ctx))
        chunk_rows = [pl.ds(pl.multiple_of((c_b if rev else c_f) * c_sz, c_sz), c_sz) for rev in seg_rev]

        def load(lo, hi):
            return [pb_ref[seg // 2, chunk_rows[seg], lo:hi] for seg in range(n_seg)]

        q_l = load(q_off, q_off + qk_w)
        k_l = load(k_off, k_off + qk_w)
        v_l = load(v_off, v_off + B_W)
        g_l = load(g_off, g_off + LANES)
        gts = jnp.concatenate(
            [pltpu.roll(g + bias_ref[...], LANES - 2 * HEADS, 1) if rev else g + bias_ref[...]
             for g, rev in zip(g_l, seg_rev)], axis=0)
        log_f = jnp.minimum(gts, 0.0) - jnp.log(1.0 + jnp.exp(-jnp.abs(gts)))
        cum_col = sum(jnp.dot(tri, p, preferred_element_type=F32) for p in _split3(log_f))
        both = sum(jnp.dot(p, sel, preferred_element_type=F32)
                   for p in _split3(jnp.concatenate([cum_col, gts], axis=1)))
        cum_t = both[:, :w]
        ig_t = both[:, w:]
        src = per_seg(lambda x: jnp.sum(jnp.where(diag4, x, 0.0), axis=0, keepdims=True), cum_t - ig_t)
        m_prev = rows_of(m_prev_rows)
        log_d = jnp.where(mask, cum_t - src, MASK_NEG)
        log_inter = cum_t + m_prev
        m_t = jnp.maximum(log_inter, head_max(log_d))
        w_intra = jnp.where(mask, jnp.exp(log_d - m_t), 0.0)
        w_inter = jnp.exp(log_inter - m_t)

        scale = B_QK ** -0.5
        qb_l = [(q * scale).astype(BF16) for q in q_l]
        kb_l = [k.astype(BF16) for k in k_l]
        zero = jnp.zeros((), BF16)
        scores = [_nt(qb, jnp.where(k_block, jnp.concatenate([kb] * HEADS, axis=0), zero))
                  for qb, kb in zip(qb_l, kb_l)]
        p = (jnp.concatenate(scores, axis=0) * w_intra).astype(BF16)

        cn_l = [cn_ref[seg] for seg in range(n_seg)]
        inter = jnp.concatenate([jnp.dot(qb, cn.astype(BF16), preferred_element_type=F32)
                                 for qb, cn in zip(qb_l, cn_l)], axis=0)
        values = [jnp.dot(p[seg_rows(seg)],
                          jnp.where(v_block, jnp.concatenate([v_l[seg].astype(BF16)] * HEADS, axis=0), zero),
                          preferred_element_type=F32) for seg in range(n_seg)]
        num = w_inter * inter[:, :w] + jnp.concatenate(values, axis=0)
        den = w_inter * inter[:, w:] + jnp.dot(p, head_ones, preferred_element_type=F32)
        h_out = num / jnp.maximum(jnp.abs(den), jnp.exp(-m_t))
        for seg, rev in enumerate(seg_rev):
            if rev:
                ob_ref[seg // 2, chunk_rows[seg], :] = h_out[seg_rows(seg)]
            else:
                o_ref[seg // 2, chunk_rows[seg], :] = h_out[seg_rows(seg)]

        ends = [cum_t[seg * c_sz + (0 if rev else c_sz - 1):seg * c_sz + (0 if rev else c_sz - 1) + 1, :]
                for seg, rev in enumerate(seg_rev)]
        log_end = rows_of(ends) - cum_t + ig_t
        m_ends = [jnp.maximum(end + m_old, jnp.max(log_end[seg_rows(seg)], axis=0, keepdims=True))
                  for seg, (end, m_old) in enumerate(zip(ends, m_prev_rows))]
        w_end = jnp.exp(log_end - rows_of(m_ends))
        for seg in range(n_seg):
            w_seg = w_end[seg_rows(seg)]
            upd = _tn(kb_l[seg], jnp.concatenate([(w_seg * v_l[seg]).astype(BF16), w_seg.astype(BF16)], axis=1))
            carry_w = jnp.exp(ends[seg] + m_prev_rows[seg] - m_ends[seg])
            cn_ref[seg] = (jnp.concatenate([carry_w, carry_w], axis=1) * cn_l[seg]
                           + jnp.where(state_block, upd, 0.0))
        return tuple(m_ends)

    cn_ref[...] = jnp.zeros_like(cn_ref)
    lax.fori_loop(0, n_all, scan_step, tuple(jnp.zeros((1, w), F32) for _ in range(n_seg)), unroll=2)

    def readout(c, carry):
        rows = pl.ds(pl.multiple_of(c * ROW_TILE, ROW_TILE), ROW_TILE)
        for b in range(n_b):
            tot = o_ref[b, rows, :] + ob_ref[b, rows, :]
            normed = jnp.concatenate(
                [_rms(tot[:, h * HEAD_V:(h + 1) * HEAD_V]) for h in range(HEADS)], axis=1) * gain_ref[...]
            og = pb_ref[b, rows, o_off:o_off + B_W]
            o_ref[b, rows, :] = normed * jax.nn.sigmoid(og)
        return carry

    lax.fori_loop(0, n_all * c_sz // ROW_TILE, readout, 0)


def _mlstm(pb3, bias, gain, n_ctx_rows):
    n_batch, rows, _ = pb3.shape
    n_b = SCAN_BATCHES if n_batch % SCAN_BATCHES == 0 else 1
    kern = functools.partial(_mlstm_kernel, n_ctx=n_ctx_rows // CHUNK, n_all=rows // CHUNK, n_b=n_b)
    return pl.pallas_call(
        kern,
        out_shape=jax.ShapeDtypeStruct((n_batch, rows, B_W), F32),
        grid=(n_batch // n_b,),
        in_specs=[pl.BlockSpec((n_b, rows, PB_W), lambda b: (b, 0, 0)),
                  pl.BlockSpec((1, LANES), lambda b: (0, 0)),
                  pl.BlockSpec((1, B_W), lambda b: (0, 0))],
        out_specs=pl.BlockSpec((n_b, rows, B_W), lambda b: (b, 0, 0)),
        scratch_shapes=[pltpu.VMEM((2 * n_b, HEADS * B_QK, 2 * B_W), F32), pltpu.VMEM((n_b, rows, B_W), F32)],
        compiler_params=_cparams(("arbitrary",)),
        name="mlstm",
    )(pb3, bias, gain)


def _attn_kernel(lam_ref, qp_ref, qr_ref, k_ref, v_ref, gain_ref, o_ref, *, n_ctx, q_tile0, lam_init):
    lam = lam_ref[0]
    q_tile = pl.program_id(2) + q_tile0
    lane = lax.broadcasted_iota(jnp.int32, (1, 2 * C_DQK), 1)

    n_all = k_ref.shape[1]

    def finish(head, parts):
        o = parts[0] - lam * parts[1]
        o_ref[0, :, head] = _rms(o) * gain_ref[...] * (1.0 - lam_init)

    def sub_query(q, j):
        return jnp.where(lane // C_DQK == j, q, jnp.zeros_like(q))

    def row_max(s):
        return jnp.max(s, axis=-1, keepdims=True)

    def row_sum(s):
        return jnp.sum(s, axis=-1, keepdims=True)

    def pv(ex, v):
        return jnp.dot(ex.astype(BF16), v, preferred_element_type=F32)

    heads = [slice(hh * C_DV, (hh + 1) * C_DV) for hh in range(ATTN_HEADS_PER_STEP)]

    def attend(pieces):
        problems = [(head, j) for head in heads for j in range(2)]
        scores = [[_nt(sub_query(q_ref[0, :, head], j), k_ref[0, k0:k1, head]) for q_ref, k0, k1 in pieces]
                  for head, j in problems]
        maxes = [functools.reduce(jnp.maximum, [row_max(s) for s in ss]) for ss in scores]
        exps = [[jnp.exp2(s - m) for s in ss] for ss, m in zip(scores, maxes)]
        outs = [sum(pv(e, v_ref[0, k0:k1, head]) for e, (_, k0, k1) in zip(es, pieces))
                / sum(row_sum(e) for e in es) for es, (head, _) in zip(exps, problems)]
        for n, head in enumerate(heads):
            finish(head, outs[2 * n:2 * n + 2])

    @pl.when(q_tile * ROW_TILE < n_ctx)
    def _():
        attend([(qp_ref, 0, n_ctx)])

    @pl.when(q_tile * ROW_TILE >= n_ctx)
    def _():
        attend([(qp_ref, 0, n_ctx), (qr_ref, n_ctx, n_all)])


def _attention(pc3, lam, gain, n_ctx_rows, lam_init, with_ctx):
    n_batch, rows, _ = pc3.shape
    q_tile0 = 0 if with_ctx else n_ctx_rows // ROW_TILE
    n_q = rows // ROW_TILE - q_tile0
    gw = ATTN_HEADS_PER_STEP * C_DV
    hb = C_W // gw
    kern = functools.partial(_attn_kernel, n_ctx=n_ctx_rows, q_tile0=q_tile0, lam_init=lam_init)
    grid_spec = pltpu.PrefetchScalarGridSpec(
        num_scalar_prefetch=1,
        grid=(n_batch, hb, n_q),
        in_specs=[pl.BlockSpec((1, ROW_TILE, gw), lambda b, h, i, lam: (b, i + q_tile0, h)),
                  pl.BlockSpec((1, ROW_TILE, gw), lambda b, h, i, lam: (b, i + q_tile0, hb + h)),
                  pl.BlockSpec((1, rows, gw), lambda b, h, i, lam: (b, 0, 2 * hb + h)),
                  pl.BlockSpec((1, rows, gw), lambda b, h, i, lam: (b, 0, 3 * hb + h)),
                  pl.BlockSpec((1, C_DV), lambda b, h, i, lam: (0, 0))],
        out_specs=pl.BlockSpec((1, ROW_TILE, gw), lambda b, h, i, lam: (b, i, h)),
    )
    return pl.pallas_call(
        kern,
        out_shape=jax.ShapeDtypeStruct((n_batch, n_q * ROW_TILE, C_W), F32),
        grid_spec=grid_spec,
        compiler_params=_cparams(("arbitrary", "arbitrary", "arbitrary")),
        name="diff_attn",
    )(lam, pc3, pc3, pc3, pc3, gain)


def _outproj_kernel(oa_ref, ob_ref, oc_ref, x_ref, mod_ref, n_ref, w_ref, rw_ref, rb_ref,
                    xn_ref, h_ref, route_ref, cnt_ref, carry_ref):
    d = x_ref.shape[1]
    n_exp = rw_ref.shape[1]

    @pl.when(pl.program_id(0) == 0)
    def _():
        carry_ref[...] = jnp.zeros_like(carry_ref)

    n_rows = x_ref.shape[0]
    halves = [slice(g * n_rows // ROW_GROUPS, (g + 1) * n_rows // ROW_GROUPS) for g in range(ROW_GROUPS)]
    mixes = [jnp.dot(oa_ref[hs, :].astype(BF16), w_ref[0:A_W, :], preferred_element_type=F32)
             + jnp.dot(ob_ref[hs, :].astype(BF16), w_ref[A_W:A_W + B_W, :], preferred_element_type=F32)
             + jnp.dot(oc_ref[hs, :].astype(BF16), w_ref[A_W + B_W:A_W + B_W + C_W, :],
                       preferred_element_type=F32) for hs in halves]
    xns = [x_ref[hs, :] + mod_ref[0, :, 2 * d:3 * d] * (_rms(mix) * n_ref[1:2, :])
           for hs, mix in zip(halves, mixes)]
    hiddens = [_rms(xn) * n_ref[2:3, :] * (1.0 + mod_ref[0, :, 4 * d:5 * d]) + mod_ref[0, :, 3 * d:4 * d]
               for xn in xns]
    for hs, xn, h in zip(halves, xns, hiddens):
        xn_ref[hs, :] = xn
        h_ref[hs, :] = _pack_bf16_pairs(h)

    h_his = [h.astype(BF16) for h in hiddens]
    h_mids = [(h - h_hi.astype(F32)).astype(BF16) for h, h_hi in zip(hiddens, h_his)]
    logits = jnp.concatenate([_nt(rw_ref[0], h_hi) + _nt(rw_ref[1], h_hi) + _nt(rw_ref[0], h_mid)
                              for h_hi, h_mid in zip(h_his, h_mids)], axis=1) + rb_ref[...]
    e_sub = lax.broadcasted_iota(jnp.int32, logits.shape, 0)
    cur = logits
    picks, vals = [], []
    for _ in range(TOP_K):
        mx = jnp.max(cur, axis=0, keepdims=True)
        idx = jnp.min(jnp.where(cur == mx, e_sub, n_exp), axis=0, keepdims=True)
        hit = e_sub == idx
        cur = jnp.where(hit, -jnp.inf, cur)
        picks.append((idx, hit.astype(F32)))
        vals.append(mx)
    exps = [jnp.exp(vv - vals[0]) for vv in vals]
    total = exps[0] + exps[1] + exps[2] + exps[3]

    chosen = picks[0][1] + picks[1][1] + picks[2][1] + picks[3][1]
    tm = logits.shape[1]
    before = (lax.broadcasted_iota(jnp.int32, (tm, tm), 0) < lax.broadcasted_iota(jnp.int32, (tm, tm), 1))
    seen = jnp.dot(chosen.astype(BF16), before.astype(BF16), preferred_element_type=F32) + carry_ref[...]
    new_carry = carry_ref[...] + jnp.sum(chosen, axis=1, keepdims=True)
    carry_ref[...] = new_carry
    cnt_ref[...] = new_carry

    o_row = lax.broadcasted_iota(jnp.int32, route_ref.shape, 0)
    route = jnp.zeros(route_ref.shape, F32)
    for kk in range(TOP_K):
        idx, hit = picks[kk]
        rank = jnp.sum(hit * seen, axis=0, keepdims=True)
        route = (route + jnp.where(o_row == kk, idx.astype(F32), 0.0)
                 + jnp.where(o_row == TOP_K + kk, exps[kk] / total, 0.0)
                 + jnp.where(o_row == 2 * TOP_K + kk, rank, 0.0))
    route_ref[...] = route


ROUTE_ROWS = 16


def _outproj(oa, ob, oc, xy, mod, norms, w_out_b, router_wt, router_b, layer, n_tiles, in_tile, oc_tile, mod_row):
    d = xy.shape[1]
    n_exp = router_wt.shape[2]
    t_out = n_tiles * ROW_TILE
    xn, h, route_t, counts = pl.pallas_call(
        _outproj_kernel,
        out_shape=(jax.ShapeDtypeStruct((t_out, d), F32),
                   jax.ShapeDtypeStruct((t_out, d // 2), jnp.int32),
                   jax.ShapeDtypeStruct((n_tiles * ROUTE_ROWS, ROW_TILE), F32),
                   jax.ShapeDtypeStruct((n_exp, 1), F32)),
        grid=(n_tiles,),
        in_specs=[pl.BlockSpec((ROW_TILE, A_W), lambda i: (in_tile(i), 0)),
                  pl.BlockSpec((ROW_TILE, B_W), lambda i: (in_tile(i), 0)),
                  pl.BlockSpec((ROW_TILE, C_W), lambda i: (oc_tile(i), 0)),
                  pl.BlockSpec((ROW_TILE, d), lambda i: (in_tile(i), 0)),
                  pl.BlockSpec((1, 1, 6 * d), lambda i: (mod_row(i), 0, 0)),
                  pl.BlockSpec((None, 4, d), lambda i: (layer, 0, 0)),
                  pl.BlockSpec((None, d, d), lambda i: (layer, 0, 0)),
                  pl.BlockSpec((None, 2, n_exp, d), lambda i: (layer, 0, 0, 0)),
                  pl.BlockSpec((None, n_exp, 1), lambda i: (layer, 0, 0))],
        out_specs=(pl.BlockSpec((ROW_TILE, d), lambda i: (i, 0)),
                   pl.BlockSpec((ROW_TILE, d // 2), lambda i: (i, 0)),
                   pl.BlockSpec((ROUTE_ROWS, ROW_TILE), lambda i: (i, 0)),
                   pl.BlockSpec((n_exp, 1), lambda i: (0, 0))),
        scratch_shapes=[pltpu.VMEM((n_exp, 1), F32)],
        compiler_params=_cparams(("arbitrary",)),
        name="outproj_router",
    )(oa, ob, oc, xy, mod, norms, w_out_b, router_wt, router_b)
    return xn, h, route_t.reshape(n_tiles, ROUTE_ROWS, ROW_TILE), counts[:, 0]


PAIR_BLOCK = 2 * LANES


def _moe_kernel(te_ref, first_ref, valid_ref, nu_ref, slot_ref, next_ref, xs_ref, w1_hbm, b1_ref, w2_hbm, b2_ref,
                ys_ref, w1p_ref, w2b_ref, w1_buf, w2_buf, w_sem, *, layer):
    i = pl.program_id(0)
    two_f = w1_buf.shape[2]
    n_blk = two_f // PAIR_BLOCK

    def fetch(expert, slot):
        return (pltpu.make_async_copy(w1_hbm.at[layer, expert], w1_buf.at[slot], w_sem.at[0, slot]),
                pltpu.make_async_copy(w2_hbm.at[layer, expert], w2_buf.at[slot], w_sem.at[1, slot]))

    @pl.when(i == 0)
    def _():
        for copy in fetch(te_ref[0], 0):
            copy.start()

    @pl.when(jnp.logical_and(i < nu_ref[0], first_ref[i] == 1))
    def _():
        slot = slot_ref[i]
        for copy in fetch(te_ref[i], slot):
            copy.wait()
        r = lax.broadcasted_iota(jnp.int32, (PAIR_BLOCK, PAIR_BLOCK), 0)
        c = lax.broadcasted_iota(jnp.int32, (PAIR_BLOCK, PAIR_BLOCK), 1)
        perm = (r == jnp.where(c < LANES, 2 * c, 2 * (c - LANES) + 1)).astype(BF16)
        for blk in range(n_blk):
            cols = slice(blk * PAIR_BLOCK, (blk + 1) * PAIR_BLOCK)
            w1p_ref[:, cols] = jnp.dot(w1_buf[slot, :, cols].astype(BF16), perm,
                                       preferred_element_type=F32).astype(BF16)
        w2b_ref[...] = w2_buf[slot].astype(BF16)

        @pl.when(next_ref[i] >= 0)
        def _():
            for copy in fetch(next_ref[i], 1 - slot):
                copy.start()

    tile_rows = xs_ref.shape[0]
    used = i < nu_ref[0]
    valid = valid_ref[i]

    def expert_ffn(n_rows):
        x = _unpack_bf16_pairs(xs_ref[0:n_rows, :]).astype(BF16)
        row = lax.broadcasted_iota(jnp.int32, (n_rows, 1), 0)
        x = jnp.where(row < valid, x, jnp.zeros_like(x))
        hid = jnp.dot(x, w1p_ref[...], preferred_element_type=F32) + b1_ref[...]
        acts = []
        for blk in range(n_blk):
            glu = jnp.minimum(hid[:, blk * PAIR_BLOCK:blk * PAIR_BLOCK + LANES], SWIGLU_LIMIT)
            lin = jnp.clip(hid[:, blk * PAIR_BLOCK + LANES:(blk + 1) * PAIR_BLOCK], -SWIGLU_LIMIT, SWIGLU_LIMIT)
            acts.append((glu * jax.nn.sigmoid(SWIGLU_ALPHA * glu) * (lin + 1.0)).astype(BF16))
        y = jnp.dot(jnp.concatenate(acts, axis=1), w2b_ref[...], preferred_element_type=F32) + b2_ref[...]
        ys_ref[0:n_rows, :] = _pack_bf16_pairs(y)
        if n_rows < tile_rows:
            ys_ref[n_rows:tile_rows, :] = jnp.zeros((tile_rows - n_rows, ys_ref.shape[1]), ys_ref.dtype)

    @pl.when(jnp.logical_and(used, valid > tile_rows // 2))
    def _():
        expert_ffn(tile_rows)

    @pl.when(jnp.logical_and(used, valid <= tile_rows // 2))
    def _():
        expert_ffn(tile_rows // 2)

    @pl.when(jnp.logical_not(used))
    def _():
        ys_ref[...] = jnp.zeros_like(ys_ref)


def _moe(plan, xs, w1, b1p, w2, b2, layer):
    r_max, half_d = xs.shape
    d = 2 * half_d
    two_f = w1.shape[3]
    f = two_f // 2
    n_tiles = r_max // MOE_TILE

    def row_tile(i, te, fi, va, nu, sl, nx):
        return (jnp.maximum(jnp.minimum(i, nu[0] - 1), 0), 0)

    def bias_block(i, te, fi, va, nu, sl, nx):
        return (layer, te[i], 0, 0)

    grid_spec = pltpu.PrefetchScalarGridSpec(
        num_scalar_prefetch=6,
        grid=(n_tiles,),
        in_specs=[pl.BlockSpec((MOE_TILE, half_d), row_tile),
                  pl.BlockSpec(memory_space=pl.ANY),
                  pl.BlockSpec((None, None, 1, two_f), bias_block),
                  pl.BlockSpec(memory_space=pl.ANY),
                  pl.BlockSpec((None, None, 1, d), bias_block)],
        out_specs=pl.BlockSpec((MOE_TILE, half_d), lambda i, te, fi, va, nu, sl, nx: (i, 0)),
        scratch_shapes=[pltpu.VMEM((d, two_f), BF16), pltpu.VMEM((f, d), BF16),
                        pltpu.VMEM((2, d, two_f), F32), pltpu.VMEM((2, f, d), F32),
                        pltpu.SemaphoreType.DMA((2, 2))],
    )
    return pl.pallas_call(
        functools.partial(_moe_kernel, layer=layer),
        out_shape=jax.ShapeDtypeStruct((r_max, half_d), jnp.int32),
        grid_spec=grid_spec,
        compiler_params=_cparams(("arbitrary",)),
        name="moe_experts",
    )(*plan, xs, w1, b1p, w2, b2)


SC_CORES = 2
SC_SUBCORES = 16
SC_CHUNK = 64


def _dispatch_rows(hp, pos, r_max):
    t, width = hp.shape
    workers = SC_CORES * SC_SUBCORES
    assert t % (workers * SC_CHUNK) == 0
    per_worker = t // (workers * SC_CHUNK)
    idx = pos.reshape(-1, TOP_K, ROW_TILE // SC_CHUNK, SC_CHUNK).transpose(0, 2, 1, 3).reshape(
        workers, per_worker * TOP_K, SC_CHUNK)
    mesh = plsc.VectorSubcoreMesh(core_axis_name="c", subcore_axis_name="s",
                                  num_cores=SC_CORES, num_subcores=SC_SUBCORES)

    @functools.partial(
        pl.kernel, mesh=mesh,
        out_type=jax.ShapeDtypeStruct((r_max, width), hp.dtype),
        scratch_types=[pltpu.VMEM((per_worker * TOP_K, SC_CHUNK), jnp.int32),
                       pltpu.VMEM((2, SC_CHUNK, width), hp.dtype),
                       pltpu.SemaphoreType.DMA((2,)),
                       pltpu.SemaphoreType.DMA((2,))],
    )
    def scatter(hp_hbm, idx_hbm, out_hbm, idx_v, rows_v, read_sem, write_sem):
        wid = lax.axis_index("s") * SC_CORES + lax.axis_index("c")
        pltpu.sync_copy(idx_hbm.at[wid], idx_v)

        def read(j):
            row0 = pl.multiple_of((wid * per_worker + j) * SC_CHUNK, SC_CHUNK)
            return pltpu.make_async_copy(hp_hbm.at[pl.ds(row0, SC_CHUNK)], rows_v.at[j % 2], read_sem.at[j % 2])

        def write(j, k):
            return pltpu.make_async_copy(rows_v.at[j % 2], out_hbm.at[idx_v.at[j * TOP_K + k]],
                                         write_sem.at[j % 2])

        read(0).start()
        for j in range(per_worker):
            read(j).wait()
            for k in range(TOP_K):
                write(j, k).start()
            if j + 1 < per_worker:
                if j >= 1:
                    for k in range(TOP_K):
                        write(j - 1, k).wait()
                read(j + 1).start()
        for j in range(max(per_worker - 2, 0), per_worker):
            for k in range(TOP_K):
                write(j, k).wait()

    return scatter(hp, idx)


def _combine_rows(ysp, pos):
    width = ysp.shape[1]
    t = pos.shape[0] * pos.shape[2]
    workers = SC_CORES * SC_SUBCORES
    assert t % (workers * SC_CHUNK) == 0
    per_worker = t // (workers * SC_CHUNK)
    units = per_worker * TOP_K
    idx = pos.reshape(-1, TOP_K, ROW_TILE // SC_CHUNK, SC_CHUNK).transpose(0, 2, 1, 3).reshape(
        workers, units, SC_CHUNK)
    mesh = plsc.VectorSubcoreMesh(core_axis_name="c", subcore_axis_name="s",
                                  num_cores=SC_CORES, num_subcores=SC_SUBCORES)

    @functools.partial(
        pl.kernel, mesh=mesh,
        out_type=jax.ShapeDtypeStruct((TOP_K, t, width), ysp.dtype),
        scratch_types=[pltpu.VMEM((units, SC_CHUNK), jnp.int32),
                       pltpu.VMEM((2, SC_CHUNK, width), ysp.dtype),
                       pltpu.SemaphoreType.DMA((2,)),
                       pltpu.SemaphoreType.DMA((2,))],
    )
    def gather(ys_hbm, idx_hbm, out_hbm, idx_v, rows_v, read_sem, write_sem):
        wid = lax.axis_index("s") * SC_CORES + lax.axis_index("c")
        pltpu.sync_copy(idx_hbm.at[wid], idx_v)

        def read(u):
            return pltpu.make_async_copy(ys_hbm.at[idx_v.at[u]], rows_v.at[u % 2], read_sem.at[u % 2])

        def write(u):
            row0 = pl.multiple_of((wid * per_worker + u // TOP_K) * SC_CHUNK, SC_CHUNK)
            return pltpu.make_async_copy(rows_v.at[u % 2], out_hbm.at[u % TOP_K, pl.ds(row0, SC_CHUNK)],
                                         write_sem.at[u % 2])

        read(0).start()
        for u in range(units):
            read(u).wait()
            write(u).start()
            if u + 1 < units:
                if u >= 1:
                    write(u - 1).wait()
                read(u + 1).start()
        for u in range(max(units - 2, 0), units):
            write(u).wait()

    return gather(ysp, idx)


def _ffn_residual_kernel(x_ref, w_ref, y0_ref, y1_ref, y2_ref, y3_ref, mod_ref, n_ref, o_ref):
    d = x_ref.shape[1]
    wts = w_ref[...]
    ffn = None
    for k, y_ref in enumerate((y0_ref, y1_ref, y2_ref, y3_ref)):
        term = wts[:, k:k + 1] * _unpack_bf16_pairs(y_ref[...])
        ffn = term if ffn is None else ffn + term
    o_ref[...] = x_ref[...] + mod_ref[0, :, 5 * d:6 * d] * (_rms(ffn) * n_ref[3:4, :])


def _ffn_residual(xn, wts, ys_by_k, mod, norms, layer, mod_row):
    t, d = xn.shape
    row_spec = pl.BlockSpec((ROW_TILE, d), lambda i: (i, 0))

    def y_spec(k):
        return pl.BlockSpec((None, ROW_TILE, d // 2), lambda i: (k, i, 0))

    return pl.pallas_call(
        _ffn_residual_kernel,
        out_shape=jax.ShapeDtypeStruct((t, d), F32),
        grid=(t // ROW_TILE,),
        in_specs=[row_spec,
                  pl.BlockSpec((ROW_TILE, TOP_K), lambda i: (i, 0)),
                  y_spec(0), y_spec(1), y_spec(2), y_spec(3),
                  pl.BlockSpec((1, 1, 6 * d), lambda i: (mod_row(i), 0, 0)),
                  pl.BlockSpec((None, 4, d), lambda i: (layer, 0, 0))],
        out_specs=row_spec,
        compiler_params=_cparams(("arbitrary",)),
        name="ffn_residual",
    )(xn, wts, ys_by_k, ys_by_k, ys_by_k, ys_by_k, mod, norms)


def _dispatch_plan(route_t, counts, r_max):
    n_row_tiles = route_t.shape[0]
    idx = route_t[:, 0:TOP_K, :].astype(jnp.int32)
    wts = route_t[:, TOP_K:2 * TOP_K, :].transpose(0, 2, 1).reshape(n_row_tiles * ROW_TILE, TOP_K)
    rank = route_t[:, 2 * TOP_K:3 * TOP_K, :].astype(jnp.int32)
    cnt = counts.astype(jnp.int32)
    padded = (cnt + MOE_TILE - 1) // MOE_TILE * MOE_TILE
    ends = jnp.cumsum(padded)
    starts = ends - padded
    pos = rank
    for e in range(cnt.shape[0]):
        pos = pos + jnp.where(idx == e, starts[e], 0)
    n_tiles = r_max // MOE_TILE
    n_used = ends[-1] // MOE_TILE
    tile_ids = jnp.minimum(jnp.arange(n_tiles, dtype=jnp.int32), n_used - 1)
    tile_expert = jnp.sum((ends // MOE_TILE)[None, :] <= tile_ids[:, None], axis=1).astype(jnp.int32)
    tile_first = jnp.concatenate(
        [jnp.ones((1,), jnp.int32), (tile_expert[1:] != tile_expert[:-1]).astype(jnp.int32)])
    of_expert = (tile_expert[:, None] == jnp.arange(cnt.shape[0], dtype=jnp.int32)[None, :]).astype(jnp.int32)
    first_tile = jnp.sum(of_expert * ((ends - padded) // MOE_TILE)[None, :], axis=1)
    tile_valid = jnp.clip(jnp.sum(of_expert * cnt[None, :], axis=1) - (tile_ids - first_tile) * MOE_TILE,
                          0, MOE_TILE).astype(jnp.int32)
    experts = jnp.arange(cnt.shape[0], dtype=jnp.int32)
    has_rows = cnt > 0
    tile_slot = (jnp.sum(of_expert * (jnp.cumsum(has_rows.astype(jnp.int32)) - 1)[None, :], axis=1) % 2).astype(jnp.int32)
    later_with_rows = jnp.logical_and(has_rows[None, :], experts[None, :] > experts[:, None])
    next_expert = jnp.min(jnp.where(later_with_rows, experts[None, :], cnt.shape[0]), axis=1)
    next_expert = jnp.where(next_expert < cnt.shape[0], next_expert, -1)
    tile_next = jnp.sum(of_expert * next_expert[None, :], axis=1).astype(jnp.int32)
    plan = (tile_expert, tile_first, tile_valid, n_used.reshape(1).astype(jnp.int32), tile_slot, tile_next)
    return pos, wts, plan


def _rope_tables(length, n_ctx_rows):
    rows = length // GRID_W
    row = jnp.repeat(jnp.arange(rows, dtype=F32), GRID_W)
    col = jnp.tile(jnp.arange(GRID_W, dtype=F32), rows)
    n_freq = C_DQK // 4
    inv_freq = ROPE_THETA ** (-jnp.arange(n_freq, dtype=F32) / n_freq)
    ang_r = row[:, None] * inv_freq
    ang_c = col[:, None] * inv_freq
    cos = jnp.concatenate([jnp.cos(ang_r), jnp.cos(ang_r), jnp.cos(ang_c), jnp.cos(ang_c)], axis=-1)
    sin = jnp.concatenate([-jnp.sin(ang_r), jnp.sin(ang_r), -jnp.sin(ang_c), jnp.sin(ang_c)], axis=-1)
    reps = C_W // C_DQK
    cos = jnp.concatenate([jnp.ones((n_ctx_rows, C_W), F32), jnp.tile(cos, (1, reps))], axis=0)
    sin = jnp.concatenate([jnp.zeros((n_ctx_rows, C_W), F32), jnp.tile(sin, (1, reps))], axis=0)
    return cos, sin


def _hgrn_lower_bound(table, layer):
    p = jax.nn.softmax(table.astype(F32), axis=1)
    cum = jnp.cumsum(p, axis=1) - p[:, :1]
    return jnp.clip(cum[:, layer], 0.0, 1.0)


def kernel(x, c, ctx, c_ctx, ada_w, ada_b, sandwich_norms, w_in, w_out, hgrn_lower_bounds, hgrn_norm,
           mlstm_gate_bias, mlstm_norm, diff_lambdas, diff_norm, router_w, router_b, moe_w1, moe_b1,
           moe_w2, moe_b2):
    n_batch, seq, d = x.shape
    n_ctx_rows = ctx.shape[1]
    depth = w_in.shape[0]
    n_exp = router_w.shape[2]
    assert seq % ROW_TILE == 0 and n_ctx_rows % ROW_TILE == 0 and seq % GRID_W == 0
    rows_b = n_ctx_rows + seq
    tiles_b = rows_b // ROW_TILE
    ctx_tiles = n_ctx_rows // ROW_TILE
    lat_tiles = seq // ROW_TILE

    b_main = PA_W + 2 * HEADS * B_QK + 2 * B_W
    w_in_p = jnp.concatenate(
        [w_in[:, :, :b_main],
         jnp.pad(w_in[:, :, b_main:b_main + N_GATE], ((0, 0), (0, 0), (0, LANES - N_GATE))),
         w_in[:, :, b_main + N_GATE:]], axis=2).astype(BF16)
    w_out_b = w_out.astype(BF16)
    two_f = moe_b1.shape[2]
    b1p = moe_b1.reshape(depth, n_exp, two_f // PAIR_BLOCK, LANES, 2).transpose(0, 1, 2, 4, 3).reshape(
        depth, n_exp, 1, two_f)
    b2 = moe_b2[:, :, None, :]
    ada_b3 = ada_b[:, None, :]
    router_b3 = router_b[:, :, None]
    rw_t = router_w.transpose(0, 2, 1)
    rw_hi = rw_t.astype(BF16)
    router_wt = jnp.stack([rw_hi, (rw_t - rw_hi.astype(F32)).astype(BF16)], axis=1)
    gate_bias = jnp.pad(mlstm_gate_bias, ((0, 0), (0, LANES - N_GATE)))
    cos, sin = _rope_tables(seq, n_ctx_rows)

    cond_rows = (n_batch + 1 + 7) // 8 * 8
    cvec = jnp.zeros((cond_rows, d), F32).at[:n_batch].set(c).at[n_batch].set(c_ctx)

    def mod_row_all(i):
        return jnp.where(i % tiles_b < ctx_tiles, n_batch, i // tiles_b)

    xy = jnp.concatenate([ctx, x], axis=1).reshape(n_batch * rows_b, d)
    mods = _adaln(cvec, ada_w, ada_b3)

    for layer in range(depth):
        last = layer == depth - 1
        mod = mods[layer].reshape(cond_rows, 1, 6 * d)
        lb = _hgrn_lower_bound(hgrn_lower_bounds, layer)
        lam_init = 0.8 - 0.6 * math.exp(-0.3 * layer)
        lq1, lk1, lq2, lk2 = diff_lambdas[layer].astype(F32)
        lam = (jnp.exp(jnp.sum(lq1 * lk1)) - jnp.exp(jnp.sum(lq2 * lk2)) + lam_init).reshape(1)

        pa, pb, pc = _inproj(xy, mod, sandwich_norms, w_in_p, cos, sin, layer, tiles_b, ctx_tiles, n_batch)
        oa = _hgrn(pa.reshape(n_batch, rows_b, PA_W), lb, jnp.tile(hgrn_norm[layer], HEADS)[None, :], n_ctx_rows)
        ob = _mlstm(pb.reshape(n_batch, rows_b, PB_W), gate_bias[layer][None, :], mlstm_norm[layer][None, :],
                    n_ctx_rows)
        oc = _attention(pc.reshape(n_batch, rows_b, PC_OUT), lam, diff_norm[layer][None, :], n_ctx_rows,
                        lam_init, not last)

        if last:
            n_tiles = n_batch * lat_tiles
            in_tile = lambda i: (i // lat_tiles) * tiles_b + ctx_tiles + i % lat_tiles
            mod_row = lambda i: i // lat_tiles
        else:
            n_tiles = n_batch * tiles_b
            in_tile = lambda i: i
            mod_row = mod_row_all
        xn, h, route, counts = _outproj(
            oa.reshape(-1, A_W), ob.reshape(-1, B_W), oc.reshape(-1, C_W), xy, mod, sandwich_norms, w_out_b,
            router_wt, router_b3, layer, n_tiles, in_tile, lambda i: i, mod_row)

        r_max = n_tiles * ROW_TILE * TOP_K + n_exp * MOE_TILE
        pos, wts, plan = _dispatch_plan(route, counts, r_max)
        xs = _dispatch_rows(h, pos, r_max)
        ys = _moe(plan, xs, moe_w1, b1p, moe_w2, b2, layer)
        xy = _ffn_residual(xn, wts, _combine_rows(ys, pos), mod, sandwich_norms, layer, mod_row)

    return xy.reshape(n_batch, seq, d)
```
